```python
import jax, jax.numpy as jnp
from jax import lax
import numpy as np

D_MODEL = 1024
BATCH = 8
SEQ = 16384
DEPTH = 4

HEAD_DIM = 64
N_TOK_HEADS = 12
N_KV_HEADS = 3
GQA_GROUP = N_TOK_HEADS // N_KV_HEADS
N_MEM_HEADS = 4
N_MEM = 256
Q_W = N_TOK_HEADS * HEAD_DIM
KV_W = N_KV_HEADS * HEAD_DIM
QM_W = N_MEM_HEADS * HEAD_DIM
IN_W = Q_W + 2 * KV_W + QM_W
MIX_WIDTH = Q_W + QM_W
D_FF = -(-8 * D_MODEL // (3 * 256)) * 256
N_MIXERS = 3
BLOCK = 128
A_RADIUS = 128
C_GROUPS = ((128, 1), (512, 4), (2048, 16))
ROPE_THETA = 500000.0
ROPE_DIMS = HEAD_DIM // 4
AXIAL_THETA = 10000.0
GRID_W = 64
EPS = 1e-6
N_A = (DEPTH + 2) // N_MIXERS
N_B = (DEPTH + 1) // N_MIXERS
ATTN_SCALE = HEAD_DIM ** -0.5

kernel_name = 'hybrid_interleaved_window_axial_dilated_encoder'


def rms_norm(x, g):
    xf = x.astype(jnp.float32)
    y = xf * lax.rsqrt(jnp.mean(xf * xf, axis=-1, keepdims=True) + EPS)
    return (y * g.astype(jnp.float32)).astype(x.dtype)


def rope_table(pos, n_dims, theta):
    inv = theta ** (-(jnp.arange(0, n_dims, 2, dtype=jnp.float32) / n_dims))
    ang = pos.astype(jnp.float32)[:, None] * inv[None, :]
    return jnp.cos(ang), jnp.sin(ang)


def apply_rotary(x, cos, sin):
    half = x.shape[-1] // 2
    xf = x.astype(jnp.float32)
    x1, x2 = xf[..., :half], xf[..., half:]
    c, s = cos[:, None, :], sin[:, None, :]
    return jnp.concatenate([x1 * c - x2 * s, x2 * c + x1 * s], axis=-1).astype(x.dtype)


def partial_rope(x, cos, sin):
    return jnp.concatenate([apply_rotary(x[..., :ROPE_DIMS], cos, sin), x[..., ROPE_DIMS:]], axis=-1)


def axial_rope(x, cos_r, sin_r, cos_c, sin_c):
    half = HEAD_DIM // 2
    return jnp.concatenate([apply_rotary(x[..., :half], cos_r, sin_r),
                            apply_rotary(x[..., half:], cos_c, sin_c)], axis=-1)


def banded_attention(q, k, v, radius, sink=None):
    B, L, KVH, G, HD = q.shape
    blk = radius
    nb = -(-L // blk)
    Lp = nb * blk
    pad = Lp - L
    qb = jnp.pad(q, [(0, 0), (0, pad), (0, 0), (0, 0), (0, 0)]).reshape(B, nb, blk, KVH, G, HD)

    def windows(t):
        tp = jnp.pad(t, [(0, 0), (blk, blk + pad), (0, 0), (0, 0)]).reshape(B, nb + 2, blk, KVH, HD)
        return jnp.concatenate([tp[:, :-2], tp[:, 1:-1], tp[:, 2:]], axis=2)

    kw, vw = windows(k), windows(v)
    qpos = jnp.arange(Lp).reshape(nb, blk)
    kpos = (jnp.arange(nb)[:, None] - 1) * blk + jnp.arange(3 * blk)[None, :]
    mask = ((jnp.abs(qpos[:, :, None] - kpos[:, None, :]) <= radius)
            & (kpos >= 0)[:, None, :] & (kpos < L)[:, None, :])
    s = jnp.einsum('bnqhgd,bnkhd->bnhgqk', qb, kw, preferred_element_type=jnp.float32) * ATTN_SCALE
    s = jnp.where(mask[None, :, None, None], s, -jnp.inf)
    m = jnp.max(s, axis=-1, keepdims=True)
    if sink is not None:
        sk = sink.astype(jnp.float32)[None, None, :, :, None, None]
        m = jnp.maximum(m, sk)
    p = jnp.exp(s - m)
    denom = jnp.sum(p, axis=-1)
    if sink is not None:
        denom = denom + jnp.exp(sk - m)[..., 0]
    o = jnp.einsum('bnhgqk,bnkhd->bnqhgd', p.astype(v.dtype), vw)
    den_t = jnp.transpose(denom, (0, 1, 4, 2, 3))
    o = (o / den_t[..., None]).astype(q.dtype).reshape(B, Lp, KVH, G, HD)[:, :L]
    lse = jnp.transpose(m[..., 0] + jnp.log(denom), (0, 1, 4, 2, 3)).reshape(B, Lp, KVH, G)[:, :L]
    return o, lse


def full_attention_blocks(q, k, v):
    B, S, KVH, G, HD = q.shape
    nb = S // BLOCK
    qb = jnp.moveaxis(q.reshape(B, nb, BLOCK, KVH, G, HD), 1, 0)

    def one_block(qblk):
        s = jnp.einsum('bqhgd,bkhd->bhgqk', qblk, k, preferred_element_type=jnp.float32) * ATTN_SCALE
        p = jax.nn.softmax(s, axis=-1)
        return jnp.einsum('bhgqk,bkhd->bqhgd', p.astype(v.dtype), v)

    o = lax.map(one_block, qb)
    return jnp.moveaxis(o, 0, 1).reshape(B, S, KVH, G, HD)


def mixer_a(q, k, v, sink, cos_p, sin_p):
    B, S = q.shape[:2]
    q = partial_rope(q, cos_p, sin_p).reshape(B, S, N_KV_HEADS, GQA_GROUP, HEAD_DIM)
    k = partial_rope(k, cos_p, sin_p)
    o, _ = banded_attention(q, k, v, A_RADIUS, sink.reshape(N_KV_HEADS, GQA_GROUP))
    return o.reshape(B, S, Q_W)


def mixer_b(q, k, v, qk_g, cos_r, sin_r, cos_c, sin_c):
    B, S = q.shape[:2]
    q = axial_rope(rms_norm(q, qk_g[0]), cos_r, sin_r, cos_c, sin_c)
    k = axial_rope(rms_norm(k, qk_g[1]), cos_r, sin_r, cos_c, sin_c)
    o = full_attention_blocks(q.reshape(B, S, N_KV_HEADS, GQA_GROUP, HEAD_DIM), k, v)
    return o.reshape(B, S, Q_W)


def dilated_group(q, k, v, dil, radius):
    B, S = q.shape[:2]
    L = S // dil

    def split(t):
        t = jnp.moveaxis(t.reshape((B, L, dil) + t.shape[2:]), 2, 1)
        return t.reshape((B * dil, L) + t.shape[3:])

    def merge(t):
        t = jnp.moveaxis(t.reshape((B, dil, L) + t.shape[2:]), 1, 2)
        return t.reshape((B, S) + t.shape[3:])

    o, lse = banded_attention(split(q)[:, :, None], split(k)[:, :, None], split(v)[:, :, None], radius)
    return merge(o[:, :, 0]), merge(lse[:, :, 0])


def mixer_c(q, k, v, cos_p, sin_p):
    B, S = q.shape[:2]
    q = partial_rope(q, cos_p, sin_p)
    k = partial_rope(k, cos_p, sin_p)
    outs, lses = [], []
    for g, (window, dil) in enumerate(C_GROUPS):
        o, l = dilated_group(q[:, :, g * GQA_GROUP:(g + 1) * GQA_GROUP], k[:, :, g], v[:, :, g],
                             dil, window // (2 * dil))
        outs.append(o)
        lses.append(l)
    alpha = jax.nn.softmax(jnp.stack(lses, axis=2), axis=2)
    o = jnp.stack(outs, axis=2) * alpha[..., None].astype(q.dtype)
    return o.reshape(B, S, Q_W)


def memory_attention(qm, km, vm):
    s = jnp.einsum('bshd,bmhd->bhsm', qm, km, preferred_element_type=jnp.float32) * ATTN_SCALE
    p = jax.nn.softmax(s, axis=-1)
    o = jnp.einsum('bhsm,bmhd->bshd', p.astype(vm.dtype), vm)
    return o.reshape(qm.shape[0], qm.shape[1], QM_W)


def _fwd_setup_inputs(seed: int = 0) -> dict:
    key = jax.random.key(seed)
    ks = jax.random.split(key, 14)
    nrm = jax.random.normal
    f32 = jnp.float32
    return {
        'x': nrm(ks[0], (BATCH, SEQ, D_MODEL), f32),
        'mem': nrm(ks[1], (BATCH, N_MEM, D_MODEL), f32),
        'mem_norm_g': 1.0 + 0.02 * nrm(ks[2], (D_MODEL,), f32),
        'w_in': nrm(ks[3], (DEPTH, D_MODEL, IN_W), f32) * D_MODEL ** -0.5,
        'w_mem_kv': nrm(ks[4], (DEPTH, D_MODEL, 2 * QM_W), f32) * D_MODEL ** -0.5,
        'w_o': nrm(ks[5], (DEPTH, MIX_WIDTH, D_MODEL), f32) * MIX_WIDTH ** -0.5,
        'g_mix_pre': 1.0 + 0.02 * nrm(ks[6], (DEPTH, D_MODEL), f32),
        'g_mix_post': 1.0 + 0.02 * nrm(ks[7], (DEPTH, D_MODEL), f32),
        'attn_sink': 0.5 * nrm(ks[8], (N_A, N_TOK_HEADS), f32),
        'qk_norm_g': 1.0 + 0.02 * nrm(ks[9], (N_B, 2, HEAD_DIM), f32),
        'w_gate_up': nrm(ks[10], (DEPTH, D_MODEL, 2 * D_FF), f32) * D_MODEL ** -0.5,
        'w_down': nrm(ks[11], (DEPTH, D_FF, D_MODEL), f32) * D_FF ** -0.5,
        'g_ffn_pre': 1.0 + 0.02 * nrm(ks[12], (DEPTH, D_MODEL), f32),
        'g_ffn_post': 1.0 + 0.02 * nrm(ks[13], (DEPTH, D_MODEL), f32),
    }


def _fwd_reference(x, mem, mem_norm_g, w_in, w_mem_kv, w_o, g_mix_pre, g_mix_post, attn_sink,
              qk_norm_g, w_gate_up, w_down, g_ffn_pre, g_ffn_post):
    B, S, _ = x.shape
    rows = S // GRID_W
    pos = jnp.arange(S, dtype=jnp.int32)
    row_ids = jnp.repeat(jnp.arange(rows, dtype=jnp.int32), GRID_W)
    col_ids = jnp.tile(jnp.arange(GRID_W, dtype=jnp.int32), rows)
    cos_p, sin_p = rope_table(pos, ROPE_DIMS, ROPE_THETA)
    cos_r, sin_r = rope_table(row_ids, HEAD_DIM // 2, AXIAL_THETA)
    cos_c, sin_c = rope_table(col_ids, HEAD_DIM // 2, AXIAL_THETA)
    mem_n = rms_norm(mem, mem_norm_g)

    for i in range(DEPTH):
        h = rms_norm(x, g_mix_pre[i])
        proj = h @ w_in[i]
        q = proj[..., :Q_W].reshape(B, S, N_TOK_HEADS, HEAD_DIM)
        k = proj[..., Q_W:Q_W + KV_W].reshape(B, S, N_KV_HEADS, HEAD_DIM)
        v = proj[..., Q_W + KV_W:Q_W + 2 * KV_W].reshape(B, S, N_KV_HEADS, HEAD_DIM)
        qm = proj[..., Q_W + 2 * KV_W:].reshape(B, S, N_MEM_HEADS, HEAD_DIM)
        kind = i % N_MIXERS
        if kind == 0:
            tok = mixer_a(q, k, v, attn_sink[i // N_MIXERS], cos_p, sin_p)
        elif kind == 1:
            tok = mixer_b(q, k, v, qk_norm_g[i // N_MIXERS], cos_r, sin_r, cos_c, sin_c)
        else:
            tok = mixer_c(q, k, v, cos_p, sin_p)
        mkv = mem_n @ w_mem_kv[i]
        km = mkv[..., :QM_W].reshape(B, N_MEM, N_MEM_HEADS, HEAD_DIM)
        vm = mkv[..., QM_W:].reshape(B, N_MEM, N_MEM_HEADS, HEAD_DIM)
        mo = memory_attention(qm, km, vm)
        o = jnp.concatenate([tok, mo], axis=-1) @ w_o[i]
        x = x + rms_norm(o, g_mix_post[i])

        h = rms_norm(x, g_ffn_pre[i])
        gu = h @ w_gate_up[i]
        f = (jax.nn.silu(gu[..., :D_FF]) * gu[..., D_FF:]) @ w_down[i]
        x = x + rms_norm(f, g_ffn_post[i])
    return x


import jax as _jax
import jax.numpy as _jnp

TWIN_FORMAT = 'train_step'
FWD_PARAMS = ['x', 'mem', 'mem_norm_g', 'w_in', 'w_mem_kv', 'w_o', 'g_mix_pre', 'g_mix_post', 'attn_sink', 'qk_norm_g', 'w_gate_up', 'w_down', 'g_ffn_pre', 'g_ffn_post']
TWIN_WEIGHTS = ['mem_norm_g', 'w_in', 'w_mem_kv', 'w_o', 'g_mix_pre', 'g_mix_post', 'attn_sink', 'qk_norm_g', 'w_gate_up', 'w_down', 'g_ffn_pre', 'g_ffn_post']
TWIN_DIFF_INPUT = 'x'
TWIN_INPUTS = ['x', 'mem', 'mem_norm_g', 'w_in', 'w_mem_kv', 'w_o', 'g_mix_pre', 'g_mix_post', 'attn_sink', 'qk_norm_g', 'w_gate_up', 'w_down', 'g_ffn_pre', 'g_ffn_post', 'loss_target', 'm_mem_norm_g', 'm_w_in', 'm_w_mem_kv', 'm_w_o', 'm_g_mix_pre', 'm_g_mix_post', 'm_attn_sink', 'm_qk_norm_g', 'm_w_gate_up', 'm_w_down', 'm_g_ffn_pre', 'm_g_ffn_post', 'v_mem_norm_g', 'v_w_in', 'v_w_mem_kv', 'v_w_o', 'v_g_mix_pre', 'v_g_mix_post', 'v_attn_sink', 'v_qk_norm_g', 'v_w_gate_up', 'v_w_down', 'v_g_ffn_pre', 'v_g_ffn_post']
TWIN_OUTPUTS = ['loss', 'grad_x', 'grad_mem_norm_g', 'grad_w_in', 'grad_w_mem_kv', 'grad_w_o', 'grad_g_mix_pre', 'grad_g_mix_post', 'grad_attn_sink', 'grad_qk_norm_g', 'grad_w_gate_up', 'grad_w_down', 'grad_g_ffn_pre', 'grad_g_ffn_post', 'delta_mem_norm_g', 'delta_w_in', 'delta_w_mem_kv', 'delta_w_o', 'delta_g_mix_pre', 'delta_g_mix_post', 'delta_attn_sink', 'delta_qk_norm_g', 'delta_w_gate_up', 'delta_w_down', 'delta_g_ffn_pre', 'delta_g_ffn_post', 'new_m_mem_norm_g', 'new_m_w_in', 'new_m_w_mem_kv', 'new_m_w_o', 'new_m_g_mix_pre', 'new_m_g_mix_post', 'new_m_attn_sink', 'new_m_qk_norm_g', 'new_m_w_gate_up', 'new_m_w_down', 'new_m_g_ffn_pre', 'new_m_g_ffn_post', 'new_v_mem_norm_g', 'new_v_w_in', 'new_v_w_mem_kv', 'new_v_w_o', 'new_v_g_mix_pre', 'new_v_g_mix_post', 'new_v_attn_sink', 'new_v_qk_norm_g', 'new_v_w_gate_up', 'new_v_w_down', 'new_v_g_ffn_pre', 'new_v_g_ffn_post']
TWIN_LEAF_KINDS = {'loss': 'loss', 'grad_x': 'grad_x', 'grad_mem_norm_g': 'grad_w', 'grad_w_in': 'grad_w', 'grad_w_mem_kv': 'grad_w', 'grad_w_o': 'grad_w', 'grad_g_mix_pre': 'grad_w', 'grad_g_mix_post': 'grad_w', 'grad_attn_sink': 'grad_w', 'grad_qk_norm_g': 'grad_w', 'grad_w_gate_up': 'grad_w', 'grad_w_down': 'grad_w', 'grad_g_ffn_pre': 'grad_w', 'grad_g_ffn_post': 'grad_w', 'delta_mem_norm_g': 'delta_w', 'delta_w_in': 'delta_w', 'delta_w_mem_kv': 'delta_w', 'delta_w_o': 'delta_w', 'delta_g_mix_pre': 'delta_w', 'delta_g_mix_post': 'delta_w', 'delta_attn_sink': 'delta_w', 'delta_qk_norm_g': 'delta_w', 'delta_w_gate_up': 'delta_w', 'delta_w_down': 'delta_w', 'delta_g_ffn_pre': 'delta_w', 'delta_g_ffn_post': 'delta_w', 'new_m_mem_norm_g': 'new_m', 'new_m_w_in': 'new_m', 'new_m_w_mem_kv': 'new_m', 'new_m_w_o': 'new_m', 'new_m_g_mix_pre': 'new_m', 'new_m_g_mix_post': 'new_m', 'new_m_attn_sink': 'new_m', 'new_m_qk_norm_g': 'new_m', 'new_m_w_gate_up': 'new_m', 'new_m_w_down': 'new_m', 'new_m_g_ffn_pre': 'new_m', 'new_m_g_ffn_post': 'new_m', 'new_v_mem_norm_g': 'new_v', 'new_v_w_in': 'new_v', 'new_v_w_mem_kv': 'new_v', 'new_v_w_o': 'new_v', 'new_v_g_mix_pre': 'new_v', 'new_v_g_mix_post': 'new_v', 'new_v_attn_sink': 'new_v', 'new_v_qk_norm_g': 'new_v', 'new_v_w_gate_up': 'new_v', 'new_v_w_down': 'new_v', 'new_v_g_ffn_pre': 'new_v', 'new_v_g_ffn_post': 'new_v'}


def _forward(args):
    return _fwd_reference(*[args[k] for k in FWD_PARAMS])


def _output_shape():
    def fwd():
        inp = _fwd_setup_inputs(0)
        return _fwd_reference(*[inp[k] for k in FWD_PARAMS])
    out = _jax.eval_shape(fwd)
    return out.shape, out.dtype

N_MICROBATCH = 1
ADAM_LR = 0.001
ADAM_B1 = 0.9
ADAM_B2 = 0.999
ADAM_EPS = 1e-08
ADAM_WD = 0.01
ADAM_STEP = 10
PER_EXAMPLE_BATCH_AXIS = {'x': 0, 'mem': 0, 'loss_target': 0}
SHARED_INPUTS = []
_WEIGHT_DTYPES = {'mem_norm_g': _jnp.float32, 'w_in': _jnp.float32, 'w_mem_kv': _jnp.float32, 'w_o': _jnp.float32, 'g_mix_pre': _jnp.float32, 'g_mix_post': _jnp.float32, 'attn_sink': _jnp.float32, 'qk_norm_g': _jnp.float32, 'w_gate_up': _jnp.float32, 'w_down': _jnp.float32, 'g_ffn_pre': _jnp.float32, 'g_ffn_post': _jnp.float32}
MOMENT_SCALE = {'mem_norm_g': 8.373996e+01, 'w_in': 5.346124e+01, 'w_mem_kv': 5.861211e+01, 'w_o': 8.137652e+01, 'g_mix_pre': 5.886181e+01, 'g_mix_post': 1.522020e+02, 'attn_sink': 4.854092e-01, 'qk_norm_g': 6.742281e+00, 'w_gate_up': 1.207709e+01, 'w_down': 2.309372e+01, 'g_ffn_pre': 3.259627e+01, 'g_ffn_post': 1.257981e+02}


def _to_microbatches(a, axis):
    t = _jnp.moveaxis(a, axis, 0)
    t = t.reshape((N_MICROBATCH, t.shape[0] // N_MICROBATCH) + t.shape[1:])
    return _jnp.moveaxis(t, 1, axis + 1)


def setup_inputs(seed: int = 0) -> dict:
    inp = _fwd_setup_inputs(seed)
    key = _jax.random.fold_in(_jax.random.key(seed), 7919)
    shape, _ = _output_shape()
    out = dict(inp)
    out["loss_target"] = _jax.random.normal(_jax.random.fold_in(key, 0), shape, _jnp.float32)
    for i, name in enumerate(TWIN_WEIGHTS):
        w = inp[name].astype(_jnp.float32)
        if MOMENT_SCALE is None:
            s = _jnp.sqrt(_jnp.mean(_jnp.square(w)) + 1e-30)
        else:
            s = MOMENT_SCALE[name]
        km, kv = _jax.random.split(_jax.random.fold_in(key, i + 1))
        out[name] = w
        out["m_" + name] = s * _jax.random.normal(km, w.shape, _jnp.float32)
        out["v_" + name] = (s * s) * _jax.random.uniform(kv, w.shape, _jnp.float32, 0.5, 1.5)
    if N_MICROBATCH > 1:
        for name, axis in PER_EXAMPLE_BATCH_AXIS.items():
            out[name] = _to_microbatches(out[name], axis)
    return {'x': out['x'], 'mem': out['mem'], 'mem_norm_g': out['mem_norm_g'], 'w_in': out['w_in'], 'w_mem_kv': out['w_mem_kv'], 'w_o': out['w_o'], 'g_mix_pre': out['g_mix_pre'], 'g_mix_post': out['g_mix_post'], 'attn_sink': out['attn_sink'], 'qk_norm_g': out['qk_norm_g'], 'w_gate_up': out['w_gate_up'], 'w_down': out['w_down'], 'g_ffn_pre': out['g_ffn_pre'], 'g_ffn_post': out['g_ffn_post'], 'loss_target': out['loss_target'], 'm_mem_norm_g': out['m_mem_norm_g'], 'm_w_in': out['m_w_in'], 'm_w_mem_kv': out['m_w_mem_kv'], 'm_w_o': out['m_w_o'], 'm_g_mix_pre': out['m_g_mix_pre'], 'm_g_mix_post': out['m_g_mix_post'], 'm_attn_sink': out['m_attn_sink'], 'm_qk_norm_g': out['m_qk_norm_g'], 'm_w_gate_up': out['m_w_gate_up'], 'm_w_down': out['m_w_down'], 'm_g_ffn_pre': out['m_g_ffn_pre'], 'm_g_ffn_post': out['m_g_ffn_post'], 'v_mem_norm_g': out['v_mem_norm_g'], 'v_w_in': out['v_w_in'], 'v_w_mem_kv': out['v_w_mem_kv'], 'v_w_o': out['v_w_o'], 'v_g_mix_pre': out['v_g_mix_pre'], 'v_g_mix_post': out['v_g_mix_post'], 'v_attn_sink': out['v_attn_sink'], 'v_qk_norm_g': out['v_qk_norm_g'], 'v_w_gate_up': out['v_w_gate_up'], 'v_w_down': out['v_w_down'], 'v_g_ffn_pre': out['v_g_ffn_pre'], 'v_g_ffn_post': out['v_g_ffn_post']}


def _loss(weights, diff, rest, loss_target):
    with _jax.named_scope("forward"):
        args = {**rest, TWIN_DIFF_INPUT: diff, **{k: w.astype(_WEIGHT_DTYPES[k]) for k, w in weights.items()}}
        y = _forward(args)
    with _jax.named_scope("loss_head"):
        err = _jnp.square(y.astype(_jnp.float32) - loss_target)
        return 0.5 * _jnp.sum(_jnp.mean(err, axis=-1)) if err.ndim else 0.5 * err


def _adamw(w, g, m, v):
    m = ADAM_B1 * m + (1.0 - ADAM_B1) * g
    v = ADAM_B2 * v + (1.0 - ADAM_B2) * _jnp.square(g)
    m_hat = m / (1.0 - ADAM_B1 ** ADAM_STEP)
    v_hat = v / (1.0 - ADAM_B2 ** ADAM_STEP)
    delta = -ADAM_LR * (m_hat / (_jnp.sqrt(v_hat) + ADAM_EPS) + ADAM_WD * w)
    return delta, m, v


def reference(x, mem, mem_norm_g, w_in, w_mem_kv, w_o, g_mix_pre, g_mix_post, attn_sink, qk_norm_g, w_gate_up, w_down, g_ffn_pre, g_ffn_post, loss_target, m_mem_norm_g, m_w_in, m_w_mem_kv, m_w_o, m_g_mix_pre, m_g_mix_post, m_attn_sink, m_qk_norm_g, m_w_gate_up, m_w_down, m_g_ffn_pre, m_g_ffn_post, v_mem_norm_g, v_w_in, v_w_mem_kv, v_w_o, v_g_mix_pre, v_g_mix_post, v_attn_sink, v_qk_norm_g, v_w_gate_up, v_w_down, v_g_ffn_pre, v_g_ffn_post):
    given = dict(x=x, mem=mem, mem_norm_g=mem_norm_g, w_in=w_in, w_mem_kv=w_mem_kv, w_o=w_o, g_mix_pre=g_mix_pre, g_mix_post=g_mix_post, attn_sink=attn_sink, qk_norm_g=qk_norm_g, w_gate_up=w_gate_up, w_down=w_down, g_ffn_pre=g_ffn_pre, g_ffn_post=g_ffn_post, loss_target=loss_target, m_mem_norm_g=m_mem_norm_g, m_w_in=m_w_in, m_w_mem_kv=m_w_mem_kv, m_w_o=m_w_o, m_g_mix_pre=m_g_mix_pre, m_g_mix_post=m_g_mix_post, m_attn_sink=m_attn_sink, m_qk_norm_g=m_qk_norm_g, m_w_gate_up=m_w_gate_up, m_w_down=m_w_down, m_g_ffn_pre=m_g_ffn_pre, m_g_ffn_post=m_g_ffn_post, v_mem_norm_g=v_mem_norm_g, v_w_in=v_w_in, v_w_mem_kv=v_w_mem_kv, v_w_o=v_w_o, v_g_mix_pre=v_g_mix_pre, v_g_mix_post=v_g_mix_post, v_attn_sink=v_attn_sink, v_qk_norm_g=v_qk_norm_g, v_w_gate_up=v_w_gate_up, v_w_down=v_w_down, v_g_ffn_pre=v_g_ffn_pre, v_g_ffn_post=v_g_ffn_post)
    weights = {n: given[n] for n in TWIN_WEIGHTS}
    shared = {n: given[n] for n in SHARED_INPUTS}
    per_example = {n: given[n] for n in ['x', 'mem']}
    grad_fn = _jax.value_and_grad(_loss, argnums=(0, 1))

    def one_microbatch(ex, loss_target):
        ex = dict(ex)
        diff = ex.pop(TWIN_DIFF_INPUT)
        return grad_fn(weights, diff, {**shared, **ex}, loss_target)

    if N_MICROBATCH == 1:
        loss, (grad_w, grad_x) = one_microbatch(per_example, given["loss_target"])
    else:
        def body(carry, xs):
            loss_sum, grad_sum = carry
            l_k, (gw_k, gx_k) = one_microbatch(xs[0], xs[1])
            with _jax.named_scope("update"):
                return (loss_sum + l_k, _jax.tree.map(_jnp.add, grad_sum, gw_k)), gx_k

        init = (_jnp.zeros((), _jnp.float32), _jax.tree.map(_jnp.zeros_like, weights))
        (loss, grad_w), grad_x = _jax.lax.scan(body, init, (per_example, given["loss_target"]))
    with _jax.named_scope("update"):
        delta_w, new_m, new_v = {}, {}, {}
        for n in TWIN_WEIGHTS:
            delta_w[n], new_m[n], new_v[n] = _adamw(weights[n], grad_w[n], given["m_" + n], given["v_" + n])
    return (loss, grad_x, *[grad_w[n] for n in TWIN_WEIGHTS], *[delta_w[n] for n in TWIN_WEIGHTS],
            *[new_m[n] for n in TWIN_WEIGHTS], *[new_v[n] for n in TWIN_WEIGHTS])
```

```python
import functools

import jax
import jax.numpy as jnp
from jax import lax
from jax.experimental import pallas as pl
from jax.experimental.pallas import tpu as pltpu

F32 = jnp.float32
BF16 = jnp.bfloat16

HEAD_DIM = 64
N_TOK_HEADS = 12
N_KV_HEADS = 3
GQA_GROUP = 4
N_MEM_HEADS = 4
Q_W = N_TOK_HEADS * HEAD_DIM
KV_W = N_KV_HEADS * HEAD_DIM
QM_W = N_MEM_HEADS * HEAD_DIM
QK_W = Q_W + KV_W
IN_W = Q_W + 2 * KV_W + QM_W
N_HEAD_SLOTS = IN_W // HEAD_DIM
N_MIXERS = 3
A_RADIUS = 128
C_GROUPS = ((128, 1), (512, 4), (2048, 16))
ROPE_THETA = 500000.0
ROPE_DIMS = HEAD_DIM // 4
AXIAL_THETA = 10000.0
GRID_W = 64
EPS = 1e-6
ATTN_SCALE = HEAD_DIM ** -0.5
NEG = -1e30

ADAM_LR = 0.001
ADAM_B1 = 0.9
ADAM_B2 = 0.999
ADAM_EPS = 1e-08
ADAM_WD = 0.01
ADAM_STEP = 10

N_DEV = 8
LANES = 128
VMEM_LIMIT = 56 * 1024 * 1024
MESH = pl.DeviceIdType.MESH
NT_DIMS = (((1,), (1,)), ((), ()))
TN_DIMS = (((0,), (0,)), ((), ()))


def _pc(body, **kw):
    return pl.pallas_call(body, **kw)


def _cp(*sem):
    return pltpu.CompilerParams(dimension_semantics=sem, vmem_limit_bytes=VMEM_LIMIT)


def _row_tile(m, cap=512):
    t = cap
    while m % t:
        t //= 2
    return t


def _rms_fwd(x, g, out_dtype, res=None, name="rms_fwd"):
    M, D = x.shape
    tm = _row_tile(M)

    def body(*refs):
        if res is None:
            x_ref, g_ref, o_ref = refs
        else:
            x_ref, g_ref, r_ref, o_ref = refs
        xv = x_ref[...]
        y = xv * lax.rsqrt(jnp.mean(xv * xv, axis=-1, keepdims=True) + EPS) * g_ref[...]
        if res is not None:
            y = r_ref[...] + y
        o_ref[...] = y.astype(o_ref.dtype)

    row = pl.BlockSpec((tm, D), lambda i: (i, 0))
    vec = pl.BlockSpec((1, D), lambda i: (0, 0))
    ins = [x, g] + ([] if res is None else [res])
    specs = [row, vec] + ([] if res is None else [row])
    return _pc(body, name=name, out_shape=jax.ShapeDtypeStruct((M, D), out_dtype), grid=(M // tm,),
               in_specs=specs, out_specs=row, compiler_params=_cp("parallel"))(*ins)


def _rms_bwd(x, g, dy, out_dtype, res=None, name="rms_bwd"):
    M, D = x.shape
    tm = _row_tile(M)

    def body(*refs):
        if res is None:
            x_ref, g_ref, dy_ref, dx_ref, dg_ref = refs
        else:
            x_ref, g_ref, dy_ref, r_ref, dx_ref, dg_ref = refs
        xv = x_ref[...]
        r = lax.rsqrt(jnp.mean(xv * xv, axis=-1, keepdims=True) + EPS)
        xh = xv * r
        d = dy_ref[...].astype(F32)
        dxh = d * g_ref[...]
        dx = r * (dxh - xh * jnp.mean(dxh * xh, axis=-1, keepdims=True))
        if res is not None:
            dx = r_ref[...] + dx
        dx_ref[...] = dx.astype(dx_ref.dtype)

        @pl.when(pl.program_id(0) == 0)
        def _():
            dg_ref[...] = jnp.zeros_like(dg_ref)

        dg_ref[...] += jnp.sum(d * xh, axis=0, keepdims=True)

    row = pl.BlockSpec((tm, D), lambda i: (i, 0))
    vec = pl.BlockSpec((1, D), lambda i: (0, 0))
    ins = [x, g, dy] + ([] if res is None else [res])
    specs = [row, vec, row] + ([] if res is None else [row])
    return _pc(body, name=name,
               out_shape=(jax.ShapeDtypeStruct((M, D), out_dtype), jax.ShapeDtypeStruct((1, D), F32)),
               grid=(M // tm,), in_specs=specs, out_specs=(row, vec), compiler_params=_cp("arbitrary"))(*ins)


def _swiglu_bwd(gu, da, name="swiglu_bwd"):
    M, F2 = gu.shape
    F = F2 // 2
    tm = _row_tile(M, 256)

    def body(gu_ref, da_ref, o_ref):
        g = gu_ref[:, :F].astype(F32)
        u = gu_ref[:, F:].astype(F32)
        d = da_ref[...].astype(F32)
        sig = 1.0 / (1.0 + jnp.exp(-g))
        o_ref[:, :F] = (d * u * (sig * (1.0 + g * (1.0 - sig)))).astype(o_ref.dtype)
        o_ref[:, F:] = (d * (g * sig)).astype(o_ref.dtype)

    return _pc(body, name=name, out_shape=jax.ShapeDtypeStruct((M, F2), BF16), grid=(M // tm,),
               in_specs=[pl.BlockSpec((tm, F2), lambda i: (i, 0)), pl.BlockSpec((tm, F), lambda i: (i, 0))],
               out_specs=pl.BlockSpec((tm, F2), lambda i: (i, 0)), compiler_params=_cp("parallel"))(gu, da)


def _loss_head(y, t, name="loss_head"):
    M, D = y.shape
    tm = _row_tile(M)

    def body(y_ref, t_ref, dy_ref, acc_ref):
        e = y_ref[...] - t_ref[...]
        dy_ref[...] = e * (1.0 / D)

        @pl.when(pl.program_id(0) == 0)
        def _():
            acc_ref[...] = jnp.zeros_like(acc_ref)

        acc_ref[...] += jnp.sum(e * e)

    row = pl.BlockSpec((tm, D), lambda i: (i, 0))
    return _pc(body, name=name,
               out_shape=(jax.ShapeDtypeStruct((M, D), F32), jax.ShapeDtypeStruct((8, LANES), F32)),
               grid=(M // tm,), in_specs=[row, row],
               out_specs=(row, pl.BlockSpec((8, LANES), lambda i: (0, 0))), compiler_params=_cp("arbitrary"))(y, t)


def _mm(a, w, out_dtype, nt=False, swiglu=False, tm=256, name="mm"):
    M, K = a.shape
    N = w.shape[0] if nt else w.shape[1]
    tm = _row_tile(M, tm)

    def body(a_ref, w_ref, *o_refs):
        if nt:
            acc = lax.dot_general(a_ref[...], w_ref[...], NT_DIMS, preferred_element_type=F32)
        else:
            acc = jnp.dot(a_ref[...], w_ref[...], preferred_element_type=F32)
        o_refs[0][...] = acc.astype(o_refs[0].dtype)
        if swiglu:
            g = acc[:, : N // 2]
            o_refs[1][...] = (g * (1.0 / (1.0 + jnp.exp(-g))) * acc[:, N // 2:]).astype(o_refs[1].dtype)

    w_spec = pl.BlockSpec(w.shape, lambda i: (0, 0), pipeline_mode=pl.Buffered(1))
    out_shape = jax.ShapeDtypeStruct((M, N), out_dtype)
    out_spec = pl.BlockSpec((tm, N), lambda i: (i, 0))
    if swiglu:
        out_shape = (out_shape, jax.ShapeDtypeStruct((M, N // 2), out_dtype))
        out_spec = (out_spec, pl.BlockSpec((tm, N // 2), lambda i: (i, 0)))
    return _pc(body, name=name, out_shape=out_shape, grid=(M // tm,),
               in_specs=[pl.BlockSpec((tm, K), lambda i: (i, 0)), w_spec], out_specs=out_spec,
               compiler_params=_cp("parallel"))(a, w)


def _mm_tn(a, b, name="mm_tn"):
    S, M = a.shape
    N = b.shape[1]
    tm = M if M <= 1408 else M // 2
    tn = N if N <= 1408 else N // 4
    ts = _row_tile(S, 1024)

    def body(a_ref, b_ref, o_ref):
        @pl.when(pl.program_id(2) == 0)
        def _():
            o_ref[...] = jnp.zeros_like(o_ref)

        o_ref[...] += lax.dot_general(a_ref[...], b_ref[...], TN_DIMS, preferred_element_type=F32)

    return _pc(body, name=name, out_shape=jax.ShapeDtypeStruct((M, N), F32), grid=(M // tm, N // tn, S // ts),
               in_specs=[pl.BlockSpec((ts, tm), lambda i, j, s: (s, i)), pl.BlockSpec((ts, tn), lambda i, j, s: (s, j))],
               out_specs=pl.BlockSpec((tm, tn), lambda i, j, s: (i, j)),
               compiler_params=_cp("parallel", "parallel", "arbitrary"))(a, b)


def _rope_tables(S):
    pos = jnp.arange(S, dtype=jnp.int32)

    def table(p, n_dims, theta):
        inv = theta ** (-(jnp.arange(0, n_dims, 2, dtype=F32) / n_dims))
        ang = p.astype(F32)[:, None] * inv[None, :]
        return jnp.cos(ang), jnp.sin(ang)

    one = lambda n: jnp.ones((S, n), F32)
    zero = lambda n: jnp.zeros((S, n), F32)
    cp, sp = table(pos, ROPE_DIMS, ROPE_THETA)
    rest = HEAD_DIM - ROPE_DIMS
    part = (jnp.concatenate([cp, cp, one(rest)], 1), jnp.concatenate([zero(8), sp, zero(rest)], 1),
            jnp.concatenate([-sp, zero(8), zero(rest)], 1))
    cr, sr = table(pos // GRID_W, HEAD_DIM // 2, AXIAL_THETA)
    cc, sc = table(pos % GRID_W, HEAD_DIM // 2, AXIAL_THETA)
    axial = (jnp.concatenate([cr, cr, cc, cc], 1), jnp.concatenate([zero(16), sr, zero(16), sc], 1),
             jnp.concatenate([-sr, zero(16), -sc, zero(16)], 1))
    rep = LANES // HEAD_DIM
    return (tuple(jnp.tile(t, (1, rep)) for t in part), ROPE_DIMS // 2), (tuple(jnp.tile(t, (1, rep)) for t in axial), HEAD_DIM // 4)


def _seg_mats():
    col = jnp.arange(IN_W)[:, None] // HEAD_DIM
    e = (col == jnp.arange(LANES)[None, :]).astype(F32)
    return e, e.T


def _qk_gain_row(qk_g):
    return jnp.concatenate([jnp.tile(qk_g[0], N_TOK_HEADS), jnp.tile(qk_g[1], N_KV_HEADS),
                            jnp.ones((IN_W - QK_W,), F32)])[None, :]


def _rope_cols(tabs, tm):
    col = lax.broadcasted_iota(jnp.int32, (tm, IN_W), 1)
    qk = col < QK_W
    c, s_lo, s_hi = (jnp.tile(t[...], (1, IN_W // LANES)) for t in tabs)
    return jnp.where(qk, c, 1.0), jnp.where(qk, s_lo, 0.0), jnp.where(qk, s_hi, 0.0), qk


def _seg_mean(v, e_ref, et_ref):
    s = jnp.dot(v, e_ref[...], precision=lax.Precision.HIGHEST, preferred_element_type=F32) * (1.0 / HEAD_DIM)
    return jnp.dot(s, et_ref[...], precision=lax.Precision.HIGHEST, preferred_element_type=F32)


def _headprep_fwd(proj, tabs, shift, qk_gain=None, name="headprep_fwd"):
    S = proj.shape[0]
    tm = _row_tile(S, 256)
    norm = qk_gain is not None

    def body(*refs):
        if norm:
            p_ref, c_ref, lo_ref, hi_ref, g_ref, e_ref, et_ref, o_ref = refs
        else:
            p_ref, c_ref, lo_ref, hi_ref, o_ref = refs
        x = p_ref[...]
        c, s_lo, s_hi, qk = _rope_cols((c_ref, lo_ref, hi_ref), tm)
        if norm:
            r = lax.rsqrt(_seg_mean(x * x, e_ref, et_ref) + EPS)
            x = x * jnp.where(qk, r, 1.0) * g_ref[...]
        y = x * c + pltpu.roll(x, shift, 1) * s_lo + pltpu.roll(x, IN_W - shift, 1) * s_hi
        o_ref[...] = y.astype(o_ref.dtype)

    row = pl.BlockSpec((tm, IN_W), lambda i: (i, 0))
    tab = pl.BlockSpec((tm, LANES), lambda i: (i, 0))
    ins = [proj, *tabs]
    specs = [row, tab, tab, tab]
    if norm:
        e, et = _seg_mats()
        ins += [qk_gain, e, et]
        specs += [pl.BlockSpec((1, IN_W), lambda i: (0, 0)), pl.BlockSpec((IN_W, LANES), lambda i: (0, 0)),
                  pl.BlockSpec((LANES, IN_W), lambda i: (0, 0))]
    return _pc(body, name=name, out_shape=jax.ShapeDtypeStruct((S, IN_W), BF16), grid=(S // tm,),
               in_specs=specs, out_specs=row, compiler_params=_cp("parallel"))(*ins)


def _headprep_bwd(dpr, tabs, shift, proj=None, qk_gain=None, name="headprep_bwd"):
    S = dpr.shape[0]
    tm = _row_tile(S, 256)
    norm = qk_gain is not None

    def body(*refs):
        if norm:
            d_ref, c_ref, lo_ref, hi_ref, p_ref, g_ref, e_ref, et_ref, o_ref, dg_ref = refs
        else:
            d_ref, c_ref, lo_ref, hi_ref, o_ref = refs
        d = d_ref[...].astype(F32)
        c, s_lo, s_hi, qk = _rope_cols((c_ref, lo_ref, hi_ref), tm)
        dx = d * c + pltpu.roll(d * s_lo, IN_W - shift, 1) + pltpu.roll(d * s_hi, shift, 1)
        if norm:
            x = p_ref[...]
            r = lax.rsqrt(_seg_mean(x * x, e_ref, et_ref) + EPS)
            xh = x * r

            @pl.when(pl.program_id(0) == 0)
            def _():
                dg_ref[...] = jnp.zeros_like(dg_ref)

            dg_ref[...] += jnp.sum(jnp.where(qk, dx * xh, 0.0), axis=0, keepdims=True)
            dxh = dx * g_ref[...]
            dn = r * (dxh - xh * _seg_mean(dxh * xh, e_ref, et_ref))
            dx = jnp.where(qk, dn, dx)
        o_ref[...] = dx.astype(o_ref.dtype)

    row = pl.BlockSpec((tm, IN_W), lambda i: (i, 0))
    tab = pl.BlockSpec((tm, LANES), lambda i: (i, 0))
    vec = pl.BlockSpec((1, IN_W), lambda i: (0, 0))
    ins = [dpr, *tabs]
    specs = [row, tab, tab, tab]
    out_shape = jax.ShapeDtypeStruct((S, IN_W), BF16)
    out_specs = row
    if norm:
        e, et = _seg_mats()
        ins += [proj, qk_gain, e, et]
        specs += [row, vec, pl.BlockSpec((IN_W, LANES), lambda i: (0, 0)), pl.BlockSpec((LANES, IN_W), lambda i: (0, 0))]
        out_shape = (out_shape, jax.ShapeDtypeStruct((1, IN_W), F32))
        out_specs = (row, vec)
    return _pc(body, name=name, out_shape=out_shape, grid=(S // tm,), in_specs=specs, out_specs=out_specs,
               compiler_params=_cp("arbitrary" if norm else "parallel"))(*ins)


def _band_mask(qb, kb, n_kb_valid, bq, tk, rows, radius):
    qpos = qb * bq + lax.broadcasted_iota(jnp.int32, (rows, 1), 0) % bq
    kpos = kb * tk + lax.broadcasted_iota(jnp.int32, (1, tk), 1)
    return (jnp.abs(qpos - kpos) <= radius) & n_kb_valid


def _flash_fwd(q, k, v, *, radius=None, sink=None, bq, tk, name="flash_fwd"):
    NB, G, L, HD = q.shape
    Lk = k.shape[1]
    nq, nk = L // bq, Lk // tk
    banded = radius is not None
    assert not banded or (bq == tk and radius <= tk and L == Lk)
    nst = 3 if banded else nk
    rows = G * bq

    def kmap(b, n, j):
        return (b, jnp.clip(n - 1 + j, 0, nk - 1), 0) if banded else (b, j, 0)

    def body(*refs):
        if sink is not None:
            sink_ref, *refs = refs
        q_ref, k_ref, v_ref, o_ref, lse_ref, m_sc, l_sc, acc_sc = refs
        b, n, j = pl.program_id(0), pl.program_id(1), pl.program_id(2)

        @pl.when(j == 0)
        def _():
            if sink is not None:
                head = lax.broadcasted_iota(jnp.int32, (rows, 1), 0) // bq
                m0 = jnp.zeros((rows, 1), F32)
                for g in range(G):
                    m0 = jnp.where(head == g, sink_ref[b * G + g], m0)
                m_sc[...] = m0
                l_sc[...] = jnp.ones_like(l_sc)
            else:
                m_sc[...] = jnp.full_like(m_sc, NEG)
                l_sc[...] = jnp.zeros_like(l_sc)
            acc_sc[...] = jnp.zeros_like(acc_sc)

        qv = q_ref[0].reshape(rows, HD)
        s = lax.dot_general(qv, k_ref[0], NT_DIMS, preferred_element_type=F32) * ATTN_SCALE
        if banded:
            kb = n - 1 + j
            mask = _band_mask(n, kb, (kb >= 0) & (kb < nk), bq, tk, rows, radius)
            s = jnp.where(mask, s, NEG)
        m_prev = m_sc[...]
        m_new = jnp.maximum(m_prev, jnp.max(s, axis=1, keepdims=True))
        p = jnp.exp(s - m_new)
        if banded:
            p = jnp.where(mask, p, 0.0)
        alpha = jnp.exp(m_prev - m_new)
        l_sc[...] = alpha * l_sc[...] + jnp.sum(p, axis=1, keepdims=True)
        acc_sc[...] = alpha * acc_sc[...] + jnp.dot(p.astype(BF16), v_ref[0], preferred_element_type=F32)
        m_sc[...] = m_new

        @pl.when(j == nst - 1)
        def _():
            l = l_sc[...]
            o_ref[0] = (acc_sc[...] / l).reshape(G, bq, HD).astype(o_ref.dtype)
            lse_ref[0] = (m_sc[...] + jnp.log(l)).reshape(G, bq, 1)

    qspec = pl.BlockSpec((1, G, bq, HD), lambda b, n, j: (b, 0, n, 0))
    kspec = pl.BlockSpec((1, tk, HD), kmap)
    ins, specs = [q, k, v], [qspec, kspec, kspec]
    if sink is not None:
        ins, specs = [sink] + ins, [pl.BlockSpec(memory_space=pltpu.SMEM)] + specs
    return _pc(body, name=name,
               out_shape=(jax.ShapeDtypeStruct((NB, G, L, HD), BF16), jax.ShapeDtypeStruct((NB, G, L, 1), F32)),
               grid=(NB, nq, nst), in_specs=specs,
               out_specs=(qspec, pl.BlockSpec((1, G, bq, 1), lambda b, n, j: (b, 0, n, 0))),
               scratch_shapes=[pltpu.VMEM((rows, 1), F32), pltpu.VMEM((rows, 1), F32), pltpu.VMEM((rows, HD), F32)],
               compiler_params=_cp("parallel", "parallel", "arbitrary"))(*ins)


def _attn_delta(do, o, *, dlse=None, lse=None, sink=None, name="attn_delta"):
    NB, G, L, HD = do.shape
    bl = _row_tile(L, 1024)

    def body(*refs):
        refs = list(refs)
        sink_ref = refs.pop(0) if sink is not None else None
        do_ref, o_ref = refs.pop(0), refs.pop(0)
        dlse_ref = refs.pop(0) if dlse is not None else None
        lse_ref = refs.pop(0) if sink is not None else None
        delta_ref = refs.pop(0)
        b = pl.program_id(0)
        delta = jnp.sum(do_ref[0].astype(F32) * o_ref[0].astype(F32), axis=-1, keepdims=True)
        if dlse is not None:
            delta = delta - dlse_ref[0]
        delta_ref[0] = delta
        if sink is not None:
            ds_ref = refs.pop(0)

            @pl.when(pl.program_id(1) == 0)
            def _():
                ds_ref[...] = jnp.zeros_like(ds_ref)

            for g in range(G):
                ps = jnp.exp(sink_ref[b * G + g] - lse_ref[0, g]) * delta[g]
                ds_ref[0, g] -= jnp.sum(ps)

    blk = pl.BlockSpec((1, G, bl, HD), lambda b, n: (b, 0, n, 0))
    col = pl.BlockSpec((1, G, bl, 1), lambda b, n: (b, 0, n, 0))
    ins, specs = [do, o], [blk, blk]
    if dlse is not None:
        ins, specs = ins + [dlse], specs + [col]
    out_shape = jax.ShapeDtypeStruct((NB, G, L, 1), F32)
    out_specs = col
    if sink is not None:
        ins, specs = [sink] + ins + [lse], [pl.BlockSpec(memory_space=pltpu.SMEM)] + specs + [col]
        out_shape = (out_shape, jax.ShapeDtypeStruct((NB, G, 1, LANES), F32))
        out_specs = (col, pl.BlockSpec((1, G, 1, LANES), lambda b, n: (b, 0, 0, 0)))
    return _pc(body, name=name, out_shape=out_shape, grid=(NB, L // bl), in_specs=specs, out_specs=out_specs,
               compiler_params=_cp("parallel", "arbitrary"))(*ins)


def _flash_dq(q, k, v, do, lse, delta, *, radius=None, bq, tk, name="flash_dq"):
    NB, G, L, HD = q.shape
    Lk = k.shape[1]
    nq, nk = L // bq, Lk // tk
    banded = radius is not None
    nst = 3 if banded else nk
    rows = G * bq

    def kmap(b, n, j):
        return (b, jnp.clip(n - 1 + j, 0, nk - 1), 0) if banded else (b, j, 0)

    def body(q_ref, k_ref, v_ref, do_ref, lse_ref, dl_ref, dq_ref, acc_sc):
        n, j = pl.program_id(1), pl.program_id(2)

        @pl.when(j == 0)
        def _():
            acc_sc[...] = jnp.zeros_like(acc_sc)

        qv = q_ref[0].reshape(rows, HD)
        dov = do_ref[0].reshape(rows, HD)
        s = lax.dot_general(qv, k_ref[0], NT_DIMS, preferred_element_type=F32) * ATTN_SCALE
        p = jnp.exp(s - lse_ref[0].reshape(rows, 1))
        if banded:
            kb = n - 1 + j
            p = jnp.where(_band_mask(n, kb, (kb >= 0) & (kb < nk), bq, tk, rows, radius), p, 0.0)
        dp = lax.dot_general(dov, v_ref[0], NT_DIMS, preferred_element_type=F32)
        ds = p * (dp - dl_ref[0].reshape(rows, 1))
        acc_sc[...] += jnp.dot(ds.astype(BF16), k_ref[0], preferred_element_type=F32)

        @pl.when(j == nst - 1)
        def _():
            dq_ref[0] = (acc_sc[...] * ATTN_SCALE).reshape(G, bq, HD).astype(dq_ref.dtype)

    qspec = pl.BlockSpec((1, G, bq, HD), lambda b, n, j: (b, 0, n, 0))
    cspec = pl.BlockSpec((1, G, bq, 1), lambda b, n, j: (b, 0, n, 0))
    kspec = pl.BlockSpec((1, tk, HD), kmap)
    return _pc(body, name=name, out_shape=jax.ShapeDtypeStruct((NB, G, L, HD), F32), grid=(NB, nq, nst),
               in_specs=[qspec, kspec, kspec, qspec, cspec, cspec], out_specs=qspec,
               scratch_shapes=[pltpu.VMEM((rows, HD), F32)],
               compiler_params=_cp("parallel", "parallel", "arbitrary"))(q, k, v, do, lse, delta)


def _flash_dkv(q, k, v, do, lse, delta, *, radius=None, bq, tk, name="flash_dkv"):
    NB, G, L, HD = q.shape
    Lk = k.shape[1]
    nq, nk = L // bq, Lk // tk
    banded = radius is not None
    nst = 3 if banded else nq
    rows = G * bq

    def qmap(b, m, j):
        return (b, 0, jnp.clip(m - 1 + j, 0, nq - 1), 0) if banded else (b, 0, j, 0)

    def body(q_ref, k_ref, v_ref, do_ref, lse_ref, dl_ref, dk_ref, dv_ref, dk_sc, dv_sc):
        m, j = pl.program_id(1), pl.program_id(2)

        @pl.when(j == 0)
        def _():
            dk_sc[...] = jnp.zeros_like(dk_sc)
            dv_sc[...] = jnp.zeros_like(dv_sc)

        qv = q_ref[0].reshape(rows, HD)
        dov = do_ref[0].reshape(rows, HD)
        s = lax.dot_general(qv, k_ref[0], NT_DIMS, preferred_element_type=F32) * ATTN_SCALE
        p = jnp.exp(s - lse_ref[0].reshape(rows, 1))
        if banded:
            qb = m - 1 + j
            p = jnp.where(_band_mask(qb, m, (qb >= 0) & (qb < nq), bq, tk, rows, radius), p, 0.0)
        dv_sc[...] += lax.dot_general(p.astype(BF16), dov, TN_DIMS, preferred_element_type=F32)
        dp = lax.dot_general(dov, v_ref[0], NT_DIMS, preferred_element_type=F32)
        ds = p * (dp - dl_ref[0].reshape(rows, 1))
        dk_sc[...] += lax.dot_general(ds.astype(BF16), qv, TN_DIMS, preferred_element_type=F32)

        @pl.when(j == nst - 1)
        def _():
            dk_ref[0] = (dk_sc[...] * ATTN_SCALE).astype(dk_ref.dtype)
            dv_ref[0] = dv_sc[...].astype(dv_ref.dtype)

    qspec = pl.BlockSpec((1, G, bq, HD), qmap)
    cspec = pl.BlockSpec((1, G, bq, 1), qmap)
    kspec = pl.BlockSpec((1, tk, HD), lambda b, m, j: (b, m, 0))
    kv_shape = jax.ShapeDtypeStruct((NB, Lk, HD), F32)
    return _pc(body, name=name, out_shape=(kv_shape, kv_shape), grid=(NB, nk, nst),
               in_specs=[qspec, kspec, kspec, qspec, cspec, cspec], out_specs=(kspec, kspec),
               scratch_shapes=[pltpu.VMEM((tk, HD), F32), pltpu.VMEM((tk, HD), F32)],
               compiler_params=_cp("parallel", "parallel", "arbitrary"))(q, k, v, do, lse, delta)


def _attention(q, k, v, *, radius=None, sink=None, bq, tk, tag):
    o, lse = _flash_fwd(q, k, v, radius=radius, sink=sink, bq=bq, tk=tk, name=f"flash_fwd_{tag}")
    return o, lse


def _attention_bwd(q, k, v, o, lse, do, *, radius=None, sink=None, dlse=None, bq, tk, tag):
    if sink is not None:
        delta, ds = _attn_delta(do, o, lse=lse, sink=sink, name=f"attn_delta_{tag}")
        dsink = ds[:, :, 0, 0].reshape(-1)
    else:
        delta, dsink = _attn_delta(do, o, dlse=dlse, name=f"attn_delta_{tag}"), None
    dq = _flash_dq(q, k, v, do, lse, delta, radius=radius, bq=bq, tk=tk, name=f"flash_dq_{tag}")
    dk, dv = _flash_dkv(q, k, v, do, lse, delta, radius=radius, bq=bq, tk=tk, name=f"flash_dkv_{tag}")
    return dq, dk, dv, dsink


def _combine_fwd(o, lse, name="combine_fwd"):
    H, S, HD = o.shape
    tm = _row_tile(S, 512)

    def body(o_ref, lse_ref, t_ref):
        for g in range(GQA_GROUP):
            hs = [kv * GQA_GROUP + g for kv in range(N_KV_HEADS)]
            ls = [lse_ref[h] for h in hs]
            mx = functools.reduce(jnp.maximum, ls)
            es = [jnp.exp(l - mx) for l in ls]
            den = functools.reduce(jnp.add, es)
            for h, e in zip(hs, es):
                t_ref[h] = (o_ref[h].astype(F32) * (e / den)).astype(t_ref.dtype)

    blk = pl.BlockSpec((H, tm, HD), lambda i: (0, i, 0))
    col = pl.BlockSpec((H, tm, 1), lambda i: (0, i, 0))
    return _pc(body, name=name, out_shape=jax.ShapeDtypeStruct((H, S, HD), BF16), grid=(S // tm,),
               in_specs=[blk, col], out_specs=blk, compiler_params=_cp("parallel"))(o, lse)


def _combine_bwd(dt, o, lse, name="combine_bwd"):
    H, S, HD = o.shape
    tm = _row_tile(S, 512)

    def body(dt_ref, o_ref, lse_ref, do_ref, dlse_ref):
        for g in range(GQA_GROUP):
            hs = [kv * GQA_GROUP + g for kv in range(N_KV_HEADS)]
            ls = [lse_ref[h] for h in hs]
            mx = functools.reduce(jnp.maximum, ls)
            es = [jnp.exp(l - mx) for l in ls]
            den = functools.reduce(jnp.add, es)
            al = [e / den for e in es]
            dts = [dt_ref[h].astype(F32) for h in hs]
            da = [jnp.sum(d * o_ref[h].astype(F32), axis=-1, keepdims=True) for h, d in zip(hs, dts)]
            dot = functools.reduce(jnp.add, [a * d for a, d in zip(al, da)])
            for h, a, d, dd in zip(hs, al, da, dts):
                do_ref[h] = (dd * a).astype(do_ref.dtype)
                dlse_ref[h] = a * (d - dot)

    blk = pl.BlockSpec((H, tm, HD), lambda i: (0, i, 0))
    col = pl.BlockSpec((H, tm, 1), lambda i: (0, i, 0))
    return _pc(body, name=name,
               out_shape=(jax.ShapeDtypeStruct((H, S, HD), BF16), jax.ShapeDtypeStruct((H, S, 1), F32)),
               grid=(S // tm,), in_specs=[blk, blk, col], out_specs=(blk, col), compiler_params=_cp("parallel"))(dt, o, lse)


def _position():
    x, y, c = lax.axis_index("x"), lax.axis_index("y"), lax.axis_index("c")
    return x, y, c


def _peer(pos, k):
    x, y, c = pos
    return (1 - x if k & 4 else x, 1 - y if k & 2 else y, 1 - c if k & 1 else c)


def _linear(p):
    return 4 * p[0] + 2 * p[1] + p[2]


def _exchange(buf, gather, name):
    out_shape = ((N_DEV,) + buf.shape) if gather else buf.shape

    def body(s_ref, r_ref, send_sems, recv_sems, local_sem):
        pos = _position()
        me = _linear(pos)
        own = pltpu.make_async_copy(s_ref if gather else s_ref.at[me], r_ref.at[me], local_sem)
        own.start()
        sends = []
        for k in range(1, N_DEV):
            peer = _peer(pos, k)
            src = s_ref if gather else s_ref.at[_linear(peer)]
            cp = pltpu.make_async_remote_copy(src_ref=src, dst_ref=r_ref.at[me], send_sem=send_sems.at[k - 1],
                                              recv_sem=recv_sems.at[k - 1], device_id=peer, device_id_type=MESH)
            cp.start()
            sends.append(cp)
        for k in range(1, N_DEV):
            peer = _peer(pos, k)
            slot = r_ref.at[_linear(peer)]
            pltpu.make_async_remote_copy(src_ref=slot, dst_ref=slot, send_sem=send_sems.at[k - 1],
                                         recv_sem=recv_sems.at[k - 1], device_id=peer, device_id_type=MESH).wait_recv()
        for cp in sends:
            cp.wait_send()
        own.wait()

    hbm = pl.BlockSpec(memory_space=pltpu.HBM)
    return _pc(body, name=name, out_shape=jax.ShapeDtypeStruct(out_shape, buf.dtype), in_specs=[hbm], out_specs=hbm,
               scratch_shapes=[pltpu.SemaphoreType.DMA((N_DEV - 1,)), pltpu.SemaphoreType.DMA((N_DEV - 1,)),
                               pltpu.SemaphoreType.DMA])(buf)


def _reduce_adamw(recv, w, m, v, name):
    _, R, C = recv.shape
    tr = _row_tile(R, 512)

    def body(r_ref, w_ref, m_ref, v_ref, g_ref, d_ref, nm_ref, nv_ref):
        g = r_ref[0].astype(F32)
        for j in range(1, N_DEV):
            g = g + r_ref[j].astype(F32)
        g_ref[...] = g
        nm = ADAM_B1 * m_ref[...] + (1.0 - ADAM_B1) * g
        nv = ADAM_B2 * v_ref[...] + (1.0 - ADAM_B2) * jnp.square(g)
        m_hat = nm / (1.0 - ADAM_B1 ** ADAM_STEP)
        v_hat = nv / (1.0 - ADAM_B2 ** ADAM_STEP)
        d_ref[...] = -ADAM_LR * (m_hat / (jnp.sqrt(v_hat) + ADAM_EPS) + ADAM_WD * w_ref[...])
        nm_ref[...] = nm
        nv_ref[...] = nv

    row = pl.BlockSpec((tr, C), lambda i: (i, 0))
    out = jax.ShapeDtypeStruct((R, C), F32)
    return _pc(body, name=name, out_shape=(out, out, out, out), grid=(R // tr,),
               in_specs=[pl.BlockSpec((N_DEV, tr, C), lambda i: (0, i, 0)), row, row, row],
               out_specs=(row, row, row, row), compiler_params=_cp("parallel"))(recv, w, m, v)


BIG = (("w_in", 2), ("w_mem_kv", 1), ("w_o", 1), ("w_gate_up", 2), ("w_down", 1))
SMALL = ("mem_norm_g", "g_mix_pre", "g_mix_post", "attn_sink", "qk_norm_g", "g_ffn_pre", "g_ffn_post")
SMALL_W = 1024


def _pack_local(shards, dtype):
    return jnp.concatenate([s.astype(dtype).reshape(-1, LANES) for s in shards], axis=0)


def _unpack_local(flat, shapes):
    out, r = [], 0
    for shp in shapes:
        n = shp[0] * shp[1] * shp[2] // LANES
        out.append(flat[r:r + n].reshape(shp))
        r += n
    return out


def _unpack_gathered(g, shapes):
    out, r = [], 0
    for (name, dim), shp in zip(BIG, shapes):
        n = shp[0] * shp[1] * shp[2] // LANES
        t = g[:, r:r + n].reshape((N_DEV,) + tuple(shp))
        if dim == 2:
            t = t.transpose(1, 2, 0, 3).reshape(shp[0], shp[1], N_DEV * shp[2])
        else:
            t = t.transpose(1, 0, 2, 3).reshape(shp[0], N_DEV * shp[1], shp[2])
        out.append(t)
        r += n
    return out


def _pack_for_scatter(full, shapes, dtype):
    parts = []
    for (name, dim), shp, t in zip(BIG, shapes, full):
        if dim == 2:
            t = t.reshape(shp[0], shp[1], N_DEV, shp[2]).transpose(2, 0, 1, 3)
        else:
            t = t.reshape(shp[0], N_DEV, shp[1], shp[2]).transpose(1, 0, 2, 3)
        parts.append(t.astype(dtype).reshape(N_DEV, -1, LANES))
    return jnp.concatenate(parts, axis=1)


def _pack_small(arrs):
    flat = jnp.concatenate([a.reshape(-1) for a in arrs])
    pad = (-flat.shape[0]) % (8 * SMALL_W)
    return jnp.pad(flat, (0, pad)).reshape(-1, SMALL_W)


def _unpack_small(flat, shapes):
    flat = flat.reshape(-1)
    out, r = [], 0
    for shp in shapes:
        n = 1
        for d in shp:
            n *= d
        out.append(flat[r:r + n].reshape(shp))
        r += n
    return out


def _heads(t, nb, g):
    S = t.shape[0]
    return t.reshape(S, nb, g, HEAD_DIM).transpose(1, 2, 0, 3)


def _unheads(t):
    nb, g, S, hd = t.shape
    return t.transpose(2, 0, 1, 3).reshape(S, nb * g * hd)


def _dilate(t, dil):
    S = t.shape[0]
    g = t.shape[1] // HEAD_DIM
    return t.reshape(S // dil, dil, g, HEAD_DIM).transpose(1, 2, 0, 3)


def _undilate(t):
    dil, g, L, w = t.shape
    return t.transpose(1, 2, 0, 3).reshape(g, L * dil, w)


def _full_tiles(S):
    return min(256, S), min(1024, S)


def _mixer_fwd(kind, pr, sink, li):
    S = pr.shape[0]
    if kind == 0:
        q = _heads(pr[:, :Q_W], N_KV_HEADS, GQA_GROUP)
        k = _heads(pr[:, Q_W:QK_W], N_KV_HEADS, 1)[:, 0]
        v = _heads(pr[:, QK_W:QK_W + KV_W], N_KV_HEADS, 1)[:, 0]
        o, lse = _attention(q, k, v, radius=A_RADIUS, sink=sink, bq=128, tk=128, tag=f"a{li}")
        return _unheads(o), (q, k, v, o, lse)
    if kind == 1:
        q = _heads(pr[:, :Q_W], N_KV_HEADS, GQA_GROUP)
        k = _heads(pr[:, Q_W:QK_W], N_KV_HEADS, 1)[:, 0]
        v = _heads(pr[:, QK_W:QK_W + KV_W], N_KV_HEADS, 1)[:, 0]
        bq, tk = _full_tiles(S)
        o, lse = _attention(q, k, v, bq=bq, tk=tk, tag=f"b{li}")
        return _unheads(o), (q, k, v, o, lse)
    saved, outs, lses = [], [], []
    for g, (window, dil) in enumerate(C_GROUPS):
        q = _dilate(pr[:, g * GQA_GROUP * HEAD_DIM:(g + 1) * GQA_GROUP * HEAD_DIM], dil)
        k = _dilate(pr[:, Q_W + g * HEAD_DIM:Q_W + (g + 1) * HEAD_DIM], dil)[:, 0]
        v = _dilate(pr[:, QK_W + g * HEAD_DIM:QK_W + (g + 1) * HEAD_DIM], dil)[:, 0]
        blk = min(128, S // dil)
        o, lse = _attention(q, k, v, radius=window // (2 * dil), bq=blk, tk=blk, tag=f"c{li}g{g}")
        saved.append((q, k, v, o, lse))
        outs.append(_undilate(o))
        lses.append(_undilate(lse))
    o_all, lse_all = jnp.concatenate(outs, 0), jnp.concatenate(lses, 0)
    tok = _combine_fwd(o_all, lse_all, name=f"combine_fwd_{li}")
    return tok.transpose(1, 0, 2).reshape(S, Q_W), (saved, o_all, lse_all)


def _mixer_bwd(kind, dtok, saved, sink, li):
    S = dtok.shape[0]
    if kind in (0, 1):
        q, k, v, o, lse = saved
        do = _heads(dtok, N_KV_HEADS, GQA_GROUP)
        if kind == 0:
            dq, dk, dv, dsink = _attention_bwd(q, k, v, o, lse, do, radius=A_RADIUS, sink=sink, bq=128, tk=128, tag=f"a{li}")
        else:
            bq, tk = _full_tiles(S)
            dq, dk, dv, dsink = _attention_bwd(q, k, v, o, lse, do, bq=bq, tk=tk, tag=f"b{li}")
        return _unheads(dq), _unheads(dk[:, None]), _unheads(dv[:, None]), dsink
    per_group, o_all, lse_all = saved
    dt = dtok.reshape(S, N_TOK_HEADS, HEAD_DIM).transpose(1, 0, 2)
    do_all, dlse_all = _combine_bwd(dt, o_all, lse_all, name=f"combine_bwd_{li}")
    dqs, dks, dvs = [], [], []
    for g, (window, dil) in enumerate(C_GROUPS):
        q, k, v, o, lse = per_group[g]
        L = S // dil
        hs = slice(g * GQA_GROUP, (g + 1) * GQA_GROUP)
        do = do_all[hs].reshape(GQA_GROUP, L, dil, HEAD_DIM).transpose(2, 0, 1, 3)
        dlse = dlse_all[hs].reshape(GQA_GROUP, L, dil, 1).transpose(2, 0, 1, 3)
        blk = min(128, L)
        dq, dk, dv, _ = _attention_bwd(q, k, v, o, lse, do, radius=window // (2 * dil), dlse=dlse, bq=blk, tk=blk,
                                       tag=f"c{li}g{g}")
        dqs.append(dq.transpose(2, 0, 1, 3).reshape(S, GQA_GROUP * HEAD_DIM))
        dks.append(dk.transpose(1, 0, 2).reshape(S, HEAD_DIM))
        dvs.append(dv.transpose(1, 0, 2).reshape(S, HEAD_DIM))
    return jnp.concatenate(dqs, 1), jnp.concatenate(dks, 1), jnp.concatenate(dvs, 1), None


def kernel(x, mem, mem_norm_g, w_in, w_mem_kv, w_o, g_mix_pre, g_mix_post, attn_sink, qk_norm_g, w_gate_up, w_down, g_ffn_pre, g_ffn_post, loss_target, m_mem_norm_g, m_w_in, m_w_mem_kv, m_w_o, m_g_mix_pre, m_g_mix_post, m_attn_sink, m_qk_norm_g, m_w_gate_up, m_w_down, m_g_ffn_pre, m_g_ffn_post, v_mem_norm_g, v_w_in, v_w_mem_kv, v_w_o, v_g_mix_pre, v_g_mix_post, v_attn_sink, v_qk_norm_g, v_w_gate_up, v_w_down, v_g_ffn_pre, v_g_ffn_post):
    given = dict(locals())
    depth = w_in.shape[0]
    S, D = x.shape[1], x.shape[2]
    x0 = x[0]
    big_local = [given[n] for n, _ in BIG]
    big_shapes = [t.shape for t in big_local]

    W_in, W_mkv, W_o, W_gu, W_dn = _unpack_gathered(_exchange(_pack_local(big_local, BF16), True, "gather_weights"), big_shapes)

    tabs = _rope_tables(S)
    mem_n = _rms_fwd(mem[0], mem_norm_g[None], BF16, name="rms_mem")

    saved = []
    xc = x0
    for i in range(depth):
        kind = i % N_MIXERS
        (tab, shift) = tabs[1] if kind == 1 else tabs[0]
        sink = attn_sink[i // N_MIXERS] if kind == 0 else None
        qk_gain = _qk_gain_row(qk_norm_g[i // N_MIXERS]) if kind == 1 else None
        h = _rms_fwd(xc, g_mix_pre[i][None], BF16, name=f"rms_pre_{i}")
        proj = _mm(h, W_in[i], F32, name=f"mm_in_{i}")
        pr = _headprep_fwd(proj, tab, shift, qk_gain, name=f"headprep_fwd_{i}")
        tok, mix_saved = _mixer_fwd(kind, pr, sink, i)
        mkv = _mm(mem_n, W_mkv[i], BF16, name=f"mm_mkv_{i}")
        qm = _heads(pr[:, QK_W + KV_W:], N_MEM_HEADS, 1)
        km = _heads(mkv[:, :QM_W], N_MEM_HEADS, 1)[:, 0]
        vm = _heads(mkv[:, QM_W:], N_MEM_HEADS, 1)[:, 0]
        mo, mlse = _attention(qm, km, vm, bq=min(1024, S), tk=km.shape[1], tag=f"m{i}")
        cat = jnp.concatenate([tok, _unheads(mo)], axis=1)
        o = _mm(cat, W_o[i], F32, name=f"mm_o_{i}")
        x1 = _rms_fwd(o, g_mix_post[i][None], F32, res=xc, name=f"rms_post_{i}")
        h2 = _rms_fwd(x1, g_ffn_pre[i][None], BF16, name=f"rms_fpre_{i}")
        gu, act = _mm(h2, W_gu[i], BF16, swiglu=True, name=f"mm_gu_{i}")
        f = _mm(act, W_dn[i], F32, name=f"mm_dn_{i}")
        x2 = _rms_fwd(f, g_ffn_post[i][None], F32, res=x1, name=f"rms_fpost_{i}")
        saved.append(dict(x=xc, h=h, proj=proj, mix=mix_saved, qm=qm, km=km, vm=vm, mo=mo, mlse=mlse, cat=cat, o=o,
                          x1=x1, h2=h2, gu=gu, act=act, f=f))
        xc = x2

    dy, sq = _loss_head(xc, loss_target[0], name="loss_head")
    loss = lax.psum(sq[0, 0] * (0.5 / D), ("x", "y", "c"))

    zero_row = jnp.zeros((1, D), F32)
    grads = {n: [None] * depth for n in ("w_in", "w_mem_kv", "w_o", "w_gate_up", "w_down", "g_mix_pre", "g_mix_post",
                                         "g_ffn_pre", "g_ffn_post")}
    d_sink = [jnp.zeros((N_TOK_HEADS,), F32) for _ in range(attn_sink.shape[0])]
    d_qkg = [jnp.zeros((2, HEAD_DIM), F32) for _ in range(qk_norm_g.shape[0])]
    dmem_n = jnp.zeros((mem.shape[1], D), F32)
    dx = dy
    for i in reversed(range(depth)):
        kind = i % N_MIXERS
        sv = saved[i]
        (tab, shift) = tabs[1] if kind == 1 else tabs[0]
        sink = attn_sink[i // N_MIXERS] if kind == 0 else None
        df, dg = _rms_bwd(sv["f"], g_ffn_post[i][None], dx, BF16, name=f"rmsb_fpost_{i}")
        grads["g_ffn_post"][i] = dg[0]
        da = _mm(df, W_dn[i], BF16, nt=True, name=f"mmb_dn_{i}")
        grads["w_down"][i] = _mm_tn(sv["act"], df, name=f"mmw_dn_{i}")
        dgu = _swiglu_bwd(sv["gu"], da, name=f"swiglu_bwd_{i}")
        dh2 = _mm(dgu, W_gu[i], F32, nt=True, name=f"mmb_gu_{i}")
        grads["w_gate_up"][i] = _mm_tn(sv["h2"], dgu, name=f"mmw_gu_{i}")
        dx1, dg = _rms_bwd(sv["x1"], g_ffn_pre[i][None], dh2, F32, res=dx, name=f"rmsb_fpre_{i}")
        grads["g_ffn_pre"][i] = dg[0]
        do, dg = _rms_bwd(sv["o"], g_mix_post[i][None], dx1, BF16, name=f"rmsb_post_{i}")
        grads["g_mix_post"][i] = dg[0]
        dcat = _mm(do, W_o[i], BF16, nt=True, name=f"mmb_o_{i}")
        grads["w_o"][i] = _mm_tn(sv["cat"], do, name=f"mmw_o_{i}")
        dmo = _heads(dcat[:, Q_W:], N_MEM_HEADS, 1)
        dqm, dkm, dvm, _ = _attention_bwd(sv["qm"], sv["km"], sv["vm"], sv["mo"], sv["mlse"], dmo, bq=min(1024, S),
                                          tk=sv["km"].shape[1], tag=f"m{i}")
        dmkv = jnp.concatenate([_unheads(dkm[:, None]), _unheads(dvm[:, None])], axis=1).astype(BF16)
        grads["w_mem_kv"][i] = _mm_tn(mem_n, dmkv, name=f"mmw_mkv_{i}")
        dmem_n = dmem_n + _mm(dmkv, W_mkv[i], F32, nt=True, name=f"mmb_mkv_{i}")
        dq, dk, dv, dsink = _mixer_bwd(kind, dcat[:, :Q_W], sv["mix"], sink, i)
        if dsink is not None:
            d_sink[i // N_MIXERS] = dsink
        dpr = jnp.concatenate([dq, dk, dv, _unheads(dqm)], axis=1)
        if kind == 1:
            dproj, dgc = _headprep_bwd(dpr, tab, shift, sv["proj"], _qk_gain_row(qk_norm_g[i // N_MIXERS]),
                                       name=f"headprep_bwd_{i}")
            d_qkg[i // N_MIXERS] = jnp.stack([dgc[0, :Q_W].reshape(N_TOK_HEADS, HEAD_DIM).sum(0),
                                              dgc[0, Q_W:QK_W].reshape(N_KV_HEADS, HEAD_DIM).sum(0)])
        else:
            dproj = _headprep_bwd(dpr, tab, shift, name=f"headprep_bwd_{i}")
        dh = _mm(dproj, W_in[i], F32, nt=True, name=f"mmb_in_{i}")
        grads["w_in"][i] = _mm_tn(sv["h"], dproj, name=f"mmw_in_{i}")
        dx, dg = _rms_bwd(sv["x"], g_mix_pre[i][None], dh, F32, res=dx1, name=f"rmsb_pre_{i}")
        grads["g_mix_pre"][i] = dg[0]
    _, dg_mem = _rms_bwd(mem[0], mem_norm_g[None], dmem_n, BF16, name="rmsb_mem")

    full = [jnp.stack(grads[n]) for n, _ in BIG]
    recv = _exchange(_pack_for_scatter(full, big_shapes, BF16), False, "scatter_grads")
    packed = lambda pre: _pack_local([given[pre + n] for n, _ in BIG], F32)
    gb, db, mb, vb = _reduce_adamw(recv, packed(""), packed("m_"), packed("v_"), name="adamw_big")

    small_grads = dict(mem_norm_g=dg_mem[0], g_mix_pre=jnp.stack(grads["g_mix_pre"]), g_mix_post=jnp.stack(grads["g_mix_post"]),
                       attn_sink=jnp.stack(d_sink), qk_norm_g=jnp.stack(d_qkg), g_ffn_pre=jnp.stack(grads["g_ffn_pre"]),
                       g_ffn_post=jnp.stack(grads["g_ffn_post"]))
    sg = _pack_small([small_grads[n] for n in SMALL])
    srecv = _exchange(sg, True, "gather_small_grads")
    spacked = lambda pre: _pack_small([given[pre + n] for n in SMALL])
    gs, ds, ms, vs = _reduce_adamw(srecv, spacked(""), spacked("m_"), spacked("v_"), name="adamw_small")

    out = {}
    for pre, fb, fs in (("grad_", gb, gs), ("delta_", db, ds), ("new_m_", mb, ms), ("new_v_", vb, vs)):
        for (n, _), t in zip(BIG, _unpack_local(fb, big_shapes)):
            out[pre + n] = t
        for n, t in zip(SMALL, _unpack_small(fs, [given[n].shape for n in SMALL])):
            out[pre + n] = t
    order = ("mem_norm_g", "w_in", "w_mem_kv", "w_o", "g_mix_pre", "g_mix_post", "attn_sink", "qk_norm_g", "w_gate_up",
             "w_down", "g_ffn_pre", "g_ffn_post")
    return (loss, dx[None], *[out[p + n] for p in ("grad_", "delta_", "new_m_", "new_v_") for n in order])
```

```python
import functools

import jax
import jax.numpy as jnp
from jax import lax
from jax.experimental import pallas as pl
from jax.experimental.pallas import tpu as pltpu

F32 = jnp.float32
BF16 = jnp.bfloat16

HEAD_DIM = 64
N_TOK_HEADS = 12
N_KV_HEADS = 3
GQA_GROUP = 4
N_MEM_HEADS = 4
Q_W = N_TOK_HEADS * HEAD_DIM
KV_W = N_KV_HEADS * HEAD_DIM
QM_W = N_MEM_HEADS * HEAD_DIM
QK_W = Q_W + KV_W
IN_W = Q_W + 2 * KV_W + QM_W
N_HEAD_SLOTS = IN_W // HEAD_DIM
N_MIXERS = 3
A_RADIUS = 128
C_GROUPS = ((128, 1), (512, 4), (2048, 16))
ROPE_THETA = 500000.0
ROPE_DIMS = HEAD_DIM // 4
AXIAL_THETA = 10000.0
GRID_W = 64
EPS = 1e-6
ATTN_SCALE = HEAD_DIM ** -0.5
NEG = -1e30

ADAM_LR = 0.001
ADAM_B1 = 0.9
ADAM_B2 = 0.999
ADAM_EPS = 1e-08
ADAM_WD = 0.01
ADAM_STEP = 10

N_DEV = 8
LANES = 128
VMEM_LIMIT = 56 * 1024 * 1024
MESH = pl.DeviceIdType.MESH
NT_DIMS = (((1,), (1,)), ((), ()))
TN_DIMS = (((0,), (0,)), ((), ()))


def _pc(body, **kw):
    return pl.pallas_call(body, **kw)


def _cp(*sem):
    return pltpu.CompilerParams(dimension_semantics=sem, vmem_limit_bytes=VMEM_LIMIT)


def _row_tile(m, cap=512):
    t = cap
    while m % t:
        t //= 2
    return t


def _rms_fwd(x, g, out_dtype, res=None, name="rms_fwd"):
    M, D = x.shape
    tm = _row_tile(M)

    def body(*refs):
        if res is None:
            x_ref, g_ref, o_ref = refs
        else:
            x_ref, g_ref, r_ref, o_ref = refs
        xv = x_ref[...]
        y = xv * lax.rsqrt(jnp.mean(xv * xv, axis=-1, keepdims=True) + EPS) * g_ref[...]
        if res is not None:
            y = r_ref[...] + y
        o_ref[...] = y.astype(o_ref.dtype)

    row = pl.BlockSpec((tm, D), lambda i: (i, 0))
    vec = pl.BlockSpec((1, D), lambda i: (0, 0))
    ins = [x, g] + ([] if res is None else [res])
    specs = [row, vec] + ([] if res is None else [row])
    return _pc(body, name=name, out_shape=jax.ShapeDtypeStruct((M, D), out_dtype), grid=(M // tm,),
               in_specs=specs, out_specs=row, compiler_params=_cp("parallel"))(*ins)


def _rms_bwd(x, g, dy, out_dtype, res=None, name="rms_bwd"):
    M, D = x.shape
    tm = _row_tile(M)

    def body(*refs):
        if res is None:
            x_ref, g_ref, dy_ref, dx_ref, dg_ref = refs
        else:
            x_ref, g_ref, dy_ref, r_ref, dx_ref, dg_ref = refs
        xv = x_ref[...]
        r = lax.rsqrt(jnp.mean(xv * xv, axis=-1, keepdims=True) + EPS)
        xh = xv * r
        d = dy_ref[...].astype(F32)
        dxh = d * g_ref[...]
        dx = r * (dxh - xh * jnp.mean(dxh * xh, axis=-1, keepdims=True))
        if res is not None:
            dx = r_ref[...] + dx
        dx_ref[...] = dx.astype(dx_ref.dtype)

        @pl.when(pl.program_id(0) == 0)
        def _():
            dg_ref[...] = jnp.zeros_like(dg_ref)

        dg_ref[...] += jnp.sum(d * xh, axis=0, keepdims=True)

    row = pl.BlockSpec((tm, D), lambda i: (i, 0))
    vec = pl.BlockSpec((1, D), lambda i: (0, 0))
    ins = [x, g, dy] + ([] if res is None else [res])
    specs = [row, vec, row] + ([] if res is None else [row])
    return _pc(body, name=name,
               out_shape=(jax.ShapeDtypeStruct((M, D), out_dtype), jax.ShapeDtypeStruct((1, D), F32)),
               grid=(M // tm,), in_specs=specs, out_specs=(row, vec), compiler_params=_cp("arbitrary"))(*ins)


def _swiglu_bwd(gu, da, name="swiglu_bwd"):
    M, F2 = gu.shape
    F = F2 // 2
    tm = _row_tile(M, 256)

    def body(gu_ref, da_ref, o_ref):
        g = gu_ref[:, :F].astype(F32)
        u = gu_ref[:, F:].astype(F32)
        d = da_ref[...].astype(F32)
        sig = 1.0 / (1.0 + jnp.exp(-g))
        o_ref[:, :F] = (d * u * (sig * (1.0 + g * (1.0 - sig)))).astype(o_ref.dtype)
        o_ref[:, F:] = (d * (g * sig)).astype(o_ref.dtype)

    return _pc(body, name=name, out_shape=jax.ShapeDtypeStruct((M, F2), BF16), grid=(M // tm,),
               in_specs=[pl.BlockSpec((tm, F2), lambda i: (i, 0)), pl.BlockSpec((tm, F), lambda i: (i, 0))],
               out_specs=pl.BlockSpec((tm, F2), lambda i: (i, 0)), compiler_params=_cp("parallel"))(gu, da)


def _loss_head(y, t, name="loss_head"):
    M, D = y.shape
    tm = _row_tile(M)

    def body(y_ref, t_ref, dy_ref, acc_ref):
        e = y_ref[...] - t_ref[...]
        dy_ref[...] = e * (1.0 / D)

        @pl.when(pl.program_id(0) == 0)
        def _():
            acc_ref[...] = jnp.zeros_like(acc_ref)

        acc_ref[...] += jnp.sum(e * e)

    row = pl.BlockSpec((tm, D), lambda i: (i, 0))
    return _pc(body, name=name,
               out_shape=(jax.ShapeDtypeStruct((M, D), F32), jax.ShapeDtypeStruct((8, LANES), F32)),
               grid=(M // tm,), in_specs=[row, row],
               out_specs=(row, pl.BlockSpec((8, LANES), lambda i: (0, 0))), compiler_params=_cp("arbitrary"))(y, t)


def _mm(a, w, out_dtype, nt=False, swiglu=False, tm=256, name="mm"):
    M, K = a.shape
    N = w.shape[0] if nt else w.shape[1]
    tm = _row_tile(M, tm)

    def body(a_ref, w_ref, *o_refs):
        if nt:
            acc = lax.dot_general(a_ref[...], w_ref[...], NT_DIMS, preferred_element_type=F32)
        else:
            acc = jnp.dot(a_ref[...], w_ref[...], preferred_element_type=F32)
        o_refs[0][...] = acc.astype(o_refs[0].dtype)
        if swiglu:
            g = acc[:, : N // 2]
            o_refs[1][...] = (g * (1.0 / (1.0 + jnp.exp(-g))) * acc[:, N // 2:]).astype(o_refs[1].dtype)

    w_spec = pl.BlockSpec(w.shape, lambda i: (0, 0), pipeline_mode=pl.Buffered(1))
    out_shape = jax.ShapeDtypeStruct((M, N), out_dtype)
    out_spec = pl.BlockSpec((tm, N), lambda i: (i, 0))
    if swiglu:
        out_shape = (out_shape, jax.ShapeDtypeStruct((M, N // 2), out_dtype))
        out_spec = (out_spec, pl.BlockSpec((tm, N // 2), lambda i: (i, 0)))
    return _pc(body, name=name, out_shape=out_shape, grid=(M // tm,),
               in_specs=[pl.BlockSpec((tm, K), lambda i: (i, 0)), w_spec], out_specs=out_spec,
               compiler_params=_cp("parallel"))(a, w)


def _mm_tn(a, b, name="mm_tn"):
    S, M = a.shape
    N = b.shape[1]
    tm = M if M <= 1408 else M // 2
    tn = N if N <= 1408 else N // 4
    ts = _row_tile(S, 1024)

    def body(a_ref, b_ref, o_ref):
        @pl.when(pl.program_id(2) == 0)
        def _():
            o_ref[...] = jnp.zeros_like(o_ref)

        o_ref[...] += lax.dot_general(a_ref[...], b_ref[...], TN_DIMS, preferred_element_type=F32)

    return _pc(body, name=name, out_shape=jax.ShapeDtypeStruct((M, N), F32), grid=(M // tm, N // tn, S // ts),
               in_specs=[pl.BlockSpec((ts, tm), lambda i, j, s: (s, i)), pl.BlockSpec((ts, tn), lambda i, j, s: (s, j))],
               out_specs=pl.BlockSpec((tm, tn), lambda i, j, s: (i, j)),
               compiler_params=_cp("parallel", "parallel", "arbitrary"))(a, b)


def _rope_tables(S):
    pos = jnp.arange(S, dtype=jnp.int32)

    def table(p, n_dims, theta):
        inv = theta ** (-(jnp.arange(0, n_dims, 2, dtype=F32) / n_dims))
        ang = p.astype(F32)[:, None] * inv[None, :]
        return jnp.cos(ang), jnp.sin(ang)

    one = lambda n: jnp.ones((S, n), F32)
    zero = lambda n: jnp.zeros((S, n), F32)
    cp, sp = table(pos, ROPE_DIMS, ROPE_THETA)
    rest = HEAD_DIM - ROPE_DIMS
    part = (jnp.concatenate([cp, cp, one(rest)], 1), jnp.concatenate([zero(8), sp, zero(rest)], 1),
            jnp.concatenate([-sp, zero(8), zero(rest)], 1))
    cr, sr = table(pos // GRID_W, HEAD_DIM // 2, AXIAL_THETA)
    cc, sc = table(pos % GRID_W, HEAD_DIM // 2, AXIAL_THETA)
    axial = (jnp.concatenate([cr, cr, cc, cc], 1), jnp.concatenate([zero(16), sr, zero(16), sc], 1),
             jnp.concatenate([-sr, zero(16), -sc, zero(16)], 1))
    rep = LANES // HEAD_DIM
    return (tuple(jnp.tile(t, (1, rep)) for t in part), ROPE_DIMS // 2), (tuple(jnp.tile(t, (1, rep)) for t in axial), HEAD_DIM // 4)


def _seg_mats():
    col = jnp.arange(IN_W)[:, None] // HEAD_DIM
    e = (col == jnp.arange(LANES)[None, :]).astype(F32)
    return e, e.T


def _qk_gain_row(qk_g):
    return jnp.concatenate([jnp.tile(qk_g[0], N_TOK_HEADS), jnp.tile(qk_g[1], N_KV_HEADS),
                            jnp.ones((IN_W - QK_W,), F32)])[None, :]


def _rope_cols(tabs, tm):
    col = lax.broadcasted_iota(jnp.int32, (tm, IN_W), 1)
    qk = col < QK_W
    c, s_lo, s_hi = (jnp.tile(t[...], (1, IN_W // LANES)) for t in tabs)
    return jnp.where(qk, c, 1.0), jnp.where(qk, s_lo, 0.0), jnp.where(qk, s_hi, 0.0), qk


def _seg_mean(v, e_ref, et_ref):
    s = jnp.dot(v, e_ref[...], precision=lax.Precision.HIGHEST, preferred_element_type=F32) * (1.0 / HEAD_DIM)
    return jnp.dot(s, et_ref[...], precision=lax.Precision.HIGHEST, preferred_element_type=F32)


def _headprep_fwd(proj, tabs, shift, qk_gain=None, name="headprep_fwd"):
    S = proj.shape[0]
    tm = _row_tile(S, 256)
    norm = qk_gain is not None

    def body(*refs):
        if norm:
            p_ref, c_ref, lo_ref, hi_ref, g_ref, e_ref, et_ref, o_ref = refs
        else:
            p_ref, c_ref, lo_ref, hi_ref, o_ref = refs
        x = p_ref[...]
        c, s_lo, s_hi, qk = _rope_cols((c_ref, lo_ref, hi_ref), tm)
        if norm:
            r = lax.rsqrt(_seg_mean(x * x, e_ref, et_ref) + EPS)
            x = x * jnp.where(qk, r, 1.0) * g_ref[...]
        y = x * c + pltpu.roll(x, shift, 1) * s_lo + pltpu.roll(x, IN_W - shift, 1) * s_hi
        o_ref[...] = y.astype(o_ref.dtype)

    row = pl.BlockSpec((tm, IN_W), lambda i: (i, 0))
    tab = pl.BlockSpec((tm, LANES), lambda i: (i, 0))
    ins = [proj, *tabs]
    specs = [row, tab, tab, tab]
    if norm:
        e, et = _seg_mats()
        ins += [qk_gain, e, et]
        specs += [pl.BlockSpec((1, IN_W), lambda i: (0, 0)), pl.BlockSpec((IN_W, LANES), lambda i: (0, 0)),
                  pl.BlockSpec((LANES, IN_W), lambda i: (0, 0))]
    return _pc(body, name=name, out_shape=jax.ShapeDtypeStruct((S, IN_W), BF16), grid=(S // tm,),
               in_specs=specs, out_specs=row, compiler_params=_cp("parallel"))(*ins)


def _headprep_bwd(dpr, tabs, shift, proj=None, qk_gain=None, name="headprep_bwd"):
    S = dpr.shape[0]
    tm = _row_tile(S, 256)
    norm = qk_gain is not None

    def body(*refs):
        if norm:
            d_ref, c_ref, lo_ref, hi_ref, p_ref, g_ref, e_ref, et_ref, o_ref, dg_ref = refs
        else:
            d_ref, c_ref, lo_ref, hi_ref, o_ref = refs
        d = d_ref[...].astype(F32)
        c, s_lo, s_hi, qk = _rope_cols((c_ref, lo_ref, hi_ref), tm)
        dx = d * c + pltpu.roll(d * s_lo, IN_W - shift, 1) + pltpu.roll(d * s_hi, shift, 1)
        if norm:
            x = p_ref[...]
            r = lax.rsqrt(_seg_mean(x * x, e_ref, et_ref) + EPS)
            xh = x * r

            @pl.when(pl.program_id(0) == 0)
            def _():
                dg_ref[...] = jnp.zeros_like(dg_ref)

            dg_ref[...] += jnp.sum(jnp.where(qk, dx * xh, 0.0), axis=0, keepdims=True)
            dxh = dx * g_ref[...]
            dn = r * (dxh - xh * _seg_mean(dxh * xh, e_ref, et_ref))
            dx = jnp.where(qk, dn, dx)
        o_ref[...] = dx.astype(o_ref.dtype)

    row = pl.BlockSpec((tm, IN_W), lambda i: (i, 0))
    tab = pl.BlockSpec((tm, LANES), lambda i: (i, 0))
    vec = pl.BlockSpec((1, IN_W), lambda i: (0, 0))
    ins = [dpr, *tabs]
    specs = [row, tab, tab, tab]
    out_shape = jax.ShapeDtypeStruct((S, IN_W), BF16)
    out_specs = row
    if norm:
        e, et = _seg_mats()
        ins += [proj, qk_gain, e, et]
        specs += [row, vec, pl.BlockSpec((IN_W, LANES), lambda i: (0, 0)), pl.BlockSpec((LANES, IN_W), lambda i: (0, 0))]
        out_shape = (out_shape, jax.ShapeDtypeStruct((1, IN_W), F32))
        out_specs = (row, vec)
    return _pc(body, name=name, out_shape=out_shape, grid=(S // tm,), in_specs=specs, out_specs=out_specs,
               compiler_params=_cp("arbitrary" if norm else "parallel"))(*ins)


def _band_mask(qb, kb, n_kb_valid, bq, tk, rows, radius):
    qpos = qb * bq + lax.rem(lax.broadcasted_iota(jnp.int32, (rows, 1), 0), bq)
    kpos = kb * tk + lax.broadcasted_iota(jnp.int32, (1, tk), 1)
    return (jnp.abs(qpos - kpos) <= radius) & n_kb_valid


def _flash_fwd(q, k, v, *, radius=None, sink=None, bq, tk, name="flash_fwd"):
    NB, G, L, HD = q.shape
    Lk = k.shape[1]
    nq, nk = L // bq, Lk // tk
    banded = radius is not None
    assert not banded or (bq == tk and radius <= tk and L == Lk)
    nst = 3 if banded else nk
    rows = G * bq

    def kmap(b, n, j):
        return (b, jnp.clip(n - 1 + j, 0, nk - 1), 0) if banded else (b, j, 0)

    def body(*refs):
        if sink is not None:
            sink_ref, *refs = refs
        q_ref, k_ref, v_ref, o_ref, lse_ref, m_sc, acc_sc = refs
        b, n, j = pl.program_id(0), pl.program_id(1), pl.program_id(2)

        @pl.when(j == 0)
        def _():
            if sink is not None:
                m_sc[...] = _sink_rows(sink_ref, b, G, bq, rows)
                acc_sc[...] = (lax.broadcasted_iota(jnp.int32, (rows, 2 * HD), 1) >= HD).astype(F32)
            else:
                m_sc[...] = jnp.full_like(m_sc, NEG)
                acc_sc[...] = jnp.zeros_like(acc_sc)

        qv = q_ref[0].reshape(rows, HD) * ATTN_SCALE
        s = lax.dot_general(qv, k_ref[0], NT_DIMS, preferred_element_type=F32)
        if banded:
            kb = n - 1 + j
            mask = _band_mask(n, kb, (kb >= 0) & (kb < nk), bq, tk, rows, radius)
            s = jnp.where(mask, s, NEG)
        m_prev = m_sc[...]
        m_new = jnp.maximum(m_prev, jnp.max(s, axis=1, keepdims=True))
        p = jnp.exp(s - m_new)
        if banded:
            p = jnp.where(mask, p, 0.0)
        v_ones = jnp.concatenate([v_ref[0], jnp.ones((tk, HD), BF16)], axis=1)
        acc_sc[...] = jnp.exp(m_prev - m_new) * acc_sc[...] + jnp.dot(p.astype(BF16), v_ones, preferred_element_type=F32)
        m_sc[...] = m_new

        @pl.when(j == nst - 1)
        def _():
            acc = acc_sc[...]
            l = acc[:, HD:HD + 1]
            o_ref[0] = (acc[:, :HD] / l).reshape(G, bq, HD).astype(o_ref.dtype)
            lse_ref[0] = (m_sc[...] + jnp.log(l)).reshape(G, bq, 1)

    qspec = pl.BlockSpec((1, G, bq, HD), lambda b, n, j: (b, 0, n, 0))
    kspec = pl.BlockSpec((1, tk, HD), kmap)
    ins, specs = [q, k, v], [qspec, kspec, kspec]
    if sink is not None:
        ins, specs = [sink] + ins, [pl.BlockSpec(memory_space=pltpu.SMEM)] + specs
    return _pc(body, name=name,
               out_shape=(jax.ShapeDtypeStruct((NB, G, L, HD), BF16), jax.ShapeDtypeStruct((NB, G, L, 1), F32)),
               grid=(NB, nq, nst), in_specs=specs,
               out_specs=(qspec, pl.BlockSpec((1, G, bq, 1), lambda b, n, j: (b, 0, n, 0))),
               scratch_shapes=[pltpu.VMEM((rows, 1), F32), pltpu.VMEM((rows, 2 * HD), F32)],
               compiler_params=_cp("parallel", "parallel", "arbitrary"))(*ins)


def _attn_delta(do, o, *, dlse=None, lse=None, sink=None, name="attn_delta"):
    NB, G, L, HD = do.shape
    bl = _row_tile(L, 1024)

    def body(*refs):
        refs = list(refs)
        sink_ref = refs.pop(0) if sink is not None else None
        do_ref, o_ref = refs.pop(0), refs.pop(0)
        dlse_ref = refs.pop(0) if dlse is not None else None
        lse_ref = refs.pop(0) if sink is not None else None
        delta_ref = refs.pop(0)
        b = pl.program_id(0)
        delta = jnp.sum(do_ref[0].astype(F32) * o_ref[0].astype(F32), axis=-1, keepdims=True)
        if dlse is not None:
            delta = delta - dlse_ref[0]
        delta_ref[0] = delta
        if sink is not None:
            ds_ref = refs.pop(0)

            @pl.when(pl.program_id(1) == 0)
            def _():
                ds_ref[...] = jnp.zeros_like(ds_ref)

            for g in range(G):
                ps = jnp.exp(sink_ref[b * G + g] - lse_ref[0, g]) * delta[g]
                ds_ref[0, g] -= jnp.sum(ps)

    blk = pl.BlockSpec((1, G, bl, HD), lambda b, n: (b, 0, n, 0))
    col = pl.BlockSpec((1, G, bl, 1), lambda b, n: (b, 0, n, 0))
    ins, specs = [do, o], [blk, blk]
    if dlse is not None:
        ins, specs = ins + [dlse], specs + [col]
    out_shape = jax.ShapeDtypeStruct((NB, G, L, 1), F32)
    out_specs = col
    if sink is not None:
        ins, specs = [sink] + ins + [lse], [pl.BlockSpec(memory_space=pltpu.SMEM)] + specs + [col]
        out_shape = (out_shape, jax.ShapeDtypeStruct((NB, G, 1, LANES), F32))
        out_specs = (col, pl.BlockSpec((1, G, 1, LANES), lambda b, n: (b, 0, 0, 0)))
    return _pc(body, name=name, out_shape=out_shape, grid=(NB, L // bl), in_specs=specs, out_specs=out_specs,
               compiler_params=_cp("parallel", "arbitrary"))(*ins)


def _flash_dq(q, k, v, do, lse, delta, *, radius=None, bq, tk, name="flash_dq"):
    NB, G, L, HD = q.shape
    Lk = k.shape[1]
    nq, nk = L // bq, Lk // tk
    banded = radius is not None
    nst = 3 if banded else nk
    rows = G * bq

    def kmap(b, n, j):
        return (b, jnp.clip(n - 1 + j, 0, nk - 1), 0) if banded else (b, j, 0)

    def body(q_ref, k_ref, v_ref, do_ref, lse_ref, dl_ref, dq_ref, acc_sc):
        n, j = pl.program_id(1), pl.program_id(2)

        @pl.when(j == 0)
        def _():
            acc_sc[...] = jnp.zeros_like(acc_sc)

        qv = q_ref[0].reshape(rows, HD)
        dov = do_ref[0].reshape(rows, HD)
        s = lax.dot_general(qv, k_ref[0], NT_DIMS, preferred_element_type=F32) * ATTN_SCALE
        p = jnp.exp(s - lse_ref[0].reshape(rows, 1))
        if banded:
            kb = n - 1 + j
            p = jnp.where(_band_mask(n, kb, (kb >= 0) & (kb < nk), bq, tk, rows, radius), p, 0.0)
        dp = lax.dot_general(dov, v_ref[0], NT_DIMS, preferred_element_type=F32)
        ds = p * (dp - dl_ref[0].reshape(rows, 1))
        acc_sc[...] += jnp.dot(ds.astype(BF16), k_ref[0], preferred_element_type=F32)

        @pl.when(j == nst - 1)
        def _():
            dq_ref[0] = (acc_sc[...] * ATTN_SCALE).reshape(G, bq, HD).astype(dq_ref.dtype)

    qspec = pl.BlockSpec((1, G, bq, HD), lambda b, n, j: (b, 0, n, 0))
    cspec = pl.BlockSpec((1, G, bq, 1), lambda b, n, j: (b, 0, n, 0))
    kspec = pl.BlockSpec((1, tk, HD), kmap)
    return _pc(body, name=name, out_shape=jax.ShapeDtypeStruct((NB, G, L, HD), F32), grid=(NB, nq, nst),
               in_specs=[qspec, kspec, kspec, qspec, cspec, cspec], out_specs=qspec,
               scratch_shapes=[pltpu.VMEM((rows, HD), F32)],
               compiler_params=_cp("parallel", "parallel", "arbitrary"))(q, k, v, do, lse, delta)


def _flash_dkv(q, k, v, do, lse, delta, *, radius=None, bq, tk, name="flash_dkv"):
    NB, G, L, HD = q.shape
    Lk = k.shape[1]
    nq, nk = L // bq, Lk // tk
    banded = radius is not None
    nst = 3 if banded else nq
    rows = G * bq

    def qmap(b, m, j):
        return (b, 0, jnp.clip(m - 1 + j, 0, nq - 1), 0) if banded else (b, 0, j, 0)

    def body(q_ref, k_ref, v_ref, do_ref, lse_ref, dl_ref, dk_ref, dv_ref, dk_sc, dv_sc):
        m, j = pl.program_id(1), pl.program_id(2)

        @pl.when(j == 0)
        def _():
            dk_sc[...] = jnp.zeros_like(dk_sc)
            dv_sc[...] = jnp.zeros_like(dv_sc)

        qv = q_ref[0].reshape(rows, HD)
        dov = do_ref[0].reshape(rows, HD)
        s = lax.dot_general(qv, k_ref[0], NT_DIMS, preferred_element_type=F32) * ATTN_SCALE
        p = jnp.exp(s - lse_ref[0].reshape(rows, 1))
        if banded:
            qb = m - 1 + j
            p = jnp.where(_band_mask(qb, m, (qb >= 0) & (qb < nq), bq, tk, rows, radius), p, 0.0)
        dv_sc[...] += lax.dot_general(p.astype(BF16), dov, TN_DIMS, preferred_element_type=F32)
        dp = lax.dot_general(dov, v_ref[0], NT_DIMS, preferred_element_type=F32)
        ds = p * (dp - dl_ref[0].reshape(rows, 1))
        dk_sc[...] += lax.dot_general(ds.astype(BF16), qv, TN_DIMS, preferred_element_type=F32)

        @pl.when(j == nst - 1)
        def _():
            dk_ref[0] = (dk_sc[...] * ATTN_SCALE).astype(dk_ref.dtype)
            dv_ref[0] = dv_sc[...].astype(dv_ref.dtype)

    qspec = pl.BlockSpec((1, G, bq, HD), qmap)
    cspec = pl.BlockSpec((1, G, bq, 1), qmap)
    kspec = pl.BlockSpec((1, tk, HD), lambda b, m, j: (b, m, 0))
    kv_shape = jax.ShapeDtypeStruct((NB, Lk, HD), F32)
    return _pc(body, name=name, out_shape=(kv_shape, kv_shape), grid=(NB, nk, nst),
               in_specs=[qspec, kspec, kspec, qspec, cspec, cspec], out_specs=(kspec, kspec),
               scratch_shapes=[pltpu.VMEM((tk, HD), F32), pltpu.VMEM((tk, HD), F32)],
               compiler_params=_cp("parallel", "parallel", "arbitrary"))(q, k, v, do, lse, delta)


def _flash_bwd_full(q, k, v, do, lse, delta, *, bq, tk, name="flash_bwd"):
    NB, G, L, HD = q.shape
    Lk = k.shape[1]
    nq, nk = L // bq, Lk // tk
    rows = G * bq

    def body(q_ref, k_ref, v_ref, do_ref, lse_ref, dl_ref, dqp_ref, dk_ref, dv_ref, dk_sc, dv_sc):
        n = pl.program_id(2)

        @pl.when(n == 0)
        def _():
            dk_sc[...] = jnp.zeros_like(dk_sc)
            dv_sc[...] = jnp.zeros_like(dv_sc)

        qs = q_ref[0].reshape(rows, HD) * ATTN_SCALE
        dov = do_ref[0].reshape(rows, HD)
        s = lax.dot_general(qs, k_ref[0], NT_DIMS, preferred_element_type=F32)
        p = jnp.exp(s - lse_ref[0].reshape(rows, 1))
        dp = lax.dot_general(dov, v_ref[0], NT_DIMS, preferred_element_type=F32)
        ds = (p * (dp - dl_ref[0].reshape(rows, 1))).astype(BF16)
        dv_sc[...] += lax.dot_general(p.astype(BF16), dov, TN_DIMS, preferred_element_type=F32)
        dk_sc[...] += lax.dot_general(ds, qs, TN_DIMS, preferred_element_type=F32)
        dqp_ref[0, 0] = (jnp.dot(ds, k_ref[0], preferred_element_type=F32) * ATTN_SCALE).reshape(G, bq, HD)

        @pl.when(n == nq - 1)
        def _():
            dk_ref[0] = dk_sc[...]
            dv_ref[0] = dv_sc[...]

    qspec = pl.BlockSpec((1, G, bq, HD), lambda b, m, n: (b, 0, n, 0))
    cspec = pl.BlockSpec((1, G, bq, 1), lambda b, m, n: (b, 0, n, 0))
    kspec = pl.BlockSpec((1, tk, HD), lambda b, m, n: (b, m, 0))
    kv_shape = jax.ShapeDtypeStruct((NB, Lk, HD), F32)
    dqp, dk, dv = _pc(body, name=name,
                      out_shape=(jax.ShapeDtypeStruct((nk, NB, G, L, HD), F32), kv_shape, kv_shape), grid=(NB, nk, nq),
                      in_specs=[qspec, kspec, kspec, qspec, cspec, cspec],
                      out_specs=(pl.BlockSpec((1, 1, G, bq, HD), lambda b, m, n: (m, b, 0, n, 0)), kspec, kspec),
                      scratch_shapes=[pltpu.VMEM((tk, HD), F32), pltpu.VMEM((tk, HD), F32)],
                      compiler_params=_cp("parallel", "parallel", "arbitrary"))(q, k, v, do, lse, delta)
    bl = _row_tile(L, 512)

    def sum_body(p_ref, o_ref):
        acc = p_ref[0, 0]
        for j in range(1, nk):
            acc = acc + p_ref[j, 0]
        o_ref[0] = acc

    dq = _pc(sum_body, name=name + "_sum", out_shape=jax.ShapeDtypeStruct((NB, G, L, HD), F32), grid=(NB, L // bl),
             in_specs=[pl.BlockSpec((nk, 1, G, bl, HD), lambda b, n: (0, b, 0, n, 0))],
             out_specs=pl.BlockSpec((1, G, bl, HD), lambda b, n: (b, 0, n, 0)),
             compiler_params=_cp("parallel", "parallel"))(dqp)
    return dq, dk, dv


def _window(L, blk, radius):
    if radius is None:
        return L, None
    W = min(L, blk + 2 * radius)
    assert blk % radius == 0 and (L - W) % radius == 0
    return W, lambda n: radius * jnp.clip(n * (blk // radius) - 1, 0, (L - W) // radius)


def _win_specs(G, W, HD, start, with_g):
    E = pl.Element
    st = (lambda n: 0) if start is None else start
    if with_g:
        return pl.BlockSpec((E(1), E(G), E(W), E(HD)), lambda b, n: (b, 0, st(n), 0))
    return pl.BlockSpec((E(1), E(W), E(HD)), lambda b, n: (b, st(n), 0))


def _sink_rows(sink_ref, b, G, per, rows):
    head = lax.div(lax.broadcasted_iota(jnp.int32, (rows, 1), 0), per)
    sk = jnp.zeros((rows, 1), F32)
    for g in range(G):
        sk = jnp.where(head == g, sink_ref[b * G + g], sk)
    return sk


def _win_fwd(q, k, v, *, radius=None, sink=None, bq, name="win_fwd"):
    NB, G, L, HD = q.shape
    Lk = k.shape[1]
    W, start = _window(Lk, bq, radius)
    rows = G * bq

    def body(*refs):
        if sink is not None:
            sink_ref, *refs = refs
        q_ref, k_ref, v_ref, o_ref, lse_ref = refs
        b, n = pl.program_id(0), pl.program_id(1)
        qv = q_ref[0].reshape(rows, HD) * ATTN_SCALE
        s = lax.dot_general(qv, k_ref[0], NT_DIMS, preferred_element_type=F32)
        if radius is not None:
            qpos = n * bq + lax.rem(lax.broadcasted_iota(jnp.int32, (rows, 1), 0), bq)
            kpos = start(n) + lax.broadcasted_iota(jnp.int32, (1, W), 1)
            s = jnp.where(jnp.abs(qpos - kpos) <= radius, s, NEG)
        m = jnp.max(s, axis=1, keepdims=True)
        if sink is not None:
            sk = _sink_rows(sink_ref, b, G, bq, rows)
            m = jnp.maximum(m, sk)
        p = jnp.exp(s - m)
        l = jnp.sum(p, axis=1, keepdims=True)
        if sink is not None:
            l = l + jnp.exp(sk - m)
        o = jnp.dot(p.astype(BF16), v_ref[0], preferred_element_type=F32) / l
        o_ref[0] = o.reshape(G, bq, HD).astype(o_ref.dtype)
        lse_ref[0] = (m + jnp.log(l)).reshape(G, bq, 1)

    qspec = pl.BlockSpec((1, G, bq, HD), lambda b, n: (b, 0, n, 0))
    kspec = _win_specs(G, W, HD, start, False)
    ins, specs = [q, k, v], [qspec, kspec, kspec]
    if sink is not None:
        ins, specs = [sink] + ins, [pl.BlockSpec(memory_space=pltpu.SMEM)] + specs
    return _pc(body, name=name,
               out_shape=(jax.ShapeDtypeStruct((NB, G, L, HD), BF16), jax.ShapeDtypeStruct((NB, G, L, 1), F32)),
               grid=(NB, L // bq), in_specs=specs,
               out_specs=(qspec, pl.BlockSpec((1, G, bq, 1), lambda b, n: (b, 0, n, 0))),
               compiler_params=_cp("parallel", "parallel"))(*ins)


def _win_dq(q, k, v, do, lse, delta, *, radius=None, bq, name="win_dq"):
    NB, G, L, HD = q.shape
    Lk = k.shape[1]
    W, start = _window(Lk, bq, radius)
    rows = G * bq

    def body(q_ref, k_ref, v_ref, do_ref, lse_ref, dl_ref, dq_ref):
        n = pl.program_id(1)
        qv = q_ref[0].reshape(rows, HD) * ATTN_SCALE
        dov = do_ref[0].reshape(rows, HD)
        s = lax.dot_general(qv, k_ref[0], NT_DIMS, preferred_element_type=F32)
        p = jnp.exp(s - lse_ref[0].reshape(rows, 1))
        if radius is not None:
            qpos = n * bq + lax.rem(lax.broadcasted_iota(jnp.int32, (rows, 1), 0), bq)
            kpos = start(n) + lax.broadcasted_iota(jnp.int32, (1, W), 1)
            p = jnp.where(jnp.abs(qpos - kpos) <= radius, p, 0.0)
        dp = lax.dot_general(dov, v_ref[0], NT_DIMS, preferred_element_type=F32)
        ds = p * (dp - dl_ref[0].reshape(rows, 1))
        dq = jnp.dot(ds.astype(BF16), k_ref[0], preferred_element_type=F32) * ATTN_SCALE
        dq_ref[0] = dq.reshape(G, bq, HD)

    qspec = pl.BlockSpec((1, G, bq, HD), lambda b, n: (b, 0, n, 0))
    cspec = pl.BlockSpec((1, G, bq, 1), lambda b, n: (b, 0, n, 0))
    kspec = _win_specs(G, W, HD, start, False)
    return _pc(body, name=name, out_shape=jax.ShapeDtypeStruct((NB, G, L, HD), F32), grid=(NB, L // bq),
               in_specs=[qspec, kspec, kspec, qspec, cspec, cspec], out_specs=qspec,
               compiler_params=_cp("parallel", "parallel"))(q, k, v, do, lse, delta)


def _win_dkv(q, k, v, do, lse, delta, *, radius, bk, name="win_dkv"):
    NB, G, L, HD = q.shape
    Lk = k.shape[1]
    W, start = _window(L, bk, radius)
    rows = G * W

    def body(q_ref, k_ref, v_ref, do_ref, lse_ref, dl_ref, dk_ref, dv_ref):
        m = pl.program_id(1)
        qv = q_ref[0].reshape(rows, HD)
        dov = do_ref[0].reshape(rows, HD)
        s = lax.dot_general(qv * ATTN_SCALE, k_ref[0], NT_DIMS, preferred_element_type=F32)
        p = jnp.exp(s - lse_ref[0].reshape(rows, 1))
        qpos = start(m) + lax.rem(lax.broadcasted_iota(jnp.int32, (rows, 1), 0), W)
        kpos = m * bk + lax.broadcasted_iota(jnp.int32, (1, bk), 1)
        p = jnp.where(jnp.abs(qpos - kpos) <= radius, p, 0.0)
        dv_ref[0] = lax.dot_general(p.astype(BF16), dov, TN_DIMS, preferred_element_type=F32)
        dp = lax.dot_general(dov, v_ref[0], NT_DIMS, preferred_element_type=F32)
        ds = p * (dp - dl_ref[0].reshape(rows, 1))
        dk_ref[0] = lax.dot_general(ds.astype(BF16), qv, TN_DIMS, preferred_element_type=F32) * ATTN_SCALE

    qspec = _win_specs(G, W, HD, start, True)
    cspec = _win_specs(G, W, 1, start, True)
    kspec = pl.BlockSpec((1, bk, HD), lambda b, m: (b, m, 0))
    kv_shape = jax.ShapeDtypeStruct((NB, Lk, HD), F32)
    return _pc(body, name=name, out_shape=(kv_shape, kv_shape), grid=(NB, Lk // bk),
               in_specs=[qspec, kspec, kspec, qspec, cspec, cspec], out_specs=(kspec, kspec),
               compiler_params=_cp("parallel", "parallel"))(q, k, v, do, lse, delta)


def _attention(q, k, v, *, radius=None, sink=None, bq, tk=None, tag):
    if tk is None:
        return _win_fwd(q, k, v, radius=radius, sink=sink, bq=bq, name=f"win_fwd_{tag}")
    return _flash_fwd(q, k, v, radius=radius, sink=sink, bq=bq, tk=tk, name=f"flash_fwd_{tag}")


def _attention_bwd(q, k, v, o, lse, do, *, radius=None, sink=None, dlse=None, bq, tk=None, tag):
    if sink is not None:
        delta, ds = _attn_delta(do, o, lse=lse, sink=sink, name=f"attn_delta_{tag}")
        dsink = ds[:, :, 0, 0].reshape(-1)
    else:
        delta, dsink = _attn_delta(do, o, dlse=dlse, name=f"attn_delta_{tag}"), None
    if tk is not None and radius is None:
        return (*_flash_bwd_full(q, k, v, do, lse, delta, bq=bq, tk=tk, name=f"flash_bwd_{tag}"), dsink)
    if tk is None:
        dq = _win_dq(q, k, v, do, lse, delta, radius=radius, bq=bq, name=f"win_dq_{tag}")
    else:
        dq = _flash_dq(q, k, v, do, lse, delta, radius=radius, bq=bq, tk=tk, name=f"flash_dq_{tag}")
    if tk is None and radius is not None:
        dk, dv = _win_dkv(q, k, v, do, lse, delta, radius=radius, bk=bq, name=f"win_dkv_{tag}")
    else:
        dk, dv = _flash_dkv(q, k, v, do, lse, delta, radius=radius, bq=bq, tk=tk or k.shape[1], name=f"flash_dkv_{tag}")
    return dq, dk, dv, dsink


def _combine_fwd(o, lse, name="combine_fwd"):
    H, S, HD = o.shape
    tm = _row_tile(S, 512)

    def body(o_ref, lse_ref, t_ref):
        for g in range(GQA_GROUP):
            hs = [kv * GQA_GROUP + g for kv in range(N_KV_HEADS)]
            ls = [lse_ref[h] for h in hs]
            mx = functools.reduce(jnp.maximum, ls)
            es = [jnp.exp(l - mx) for l in ls]
            den = functools.reduce(jnp.add, es)
            for h, e in zip(hs, es):
                t_ref[h] = (o_ref[h].astype(F32) * (e / den)).astype(t_ref.dtype)

    blk = pl.BlockSpec((H, tm, HD), lambda i: (0, i, 0))
    col = pl.BlockSpec((H, tm, 1), lambda i: (0, i, 0))
    return _pc(body, name=name, out_shape=jax.ShapeDtypeStruct((H, S, HD), BF16), grid=(S // tm,),
               in_specs=[blk, col], out_specs=blk, compiler_params=_cp("parallel"))(o, lse)


def _combine_bwd(dt, o, lse, name="combine_bwd"):
    H, S, HD = o.shape
    tm = _row_tile(S, 512)

    def body(dt_ref, o_ref, lse_ref, do_ref, dlse_ref):
        for g in range(GQA_GROUP):
            hs = [kv * GQA_GROUP + g for kv in range(N_KV_HEADS)]
            ls = [lse_ref[h] for h in hs]
            mx = functools.reduce(jnp.maximum, ls)
            es = [jnp.exp(l - mx) for l in ls]
            den = functools.reduce(jnp.add, es)
            al = [e / den for e in es]
            dts = [dt_ref[h].astype(F32) for h in hs]
            da = [jnp.sum(d * o_ref[h].astype(F32), axis=-1, keepdims=True) for h, d in zip(hs, dts)]
            dot = functools.reduce(jnp.add, [a * d for a, d in zip(al, da)])
            for h, a, d, dd in zip(hs, al, da, dts):
                do_ref[h] = (dd * a).astype(do_ref.dtype)
                dlse_ref[h] = a * (d - dot)

    blk = pl.BlockSpec((H, tm, HD), lambda i: (0, i, 0))
    col = pl.BlockSpec((H, tm, 1), lambda i: (0, i, 0))
    return _pc(body, name=name,
               out_shape=(jax.ShapeDtypeStruct((H, S, HD), BF16), jax.ShapeDtypeStruct((H, S, 1), F32)),
               grid=(S // tm,), in_specs=[blk, blk, col], out_specs=(blk, col), compiler_params=_cp("parallel"))(dt, o, lse)


def _position():
    x, y, c = lax.axis_index("x"), lax.axis_index("y"), lax.axis_index("c")
    return x, y, c


def _peer(pos, k):
    x, y, c = pos
    return (1 - x if k & 4 else x, 1 - y if k & 2 else y, 1 - c if k & 1 else c)


def _linear(p):
    return 4 * p[0] + 2 * p[1] + p[2]


def _exchange(buf, gather, name):
    out_shape = ((N_DEV,) + buf.shape) if gather else buf.shape

    def body(s_ref, r_ref, send_sems, recv_sems, local_sem):
        pos = _position()
        me = _linear(pos)
        own = pltpu.make_async_copy(s_ref if gather else s_ref.at[me], r_ref.at[me], local_sem)
        own.start()
        sends = []
        for k in range(1, N_DEV):
            peer = _peer(pos, k)
            src = s_ref if gather else s_ref.at[_linear(peer)]
            cp = pltpu.make_async_remote_copy(src_ref=src, dst_ref=r_ref.at[me], send_sem=send_sems.at[k - 1],
                                              recv_sem=recv_sems.at[k - 1], device_id=peer, device_id_type=MESH)
            cp.start()
            sends.append(cp)
        for k in range(1, N_DEV):
            peer = _peer(pos, k)
            slot = r_ref.at[_linear(peer)]
            pltpu.make_async_remote_copy(src_ref=slot, dst_ref=slot, send_sem=send_sems.at[k - 1],
                                         recv_sem=recv_sems.at[k - 1], device_id=peer, device_id_type=MESH).wait_recv()
        for cp in sends:
            cp.wait_send()
        own.wait()

    hbm = pl.BlockSpec(memory_space=pltpu.HBM)
    return _pc(body, name=name, out_shape=jax.ShapeDtypeStruct(out_shape, buf.dtype), in_specs=[hbm], out_specs=hbm,
               scratch_shapes=[pltpu.SemaphoreType.DMA((N_DEV - 1,)), pltpu.SemaphoreType.DMA((N_DEV - 1,)),
                               pltpu.SemaphoreType.DMA])(buf)


def _reduce_adamw(recv, w, m, v, name):
    _, R, C = recv.shape
    tr = _row_tile(R, 512)

    def body(r_ref, w_ref, m_ref, v_ref, g_ref, d_ref, nm_ref, nv_ref):
        g = r_ref[0].astype(F32)
        for j in range(1, N_DEV):
            g = g + r_ref[j].astype(F32)
        g_ref[...] = g
        nm = ADAM_B1 * m_ref[...] + (1.0 - ADAM_B1) * g
        nv = ADAM_B2 * v_ref[...] + (1.0 - ADAM_B2) * jnp.square(g)
        m_hat = nm / (1.0 - ADAM_B1 ** ADAM_STEP)
        v_hat = nv / (1.0 - ADAM_B2 ** ADAM_STEP)
        d_ref[...] = -ADAM_LR * (m_hat / (jnp.sqrt(v_hat) + ADAM_EPS) + ADAM_WD * w_ref[...])
        nm_ref[...] = nm
        nv_ref[...] = nv

    row = pl.BlockSpec((tr, C), lambda i: (i, 0))
    out = jax.ShapeDtypeStruct((R, C), F32)
    return _pc(body, name=name, out_shape=(out, out, out, out), grid=(R // tr,),
               in_specs=[pl.BlockSpec((N_DEV, tr, C), lambda i: (0, i, 0)), row, row, row],
               out_specs=(row, row, row, row), compiler_params=_cp("parallel"))(recv, w, m, v)


BIG = (("w_in", 2), ("w_mem_kv", 1), ("w_o", 1), ("w_gate_up", 2), ("w_down", 1))
SMALL = ("mem_norm_g", "g_mix_pre", "g_mix_post", "attn_sink", "qk_norm_g", "g_ffn_pre", "g_ffn_post")
SMALL_W = 1024


def _pack_local(shards, dtype):
    return jnp.concatenate([s.astype(dtype).reshape(-1, LANES) for s in shards], axis=0)


def _unpack_local(flat, shapes):
    out, r = [], 0
    for shp in shapes:
        n = shp[0] * shp[1] * shp[2] // LANES
        out.append(flat[r:r + n].reshape(shp))
        r += n
    return out


def _unpack_gathered(g, shapes):
    out, r = [], 0
    for (name, dim), shp in zip(BIG, shapes):
        n = shp[0] * shp[1] * shp[2] // LANES
        t = g[:, r:r + n].reshape((N_DEV,) + tuple(shp))
        if dim == 2:
            t = t.transpose(1, 2, 0, 3).reshape(shp[0], shp[1], N_DEV * shp[2])
        else:
            t = t.transpose(1, 0, 2, 3).reshape(shp[0], N_DEV * shp[1], shp[2])
        out.append(t)
        r += n
    return out


def _pack_for_scatter(full, shapes, dtype):
    parts = []
    for (name, dim), shp, t in zip(BIG, shapes, full):
        if dim == 2:
            t = t.reshape(shp[0], shp[1], N_DEV, shp[2]).transpose(2, 0, 1, 3)
        else:
            t = t.reshape(shp[0], N_DEV, shp[1], shp[2]).transpose(1, 0, 2, 3)
        parts.append(t.astype(dtype).reshape(N_DEV, -1, LANES))
    return jnp.concatenate(parts, axis=1)


def _pack_small(arrs):
    flat = jnp.concatenate([a.reshape(-1) for a in arrs])
    pad = (-flat.shape[0]) % (8 * SMALL_W)
    return jnp.pad(flat, (0, pad)).reshape(-1, SMALL_W)


def _unpack_small(flat, shapes):
    flat = flat.reshape(-1)
    out, r = [], 0
    for shp in shapes:
        n = 1
        for d in shp:
            n *= d
        out.append(flat[r:r + n].reshape(shp))
        r += n
    return out


def _heads(t, nb, g):
    S = t.shape[0]
    return t.reshape(S, nb, g, HEAD_DIM).transpose(1, 2, 0, 3)


def _unheads(t):
    nb, g, S, hd = t.shape
    return t.transpose(2, 0, 1, 3).reshape(S, nb * g * hd)


def _dilate(t, dil):
    S = t.shape[0]
    g = t.shape[1] // HEAD_DIM
    return t.reshape(S // dil, dil, g, HEAD_DIM).transpose(1, 2, 0, 3)


def _undilate(t):
    dil, g, L, w = t.shape
    return t.transpose(1, 2, 0, 3).reshape(g, L * dil, w)


def _full_tiles(S):
    return min(256, S), min(1024, S)


def _mixer_fwd(kind, pr, sink, li):
    S = pr.shape[0]
    if kind == 0:
        q = _heads(pr[:, :Q_W], N_KV_HEADS, GQA_GROUP)
        k = _heads(pr[:, Q_W:QK_W], N_KV_HEADS, 1)[:, 0]
        v = _heads(pr[:, QK_W:QK_W + KV_W], N_KV_HEADS, 1)[:, 0]
        o, lse = _attention(q, k, v, radius=A_RADIUS, sink=sink, bq=min(256, S), tag=f"a{li}")
        return _unheads(o), (q, k, v, o, lse)
    if kind == 1:
        q = _heads(pr[:, :Q_W], N_KV_HEADS, GQA_GROUP)
        k = _heads(pr[:, Q_W:QK_W], N_KV_HEADS, 1)[:, 0]
        v = _heads(pr[:, QK_W:QK_W + KV_W], N_KV_HEADS, 1)[:, 0]
        bq, tk = _full_tiles(S)
        o, lse = _attention(q, k, v, bq=bq, tk=tk, tag=f"b{li}")
        return _unheads(o), (q, k, v, o, lse)
    saved, outs, lses = [], [], []
    for g, (window, dil) in enumerate(C_GROUPS):
        q = _dilate(pr[:, g * GQA_GROUP * HEAD_DIM:(g + 1) * GQA_GROUP * HEAD_DIM], dil)
        k = _dilate(pr[:, Q_W + g * HEAD_DIM:Q_W + (g + 1) * HEAD_DIM], dil)[:, 0]
        v = _dilate(pr[:, QK_W + g * HEAD_DIM:QK_W + (g + 1) * HEAD_DIM], dil)[:, 0]
        o, lse = _attention(q, k, v, radius=window // (2 * dil), bq=min(256, S // dil), tag=f"c{li}g{g}")
        saved.append((q, k, v, o, lse))
        outs.append(_undilate(o))
        lses.append(_undilate(lse))
    o_all, lse_all = jnp.concatenate(outs, 0), jnp.concatenate(lses, 0)
    tok = _combine_fwd(o_all, lse_all, name=f"combine_fwd_{li}")
    return tok.transpose(1, 0, 2).reshape(S, Q_W), (saved, o_all, lse_all)


def _mixer_bwd(kind, dtok, saved, sink, li):
    S = dtok.shape[0]
    if kind in (0, 1):
        q, k, v, o, lse = saved
        do = _heads(dtok, N_KV_HEADS, GQA_GROUP)
        if kind == 0:
            dq, dk, dv, dsink = _attention_bwd(q, k, v, o, lse, do, radius=A_RADIUS, sink=sink, bq=min(256, S), tag=f"a{li}")
        else:
            bq, tk = _full_tiles(S)
            dq, dk, dv, dsink = _attention_bwd(q, k, v, o, lse, do, bq=bq, tk=tk, tag=f"b{li}")
        return _unheads(dq), _unheads(dk[:, None]), _unheads(dv[:, None]), dsink
    per_group, o_all, lse_all = saved
    dt = dtok.reshape(S, N_TOK_HEADS, HEAD_DIM).transpose(1, 0, 2)
    do_all, dlse_all = _combine_bwd(dt, o_all, lse_all, name=f"combine_bwd_{li}")
    dqs, dks, dvs = [], [], []
    for g, (window, dil) in enumerate(C_GROUPS):
        q, k, v, o, lse = per_group[g]
        L = S // dil
        hs = slice(g * GQA_GROUP, (g + 1) * GQA_GROUP)
        do = do_all[hs].reshape(GQA_GROUP, L, dil, HEAD_DIM).transpose(2, 0, 1, 3)
        dlse = dlse_all[hs].reshape(GQA_GROUP, L, dil, 1).transpose(2, 0, 1, 3)
        dq, dk, dv, _ = _attention_bwd(q, k, v, o, lse, do, radius=window // (2 * dil), dlse=dlse, bq=min(256, L),
                                       tag=f"c{li}g{g}")
        dqs.append(dq.transpose(2, 0, 1, 3).reshape(S, GQA_GROUP * HEAD_DIM))
        dks.append(dk.transpose(1, 0, 2).reshape(S, HEAD_DIM))
        dvs.append(dv.transpose(1, 0, 2).reshape(S, HEAD_DIM))
    return jnp.concatenate(dqs, 1), jnp.concatenate(dks, 1), jnp.concatenate(dvs, 1), None


def kernel(x, mem, mem_norm_g, w_in, w_mem_kv, w_o, g_mix_pre, g_mix_post, attn_sink, qk_norm_g, w_gate_up, w_down, g_ffn_pre, g_ffn_post, loss_target, m_mem_norm_g, m_w_in, m_w_mem_kv, m_w_o, m_g_mix_pre, m_g_mix_post, m_attn_sink, m_qk_norm_g, m_w_gate_up, m_w_down, m_g_ffn_pre, m_g_ffn_post, v_mem_norm_g, v_w_in, v_w_mem_kv, v_w_o, v_g_mix_pre, v_g_mix_post, v_attn_sink, v_qk_norm_g, v_w_gate_up, v_w_down, v_g_ffn_pre, v_g_ffn_post):
    given = dict(locals())
    depth = w_in.shape[0]
    S, D = x.shape[1], x.shape[2]
    x0 = x[0]
    big_local = [given[n] for n, _ in BIG]
    big_shapes = [t.shape for t in big_local]

    W_in, W_mkv, W_o, W_gu, W_dn = _unpack_gathered(_exchange(_pack_local(big_local, BF16), True, "gather_weights"), big_shapes)

    tabs = _rope_tables(S)
    mem_n = _rms_fwd(mem[0], mem_norm_g[None], BF16, name="rms_mem")

    saved = []
    xc = x0
    for i in range(depth):
        kind = i % N_MIXERS
        (tab, shift) = tabs[1] if kind == 1 else tabs[0]
        sink = attn_sink[i // N_MIXERS] if kind == 0 else None
        qk_gain = _qk_gain_row(qk_norm_g[i // N_MIXERS]) if kind == 1 else None
        h = _rms_fwd(xc, g_mix_pre[i][None], BF16, name=f"rms_pre_{i}")
        proj = _mm(h, W_in[i], F32, name=f"mm_in_{i}")
        pr = _headprep_fwd(proj, tab, shift, qk_gain, name=f"headprep_fwd_{i}")
        tok, mix_saved = _mixer_fwd(kind, pr, sink, i)
        mkv = _mm(mem_n, W_mkv[i], BF16, name=f"mm_mkv_{i}")
        qm = _heads(pr[:, QK_W + KV_W:], N_MEM_HEADS, 1)
        km = _heads(mkv[:, :QM_W], N_MEM_HEADS, 1)[:, 0]
        vm = _heads(mkv[:, QM_W:], N_MEM_HEADS, 1)[:, 0]
        mo, mlse = _attention(qm, km, vm, bq=min(1024, S), tag=f"m{i}")
        cat = jnp.concatenate([tok, _unheads(mo)], axis=1)
        o = _mm(cat, W_o[i], F32, name=f"mm_o_{i}")
        x1 = _rms_fwd(o, g_mix_post[i][None], F32, res=xc, name=f"rms_post_{i}")
        h2 = _rms_fwd(x1, g_ffn_pre[i][None], BF16, name=f"rms_fpre_{i}")
        gu, act = _mm(h2, W_gu[i], BF16, swiglu=True, name=f"mm_gu_{i}")
        f = _mm(act, W_dn[i], F32, name=f"mm_dn_{i}")
        x2 = _rms_fwd(f, g_ffn_post[i][None], F32, res=x1, name=f"rms_fpost_{i}")
        saved.append(dict(x=xc, h=h, proj=proj, mix=mix_saved, qm=qm, km=km, vm=vm, mo=mo, mlse=mlse, cat=cat, o=o,
                          x1=x1, h2=h2, gu=gu, act=act, f=f))
        xc = x2

    dy, sq = _loss_head(xc, loss_target[0], name="loss_head")
    loss = lax.psum(sq[0, 0] * (0.5 / D), ("x", "y", "c"))

    zero_row = jnp.zeros((1, D), F32)
    grads = {n: [None] * depth for n in ("w_in", "w_mem_kv", "w_o", "w_gate_up", "w_down", "g_mix_pre", "g_mix_post",
                                         "g_ffn_pre", "g_ffn_post")}
    d_sink = [jnp.zeros((N_TOK_HEADS,), F32) for _ in range(attn_sink.shape[0])]
    d_qkg = [jnp.zeros((2, HEAD_DIM), F32) for _ in range(qk_norm_g.shape[0])]
    dmem_n = jnp.zeros((mem.shape[1], D), F32)
    dx = dy
    for i in reversed(range(depth)):
        kind = i % N_MIXERS
        sv = saved[i]
        (tab, shift) = tabs[1] if kind == 1 else tabs[0]
        sink = attn_sink[i // N_MIXERS] if kind == 0 else None
        df, dg = _rms_bwd(sv["f"], g_ffn_post[i][None], dx, BF16, name=f"rmsb_fpost_{i}")
        grads["g_ffn_post"][i] = dg[0]
        da = _mm(df, W_dn[i], BF16, nt=True, name=f"mmb_dn_{i}")
        grads["w_down"][i] = _mm_tn(sv["act"], df, name=f"mmw_dn_{i}")
        dgu = _swiglu_bwd(sv["gu"], da, name=f"swiglu_bwd_{i}")
        dh2 = _mm(dgu, W_gu[i], F32, nt=True, name=f"mmb_gu_{i}")
        grads["w_gate_up"][i] = _mm_tn(sv["h2"], dgu, name=f"mmw_gu_{i}")
        dx1, dg = _rms_bwd(sv["x1"], g_ffn_pre[i][None], dh2, F32, res=dx, name=f"rmsb_fpre_{i}")
        grads["g_ffn_pre"][i] = dg[0]
        do, dg = _rms_bwd(sv["o"], g_mix_post[i][None], dx1, BF16, name=f"rmsb_post_{i}")
        grads["g_mix_post"][i] = dg[0]
        dcat = _mm(do, W_o[i], BF16, nt=True, name=f"mmb_o_{i}")
        grads["w_o"][i] = _mm_tn(sv["cat"], do, name=f"mmw_o_{i}")
        dmo = _heads(dcat[:, Q_W:], N_MEM_HEADS, 1)
        dqm, dkm, dvm, _ = _attention_bwd(sv["qm"], sv["km"], sv["vm"], sv["mo"], sv["mlse"], dmo, bq=min(1024, S),
                                          tag=f"m{i}")
        dmkv = jnp.concatenate([_unheads(dkm[:, None]), _unheads(dvm[:, None])], axis=1).astype(BF16)
        grads["w_mem_kv"][i] = _mm_tn(mem_n, dmkv, name=f"mmw_mkv_{i}")
        dmem_n = dmem_n + _mm(dmkv, W_mkv[i], F32, nt=True, name=f"mmb_mkv_{i}")
        dq, dk, dv, dsink = _mixer_bwd(kind, dcat[:, :Q_W], sv["mix"], sink, i)
        if dsink is not None:
            d_sink[i // N_MIXERS] = dsink
        dpr = jnp.concatenate([dq, dk, dv, _unheads(dqm)], axis=1)
        if kind == 1:
            dproj, dgc = _headprep_bwd(dpr, tab, shift, sv["proj"], _qk_gain_row(qk_norm_g[i // N_MIXERS]),
                                       name=f"headprep_bwd_{i}")
            d_qkg[i // N_MIXERS] = jnp.stack([dgc[0, :Q_W].reshape(N_TOK_HEADS, HEAD_DIM).sum(0),
                                              dgc[0, Q_W:QK_W].reshape(N_KV_HEADS, HEAD_DIM).sum(0)])
        else:
            dproj = _headprep_bwd(dpr, tab, shift, name=f"headprep_bwd_{i}")
        dh = _mm(dproj, W_in[i], F32, nt=True, name=f"mmb_in_{i}")
        grads["w_in"][i] = _mm_tn(sv["h"], dproj, name=f"mmw_in_{i}")
        dx, dg = _rms_bwd(sv["x"], g_mix_pre[i][None], dh, F32, res=dx1, name=f"rmsb_pre_{i}")
        grads["g_mix_pre"][i] = dg[0]
    _, dg_mem = _rms_bwd(mem[0], mem_norm_g[None], dmem_n, BF16, name="rmsb_mem")

    full = [jnp.stack(grads[n]) for n, _ in BIG]
    recv = _exchange(_pack_for_scatter(full, big_shapes, BF16), False, "scatter_grads")
    packed = lambda pre: _pack_local([given[pre + n] for n, _ in BIG], F32)
    gb, db, mb, vb = _reduce_adamw(recv, packed(""), packed("m_"), packed("v_"), name="adamw_big")

    small_grads = dict(mem_norm_g=dg_mem[0], g_mix_pre=jnp.stack(grads["g_mix_pre"]), g_mix_post=jnp.stack(grads["g_mix_post"]),
                       attn_sink=jnp.stack(d_sink), qk_norm_g=jnp.stack(d_qkg), g_ffn_pre=jnp.stack(grads["g_ffn_pre"]),
                       g_ffn_post=jnp.stack(grads["g_ffn_post"]))
    sg = _pack_small([small_grads[n] for n in SMALL])
    srecv = _exchange(sg, True, "gather_small_grads")
    spacked = lambda pre: _pack_small([given[pre + n] for n in SMALL])
    gs, ds, ms, vs = _reduce_adamw(srecv, spacked(""), spacked("m_"), spacked("v_"), name="adamw_small")

    out = {}
    for pre, fb, fs in (("grad_", gb, gs), ("delta_", db, ds), ("new_m_", mb, ms), ("new_v_", vb, vs)):
        for (n, _), t in zip(BIG, _unpack_local(fb, big_shapes)):
            out[pre + n] = t
        for n, t in zip(SMALL, _unpack_small(fs, [given[n].shape for n in SMALL])):
            out[pre + n] = t
    order = ("mem_norm_g", "w_in", "w_mem_kv", "w_o", "g_mix_pre", "g_mix_post", "attn_sink", "qk_norm_g", "w_gate_up",
             "w_down", "g_ffn_pre", "g_ffn_post")
    return (loss, dx[None], *[out[p + n] for p in ("grad_", "delta_", "new_m_", "new_v_") for n in order])
```

```python
import functools

import jax
import jax.numpy as jnp
from jax import lax
from jax.experimental import pallas as pl
from jax.experimental.pallas import tpu as pltpu

F32 = jnp.float32
BF16 = jnp.bfloat16

HEAD_DIM = 64
N_TOK_HEADS = 12
N_KV_HEADS = 3
GQA_GROUP = 4
N_MEM_HEADS = 4
Q_W = N_TOK_HEADS * HEAD_DIM
KV_W = N_KV_HEADS * HEAD_DIM
QM_W = N_MEM_HEADS * HEAD_DIM
QK_W = Q_W + KV_W
IN_W = Q_W + 2 * KV_W + QM_W
N_HEAD_SLOTS = IN_W // HEAD_DIM
N_MIXERS = 3
A_RADIUS = 128
C_GROUPS = ((128, 1), (512, 4), (2048, 16))
ROPE_THETA = 500000.0
ROPE_DIMS = HEAD_DIM // 4
AXIAL_THETA = 10000.0
GRID_W = 64
EPS = 1e-6
ATTN_SCALE = HEAD_DIM ** -0.5
NEG = -1e30

ADAM_LR = 0.001
ADAM_B1 = 0.9
ADAM_B2 = 0.999
ADAM_EPS = 1e-08
ADAM_WD = 0.01
ADAM_STEP = 10

N_DEV = 8
LANES = 128
VMEM_LIMIT = 56 * 1024 * 1024
MESH = pl.DeviceIdType.MESH
NT_DIMS = (((1,), (1,)), ((), ()))
TN_DIMS = (((0,), (0,)), ((), ()))


def _pc(body, **kw):
    return pl.pallas_call(body, **kw)


def _cp(*sem):
    return pltpu.CompilerParams(dimension_semantics=sem, vmem_limit_bytes=VMEM_LIMIT)


def _row_tile(m, cap=512):
    t = cap
    while m % t:
        t //= 2
    return t


def _rms_fwd(x, g, out_dtype, res=None, name="rms_fwd"):
    M, D = x.shape
    tm = _row_tile(M)

    def body(*refs):
        if res is None:
            x_ref, g_ref, o_ref = refs
        else:
            x_ref, g_ref, r_ref, o_ref = refs
        xv = x_ref[...]
        y = xv * lax.rsqrt(jnp.mean(xv * xv, axis=-1, keepdims=True) + EPS) * g_ref[...]
        if res is not None:
            y = r_ref[...] + y
        o_ref[...] = y.astype(o_ref.dtype)

    row = pl.BlockSpec((tm, D), lambda i: (i, 0))
    vec = pl.BlockSpec((1, D), lambda i: (0, 0))
    ins = [x, g] + ([] if res is None else [res])
    specs = [row, vec] + ([] if res is None else [row])
    return _pc(body, name=name, out_shape=jax.ShapeDtypeStruct((M, D), out_dtype), grid=(M // tm,),
               in_specs=specs, out_specs=row, compiler_params=_cp("parallel"))(*ins)


def _rms_bwd(x, g, dy, out_dtype, res=None, name="rms_bwd"):
    M, D = x.shape
    tm = _row_tile(M)

    def body(*refs):
        if res is None:
            x_ref, g_ref, dy_ref, dx_ref, dg_ref = refs
        else:
            x_ref, g_ref, dy_ref, r_ref, dx_ref, dg_ref = refs
        xv = x_ref[...]
        r = lax.rsqrt(jnp.mean(xv * xv, axis=-1, keepdims=True) + EPS)
        xh = xv * r
        d = dy_ref[...].astype(F32)
        dxh = d * g_ref[...]
        dx = r * (dxh - xh * jnp.mean(dxh * xh, axis=-1, keepdims=True))
        if res is not None:
            dx = r_ref[...] + dx
        dx_ref[...] = dx.astype(dx_ref.dtype)

        @pl.when(pl.program_id(0) == 0)
        def _():
            dg_ref[...] = jnp.zeros_like(dg_ref)

        dg_ref[...] += jnp.sum(d * xh, axis=0, keepdims=True)

    row = pl.BlockSpec((tm, D), lambda i: (i, 0))
    vec = pl.BlockSpec((1, D), lambda i: (0, 0))
    ins = [x, g, dy] + ([] if res is None else [res])
    specs = [row, vec, row] + ([] if res is None else [row])
    return _pc(body, name=name,
               out_shape=(jax.ShapeDtypeStruct((M, D), out_dtype), jax.ShapeDtypeStruct((1, D), F32)),
               grid=(M // tm,), in_specs=specs, out_specs=(row, vec), compiler_params=_cp("arbitrary"))(*ins)


def _swiglu_bwd(gu, da, name="swiglu_bwd"):
    M, F2 = gu.shape
    F = F2 // 2
    tm = _row_tile(M, 256)

    def body(gu_ref, da_ref, o_ref):
        g = gu_ref[:, :F].astype(F32)
        u = gu_ref[:, F:].astype(F32)
        d = da_ref[...].astype(F32)
        sig = 1.0 / (1.0 + jnp.exp(-g))
        o_ref[:, :F] = (d * u * (sig * (1.0 + g * (1.0 - sig)))).astype(o_ref.dtype)
        o_ref[:, F:] = (d * (g * sig)).astype(o_ref.dtype)

    return _pc(body, name=name, out_shape=jax.ShapeDtypeStruct((M, F2), BF16), grid=(M // tm,),
               in_specs=[pl.BlockSpec((tm, F2), lambda i: (i, 0)), pl.BlockSpec((tm, F), lambda i: (i, 0))],
               out_specs=pl.BlockSpec((tm, F2), lambda i: (i, 0)), compiler_params=_cp("parallel"))(gu, da)


def _loss_head(y, t, name="loss_head"):
    M, D = y.shape
    tm = _row_tile(M)

    def body(y_ref, t_ref, dy_ref, acc_ref):
        e = y_ref[...] - t_ref[...]
        dy_ref[...] = e * (1.0 / D)

        @pl.when(pl.program_id(0) == 0)
        def _():
            acc_ref[...] = jnp.zeros_like(acc_ref)

        acc_ref[...] += jnp.sum(e * e)

    row = pl.BlockSpec((tm, D), lambda i: (i, 0))
    return _pc(body, name=name,
               out_shape=(jax.ShapeDtypeStruct((M, D), F32), jax.ShapeDtypeStruct((8, LANES), F32)),
               grid=(M // tm,), in_specs=[row, row],
               out_specs=(row, pl.BlockSpec((8, LANES), lambda i: (0, 0))), compiler_params=_cp("arbitrary"))(y, t)


def _mm(a, w, out_dtype, nt=False, pre_g=None, swiglu=False, post=None, gu=None, tm=256, name="mm"):
    M, K = a.shape
    N = w.shape[0] if nt else w.shape[1]
    tm = _row_tile(M, tm)

    def body(*refs):
        refs = list(refs)
        a_ref, w_ref = refs.pop(0), refs.pop(0)
        pg_ref = refs.pop(0) if pre_g is not None else None
        g_ref, r_ref = (refs.pop(0), refs.pop(0)) if post is not None else (None, None)
        gu_ref = refs.pop(0) if gu is not None else None
        lhs = a_ref[...]
        if pre_g is not None:
            lhs = (lhs * lax.rsqrt(jnp.mean(lhs * lhs, axis=-1, keepdims=True) + EPS) * pg_ref[...]).astype(BF16)
            refs.pop(0)[...] = lhs
        if nt:
            acc = lax.dot_general(lhs, w_ref[...], NT_DIMS, preferred_element_type=F32)
        else:
            acc = jnp.dot(lhs, w_ref[...], preferred_element_type=F32)
        o_ref = refs.pop(0)
        if gu is None:
            o_ref[...] = acc.astype(o_ref.dtype)
        else:
            gate = gu_ref[:, :N].astype(F32)
            sig = 1.0 / (1.0 + jnp.exp(-gate))
            o_ref[:, :N] = (acc * gu_ref[:, N:].astype(F32) * (sig * (1.0 + gate * (1.0 - sig)))).astype(o_ref.dtype)
            o_ref[:, N:] = (acc * (gate * sig)).astype(o_ref.dtype)
        if swiglu:
            gate = acc[:, : N // 2]
            refs.pop(0)[...] = (gate * (1.0 / (1.0 + jnp.exp(-gate))) * acc[:, N // 2:]).astype(BF16)
        if post is not None:
            y = acc * lax.rsqrt(jnp.mean(acc * acc, axis=-1, keepdims=True) + EPS) * g_ref[...]
            refs.pop(0)[...] = r_ref[...] + y

    row = lambda n: pl.BlockSpec((tm, n), lambda i: (i, 0))
    vec = lambda n: pl.BlockSpec((1, n), lambda i: (0, 0))
    ins = [a, w]
    specs = [row(K), pl.BlockSpec(w.shape, lambda i: (0, 0), pipeline_mode=pl.Buffered(1))]
    outs, ospecs = [], []
    if pre_g is not None:
        ins, specs = ins + [pre_g], specs + [vec(K)]
        outs, ospecs = outs + [jax.ShapeDtypeStruct((M, K), BF16)], ospecs + [row(K)]
    if post is not None:
        ins, specs = ins + list(post), specs + [vec(N), row(N)]
    if gu is not None:
        ins, specs = ins + [gu], specs + [row(2 * N)]
        outs, ospecs = outs + [jax.ShapeDtypeStruct((M, 2 * N), BF16)], ospecs + [row(2 * N)]
    else:
        outs, ospecs = outs + [jax.ShapeDtypeStruct((M, N), out_dtype)], ospecs + [row(N)]
    if swiglu:
        outs, ospecs = outs + [jax.ShapeDtypeStruct((M, N // 2), BF16)], ospecs + [row(N // 2)]
    if post is not None:
        outs, ospecs = outs + [jax.ShapeDtypeStruct((M, N), F32)], ospecs + [row(N)]
    return _pc(body, name=name, out_shape=tuple(outs), grid=(M // tm,), in_specs=specs, out_specs=tuple(ospecs),
               compiler_params=_cp("parallel"))(*ins)


def _mm_tn(a, b, name="mm_tn"):
    S, M = a.shape
    N = b.shape[1]
    tm = M if M <= 1408 else M // 2
    tn = N if N <= 1408 else N // 4
    ts = _row_tile(S, 1024)

    def body(a_ref, b_ref, o_ref):
        @pl.when(pl.program_id(2) == 0)
        def _():
            o_ref[...] = jnp.zeros_like(o_ref)

        o_ref[...] += lax.dot_general(a_ref[...], b_ref[...], TN_DIMS, preferred_element_type=F32)

    return _pc(body, name=name, out_shape=jax.ShapeDtypeStruct((M, N), F32), grid=(M // tm, N // tn, S // ts),
               in_specs=[pl.BlockSpec((ts, tm), lambda i, j, s: (s, i)), pl.BlockSpec((ts, tn), lambda i, j, s: (s, j))],
               out_specs=pl.BlockSpec((tm, tn), lambda i, j, s: (i, j)),
               compiler_params=_cp("parallel", "parallel", "arbitrary"))(a, b)


def _rope_tables(S):
    pos = jnp.arange(S, dtype=jnp.int32)

    def table(p, n_dims, theta):
        inv = theta ** (-(jnp.arange(0, n_dims, 2, dtype=F32) / n_dims))
        ang = p.astype(F32)[:, None] * inv[None, :]
        return jnp.cos(ang), jnp.sin(ang)

    one = lambda n: jnp.ones((S, n), F32)
    zero = lambda n: jnp.zeros((S, n), F32)
    cp, sp = table(pos, ROPE_DIMS, ROPE_THETA)
    rest = HEAD_DIM - ROPE_DIMS
    part = (jnp.concatenate([cp, cp, one(rest)], 1), jnp.concatenate([zero(8), sp, zero(rest)], 1),
            jnp.concatenate([-sp, zero(8), zero(rest)], 1))
    cr, sr = table(pos // GRID_W, HEAD_DIM // 2, AXIAL_THETA)
    cc, sc = table(pos % GRID_W, HEAD_DIM // 2, AXIAL_THETA)
    axial = (jnp.concatenate([cr, cr, cc, cc], 1), jnp.concatenate([zero(16), sr, zero(16), sc], 1),
             jnp.concatenate([-sr, zero(16), -sc, zero(16)], 1))
    rep = LANES // HEAD_DIM
    return (tuple(jnp.tile(t, (1, rep)) for t in part), ROPE_DIMS // 2), (tuple(jnp.tile(t, (1, rep)) for t in axial), HEAD_DIM // 4)


def _seg_mats():
    col = jnp.arange(IN_W)[:, None] // HEAD_DIM
    e = (col == jnp.arange(LANES)[None, :]).astype(F32)
    return e, e.T


def _qk_gain_row(qk_g):
    return jnp.concatenate([jnp.tile(qk_g[0], N_TOK_HEADS), jnp.tile(qk_g[1], N_KV_HEADS),
                            jnp.ones((IN_W - QK_W,), F32)])[None, :]


def _rope_cols(tabs, tm):
    col = lax.broadcasted_iota(jnp.int32, (tm, IN_W), 1)
    qk = col < QK_W
    c, s_lo, s_hi = (jnp.tile(t[...], (1, IN_W // LANES)) for t in tabs)
    return jnp.where(qk, c, 1.0), jnp.where(qk, s_lo, 0.0), jnp.where(qk, s_hi, 0.0), qk


def _seg_mean(v, e_ref, et_ref):
    s = jnp.dot(v, e_ref[...], precision=lax.Precision.HIGHEST, preferred_element_type=F32) * (1.0 / HEAD_DIM)
    return jnp.dot(s, et_ref[...], precision=lax.Precision.HIGHEST, preferred_element_type=F32)


def _headprep_fwd(proj, tabs, shift, qk_gain=None, name="headprep_fwd"):
    S = proj.shape[0]
    tm = _row_tile(S, 256)
    norm = qk_gain is not None

    def body(*refs):
        if norm:
            p_ref, c_ref, lo_ref, hi_ref, g_ref, e_ref, et_ref, o_ref = refs
        else:
            p_ref, c_ref, lo_ref, hi_ref, o_ref = refs
        x = p_ref[...]
        c, s_lo, s_hi, qk = _rope_cols((c_ref, lo_ref, hi_ref), tm)
        if norm:
            r = lax.rsqrt(_seg_mean(x * x, e_ref, et_ref) + EPS)
            x = x * jnp.where(qk, r, 1.0) * g_ref[...]
        y = x * c + pltpu.roll(x, shift, 1) * s_lo + pltpu.roll(x, IN_W - shift, 1) * s_hi
        o_ref[...] = y.astype(o_ref.dtype)

    row = pl.BlockSpec((tm, IN_W), lambda i: (i, 0))
    tab = pl.BlockSpec((tm, LANES), lambda i: (i, 0))
    ins = [proj, *tabs]
    specs = [row, tab, tab, tab]
    if norm:
        e, et = _seg_mats()
        ins += [qk_gain, e, et]
        specs += [pl.BlockSpec((1, IN_W), lambda i: (0, 0)), pl.BlockSpec((IN_W, LANES), lambda i: (0, 0)),
                  pl.BlockSpec((LANES, IN_W), lambda i: (0, 0))]
    return _pc(body, name=name, out_shape=jax.ShapeDtypeStruct((S, IN_W), BF16), grid=(S // tm,),
               in_specs=specs, out_specs=row, compiler_params=_cp("parallel"))(*ins)


def _headprep_bwd(dpr, tabs, shift, proj=None, qk_gain=None, name="headprep_bwd"):
    S = dpr.shape[0]
    tm = _row_tile(S, 256)
    norm = qk_gain is not None

    def body(*refs):
        if norm:
            d_ref, c_ref, lo_ref, hi_ref, p_ref, g_ref, e_ref, et_ref, o_ref, dg_ref = refs
        else:
            d_ref, c_ref, lo_ref, hi_ref, o_ref = refs
        d = d_ref[...].astype(F32)
        c, s_lo, s_hi, qk = _rope_cols((c_ref, lo_ref, hi_ref), tm)
        dx = d * c + pltpu.roll(d * s_lo, IN_W - shift, 1) + pltpu.roll(d * s_hi, shift, 1)
        if norm:
            x = p_ref[...]
            r = lax.rsqrt(_seg_mean(x * x, e_ref, et_ref) + EPS)
            xh = x * r

            @pl.when(pl.program_id(0) == 0)
            def _():
                dg_ref[...] = jnp.zeros_like(dg_ref)

            dg_ref[...] += jnp.sum(jnp.where(qk, dx * xh, 0.0), axis=0, keepdims=True)
            dxh = dx * g_ref[...]
            dn = r * (dxh - xh * _seg_mean(dxh * xh, e_ref, et_ref))
            dx = jnp.where(qk, dn, dx)
        o_ref[...] = dx.astype(o_ref.dtype)

    row = pl.BlockSpec((tm, IN_W), lambda i: (i, 0))
    tab = pl.BlockSpec((tm, LANES), lambda i: (i, 0))
    vec = pl.BlockSpec((1, IN_W), lambda i: (0, 0))
    ins = [dpr, *tabs]
    specs = [row, tab, tab, tab]
    out_shape = jax.ShapeDtypeStruct((S, IN_W), BF16)
    out_specs = row
    if norm:
        e, et = _seg_mats()
        ins += [proj, qk_gain, e, et]
        specs += [row, vec, pl.BlockSpec((IN_W, LANES), lambda i: (0, 0)), pl.BlockSpec((LANES, IN_W), lambda i: (0, 0))]
        out_shape = (out_shape, jax.ShapeDtypeStruct((1, IN_W), F32))
        out_specs = (row, vec)
    return _pc(body, name=name, out_shape=out_shape, grid=(S // tm,), in_specs=specs, out_specs=out_specs,
               compiler_params=_cp("arbitrary" if norm else "parallel"))(*ins)


def _band_mask(qb, kb, n_kb_valid, bq, tk, rows, radius):
    qpos = qb * bq + lax.rem(lax.broadcasted_iota(jnp.int32, (rows, 1), 0), bq)
    kpos = kb * tk + lax.broadcasted_iota(jnp.int32, (1, tk), 1)
    return (jnp.abs(qpos - kpos) <= radius) & n_kb_valid


def _flash_fwd(q, k, v, *, radius=None, sink=None, bq, tk, name="flash_fwd"):
    NB, G, L, HD = q.shape
    Lk = k.shape[1]
    nq, nk = L // bq, Lk // tk
    banded = radius is not None
    assert not banded or (bq == tk and radius <= tk and L == Lk)
    nst = 3 if banded else nk
    rows = G * bq

    def kmap(b, n, j):
        return (b, jnp.clip(n - 1 + j, 0, nk - 1), 0) if banded else (b, j, 0)

    def body(*refs):
        if sink is not None:
            sink_ref, *refs = refs
        q_ref, k_ref, v_ref, o_ref, lse_ref, m_sc, acc_sc = refs
        b, n, j = pl.program_id(0), pl.program_id(1), pl.program_id(2)

        @pl.when(j == 0)
        def _():
            if sink is not None:
                m_sc[...] = _sink_rows(sink_ref, b, G, bq, rows)
                acc_sc[...] = (lax.broadcasted_iota(jnp.int32, (rows, 2 * HD), 1) >= HD).astype(F32)
            else:
                m_sc[...] = jnp.full_like(m_sc, NEG)
                acc_sc[...] = jnp.zeros_like(acc_sc)

        qv = q_ref[0].reshape(rows, HD) * ATTN_SCALE
        s = lax.dot_general(qv, k_ref[0], NT_DIMS, preferred_element_type=F32)
        if banded:
            kb = n - 1 + j
            mask = _band_mask(n, kb, (kb >= 0) & (kb < nk), bq, tk, rows, radius)
            s = jnp.where(mask, s, NEG)
        m_prev = m_sc[...]
        m_new = jnp.maximum(m_prev, jnp.max(s, axis=1, keepdims=True))
        p = jnp.exp(s - m_new)
        if banded:
            p = jnp.where(mask, p, 0.0)
        v_ones = jnp.concatenate([v_ref[0], jnp.ones((tk, HD), BF16)], axis=1)
        acc_sc[...] = jnp.exp(m_prev - m_new) * acc_sc[...] + jnp.dot(p.astype(BF16), v_ones, preferred_element_type=F32)
        m_sc[...] = m_new

        @pl.when(j == nst - 1)
        def _():
            acc = acc_sc[...]
            l = acc[:, HD:HD + 1]
            o_ref[0] = (acc[:, :HD] / l).reshape(G, bq, HD).astype(o_ref.dtype)
            lse_ref[0] = (m_sc[...] + jnp.log(l)).reshape(G, bq, 1)

    qspec = pl.BlockSpec((1, G, bq, HD), lambda b, n, j: (b, 0, n, 0))
    kspec = pl.BlockSpec((1, tk, HD), kmap)
    ins, specs = [q, k, v], [qspec, kspec, kspec]
    if sink is not None:
        ins, specs = [sink] + ins, [pl.BlockSpec(memory_space=pltpu.SMEM)] + specs
    return _pc(body, name=name,
               out_shape=(jax.ShapeDtypeStruct((NB, G, L, HD), BF16), jax.ShapeDtypeStruct((NB, G, L, 1), F32)),
               grid=(NB, nq, nst), in_specs=specs,
               out_specs=(qspec, pl.BlockSpec((1, G, bq, 1), lambda b, n, j: (b, 0, n, 0))),
               scratch_shapes=[pltpu.VMEM((rows, 1), F32), pltpu.VMEM((rows, 2 * HD), F32)],
               compiler_params=_cp("parallel", "parallel", "arbitrary"))(*ins)


def _attn_delta(do, o, *, dlse=None, lse=None, sink=None, name="attn_delta"):
    NB, G, L, HD = do.shape
    bl = _row_tile(L, 1024)

    def body(*refs):
        refs = list(refs)
        sink_ref = refs.pop(0) if sink is not None else None
        do_ref, o_ref = refs.pop(0), refs.pop(0)
        dlse_ref = refs.pop(0) if dlse is not None else None
        lse_ref = refs.pop(0) if sink is not None else None
        delta_ref = refs.pop(0)
        b = pl.program_id(0)
        delta = jnp.sum(do_ref[0].astype(F32) * o_ref[0].astype(F32), axis=-1, keepdims=True)
        if dlse is not None:
            delta = delta - dlse_ref[0]
        delta_ref[0] = delta
        if sink is not None:
            ds_ref = refs.pop(0)

            @pl.when(pl.program_id(1) == 0)
            def _():
                ds_ref[...] = jnp.zeros_like(ds_ref)

            for g in range(G):
                ps = jnp.exp(sink_ref[b * G + g] - lse_ref[0, g]) * delta[g]
                ds_ref[0, g] -= jnp.sum(ps)

    blk = pl.BlockSpec((1, G, bl, HD), lambda b, n: (b, 0, n, 0))
    col = pl.BlockSpec((1, G, bl, 1), lambda b, n: (b, 0, n, 0))
    ins, specs = [do, o], [blk, blk]
    if dlse is not None:
        ins, specs = ins + [dlse], specs + [col]
    out_shape = jax.ShapeDtypeStruct((NB, G, L, 1), F32)
    out_specs = col
    if sink is not None:
        ins, specs = [sink] + ins + [lse], [pl.BlockSpec(memory_space=pltpu.SMEM)] + specs + [col]
        out_shape = (out_shape, jax.ShapeDtypeStruct((NB, G, 1, LANES), F32))
        out_specs = (col, pl.BlockSpec((1, G, 1, LANES), lambda b, n: (b, 0, 0, 0)))
    return _pc(body, name=name, out_shape=out_shape, grid=(NB, L // bl), in_specs=specs, out_specs=out_specs,
               compiler_params=_cp("parallel", "arbitrary"))(*ins)


def _flash_dq(q, k, v, do, lse, delta, *, radius=None, bq, tk, name="flash_dq"):
    NB, G, L, HD = q.shape
    Lk = k.shape[1]
    nq, nk = L // bq, Lk // tk
    banded = radius is not None
    nst = 3 if banded else nk
    rows = G * bq

    def kmap(b, n, j):
        return (b, jnp.clip(n - 1 + j, 0, nk - 1), 0) if banded else (b, j, 0)

    def body(q_ref, k_ref, v_ref, do_ref, lse_ref, dl_ref, dq_ref, acc_sc):
        n, j = pl.program_id(1), pl.program_id(2)

        @pl.when(j == 0)
        def _():
            acc_sc[...] = jnp.zeros_like(acc_sc)

        qv = q_ref[0].reshape(rows, HD)
        dov = do_ref[0].reshape(rows, HD)
        s = lax.dot_general(qv, k_ref[0], NT_DIMS, preferred_element_type=F32) * ATTN_SCALE
        p = jnp.exp(s - lse_ref[0].reshape(rows, 1))
        if banded:
            kb = n - 1 + j
            p = jnp.where(_band_mask(n, kb, (kb >= 0) & (kb < nk), bq, tk, rows, radius), p, 0.0)
        dp = lax.dot_general(dov, v_ref[0], NT_DIMS, preferred_element_type=F32)
        ds = p * (dp - dl_ref[0].reshape(rows, 1))
        acc_sc[...] += jnp.dot(ds.astype(BF16), k_ref[0], preferred_element_type=F32)

        @pl.when(j == nst - 1)
        def _():
            dq_ref[0] = (acc_sc[...] * ATTN_SCALE).reshape(G, bq, HD).astype(dq_ref.dtype)

    qspec = pl.BlockSpec((1, G, bq, HD), lambda b, n, j: (b, 0, n, 0))
    cspec = pl.BlockSpec((1, G, bq, 1), lambda b, n, j: (b, 0, n, 0))
    kspec = pl.BlockSpec((1, tk, HD), kmap)
    return _pc(body, name=name, out_shape=jax.ShapeDtypeStruct((NB, G, L, HD), F32), grid=(NB, nq, nst),
               in_specs=[qspec, kspec, kspec, qspec, cspec, cspec], out_specs=qspec,
               scratch_shapes=[pltpu.VMEM((rows, HD), F32)],
               compiler_params=_cp("parallel", "parallel", "arbitrary"))(q, k, v, do, lse, delta)


def _flash_dkv(q, k, v, do, lse, delta, *, radius=None, bq, tk, name="flash_dkv"):
    NB, G, L, HD = q.shape
    Lk = k.shape[1]
    nq, nk = L // bq, Lk // tk
    banded = radius is not None
    nst = 3 if banded else nq
    rows = G * bq

    def qmap(b, m, j):
        return (b, 0, jnp.clip(m - 1 + j, 0, nq - 1), 0) if banded else (b, 0, j, 0)

    def body(q_ref, k_ref, v_ref, do_ref, lse_ref, dl_ref, dk_ref, dv_ref, dk_sc, dv_sc):
        m, j = pl.program_id(1), pl.program_id(2)

        @pl.when(j == 0)
        def _():
            dk_sc[...] = jnp.zeros_like(dk_sc)
            dv_sc[...] = jnp.zeros_like(dv_sc)

        qv = q_ref[0].reshape(rows, HD)
        dov = do_ref[0].reshape(rows, HD)
        s = lax.dot_general(qv, k_ref[0], NT_DIMS, preferred_element_type=F32) * ATTN_SCALE
        p = jnp.exp(s - lse_ref[0].reshape(rows, 1))
        if banded:
            qb = m - 1 + j
            p = jnp.where(_band_mask(qb, m, (qb >= 0) & (qb < nq), bq, tk, rows, radius), p, 0.0)
        dv_sc[...] += lax.dot_general(p.astype(BF16), dov, TN_DIMS, preferred_element_type=F32)
        dp = lax.dot_general(dov, v_ref[0], NT_DIMS, preferred_element_type=F32)
        ds = p * (dp - dl_ref[0].reshape(rows, 1))
        dk_sc[...] += lax.dot_general(ds.astype(BF16), qv, TN_DIMS, preferred_element_type=F32)

        @pl.when(j == nst - 1)
        def _():
            dk_ref[0] = (dk_sc[...] * ATTN_SCALE).astype(dk_ref.dtype)
            dv_ref[0] = dv_sc[...].astype(dv_ref.dtype)

    qspec = pl.BlockSpec((1, G, bq, HD), qmap)
    cspec = pl.BlockSpec((1, G, bq, 1), qmap)
    kspec = pl.BlockSpec((1, tk, HD), lambda b, m, j: (b, m, 0))
    kv_shape = jax.ShapeDtypeStruct((NB, Lk, HD), F32)
    return _pc(body, name=name, out_shape=(kv_shape, kv_shape), grid=(NB, nk, nst),
               in_specs=[qspec, kspec, kspec, qspec, cspec, cspec], out_specs=(kspec, kspec),
               scratch_shapes=[pltpu.VMEM((tk, HD), F32), pltpu.VMEM((tk, HD), F32)],
               compiler_params=_cp("parallel", "parallel", "arbitrary"))(q, k, v, do, lse, delta)


def _flash_bwd_full(q, k, v, do, lse, delta, *, bq, tk, name="flash_bwd"):
    NB, G, L, HD = q.shape
    Lk = k.shape[1]
    nq, nk = L // bq, Lk // tk
    rows = G * bq

    def body(q_ref, k_ref, v_ref, do_ref, lse_ref, dl_ref, dqp_ref, dk_ref, dv_ref, dk_sc, dv_sc):
        n = pl.program_id(2)

        @pl.when(n == 0)
        def _():
            dk_sc[...] = jnp.zeros_like(dk_sc)
            dv_sc[...] = jnp.zeros_like(dv_sc)

        qs = q_ref[0].reshape(rows, HD) * ATTN_SCALE
        dov = do_ref[0].reshape(rows, HD)
        s = lax.dot_general(qs, k_ref[0], NT_DIMS, preferred_element_type=F32)
        p = jnp.exp(s - lse_ref[0].reshape(rows, 1))
        dp = lax.dot_general(dov, v_ref[0], NT_DIMS, preferred_element_type=F32)
        ds = (p * (dp - dl_ref[0].reshape(rows, 1))).astype(BF16)
        dv_sc[...] += lax.dot_general(p.astype(BF16), dov, TN_DIMS, preferred_element_type=F32)
        dk_sc[...] += lax.dot_general(ds, qs, TN_DIMS, preferred_element_type=F32)
        dqp_ref[0, 0] = (jnp.dot(ds, k_ref[0], preferred_element_type=F32) * ATTN_SCALE).reshape(G, bq, HD)

        @pl.when(n == nq - 1)
        def _():
            dk_ref[0] = dk_sc[...]
            dv_ref[0] = dv_sc[...]

    qspec = pl.BlockSpec((1, G, bq, HD), lambda b, m, n: (b, 0, n, 0))
    cspec = pl.BlockSpec((1, G, bq, 1), lambda b, m, n: (b, 0, n, 0))
    kspec = pl.BlockSpec((1, tk, HD), lambda b, m, n: (b, m, 0))
    kv_shape = jax.ShapeDtypeStruct((NB, Lk, HD), F32)
    dqp, dk, dv = _pc(body, name=name,
                      out_shape=(jax.ShapeDtypeStruct((nk, NB, G, L, HD), F32), kv_shape, kv_shape), grid=(NB, nk, nq),
                      in_specs=[qspec, kspec, kspec, qspec, cspec, cspec],
                      out_specs=(pl.BlockSpec((1, 1, G, bq, HD), lambda b, m, n: (m, b, 0, n, 0)), kspec, kspec),
                      scratch_shapes=[pltpu.VMEM((tk, HD), F32), pltpu.VMEM((tk, HD), F32)],
                      compiler_params=_cp("parallel", "parallel", "arbitrary"))(q, k, v, do, lse, delta)
    bl = _row_tile(L, 512)

    def sum_body(p_ref, o_ref):
        acc = p_ref[0, 0]
        for j in range(1, nk):
            acc = acc + p_ref[j, 0]
        o_ref[0] = acc

    dq = _pc(sum_body, name=name + "_sum", out_shape=jax.ShapeDtypeStruct((NB, G, L, HD), F32), grid=(NB, L // bl),
             in_specs=[pl.BlockSpec((nk, 1, G, bl, HD), lambda b, n: (0, b, 0, n, 0))],
             out_specs=pl.BlockSpec((1, G, bl, HD), lambda b, n: (b, 0, n, 0)),
             compiler_params=_cp("parallel", "parallel"))(dqp)
    return dq, dk, dv


def _window(L, blk, radius):
    if radius is None:
        return L, None
    W = min(L, blk + 2 * radius)
    assert blk % radius == 0 and (L - W) % radius == 0
    return W, lambda n: radius * jnp.clip(n * (blk // radius) - 1, 0, (L - W) // radius)


def _win_specs(G, W, HD, start, with_g):
    E = pl.Element
    st = (lambda n: 0) if start is None else start
    if with_g:
        return pl.BlockSpec((E(1), E(G), E(W), E(HD)), lambda b, n: (b, 0, st(n), 0))
    return pl.BlockSpec((E(1), E(W), E(HD)), lambda b, n: (b, st(n), 0))


def _sink_rows(sink_ref, b, G, per, rows):
    head = lax.div(lax.broadcasted_iota(jnp.int32, (rows, 1), 0), per)
    sk = jnp.zeros((rows, 1), F32)
    for g in range(G):
        sk = jnp.where(head == g, sink_ref[b * G + g], sk)
    return sk


def _win_fwd(q, k, v, *, radius=None, sink=None, bq, name="win_fwd"):
    NB, G, L, HD = q.shape
    Lk = k.shape[1]
    W, start = _window(Lk, bq, radius)
    rows = G * bq

    def body(*refs):
        if sink is not None:
            sink_ref, *refs = refs
        q_ref, k_ref, v_ref, o_ref, lse_ref = refs
        b, n = pl.program_id(0), pl.program_id(1)
        qv = q_ref[0].reshape(rows, HD) * ATTN_SCALE
        s = lax.dot_general(qv, k_ref[0], NT_DIMS, preferred_element_type=F32)
        if radius is not None:
            qpos = n * bq + lax.rem(lax.broadcasted_iota(jnp.int32, (rows, 1), 0), bq)
            kpos = start(n) + lax.broadcasted_iota(jnp.int32, (1, W), 1)
            s = jnp.where(jnp.abs(qpos - kpos) <= radius, s, NEG)
        m = jnp.max(s, axis=1, keepdims=True)
        if sink is not None:
            sk = _sink_rows(sink_ref, b, G, bq, rows)
            m = jnp.maximum(m, sk)
        p = jnp.exp(s - m)
        l = jnp.sum(p, axis=1, keepdims=True)
        if sink is not None:
            l = l + jnp.exp(sk - m)
        o = jnp.dot(p.astype(BF16), v_ref[0], preferred_element_type=F32) / l
        o_ref[0] = o.reshape(G, bq, HD).astype(o_ref.dtype)
        lse_ref[0] = (m + jnp.log(l)).reshape(G, bq, 1)

    qspec = pl.BlockSpec((1, G, bq, HD), lambda b, n: (b, 0, n, 0))
    kspec = _win_specs(G, W, HD, start, False)
    ins, specs = [q, k, v], [qspec, kspec, kspec]
    if sink is not None:
        ins, specs = [sink] + ins, [pl.BlockSpec(memory_space=pltpu.SMEM)] + specs
    return _pc(body, name=name,
               out_shape=(jax.ShapeDtypeStruct((NB, G, L, HD), BF16), jax.ShapeDtypeStruct((NB, G, L, 1), F32)),
               grid=(NB, L // bq), in_specs=specs,
               out_specs=(qspec, pl.BlockSpec((1, G, bq, 1), lambda b, n: (b, 0, n, 0))),
               compiler_params=_cp("parallel", "parallel"))(*ins)


def _win_dq(q, k, v, do, lse, delta, *, radius=None, bq, name="win_dq"):
    NB, G, L, HD = q.shape
    Lk = k.shape[1]
    W, start = _window(Lk, bq, radius)
    rows = G * bq

    def body(q_ref, k_ref, v_ref, do_ref, lse_ref, dl_ref, dq_ref):
        n = pl.program_id(1)
        qv = q_ref[0].reshape(rows, HD) * ATTN_SCALE
        dov = do_ref[0].reshape(rows, HD)
        s = lax.dot_general(qv, k_ref[0], NT_DIMS, preferred_element_type=F32)
        p = jnp.exp(s - lse_ref[0].reshape(rows, 1))
        if radius is not None:
            qpos = n * bq + lax.rem(lax.broadcasted_iota(jnp.int32, (rows, 1), 0), bq)
            kpos = start(n) + lax.broadcasted_iota(jnp.int32, (1, W), 1)
            p = jnp.where(jnp.abs(qpos - kpos) <= radius, p, 0.0)
        dp = lax.dot_general(dov, v_ref[0], NT_DIMS, preferred_element_type=F32)
        ds = p * (dp - dl_ref[0].reshape(rows, 1))
        dq = jnp.dot(ds.astype(BF16), k_ref[0], preferred_element_type=F32) * ATTN_SCALE
        dq_ref[0] = dq.reshape(G, bq, HD)

    qspec = pl.BlockSpec((1, G, bq, HD), lambda b, n: (b, 0, n, 0))
    cspec = pl.BlockSpec((1, G, bq, 1), lambda b, n: (b, 0, n, 0))
    kspec = _win_specs(G, W, HD, start, False)
    return _pc(body, name=name, out_shape=jax.ShapeDtypeStruct((NB, G, L, HD), F32), grid=(NB, L // bq),
               in_specs=[qspec, kspec, kspec, qspec, cspec, cspec], out_specs=qspec,
               compiler_params=_cp("parallel", "parallel"))(q, k, v, do, lse, delta)


def _win_dkv(q, k, v, do, lse, delta, *, radius, bk, name="win_dkv"):
    NB, G, L, HD = q.shape
    Lk = k.shape[1]
    W, start = _window(L, bk, radius)
    rows = G * W

    def body(q_ref, k_ref, v_ref, do_ref, lse_ref, dl_ref, dk_ref, dv_ref):
        m = pl.program_id(1)
        qv = q_ref[0].reshape(rows, HD)
        dov = do_ref[0].reshape(rows, HD)
        s = lax.dot_general(qv * ATTN_SCALE, k_ref[0], NT_DIMS, preferred_element_type=F32)
        p = jnp.exp(s - lse_ref[0].reshape(rows, 1))
        qpos = start(m) + lax.rem(lax.broadcasted_iota(jnp.int32, (rows, 1), 0), W)
        kpos = m * bk + lax.broadcasted_iota(jnp.int32, (1, bk), 1)
        p = jnp.where(jnp.abs(qpos - kpos) <= radius, p, 0.0)
        dv_ref[0] = lax.dot_general(p.astype(BF16), dov, TN_DIMS, preferred_element_type=F32)
        dp = lax.dot_general(dov, v_ref[0], NT_DIMS, preferred_element_type=F32)
        ds = p * (dp - dl_ref[0].reshape(rows, 1))
        dk_ref[0] = lax.dot_general(ds.astype(BF16), qv, TN_DIMS, preferred_element_type=F32) * ATTN_SCALE

    qspec = _win_specs(G, W, HD, start, True)
    cspec = _win_specs(G, W, 1, start, True)
    kspec = pl.BlockSpec((1, bk, HD), lambda b, m: (b, m, 0))
    kv_shape = jax.ShapeDtypeStruct((NB, Lk, HD), F32)
    return _pc(body, name=name, out_shape=(kv_shape, kv_shape), grid=(NB, Lk // bk),
               in_specs=[qspec, kspec, kspec, qspec, cspec, cspec], out_specs=(kspec, kspec),
               compiler_params=_cp("parallel", "parallel"))(q, k, v, do, lse, delta)


def _attention(q, k, v, *, radius=None, sink=None, bq, tk=None, tag):
    if tk is None:
        return _win_fwd(q, k, v, radius=radius, sink=sink, bq=bq, name=f"win_fwd_{tag}")
    return _flash_fwd(q, k, v, radius=radius, sink=sink, bq=bq, tk=tk, name=f"flash_fwd_{tag}")


def _attention_bwd(q, k, v, o, lse, do, *, radius=None, sink=None, dlse=None, bq, tk=None, tag):
    if sink is not None:
        delta, ds = _attn_delta(do, o, lse=lse, sink=sink, name=f"attn_delta_{tag}")
        dsink = ds[:, :, 0, 0].reshape(-1)
    else:
        delta, dsink = _attn_delta(do, o, dlse=dlse, name=f"attn_delta_{tag}"), None
    if tk is not None and radius is None:
        return (*_flash_bwd_full(q, k, v, do, lse, delta, bq=bq, tk=tk, name=f"flash_bwd_{tag}"), dsink)
    if tk is None:
        dq = _win_dq(q, k, v, do, lse, delta, radius=radius, bq=bq, name=f"win_dq_{tag}")
    else:
        dq = _flash_dq(q, k, v, do, lse, delta, radius=radius, bq=bq, tk=tk, name=f"flash_dq_{tag}")
    if tk is None and radius is not None:
        dk, dv = _win_dkv(q, k, v, do, lse, delta, radius=radius, bk=bq, name=f"win_dkv_{tag}")
    else:
        dk, dv = _flash_dkv(q, k, v, do, lse, delta, radius=radius, bq=bq, tk=tk or k.shape[1], name=f"flash_dkv_{tag}")
    return dq, dk, dv, dsink


CHAIN_ROWS = 128


def _skewed(n, stages):
    for t in range(n + len(stages) - 1):
        for s, stage in enumerate(stages):
            if 0 <= t - s < n:
                stage(t - s)


def _chain_slices(G, bq):
    cr = min(CHAIN_ROWS, bq)
    per = bq // cr
    return [(c // per, slice((c % per) * cr, (c % per + 1) * cr), slice(c * cr, (c + 1) * cr)) for c in range(G * per)]


def _v_ones(v):
    return jnp.concatenate([v, jnp.ones(v.shape, v.dtype)], axis=1)


def _attn_fwd_full(q, k, v, *, bq, tk, name="attn_fwd_full"):
    NB, G, L, HD = q.shape
    Lk = k.shape[1]
    nq, nk = L // bq, Lk // tk
    rows = G * bq
    chains = _chain_slices(G, bq)

    def body(q_ref, k_ref, v_ref, o_ref, lse_ref, m_sc, acc_sc):
        j = pl.program_id(2)

        @pl.when(j == 0)
        def _():
            m_sc[...] = jnp.full_like(m_sc, NEG)
            acc_sc[...] = jnp.zeros_like(acc_sc)

        kk = k_ref[0]
        vv = _v_ones(v_ref[0])
        st = [dict() for _ in chains]

        def scores(c):
            g, hr, _ = chains[c]
            st[c]["s"] = lax.dot_general(q_ref[0, g, hr, :] * ATTN_SCALE, kk, NT_DIMS, preferred_element_type=F32)

        def softmax(c):
            sl = chains[c][2]
            m_prev = m_sc[sl]
            m_new = jnp.maximum(m_prev, jnp.max(st[c]["s"], axis=1, keepdims=True))
            st[c]["p"] = jnp.exp(st[c].pop("s") - m_new).astype(BF16)
            st[c]["alpha"] = jnp.exp(m_prev - m_new)
            m_sc[sl] = m_new

        def values(c):
            sl = chains[c][2]
            acc_sc[sl] = st[c].pop("alpha") * acc_sc[sl] + jnp.dot(st[c].pop("p"), vv, preferred_element_type=F32)

        _skewed(len(chains), (scores, softmax, values))

        @pl.when(j == nk - 1)
        def _():
            acc = acc_sc[...]
            l = acc[:, HD:HD + 1]
            o_ref[0] = (acc[:, :HD] / l).reshape(G, bq, HD).astype(o_ref.dtype)
            lse_ref[0] = (m_sc[...] + jnp.log(l)).reshape(G, bq, 1)

    qspec = pl.BlockSpec((1, G, bq, HD), lambda b, n, j: (b, 0, n, 0))
    kspec = pl.BlockSpec((1, tk, HD), lambda b, n, j: (b, j, 0))
    return _pc(body, name=name,
               out_shape=(jax.ShapeDtypeStruct((NB, G, L, HD), BF16), jax.ShapeDtypeStruct((NB, G, L, 1), F32)),
               grid=(NB, nq, nk), in_specs=[qspec, kspec, kspec],
               out_specs=(qspec, pl.BlockSpec((1, G, bq, 1), lambda b, n, j: (b, 0, n, 0))),
               scratch_shapes=[pltpu.VMEM((rows, 1), F32), pltpu.VMEM((rows, 2 * HD), F32)],
               compiler_params=_cp("parallel", "parallel", "arbitrary"))(q, k, v)


def _attn_bwd_full(q, k, v, do, lse, delta, *, bq, tk, name="attn_bwd_full"):
    NB, G, L, HD = q.shape
    Lk = k.shape[1]
    nq, nk = L // bq, Lk // tk
    rows = G * bq
    chains = _chain_slices(G, bq)

    def body(q_ref, k_ref, v_ref, do_ref, lse_ref, dl_ref, dqp_ref, dk_ref, dv_ref, dk_sc, dv_sc):
        n = pl.program_id(2)

        @pl.when(n == 0)
        def _():
            dk_sc[...] = jnp.zeros_like(dk_sc)
            dv_sc[...] = jnp.zeros_like(dv_sc)

        kk, vv = k_ref[0], v_ref[0]
        st = [dict() for _ in chains]

        def scores(c):
            g, hr, _ = chains[c]
            st[c]["q"] = q_ref[0, g, hr, :] * ATTN_SCALE
            st[c]["do"] = do_ref[0, g, hr, :]
            st[c]["s"] = lax.dot_general(st[c]["q"], kk, NT_DIMS, preferred_element_type=F32)
            st[c]["dp"] = lax.dot_general(st[c]["do"], vv, NT_DIMS, preferred_element_type=F32)

        def softmax(c):
            g, hr, _ = chains[c]
            p = jnp.exp(st[c].pop("s") - lse_ref[0, g, hr, :])
            st[c]["ds"] = (p * (st[c].pop("dp") - dl_ref[0, g, hr, :])).astype(BF16)
            st[c]["p"] = p.astype(BF16)

        def grads(c):
            g, hr, _ = chains[c]
            ds = st[c].pop("ds")
            dv_sc[...] += lax.dot_general(st[c].pop("p"), st[c].pop("do"), TN_DIMS, preferred_element_type=F32)
            dk_sc[...] += lax.dot_general(ds, st[c].pop("q"), TN_DIMS, preferred_element_type=F32)
            dqp_ref[0, 0, g, hr, :] = jnp.dot(ds, kk, preferred_element_type=F32) * ATTN_SCALE

        _skewed(len(chains), (scores, softmax, grads))

        @pl.when(n == nq - 1)
        def _():
            dk_ref[0] = dk_sc[...]
            dv_ref[0] = dv_sc[...]

    qspec = pl.BlockSpec((1, G, bq, HD), lambda b, m, n: (b, 0, n, 0))
    cspec = pl.BlockSpec((1, G, bq, 1), lambda b, m, n: (b, 0, n, 0))
    kspec = pl.BlockSpec((1, tk, HD), lambda b, m, n: (b, m, 0))
    kv_shape = jax.ShapeDtypeStruct((NB, Lk, HD), F32)
    dqp, dk, dv = _pc(body, name=name,
                      out_shape=(jax.ShapeDtypeStruct((nk, NB, G, L, HD), F32), kv_shape, kv_shape), grid=(NB, nk, nq),
                      in_specs=[qspec, kspec, kspec, qspec, cspec, cspec],
                      out_specs=(pl.BlockSpec((1, 1, G, bq, HD), lambda b, m, n: (m, b, 0, n, 0)), kspec, kspec),
                      scratch_shapes=[pltpu.VMEM((tk, HD), F32), pltpu.VMEM((tk, HD), F32)],
                      compiler_params=_cp("parallel", "parallel", "arbitrary"))(q, k, v, do, lse, delta)
    if nk == 1:
        return dqp[0], dk, dv
    bl = _row_tile(L, 512)

    def sum_body(p_ref, o_ref):
        acc = p_ref[0, 0]
        for j in range(1, nk):
            acc = acc + p_ref[j, 0]
        o_ref[0] = acc

    dq = _pc(sum_body, name=name + "_sum", out_shape=jax.ShapeDtypeStruct((NB, G, L, HD), F32), grid=(NB, L // bl),
             in_specs=[pl.BlockSpec((nk, 1, G, bl, HD), lambda b, n: (0, b, 0, n, 0))],
             out_specs=pl.BlockSpec((1, G, bl, HD), lambda b, n: (b, 0, n, 0)),
             compiler_params=_cp("parallel", "parallel"))(dqp)
    return dq, dk, dv


def _attn_fwd_win(q, k, v, *, radius, sink=None, bq, name="attn_fwd_win"):
    NB, G, L, HD = q.shape
    W, start = _window(k.shape[1], bq, radius)
    chains = _chain_slices(G, bq)

    def body(*refs):
        if sink is not None:
            sink_ref, *refs = refs
        q_ref, k_ref, v_ref, o_ref, lse_ref = refs
        b, n = pl.program_id(0), pl.program_id(1)
        kk = k_ref[0]
        vv = _v_ones(v_ref[0])
        st = [dict() for _ in chains]

        def scores(c):
            g, hr, _ = chains[c]
            s = lax.dot_general(q_ref[0, g, hr, :] * ATTN_SCALE, kk, NT_DIMS, preferred_element_type=F32)
            if radius is not None:
                qpos = n * bq + hr.start + lax.broadcasted_iota(jnp.int32, (hr.stop - hr.start, 1), 0)
                kpos = start(n) + lax.broadcasted_iota(jnp.int32, (1, W), 1)
                s = jnp.where(jnp.abs(qpos - kpos) <= radius, s, NEG)
            st[c]["s"] = s

        def softmax(c):
            g = chains[c][0]
            m = jnp.max(st[c]["s"], axis=1, keepdims=True)
            if sink is not None:
                m = jnp.maximum(m, sink_ref[b * G + g])
            st[c]["p"] = jnp.exp(st[c].pop("s") - m).astype(BF16)
            st[c]["m"] = m

        def values(c):
            g, hr, _ = chains[c]
            acc = jnp.dot(st[c].pop("p"), vv, preferred_element_type=F32)
            m = st[c].pop("m")
            l = acc[:, HD:HD + 1]
            if sink is not None:
                l = l + jnp.exp(sink_ref[b * G + g] - m)
            o_ref[0, g, hr, :] = (acc[:, :HD] / l).astype(o_ref.dtype)
            lse_ref[0, g, hr, :] = m + jnp.log(l)

        _skewed(len(chains), (scores, softmax, values))

    qspec = pl.BlockSpec((1, G, bq, HD), lambda b, n: (b, 0, n, 0))
    kspec = _win_specs(G, W, HD, start, False)
    ins, specs = [q, k, v], [qspec, kspec, kspec]
    if sink is not None:
        ins, specs = [sink] + ins, [pl.BlockSpec(memory_space=pltpu.SMEM)] + specs
    return _pc(body, name=name,
               out_shape=(jax.ShapeDtypeStruct((NB, G, L, HD), BF16), jax.ShapeDtypeStruct((NB, G, L, 1), F32)),
               grid=(NB, L // bq), in_specs=specs,
               out_specs=(qspec, pl.BlockSpec((1, G, bq, 1), lambda b, n: (b, 0, n, 0))),
               compiler_params=_cp("parallel", "parallel"))(*ins)


def _attn_dq_win(q, k, v, do, lse, delta, *, radius, bq, name="attn_dq_win"):
    NB, G, L, HD = q.shape
    W, start = _window(L, bq, radius)
    chains = _chain_slices(G, bq)

    def body(q_ref, k_ref, v_ref, do_ref, lse_ref, dl_ref, dq_ref):
        n = pl.program_id(1)
        kk, vv = k_ref[0], v_ref[0]
        kpos = start(n) + lax.broadcasted_iota(jnp.int32, (1, W), 1)
        st = [dict() for _ in chains]

        def scores(c):
            g, hr, _ = chains[c]
            st[c]["s"] = lax.dot_general(q_ref[0, g, hr, :] * ATTN_SCALE, kk, NT_DIMS, preferred_element_type=F32)
            st[c]["dp"] = lax.dot_general(do_ref[0, g, hr, :], vv, NT_DIMS, preferred_element_type=F32)

        def softmax(c):
            g, hr, _ = chains[c]
            qpos = n * bq + hr.start + lax.broadcasted_iota(jnp.int32, (hr.stop - hr.start, 1), 0)
            p = jnp.where(jnp.abs(qpos - kpos) <= radius, jnp.exp(st[c].pop("s") - lse_ref[0, g, hr, :]), 0.0)
            st[c]["ds"] = (p * (st[c].pop("dp") - dl_ref[0, g, hr, :])).astype(BF16)

        def grads(c):
            g, hr, _ = chains[c]
            dq_ref[0, g, hr, :] = jnp.dot(st[c].pop("ds"), kk, preferred_element_type=F32) * ATTN_SCALE

        _skewed(len(chains), (scores, softmax, grads))

    qspec = pl.BlockSpec((1, G, bq, HD), lambda b, n: (b, 0, n, 0))
    cspec = pl.BlockSpec((1, G, bq, 1), lambda b, n: (b, 0, n, 0))
    kspec = _win_specs(G, W, HD, start, False)
    return _pc(body, name=name, out_shape=jax.ShapeDtypeStruct((NB, G, L, HD), F32), grid=(NB, L // bq),
               in_specs=[qspec, kspec, kspec, qspec, cspec, cspec], out_specs=qspec,
               compiler_params=_cp("parallel", "parallel"))(q, k, v, do, lse, delta)


def _attn_dkv_win(q, k, v, do, lse, delta, *, radius, bk, name="attn_dkv_win"):
    NB, G, L, HD = q.shape
    W, start = _window(L, bk, radius)
    chains = _chain_slices(G, W)

    def body(q_ref, k_ref, v_ref, do_ref, lse_ref, dl_ref, dk_ref, dv_ref):
        m = pl.program_id(1)
        kk, vv = k_ref[0], v_ref[0]
        kpos = m * bk + lax.broadcasted_iota(jnp.int32, (1, bk), 1)
        st = [dict() for _ in chains]
        out = dict(dk=jnp.zeros((bk, HD), F32), dv=jnp.zeros((bk, HD), F32))

        def scores(c):
            g, hr, _ = chains[c]
            st[c]["q"] = q_ref[0, g, hr, :] * ATTN_SCALE
            st[c]["do"] = do_ref[0, g, hr, :]
            st[c]["s"] = lax.dot_general(st[c]["q"], kk, NT_DIMS, preferred_element_type=F32)
            st[c]["dp"] = lax.dot_general(st[c]["do"], vv, NT_DIMS, preferred_element_type=F32)

        def softmax(c):
            g, hr, _ = chains[c]
            qpos = start(m) + hr.start + lax.broadcasted_iota(jnp.int32, (hr.stop - hr.start, 1), 0)
            p = jnp.where(jnp.abs(qpos - kpos) <= radius, jnp.exp(st[c].pop("s") - lse_ref[0, g, hr, :]), 0.0)
            st[c]["ds"] = (p * (st[c].pop("dp") - dl_ref[0, g, hr, :])).astype(BF16)
            st[c]["p"] = p.astype(BF16)

        def grads(c):
            out["dv"] = out["dv"] + lax.dot_general(st[c].pop("p"), st[c].pop("do"), TN_DIMS, preferred_element_type=F32)
            out["dk"] = out["dk"] + lax.dot_general(st[c].pop("ds"), st[c].pop("q"), TN_DIMS, preferred_element_type=F32)

        _skewed(len(chains), (scores, softmax, grads))
        dk_ref[0] = out["dk"]
        dv_ref[0] = out["dv"]

    qspec = _win_specs(G, W, HD, start, True)
    cspec = _win_specs(G, W, 1, start, True)
    kspec = pl.BlockSpec((1, bk, HD), lambda b, m: (b, m, 0))
    kv_shape = jax.ShapeDtypeStruct((NB, L, HD), F32)
    return _pc(body, name=name, out_shape=(kv_shape, kv_shape), grid=(NB, L // bk),
               in_specs=[qspec, kspec, kspec, qspec, cspec, cspec], out_specs=(kspec, kspec),
               compiler_params=_cp("parallel", "parallel"))(q, k, v, do, lse, delta)


def _attn(q, k, v, *, radius=None, sink=None, bq, tk=None, tag):
    if radius is None and tk < k.shape[1]:
        return _attn_fwd_full(q, k, v, bq=bq, tk=tk, name=f"attn_fwd_{tag}")
    return _attn_fwd_win(q, k, v, radius=radius, sink=sink, bq=bq, name=f"attn_fwd_{tag}")


def _attn_bwd(q, k, v, o, lse, do, *, radius=None, sink=None, dlse=None, bq, tk=None, tag):
    if sink is not None:
        delta, ds = _attn_delta(do, o, lse=lse, sink=sink, name=f"attn_delta_{tag}")
        dsink = ds[:, :, 0, 0].reshape(-1)
    else:
        delta, dsink = _attn_delta(do, o, dlse=dlse, name=f"attn_delta_{tag}"), None
    if radius is None:
        return (*_attn_bwd_full(q, k, v, do, lse, delta, bq=bq, tk=tk, name=f"attn_bwd_{tag}"), dsink)
    dq = _attn_dq_win(q, k, v, do, lse, delta, radius=radius, bq=bq, name=f"attn_dq_{tag}")
    dk, dv = _attn_dkv_win(q, k, v, do, lse, delta, radius=radius, bk=bq, name=f"attn_dkv_{tag}")
    return dq, dk, dv, dsink


def _combine_fwd(o, lse, name="combine_fwd"):
    H, S, HD = o.shape
    tm = _row_tile(S, 512)

    def body(o_ref, lse_ref, t_ref):
        for g in range(GQA_GROUP):
            hs = [kv * GQA_GROUP + g for kv in range(N_KV_HEADS)]
            ls = [lse_ref[h] for h in hs]
            mx = functools.reduce(jnp.maximum, ls)
            es = [jnp.exp(l - mx) for l in ls]
            den = functools.reduce(jnp.add, es)
            for h, e in zip(hs, es):
                t_ref[h] = (o_ref[h].astype(F32) * (e / den)).astype(t_ref.dtype)

    blk = pl.BlockSpec((H, tm, HD), lambda i: (0, i, 0))
    col = pl.BlockSpec((H, tm, 1), lambda i: (0, i, 0))
    return _pc(body, name=name, out_shape=jax.ShapeDtypeStruct((H, S, HD), BF16), grid=(S // tm,),
               in_specs=[blk, col], out_specs=blk, compiler_params=_cp("parallel"))(o, lse)


def _combine_bwd(dt, o, lse, name="combine_bwd"):
    H, S, HD = o.shape
    tm = _row_tile(S, 512)

    def body(dt_ref, o_ref, lse_ref, do_ref, dlse_ref):
        for g in range(GQA_GROUP):
            hs = [kv * GQA_GROUP + g for kv in range(N_KV_HEADS)]
            ls = [lse_ref[h] for h in hs]
            mx = functools.reduce(jnp.maximum, ls)
            es = [jnp.exp(l - mx) for l in ls]
            den = functools.reduce(jnp.add, es)
            al = [e / den for e in es]
            dts = [dt_ref[h].astype(F32) for h in hs]
            da = [jnp.sum(d * o_ref[h].astype(F32), axis=-1, keepdims=True) for h, d in zip(hs, dts)]
            dot = functools.reduce(jnp.add, [a * d for a, d in zip(al, da)])
            for h, a, d, dd in zip(hs, al, da, dts):
                do_ref[h] = (dd * a).astype(do_ref.dtype)
                dlse_ref[h] = a * (d - dot)

    blk = pl.BlockSpec((H, tm, HD), lambda i: (0, i, 0))
    col = pl.BlockSpec((H, tm, 1), lambda i: (0, i, 0))
    return _pc(body, name=name,
               out_shape=(jax.ShapeDtypeStruct((H, S, HD), BF16), jax.ShapeDtypeStruct((H, S, 1), F32)),
               grid=(S // tm,), in_specs=[blk, blk, col], out_specs=(blk, col), compiler_params=_cp("parallel"))(dt, o, lse)


def _position():
    x, y, c = lax.axis_index("x"), lax.axis_index("y"), lax.axis_index("c")
    return x, y, c


def _peer(pos, k):
    x, y, c = pos
    return (1 - x if k & 4 else x, 1 - y if k & 2 else y, 1 - c if k & 1 else c)


def _linear(p):
    return 4 * p[0] + 2 * p[1] + p[2]


def _exchange(buf, gather, name):
    out_shape = ((N_DEV,) + buf.shape) if gather else buf.shape

    def body(s_ref, r_ref, send_sems, recv_sems, local_sem):
        pos = _position()
        me = _linear(pos)
        own = pltpu.make_async_copy(s_ref if gather else s_ref.at[me], r_ref.at[me], local_sem)
        own.start()
        sends = []
        for k in range(1, N_DEV):
            peer = _peer(pos, k)
            src = s_ref if gather else s_ref.at[_linear(peer)]
            cp = pltpu.make_async_remote_copy(src_ref=src, dst_ref=r_ref.at[me], send_sem=send_sems.at[k - 1],
                                              recv_sem=recv_sems.at[k - 1], device_id=peer, device_id_type=MESH)
            cp.start()
            sends.append(cp)
        for k in range(1, N_DEV):
            peer = _peer(pos, k)
            slot = r_ref.at[_linear(peer)]
            pltpu.make_async_remote_copy(src_ref=slot, dst_ref=slot, send_sem=send_sems.at[k - 1],
                                         recv_sem=recv_sems.at[k - 1], device_id=peer, device_id_type=MESH).wait_recv()
        for cp in sends:
            cp.wait_send()
        own.wait()

    hbm = pl.BlockSpec(memory_space=pltpu.HBM)
    return _pc(body, name=name, out_shape=jax.ShapeDtypeStruct(out_shape, buf.dtype), in_specs=[hbm], out_specs=hbm,
               scratch_shapes=[pltpu.SemaphoreType.DMA((N_DEV - 1,)), pltpu.SemaphoreType.DMA((N_DEV - 1,)),
                               pltpu.SemaphoreType.DMA])(buf)


def _reduce_adamw(recv, w, m, v, name):
    _, R, C = recv.shape
    tr = _row_tile(R, 512)

    def body(r_ref, w_ref, m_ref, v_ref, g_ref, d_ref, nm_ref, nv_ref):
        g = r_ref[0].astype(F32)
        for j in range(1, N_DEV):
            g = g + r_ref[j].astype(F32)
        g_ref[...] = g
        nm = ADAM_B1 * m_ref[...] + (1.0 - ADAM_B1) * g
        nv = ADAM_B2 * v_ref[...] + (1.0 - ADAM_B2) * jnp.square(g)
        m_hat = nm / (1.0 - ADAM_B1 ** ADAM_STEP)
        v_hat = nv / (1.0 - ADAM_B2 ** ADAM_STEP)
        d_ref[...] = -ADAM_LR * (m_hat / (jnp.sqrt(v_hat) + ADAM_EPS) + ADAM_WD * w_ref[...])
        nm_ref[...] = nm
        nv_ref[...] = nv

    row = pl.BlockSpec((tr, C), lambda i: (i, 0))
    out = jax.ShapeDtypeStruct((R, C), F32)
    return _pc(body, name=name, out_shape=(out, out, out, out), grid=(R // tr,),
               in_specs=[pl.BlockSpec((N_DEV, tr, C), lambda i: (0, i, 0)), row, row, row],
               out_specs=(row, row, row, row), compiler_params=_cp("parallel"))(recv, w, m, v)


BIG = (("w_in", 2), ("w_mem_kv", 1), ("w_o", 1), ("w_gate_up", 2), ("w_down", 1))
SMALL = ("mem_norm_g", "g_mix_pre", "g_mix_post", "attn_sink", "qk_norm_g", "g_ffn_pre", "g_ffn_post")
SMALL_W = 1024


def _pack_local(shards, dtype):
    return jnp.concatenate([s.astype(dtype).reshape(-1, LANES) for s in shards], axis=0)


def _unpack_local(flat, shapes):
    out, r = [], 0
    for shp in shapes:
        n = shp[0] * shp[1] * shp[2] // LANES
        out.append(flat[r:r + n].reshape(shp))
        r += n
    return out


def _unpack_gathered(g, shapes):
    out, r = [], 0
    for (name, dim), shp in zip(BIG, shapes):
        n = shp[0] * shp[1] * shp[2] // LANES
        t = g[:, r:r + n].reshape((N_DEV,) + tuple(shp))
        if dim == 2:
            t = t.transpose(1, 2, 0, 3).reshape(shp[0], shp[1], N_DEV * shp[2])
        else:
            t = t.transpose(1, 0, 2, 3).reshape(shp[0], N_DEV * shp[1], shp[2])
        out.append(t)
        r += n
    return out


def _pack_for_scatter(full, shapes, dtype):
    parts = []
    for (name, dim), shp, t in zip(BIG, shapes, full):
        if dim == 2:
            t = t.reshape(shp[0], shp[1], N_DEV, shp[2]).transpose(2, 0, 1, 3)
        else:
            t = t.reshape(shp[0], N_DEV, shp[1], shp[2]).transpose(1, 0, 2, 3)
        parts.append(t.astype(dtype).reshape(N_DEV, -1, LANES))
    return jnp.concatenate(parts, axis=1)


def _pack_small(arrs):
    flat = jnp.concatenate([a.reshape(-1) for a in arrs])
    pad = (-flat.shape[0]) % (8 * SMALL_W)
    return jnp.pad(flat, (0, pad)).reshape(-1, SMALL_W)


def _unpack_small(flat, shapes):
    flat = flat.reshape(-1)
    out, r = [], 0
    for shp in shapes:
        n = 1
        for d in shp:
            n *= d
        out.append(flat[r:r + n].reshape(shp))
        r += n
    return out


def _heads(t, nb, g):
    S = t.shape[0]
    return t.reshape(S, nb, g, HEAD_DIM).transpose(1, 2, 0, 3)


def _unheads(t):
    nb, g, S, hd = t.shape
    return t.transpose(2, 0, 1, 3).reshape(S, nb * g * hd)


def _dilate(t, dil):
    S = t.shape[0]
    g = t.shape[1] // HEAD_DIM
    return t.reshape(S // dil, dil, g, HEAD_DIM).transpose(1, 2, 0, 3)


def _undilate(t):
    dil, g, L, w = t.shape
    return t.transpose(1, 2, 0, 3).reshape(g, L * dil, w)


FULL_BQ = 256
FULL_TK_FWD = 4096
FULL_TK_BWD = 2048


def _mixer_fwd(kind, pr, sink, li):
    S = pr.shape[0]
    if kind == 0:
        q = _heads(pr[:, :Q_W], N_KV_HEADS, GQA_GROUP)
        k = _heads(pr[:, Q_W:QK_W], N_KV_HEADS, 1)[:, 0]
        v = _heads(pr[:, QK_W:QK_W + KV_W], N_KV_HEADS, 1)[:, 0]
        o, lse = _attn(q, k, v, radius=A_RADIUS, sink=sink, bq=min(256, S), tag=f"a{li}")
        return _unheads(o), (q, k, v, o, lse)
    if kind == 1:
        q = _heads(pr[:, :Q_W], N_KV_HEADS, GQA_GROUP)
        k = _heads(pr[:, Q_W:QK_W], N_KV_HEADS, 1)[:, 0]
        v = _heads(pr[:, QK_W:QK_W + KV_W], N_KV_HEADS, 1)[:, 0]
        o, lse = _attn(q, k, v, bq=min(FULL_BQ, S), tk=min(FULL_TK_FWD, S), tag=f"b{li}")
        return _unheads(o), (q, k, v, o, lse)
    saved, outs, lses = [], [], []
    for g, (window, dil) in enumerate(C_GROUPS):
        q = _dilate(pr[:, g * GQA_GROUP * HEAD_DIM:(g + 1) * GQA_GROUP * HEAD_DIM], dil)
        k = _dilate(pr[:, Q_W + g * HEAD_DIM:Q_W + (g + 1) * HEAD_DIM], dil)[:, 0]
        v = _dilate(pr[:, QK_W + g * HEAD_DIM:QK_W + (g + 1) * HEAD_DIM], dil)[:, 0]
        o, lse = _attn(q, k, v, radius=window // (2 * dil), bq=min(256, S // dil), tag=f"c{li}g{g}")
        saved.append((q, k, v, o, lse))
        outs.append(_undilate(o))
        lses.append(_undilate(lse))
    o_all, lse_all = jnp.concatenate(outs, 0), jnp.concatenate(lses, 0)
    tok = _combine_fwd(o_all, lse_all, name=f"combine_fwd_{li}")
    return tok.transpose(1, 0, 2).reshape(S, Q_W), (saved, o_all, lse_all)


def _mixer_bwd(kind, dtok, saved, sink, li):
    S = dtok.shape[0]
    if kind in (0, 1):
        q, k, v, o, lse = saved
        do = _heads(dtok, N_KV_HEADS, GQA_GROUP)
        if kind == 0:
            dq, dk, dv, dsink = _attn_bwd(q, k, v, o, lse, do, radius=A_RADIUS, sink=sink, bq=min(256, S), tag=f"a{li}")
        else:
            dq, dk, dv, dsink = _attn_bwd(q, k, v, o, lse, do, bq=min(FULL_BQ, S), tk=min(FULL_TK_BWD, S), tag=f"b{li}")
        return _unheads(dq), _unheads(dk[:, None]), _unheads(dv[:, None]), dsink
    per_group, o_all, lse_all = saved
    dt = dtok.reshape(S, N_TOK_HEADS, HEAD_DIM).transpose(1, 0, 2)
    do_all, dlse_all = _combine_bwd(dt, o_all, lse_all, name=f"combine_bwd_{li}")
    dqs, dks, dvs = [], [], []
    for g, (window, dil) in enumerate(C_GROUPS):
        q, k, v, o, lse = per_group[g]
        L = S // dil
        hs = slice(g * GQA_GROUP, (g + 1) * GQA_GROUP)
        do = do_all[hs].reshape(GQA_GROUP, L, dil, HEAD_DIM).transpose(2, 0, 1, 3)
        dlse = dlse_all[hs].reshape(GQA_GROUP, L, dil, 1).transpose(2, 0, 1, 3)
        dq, dk, dv, _ = _attn_bwd(q, k, v, o, lse, do, radius=window // (2 * dil), dlse=dlse, bq=min(256, L),
                                       tag=f"c{li}g{g}")
        dqs.append(dq.transpose(2, 0, 1, 3).reshape(S, GQA_GROUP * HEAD_DIM))
        dks.append(dk.transpose(1, 0, 2).reshape(S, HEAD_DIM))
        dvs.append(dv.transpose(1, 0, 2).reshape(S, HEAD_DIM))
    return jnp.concatenate(dqs, 1), jnp.concatenate(dks, 1), jnp.concatenate(dvs, 1), None


def kernel(x, mem, mem_norm_g, w_in, w_mem_kv, w_o, g_mix_pre, g_mix_post, attn_sink, qk_norm_g, w_gate_up, w_down, g_ffn_pre, g_ffn_post, loss_target, m_mem_norm_g, m_w_in, m_w_mem_kv, m_w_o, m_g_mix_pre, m_g_mix_post, m_attn_sink, m_qk_norm_g, m_w_gate_up, m_w_down, m_g_ffn_pre, m_g_ffn_post, v_mem_norm_g, v_w_in, v_w_mem_kv, v_w_o, v_g_mix_pre, v_g_mix_post, v_attn_sink, v_qk_norm_g, v_w_gate_up, v_w_down, v_g_ffn_pre, v_g_ffn_post):
    given = dict(locals())
    depth = w_in.shape[0]
    S, D = x.shape[1], x.shape[2]
    x0 = x[0]
    big_local = [given[n] for n, _ in BIG]
    big_shapes = [t.shape for t in big_local]

    W_in, W_mkv, W_o, W_gu, W_dn = _unpack_gathered(_exchange(_pack_local(big_local, BF16), True, "gather_weights"), big_shapes)

    tabs = _rope_tables(S)
    mem_n = _rms_fwd(mem[0], mem_norm_g[None], BF16, name="rms_mem")

    saved = []
    xc = x0
    for i in range(depth):
        kind = i % N_MIXERS
        (tab, shift) = tabs[1] if kind == 1 else tabs[0]
        sink = attn_sink[i // N_MIXERS] if kind == 0 else None
        qk_gain = _qk_gain_row(qk_norm_g[i // N_MIXERS]) if kind == 1 else None
        h, proj = _mm(xc, W_in[i], F32, pre_g=g_mix_pre[i][None], name=f"mm_in_{i}")
        pr = _headprep_fwd(proj, tab, shift, qk_gain, name=f"headprep_fwd_{i}")
        tok, mix_saved = _mixer_fwd(kind, pr, sink, i)
        (mkv,) = _mm(mem_n, W_mkv[i], BF16, name=f"mm_mkv_{i}")
        qm = _heads(pr[:, QK_W + KV_W:], N_MEM_HEADS, 1)
        km = _heads(mkv[:, :QM_W], N_MEM_HEADS, 1)[:, 0]
        vm = _heads(mkv[:, QM_W:], N_MEM_HEADS, 1)[:, 0]
        mo, mlse = _attn(qm, km, vm, bq=min(1024, S), tk=km.shape[1], tag=f"m{i}")
        cat = jnp.concatenate([tok, _unheads(mo)], axis=1)
        o, x1 = _mm(cat, W_o[i], F32, post=(g_mix_post[i][None], xc), name=f"mm_o_{i}")
        h2, gu, act = _mm(x1, W_gu[i], BF16, pre_g=g_ffn_pre[i][None], swiglu=True, name=f"mm_gu_{i}")
        f, x2 = _mm(act, W_dn[i], F32, post=(g_ffn_post[i][None], x1), name=f"mm_dn_{i}")
        saved.append(dict(x=xc, h=h, proj=proj, mix=mix_saved, qm=qm, km=km, vm=vm, mo=mo, mlse=mlse, cat=cat, o=o,
                          x1=x1, h2=h2, gu=gu, act=act, f=f))
        xc = x2

    dy, sq = _loss_head(xc, loss_target[0], name="loss_head")
    loss = lax.psum(sq[0, 0] * (0.5 / D), ("x", "y", "c"))

    zero_row = jnp.zeros((1, D), F32)
    grads = {n: [None] * depth for n in ("w_in", "w_mem_kv", "w_o", "w_gate_up", "w_down", "g_mix_pre", "g_mix_post",
                                         "g_ffn_pre", "g_ffn_post")}
    d_sink = [jnp.zeros((N_TOK_HEADS,), F32) for _ in range(attn_sink.shape[0])]
    d_qkg = [jnp.zeros((2, HEAD_DIM), F32) for _ in range(qk_norm_g.shape[0])]
    dmem_n = jnp.zeros((mem.shape[1], D), F32)
    dx = dy
    for i in reversed(range(depth)):
        kind = i % N_MIXERS
        sv = saved[i]
        (tab, shift) = tabs[1] if kind == 1 else tabs[0]
        sink = attn_sink[i // N_MIXERS] if kind == 0 else None
        df, dg = _rms_bwd(sv["f"], g_ffn_post[i][None], dx, BF16, name=f"rmsb_fpost_{i}")
        grads["g_ffn_post"][i] = dg[0]
        (dgu,) = _mm(df, W_dn[i], BF16, nt=True, gu=sv["gu"], name=f"mmb_dn_{i}")
        grads["w_down"][i] = _mm_tn(sv["act"], df, name=f"mmw_dn_{i}")
        (dh2,) = _mm(dgu, W_gu[i], F32, nt=True, name=f"mmb_gu_{i}")
        grads["w_gate_up"][i] = _mm_tn(sv["h2"], dgu, name=f"mmw_gu_{i}")
        dx1, dg = _rms_bwd(sv["x1"], g_ffn_pre[i][None], dh2, F32, res=dx, name=f"rmsb_fpre_{i}")
        grads["g_ffn_pre"][i] = dg[0]
        do, dg = _rms_bwd(sv["o"], g_mix_post[i][None], dx1, BF16, name=f"rmsb_post_{i}")
        grads["g_mix_post"][i] = dg[0]
        (dcat,) = _mm(do, W_o[i], BF16, nt=True, name=f"mmb_o_{i}")
        grads["w_o"][i] = _mm_tn(sv["cat"], do, name=f"mmw_o_{i}")
        dmo = _heads(dcat[:, Q_W:], N_MEM_HEADS, 1)
        dqm, dkm, dvm, _ = _attn_bwd(sv["qm"], sv["km"], sv["vm"], sv["mo"], sv["mlse"], dmo, bq=min(1024, S),
                                     tk=sv["km"].shape[1], tag=f"m{i}")
        dmkv = jnp.concatenate([_unheads(dkm[:, None]), _unheads(dvm[:, None])], axis=1).astype(BF16)
        grads["w_mem_kv"][i] = _mm_tn(mem_n, dmkv, name=f"mmw_mkv_{i}")
        dmem_n = dmem_n + _mm(dmkv, W_mkv[i], F32, nt=True, name=f"mmb_mkv_{i}")[0]
        dq, dk, dv, dsink = _mixer_bwd(kind, dcat[:, :Q_W], sv["mix"], sink, i)
        if dsink is not None:
            d_sink[i // N_MIXERS] = dsink
        dpr = jnp.concatenate([dq, dk, dv, _unheads(dqm)], axis=1)
        if kind == 1:
            dproj, dgc = _headprep_bwd(dpr, tab, shift, sv["proj"], _qk_gain_row(qk_norm_g[i // N_MIXERS]),
                                       name=f"headprep_bwd_{i}")
            d_qkg[i // N_MIXERS] = jnp.stack([dgc[0, :Q_W].reshape(N_TOK_HEADS, HEAD_DIM).sum(0),
                                              dgc[0, Q_W:QK_W].reshape(N_KV_HEADS, HEAD_DIM).sum(0)])
        else:
            dproj = _headprep_bwd(dpr, tab, shift, name=f"headprep_bwd_{i}")
        (dh,) = _mm(dproj, W_in[i], F32, nt=True, name=f"mmb_in_{i}")
        grads["w_in"][i] = _mm_tn(sv["h"], dproj, name=f"mmw_in_{i}")
        dx, dg = _rms_bwd(sv["x"], g_mix_pre[i][None], dh, F32, res=dx1, name=f"rmsb_pre_{i}")
        grads["g_mix_pre"][i] = dg[0]
    _, dg_mem = _rms_bwd(mem[0], mem_norm_g[None], dmem_n, BF16, name="rmsb_mem")

    full = [jnp.stack(grads[n]) for n, _ in BIG]
    recv = _exchange(_pack_for_scatter(full, big_shapes, BF16), False, "scatter_grads")
    packed = lambda pre: _pack_local([given[pre + n] for n, _ in BIG], F32)
    gb, db, mb, vb = _reduce_adamw(recv, packed(""), packed("m_"), packed("v_"), name="adamw_big")

    small_grads = dict(mem_norm_g=dg_mem[0], g_mix_pre=jnp.stack(grads["g_mix_pre"]), g_mix_post=jnp.stack(grads["g_mix_post"]),
                       attn_sink=jnp.stack(d_sink), qk_norm_g=jnp.stack(d_qkg), g_ffn_pre=jnp.stack(grads["g_ffn_pre"]),
                       g_ffn_post=jnp.stack(grads["g_ffn_post"]))
    sg = _pack_small([small_grads[n] for n in SMALL])
    srecv = _exchange(sg, True, "gather_small_grads")
    spacked = lambda pre: _pack_small([given[pre + n] for n in SMALL])
    gs, ds, ms, vs = _reduce_adamw(srecv, spacked(""), spacked("m_"), spacked("v_"), name="adamw_small")

    out = {}
    for pre, fb, fs in (("grad_", gb, gs), ("delta_", db, ds), ("new_m_", mb, ms), ("new_v_", vb, vs)):
        for (n, _), t in zip(BIG, _unpack_local(fb, big_shapes)):
            out[pre + n] = t
        for n, t in zip(SMALL, _unpack_small(fs, [given[n].shape for n in SMALL])):
            out[pre + n] = t
    order = ("mem_norm_g", "w_in", "w_mem_kv", "w_o", "g_mix_pre", "g_mix_post", "attn_sink", "qk_norm_g", "w_gate_up",
             "w_down", "g_ffn_pre", "g_ffn_post")
    return (loss, dx[None], *[out[p + n] for p in ("grad_", "delta_", "new_m_", "new_v_") for n in order])
```

```python
import functools

import jax
import jax.numpy as jnp
from jax import lax
from jax.experimental import pallas as pl
from jax.experimental.pallas import tpu as pltpu

F32 = jnp.float32
BF16 = jnp.bfloat16

HEAD_DIM = 64
N_TOK_HEADS = 12
N_KV_HEADS = 3
GQA_GROUP = 4
N_MEM_HEADS = 4
Q_W = N_TOK_HEADS * HEAD_DIM
KV_W = N_KV_HEADS * HEAD_DIM
QM_W = N_MEM_HEADS * HEAD_DIM
QK_W = Q_W + KV_W
IN_W = Q_W + 2 * KV_W + QM_W
N_HEAD_SLOTS = IN_W // HEAD_DIM
N_MIXERS = 3
A_RADIUS = 128
C_GROUPS = ((128, 1), (512, 4), (2048, 16))
ROPE_THETA = 500000.0
ROPE_DIMS = HEAD_DIM // 4
AXIAL_THETA = 10000.0
GRID_W = 64
EPS = 1e-6
ATTN_SCALE = HEAD_DIM ** -0.5
NEG = -1e30

ADAM_LR = 0.001
ADAM_B1 = 0.9
ADAM_B2 = 0.999
ADAM_EPS = 1e-08
ADAM_WD = 0.01
ADAM_STEP = 10

N_DEV = 8
LANES = 128
VMEM_LIMIT = 56 * 1024 * 1024
MESH = pl.DeviceIdType.MESH
NT_DIMS = (((1,), (1,)), ((), ()))
TN_DIMS = (((0,), (0,)), ((), ()))


def _pc(body, **kw):
    return pl.pallas_call(body, **kw)


def _cp(*sem):
    return pltpu.CompilerParams(dimension_semantics=sem, vmem_limit_bytes=VMEM_LIMIT)


def _row_tile(m, cap=512):
    t = cap
    while m % t:
        t //= 2
    return t


def _rms_fwd(x, g, out_dtype, res=None, name="rms_fwd"):
    M, D = x.shape
    tm = _row_tile(M)

    def body(*refs):
        if res is None:
            x_ref, g_ref, o_ref = refs
        else:
            x_ref, g_ref, r_ref, o_ref = refs
        xv = x_ref[...]
        y = xv * lax.rsqrt(jnp.mean(xv * xv, axis=-1, keepdims=True) + EPS) * g_ref[...]
        if res is not None:
            y = r_ref[...] + y
        o_ref[...] = y.astype(o_ref.dtype)

    row = pl.BlockSpec((tm, D), lambda i: (i, 0))
    vec = pl.BlockSpec((1, D), lambda i: (0, 0))
    ins = [x, g] + ([] if res is None else [res])
    specs = [row, vec] + ([] if res is None else [row])
    return _pc(body, name=name, out_shape=jax.ShapeDtypeStruct((M, D), out_dtype), grid=(M // tm,),
               in_specs=specs, out_specs=row, compiler_params=_cp("parallel"))(*ins)


def _rms_bwd(x, g, dy, out_dtype, res=None, name="rms_bwd"):
    M, D = x.shape
    tm = _row_tile(M)

    def body(*refs):
        if res is None:
            x_ref, g_ref, dy_ref, dx_ref, dg_ref = refs
        else:
            x_ref, g_ref, dy_ref, r_ref, dx_ref, dg_ref = refs
        xv = x_ref[...]
        r = lax.rsqrt(jnp.mean(xv * xv, axis=-1, keepdims=True) + EPS)
        xh = xv * r
        d = dy_ref[...].astype(F32)
        dxh = d * g_ref[...]
        dx = r * (dxh - xh * jnp.mean(dxh * xh, axis=-1, keepdims=True))
        if res is not None:
            dx = r_ref[...] + dx
        dx_ref[...] = dx.astype(dx_ref.dtype)

        @pl.when(pl.program_id(0) == 0)
        def _():
            dg_ref[...] = jnp.zeros_like(dg_ref)

        dg_ref[...] += jnp.sum(d * xh, axis=0, keepdims=True)

    row = pl.BlockSpec((tm, D), lambda i: (i, 0))
    vec = pl.BlockSpec((1, D), lambda i: (0, 0))
    ins = [x, g, dy] + ([] if res is None else [res])
    specs = [row, vec, row] + ([] if res is None else [row])
    return _pc(body, name=name,
               out_shape=(jax.ShapeDtypeStruct((M, D), out_dtype), jax.ShapeDtypeStruct((1, D), F32)),
               grid=(M // tm,), in_specs=specs, out_specs=(row, vec), compiler_params=_cp("arbitrary"))(*ins)


def _swiglu_bwd(gu, da, name="swiglu_bwd"):
    M, F2 = gu.shape
    F = F2 // 2
    tm = _row_tile(M, 256)

    def body(gu_ref, da_ref, o_ref):
        g = gu_ref[:, :F].astype(F32)
        u = gu_ref[:, F:].astype(F32)
        d = da_ref[...].astype(F32)
        sig = 1.0 / (1.0 + jnp.exp(-g))
        o_ref[:, :F] = (d * u * (sig * (1.0 + g * (1.0 - sig)))).astype(o_ref.dtype)
        o_ref[:, F:] = (d * (g * sig)).astype(o_ref.dtype)

    return _pc(body, name=name, out_shape=jax.ShapeDtypeStruct((M, F2), BF16), grid=(M // tm,),
               in_specs=[pl.BlockSpec((tm, F2), lambda i: (i, 0)), pl.BlockSpec((tm, F), lambda i: (i, 0))],
               out_specs=pl.BlockSpec((tm, F2), lambda i: (i, 0)), compiler_params=_cp("parallel"))(gu, da)


def _loss_head(y, t, name="loss_head"):
    M, D = y.shape
    tm = _row_tile(M)

    def body(y_ref, t_ref, dy_ref, acc_ref):
        e = y_ref[...] - t_ref[...]
        dy_ref[...] = e * (1.0 / D)

        @pl.when(pl.program_id(0) == 0)
        def _():
            acc_ref[...] = jnp.zeros_like(acc_ref)

        acc_ref[...] += jnp.sum(e * e)

    row = pl.BlockSpec((tm, D), lambda i: (i, 0))
    return _pc(body, name=name,
               out_shape=(jax.ShapeDtypeStruct((M, D), F32), jax.ShapeDtypeStruct((8, LANES), F32)),
               grid=(M // tm,), in_specs=[row, row],
               out_specs=(row, pl.BlockSpec((8, LANES), lambda i: (0, 0))), compiler_params=_cp("arbitrary"))(y, t)


def _mm(a, w, out_dtype, nt=False, pre_g=None, swiglu=False, post=None, gu=None, carry=None, tm=256, name="mm"):
    M, K = a.shape
    N = w.shape[0] if nt else w.shape[1]
    tm = _row_tile(M, tm)

    def body(*refs):
        refs = list(refs)
        a_ref, w_ref = refs.pop(0), refs.pop(0)
        pg_ref = refs.pop(0) if pre_g is not None else None
        g_ref, r_ref = (refs.pop(0), refs.pop(0)) if post is not None else (None, None)
        gu_ref = refs.pop(0) if gu is not None else None
        lhs = a_ref[...]
        if pre_g is not None:
            lhs = (lhs * lax.rsqrt(jnp.mean(lhs * lhs, axis=-1, keepdims=True) + EPS) * pg_ref[...]).astype(BF16)
            refs.pop(0)[...] = lhs
        if nt:
            acc = lax.dot_general(lhs, w_ref[...], NT_DIMS, preferred_element_type=F32)
        else:
            acc = jnp.dot(lhs, w_ref[...], preferred_element_type=F32)
        o_ref = refs.pop(0)
        if gu is None:
            o_ref[...] = acc.astype(o_ref.dtype)
        else:
            gate = gu_ref[:, :N].astype(F32)
            sig = 1.0 / (1.0 + jnp.exp(-gate))
            o_ref[:, :N] = (acc * gu_ref[:, N:].astype(F32) * (sig * (1.0 + gate * (1.0 - sig)))).astype(o_ref.dtype)
            o_ref[:, N:] = (acc * (gate * sig)).astype(o_ref.dtype)
        if swiglu:
            gate = acc[:, : N // 2]
            refs.pop(0)[...] = (gate * (1.0 / (1.0 + jnp.exp(-gate))) * acc[:, N // 2:]).astype(BF16)
        if post is not None:
            y = acc * lax.rsqrt(jnp.mean(acc * acc, axis=-1, keepdims=True) + EPS) * g_ref[...]
            refs.pop(0)[...] = r_ref[...] + y

    row = lambda n: pl.BlockSpec((tm, n), lambda i: (i, 0))
    vec = lambda n: pl.BlockSpec((1, n), lambda i: (0, 0))
    ins = [a, w]
    specs = [row(K), pl.BlockSpec(w.shape, lambda i: (0, 0), pipeline_mode=pl.Buffered(1))]
    outs, ospecs = [], []
    if pre_g is not None:
        ins, specs = ins + [pre_g], specs + [vec(K)]
        outs, ospecs = outs + [jax.ShapeDtypeStruct((M, K), BF16)], ospecs + [row(K)]
    if post is not None:
        ins, specs = ins + list(post), specs + [vec(N), row(N)]
    if gu is not None:
        ins, specs = ins + [gu], specs + [row(2 * N)]
        outs, ospecs = outs + [jax.ShapeDtypeStruct((M, 2 * N), BF16)], ospecs + [row(2 * N)]
    else:
        outs, ospecs = outs + [jax.ShapeDtypeStruct((M, N), out_dtype)], ospecs + [row(N)]
    if swiglu:
        outs, ospecs = outs + [jax.ShapeDtypeStruct((M, N // 2), BF16)], ospecs + [row(N // 2)]
    if post is not None:
        outs, ospecs = outs + [jax.ShapeDtypeStruct((M, N), F32)], ospecs + [row(N)]
    return _call(body, carry, ins, name=name, out_shape=outs, grid=(M // tm,), in_specs=specs, out_specs=ospecs,
                 sem=("parallel",))


def _mm_tn(a, b, carry=None, name="mm_tn"):
    S, M = a.shape
    N = b.shape[1]
    tm = M if M <= 1408 else M // 2
    tn = N if N <= 1408 else N // 4
    ts = _row_tile(S, 1024)

    def body(a_ref, b_ref, o_ref):
        @pl.when(pl.program_id(2) == 0)
        def _():
            o_ref[...] = jnp.zeros_like(o_ref)

        o_ref[...] += lax.dot_general(a_ref[...], b_ref[...], TN_DIMS, preferred_element_type=F32)

    res = _call(body, carry, [a, b], name=name, out_shape=[jax.ShapeDtypeStruct((M, N), F32)],
                grid=(M // tm, N // tn, S // ts),
                in_specs=[pl.BlockSpec((ts, tm), lambda i, j, s: (s, i)), pl.BlockSpec((ts, tn), lambda i, j, s: (s, j))],
                out_specs=[pl.BlockSpec((tm, tn), lambda i, j, s: (i, j))], sem=("parallel", "parallel", "arbitrary"))
    return res[0] if carry is None else res


def _rope_tables(S):
    pos = jnp.arange(S, dtype=jnp.int32)

    def table(p, n_dims, theta):
        inv = theta ** (-(jnp.arange(0, n_dims, 2, dtype=F32) / n_dims))
        ang = p.astype(F32)[:, None] * inv[None, :]
        return jnp.cos(ang), jnp.sin(ang)

    one = lambda n: jnp.ones((S, n), F32)
    zero = lambda n: jnp.zeros((S, n), F32)
    cp, sp = table(pos, ROPE_DIMS, ROPE_THETA)
    rest = HEAD_DIM - ROPE_DIMS
    part = (jnp.concatenate([cp, cp, one(rest)], 1), jnp.concatenate([zero(8), sp, zero(rest)], 1),
            jnp.concatenate([-sp, zero(8), zero(rest)], 1))
    cr, sr = table(pos // GRID_W, HEAD_DIM // 2, AXIAL_THETA)
    cc, sc = table(pos % GRID_W, HEAD_DIM // 2, AXIAL_THETA)
    axial = (jnp.concatenate([cr, cr, cc, cc], 1), jnp.concatenate([zero(16), sr, zero(16), sc], 1),
             jnp.concatenate([-sr, zero(16), -sc, zero(16)], 1))
    rep = LANES // HEAD_DIM
    return (tuple(jnp.tile(t, (1, rep)) for t in part), ROPE_DIMS // 2), (tuple(jnp.tile(t, (1, rep)) for t in axial), HEAD_DIM // 4)


def _seg_mats():
    col = jnp.arange(IN_W)[:, None] // HEAD_DIM
    e = (col == jnp.arange(LANES)[None, :]).astype(BF16)
    return e, e.T


def _qk_gain_row(qk_g):
    return jnp.concatenate([jnp.tile(qk_g[0], N_TOK_HEADS), jnp.tile(qk_g[1], N_KV_HEADS),
                            jnp.ones((IN_W - QK_W,), F32)])[None, :]


def _rope_cols(tabs, tm):
    col = lax.broadcasted_iota(jnp.int32, (tm, IN_W), 1)
    qk = col < QK_W
    c, s_lo, s_hi = (jnp.tile(t[...], (1, IN_W // LANES)) for t in tabs)
    return jnp.where(qk, c, 1.0), jnp.where(qk, s_lo, 0.0), jnp.where(qk, s_hi, 0.0), qk


def _seg_mean(v, e_ref, et_ref):
    def split_dot(t, m_ref):
        hi = t.astype(BF16)
        lo = (t - hi.astype(F32)).astype(BF16)
        return jnp.dot(hi, m_ref[...], preferred_element_type=F32) + jnp.dot(lo, m_ref[...], preferred_element_type=F32)

    return split_dot(split_dot(v, e_ref) * (1.0 / HEAD_DIM), et_ref)


def _headprep_fwd(proj, tabs, shift, qk_gain=None, name="headprep_fwd"):
    S = proj.shape[0]
    tm = _row_tile(S, 256)
    norm = qk_gain is not None

    def body(*refs):
        if norm:
            p_ref, c_ref, lo_ref, hi_ref, g_ref, e_ref, et_ref, o_ref = refs
        else:
            p_ref, c_ref, lo_ref, hi_ref, o_ref = refs
        x = p_ref[...]
        c, s_lo, s_hi, qk = _rope_cols((c_ref, lo_ref, hi_ref), tm)
        if norm:
            r = lax.rsqrt(_seg_mean(x * x, e_ref, et_ref) + EPS)
            x = x * jnp.where(qk, r, 1.0) * g_ref[...]
        y = x * c + pltpu.roll(x, shift, 1) * s_lo + pltpu.roll(x, IN_W - shift, 1) * s_hi
        o_ref[...] = y.astype(o_ref.dtype)

    row = pl.BlockSpec((tm, IN_W), lambda i: (i, 0))
    tab = pl.BlockSpec((tm, LANES), lambda i: (i, 0))
    ins = [proj, *tabs]
    specs = [row, tab, tab, tab]
    if norm:
        e, et = _seg_mats()
        ins += [qk_gain, e, et]
        specs += [pl.BlockSpec((1, IN_W), lambda i: (0, 0)), pl.BlockSpec((IN_W, LANES), lambda i: (0, 0)),
                  pl.BlockSpec((LANES, IN_W), lambda i: (0, 0))]
    return _pc(body, name=name, out_shape=jax.ShapeDtypeStruct((S, IN_W), BF16), grid=(S // tm,),
               in_specs=specs, out_specs=row, compiler_params=_cp("parallel"))(*ins)


def _headprep_bwd(dpr, tabs, shift, proj=None, qk_gain=None, name="headprep_bwd"):
    S = dpr.shape[0]
    tm = _row_tile(S, 256)
    norm = qk_gain is not None

    def body(*refs):
        if norm:
            d_ref, c_ref, lo_ref, hi_ref, p_ref, g_ref, e_ref, et_ref, o_ref, dg_ref = refs
        else:
            d_ref, c_ref, lo_ref, hi_ref, o_ref = refs
        d = d_ref[...].astype(F32)
        c, s_lo, s_hi, qk = _rope_cols((c_ref, lo_ref, hi_ref), tm)
        dx = d * c + pltpu.roll(d * s_lo, IN_W - shift, 1) + pltpu.roll(d * s_hi, shift, 1)
        if norm:
            x = p_ref[...]
            r = lax.rsqrt(_seg_mean(x * x, e_ref, et_ref) + EPS)
            xh = x * r

            @pl.when(pl.program_id(0) == 0)
            def _():
                dg_ref[...] = jnp.zeros_like(dg_ref)

            dg_ref[...] += jnp.sum(jnp.where(qk, dx * xh, 0.0), axis=0, keepdims=True)
            dxh = dx * g_ref[...]
            dn = r * (dxh - xh * _seg_mean(dxh * xh, e_ref, et_ref))
            dx = jnp.where(qk, dn, dx)
        o_ref[...] = dx.astype(o_ref.dtype)

    row = pl.BlockSpec((tm, IN_W), lambda i: (i, 0))
    tab = pl.BlockSpec((tm, LANES), lambda i: (i, 0))
    vec = pl.BlockSpec((1, IN_W), lambda i: (0, 0))
    ins = [dpr, *tabs]
    specs = [row, tab, tab, tab]
    out_shape = jax.ShapeDtypeStruct((S, IN_W), BF16)
    out_specs = row
    if norm:
        e, et = _seg_mats()
        ins += [proj, qk_gain, e, et]
        specs += [row, vec, pl.BlockSpec((IN_W, LANES), lambda i: (0, 0)), pl.BlockSpec((LANES, IN_W), lambda i: (0, 0))]
        out_shape = (out_shape, jax.ShapeDtypeStruct((1, IN_W), F32))
        out_specs = (row, vec)
    return _pc(body, name=name, out_shape=out_shape, grid=(S // tm,), in_specs=specs, out_specs=out_specs,
               compiler_params=_cp("arbitrary" if norm else "parallel"))(*ins)


def _band_mask(qb, kb, n_kb_valid, bq, tk, rows, radius):
    qpos = qb * bq + lax.rem(lax.broadcasted_iota(jnp.int32, (rows, 1), 0), bq)
    kpos = kb * tk + lax.broadcasted_iota(jnp.int32, (1, tk), 1)
    return (jnp.abs(qpos - kpos) <= radius) & n_kb_valid


def _flash_fwd(q, k, v, *, radius=None, sink=None, bq, tk, name="flash_fwd"):
    NB, G, L, HD = q.shape
    Lk = k.shape[1]
    nq, nk = L // bq, Lk // tk
    banded = radius is not None
    assert not banded or (bq == tk and radius <= tk and L == Lk)
    nst = 3 if banded else nk
    rows = G * bq

    def kmap(b, n, j):
        return (b, jnp.clip(n - 1 + j, 0, nk - 1), 0) if banded else (b, j, 0)

    def body(*refs):
        if sink is not None:
            sink_ref, *refs = refs
        q_ref, k_ref, v_ref, o_ref, lse_ref, m_sc, acc_sc = refs
        b, n, j = pl.program_id(0), pl.program_id(1), pl.program_id(2)

        @pl.when(j == 0)
        def _():
            if sink is not None:
                m_sc[...] = _sink_rows(sink_ref, b, G, bq, rows)
                acc_sc[...] = (lax.broadcasted_iota(jnp.int32, (rows, 2 * HD), 1) >= HD).astype(F32)
            else:
                m_sc[...] = jnp.full_like(m_sc, NEG)
                acc_sc[...] = jnp.zeros_like(acc_sc)

        qv = q_ref[0].reshape(rows, HD) * ATTN_SCALE
        s = lax.dot_general(qv, k_ref[0], NT_DIMS, preferred_element_type=F32)
        if banded:
            kb = n - 1 + j
            mask = _band_mask(n, kb, (kb >= 0) & (kb < nk), bq, tk, rows, radius)
            s = jnp.where(mask, s, NEG)
        m_prev = m_sc[...]
        m_new = jnp.maximum(m_prev, jnp.max(s, axis=1, keepdims=True))
        p = jnp.exp(s - m_new)
        if banded:
            p = jnp.where(mask, p, 0.0)
        v_ones = jnp.concatenate([v_ref[0], jnp.ones((tk, HD), BF16)], axis=1)
        acc_sc[...] = jnp.exp(m_prev - m_new) * acc_sc[...] + jnp.dot(p.astype(BF16), v_ones, preferred_element_type=F32)
        m_sc[...] = m_new

        @pl.when(j == nst - 1)
        def _():
            acc = acc_sc[...]
            l = acc[:, HD:HD + 1]
            o_ref[0] = (acc[:, :HD] / l).reshape(G, bq, HD).astype(o_ref.dtype)
            lse_ref[0] = (m_sc[...] + jnp.log(l)).reshape(G, bq, 1)

    qspec = pl.BlockSpec((1, G, bq, HD), lambda b, n, j: (b, 0, n, 0))
    kspec = pl.BlockSpec((1, tk, HD), kmap)
    ins, specs = [q, k, v], [qspec, kspec, kspec]
    if sink is not None:
        ins, specs = [sink] + ins, [pl.BlockSpec(memory_space=pltpu.SMEM)] + specs
    return _pc(body, name=name,
               out_shape=(jax.ShapeDtypeStruct((NB, G, L, HD), BF16), jax.ShapeDtypeStruct((NB, G, L, 1), F32)),
               grid=(NB, nq, nst), in_specs=specs,
               out_specs=(qspec, pl.BlockSpec((1, G, bq, 1), lambda b, n, j: (b, 0, n, 0))),
               scratch_shapes=[pltpu.VMEM((rows, 1), F32), pltpu.VMEM((rows, 2 * HD), F32)],
               compiler_params=_cp("parallel", "parallel", "arbitrary"))(*ins)


def _attn_delta(do, o, *, dlse=None, lse=None, sink=None, name="attn_delta"):
    NB, G, L, HD = do.shape
    bl = _row_tile(L, 1024)

    def body(*refs):
        refs = list(refs)
        sink_ref = refs.pop(0) if sink is not None else None
        do_ref, o_ref = refs.pop(0), refs.pop(0)
        dlse_ref = refs.pop(0) if dlse is not None else None
        lse_ref = refs.pop(0) if sink is not None else None
        delta_ref = refs.pop(0)
        b = pl.program_id(0)
        delta = jnp.sum(do_ref[0].astype(F32) * o_ref[0].astype(F32), axis=-1, keepdims=True)
        if dlse is not None:
            delta = delta - dlse_ref[0]
        delta_ref[0] = delta
        if sink is not None:
            ds_ref = refs.pop(0)

            @pl.when(pl.program_id(1) == 0)
            def _():
                ds_ref[...] = jnp.zeros_like(ds_ref)

            for g in range(G):
                ps = jnp.exp(sink_ref[b * G + g] - lse_ref[0, g]) * delta[g]
                ds_ref[0, g] -= jnp.sum(ps)

    blk = pl.BlockSpec((1, G, bl, HD), lambda b, n: (b, 0, n, 0))
    col = pl.BlockSpec((1, G, bl, 1), lambda b, n: (b, 0, n, 0))
    ins, specs = [do, o], [blk, blk]
    if dlse is not None:
        ins, specs = ins + [dlse], specs + [col]
    out_shape = jax.ShapeDtypeStruct((NB, G, L, 1), F32)
    out_specs = col
    if sink is not None:
        ins, specs = [sink] + ins + [lse], [pl.BlockSpec(memory_space=pltpu.SMEM)] + specs + [col]
        out_shape = (out_shape, jax.ShapeDtypeStruct((NB, G, 1, LANES), F32))
        out_specs = (col, pl.BlockSpec((1, G, 1, LANES), lambda b, n: (b, 0, 0, 0)))
    return _pc(body, name=name, out_shape=out_shape, grid=(NB, L // bl), in_specs=specs, out_specs=out_specs,
               compiler_params=_cp("parallel", "arbitrary"))(*ins)


def _flash_dq(q, k, v, do, lse, delta, *, radius=None, bq, tk, name="flash_dq"):
    NB, G, L, HD = q.shape
    Lk = k.shape[1]
    nq, nk = L // bq, Lk // tk
    banded = radius is not None
    nst = 3 if banded else nk
    rows = G * bq

    def kmap(b, n, j):
        return (b, jnp.clip(n - 1 + j, 0, nk - 1), 0) if banded else (b, j, 0)

    def body(q_ref, k_ref, v_ref, do_ref, lse_ref, dl_ref, dq_ref, acc_sc):
        n, j = pl.program_id(1), pl.program_id(2)

        @pl.when(j == 0)
        def _():
            acc_sc[...] = jnp.zeros_like(acc_sc)

        qv = q_ref[0].reshape(rows, HD)
        dov = do_ref[0].reshape(rows, HD)
        s = lax.dot_general(qv, k_ref[0], NT_DIMS, preferred_element_type=F32) * ATTN_SCALE
        p = jnp.exp(s - lse_ref[0].reshape(rows, 1))
        if banded:
            kb = n - 1 + j
            p = jnp.where(_band_mask(n, kb, (kb >= 0) & (kb < nk), bq, tk, rows, radius), p, 0.0)
        dp = lax.dot_general(dov, v_ref[0], NT_DIMS, preferred_element_type=F32)
        ds = p * (dp - dl_ref[0].reshape(rows, 1))
        acc_sc[...] += jnp.dot(ds.astype(BF16), k_ref[0], preferred_element_type=F32)

        @pl.when(j == nst - 1)
        def _():
            dq_ref[0] = (acc_sc[...] * ATTN_SCALE).reshape(G, bq, HD).astype(dq_ref.dtype)

    qspec = pl.BlockSpec((1, G, bq, HD), lambda b, n, j: (b, 0, n, 0))
    cspec = pl.BlockSpec((1, G, bq, 1), lambda b, n, j: (b, 0, n, 0))
    kspec = pl.BlockSpec((1, tk, HD), kmap)
    return _pc(body, name=name, out_shape=jax.ShapeDtypeStruct((NB, G, L, HD), F32), grid=(NB, nq, nst),
               in_specs=[qspec, kspec, kspec, qspec, cspec, cspec], out_specs=qspec,
               scratch_shapes=[pltpu.VMEM((rows, HD), F32)],
               compiler_params=_cp("parallel", "parallel", "arbitrary"))(q, k, v, do, lse, delta)


def _flash_dkv(q, k, v, do, lse, delta, *, radius=None, bq, tk, name="flash_dkv"):
    NB, G, L, HD = q.shape
    Lk = k.shape[1]
    nq, nk = L // bq, Lk // tk
    banded = radius is not None
    nst = 3 if banded else nq
    rows = G * bq

    def qmap(b, m, j):
        return (b, 0, jnp.clip(m - 1 + j, 0, nq - 1), 0) if banded else (b, 0, j, 0)

    def body(q_ref, k_ref, v_ref, do_ref, lse_ref, dl_ref, dk_ref, dv_ref, dk_sc, dv_sc):
        m, j = pl.program_id(1), pl.program_id(2)

        @pl.when(j == 0)
        def _():
            dk_sc[...] = jnp.zeros_like(dk_sc)
            dv_sc[...] = jnp.zeros_like(dv_sc)

        qv = q_ref[0].reshape(rows, HD)
        dov = do_ref[0].reshape(rows, HD)
        s = lax.dot_general(qv, k_ref[0], NT_DIMS, preferred_element_type=F32) * ATTN_SCALE
        p = jnp.exp(s - lse_ref[0].reshape(rows, 1))
        if banded:
            qb = m - 1 + j
            p = jnp.where(_band_mask(qb, m, (qb >= 0) & (qb < nq), bq, tk, rows, radius), p, 0.0)
        dv_sc[...] += lax.dot_general(p.astype(BF16), dov, TN_DIMS, preferred_element_type=F32)
        dp = lax.dot_general(dov, v_ref[0], NT_DIMS, preferred_element_type=F32)
        ds = p * (dp - dl_ref[0].reshape(rows, 1))
        dk_sc[...] += lax.dot_general(ds.astype(BF16), qv, TN_DIMS, preferred_element_type=F32)

        @pl.when(j == nst - 1)
        def _():
            dk_ref[0] = (dk_sc[...] * ATTN_SCALE).astype(dk_ref.dtype)
            dv_ref[0] = dv_sc[...].astype(dv_ref.dtype)

    qspec = pl.BlockSpec((1, G, bq, HD), qmap)
    cspec = pl.BlockSpec((1, G, bq, 1), qmap)
    kspec = pl.BlockSpec((1, tk, HD), lambda b, m, j: (b, m, 0))
    kv_shape = jax.ShapeDtypeStruct((NB, Lk, HD), F32)
    return _pc(body, name=name, out_shape=(kv_shape, kv_shape), grid=(NB, nk, nst),
               in_specs=[qspec, kspec, kspec, qspec, cspec, cspec], out_specs=(kspec, kspec),
               scratch_shapes=[pltpu.VMEM((tk, HD), F32), pltpu.VMEM((tk, HD), F32)],
               compiler_params=_cp("parallel", "parallel", "arbitrary"))(q, k, v, do, lse, delta)


def _flash_bwd_full(q, k, v, do, lse, delta, *, bq, tk, name="flash_bwd"):
    NB, G, L, HD = q.shape
    Lk = k.shape[1]
    nq, nk = L // bq, Lk // tk
    rows = G * bq

    def body(q_ref, k_ref, v_ref, do_ref, lse_ref, dl_ref, dqp_ref, dk_ref, dv_ref, dk_sc, dv_sc):
        n = pl.program_id(2)

        @pl.when(n == 0)
        def _():
            dk_sc[...] = jnp.zeros_like(dk_sc)
            dv_sc[...] = jnp.zeros_like(dv_sc)

        qs = q_ref[0].reshape(rows, HD) * ATTN_SCALE
        dov = do_ref[0].reshape(rows, HD)
        s = lax.dot_general(qs, k_ref[0], NT_DIMS, preferred_element_type=F32)
        p = jnp.exp(s - lse_ref[0].reshape(rows, 1))
        dp = lax.dot_general(dov, v_ref[0], NT_DIMS, preferred_element_type=F32)
        ds = (p * (dp - dl_ref[0].reshape(rows, 1))).astype(BF16)
        dv_sc[...] += lax.dot_general(p.astype(BF16), dov, TN_DIMS, preferred_element_type=F32)
        dk_sc[...] += lax.dot_general(ds, qs, TN_DIMS, preferred_element_type=F32)
        dqp_ref[0, 0] = (jnp.dot(ds, k_ref[0], preferred_element_type=F32) * ATTN_SCALE).reshape(G, bq, HD)

        @pl.when(n == nq - 1)
        def _():
            dk_ref[0] = dk_sc[...]
            dv_ref[0] = dv_sc[...]

    qspec = pl.BlockSpec((1, G, bq, HD), lambda b, m, n: (b, 0, n, 0))
    cspec = pl.BlockSpec((1, G, bq, 1), lambda b, m, n: (b, 0, n, 0))
    kspec = pl.BlockSpec((1, tk, HD), lambda b, m, n: (b, m, 0))
    kv_shape = jax.ShapeDtypeStruct((NB, Lk, HD), F32)
    dqp, dk, dv = _pc(body, name=name,
                      out_shape=(jax.ShapeDtypeStruct((nk, NB, G, L, HD), F32), kv_shape, kv_shape), grid=(NB, nk, nq),
                      in_specs=[qspec, kspec, kspec, qspec, cspec, cspec],
                      out_specs=(pl.BlockSpec((1, 1, G, bq, HD), lambda b, m, n: (m, b, 0, n, 0)), kspec, kspec),
                      scratch_shapes=[pltpu.VMEM((tk, HD), F32), pltpu.VMEM((tk, HD), F32)],
                      compiler_params=_cp("parallel", "parallel", "arbitrary"))(q, k, v, do, lse, delta)
    bl = _row_tile(L, 512)

    def sum_body(p_ref, o_ref):
        acc = p_ref[0, 0]
        for j in range(1, nk):
            acc = acc + p_ref[j, 0]
        o_ref[0] = acc

    dq = _pc(sum_body, name=name + "_sum", out_shape=jax.ShapeDtypeStruct((NB, G, L, HD), F32), grid=(NB, L // bl),
             in_specs=[pl.BlockSpec((nk, 1, G, bl, HD), lambda b, n: (0, b, 0, n, 0))],
             out_specs=pl.BlockSpec((1, G, bl, HD), lambda b, n: (b, 0, n, 0)),
             compiler_params=_cp("parallel", "parallel"))(dqp)
    return dq, dk, dv


def _window(L, blk, radius):
    if radius is None:
        return L, None
    W = min(L, blk + 2 * radius)
    assert blk % radius == 0 and (L - W) % radius == 0
    return W, lambda n: radius * jnp.clip(n * (blk // radius) - 1, 0, (L - W) // radius)


def _win_specs(G, W, HD, start, with_g):
    E = pl.Element
    st = (lambda n: 0) if start is None else start
    if with_g:
        return pl.BlockSpec((E(1), E(G), E(W), E(HD)), lambda b, n: (b, 0, st(n), 0))
    return pl.BlockSpec((E(1), E(W), E(HD)), lambda b, n: (b, st(n), 0))


def _sink_rows(sink_ref, b, G, per, rows):
    head = lax.div(lax.broadcasted_iota(jnp.int32, (rows, 1), 0), per)
    sk = jnp.zeros((rows, 1), F32)
    for g in range(G):
        sk = jnp.where(head == g, sink_ref[b * G + g], sk)
    return sk


def _win_fwd(q, k, v, *, radius=None, sink=None, bq, name="win_fwd"):
    NB, G, L, HD = q.shape
    Lk = k.shape[1]
    W, start = _window(Lk, bq, radius)
    rows = G * bq

    def body(*refs):
        if sink is not None:
            sink_ref, *refs = refs
        q_ref, k_ref, v_ref, o_ref, lse_ref = refs
        b, n = pl.program_id(0), pl.program_id(1)
        qv = q_ref[0].reshape(rows, HD) * ATTN_SCALE
        s = lax.dot_general(qv, k_ref[0], NT_DIMS, preferred_element_type=F32)
        if radius is not None:
            qpos = n * bq + lax.rem(lax.broadcasted_iota(jnp.int32, (rows, 1), 0), bq)
            kpos = start(n) + lax.broadcasted_iota(jnp.int32, (1, W), 1)
            s = jnp.where(jnp.abs(qpos - kpos) <= radius, s, NEG)
        m = jnp.max(s, axis=1, keepdims=True)
        if sink is not None:
            sk = _sink_rows(sink_ref, b, G, bq, rows)
            m = jnp.maximum(m, sk)
        p = jnp.exp(s - m)
        l = jnp.sum(p, axis=1, keepdims=True)
        if sink is not None:
            l = l + jnp.exp(sk - m)
        o = jnp.dot(p.astype(BF16), v_ref[0], preferred_element_type=F32) / l
        o_ref[0] = o.reshape(G, bq, HD).astype(o_ref.dtype)
        lse_ref[0] = (m + jnp.log(l)).reshape(G, bq, 1)

    qspec = pl.BlockSpec((1, G, bq, HD), lambda b, n: (b, 0, n, 0))
    kspec = _win_specs(G, W, HD, start, False)
    ins, specs = [q, k, v], [qspec, kspec, kspec]
    if sink is not None:
        ins, specs = [sink] + ins, [pl.BlockSpec(memory_space=pltpu.SMEM)] + specs
    return _pc(body, name=name,
               out_shape=(jax.ShapeDtypeStruct((NB, G, L, HD), BF16), jax.ShapeDtypeStruct((NB, G, L, 1), F32)),
               grid=(NB, L // bq), in_specs=specs,
               out_specs=(qspec, pl.BlockSpec((1, G, bq, 1), lambda b, n: (b, 0, n, 0))),
               compiler_params=_cp("parallel", "parallel"))(*ins)


def _win_dq(q, k, v, do, lse, delta, *, radius=None, bq, name="win_dq"):
    NB, G, L, HD = q.shape
    Lk = k.shape[1]
    W, start = _window(Lk, bq, radius)
    rows = G * bq

    def body(q_ref, k_ref, v_ref, do_ref, lse_ref, dl_ref, dq_ref):
        n = pl.program_id(1)
        qv = q_ref[0].reshape(rows, HD) * ATTN_SCALE
        dov = do_ref[0].reshape(rows, HD)
        s = lax.dot_general(qv, k_ref[0], NT_DIMS, preferred_element_type=F32)
        p = jnp.exp(s - lse_ref[0].reshape(rows, 1))
        if radius is not None:
            qpos = n * bq + lax.rem(lax.broadcasted_iota(jnp.int32, (rows, 1), 0), bq)
            kpos = start(n) + lax.broadcasted_iota(jnp.int32, (1, W), 1)
            p = jnp.where(jnp.abs(qpos - kpos) <= radius, p, 0.0)
        dp = lax.dot_general(dov, v_ref[0], NT_DIMS, preferred_element_type=F32)
        ds = p * (dp - dl_ref[0].reshape(rows, 1))
        dq = jnp.dot(ds.astype(BF16), k_ref[0], preferred_element_type=F32) * ATTN_SCALE
        dq_ref[0] = dq.reshape(G, bq, HD)

    qspec = pl.BlockSpec((1, G, bq, HD), lambda b, n: (b, 0, n, 0))
    cspec = pl.BlockSpec((1, G, bq, 1), lambda b, n: (b, 0, n, 0))
    kspec = _win_specs(G, W, HD, start, False)
    return _pc(body, name=name, out_shape=jax.ShapeDtypeStruct((NB, G, L, HD), F32), grid=(NB, L // bq),
               in_specs=[qspec, kspec, kspec, qspec, cspec, cspec], out_specs=qspec,
               compiler_params=_cp("parallel", "parallel"))(q, k, v, do, lse, delta)


def _win_dkv(q, k, v, do, lse, delta, *, radius, bk, name="win_dkv"):
    NB, G, L, HD = q.shape
    Lk = k.shape[1]
    W, start = _window(L, bk, radius)
    rows = G * W

    def body(q_ref, k_ref, v_ref, do_ref, lse_ref, dl_ref, dk_ref, dv_ref):
        m = pl.program_id(1)
        qv = q_ref[0].reshape(rows, HD)
        dov = do_ref[0].reshape(rows, HD)
        s = lax.dot_general(qv * ATTN_SCALE, k_ref[0], NT_DIMS, preferred_element_type=F32)
        p = jnp.exp(s - lse_ref[0].reshape(rows, 1))
        qpos = start(m) + lax.rem(lax.broadcasted_iota(jnp.int32, (rows, 1), 0), W)
        kpos = m * bk + lax.broadcasted_iota(jnp.int32, (1, bk), 1)
        p = jnp.where(jnp.abs(qpos - kpos) <= radius, p, 0.0)
        dv_ref[0] = lax.dot_general(p.astype(BF16), dov, TN_DIMS, preferred_element_type=F32)
        dp = lax.dot_general(dov, v_ref[0], NT_DIMS, preferred_element_type=F32)
        ds = p * (dp - dl_ref[0].reshape(rows, 1))
        dk_ref[0] = lax.dot_general(ds.astype(BF16), qv, TN_DIMS, preferred_element_type=F32) * ATTN_SCALE

    qspec = _win_specs(G, W, HD, start, True)
    cspec = _win_specs(G, W, 1, start, True)
    kspec = pl.BlockSpec((1, bk, HD), lambda b, m: (b, m, 0))
    kv_shape = jax.ShapeDtypeStruct((NB, Lk, HD), F32)
    return _pc(body, name=name, out_shape=(kv_shape, kv_shape), grid=(NB, Lk // bk),
               in_specs=[qspec, kspec, kspec, qspec, cspec, cspec], out_specs=(kspec, kspec),
               compiler_params=_cp("parallel", "parallel"))(q, k, v, do, lse, delta)


def _attention(q, k, v, *, radius=None, sink=None, bq, tk=None, tag):
    if tk is None:
        return _win_fwd(q, k, v, radius=radius, sink=sink, bq=bq, name=f"win_fwd_{tag}")
    return _flash_fwd(q, k, v, radius=radius, sink=sink, bq=bq, tk=tk, name=f"flash_fwd_{tag}")


def _attention_bwd(q, k, v, o, lse, do, *, radius=None, sink=None, dlse=None, bq, tk=None, tag):
    if sink is not None:
        delta, ds = _attn_delta(do, o, lse=lse, sink=sink, name=f"attn_delta_{tag}")
        dsink = ds[:, :, 0, 0].reshape(-1)
    else:
        delta, dsink = _attn_delta(do, o, dlse=dlse, name=f"attn_delta_{tag}"), None
    if tk is not None and radius is None:
        return (*_flash_bwd_full(q, k, v, do, lse, delta, bq=bq, tk=tk, name=f"flash_bwd_{tag}"), dsink)
    if tk is None:
        dq = _win_dq(q, k, v, do, lse, delta, radius=radius, bq=bq, name=f"win_dq_{tag}")
    else:
        dq = _flash_dq(q, k, v, do, lse, delta, radius=radius, bq=bq, tk=tk, name=f"flash_dq_{tag}")
    if tk is None and radius is not None:
        dk, dv = _win_dkv(q, k, v, do, lse, delta, radius=radius, bk=bq, name=f"win_dkv_{tag}")
    else:
        dk, dv = _flash_dkv(q, k, v, do, lse, delta, radius=radius, bq=bq, tk=tk or k.shape[1], name=f"flash_dkv_{tag}")
    return dq, dk, dv, dsink


CHAIN_ROWS = 128


def _skewed(n, stages):
    for t in range(n + len(stages) - 1):
        for s, stage in enumerate(stages):
            if 0 <= t - s < n:
                stage(t - s)


def _chain_slices(G, bq):
    cr = min(CHAIN_ROWS, bq)
    per = bq // cr
    return [(c // per, slice((c % per) * cr, (c % per + 1) * cr), slice(c * cr, (c + 1) * cr)) for c in range(G * per)]


def _v_ones(v):
    return jnp.concatenate([v, jnp.ones(v.shape, v.dtype)], axis=1)


def _attn_fwd_full(q, k, v, *, bq, tk, name="attn_fwd_full"):
    NB, G, L, HD = q.shape
    Lk = k.shape[1]
    nq, nk = L // bq, Lk // tk
    rows = G * bq
    chains = _chain_slices(G, bq)

    def body(q_ref, k_ref, v_ref, o_ref, lse_ref, m_sc, acc_sc):
        j = pl.program_id(2)

        @pl.when(j == 0)
        def _():
            m_sc[...] = jnp.full_like(m_sc, NEG)
            acc_sc[...] = jnp.zeros_like(acc_sc)

        kk = k_ref[0]
        vv = _v_ones(v_ref[0])
        st = [dict() for _ in chains]

        def scores(c):
            g, hr, _ = chains[c]
            st[c]["s"] = lax.dot_general(q_ref[0, g, hr, :] * ATTN_SCALE, kk, NT_DIMS, preferred_element_type=F32)

        def softmax(c):
            sl = chains[c][2]
            m_prev = m_sc[sl]
            m_new = jnp.maximum(m_prev, jnp.max(st[c]["s"], axis=1, keepdims=True))
            st[c]["p"] = jnp.exp(st[c].pop("s") - m_new).astype(BF16)
            st[c]["alpha"] = jnp.exp(m_prev - m_new)
            m_sc[sl] = m_new

        def values(c):
            sl = chains[c][2]
            acc_sc[sl] = st[c].pop("alpha") * acc_sc[sl] + jnp.dot(st[c].pop("p"), vv, preferred_element_type=F32)

        _skewed(len(chains), (scores, softmax, values))

        @pl.when(j == nk - 1)
        def _():
            acc = acc_sc[...]
            l = acc[:, HD:HD + 1]
            o_ref[0] = (acc[:, :HD] / l).reshape(G, bq, HD).astype(o_ref.dtype)
            lse_ref[0] = (m_sc[...] + jnp.log(l)).reshape(G, bq, 1)

    qspec = pl.BlockSpec((1, G, bq, HD), lambda b, n, j: (b, 0, n, 0))
    kspec = pl.BlockSpec((1, tk, HD), lambda b, n, j: (b, j, 0))
    return _pc(body, name=name,
               out_shape=(jax.ShapeDtypeStruct((NB, G, L, HD), BF16), jax.ShapeDtypeStruct((NB, G, L, 1), F32)),
               grid=(NB, nq, nk), in_specs=[qspec, kspec, kspec],
               out_specs=(qspec, pl.BlockSpec((1, G, bq, 1), lambda b, n, j: (b, 0, n, 0))),
               scratch_shapes=[pltpu.VMEM((rows, 1), F32), pltpu.VMEM((rows, 2 * HD), F32)],
               compiler_params=_cp("parallel", "parallel", "arbitrary"))(q, k, v)


def _attn_bwd_full(q, k, v, do, lse, delta, *, bq, tk, name="attn_bwd_full"):
    NB, G, L, HD = q.shape
    Lk = k.shape[1]
    nq, nk = L // bq, Lk // tk
    rows = G * bq
    chains = _chain_slices(G, bq)

    def body(q_ref, k_ref, v_ref, do_ref, lse_ref, dl_ref, dqp_ref, dk_ref, dv_ref, dk_sc, dv_sc):
        n = pl.program_id(2)

        @pl.when(n == 0)
        def _():
            dk_sc[...] = jnp.zeros_like(dk_sc)
            dv_sc[...] = jnp.zeros_like(dv_sc)

        kk, vv = k_ref[0], v_ref[0]
        st = [dict() for _ in chains]

        def scores(c):
            g, hr, _ = chains[c]
            st[c]["q"] = q_ref[0, g, hr, :] * ATTN_SCALE
            st[c]["do"] = do_ref[0, g, hr, :]
            st[c]["s"] = lax.dot_general(st[c]["q"], kk, NT_DIMS, preferred_element_type=F32)
            st[c]["dp"] = lax.dot_general(st[c]["do"], vv, NT_DIMS, preferred_element_type=F32)

        def softmax(c):
            g, hr, _ = chains[c]
            p = jnp.exp(st[c].pop("s") - lse_ref[0, g, hr, :])
            st[c]["ds"] = (p * (st[c].pop("dp") - dl_ref[0, g, hr, :])).astype(BF16)
            st[c]["p"] = p.astype(BF16)

        def grads(c):
            g, hr, _ = chains[c]
            ds = st[c].pop("ds")
            dv_sc[...] += lax.dot_general(st[c].pop("p"), st[c].pop("do"), TN_DIMS, preferred_element_type=F32)
            dk_sc[...] += lax.dot_general(ds, st[c].pop("q"), TN_DIMS, preferred_element_type=F32)
            dqp_ref[0, 0, g, hr, :] = jnp.dot(ds, kk, preferred_element_type=F32) * ATTN_SCALE

        _skewed(len(chains), (scores, softmax, grads))

        @pl.when(n == nq - 1)
        def _():
            dk_ref[0] = dk_sc[...].astype(dk_ref.dtype)
            dv_ref[0] = dv_sc[...].astype(dv_ref.dtype)

    qspec = pl.BlockSpec((1, G, bq, HD), lambda b, m, n: (b, 0, n, 0))
    cspec = pl.BlockSpec((1, G, bq, 1), lambda b, m, n: (b, 0, n, 0))
    kspec = pl.BlockSpec((1, tk, HD), lambda b, m, n: (b, m, 0))
    kv_shape = jax.ShapeDtypeStruct((NB, Lk, HD), BF16)
    dqp, dk, dv = _pc(body, name=name,
                      out_shape=(jax.ShapeDtypeStruct((nk, NB, G, L, HD), F32), kv_shape, kv_shape), grid=(NB, nk, nq),
                      in_specs=[qspec, kspec, kspec, qspec, cspec, cspec],
                      out_specs=(pl.BlockSpec((1, 1, G, bq, HD), lambda b, m, n: (m, b, 0, n, 0)), kspec, kspec),
                      scratch_shapes=[pltpu.VMEM((tk, HD), F32), pltpu.VMEM((tk, HD), F32)],
                      compiler_params=_cp("parallel", "parallel", "arbitrary"))(q, k, v, do, lse, delta)
    if nk == 1:
        return dqp[0].astype(BF16), dk, dv
    bl = _row_tile(L, 512)

    def sum_body(p_ref, o_ref):
        acc = p_ref[0, 0]
        for j in range(1, nk):
            acc = acc + p_ref[j, 0]
        o_ref[0] = acc.astype(o_ref.dtype)

    dq = _pc(sum_body, name=name + "_sum", out_shape=jax.ShapeDtypeStruct((NB, G, L, HD), BF16), grid=(NB, L // bl),
             in_specs=[pl.BlockSpec((nk, 1, G, bl, HD), lambda b, n: (0, b, 0, n, 0))],
             out_specs=pl.BlockSpec((1, G, bl, HD), lambda b, n: (b, 0, n, 0)),
             compiler_params=_cp("parallel", "parallel"))(dqp)
    return dq, dk, dv


def _attn_fwd_win(q, k, v, *, radius, sink=None, bq, carry=None, name="attn_fwd_win"):
    NB, G, L, HD = q.shape
    W, start = _window(k.shape[1], bq, radius)
    chains = _chain_slices(G, bq)

    def body(*refs):
        if sink is not None:
            sink_ref, *refs = refs
        q_ref, k_ref, v_ref, o_ref, lse_ref = refs
        b, n = pl.program_id(0), pl.program_id(1)
        kk = k_ref[0]
        vv = _v_ones(v_ref[0])
        st = [dict() for _ in chains]

        def scores(c):
            g, hr, _ = chains[c]
            s = lax.dot_general(q_ref[0, g, hr, :] * ATTN_SCALE, kk, NT_DIMS, preferred_element_type=F32)
            if radius is not None:
                qpos = n * bq + hr.start + lax.broadcasted_iota(jnp.int32, (hr.stop - hr.start, 1), 0)
                kpos = start(n) + lax.broadcasted_iota(jnp.int32, (1, W), 1)
                s = jnp.where(jnp.abs(qpos - kpos) <= radius, s, NEG)
            st[c]["s"] = s

        def softmax(c):
            g = chains[c][0]
            m = jnp.max(st[c]["s"], axis=1, keepdims=True)
            if sink is not None:
                m = jnp.maximum(m, sink_ref[b * G + g])
            st[c]["p"] = jnp.exp(st[c].pop("s") - m).astype(BF16)
            st[c]["m"] = m

        def values(c):
            g, hr, _ = chains[c]
            acc = jnp.dot(st[c].pop("p"), vv, preferred_element_type=F32)
            m = st[c].pop("m")
            l = acc[:, HD:HD + 1]
            if sink is not None:
                l = l + jnp.exp(sink_ref[b * G + g] - m)
            o_ref[0, g, hr, :] = (acc[:, :HD] / l).astype(o_ref.dtype)
            lse_ref[0, g, hr, :] = m + jnp.log(l)

        _skewed(len(chains), (scores, softmax, values))

    qspec = pl.BlockSpec((1, G, bq, HD), lambda b, n: (b, 0, n, 0))
    kspec = _win_specs(G, W, HD, start, False)
    ins, specs = [q, k, v], [qspec, kspec, kspec]
    if sink is not None:
        ins, specs = [sink] + ins, [pl.BlockSpec(memory_space=pltpu.SMEM)] + specs
    return _call(body, carry, ins, name=name,
                 out_shape=(jax.ShapeDtypeStruct((NB, G, L, HD), BF16), jax.ShapeDtypeStruct((NB, G, L, 1), F32)),
                 grid=(NB, L // bq), in_specs=specs,
                 out_specs=(qspec, pl.BlockSpec((1, G, bq, 1), lambda b, n: (b, 0, n, 0))), sem=("parallel", "parallel"))


def _attn_dq_win(q, k, v, do, lse, delta, *, radius, bq, name="attn_dq_win"):
    NB, G, L, HD = q.shape
    W, start = _window(L, bq, radius)
    chains = _chain_slices(G, bq)

    def body(q_ref, k_ref, v_ref, do_ref, lse_ref, dl_ref, dq_ref):
        n = pl.program_id(1)
        kk, vv = k_ref[0], v_ref[0]
        kpos = start(n) + lax.broadcasted_iota(jnp.int32, (1, W), 1)
        st = [dict() for _ in chains]

        def scores(c):
            g, hr, _ = chains[c]
            st[c]["s"] = lax.dot_general(q_ref[0, g, hr, :] * ATTN_SCALE, kk, NT_DIMS, preferred_element_type=F32)
            st[c]["dp"] = lax.dot_general(do_ref[0, g, hr, :], vv, NT_DIMS, preferred_element_type=F32)

        def softmax(c):
            g, hr, _ = chains[c]
            qpos = n * bq + hr.start + lax.broadcasted_iota(jnp.int32, (hr.stop - hr.start, 1), 0)
            p = jnp.where(jnp.abs(qpos - kpos) <= radius, jnp.exp(st[c].pop("s") - lse_ref[0, g, hr, :]), 0.0)
            st[c]["ds"] = (p * (st[c].pop("dp") - dl_ref[0, g, hr, :])).astype(BF16)

        def grads(c):
            g, hr, _ = chains[c]
            dq = jnp.dot(st[c].pop("ds"), kk, preferred_element_type=F32) * ATTN_SCALE
            dq_ref[0, g, hr, :] = dq.astype(dq_ref.dtype)

        _skewed(len(chains), (scores, softmax, grads))

    qspec = pl.BlockSpec((1, G, bq, HD), lambda b, n: (b, 0, n, 0))
    cspec = pl.BlockSpec((1, G, bq, 1), lambda b, n: (b, 0, n, 0))
    kspec = _win_specs(G, W, HD, start, False)
    return _pc(body, name=name, out_shape=jax.ShapeDtypeStruct((NB, G, L, HD), BF16), grid=(NB, L // bq),
               in_specs=[qspec, kspec, kspec, qspec, cspec, cspec], out_specs=qspec,
               compiler_params=_cp("parallel", "parallel"))(q, k, v, do, lse, delta)


def _attn_dkv_win(q, k, v, do, lse, delta, *, radius, bk, name="attn_dkv_win"):
    NB, G, L, HD = q.shape
    W, start = _window(L, bk, radius)
    chains = _chain_slices(G, W)

    def body(q_ref, k_ref, v_ref, do_ref, lse_ref, dl_ref, dk_ref, dv_ref):
        m = pl.program_id(1)
        kk, vv = k_ref[0], v_ref[0]
        kpos = m * bk + lax.broadcasted_iota(jnp.int32, (1, bk), 1)
        st = [dict() for _ in chains]
        out = dict(dk=jnp.zeros((bk, HD), F32), dv=jnp.zeros((bk, HD), F32))

        def scores(c):
            g, hr, _ = chains[c]
            st[c]["q"] = q_ref[0, g, hr, :] * ATTN_SCALE
            st[c]["do"] = do_ref[0, g, hr, :]
            st[c]["s"] = lax.dot_general(st[c]["q"], kk, NT_DIMS, preferred_element_type=F32)
            st[c]["dp"] = lax.dot_general(st[c]["do"], vv, NT_DIMS, preferred_element_type=F32)

        def softmax(c):
            g, hr, _ = chains[c]
            qpos = start(m) + hr.start + lax.broadcasted_iota(jnp.int32, (hr.stop - hr.start, 1), 0)
            p = jnp.where(jnp.abs(qpos - kpos) <= radius, jnp.exp(st[c].pop("s") - lse_ref[0, g, hr, :]), 0.0)
            st[c]["ds"] = (p * (st[c].pop("dp") - dl_ref[0, g, hr, :])).astype(BF16)
            st[c]["p"] = p.astype(BF16)

        def grads(c):
            out["dv"] = out["dv"] + lax.dot_general(st[c].pop("p"), st[c].pop("do"), TN_DIMS, preferred_element_type=F32)
            out["dk"] = out["dk"] + lax.dot_general(st[c].pop("ds"), st[c].pop("q"), TN_DIMS, preferred_element_type=F32)

        _skewed(len(chains), (scores, softmax, grads))
        dk_ref[0] = out["dk"].astype(dk_ref.dtype)
        dv_ref[0] = out["dv"].astype(dv_ref.dtype)

    qspec = _win_specs(G, W, HD, start, True)
    cspec = _win_specs(G, W, 1, start, True)
    kspec = pl.BlockSpec((1, bk, HD), lambda b, m: (b, m, 0))
    kv_shape = jax.ShapeDtypeStruct((NB, L, HD), BF16)
    return _pc(body, name=name, out_shape=(kv_shape, kv_shape), grid=(NB, L // bk),
               in_specs=[qspec, kspec, kspec, qspec, cspec, cspec], out_specs=(kspec, kspec),
               compiler_params=_cp("parallel", "parallel"))(q, k, v, do, lse, delta)


def _attn(q, k, v, *, radius=None, sink=None, bq, tk=None, carry=None, tag):
    if radius is None and tk < k.shape[1]:
        return _attn_fwd_full(q, k, v, bq=bq, tk=tk, name=f"attn_fwd_{tag}")
    return _attn_fwd_win(q, k, v, radius=radius, sink=sink, bq=bq, carry=carry, name=f"attn_fwd_{tag}")


def _attn_bwd(q, k, v, o, lse, do, *, radius=None, sink=None, dlse=None, bq, tk=None, tag):
    if sink is not None:
        delta, ds = _attn_delta(do, o, lse=lse, sink=sink, name=f"attn_delta_{tag}")
        dsink = ds[:, :, 0, 0].reshape(-1)
    else:
        delta, dsink = _attn_delta(do, o, dlse=dlse, name=f"attn_delta_{tag}"), None
    if radius is None:
        return (*_attn_bwd_full(q, k, v, do, lse, delta, bq=bq, tk=tk, name=f"attn_bwd_{tag}"), dsink)
    dq = _attn_dq_win(q, k, v, do, lse, delta, radius=radius, bq=bq, name=f"attn_dq_{tag}")
    dk, dv = _attn_dkv_win(q, k, v, do, lse, delta, radius=radius, bk=bq, name=f"attn_dkv_{tag}")
    return dq, dk, dv, dsink


def _combine_fwd(o, lse, name="combine_fwd"):
    H, S, HD = o.shape
    tm = _row_tile(S, 512)

    def body(o_ref, lse_ref, t_ref):
        for g in range(GQA_GROUP):
            hs = [kv * GQA_GROUP + g for kv in range(N_KV_HEADS)]
            ls = [lse_ref[h] for h in hs]
            mx = functools.reduce(jnp.maximum, ls)
            es = [jnp.exp(l - mx) for l in ls]
            den = functools.reduce(jnp.add, es)
            for h, e in zip(hs, es):
                t_ref[h] = (o_ref[h].astype(F32) * (e / den)).astype(t_ref.dtype)

    blk = pl.BlockSpec((H, tm, HD), lambda i: (0, i, 0))
    col = pl.BlockSpec((H, tm, 1), lambda i: (0, i, 0))
    return _pc(body, name=name, out_shape=jax.ShapeDtypeStruct((H, S, HD), BF16), grid=(S // tm,),
               in_specs=[blk, col], out_specs=blk, compiler_params=_cp("parallel"))(o, lse)


def _combine_bwd(dt, o, lse, name="combine_bwd"):
    H, S, HD = o.shape
    tm = _row_tile(S, 512)

    def body(dt_ref, o_ref, lse_ref, do_ref, dlse_ref):
        for g in range(GQA_GROUP):
            hs = [kv * GQA_GROUP + g for kv in range(N_KV_HEADS)]
            ls = [lse_ref[h] for h in hs]
            mx = functools.reduce(jnp.maximum, ls)
            es = [jnp.exp(l - mx) for l in ls]
            den = functools.reduce(jnp.add, es)
            al = [e / den for e in es]
            dts = [dt_ref[h].astype(F32) for h in hs]
            da = [jnp.sum(d * o_ref[h].astype(F32), axis=-1, keepdims=True) for h, d in zip(hs, dts)]
            dot = functools.reduce(jnp.add, [a * d for a, d in zip(al, da)])
            for h, a, d, dd in zip(hs, al, da, dts):
                do_ref[h] = (dd * a).astype(do_ref.dtype)
                dlse_ref[h] = a * (d - dot)

    blk = pl.BlockSpec((H, tm, HD), lambda i: (0, i, 0))
    col = pl.BlockSpec((H, tm, 1), lambda i: (0, i, 0))
    return _pc(body, name=name,
               out_shape=(jax.ShapeDtypeStruct((H, S, HD), BF16), jax.ShapeDtypeStruct((H, S, 1), F32)),
               grid=(S // tm,), in_specs=[blk, blk, col], out_specs=(blk, col), compiler_params=_cp("parallel"))(dt, o, lse)


def _position():
    x, y, c = lax.axis_index("x"), lax.axis_index("y"), lax.axis_index("c")
    return x, y, c


def _peer(pos, k):
    x, y, c = pos
    return (1 - x if k & 4 else x, 1 - y if k & 2 else y, 1 - c if k & 1 else c)


def _linear(p):
    return 4 * p[0] + 2 * p[1] + p[2]


def _exchange_steps(s_ref, r_ref, send_sems, recv_sems, local_sem, gather):
    pos = _position()
    me = _linear(pos)
    own = pltpu.make_async_copy(s_ref if gather else s_ref.at[me], r_ref.at[me], local_sem)
    peers = range(1, N_DEV)

    def sems(k):
        return dict(send_sem=send_sems.at[k - 1], recv_sem=recv_sems.at[k - 1], device_id=_peer(pos, k), device_id_type=MESH)

    def send(k):
        src = s_ref if gather else s_ref.at[_linear(_peer(pos, k))]
        return pltpu.make_async_remote_copy(src_ref=src, dst_ref=r_ref.at[me], **sems(k))

    def arrival(k):
        slot = r_ref.at[_linear(_peer(pos, k))]
        return pltpu.make_async_remote_copy(src_ref=slot, dst_ref=slot, **sems(k))

    def start():
        own.start()
        for k in peers:
            send(k).start()

    def wait():
        for k in peers:
            arrival(k).wait_recv()
        for k in peers:
            send(k).wait_send()
        own.wait()

    return start, wait


EXCHANGE_SEMS = [pltpu.SemaphoreType.DMA((N_DEV - 1,)), pltpu.SemaphoreType.DMA((N_DEV - 1,)), pltpu.SemaphoreType.DMA]


def _exchange(buf, gather, name):
    def body(s_ref, r_ref, *sems):
        start, wait = _exchange_steps(s_ref, r_ref, *sems, gather)
        start()
        wait()

    hbm = pl.BlockSpec(memory_space=pltpu.HBM)
    out_shape = ((N_DEV,) + buf.shape) if gather else buf.shape
    return _pc(body, name=name, out_shape=jax.ShapeDtypeStruct(out_shape, buf.dtype), in_specs=[hbm], out_specs=hbm,
               scratch_shapes=list(EXCHANGE_SEMS))(buf)


def _call(body, carry, ins, *, name, out_shape, grid, in_specs, out_specs, scratch_shapes=(), sem):
    if carry is None:
        return _pc(body, name=name, out_shape=tuple(out_shape), grid=grid, in_specs=list(in_specs),
                   out_specs=tuple(out_specs), scratch_shapes=list(scratch_shapes), compiler_params=_cp(*sem))(*ins)
    buf, gather = carry
    n_in, n_out, n_sc = len(ins), len(out_shape), len(scratch_shapes)

    def wrapped(*refs):
        in_refs, buf_ref = refs[:n_in], refs[n_in]
        out_refs, recv_ref = refs[n_in + 1:n_in + 1 + n_out], refs[n_in + 1 + n_out]
        rest = refs[n_in + 2 + n_out:]
        first = functools.reduce(jnp.logical_and, [pl.program_id(a) == 0 for a in range(len(grid))])
        last = functools.reduce(jnp.logical_and, [pl.program_id(a) == grid[a] - 1 for a in range(len(grid))])

        @pl.when(first)
        def _():
            _exchange_steps(buf_ref, recv_ref, *rest[n_sc:], gather)[0]()

        body(*in_refs, *out_refs, *rest[:n_sc])

        @pl.when(last)
        def _():
            _exchange_steps(buf_ref, recv_ref, *rest[n_sc:], gather)[1]()

    hbm = pl.BlockSpec(memory_space=pltpu.HBM)
    recv_shape = ((N_DEV,) + buf.shape) if gather else buf.shape
    return _pc(wrapped, name=name, out_shape=(*out_shape, jax.ShapeDtypeStruct(recv_shape, buf.dtype)), grid=grid,
               in_specs=[*in_specs, hbm], out_specs=(*out_specs, hbm), scratch_shapes=[*scratch_shapes, *EXCHANGE_SEMS],
               compiler_params=_cp(*(("arbitrary",) * len(grid))))(*ins, buf)


def _reduce_adamw(recv, w, m, v, name):
    _, R, C = recv.shape
    tr = _row_tile(R, 512)

    def body(r_ref, w_ref, m_ref, v_ref, g_ref, d_ref, nm_ref, nv_ref):
        g = r_ref[0].astype(F32)
        for j in range(1, N_DEV):
            g = g + r_ref[j].astype(F32)
        g_ref[...] = g
        nm = ADAM_B1 * m_ref[...] + (1.0 - ADAM_B1) * g
        nv = ADAM_B2 * v_ref[...] + (1.0 - ADAM_B2) * jnp.square(g)
        m_hat = nm / (1.0 - ADAM_B1 ** ADAM_STEP)
        v_hat = nv / (1.0 - ADAM_B2 ** ADAM_STEP)
        d_ref[...] = -ADAM_LR * (m_hat / (jnp.sqrt(v_hat) + ADAM_EPS) + ADAM_WD * w_ref[...])
        nm_ref[...] = nm
        nv_ref[...] = nv

    row = pl.BlockSpec((tr, C), lambda i: (i, 0))
    out = jax.ShapeDtypeStruct((R, C), F32)
    return _pc(body, name=name, out_shape=(out, out, out, out), grid=(R // tr,),
               in_specs=[pl.BlockSpec((N_DEV, tr, C), lambda i: (0, i, 0)), row, row, row],
               out_specs=(row, row, row, row), compiler_params=_cp("parallel"))(recv, w, m, v)


BIG = (("w_in", 2), ("w_mem_kv", 1), ("w_o", 1), ("w_gate_up", 2), ("w_down", 1))
SMALL = ("mem_norm_g", "g_mix_pre", "g_mix_post", "attn_sink", "qk_norm_g", "g_ffn_pre", "g_ffn_post")
SMALL_W = 1024


def _pack_local(shards, dtype):
    return jnp.concatenate([s.astype(dtype).reshape(-1, LANES) for s in shards], axis=0)


def _unpack_local(flat, shapes):
    out, r = [], 0
    for shp in shapes:
        n = shp[0] * shp[1] * shp[2] // LANES
        out.append(flat[r:r + n].reshape(shp))
        r += n
    return out


def _unpack_gathered(g, shapes):
    out, r = [], 0
    for (name, dim), shp in zip(BIG, shapes):
        n = shp[0] * shp[1] * shp[2] // LANES
        t = g[:, r:r + n].reshape((N_DEV,) + tuple(shp))
        if dim == 2:
            t = t.transpose(1, 2, 0, 3).reshape(shp[0], shp[1], N_DEV * shp[2])
        else:
            t = t.transpose(1, 0, 2, 3).reshape(shp[0], N_DEV * shp[1], shp[2])
        out.append(t)
        r += n
    return out


def _pack_for_scatter(full, shapes, dtype):
    parts = []
    for (name, dim), shp, t in zip(BIG, shapes, full):
        if dim == 2:
            t = t.reshape(shp[0], shp[1], N_DEV, shp[2]).transpose(2, 0, 1, 3)
        else:
            t = t.reshape(shp[0], N_DEV, shp[1], shp[2]).transpose(1, 0, 2, 3)
        parts.append(t.astype(dtype).reshape(N_DEV, -1, LANES))
    return jnp.concatenate(parts, axis=1)


def _pack_small(arrs):
    flat = jnp.concatenate([a.reshape(-1) for a in arrs])
    pad = (-flat.shape[0]) % (8 * SMALL_W)
    return jnp.pad(flat, (0, pad)).reshape(-1, SMALL_W)


def _unpack_small(flat, shapes):
    flat = flat.reshape(-1)
    out, r = [], 0
    for shp in shapes:
        n = 1
        for d in shp:
            n *= d
        out.append(flat[r:r + n].reshape(shp))
        r += n
    return out


def _heads(t, nb, g):
    S = t.shape[0]
    return t.reshape(S, nb, g, HEAD_DIM).transpose(1, 2, 0, 3)


def _unheads(t):
    nb, g, S, hd = t.shape
    return t.transpose(2, 0, 1, 3).reshape(S, nb * g * hd)


def _dilate(t, dil):
    S = t.shape[0]
    g = t.shape[1] // HEAD_DIM
    return t.reshape(S // dil, dil, g, HEAD_DIM).transpose(1, 2, 0, 3)


def _undilate(t):
    dil, g, L, w = t.shape
    return t.transpose(1, 2, 0, 3).reshape(g, L * dil, w)


FULL_BQ = 256
FULL_TK_FWD = 4096
FULL_TK_BWD = 2048


def _mixer_fwd(kind, pr, sink, li, carry=None):
    S = pr.shape[0]
    if kind == 0:
        q = _heads(pr[:, :Q_W], N_KV_HEADS, GQA_GROUP)
        k = _heads(pr[:, Q_W:QK_W], N_KV_HEADS, 1)[:, 0]
        v = _heads(pr[:, QK_W:QK_W + KV_W], N_KV_HEADS, 1)[:, 0]
        o, lse, *arrived = _attn(q, k, v, radius=A_RADIUS, sink=sink, bq=min(256, S), carry=carry, tag=f"a{li}")
        return _unheads(o), (q, k, v, o, lse), (arrived[0] if arrived else None)
    assert carry is None
    if kind == 1:
        q = _heads(pr[:, :Q_W], N_KV_HEADS, GQA_GROUP)
        k = _heads(pr[:, Q_W:QK_W], N_KV_HEADS, 1)[:, 0]
        v = _heads(pr[:, QK_W:QK_W + KV_W], N_KV_HEADS, 1)[:, 0]
        o, lse = _attn(q, k, v, bq=min(FULL_BQ, S), tk=min(FULL_TK_FWD, S), tag=f"b{li}")
        return _unheads(o), (q, k, v, o, lse), None
    saved, outs, lses = [], [], []
    for g, (window, dil) in enumerate(C_GROUPS):
        q = _dilate(pr[:, g * GQA_GROUP * HEAD_DIM:(g + 1) * GQA_GROUP * HEAD_DIM], dil)
        k = _dilate(pr[:, Q_W + g * HEAD_DIM:Q_W + (g + 1) * HEAD_DIM], dil)[:, 0]
        v = _dilate(pr[:, QK_W + g * HEAD_DIM:QK_W + (g + 1) * HEAD_DIM], dil)[:, 0]
        o, lse = _attn(q, k, v, radius=window // (2 * dil), bq=min(256, S // dil), tag=f"c{li}g{g}")
        saved.append((q, k, v, o, lse))
        outs.append(_undilate(o))
        lses.append(_undilate(lse))
    o_all, lse_all = jnp.concatenate(outs, 0), jnp.concatenate(lses, 0)
    tok = _combine_fwd(o_all, lse_all, name=f"combine_fwd_{li}")
    return tok.transpose(1, 0, 2).reshape(S, Q_W), (saved, o_all, lse_all), None


def _mixer_bwd(kind, dtok, saved, sink, li):
    S = dtok.shape[0]
    if kind in (0, 1):
        q, k, v, o, lse = saved
        do = _heads(dtok, N_KV_HEADS, GQA_GROUP)
        if kind == 0:
            dq, dk, dv, dsink = _attn_bwd(q, k, v, o, lse, do, radius=A_RADIUS, sink=sink, bq=min(256, S), tag=f"a{li}")
        else:
            dq, dk, dv, dsink = _attn_bwd(q, k, v, o, lse, do, bq=min(FULL_BQ, S), tk=min(FULL_TK_BWD, S), tag=f"b{li}")
        return _unheads(dq), _unheads(dk[:, None]), _unheads(dv[:, None]), dsink
    per_group, o_all, lse_all = saved
    dt = dtok.reshape(S, N_TOK_HEADS, HEAD_DIM).transpose(1, 0, 2)
    do_all, dlse_all = _combine_bwd(dt, o_all, lse_all, name=f"combine_bwd_{li}")
    dqs, dks, dvs = [], [], []
    for g, (window, dil) in enumerate(C_GROUPS):
        q, k, v, o, lse = per_group[g]
        L = S // dil
        hs = slice(g * GQA_GROUP, (g + 1) * GQA_GROUP)
        do = do_all[hs].reshape(GQA_GROUP, L, dil, HEAD_DIM).transpose(2, 0, 1, 3)
        dlse = dlse_all[hs].reshape(GQA_GROUP, L, dil, 1).transpose(2, 0, 1, 3)
        dq, dk, dv, _ = _attn_bwd(q, k, v, o, lse, do, radius=window // (2 * dil), dlse=dlse, bq=min(256, L),
                                       tag=f"c{li}g{g}")
        dqs.append(dq.transpose(2, 0, 1, 3).reshape(S, GQA_GROUP * HEAD_DIM))
        dks.append(dk.transpose(1, 0, 2).reshape(S, HEAD_DIM))
        dvs.append(dv.transpose(1, 0, 2).reshape(S, HEAD_DIM))
    return jnp.concatenate(dqs, 1), jnp.concatenate(dks, 1), jnp.concatenate(dvs, 1), None


def kernel(x, mem, mem_norm_g, w_in, w_mem_kv, w_o, g_mix_pre, g_mix_post, attn_sink, qk_norm_g, w_gate_up, w_down, g_ffn_pre, g_ffn_post, loss_target, m_mem_norm_g, m_w_in, m_w_mem_kv, m_w_o, m_g_mix_pre, m_g_mix_post, m_attn_sink, m_qk_norm_g, m_w_gate_up, m_w_down, m_g_ffn_pre, m_g_ffn_post, v_mem_norm_g, v_w_in, v_w_mem_kv, v_w_o, v_g_mix_pre, v_g_mix_post, v_attn_sink, v_qk_norm_g, v_w_gate_up, v_w_down, v_g_ffn_pre, v_g_ffn_post):
    given = dict(locals())
    depth = w_in.shape[0]
    S, D = x.shape[1], x.shape[2]
    x0 = x[0]
    layer_shapes = [(1,) + tuple(given[n].shape[1:]) for n, _ in BIG]

    def layer_pack(pre, l, dtype):
        return _pack_local([given[pre + n][l:l + 1] for n, _ in BIG], dtype)

    def layer_weights(gathered):
        return [t[0] for t in _unpack_gathered(gathered, layer_shapes)]

    W = [None] * depth
    W[0] = layer_weights(_exchange(layer_pack("", 0, BF16), True, "gather_w0"))

    tabs = _rope_tables(S)
    mem_n = _rms_fwd(mem[0], mem_norm_g[None], BF16, name="rms_mem")

    saved = []
    xc = x0
    for i in range(depth):
        kind = i % N_MIXERS
        (tab, shift) = tabs[1] if kind == 1 else tabs[0]
        sink = attn_sink[i // N_MIXERS] if kind == 0 else None
        qk_gain = _qk_gain_row(qk_norm_g[i // N_MIXERS]) if kind == 1 else None
        W_in, W_mkv, W_o, W_gu, W_dn = W[i]
        h, proj = _mm(xc, W_in, F32, pre_g=g_mix_pre[i][None], name=f"mm_in_{i}")
        pr = _headprep_fwd(proj, tab, shift, qk_gain, name=f"headprep_fwd_{i}")
        carry = (layer_pack("", 1, BF16), True) if i == 0 and depth > 1 else None
        tok, mix_saved, arrived = _mixer_fwd(kind, pr, sink, i, carry)
        if carry is not None:
            W[1] = layer_weights(arrived)
        (mkv,) = _mm(mem_n, W_mkv, BF16, name=f"mm_mkv_{i}")
        qm = _heads(pr[:, QK_W + KV_W:], N_MEM_HEADS, 1)
        km = _heads(mkv[:, :QM_W], N_MEM_HEADS, 1)[:, 0]
        vm = _heads(mkv[:, QM_W:], N_MEM_HEADS, 1)[:, 0]
        mo, mlse = _attn(qm, km, vm, bq=min(1024, S), tk=km.shape[1], tag=f"m{i}")
        cat = jnp.concatenate([tok, _unheads(mo)], axis=1)
        o, x1 = _mm(cat, W_o, F32, post=(g_mix_post[i][None], xc), name=f"mm_o_{i}")
        carry = (layer_pack("", i + 2, BF16), True) if i + 2 < depth else None
        h2, gu, act, *arrived = _mm(x1, W_gu, BF16, pre_g=g_ffn_pre[i][None], swiglu=True, carry=carry, name=f"mm_gu_{i}")
        if carry is not None:
            W[i + 2] = layer_weights(arrived[0])
        f, x2 = _mm(act, W_dn, F32, post=(g_ffn_post[i][None], x1), name=f"mm_dn_{i}")
        saved.append(dict(x=xc, h=h, proj=proj, mix=mix_saved, qm=qm, km=km, vm=vm, mo=mo, mlse=mlse, cat=cat, o=o,
                          x1=x1, h2=h2, gu=gu, act=act, f=f))
        xc = x2

    dy, sq = _loss_head(xc, loss_target[0], name="loss_head")
    loss = lax.psum(sq[0, 0] * (0.5 / D), ("x", "y", "c"))

    zero_row = jnp.zeros((1, D), F32)
    grads = {n: [None] * depth for n in ("w_in", "w_mem_kv", "w_o", "w_gate_up", "w_down", "g_mix_pre", "g_mix_post",
                                         "g_ffn_pre", "g_ffn_post")}
    d_sink = [jnp.zeros((N_TOK_HEADS,), F32) for _ in range(attn_sink.shape[0])]
    d_qkg = [jnp.zeros((2, HEAD_DIM), F32) for _ in range(qk_norm_g.shape[0])]
    dmem_n = jnp.zeros((mem.shape[1], D), F32)
    recv = [None] * depth

    def scatter_pack(l):
        return _pack_for_scatter([grads[n][l][None] for n, _ in BIG], layer_shapes, BF16)

    dx = dy
    for i in reversed(range(depth)):
        kind = i % N_MIXERS
        sv = saved[i]
        (tab, shift) = tabs[1] if kind == 1 else tabs[0]
        sink = attn_sink[i // N_MIXERS] if kind == 0 else None
        W_in, W_mkv, W_o, W_gu, W_dn = W[i]
        df, dg = _rms_bwd(sv["f"], g_ffn_post[i][None], dx, BF16, name=f"rmsb_fpost_{i}")
        grads["g_ffn_post"][i] = dg[0]
        (dgu,) = _mm(df, W_dn, BF16, nt=True, gu=sv["gu"], name=f"mmb_dn_{i}")
        grads["w_down"][i] = _mm_tn(sv["act"], df, name=f"mmw_dn_{i}")
        (dh2,) = _mm(dgu, W_gu, F32, nt=True, name=f"mmb_gu_{i}")
        if i + 1 < depth:
            grads["w_gate_up"][i], recv[i + 1] = _mm_tn(sv["h2"], dgu, carry=(scatter_pack(i + 1), False), name=f"mmw_gu_{i}")
        else:
            grads["w_gate_up"][i] = _mm_tn(sv["h2"], dgu, name=f"mmw_gu_{i}")
        dx1, dg = _rms_bwd(sv["x1"], g_ffn_pre[i][None], dh2, F32, res=dx, name=f"rmsb_fpre_{i}")
        grads["g_ffn_pre"][i] = dg[0]
        do, dg = _rms_bwd(sv["o"], g_mix_post[i][None], dx1, BF16, name=f"rmsb_post_{i}")
        grads["g_mix_post"][i] = dg[0]
        (dcat,) = _mm(do, W_o, BF16, nt=True, name=f"mmb_o_{i}")
        grads["w_o"][i] = _mm_tn(sv["cat"], do, name=f"mmw_o_{i}")
        dmo = _heads(dcat[:, Q_W:], N_MEM_HEADS, 1)
        dqm, dkm, dvm, _ = _attn_bwd(sv["qm"], sv["km"], sv["vm"], sv["mo"], sv["mlse"], dmo, bq=min(1024, S),
                                     tk=sv["km"].shape[1], tag=f"m{i}")
        dmkv = jnp.concatenate([_unheads(dkm[:, None]), _unheads(dvm[:, None])], axis=1).astype(BF16)
        grads["w_mem_kv"][i] = _mm_tn(mem_n, dmkv, name=f"mmw_mkv_{i}")
        dmem_n = dmem_n + _mm(dmkv, W_mkv, F32, nt=True, name=f"mmb_mkv_{i}")[0]
        dq, dk, dv, dsink = _mixer_bwd(kind, dcat[:, :Q_W], sv["mix"], sink, i)
        if dsink is not None:
            d_sink[i // N_MIXERS] = dsink
        dpr = jnp.concatenate([dq, dk, dv, _unheads(dqm)], axis=1)
        if kind == 1:
            dproj, dgc = _headprep_bwd(dpr, tab, shift, sv["proj"], _qk_gain_row(qk_norm_g[i // N_MIXERS]),
                                       name=f"headprep_bwd_{i}")
            d_qkg[i // N_MIXERS] = jnp.stack([dgc[0, :Q_W].reshape(N_TOK_HEADS, HEAD_DIM).sum(0),
                                              dgc[0, Q_W:QK_W].reshape(N_KV_HEADS, HEAD_DIM).sum(0)])
        else:
            dproj = _headprep_bwd(dpr, tab, shift, name=f"headprep_bwd_{i}")
        (dh,) = _mm(dproj, W_in, F32, nt=True, name=f"mmb_in_{i}")
        grads["w_in"][i] = _mm_tn(sv["h"], dproj, name=f"mmw_in_{i}")
        dx, dg = _rms_bwd(sv["x"], g_mix_pre[i][None], dh, F32, res=dx1, name=f"rmsb_pre_{i}")
        grads["g_mix_pre"][i] = dg[0]
    _, dg_mem = _rms_bwd(mem[0], mem_norm_g[None], dmem_n, BF16, name="rmsb_mem")

    recv[0] = _exchange(scatter_pack(0), False, "scatter_g0")
    per_layer = [_reduce_adamw(recv[l], layer_pack("", l, F32), layer_pack("m_", l, F32), layer_pack("v_", l, F32),
                               name=f"adamw_{l}") for l in range(depth)]
    big_out = [[jnp.concatenate(ts, axis=0) for ts in zip(*[_unpack_local(per_layer[l][j], layer_shapes) for l in range(depth)])]
               for j in range(4)]

    small_grads = dict(mem_norm_g=dg_mem[0], g_mix_pre=jnp.stack(grads["g_mix_pre"]), g_mix_post=jnp.stack(grads["g_mix_post"]),
                       attn_sink=jnp.stack(d_sink), qk_norm_g=jnp.stack(d_qkg), g_ffn_pre=jnp.stack(grads["g_ffn_pre"]),
                       g_ffn_post=jnp.stack(grads["g_ffn_post"]))
    sg = _pack_small([small_grads[n] for n in SMALL])
    srecv = _exchange(sg, True, "gather_small_grads")
    spacked = lambda pre: _pack_small([given[pre + n] for n in SMALL])
    gs, ds, ms, vs = _reduce_adamw(srecv, spacked(""), spacked("m_"), spacked("v_"), name="adamw_small")

    out = {}
    for pre, fb, fs in zip(("grad_", "delta_", "new_m_", "new_v_"), big_out, (gs, ds, ms, vs)):
        for (n, _), t in zip(BIG, fb):
            out[pre + n] = t
        for n, t in zip(SMALL, _unpack_small(fs, [given[n].shape for n in SMALL])):
            out[pre + n] = t
    order = ("mem_norm_g", "w_in", "w_mem_kv", "w_o", "g_mix_pre", "g_mix_post", "attn_sink", "qk_norm_g", "w_gate_up",
             "w_down", "g_ffn_pre", "g_ffn_post")
    return (loss, dx[None], *[out[p + n] for p in ("grad_", "delta_", "new_m_", "new_v_") for n in order])
```

```python
import functools

import jax
import jax.numpy as jnp
from jax import lax
from jax.experimental import pallas as pl
from jax.experimental.pallas import tpu as pltpu

F32 = jnp.float32
BF16 = jnp.bfloat16

HEAD_DIM = 64
N_TOK_HEADS = 12
N_KV_HEADS = 3
GQA_GROUP = 4
N_MEM_HEADS = 4
Q_W = N_TOK_HEADS * HEAD_DIM
KV_W = N_KV_HEADS * HEAD_DIM
QM_W = N_MEM_HEADS * HEAD_DIM
QK_W = Q_W + KV_W
IN_W = Q_W + 2 * KV_W + QM_W
N_HEAD_SLOTS = IN_W // HEAD_DIM
N_MIXERS = 3
A_RADIUS = 128
C_GROUPS = ((128, 1), (512, 4), (2048, 16))
ROPE_THETA = 500000.0
ROPE_DIMS = HEAD_DIM // 4
AXIAL_THETA = 10000.0
GRID_W = 64
EPS = 1e-6
ATTN_SCALE = HEAD_DIM ** -0.5
NEG = -1e30

ADAM_LR = 0.001
ADAM_B1 = 0.9
ADAM_B2 = 0.999
ADAM_EPS = 1e-08
ADAM_WD = 0.01
ADAM_STEP = 10

N_DEV = 8
LANES = 128
VMEM_LIMIT = 56 * 1024 * 1024
MESH = pl.DeviceIdType.MESH
NT_DIMS = (((1,), (1,)), ((), ()))
TN_DIMS = (((0,), (0,)), ((), ()))


def _pc(body, **kw):
    return pl.pallas_call(body, **kw)


def _cp(*sem):
    return pltpu.CompilerParams(dimension_semantics=sem, vmem_limit_bytes=VMEM_LIMIT)


def _row_tile(m, cap=512):
    t = cap
    while m % t:
        t //= 2
    return t


def _rms_fwd(x, g, out_dtype, res=None, name="rms_fwd"):
    M, D = x.shape
    tm = _row_tile(M)

    def body(*refs):
        if res is None:
            x_ref, g_ref, o_ref = refs
        else:
            x_ref, g_ref, r_ref, o_ref = refs
        xv = x_ref[...]
        y = xv * lax.rsqrt(jnp.mean(xv * xv, axis=-1, keepdims=True) + EPS) * g_ref[...]
        if res is not None:
            y = r_ref[...] + y
        o_ref[...] = y.astype(o_ref.dtype)

    row = pl.BlockSpec((tm, D), lambda i: (i, 0))
    vec = pl.BlockSpec((1, D), lambda i: (0, 0))
    ins = [x, g] + ([] if res is None else [res])
    specs = [row, vec] + ([] if res is None else [row])
    return _pc(body, name=name, out_shape=jax.ShapeDtypeStruct((M, D), out_dtype), grid=(M // tm,),
               in_specs=specs, out_specs=row, compiler_params=_cp("parallel"))(*ins)


def _rms_bwd(x, g, dy, out_dtype, res=None, name="rms_bwd"):
    M, D = x.shape
    tm = _row_tile(M)

    def body(*refs):
        if res is None:
            x_ref, g_ref, dy_ref, dx_ref, dg_ref = refs
        else:
            x_ref, g_ref, dy_ref, r_ref, dx_ref, dg_ref = refs
        xv = x_ref[...]
        r = lax.rsqrt(jnp.mean(xv * xv, axis=-1, keepdims=True) + EPS)
        xh = xv * r
        d = dy_ref[...].astype(F32)
        dxh = d * g_ref[...]
        dx = r * (dxh - xh * jnp.mean(dxh * xh, axis=-1, keepdims=True))
        if res is not None:
            dx = r_ref[...] + dx
        dx_ref[...] = dx.astype(dx_ref.dtype)

        @pl.when(pl.program_id(0) == 0)
        def _():
            dg_ref[...] = jnp.zeros_like(dg_ref)

        dg_ref[...] += jnp.sum(d * xh, axis=0, keepdims=True)

    row = pl.BlockSpec((tm, D), lambda i: (i, 0))
    vec = pl.BlockSpec((1, D), lambda i: (0, 0))
    ins = [x, g, dy] + ([] if res is None else [res])
    specs = [row, vec, row] + ([] if res is None else [row])
    return _pc(body, name=name,
               out_shape=(jax.ShapeDtypeStruct((M, D), out_dtype), jax.ShapeDtypeStruct((1, D), F32)),
               grid=(M // tm,), in_specs=specs, out_specs=(row, vec), compiler_params=_cp("arbitrary"))(*ins)


def _swiglu_bwd(gu, da, name="swiglu_bwd"):
    M, F2 = gu.shape
    F = F2 // 2
    tm = _row_tile(M, 256)

    def body(gu_ref, da_ref, o_ref):
        g = gu_ref[:, :F].astype(F32)
        u = gu_ref[:, F:].astype(F32)
        d = da_ref[...].astype(F32)
        sig = 1.0 / (1.0 + jnp.exp(-g))
        o_ref[:, :F] = (d * u * (sig * (1.0 + g * (1.0 - sig)))).astype(o_ref.dtype)
        o_ref[:, F:] = (d * (g * sig)).astype(o_ref.dtype)

    return _pc(body, name=name, out_shape=jax.ShapeDtypeStruct((M, F2), BF16), grid=(M // tm,),
               in_specs=[pl.BlockSpec((tm, F2), lambda i: (i, 0)), pl.BlockSpec((tm, F), lambda i: (i, 0))],
               out_specs=pl.BlockSpec((tm, F2), lambda i: (i, 0)), compiler_params=_cp("parallel"))(gu, da)


def _loss_head(y, t, name="loss_head"):
    M, D = y.shape
    tm = _row_tile(M)

    def body(y_ref, t_ref, dy_ref, acc_ref):
        e = y_ref[...] - t_ref[...]
        dy_ref[...] = e * (1.0 / D)

        @pl.when(pl.program_id(0) == 0)
        def _():
            acc_ref[...] = jnp.zeros_like(acc_ref)

        acc_ref[...] += jnp.sum(e * e)

    row = pl.BlockSpec((tm, D), lambda i: (i, 0))
    return _pc(body, name=name,
               out_shape=(jax.ShapeDtypeStruct((M, D), F32), jax.ShapeDtypeStruct((8, LANES), F32)),
               grid=(M // tm,), in_specs=[row, row],
               out_specs=(row, pl.BlockSpec((8, LANES), lambda i: (0, 0))), compiler_params=_cp("arbitrary"))(y, t)


def _rms_bwd_tile(xv, g, d):
    r = lax.rsqrt(jnp.mean(xv * xv, axis=-1, keepdims=True) + EPS)
    xh = xv * r
    dxh = d * g
    return r * (dxh - xh * jnp.mean(dxh * xh, axis=-1, keepdims=True)), jnp.sum(d * xh, axis=0, keepdims=True)


def _mm(a, w, out_dtype, nt=False, pre_g=None, swiglu=False, post=None, gu=None, pre_bwd=None, post_bwd=None, carry=None,
        tm=256, name="mm"):
    M, K = a.shape
    N = w.shape[0] if nt else w.shape[1]
    tm = _row_tile(M, tm)
    gain_grad = pre_bwd is not None or post_bwd is not None

    def body(*refs):
        refs = list(refs)
        a_ref, w_ref = refs.pop(0), refs.pop(0)
        pg_ref = refs.pop(0) if pre_g is not None else None
        g_ref, r_ref = (refs.pop(0), refs.pop(0)) if post is not None else (None, None)
        gu_ref = refs.pop(0) if gu is not None else None
        bwd_refs = [refs.pop(0) for _ in (pre_bwd or post_bwd or ())]
        lhs = a_ref[...]
        if pre_g is not None:
            lhs = (lhs * lax.rsqrt(jnp.mean(lhs * lhs, axis=-1, keepdims=True) + EPS) * pg_ref[...]).astype(BF16)
            refs.pop(0)[...] = lhs
        if pre_bwd is not None:
            lhs, dg = _rms_bwd_tile(bwd_refs[0][...], bwd_refs[1][...], lhs.astype(F32))
            lhs = lhs.astype(BF16)
            refs.pop(0)[...] = lhs
        if nt:
            acc = lax.dot_general(lhs, w_ref[...], NT_DIMS, preferred_element_type=F32)
        else:
            acc = jnp.dot(lhs, w_ref[...], preferred_element_type=F32)
        o_ref = refs.pop(0)
        if post_bwd is not None:
            dx, dg = _rms_bwd_tile(bwd_refs[0][...], bwd_refs[1][...], acc)
            o_ref[...] = bwd_refs[2][...] + dx
        elif gu is None:
            o_ref[...] = acc.astype(o_ref.dtype)
        else:
            gate = gu_ref[:, :N].astype(F32)
            sig = 1.0 / (1.0 + jnp.exp(-gate))
            o_ref[:, :N] = (acc * gu_ref[:, N:].astype(F32) * (sig * (1.0 + gate * (1.0 - sig)))).astype(o_ref.dtype)
            o_ref[:, N:] = (acc * (gate * sig)).astype(o_ref.dtype)
        if swiglu:
            gate = acc[:, : N // 2]
            refs.pop(0)[...] = (gate * (1.0 / (1.0 + jnp.exp(-gate))) * acc[:, N // 2:]).astype(BF16)
        if post is not None:
            y = acc * lax.rsqrt(jnp.mean(acc * acc, axis=-1, keepdims=True) + EPS) * g_ref[...]
            refs.pop(0)[...] = r_ref[...] + y
        if gain_grad:
            dg_ref = refs.pop(0)

            @pl.when(pl.program_id(0) == 0)
            def _():
                dg_ref[...] = jnp.zeros_like(dg_ref)

            dg_ref[...] += dg

    row = lambda n: pl.BlockSpec((tm, n), lambda i: (i, 0))
    vec = lambda n: pl.BlockSpec((1, n), lambda i: (0, 0))
    ins = [a, w]
    specs = [row(K), pl.BlockSpec(w.shape, lambda i: (0, 0), pipeline_mode=pl.Buffered(1))]
    outs, ospecs = [], []
    if pre_g is not None:
        ins, specs = ins + [pre_g], specs + [vec(K)]
    if pre_g is not None or pre_bwd is not None:
        outs, ospecs = outs + [jax.ShapeDtypeStruct((M, K), BF16)], ospecs + [row(K)]
    if post is not None:
        ins, specs = ins + list(post), specs + [vec(N), row(N)]
    if gu is not None:
        ins, specs = ins + [gu], specs + [row(2 * N)]
        outs, ospecs = outs + [jax.ShapeDtypeStruct((M, 2 * N), BF16)], ospecs + [row(2 * N)]
    else:
        outs, ospecs = outs + [jax.ShapeDtypeStruct((M, N), out_dtype)], ospecs + [row(N)]
    if pre_bwd is not None:
        ins, specs = ins + list(pre_bwd), specs + [row(K), vec(K)]
    if post_bwd is not None:
        ins, specs = ins + list(post_bwd), specs + [row(N), vec(N), row(N)]
    if swiglu:
        outs, ospecs = outs + [jax.ShapeDtypeStruct((M, N // 2), BF16)], ospecs + [row(N // 2)]
    if post is not None:
        outs, ospecs = outs + [jax.ShapeDtypeStruct((M, N), F32)], ospecs + [row(N)]
    if gain_grad:
        D = K if pre_bwd is not None else N
        outs, ospecs = outs + [jax.ShapeDtypeStruct((1, D), F32)], ospecs + [vec(D)]
    return _call(body, carry, ins, name=name, out_shape=outs, grid=(M // tm,), in_specs=specs, out_specs=ospecs,
                 sem=("arbitrary" if gain_grad else "parallel",))


def _mm_tn(a, b, carry=None, name="mm_tn"):
    S, M = a.shape
    N = b.shape[1]
    tm = M if M <= 1408 else M // 2
    tn = N if N <= 1408 else N // 4
    ts = _row_tile(S, 1024)

    def body(a_ref, b_ref, o_ref):
        @pl.when(pl.program_id(2) == 0)
        def _():
            o_ref[...] = jnp.zeros_like(o_ref)

        o_ref[...] += lax.dot_general(a_ref[...], b_ref[...], TN_DIMS, preferred_element_type=F32)

    res = _call(body, carry, [a, b], name=name, out_shape=[jax.ShapeDtypeStruct((M, N), F32)],
                grid=(M // tm, N // tn, S // ts),
                in_specs=[pl.BlockSpec((ts, tm), lambda i, j, s: (s, i)), pl.BlockSpec((ts, tn), lambda i, j, s: (s, j))],
                out_specs=[pl.BlockSpec((tm, tn), lambda i, j, s: (i, j))], sem=("parallel", "parallel", "arbitrary"))
    return res[0] if carry is None else res


def _rope_tables(S):
    pos = jnp.arange(S, dtype=jnp.int32)

    def table(p, n_dims, theta):
        inv = theta ** (-(jnp.arange(0, n_dims, 2, dtype=F32) / n_dims))
        ang = p.astype(F32)[:, None] * inv[None, :]
        return jnp.cos(ang), jnp.sin(ang)

    one = lambda n: jnp.ones((S, n), F32)
    zero = lambda n: jnp.zeros((S, n), F32)
    cp, sp = table(pos, ROPE_DIMS, ROPE_THETA)
    rest = HEAD_DIM - ROPE_DIMS
    part = (jnp.concatenate([cp, cp, one(rest)], 1), jnp.concatenate([zero(8), sp, zero(rest)], 1),
            jnp.concatenate([-sp, zero(8), zero(rest)], 1))
    cr, sr = table(pos // GRID_W, HEAD_DIM // 2, AXIAL_THETA)
    cc, sc = table(pos % GRID_W, HEAD_DIM // 2, AXIAL_THETA)
    axial = (jnp.concatenate([cr, cr, cc, cc], 1), jnp.concatenate([zero(16), sr, zero(16), sc], 1),
             jnp.concatenate([-sr, zero(16), -sc, zero(16)], 1))
    rep = LANES // HEAD_DIM
    return (tuple(jnp.tile(t, (1, rep)) for t in part), ROPE_DIMS // 2), (tuple(jnp.tile(t, (1, rep)) for t in axial), HEAD_DIM // 4)


def _seg_mats():
    col = jnp.arange(IN_W)[:, None] // HEAD_DIM
    e = (col == jnp.arange(LANES)[None, :]).astype(BF16)
    return e, e.T


def _qk_gain_row(qk_g):
    return jnp.concatenate([jnp.tile(qk_g[0], N_TOK_HEADS), jnp.tile(qk_g[1], N_KV_HEADS),
                            jnp.ones((IN_W - QK_W,), F32)])[None, :]


def _rope_cols(tabs, tm):
    col = lax.broadcasted_iota(jnp.int32, (tm, IN_W), 1)
    qk = col < QK_W
    c, s_lo, s_hi = (jnp.tile(t[...], (1, IN_W // LANES)) for t in tabs)
    return jnp.where(qk, c, 1.0), jnp.where(qk, s_lo, 0.0), jnp.where(qk, s_hi, 0.0), qk


def _seg_mean(v, e_ref, et_ref):
    def split_dot(t, m_ref):
        hi = t.astype(BF16)
        lo = (t - hi.astype(F32)).astype(BF16)
        return jnp.dot(hi, m_ref[...], preferred_element_type=F32) + jnp.dot(lo, m_ref[...], preferred_element_type=F32)

    return split_dot(split_dot(v, e_ref) * (1.0 / HEAD_DIM), et_ref)


def _headprep_fwd(proj, tabs, shift, qk_gain=None, name="headprep_fwd"):
    S = proj.shape[0]
    tm = _row_tile(S, 256)
    norm = qk_gain is not None

    def body(*refs):
        if norm:
            p_ref, c_ref, lo_ref, hi_ref, g_ref, e_ref, et_ref, o_ref = refs
        else:
            p_ref, c_ref, lo_ref, hi_ref, o_ref = refs
        x = p_ref[...]
        c, s_lo, s_hi, qk = _rope_cols((c_ref, lo_ref, hi_ref), tm)
        if norm:
            r = lax.rsqrt(_seg_mean(x * x, e_ref, et_ref) + EPS)
            x = x * jnp.where(qk, r, 1.0) * g_ref[...]
        y = x * c + pltpu.roll(x, shift, 1) * s_lo + pltpu.roll(x, IN_W - shift, 1) * s_hi
        o_ref[...] = y.astype(o_ref.dtype)

    row = pl.BlockSpec((tm, IN_W), lambda i: (i, 0))
    tab = pl.BlockSpec((tm, LANES), lambda i: (i, 0))
    ins = [proj, *tabs]
    specs = [row, tab, tab, tab]
    if norm:
        e, et = _seg_mats()
        ins += [qk_gain, e, et]
        specs += [pl.BlockSpec((1, IN_W), lambda i: (0, 0)), pl.BlockSpec((IN_W, LANES), lambda i: (0, 0)),
                  pl.BlockSpec((LANES, IN_W), lambda i: (0, 0))]
    return _pc(body, name=name, out_shape=jax.ShapeDtypeStruct((S, IN_W), BF16), grid=(S // tm,),
               in_specs=specs, out_specs=row, compiler_params=_cp("parallel"))(*ins)


def _headprep_bwd(dpr, tabs, shift, proj=None, qk_gain=None, name="headprep_bwd"):
    S = dpr.shape[0]
    tm = _row_tile(S, 256)
    norm = qk_gain is not None

    def body(*refs):
        if norm:
            d_ref, c_ref, lo_ref, hi_ref, p_ref, g_ref, e_ref, et_ref, o_ref, dg_ref = refs
        else:
            d_ref, c_ref, lo_ref, hi_ref, o_ref = refs
        d = d_ref[...].astype(F32)
        c, s_lo, s_hi, qk = _rope_cols((c_ref, lo_ref, hi_ref), tm)
        dx = d * c + pltpu.roll(d * s_lo, IN_W - shift, 1) + pltpu.roll(d * s_hi, shift, 1)
        if norm:
            x = p_ref[...]
            r = lax.rsqrt(_seg_mean(x * x, e_ref, et_ref) + EPS)
            xh = x * r

            @pl.when(pl.program_id(0) == 0)
            def _():
                dg_ref[...] = jnp.zeros_like(dg_ref)

            dg_ref[...] += jnp.sum(jnp.where(qk, dx * xh, 0.0), axis=0, keepdims=True)
            dxh = dx * g_ref[...]
            dn = r * (dxh - xh * _seg_mean(dxh * xh, e_ref, et_ref))
            dx = jnp.where(qk, dn, dx)
        o_ref[...] = dx.astype(o_ref.dtype)

    row = pl.BlockSpec((tm, IN_W), lambda i: (i, 0))
    tab = pl.BlockSpec((tm, LANES), lambda i: (i, 0))
    vec = pl.BlockSpec((1, IN_W), lambda i: (0, 0))
    ins = [dpr, *tabs]
    specs = [row, tab, tab, tab]
    out_shape = jax.ShapeDtypeStruct((S, IN_W), BF16)
    out_specs = row
    if norm:
        e, et = _seg_mats()
        ins += [proj, qk_gain, e, et]
        specs += [row, vec, pl.BlockSpec((IN_W, LANES), lambda i: (0, 0)), pl.BlockSpec((LANES, IN_W), lambda i: (0, 0))]
        out_shape = (out_shape, jax.ShapeDtypeStruct((1, IN_W), F32))
        out_specs = (row, vec)
    return _pc(body, name=name, out_shape=out_shape, grid=(S // tm,), in_specs=specs, out_specs=out_specs,
               compiler_params=_cp("arbitrary" if norm else "parallel"))(*ins)


def _band_mask(qb, kb, n_kb_valid, bq, tk, rows, radius):
    qpos = qb * bq + lax.rem(lax.broadcasted_iota(jnp.int32, (rows, 1), 0), bq)
    kpos = kb * tk + lax.broadcasted_iota(jnp.int32, (1, tk), 1)
    return (jnp.abs(qpos - kpos) <= radius) & n_kb_valid


def _flash_fwd(q, k, v, *, radius=None, sink=None, bq, tk, name="flash_fwd"):
    NB, G, L, HD = q.shape
    Lk = k.shape[1]
    nq, nk = L // bq, Lk // tk
    banded = radius is not None
    assert not banded or (bq == tk and radius <= tk and L == Lk)
    nst = 3 if banded else nk
    rows = G * bq

    def kmap(b, n, j):
        return (b, jnp.clip(n - 1 + j, 0, nk - 1), 0) if banded else (b, j, 0)

    def body(*refs):
        if sink is not None:
            sink_ref, *refs = refs
        q_ref, k_ref, v_ref, o_ref, lse_ref, m_sc, acc_sc = refs
        b, n, j = pl.program_id(0), pl.program_id(1), pl.program_id(2)

        @pl.when(j == 0)
        def _():
            if sink is not None:
                m_sc[...] = _sink_rows(sink_ref, b, G, bq, rows)
                acc_sc[...] = (lax.broadcasted_iota(jnp.int32, (rows, 2 * HD), 1) >= HD).astype(F32)
            else:
                m_sc[...] = jnp.full_like(m_sc, NEG)
                acc_sc[...] = jnp.zeros_like(acc_sc)

        qv = q_ref[0].reshape(rows, HD) * ATTN_SCALE
        s = lax.dot_general(qv, k_ref[0], NT_DIMS, preferred_element_type=F32)
        if banded:
            kb = n - 1 + j
            mask = _band_mask(n, kb, (kb >= 0) & (kb < nk), bq, tk, rows, radius)
            s = jnp.where(mask, s, NEG)
        m_prev = m_sc[...]
        m_new = jnp.maximum(m_prev, jnp.max(s, axis=1, keepdims=True))
        p = jnp.exp(s - m_new)
        if banded:
            p = jnp.where(mask, p, 0.0)
        v_ones = jnp.concatenate([v_ref[0], jnp.ones((tk, HD), BF16)], axis=1)
        acc_sc[...] = jnp.exp(m_prev - m_new) * acc_sc[...] + jnp.dot(p.astype(BF16), v_ones, preferred_element_type=F32)
        m_sc[...] = m_new

        @pl.when(j == nst - 1)
        def _():
            acc = acc_sc[...]
            l = acc[:, HD:HD + 1]
            o_ref[0] = (acc[:, :HD] / l).reshape(G, bq, HD).astype(o_ref.dtype)
            lse_ref[0] = (m_sc[...] + jnp.log(l)).reshape(G, bq, 1)

    qspec = pl.BlockSpec((1, G, bq, HD), lambda b, n, j: (b, 0, n, 0))
    kspec = pl.BlockSpec((1, tk, HD), kmap)
    ins, specs = [q, k, v], [qspec, kspec, kspec]
    if sink is not None:
        ins, specs = [sink] + ins, [pl.BlockSpec(memory_space=pltpu.SMEM)] + specs
    return _pc(body, name=name,
               out_shape=(jax.ShapeDtypeStruct((NB, G, L, HD), BF16), jax.ShapeDtypeStruct((NB, G, L, 1), F32)),
               grid=(NB, nq, nst), in_specs=specs,
               out_specs=(qspec, pl.BlockSpec((1, G, bq, 1), lambda b, n, j: (b, 0, n, 0))),
               scratch_shapes=[pltpu.VMEM((rows, 1), F32), pltpu.VMEM((rows, 2 * HD), F32)],
               compiler_params=_cp("parallel", "parallel", "arbitrary"))(*ins)


def _attn_delta(do, o, *, dlse=None, lse=None, sink=None, name="attn_delta"):
    NB, G, L, HD = do.shape
    bl = _row_tile(L, 1024)

    def body(*refs):
        refs = list(refs)
        sink_ref = refs.pop(0) if sink is not None else None
        do_ref, o_ref = refs.pop(0), refs.pop(0)
        dlse_ref = refs.pop(0) if dlse is not None else None
        lse_ref = refs.pop(0) if sink is not None else None
        delta_ref = refs.pop(0)
        b = pl.program_id(0)
        delta = jnp.sum(do_ref[0].astype(F32) * o_ref[0].astype(F32), axis=-1, keepdims=True)
        if dlse is not None:
            delta = delta - dlse_ref[0]
        delta_ref[0] = delta
        if sink is not None:
            ds_ref = refs.pop(0)

            @pl.when(pl.program_id(1) == 0)
            def _():
                ds_ref[...] = jnp.zeros_like(ds_ref)

            for g in range(G):
                ps = jnp.exp(sink_ref[b * G + g] - lse_ref[0, g]) * delta[g]
                ds_ref[0, g] -= jnp.sum(ps)

    blk = pl.BlockSpec((1, G, bl, HD), lambda b, n: (b, 0, n, 0))
    col = pl.BlockSpec((1, G, bl, 1), lambda b, n: (b, 0, n, 0))
    ins, specs = [do, o], [blk, blk]
    if dlse is not None:
        ins, specs = ins + [dlse], specs + [col]
    out_shape = jax.ShapeDtypeStruct((NB, G, L, 1), F32)
    out_specs = col
    if sink is not None:
        ins, specs = [sink] + ins + [lse], [pl.BlockSpec(memory_space=pltpu.SMEM)] + specs + [col]
        out_shape = (out_shape, jax.ShapeDtypeStruct((NB, G, 1, LANES), F32))
        out_specs = (col, pl.BlockSpec((1, G, 1, LANES), lambda b, n: (b, 0, 0, 0)))
    return _pc(body, name=name, out_shape=out_shape, grid=(NB, L // bl), in_specs=specs, out_specs=out_specs,
               compiler_params=_cp("parallel", "arbitrary"))(*ins)


def _flash_dq(q, k, v, do, lse, delta, *, radius=None, bq, tk, name="flash_dq"):
    NB, G, L, HD = q.shape
    Lk = k.shape[1]
    nq, nk = L // bq, Lk // tk
    banded = radius is not None
    nst = 3 if banded else nk
    rows = G * bq

    def kmap(b, n, j):
        return (b, jnp.clip(n - 1 + j, 0, nk - 1), 0) if banded else (b, j, 0)

    def body(q_ref, k_ref, v_ref, do_ref, lse_ref, dl_ref, dq_ref, acc_sc):
        n, j = pl.program_id(1), pl.program_id(2)

        @pl.when(j == 0)
        def _():
            acc_sc[...] = jnp.zeros_like(acc_sc)

        qv = q_ref[0].reshape(rows, HD)
        dov = do_ref[0].reshape(rows, HD)
        s = lax.dot_general(qv, k_ref[0], NT_DIMS, preferred_element_type=F32) * ATTN_SCALE
        p = jnp.exp(s - lse_ref[0].reshape(rows, 1))
        if banded:
            kb = n - 1 + j
            p = jnp.where(_band_mask(n, kb, (kb >= 0) & (kb < nk), bq, tk, rows, radius), p, 0.0)
        dp = lax.dot_general(dov, v_ref[0], NT_DIMS, preferred_element_type=F32)
        ds = p * (dp - dl_ref[0].reshape(rows, 1))
        acc_sc[...] += jnp.dot(ds.astype(BF16), k_ref[0], preferred_element_type=F32)

        @pl.when(j == nst - 1)
        def _():
            dq_ref[0] = (acc_sc[...] * ATTN_SCALE).reshape(G, bq, HD).astype(dq_ref.dtype)

    qspec = pl.BlockSpec((1, G, bq, HD), lambda b, n, j: (b, 0, n, 0))
    cspec = pl.BlockSpec((1, G, bq, 1), lambda b, n, j: (b, 0, n, 0))
    kspec = pl.BlockSpec((1, tk, HD), kmap)
    return _pc(body, name=name, out_shape=jax.ShapeDtypeStruct((NB, G, L, HD), F32), grid=(NB, nq, nst),
               in_specs=[qspec, kspec, kspec, qspec, cspec, cspec], out_specs=qspec,
               scratch_shapes=[pltpu.VMEM((rows, HD), F32)],
               compiler_params=_cp("parallel", "parallel", "arbitrary"))(q, k, v, do, lse, delta)


def _flash_dkv(q, k, v, do, lse, delta, *, radius=None, bq, tk, name="flash_dkv"):
    NB, G, L, HD = q.shape
    Lk = k.shape[1]
    nq, nk = L // bq, Lk // tk
    banded = radius is not None
    nst = 3 if banded else nq
    rows = G * bq

    def qmap(b, m, j):
        return (b, 0, jnp.clip(m - 1 + j, 0, nq - 1), 0) if banded else (b, 0, j, 0)

    def body(q_ref, k_ref, v_ref, do_ref, lse_ref, dl_ref, dk_ref, dv_ref, dk_sc, dv_sc):
        m, j = pl.program_id(1), pl.program_id(2)

        @pl.when(j == 0)
        def _():
            dk_sc[...] = jnp.zeros_like(dk_sc)
            dv_sc[...] = jnp.zeros_like(dv_sc)

        qv = q_ref[0].reshape(rows, HD)
        dov = do_ref[0].reshape(rows, HD)
        s = lax.dot_general(qv, k_ref[0], NT_DIMS, preferred_element_type=F32) * ATTN_SCALE
        p = jnp.exp(s - lse_ref[0].reshape(rows, 1))
        if banded:
            qb = m - 1 + j
            p = jnp.where(_band_mask(qb, m, (qb >= 0) & (qb < nq), bq, tk, rows, radius), p, 0.0)
        dv_sc[...] += lax.dot_general(p.astype(BF16), dov, TN_DIMS, preferred_element_type=F32)
        dp = lax.dot_general(dov, v_ref[0], NT_DIMS, preferred_element_type=F32)
        ds = p * (dp - dl_ref[0].reshape(rows, 1))
        dk_sc[...] += lax.dot_general(ds.astype(BF16), qv, TN_DIMS, preferred_element_type=F32)

        @pl.when(j == nst - 1)
        def _():
            dk_ref[0] = (dk_sc[...] * ATTN_SCALE).astype(dk_ref.dtype)
            dv_ref[0] = dv_sc[...].astype(dv_ref.dtype)

    qspec = pl.BlockSpec((1, G, bq, HD), qmap)
    cspec = pl.BlockSpec((1, G, bq, 1), qmap)
    kspec = pl.BlockSpec((1, tk, HD), lambda b, m, j: (b, m, 0))
    kv_shape = jax.ShapeDtypeStruct((NB, Lk, HD), F32)
    return _pc(body, name=name, out_shape=(kv_shape, kv_shape), grid=(NB, nk, nst),
               in_specs=[qspec, kspec, kspec, qspec, cspec, cspec], out_specs=(kspec, kspec),
               scratch_shapes=[pltpu.VMEM((tk, HD), F32), pltpu.VMEM((tk, HD), F32)],
               compiler_params=_cp("parallel", "parallel", "arbitrary"))(q, k, v, do, lse, delta)


def _flash_bwd_full(q, k, v, do, lse, delta, *, bq, tk, name="flash_bwd"):
    NB, G, L, HD = q.shape
    Lk = k.shape[1]
    nq, nk = L // bq, Lk // tk
    rows = G * bq

    def body(q_ref, k_ref, v_ref, do_ref, lse_ref, dl_ref, dqp_ref, dk_ref, dv_ref, dk_sc, dv_sc):
        n = pl.program_id(2)

        @pl.when(n == 0)
        def _():
            dk_sc[...] = jnp.zeros_like(dk_sc)
            dv_sc[...] = jnp.zeros_like(dv_sc)

        qs = q_ref[0].reshape(rows, HD) * ATTN_SCALE
        dov = do_ref[0].reshape(rows, HD)
        s = lax.dot_general(qs, k_ref[0], NT_DIMS, preferred_element_type=F32)
        p = jnp.exp(s - lse_ref[0].reshape(rows, 1))
        dp = lax.dot_general(dov, v_ref[0], NT_DIMS, preferred_element_type=F32)
        ds = (p * (dp - dl_ref[0].reshape(rows, 1))).astype(BF16)
        dv_sc[...] += lax.dot_general(p.astype(BF16), dov, TN_DIMS, preferred_element_type=F32)
        dk_sc[...] += lax.dot_general(ds, qs, TN_DIMS, preferred_element_type=F32)
        dqp_ref[0, 0] = (jnp.dot(ds, k_ref[0], preferred_element_type=F32) * ATTN_SCALE).reshape(G, bq, HD)

        @pl.when(n == nq - 1)
        def _():
            dk_ref[0] = dk_sc[...]
            dv_ref[0] = dv_sc[...]

    qspec = pl.BlockSpec((1, G, bq, HD), lambda b, m, n: (b, 0, n, 0))
    cspec = pl.BlockSpec((1, G, bq, 1), lambda b, m, n: (b, 0, n, 0))
    kspec = pl.BlockSpec((1, tk, HD), lambda b, m, n: (b, m, 0))
    kv_shape = jax.ShapeDtypeStruct((NB, Lk, HD), F32)
    dqp, dk, dv = _pc(body, name=name,
                      out_shape=(jax.ShapeDtypeStruct((nk, NB, G, L, HD), F32), kv_shape, kv_shape), grid=(NB, nk, nq),
                      in_specs=[qspec, kspec, kspec, qspec, cspec, cspec],
                      out_specs=(pl.BlockSpec((1, 1, G, bq, HD), lambda b, m, n: (m, b, 0, n, 0)), kspec, kspec),
                      scratch_shapes=[pltpu.VMEM((tk, HD), F32), pltpu.VMEM((tk, HD), F32)],
                      compiler_params=_cp("parallel", "parallel", "arbitrary"))(q, k, v, do, lse, delta)
    bl = _row_tile(L, 512)

    def sum_body(p_ref, o_ref):
        acc = p_ref[0, 0]
        for j in range(1, nk):
            acc = acc + p_ref[j, 0]
        o_ref[0] = acc

    dq = _pc(sum_body, name=name + "_sum", out_shape=jax.ShapeDtypeStruct((NB, G, L, HD), F32), grid=(NB, L // bl),
             in_specs=[pl.BlockSpec((nk, 1, G, bl, HD), lambda b, n: (0, b, 0, n, 0))],
             out_specs=pl.BlockSpec((1, G, bl, HD), lambda b, n: (b, 0, n, 0)),
             compiler_params=_cp("parallel", "parallel"))(dqp)
    return dq, dk, dv


def _window(L, blk, radius):
    if radius is None:
        return L, None
    W = min(L, blk + 2 * radius)
    assert blk % radius == 0 and (L - W) % radius == 0
    return W, lambda n: radius * jnp.clip(n * (blk // radius) - 1, 0, (L - W) // radius)


def _win_specs(G, W, HD, start, with_g):
    E = pl.Element
    st = (lambda n: 0) if start is None else start
    if with_g:
        return pl.BlockSpec((E(1), E(G), E(W), E(HD)), lambda b, n: (b, 0, st(n), 0))
    return pl.BlockSpec((E(1), E(W), E(HD)), lambda b, n: (b, st(n), 0))


def _sink_rows(sink_ref, b, G, per, rows):
    head = lax.div(lax.broadcasted_iota(jnp.int32, (rows, 1), 0), per)
    sk = jnp.zeros((rows, 1), F32)
    for g in range(G):
        sk = jnp.where(head == g, sink_ref[b * G + g], sk)
    return sk


def _win_fwd(q, k, v, *, radius=None, sink=None, bq, name="win_fwd"):
    NB, G, L, HD = q.shape
    Lk = k.shape[1]
    W, start = _window(Lk, bq, radius)
    rows = G * bq

    def body(*refs):
        if sink is not None:
            sink_ref, *refs = refs
        q_ref, k_ref, v_ref, o_ref, lse_ref = refs
        b, n = pl.program_id(0), pl.program_id(1)
        qv = q_ref[0].reshape(rows, HD) * ATTN_SCALE
        s = lax.dot_general(qv, k_ref[0], NT_DIMS, preferred_element_type=F32)
        if radius is not None:
            qpos = n * bq + lax.rem(lax.broadcasted_iota(jnp.int32, (rows, 1), 0), bq)
            kpos = start(n) + lax.broadcasted_iota(jnp.int32, (1, W), 1)
            s = jnp.where(jnp.abs(qpos - kpos) <= radius, s, NEG)
        m = jnp.max(s, axis=1, keepdims=True)
        if sink is not None:
            sk = _sink_rows(sink_ref, b, G, bq, rows)
            m = jnp.maximum(m, sk)
        p = jnp.exp(s - m)
        l = jnp.sum(p, axis=1, keepdims=True)
        if sink is not None:
            l = l + jnp.exp(sk - m)
        o = jnp.dot(p.astype(BF16), v_ref[0], preferred_element_type=F32) / l
        o_ref[0] = o.reshape(G, bq, HD).astype(o_ref.dtype)
        lse_ref[0] = (m + jnp.log(l)).reshape(G, bq, 1)

    qspec = pl.BlockSpec((1, G, bq, HD), lambda b, n: (b, 0, n, 0))
    kspec = _win_specs(G, W, HD, start, False)
    ins, specs = [q, k, v], [qspec, kspec, kspec]
    if sink is not None:
        ins, specs = [sink] + ins, [pl.BlockSpec(memory_space=pltpu.SMEM)] + specs
    return _pc(body, name=name,
               out_shape=(jax.ShapeDtypeStruct((NB, G, L, HD), BF16), jax.ShapeDtypeStruct((NB, G, L, 1), F32)),
               grid=(NB, L // bq), in_specs=specs,
               out_specs=(qspec, pl.BlockSpec((1, G, bq, 1), lambda b, n: (b, 0, n, 0))),
               compiler_params=_cp("parallel", "parallel"))(*ins)


def _win_dq(q, k, v, do, lse, delta, *, radius=None, bq, name="win_dq"):
    NB, G, L, HD = q.shape
    Lk = k.shape[1]
    W, start = _window(Lk, bq, radius)
    rows = G * bq

    def body(q_ref, k_ref, v_ref, do_ref, lse_ref, dl_ref, dq_ref):
        n = pl.program_id(1)
        qv = q_ref[0].reshape(rows, HD) * ATTN_SCALE
        dov = do_ref[0].reshape(rows, HD)
        s = lax.dot_general(qv, k_ref[0], NT_DIMS, preferred_element_type=F32)
        p = jnp.exp(s - lse_ref[0].reshape(rows, 1))
        if radius is not None:
            qpos = n * bq + lax.rem(lax.broadcasted_iota(jnp.int32, (rows, 1), 0), bq)
            kpos = start(n) + lax.broadcasted_iota(jnp.int32, (1, W), 1)
            p = jnp.where(jnp.abs(qpos - kpos) <= radius, p, 0.0)
        dp = lax.dot_general(dov, v_ref[0], NT_DIMS, preferred_element_type=F32)
        ds = p * (dp - dl_ref[0].reshape(rows, 1))
        dq = jnp.dot(ds.astype(BF16), k_ref[0], preferred_element_type=F32) * ATTN_SCALE
        dq_ref[0] = dq.reshape(G, bq, HD)

    qspec = pl.BlockSpec((1, G, bq, HD), lambda b, n: (b, 0, n, 0))
    cspec = pl.BlockSpec((1, G, bq, 1), lambda b, n: (b, 0, n, 0))
    kspec = _win_specs(G, W, HD, start, False)
    return _pc(body, name=name, out_shape=jax.ShapeDtypeStruct((NB, G, L, HD), F32), grid=(NB, L // bq),
               in_specs=[qspec, kspec, kspec, qspec, cspec, cspec], out_specs=qspec,
               compiler_params=_cp("parallel", "parallel"))(q, k, v, do, lse, delta)


def _win_dkv(q, k, v, do, lse, delta, *, radius, bk, name="win_dkv"):
    NB, G, L, HD = q.shape
    Lk = k.shape[1]
    W, start = _window(L, bk, radius)
    rows = G * W

    def body(q_ref, k_ref, v_ref, do_ref, lse_ref, dl_ref, dk_ref, dv_ref):
        m = pl.program_id(1)
        qv = q_ref[0].reshape(rows, HD)
        dov = do_ref[0].reshape(rows, HD)
        s = lax.dot_general(qv * ATTN_SCALE, k_ref[0], NT_DIMS, preferred_element_type=F32)
        p = jnp.exp(s - lse_ref[0].reshape(rows, 1))
        qpos = start(m) + lax.rem(lax.broadcasted_iota(jnp.int32, (rows, 1), 0), W)
        kpos = m * bk + lax.broadcasted_iota(jnp.int32, (1, bk), 1)
        p = jnp.where(jnp.abs(qpos - kpos) <= radius, p, 0.0)
        dv_ref[0] = lax.dot_general(p.astype(BF16), dov, TN_DIMS, preferred_element_type=F32)
        dp = lax.dot_general(dov, v_ref[0], NT_DIMS, preferred_element_type=F32)
        ds = p * (dp - dl_ref[0].reshape(rows, 1))
        dk_ref[0] = lax.dot_general(ds.astype(BF16), qv, TN_DIMS, preferred_element_type=F32) * ATTN_SCALE

    qspec = _win_specs(G, W, HD, start, True)
    cspec = _win_specs(G, W, 1, start, True)
    kspec = pl.BlockSpec((1, bk, HD), lambda b, m: (b, m, 0))
    kv_shape = jax.ShapeDtypeStruct((NB, Lk, HD), F32)
    return _pc(body, name=name, out_shape=(kv_shape, kv_shape), grid=(NB, Lk // bk),
               in_specs=[qspec, kspec, kspec, qspec, cspec, cspec], out_specs=(kspec, kspec),
               compiler_params=_cp("parallel", "parallel"))(q, k, v, do, lse, delta)


def _attention(q, k, v, *, radius=None, sink=None, bq, tk=None, tag):
    if tk is None:
        return _win_fwd(q, k, v, radius=radius, sink=sink, bq=bq, name=f"win_fwd_{tag}")
    return _flash_fwd(q, k, v, radius=radius, sink=sink, bq=bq, tk=tk, name=f"flash_fwd_{tag}")


def _attention_bwd(q, k, v, o, lse, do, *, radius=None, sink=None, dlse=None, bq, tk=None, tag):
    if sink is not None:
        delta, ds = _attn_delta(do, o, lse=lse, sink=sink, name=f"attn_delta_{tag}")
        dsink = ds[:, :, 0, 0].reshape(-1)
    else:
        delta, dsink = _attn_delta(do, o, dlse=dlse, name=f"attn_delta_{tag}"), None
    if tk is not None and radius is None:
        return (*_flash_bwd_full(q, k, v, do, lse, delta, bq=bq, tk=tk, name=f"flash_bwd_{tag}"), dsink)
    if tk is None:
        dq = _win_dq(q, k, v, do, lse, delta, radius=radius, bq=bq, name=f"win_dq_{tag}")
    else:
        dq = _flash_dq(q, k, v, do, lse, delta, radius=radius, bq=bq, tk=tk, name=f"flash_dq_{tag}")
    if tk is None and radius is not None:
        dk, dv = _win_dkv(q, k, v, do, lse, delta, radius=radius, bk=bq, name=f"win_dkv_{tag}")
    else:
        dk, dv = _flash_dkv(q, k, v, do, lse, delta, radius=radius, bq=bq, tk=tk or k.shape[1], name=f"flash_dkv_{tag}")
    return dq, dk, dv, dsink


CHAIN_ROWS = 128


def _skewed(n, stages):
    for t in range(n + len(stages) - 1):
        for s, stage in enumerate(stages):
            if 0 <= t - s < n:
                stage(t - s)


def _chain_slices(G, bq):
    cr = min(CHAIN_ROWS, bq)
    per = bq // cr
    return [(c // per, slice((c % per) * cr, (c % per + 1) * cr), slice(c * cr, (c + 1) * cr)) for c in range(G * per)]


def _v_ones(v):
    return jnp.concatenate([v, jnp.ones(v.shape, v.dtype)], axis=1)


def _attn_fwd_full(q, k, v, *, bq, tk, name="attn_fwd_full"):
    NB, G, L, HD = q.shape
    Lk = k.shape[1]
    nq, nk = L // bq, Lk // tk
    rows = G * bq
    chains = _chain_slices(G, bq)

    def body(q_ref, k_ref, v_ref, o_ref, lse_ref, m_sc, acc_sc):
        j = pl.program_id(2)

        @pl.when(j == 0)
        def _():
            m_sc[...] = jnp.full_like(m_sc, NEG)
            acc_sc[...] = jnp.zeros_like(acc_sc)

        kk = k_ref[0]
        vv = _v_ones(v_ref[0])
        st = [dict() for _ in chains]

        def scores(c):
            g, hr, _ = chains[c]
            st[c]["s"] = lax.dot_general(q_ref[0, g, hr, :] * ATTN_SCALE, kk, NT_DIMS, preferred_element_type=F32)

        def softmax(c):
            sl = chains[c][2]
            m_prev = m_sc[sl]
            m_new = jnp.maximum(m_prev, jnp.max(st[c]["s"], axis=1, keepdims=True))
            st[c]["p"] = jnp.exp(st[c].pop("s") - m_new).astype(BF16)
            st[c]["alpha"] = jnp.exp(m_prev - m_new)
            m_sc[sl] = m_new

        def values(c):
            sl = chains[c][2]
            acc_sc[sl] = st[c].pop("alpha") * acc_sc[sl] + jnp.dot(st[c].pop("p"), vv, preferred_element_type=F32)

        _skewed(len(chains), (scores, softmax, values))

        @pl.when(j == nk - 1)
        def _():
            acc = acc_sc[...]
            l = acc[:, HD:HD + 1]
            o_ref[0] = (acc[:, :HD] / l).reshape(G, bq, HD).astype(o_ref.dtype)
            lse_ref[0] = (m_sc[...] + jnp.log(l)).reshape(G, bq, 1)

    qspec = pl.BlockSpec((1, G, bq, HD), lambda b, n, j: (b, 0, n, 0))
    kspec = pl.BlockSpec((1, tk, HD), lambda b, n, j: (b, j, 0))
    return _pc(body, name=name,
               out_shape=(jax.ShapeDtypeStruct((NB, G, L, HD), BF16), jax.ShapeDtypeStruct((NB, G, L, 1), F32)),
               grid=(NB, nq, nk), in_specs=[qspec, kspec, kspec],
               out_specs=(qspec, pl.BlockSpec((1, G, bq, 1), lambda b, n, j: (b, 0, n, 0))),
               scratch_shapes=[pltpu.VMEM((rows, 1), F32), pltpu.VMEM((rows, 2 * HD), F32)],
               compiler_params=_cp("parallel", "parallel", "arbitrary"))(q, k, v)


def _attn_bwd_full(q, k, v, do, lse, o, *, bq, tk, name="attn_bwd_full"):
    NB, G, L, HD = q.shape
    Lk = k.shape[1]
    nq, nk = L // bq, Lk // tk
    rows = G * bq
    chains = _chain_slices(G, bq)

    def body(q_ref, k_ref, v_ref, do_ref, lse_ref, o_ref, dqp_ref, dk_ref, dv_ref, dk_sc, dv_sc):
        n = pl.program_id(2)

        @pl.when(n == 0)
        def _():
            dk_sc[...] = jnp.zeros_like(dk_sc)
            dv_sc[...] = jnp.zeros_like(dv_sc)

        kk, vv = k_ref[0], v_ref[0]
        st = [dict() for _ in chains]

        def scores(c):
            g, hr, _ = chains[c]
            st[c]["q"] = q_ref[0, g, hr, :] * ATTN_SCALE
            st[c]["do"] = do_ref[0, g, hr, :]
            st[c]["s"] = lax.dot_general(st[c]["q"], kk, NT_DIMS, preferred_element_type=F32)
            st[c]["dp"] = lax.dot_general(st[c]["do"], vv, NT_DIMS, preferred_element_type=F32)
            st[c]["delta"] = jnp.sum(st[c]["do"].astype(F32) * o_ref[0, g, hr, :].astype(F32), axis=-1, keepdims=True)

        def softmax(c):
            g, hr, _ = chains[c]
            p = jnp.exp(st[c].pop("s") - lse_ref[0, g, hr, :])
            st[c]["ds"] = (p * (st[c].pop("dp") - st[c].pop("delta"))).astype(BF16)
            st[c]["p"] = p.astype(BF16)

        def grads(c):
            g, hr, _ = chains[c]
            ds = st[c].pop("ds")
            dv_sc[...] += lax.dot_general(st[c].pop("p"), st[c].pop("do"), TN_DIMS, preferred_element_type=F32)
            dk_sc[...] += lax.dot_general(ds, st[c].pop("q"), TN_DIMS, preferred_element_type=F32)
            dqp_ref[0, 0, g, hr, :] = jnp.dot(ds, kk, preferred_element_type=F32) * ATTN_SCALE

        _skewed(len(chains), (scores, softmax, grads))

        @pl.when(n == nq - 1)
        def _():
            dk_ref[0] = dk_sc[...].astype(dk_ref.dtype)
            dv_ref[0] = dv_sc[...].astype(dv_ref.dtype)

    qspec = pl.BlockSpec((1, G, bq, HD), lambda b, m, n: (b, 0, n, 0))
    cspec = pl.BlockSpec((1, G, bq, 1), lambda b, m, n: (b, 0, n, 0))
    kspec = pl.BlockSpec((1, tk, HD), lambda b, m, n: (b, m, 0))
    kv_shape = jax.ShapeDtypeStruct((NB, Lk, HD), BF16)
    dqp, dk, dv = _pc(body, name=name,
                      out_shape=(jax.ShapeDtypeStruct((nk, NB, G, L, HD), F32), kv_shape, kv_shape), grid=(NB, nk, nq),
                      in_specs=[qspec, kspec, kspec, qspec, cspec, qspec],
                      out_specs=(pl.BlockSpec((1, 1, G, bq, HD), lambda b, m, n: (m, b, 0, n, 0)), kspec, kspec),
                      scratch_shapes=[pltpu.VMEM((tk, HD), F32), pltpu.VMEM((tk, HD), F32)],
                      compiler_params=_cp("parallel", "parallel", "arbitrary"))(q, k, v, do, lse, o)
    if nk == 1:
        return dqp[0].astype(BF16), dk, dv
    bl = _row_tile(L, 512)

    def sum_body(p_ref, o_ref):
        acc = p_ref[0, 0]
        for j in range(1, nk):
            acc = acc + p_ref[j, 0]
        o_ref[0] = acc.astype(o_ref.dtype)

    dq = _pc(sum_body, name=name + "_sum", out_shape=jax.ShapeDtypeStruct((NB, G, L, HD), BF16), grid=(NB, L // bl),
             in_specs=[pl.BlockSpec((nk, 1, G, bl, HD), lambda b, n: (0, b, 0, n, 0))],
             out_specs=pl.BlockSpec((1, G, bl, HD), lambda b, n: (b, 0, n, 0)),
             compiler_params=_cp("parallel", "parallel"))(dqp)
    return dq, dk, dv


def _attn_fwd_win(q, k, v, *, radius, sink=None, bq, carry=None, name="attn_fwd_win"):
    NB, G, L, HD = q.shape
    W, start = _window(k.shape[1], bq, radius)
    chains = _chain_slices(G, bq)

    def body(*refs):
        if sink is not None:
            sink_ref, *refs = refs
        q_ref, k_ref, v_ref, o_ref, lse_ref = refs
        b, n = pl.program_id(0), pl.program_id(1)
        kk = k_ref[0]
        vv = _v_ones(v_ref[0])
        st = [dict() for _ in chains]

        def scores(c):
            g, hr, _ = chains[c]
            s = lax.dot_general(q_ref[0, g, hr, :] * ATTN_SCALE, kk, NT_DIMS, preferred_element_type=F32)
            if radius is not None:
                qpos = n * bq + hr.start + lax.broadcasted_iota(jnp.int32, (hr.stop - hr.start, 1), 0)
                kpos = start(n) + lax.broadcasted_iota(jnp.int32, (1, W), 1)
                s = jnp.where(jnp.abs(qpos - kpos) <= radius, s, NEG)
            st[c]["s"] = s

        def softmax(c):
            g = chains[c][0]
            m = jnp.max(st[c]["s"], axis=1, keepdims=True)
            if sink is not None:
                m = jnp.maximum(m, sink_ref[b * G + g])
            st[c]["p"] = jnp.exp(st[c].pop("s") - m).astype(BF16)
            st[c]["m"] = m

        def values(c):
            g, hr, _ = chains[c]
            acc = jnp.dot(st[c].pop("p"), vv, preferred_element_type=F32)
            m = st[c].pop("m")
            l = acc[:, HD:HD + 1]
            if sink is not None:
                l = l + jnp.exp(sink_ref[b * G + g] - m)
            o_ref[0, g, hr, :] = (acc[:, :HD] / l).astype(o_ref.dtype)
            lse_ref[0, g, hr, :] = m + jnp.log(l)

        _skewed(len(chains), (scores, softmax, values))

    qspec = pl.BlockSpec((1, G, bq, HD), lambda b, n: (b, 0, n, 0))
    kspec = _win_specs(G, W, HD, start, False)
    ins, specs = [q, k, v], [qspec, kspec, kspec]
    if sink is not None:
        ins, specs = [sink] + ins, [pl.BlockSpec(memory_space=pltpu.SMEM)] + specs
    return _call(body, carry, ins, name=name,
                 out_shape=(jax.ShapeDtypeStruct((NB, G, L, HD), BF16), jax.ShapeDtypeStruct((NB, G, L, 1), F32)),
                 grid=(NB, L // bq), in_specs=specs,
                 out_specs=(qspec, pl.BlockSpec((1, G, bq, 1), lambda b, n: (b, 0, n, 0))), sem=("parallel", "parallel"))


def _attn_dq_win(q, k, v, do, lse, delta, *, radius, bq, name="attn_dq_win"):
    NB, G, L, HD = q.shape
    W, start = _window(L, bq, radius)
    chains = _chain_slices(G, bq)

    def body(q_ref, k_ref, v_ref, do_ref, lse_ref, dl_ref, dq_ref):
        n = pl.program_id(1)
        kk, vv = k_ref[0], v_ref[0]
        kpos = start(n) + lax.broadcasted_iota(jnp.int32, (1, W), 1)
        st = [dict() for _ in chains]

        def scores(c):
            g, hr, _ = chains[c]
            st[c]["s"] = lax.dot_general(q_ref[0, g, hr, :] * ATTN_SCALE, kk, NT_DIMS, preferred_element_type=F32)
            st[c]["dp"] = lax.dot_general(do_ref[0, g, hr, :], vv, NT_DIMS, preferred_element_type=F32)

        def softmax(c):
            g, hr, _ = chains[c]
            qpos = n * bq + hr.start + lax.broadcasted_iota(jnp.int32, (hr.stop - hr.start, 1), 0)
            p = jnp.where(jnp.abs(qpos - kpos) <= radius, jnp.exp(st[c].pop("s") - lse_ref[0, g, hr, :]), 0.0)
            st[c]["ds"] = (p * (st[c].pop("dp") - dl_ref[0, g, hr, :])).astype(BF16)

        def grads(c):
            g, hr, _ = chains[c]
            dq = jnp.dot(st[c].pop("ds"), kk, preferred_element_type=F32) * ATTN_SCALE
            dq_ref[0, g, hr, :] = dq.astype(dq_ref.dtype)

        _skewed(len(chains), (scores, softmax, grads))

    qspec = pl.BlockSpec((1, G, bq, HD), lambda b, n: (b, 0, n, 0))
    cspec = pl.BlockSpec((1, G, bq, 1), lambda b, n: (b, 0, n, 0))
    kspec = _win_specs(G, W, HD, start, False)
    return _pc(body, name=name, out_shape=jax.ShapeDtypeStruct((NB, G, L, HD), BF16), grid=(NB, L // bq),
               in_specs=[qspec, kspec, kspec, qspec, cspec, cspec], out_specs=qspec,
               compiler_params=_cp("parallel", "parallel"))(q, k, v, do, lse, delta)


def _attn_dkv_win(q, k, v, do, lse, delta, *, radius, bk, name="attn_dkv_win"):
    NB, G, L, HD = q.shape
    W, start = _window(L, bk, radius)
    chains = _chain_slices(G, W)

    def body(q_ref, k_ref, v_ref, do_ref, lse_ref, dl_ref, dk_ref, dv_ref):
        m = pl.program_id(1)
        kk, vv = k_ref[0], v_ref[0]
        kpos = m * bk + lax.broadcasted_iota(jnp.int32, (1, bk), 1)
        st = [dict() for _ in chains]
        out = dict(dk=jnp.zeros((bk, HD), F32), dv=jnp.zeros((bk, HD), F32))

        def scores(c):
            g, hr, _ = chains[c]
            st[c]["q"] = q_ref[0, g, hr, :] * ATTN_SCALE
            st[c]["do"] = do_ref[0, g, hr, :]
            st[c]["s"] = lax.dot_general(st[c]["q"], kk, NT_DIMS, preferred_element_type=F32)
            st[c]["dp"] = lax.dot_general(st[c]["do"], vv, NT_DIMS, preferred_element_type=F32)

        def softmax(c):
            g, hr, _ = chains[c]
            qpos = start(m) + hr.start + lax.broadcasted_iota(jnp.int32, (hr.stop - hr.start, 1), 0)
            p = jnp.where(jnp.abs(qpos - kpos) <= radius, jnp.exp(st[c].pop("s") - lse_ref[0, g, hr, :]), 0.0)
            st[c]["ds"] = (p * (st[c].pop("dp") - dl_ref[0, g, hr, :])).astype(BF16)
            st[c]["p"] = p.astype(BF16)

        def grads(c):
            out["dv"] = out["dv"] + lax.dot_general(st[c].pop("p"), st[c].pop("do"), TN_DIMS, preferred_element_type=F32)
            out["dk"] = out["dk"] + lax.dot_general(st[c].pop("ds"), st[c].pop("q"), TN_DIMS, preferred_element_type=F32)

        _skewed(len(chains), (scores, softmax, grads))
        dk_ref[0] = out["dk"].astype(dk_ref.dtype)
        dv_ref[0] = out["dv"].astype(dv_ref.dtype)

    qspec = _win_specs(G, W, HD, start, True)
    cspec = _win_specs(G, W, 1, start, True)
    kspec = pl.BlockSpec((1, bk, HD), lambda b, m: (b, m, 0))
    kv_shape = jax.ShapeDtypeStruct((NB, L, HD), BF16)
    return _pc(body, name=name, out_shape=(kv_shape, kv_shape), grid=(NB, L // bk),
               in_specs=[qspec, kspec, kspec, qspec, cspec, cspec], out_specs=(kspec, kspec),
               compiler_params=_cp("parallel", "parallel"))(q, k, v, do, lse, delta)


def _attn(q, k, v, *, radius=None, sink=None, bq, tk=None, carry=None, tag):
    if radius is None and tk < k.shape[1]:
        return _attn_fwd_full(q, k, v, bq=bq, tk=tk, name=f"attn_fwd_{tag}")
    return _attn_fwd_win(q, k, v, radius=radius, sink=sink, bq=bq, carry=carry, name=f"attn_fwd_{tag}")


def _attn_bwd(q, k, v, o, lse, do, *, radius=None, sink=None, dlse=None, bq, tk=None, tag):
    if radius is None:
        return (*_attn_bwd_full(q, k, v, do, lse, o, bq=bq, tk=tk, name=f"attn_bwd_{tag}"), None)
    if sink is not None:
        delta, ds = _attn_delta(do, o, lse=lse, sink=sink, name=f"attn_delta_{tag}")
        dsink = ds[:, :, 0, 0].reshape(-1)
    else:
        delta, dsink = _attn_delta(do, o, dlse=dlse, name=f"attn_delta_{tag}"), None
    dq = _attn_dq_win(q, k, v, do, lse, delta, radius=radius, bq=bq, name=f"attn_dq_{tag}")
    dk, dv = _attn_dkv_win(q, k, v, do, lse, delta, radius=radius, bk=bq, name=f"attn_dkv_{tag}")
    return dq, dk, dv, dsink


def _combine_fwd(o, lse, name="combine_fwd"):
    H, S, HD = o.shape
    tm = _row_tile(S, 512)

    def body(o_ref, lse_ref, t_ref):
        for g in range(GQA_GROUP):
            hs = [kv * GQA_GROUP + g for kv in range(N_KV_HEADS)]
            ls = [lse_ref[h] for h in hs]
            mx = functools.reduce(jnp.maximum, ls)
            es = [jnp.exp(l - mx) for l in ls]
            den = functools.reduce(jnp.add, es)
            for h, e in zip(hs, es):
                t_ref[h] = (o_ref[h].astype(F32) * (e / den)).astype(t_ref.dtype)

    blk = pl.BlockSpec((H, tm, HD), lambda i: (0, i, 0))
    col = pl.BlockSpec((H, tm, 1), lambda i: (0, i, 0))
    return _pc(body, name=name, out_shape=jax.ShapeDtypeStruct((H, S, HD), BF16), grid=(S // tm,),
               in_specs=[blk, col], out_specs=blk, compiler_params=_cp("parallel"))(o, lse)


def _combine_bwd(dt, o, lse, name="combine_bwd"):
    H, S, HD = o.shape
    tm = _row_tile(S, 512)

    def body(dt_ref, o_ref, lse_ref, do_ref, dlse_ref):
        for g in range(GQA_GROUP):
            hs = [kv * GQA_GROUP + g for kv in range(N_KV_HEADS)]
            ls = [lse_ref[h] for h in hs]
            mx = functools.reduce(jnp.maximum, ls)
            es = [jnp.exp(l - mx) for l in ls]
            den = functools.reduce(jnp.add, es)
            al = [e / den for e in es]
            dts = [dt_ref[h].astype(F32) for h in hs]
            da = [jnp.sum(d * o_ref[h].astype(F32), axis=-1, keepdims=True) for h, d in zip(hs, dts)]
            dot = functools.reduce(jnp.add, [a * d for a, d in zip(al, da)])
            for h, a, d, dd in zip(hs, al, da, dts):
                do_ref[h] = (dd * a).astype(do_ref.dtype)
                dlse_ref[h] = a * (d - dot)

    blk = pl.BlockSpec((H, tm, HD), lambda i: (0, i, 0))
    col = pl.BlockSpec((H, tm, 1), lambda i: (0, i, 0))
    return _pc(body, name=name,
               out_shape=(jax.ShapeDtypeStruct((H, S, HD), BF16), jax.ShapeDtypeStruct((H, S, 1), F32)),
               grid=(S // tm,), in_specs=[blk, blk, col], out_specs=(blk, col), compiler_params=_cp("parallel"))(dt, o, lse)


def _position():
    x, y, c = lax.axis_index("x"), lax.axis_index("y"), lax.axis_index("c")
    return x, y, c


def _peer(pos, k):
    x, y, c = pos
    return (1 - x if k & 4 else x, 1 - y if k & 2 else y, 1 - c if k & 1 else c)


def _linear(p):
    return 4 * p[0] + 2 * p[1] + p[2]


def _exchange_steps(s_ref, r_ref, send_sems, recv_sems, local_sem, gather):
    pos = _position()
    me = _linear(pos)
    own = pltpu.make_async_copy(s_ref if gather else s_ref.at[me], r_ref.at[me], local_sem)
    peers = range(1, N_DEV)

    def sems(k):
        return dict(send_sem=send_sems.at[k - 1], recv_sem=recv_sems.at[k - 1], device_id=_peer(pos, k), device_id_type=MESH)

    def send(k):
        src = s_ref if gather else s_ref.at[_linear(_peer(pos, k))]
        return pltpu.make_async_remote_copy(src_ref=src, dst_ref=r_ref.at[me], **sems(k))

    def arrival(k):
        slot = r_ref.at[_linear(_peer(pos, k))]
        return pltpu.make_async_remote_copy(src_ref=slot, dst_ref=slot, **sems(k))

    def start():
        own.start()
        for k in peers:
            send(k).start()

    def wait():
        for k in peers:
            arrival(k).wait_recv()
        for k in peers:
            send(k).wait_send()
        own.wait()

    return start, wait


EXCHANGE_SEMS = [pltpu.SemaphoreType.DMA((N_DEV - 1,)), pltpu.SemaphoreType.DMA((N_DEV - 1,)), pltpu.SemaphoreType.DMA]


def _exchange(buf, gather, name):
    def body(s_ref, r_ref, *sems):
        start, wait = _exchange_steps(s_ref, r_ref, *sems, gather)
        start()
        wait()

    hbm = pl.BlockSpec(memory_space=pltpu.HBM)
    out_shape = ((N_DEV,) + buf.shape) if gather else buf.shape
    return _pc(body, name=name, out_shape=jax.ShapeDtypeStruct(out_shape, buf.dtype), in_specs=[hbm], out_specs=hbm,
               scratch_shapes=list(EXCHANGE_SEMS))(buf)


def _call(body, carry, ins, *, name, out_shape, grid, in_specs, out_specs, scratch_shapes=(), sem):
    if carry is None:
        return _pc(body, name=name, out_shape=tuple(out_shape), grid=grid, in_specs=list(in_specs),
                   out_specs=tuple(out_specs), scratch_shapes=list(scratch_shapes), compiler_params=_cp(*sem))(*ins)
    buf, gather = carry
    n_in, n_out, n_sc = len(ins), len(out_shape), len(scratch_shapes)

    def wrapped(*refs):
        in_refs, buf_ref = refs[:n_in], refs[n_in]
        out_refs, recv_ref = refs[n_in + 1:n_in + 1 + n_out], refs[n_in + 1 + n_out]
        rest = refs[n_in + 2 + n_out:]
        first = functools.reduce(jnp.logical_and, [pl.program_id(a) == 0 for a in range(len(grid))])
        last = functools.reduce(jnp.logical_and, [pl.program_id(a) == grid[a] - 1 for a in range(len(grid))])

        @pl.when(first)
        def _():
            _exchange_steps(buf_ref, recv_ref, *rest[n_sc:], gather)[0]()

        body(*in_refs, *out_refs, *rest[:n_sc])

        @pl.when(last)
        def _():
            _exchange_steps(buf_ref, recv_ref, *rest[n_sc:], gather)[1]()

    hbm = pl.BlockSpec(memory_space=pltpu.HBM)
    recv_shape = ((N_DEV,) + buf.shape) if gather else buf.shape
    return _pc(wrapped, name=name, out_shape=(*out_shape, jax.ShapeDtypeStruct(recv_shape, buf.dtype)), grid=grid,
               in_specs=[*in_specs, hbm], out_specs=(*out_specs, hbm), scratch_shapes=[*scratch_shapes, *EXCHANGE_SEMS],
               compiler_params=_cp(*(("arbitrary",) * len(grid))))(*ins, buf)


def _reduce_adamw(recv, w, m, v, name):
    _, R, C = recv.shape
    tr = _row_tile(R, 512)

    def body(r_ref, w_ref, m_ref, v_ref, g_ref, d_ref, nm_ref, nv_ref):
        g = r_ref[0].astype(F32)
        for j in range(1, N_DEV):
            g = g + r_ref[j].astype(F32)
        g_ref[...] = g
        nm = ADAM_B1 * m_ref[...] + (1.0 - ADAM_B1) * g
        nv = ADAM_B2 * v_ref[...] + (1.0 - ADAM_B2) * jnp.square(g)
        m_hat = nm / (1.0 - ADAM_B1 ** ADAM_STEP)
        v_hat = nv / (1.0 - ADAM_B2 ** ADAM_STEP)
        d_ref[...] = -ADAM_LR * (m_hat / (jnp.sqrt(v_hat) + ADAM_EPS) + ADAM_WD * w_ref[...])
        nm_ref[...] = nm
        nv_ref[...] = nv

    row = pl.BlockSpec((tr, C), lambda i: (i, 0))
    out = jax.ShapeDtypeStruct((R, C), F32)
    return _pc(body, name=name, out_shape=(out, out, out, out), grid=(R // tr,),
               in_specs=[pl.BlockSpec((N_DEV, tr, C), lambda i: (0, i, 0)), row, row, row],
               out_specs=(row, row, row, row), compiler_params=_cp("parallel"))(recv, w, m, v)


BIG = (("w_in", 2), ("w_mem_kv", 1), ("w_o", 1), ("w_gate_up", 2), ("w_down", 1))
SMALL = ("mem_norm_g", "g_mix_pre", "g_mix_post", "attn_sink", "qk_norm_g", "g_ffn_pre", "g_ffn_post")
SMALL_W = 1024


def _pack_local(shards, dtype):
    return jnp.concatenate([s.astype(dtype).reshape(-1, LANES) for s in shards], axis=0)


def _unpack_local(flat, shapes):
    out, r = [], 0
    for shp in shapes:
        n = shp[0] * shp[1] * shp[2] // LANES
        out.append(flat[r:r + n].reshape(shp))
        r += n
    return out


def _unpack_gathered(g, shapes):
    out, r = [], 0
    for (name, dim), shp in zip(BIG, shapes):
        n = shp[0] * shp[1] * shp[2] // LANES
        t = g[:, r:r + n].reshape((N_DEV,) + tuple(shp))
        if dim == 2:
            t = t.transpose(1, 2, 0, 3).reshape(shp[0], shp[1], N_DEV * shp[2])
        else:
            t = t.transpose(1, 0, 2, 3).reshape(shp[0], N_DEV * shp[1], shp[2])
        out.append(t)
        r += n
    return out


def _pack_for_scatter(full, shapes, dtype):
    parts = []
    for (name, dim), shp, t in zip(BIG, shapes, full):
        if dim == 2:
            t = t.reshape(shp[0], shp[1], N_DEV, shp[2]).transpose(2, 0, 1, 3)
        else:
            t = t.reshape(shp[0], N_DEV, shp[1], shp[2]).transpose(1, 0, 2, 3)
        parts.append(t.astype(dtype).reshape(N_DEV, -1, LANES))
    return jnp.concatenate(parts, axis=1)


def _pack_small(arrs):
    flat = jnp.concatenate([a.reshape(-1) for a in arrs])
    pad = (-flat.shape[0]) % (8 * SMALL_W)
    return jnp.pad(flat, (0, pad)).reshape(-1, SMALL_W)


def _unpack_small(flat, shapes):
    flat = flat.reshape(-1)
    out, r = [], 0
    for shp in shapes:
        n = 1
        for d in shp:
            n *= d
        out.append(flat[r:r + n].reshape(shp))
        r += n
    return out


def _heads(t, nb, g):
    S = t.shape[0]
    return t.reshape(S, nb, g, HEAD_DIM).transpose(1, 2, 0, 3)


def _unheads(t):
    nb, g, S, hd = t.shape
    return t.transpose(2, 0, 1, 3).reshape(S, nb * g * hd)


def _dilate(t, dil):
    S = t.shape[0]
    g = t.shape[1] // HEAD_DIM
    return t.reshape(S // dil, dil, g, HEAD_DIM).transpose(1, 2, 0, 3)


def _undilate(t):
    dil, g, L, w = t.shape
    return t.transpose(1, 2, 0, 3).reshape(g, L * dil, w)


FULL_BQ_FWD, FULL_TK_FWD = 512, 4096
FULL_BQ_BWD, FULL_TK_BWD = 1024, 2048


def _mixer_fwd(kind, pr, sink, li, carry=None):
    S = pr.shape[0]
    if kind == 0:
        q = _heads(pr[:, :Q_W], N_KV_HEADS, GQA_GROUP)
        k = _heads(pr[:, Q_W:QK_W], N_KV_HEADS, 1)[:, 0]
        v = _heads(pr[:, QK_W:QK_W + KV_W], N_KV_HEADS, 1)[:, 0]
        o, lse, *arrived = _attn(q, k, v, radius=A_RADIUS, sink=sink, bq=min(256, S), carry=carry, tag=f"a{li}")
        return _unheads(o), (q, k, v, o, lse), (arrived[0] if arrived else None)
    assert carry is None
    if kind == 1:
        q = _heads(pr[:, :Q_W], N_KV_HEADS, GQA_GROUP)
        k = _heads(pr[:, Q_W:QK_W], N_KV_HEADS, 1)[:, 0]
        v = _heads(pr[:, QK_W:QK_W + KV_W], N_KV_HEADS, 1)[:, 0]
        o, lse = _attn(q, k, v, bq=min(FULL_BQ_FWD, S), tk=min(FULL_TK_FWD, S), tag=f"b{li}")
        return _unheads(o), (q, k, v, o, lse), None
    saved, outs, lses = [], [], []
    for g, (window, dil) in enumerate(C_GROUPS):
        q = _dilate(pr[:, g * GQA_GROUP * HEAD_DIM:(g + 1) * GQA_GROUP * HEAD_DIM], dil)
        k = _dilate(pr[:, Q_W + g * HEAD_DIM:Q_W + (g + 1) * HEAD_DIM], dil)[:, 0]
        v = _dilate(pr[:, QK_W + g * HEAD_DIM:QK_W + (g + 1) * HEAD_DIM], dil)[:, 0]
        o, lse = _attn(q, k, v, radius=window // (2 * dil), bq=min(256, S // dil), tag=f"c{li}g{g}")
        saved.append((q, k, v, o, lse))
        outs.append(_undilate(o))
        lses.append(_undilate(lse))
    o_all, lse_all = jnp.concatenate(outs, 0), jnp.concatenate(lses, 0)
    tok = _combine_fwd(o_all, lse_all, name=f"combine_fwd_{li}")
    return tok.transpose(1, 0, 2).reshape(S, Q_W), (saved, o_all, lse_all), None


def _mixer_bwd(kind, dtok, saved, sink, li):
    S = dtok.shape[0]
    if kind in (0, 1):
        q, k, v, o, lse = saved
        do = _heads(dtok, N_KV_HEADS, GQA_GROUP)
        if kind == 0:
            dq, dk, dv, dsink = _attn_bwd(q, k, v, o, lse, do, radius=A_RADIUS, sink=sink, bq=min(256, S), tag=f"a{li}")
        else:
            dq, dk, dv, dsink = _attn_bwd(q, k, v, o, lse, do, bq=min(FULL_BQ_BWD, S), tk=min(FULL_TK_BWD, S), tag=f"b{li}")
        return _unheads(dq), _unheads(dk[:, None]), _unheads(dv[:, None]), dsink
    per_group, o_all, lse_all = saved
    dt = dtok.reshape(S, N_TOK_HEADS, HEAD_DIM).transpose(1, 0, 2)
    do_all, dlse_all = _combine_bwd(dt, o_all, lse_all, name=f"combine_bwd_{li}")
    dqs, dks, dvs = [], [], []
    for g, (window, dil) in enumerate(C_GROUPS):
        q, k, v, o, lse = per_group[g]
        L = S // dil
        hs = slice(g * GQA_GROUP, (g + 1) * GQA_GROUP)
        do = do_all[hs].reshape(GQA_GROUP, L, dil, HEAD_DIM).transpose(2, 0, 1, 3)
        dlse = dlse_all[hs].reshape(GQA_GROUP, L, dil, 1).transpose(2, 0, 1, 3)
        dq, dk, dv, _ = _attn_bwd(q, k, v, o, lse, do, radius=window // (2 * dil), dlse=dlse, bq=min(256, L),
                                       tag=f"c{li}g{g}")
        dqs.append(dq.transpose(2, 0, 1, 3).reshape(S, GQA_GROUP * HEAD_DIM))
        dks.append(dk.transpose(1, 0, 2).reshape(S, HEAD_DIM))
        dvs.append(dv.transpose(1, 0, 2).reshape(S, HEAD_DIM))
    return jnp.concatenate(dqs, 1), jnp.concatenate(dks, 1), jnp.concatenate(dvs, 1), None


def kernel(x, mem, mem_norm_g, w_in, w_mem_kv, w_o, g_mix_pre, g_mix_post, attn_sink, qk_norm_g, w_gate_up, w_down, g_ffn_pre, g_ffn_post, loss_target, m_mem_norm_g, m_w_in, m_w_mem_kv, m_w_o, m_g_mix_pre, m_g_mix_post, m_attn_sink, m_qk_norm_g, m_w_gate_up, m_w_down, m_g_ffn_pre, m_g_ffn_post, v_mem_norm_g, v_w_in, v_w_mem_kv, v_w_o, v_g_mix_pre, v_g_mix_post, v_attn_sink, v_qk_norm_g, v_w_gate_up, v_w_down, v_g_ffn_pre, v_g_ffn_post):
    given = dict(locals())
    depth = w_in.shape[0]
    S, D = x.shape[1], x.shape[2]
    x0 = x[0]
    layer_shapes = [(1,) + tuple(given[n].shape[1:]) for n, _ in BIG]

    def layer_pack(pre, l, dtype):
        return _pack_local([given[pre + n][l:l + 1] for n, _ in BIG], dtype)

    def layer_weights(gathered):
        return [t[0] for t in _unpack_gathered(gathered, layer_shapes)]

    W = [None] * depth
    W[0] = layer_weights(_exchange(layer_pack("", 0, BF16), True, "gather_w0"))

    tabs = _rope_tables(S)
    mem_n = _rms_fwd(mem[0], mem_norm_g[None], BF16, name="rms_mem")

    saved = []
    xc = x0
    for i in range(depth):
        kind = i % N_MIXERS
        (tab, shift) = tabs[1] if kind == 1 else tabs[0]
        sink = attn_sink[i // N_MIXERS] if kind == 0 else None
        qk_gain = _qk_gain_row(qk_norm_g[i // N_MIXERS]) if kind == 1 else None
        W_in, W_mkv, W_o, W_gu, W_dn = W[i]
        h, proj = _mm(xc, W_in, F32, pre_g=g_mix_pre[i][None], name=f"mm_in_{i}")
        pr = _headprep_fwd(proj, tab, shift, qk_gain, name=f"headprep_fwd_{i}")
        carry = (layer_pack("", 1, BF16), True) if i == 0 and depth > 1 else None
        tok, mix_saved, arrived = _mixer_fwd(kind, pr, sink, i, carry)
        if carry is not None:
            W[1] = layer_weights(arrived)
        (mkv,) = _mm(mem_n, W_mkv, BF16, name=f"mm_mkv_{i}")
        qm = _heads(pr[:, QK_W + KV_W:], N_MEM_HEADS, 1)
        km = _heads(mkv[:, :QM_W], N_MEM_HEADS, 1)[:, 0]
        vm = _heads(mkv[:, QM_W:], N_MEM_HEADS, 1)[:, 0]
        mo, mlse = _attn(qm, km, vm, bq=min(1024, S), tk=km.shape[1], tag=f"m{i}")
        cat = jnp.concatenate([tok, _unheads(mo)], axis=1)
        o, x1 = _mm(cat, W_o, F32, post=(g_mix_post[i][None], xc), name=f"mm_o_{i}")
        carry = (layer_pack("", i + 2, BF16), True) if i + 2 < depth else None
        h2, gu, act, *arrived = _mm(x1, W_gu, BF16, pre_g=g_ffn_pre[i][None], swiglu=True, carry=carry, name=f"mm_gu_{i}")
        if carry is not None:
            W[i + 2] = layer_weights(arrived[0])
        f, x2 = _mm(act, W_dn, F32, post=(g_ffn_post[i][None], x1), name=f"mm_dn_{i}")
        saved.append(dict(x=xc, h=h, proj=proj, mix=mix_saved, qm=qm, km=km, vm=vm, mo=mo, mlse=mlse, cat=cat, o=o,
                          x1=x1, h2=h2, gu=gu, act=act, f=f))
        xc = x2

    dy, sq = _loss_head(xc, loss_target[0], name="loss_head")
    loss = lax.psum(sq[0, 0] * (0.5 / D), ("x", "y", "c"))

    zero_row = jnp.zeros((1, D), F32)
    grads = {n: [None] * depth for n in ("w_in", "w_mem_kv", "w_o", "w_gate_up", "w_down", "g_mix_pre", "g_mix_post",
                                         "g_ffn_pre", "g_ffn_post")}
    d_sink = [jnp.zeros((N_TOK_HEADS,), F32) for _ in range(attn_sink.shape[0])]
    d_qkg = [jnp.zeros((2, HEAD_DIM), F32) for _ in range(qk_norm_g.shape[0])]
    dmem_n = jnp.zeros((mem.shape[1], D), F32)
    recv = [None] * depth

    def scatter_pack(l):
        return _pack_for_scatter([grads[n][l][None] for n, _ in BIG], layer_shapes, BF16)

    dx = dy
    for i in reversed(range(depth)):
        kind = i % N_MIXERS
        sv = saved[i]
        (tab, shift) = tabs[1] if kind == 1 else tabs[0]
        sink = attn_sink[i // N_MIXERS] if kind == 0 else None
        W_in, W_mkv, W_o, W_gu, W_dn = W[i]
        df, dgu, dg = _mm(dx, W_dn, BF16, nt=True, gu=sv["gu"], pre_bwd=(sv["f"], g_ffn_post[i][None]), name=f"mmb_dn_{i}")
        grads["g_ffn_post"][i] = dg[0]
        grads["w_down"][i] = _mm_tn(sv["act"], df, name=f"mmw_dn_{i}")
        dx1, dg = _mm(dgu, W_gu, F32, nt=True, post_bwd=(sv["x1"], g_ffn_pre[i][None], dx), name=f"mmb_gu_{i}")
        grads["g_ffn_pre"][i] = dg[0]
        if i + 1 < depth:
            grads["w_gate_up"][i], recv[i + 1] = _mm_tn(sv["h2"], dgu, carry=(scatter_pack(i + 1), False), name=f"mmw_gu_{i}")
        else:
            grads["w_gate_up"][i] = _mm_tn(sv["h2"], dgu, name=f"mmw_gu_{i}")
        do, dcat, dg = _mm(dx1, W_o, BF16, nt=True, pre_bwd=(sv["o"], g_mix_post[i][None]), name=f"mmb_o_{i}")
        grads["g_mix_post"][i] = dg[0]
        grads["w_o"][i] = _mm_tn(sv["cat"], do, name=f"mmw_o_{i}")
        dmo = _heads(dcat[:, Q_W:], N_MEM_HEADS, 1)
        dqm, dkm, dvm, _ = _attn_bwd(sv["qm"], sv["km"], sv["vm"], sv["mo"], sv["mlse"], dmo, bq=min(1024, S),
                                     tk=sv["km"].shape[1], tag=f"m{i}")
        dmkv = jnp.concatenate([_unheads(dkm[:, None]), _unheads(dvm[:, None])], axis=1).astype(BF16)
        grads["w_mem_kv"][i] = _mm_tn(mem_n, dmkv, name=f"mmw_mkv_{i}")
        dmem_n = dmem_n + _mm(dmkv, W_mkv, F32, nt=True, name=f"mmb_mkv_{i}")[0]
        dq, dk, dv, dsink = _mixer_bwd(kind, dcat[:, :Q_W], sv["mix"], sink, i)
        if dsink is not None:
            d_sink[i // N_MIXERS] = dsink
        dpr = jnp.concatenate([dq, dk, dv, _unheads(dqm)], axis=1)
        if kind == 1:
            dproj, dgc = _headprep_bwd(dpr, tab, shift, sv["proj"], _qk_gain_row(qk_norm_g[i // N_MIXERS]),
                                       name=f"headprep_bwd_{i}")
            d_qkg[i // N_MIXERS] = jnp.stack([dgc[0, :Q_W].reshape(N_TOK_HEADS, HEAD_DIM).sum(0),
                                              dgc[0, Q_W:QK_W].reshape(N_KV_HEADS, HEAD_DIM).sum(0)])
        else:
            dproj = _headprep_bwd(dpr, tab, shift, name=f"headprep_bwd_{i}")
        dx, dg = _mm(dproj, W_in, F32, nt=True, post_bwd=(sv["x"], g_mix_pre[i][None], dx1), name=f"mmb_in_{i}")
        grads["g_mix_pre"][i] = dg[0]
        grads["w_in"][i] = _mm_tn(sv["h"], dproj, name=f"mmw_in_{i}")
    _, dg_mem = _rms_bwd(mem[0], mem_norm_g[None], dmem_n, BF16, name="rmsb_mem")

    recv[0] = _exchange(scatter_pack(0), False, "scatter_g0")
    per_layer = [_reduce_adamw(recv[l], layer_pack("", l, F32), layer_pack("m_", l, F32), layer_pack("v_", l, F32),
                               name=f"adamw_{l}") for l in range(depth)]
    big_out = [[jnp.concatenate(ts, axis=0) for ts in zip(*[_unpack_local(per_layer[l][j], layer_shapes) for l in range(depth)])]
               for j in range(4)]

    small_grads = dict(mem_norm_g=dg_mem[0], g_mix_pre=jnp.stack(grads["g_mix_pre"]), g_mix_post=jnp.stack(grads["g_mix_post"]),
                       attn_sink=jnp.stack(d_sink), qk_norm_g=jnp.stack(d_qkg), g_ffn_pre=jnp.stack(grads["g_ffn_pre"]),
                       g_ffn_post=jnp.stack(grads["g_ffn_post"]))
    sg = _pack_small([small_grads[n] for n in SMALL])
    srecv = _exchange(sg, True, "gather_small_grads")
    spacked = lambda pre: _pack_small([given[pre + n] for n in SMALL])
    gs, ds, ms, vs = _reduce_adamw(srecv, spacked(""), spacked("m_"), spacked("v_"), name="adamw_small")

    out = {}
    for pre, fb, fs in zip(("grad_", "delta_", "new_m_", "new_v_"), big_out, (gs, ds, ms, vs)):
        for (n, _), t in zip(BIG, fb):
            out[pre + n] = t
        for n, t in zip(SMALL, _unpack_small(fs, [given[n].shape for n in SMALL])):
            out[pre + n] = t
    order = ("mem_norm_g", "w_in", "w_mem_kv", "w_o", "g_mix_pre", "g_mix_post", "attn_sink", "qk_norm_g", "w_gate_up",
             "w_down", "g_ffn_pre", "g_ffn_post")
    return (loss, dx[None], *[out[p + n] for p in ("grad_", "delta_", "new_m_", "new_v_") for n in order])
```

```python
import functools

import jax
import jax.numpy as jnp
from jax import lax
from jax.experimental import pallas as pl
from jax.experimental.pallas import tpu as pltpu

F32 = jnp.float32
BF16 = jnp.bfloat16

HEAD_DIM = 64
N_TOK_HEADS = 12
N_KV_HEADS = 3
GQA_GROUP = 4
N_MEM_HEADS = 4
Q_W = N_TOK_HEADS * HEAD_DIM
KV_W = N_KV_HEADS * HEAD_DIM
QM_W = N_MEM_HEADS * HEAD_DIM
QK_W = Q_W + KV_W
IN_W = Q_W + 2 * KV_W + QM_W
N_MIXERS = 3
A_RADIUS = 128
C_GROUPS = ((128, 1), (512, 4), (2048, 16))
ROPE_THETA = 500000.0
ROPE_DIMS = HEAD_DIM // 4
AXIAL_THETA = 10000.0
GRID_W = 64
EPS = 1e-6
ATTN_SCALE = HEAD_DIM ** -0.5
NEG = -1e30

ADAM_LR = 0.001
ADAM_B1 = 0.9
ADAM_B2 = 0.999
ADAM_EPS = 1e-08
ADAM_WD = 0.01
ADAM_STEP = 10

N_DEV = 8
LANES = 128
VMEM_LIMIT = 56 * 1024 * 1024
MESH = pl.DeviceIdType.MESH
NT_DIMS = (((1,), (1,)), ((), ()))
TN_DIMS = (((0,), (0,)), ((), ()))


def _pc(body, **kw):
    return pl.pallas_call(body, **kw)


def _cp(*sem):
    return pltpu.CompilerParams(dimension_semantics=sem, vmem_limit_bytes=VMEM_LIMIT)


def _row_tile(m, cap=512):
    t = cap
    while m % t:
        t //= 2
    return t


def _rms_fwd(x, g, out_dtype, name="rms_fwd"):
    M, D = x.shape
    tm = _row_tile(M)

    def body(x_ref, g_ref, o_ref):
        xv = x_ref[...]
        y = xv * lax.rsqrt(jnp.mean(xv * xv, axis=-1, keepdims=True) + EPS) * g_ref[...]
        o_ref[...] = y.astype(o_ref.dtype)

    row = pl.BlockSpec((tm, D), lambda i: (i, 0))
    vec = pl.BlockSpec((1, D), lambda i: (0, 0))
    return _pc(body, name=name, out_shape=jax.ShapeDtypeStruct((M, D), out_dtype), grid=(M // tm,),
               in_specs=[row, vec], out_specs=row, compiler_params=_cp("parallel"))(x, g)


def _rms_bwd_tile(xv, g, d):
    r = lax.rsqrt(jnp.mean(xv * xv, axis=-1, keepdims=True) + EPS)
    xh = xv * r
    dxh = d * g
    return r * (dxh - xh * jnp.mean(dxh * xh, axis=-1, keepdims=True)), jnp.sum(d * xh, axis=0, keepdims=True)


def _rms_bwd(x, g, dy, out_dtype, name="rms_bwd"):
    M, D = x.shape
    tm = _row_tile(M)

    def body(x_ref, g_ref, dy_ref, dx_ref, dg_ref):
        dx, dg = _rms_bwd_tile(x_ref[...], g_ref[...], dy_ref[...].astype(F32))
        dx_ref[...] = dx.astype(dx_ref.dtype)

        @pl.when(pl.program_id(0) == 0)
        def _():
            dg_ref[...] = jnp.zeros_like(dg_ref)

        dg_ref[...] += dg

    row = pl.BlockSpec((tm, D), lambda i: (i, 0))
    vec = pl.BlockSpec((1, D), lambda i: (0, 0))
    return _pc(body, name=name,
               out_shape=(jax.ShapeDtypeStruct((M, D), out_dtype), jax.ShapeDtypeStruct((1, D), F32)),
               grid=(M // tm,), in_specs=[row, vec, row], out_specs=(row, vec), compiler_params=_cp("arbitrary"))(x, g, dy)


def _loss_head(y, t, name="loss_head"):
    M, D = y.shape
    tm = _row_tile(M)

    def body(y_ref, t_ref, dy_ref, acc_ref):
        e = y_ref[...] - t_ref[...]
        dy_ref[...] = e * (1.0 / D)

        @pl.when(pl.program_id(0) == 0)
        def _():
            acc_ref[...] = jnp.zeros_like(acc_ref)

        acc_ref[...] += jnp.sum(e * e)

    row = pl.BlockSpec((tm, D), lambda i: (i, 0))
    return _pc(body, name=name,
               out_shape=(jax.ShapeDtypeStruct((M, D), F32), jax.ShapeDtypeStruct((8, LANES), F32)),
               grid=(M // tm,), in_specs=[row, row],
               out_specs=(row, pl.BlockSpec((8, LANES), lambda i: (0, 0))), compiler_params=_cp("arbitrary"))(y, t)


def _mm(a, w, out_dtype, nt=False, pre_g=None, swiglu=False, post=None, gu=None, pre_bwd=None, post_bwd=None, carry=None,
        tm=256, name="mm"):
    M, K = a.shape
    N = w.shape[0] if nt else w.shape[1]
    tm = _row_tile(M, tm)
    gain_grad = pre_bwd is not None or post_bwd is not None

    def body(*refs):
        refs = list(refs)
        a_ref, w_ref = refs.pop(0), refs.pop(0)
        pg_ref = refs.pop(0) if pre_g is not None else None
        g_ref, r_ref = (refs.pop(0), refs.pop(0)) if post is not None else (None, None)
        gu_ref = refs.pop(0) if gu is not None else None
        bwd_refs = [refs.pop(0) for _ in (pre_bwd or post_bwd or ())]
        lhs = a_ref[...]
        if pre_g is not None:
            lhs = (lhs * lax.rsqrt(jnp.mean(lhs * lhs, axis=-1, keepdims=True) + EPS) * pg_ref[...]).astype(BF16)
            refs.pop(0)[...] = lhs
        if pre_bwd is not None:
            lhs, dg = _rms_bwd_tile(bwd_refs[0][...], bwd_refs[1][...], lhs.astype(F32))
            lhs = lhs.astype(BF16)
            refs.pop(0)[...] = lhs
        if nt:
            acc = lax.dot_general(lhs, w_ref[...], NT_DIMS, preferred_element_type=F32)
        else:
            acc = jnp.dot(lhs, w_ref[...], preferred_element_type=F32)
        o_ref = refs.pop(0)
        if post_bwd is not None:
            dx, dg = _rms_bwd_tile(bwd_refs[0][...], bwd_refs[1][...], acc)
            o_ref[...] = bwd_refs[2][...] + dx
        elif gu is None:
            o_ref[...] = acc.astype(o_ref.dtype)
        else:
            gate = gu_ref[:, :N].astype(F32)
            sig = 1.0 / (1.0 + jnp.exp(-gate))
            o_ref[:, :N] = (acc * gu_ref[:, N:].astype(F32) * (sig * (1.0 + gate * (1.0 - sig)))).astype(o_ref.dtype)
            o_ref[:, N:] = (acc * (gate * sig)).astype(o_ref.dtype)
        if swiglu:
            gate = acc[:, : N // 2]
            refs.pop(0)[...] = (gate * (1.0 / (1.0 + jnp.exp(-gate))) * acc[:, N // 2:]).astype(BF16)
        if post is not None:
            y = acc * lax.rsqrt(jnp.mean(acc * acc, axis=-1, keepdims=True) + EPS) * g_ref[...]
            refs.pop(0)[...] = r_ref[...] + y
        if gain_grad:
            dg_ref = refs.pop(0)

            @pl.when(pl.program_id(0) == 0)
            def _():
                dg_ref[...] = jnp.zeros_like(dg_ref)

            dg_ref[...] += dg

    row = lambda n: pl.BlockSpec((tm, n), lambda i: (i, 0))
    vec = lambda n: pl.BlockSpec((1, n), lambda i: (0, 0))
    ins = [a, w]
    specs = [row(K), pl.BlockSpec(w.shape, lambda i: (0, 0), pipeline_mode=pl.Buffered(1))]
    outs, ospecs = [], []
    if pre_g is not None:
        ins, specs = ins + [pre_g], specs + [vec(K)]
    if pre_g is not None or pre_bwd is not None:
        outs, ospecs = outs + [jax.ShapeDtypeStruct((M, K), BF16)], ospecs + [row(K)]
    if post is not None:
        ins, specs = ins + list(post), specs + [vec(N), row(N)]
    if gu is not None:
        ins, specs = ins + [gu], specs + [row(2 * N)]
        outs, ospecs = outs + [jax.ShapeDtypeStruct((M, 2 * N), BF16)], ospecs + [row(2 * N)]
    else:
        outs, ospecs = outs + [jax.ShapeDtypeStruct((M, N), out_dtype)], ospecs + [row(N)]
    if pre_bwd is not None:
        ins, specs = ins + list(pre_bwd), specs + [row(K), vec(K)]
    if post_bwd is not None:
        ins, specs = ins + list(post_bwd), specs + [row(N), vec(N), row(N)]
    if swiglu:
        outs, ospecs = outs + [jax.ShapeDtypeStruct((M, N // 2), BF16)], ospecs + [row(N // 2)]
    if post is not None:
        outs, ospecs = outs + [jax.ShapeDtypeStruct((M, N), F32)], ospecs + [row(N)]
    if gain_grad:
        D = K if pre_bwd is not None else N
        outs, ospecs = outs + [jax.ShapeDtypeStruct((1, D), F32)], ospecs + [vec(D)]
    return _call(body, carry, ins, name=name, out_shape=outs, grid=(M // tm,), in_specs=specs, out_specs=ospecs,
                 sem=("arbitrary" if gain_grad else "parallel",))


def _mm_tn(a, b, carry=None, name="mm_tn"):
    S, M = a.shape
    N = b.shape[1]
    tm = M if M <= 1408 else M // 2
    tn = N if N <= 1408 else N // 4
    ts = _row_tile(S, 1024)

    def body(a_ref, b_ref, o_ref):
        @pl.when(pl.program_id(2) == 0)
        def _():
            o_ref[...] = jnp.zeros_like(o_ref)

        o_ref[...] += lax.dot_general(a_ref[...], b_ref[...], TN_DIMS, preferred_element_type=F32)

    res = _call(body, carry, [a, b], name=name, out_shape=[jax.ShapeDtypeStruct((M, N), F32)],
                grid=(M // tm, N // tn, S // ts),
                in_specs=[pl.BlockSpec((ts, tm), lambda i, j, s: (s, i)), pl.BlockSpec((ts, tn), lambda i, j, s: (s, j))],
                out_specs=[pl.BlockSpec((tm, tn), lambda i, j, s: (i, j))], sem=("parallel", "parallel", "arbitrary"))
    return res[0] if carry is None else res


def _rope_tables(S):
    pos = jnp.arange(S, dtype=jnp.int32)

    def table(p, n_dims, theta):
        inv = theta ** (-(jnp.arange(0, n_dims, 2, dtype=F32) / n_dims))
        ang = p.astype(F32)[:, None] * inv[None, :]
        return jnp.cos(ang), jnp.sin(ang)

    one = lambda n: jnp.ones((S, n), F32)
    zero = lambda n: jnp.zeros((S, n), F32)
    cp, sp = table(pos, ROPE_DIMS, ROPE_THETA)
    rest = HEAD_DIM - ROPE_DIMS
    part = (jnp.concatenate([cp, cp, one(rest)], 1), jnp.concatenate([zero(8), sp, zero(rest)], 1),
            jnp.concatenate([-sp, zero(8), zero(rest)], 1))
    cr, sr = table(pos // GRID_W, HEAD_DIM // 2, AXIAL_THETA)
    cc, sc = table(pos % GRID_W, HEAD_DIM // 2, AXIAL_THETA)
    axial = (jnp.concatenate([cr, cr, cc, cc], 1), jnp.concatenate([zero(16), sr, zero(16), sc], 1),
             jnp.concatenate([-sr, zero(16), -sc, zero(16)], 1))
    rep = LANES // HEAD_DIM
    return (tuple(jnp.tile(t, (1, rep)) for t in part), ROPE_DIMS // 2), (tuple(jnp.tile(t, (1, rep)) for t in axial), HEAD_DIM // 4)


def _seg_mats():
    col = jnp.arange(IN_W)[:, None] // HEAD_DIM
    e = (col == jnp.arange(LANES)[None, :]).astype(BF16)
    return e, e.T


def _qk_gain_row(qk_g):
    return jnp.concatenate([jnp.tile(qk_g[0], N_TOK_HEADS), jnp.tile(qk_g[1], N_KV_HEADS),
                            jnp.ones((IN_W - QK_W,), F32)])[None, :]


def _rope_cols(tabs, tm):
    col = lax.broadcasted_iota(jnp.int32, (tm, IN_W), 1)
    qk = col < QK_W
    c, s_lo, s_hi = (jnp.tile(t[...], (1, IN_W // LANES)) for t in tabs)
    return jnp.where(qk, c, 1.0), jnp.where(qk, s_lo, 0.0), jnp.where(qk, s_hi, 0.0), qk


def _seg_mean(v, e_ref, et_ref):
    def split_dot(t, m_ref):
        hi = t.astype(BF16)
        lo = (t - hi.astype(F32)).astype(BF16)
        return jnp.dot(hi, m_ref[...], preferred_element_type=F32) + jnp.dot(lo, m_ref[...], preferred_element_type=F32)

    return split_dot(split_dot(v, e_ref) * (1.0 / HEAD_DIM), et_ref)


def _headprep_fwd(proj, tabs, shift, qk_gain=None, kv_heads=False, name="headprep_fwd"):
    S = proj.shape[0]
    tm = _row_tile(S, 256)
    norm = qk_gain is not None

    def body(*refs):
        refs = list(refs)
        p_ref, c_ref, lo_ref, hi_ref = (refs.pop(0) for _ in range(4))
        g_ref, e_ref, et_ref = (refs.pop(0) for _ in range(3)) if norm else (None, None, None)
        o_ref = refs.pop(0)
        x = p_ref[...]
        c, s_lo, s_hi, qk = _rope_cols((c_ref, lo_ref, hi_ref), tm)
        if norm:
            r = lax.rsqrt(_seg_mean(x * x, e_ref, et_ref) + EPS)
            x = x * jnp.where(qk, r, 1.0) * g_ref[...]
        y = (x * c + pltpu.roll(x, shift, 1) * s_lo + pltpu.roll(x, IN_W - shift, 1) * s_hi).astype(o_ref.dtype)
        o_ref[...] = y
        if kv_heads:
            k_ref, v_ref = refs
            for h in range(N_KV_HEADS):
                k_ref[h] = y[:, Q_W + h * HEAD_DIM:Q_W + (h + 1) * HEAD_DIM]
                v_ref[h] = y[:, QK_W + h * HEAD_DIM:QK_W + (h + 1) * HEAD_DIM]

    row = pl.BlockSpec((tm, IN_W), lambda i: (i, 0))
    tab = pl.BlockSpec((tm, LANES), lambda i: (i, 0))
    ins = [proj, *tabs]
    specs = [row, tab, tab, tab]
    if norm:
        e, et = _seg_mats()
        ins += [qk_gain, e, et]
        specs += [pl.BlockSpec((1, IN_W), lambda i: (0, 0)), pl.BlockSpec((IN_W, LANES), lambda i: (0, 0)),
                  pl.BlockSpec((LANES, IN_W), lambda i: (0, 0))]
    out_shape, out_specs = [jax.ShapeDtypeStruct((S, IN_W), BF16)], [row]
    if kv_heads:
        out_shape += [jax.ShapeDtypeStruct((N_KV_HEADS, S, HEAD_DIM), BF16)] * 2
        out_specs += [pl.BlockSpec((N_KV_HEADS, tm, HEAD_DIM), lambda i: (0, i, 0))] * 2
    res = _pc(body, name=name, out_shape=tuple(out_shape), grid=(S // tm,),
              in_specs=specs, out_specs=tuple(out_specs), compiler_params=_cp("parallel"))(*ins)
    return res if kv_heads else res[0]


def _headprep_bwd(dpr, tabs, shift, proj=None, qk_gain=None, name="headprep_bwd"):
    parts = isinstance(dpr, (tuple, list))
    S = dpr[0].shape[0] if parts else dpr.shape[0]
    tm = _row_tile(S, 256)
    norm = qk_gain is not None

    def body(*refs):
        refs = list(refs)
        d_refs = [refs.pop(0) for _ in range(4 if parts else 1)]
        c_ref, lo_ref, hi_ref = (refs.pop(0) for _ in range(3))
        if norm:
            p_ref, g_ref, e_ref, et_ref, o_ref, dg_ref = refs
        else:
            (o_ref,) = refs
        if parts:
            dq_ref, dk_ref, dv_ref, dqm_ref = d_refs
            d = jnp.concatenate([dq_ref[...]] + [dk_ref[h] for h in range(N_KV_HEADS)]
                                + [dv_ref[h] for h in range(N_KV_HEADS)] + [dqm_ref[...]], axis=1).astype(F32)
        else:
            d = d_refs[0][...].astype(F32)
        c, s_lo, s_hi, qk = _rope_cols((c_ref, lo_ref, hi_ref), tm)
        dx = d * c + pltpu.roll(d * s_lo, IN_W - shift, 1) + pltpu.roll(d * s_hi, shift, 1)
        if norm:
            x = p_ref[...]
            r = lax.rsqrt(_seg_mean(x * x, e_ref, et_ref) + EPS)
            xh = x * r

            @pl.when(pl.program_id(0) == 0)
            def _():
                dg_ref[...] = jnp.zeros_like(dg_ref)

            dg_ref[...] += jnp.sum(jnp.where(qk, dx * xh, 0.0), axis=0, keepdims=True)
            dxh = dx * g_ref[...]
            dn = r * (dxh - xh * _seg_mean(dxh * xh, e_ref, et_ref))
            dx = jnp.where(qk, dn, dx)
        o_ref[...] = dx.astype(o_ref.dtype)

    row = pl.BlockSpec((tm, IN_W), lambda i: (i, 0))
    tab = pl.BlockSpec((tm, LANES), lambda i: (i, 0))
    vec = pl.BlockSpec((1, IN_W), lambda i: (0, 0))
    if parts:
        heads = pl.BlockSpec((N_KV_HEADS, tm, HEAD_DIM), lambda i: (0, i, 0))
        ins = [*dpr, *tabs]
        specs = [pl.BlockSpec((tm, Q_W), lambda i: (i, 0)), heads, heads, pl.BlockSpec((tm, QM_W), lambda i: (i, 0)), tab, tab, tab]
    else:
        ins = [dpr, *tabs]
        specs = [row, tab, tab, tab]
    out_shape = jax.ShapeDtypeStruct((S, IN_W), BF16)
    out_specs = row
    if norm:
        e, et = _seg_mats()
        ins += [proj, qk_gain, e, et]
        specs += [row, vec, pl.BlockSpec((IN_W, LANES), lambda i: (0, 0)), pl.BlockSpec((LANES, IN_W), lambda i: (0, 0))]
        out_shape = (out_shape, jax.ShapeDtypeStruct((1, IN_W), F32))
        out_specs = (row, vec)
    return _pc(body, name=name, out_shape=out_shape, grid=(S // tm,), in_specs=specs, out_specs=out_specs,
               compiler_params=_cp("arbitrary" if norm else "parallel"))(*ins)


CHAIN_ROWS = 128


def _skewed(n, stages):
    for t in range(n + len(stages) - 1):
        for s, stage in enumerate(stages):
            if 0 <= t - s < n:
                stage(t - s)


def _chain_slices(G, bq):
    cr = min(CHAIN_ROWS, bq)
    per = bq // cr
    return [(c // per, slice((c % per) * cr, (c % per + 1) * cr), slice(c * cr, (c + 1) * cr)) for c in range(G * per)]


def _v_ones(v):
    return jnp.concatenate([v, jnp.ones(v.shape, v.dtype)], axis=1)


def _q_dims(q, k, heads):
    if heads is None:
        return q.shape
    return k.shape[0], heads, q.shape[0], k.shape[2]


def _q_shape(heads, NB, G, L, HD):
    return (NB, G, L, HD) if heads is None else (L, NB * G * HD)


def _q_spec(heads, G, rows, HD, index):
    if heads is None:
        return pl.BlockSpec((1, G, rows, HD), lambda *ids: (index(*ids)[0], 0, index(*ids)[1], 0))
    return pl.BlockSpec((rows, G * HD), lambda *ids: index(*ids)[::-1])


def _q_at(heads, g, hr, HD):
    return (0, g, hr, slice(None)) if heads is None else (hr, slice(g * HD, (g + 1) * HD))


def _window(L, blk, radius):
    if radius is None:
        return L, None
    W = min(L, blk + 2 * radius)
    assert blk % radius == 0 and (L - W) % radius == 0
    return W, lambda n: radius * jnp.clip(n * (blk // radius) - 1, 0, (L - W) // radius)


def _win_specs(G, W, HD, start, with_g):
    E = pl.Element
    st = (lambda n: 0) if start is None else start
    if with_g:
        return pl.BlockSpec((E(1), E(G), E(W), E(HD)), lambda b, n: (b, 0, st(n), 0))
    return pl.BlockSpec((E(1), E(W), E(HD)), lambda b, n: (b, st(n), 0))


def _attn_delta(do, o, *, dlse=None, lse=None, sink=None, heads=None, name="attn_delta"):
    HD = HEAD_DIM
    (NB, G), L = (heads, do.shape[0]) if heads is not None else (do.shape[:2], do.shape[2])
    bl = _row_tile(L, 1024)

    def body(*refs):
        refs = list(refs)
        sink_ref = refs.pop(0) if sink is not None else None
        do_ref, o_ref = refs.pop(0), refs.pop(0)
        dlse_ref = refs.pop(0) if dlse is not None else None
        lse_ref = refs.pop(0) if sink is not None else None
        delta_ref = refs.pop(0)
        b = pl.program_id(0)
        if heads is None:
            delta = jnp.sum(do_ref[0].astype(F32) * o_ref[0].astype(F32), axis=-1, keepdims=True)
        else:
            prod = do_ref[...].astype(F32) * o_ref[...].astype(F32)
            delta = jnp.concatenate([jnp.sum(prod[:, g * HD:(g + 1) * HD], axis=-1, keepdims=True)[None] for g in range(G)])
        if dlse is not None:
            delta = delta - dlse_ref[0]
        delta_ref[0] = delta
        if sink is not None:
            ds_ref = refs.pop(0)

            @pl.when(pl.program_id(1) == 0)
            def _():
                ds_ref[...] = jnp.zeros_like(ds_ref)

            for g in range(G):
                ps = jnp.exp(sink_ref[b * G + g] - lse_ref[0, g]) * delta[g]
                ds_ref[0, g] -= jnp.sum(ps)

    blk = _q_spec(None if heads is None else G, G, bl, HD, lambda b, n: (b, n))
    col = pl.BlockSpec((1, G, bl, 1), lambda b, n: (b, 0, n, 0))
    ins, specs = [do, o], [blk, blk]
    if dlse is not None:
        ins, specs = ins + [dlse], specs + [col]
    out_shape = jax.ShapeDtypeStruct((NB, G, L, 1), F32)
    out_specs = col
    if sink is not None:
        ins, specs = [sink] + ins + [lse], [pl.BlockSpec(memory_space=pltpu.SMEM)] + specs + [col]
        out_shape = (out_shape, jax.ShapeDtypeStruct((NB, G, 1, LANES), F32))
        out_specs = (col, pl.BlockSpec((1, G, 1, LANES), lambda b, n: (b, 0, 0, 0)))
    return _pc(body, name=name, out_shape=out_shape, grid=(NB, L // bl), in_specs=specs, out_specs=out_specs,
               compiler_params=_cp("parallel", "arbitrary"))(*ins)


def _attn_fwd_full(q, k, v, *, bq, tk, heads=None, name="attn_fwd_full"):
    NB, G, L, HD = _q_dims(q, k, heads)
    Lk = k.shape[1]
    nq, nk = L // bq, Lk // tk
    rows = G * bq
    chains = _chain_slices(G, bq)

    def body(q_ref, k_ref, v_ref, o_ref, lse_ref, m_sc, acc_sc, q_sc):
        j = pl.program_id(2)

        @pl.when(j == 0)
        def _():
            m_sc[...] = jnp.full_like(m_sc, NEG)
            acc_sc[...] = jnp.zeros_like(acc_sc)
            for g, hr, sl in chains:
                q_sc[sl] = q_ref[_q_at(heads, g, hr, HD)] * ATTN_SCALE

        kk = k_ref[0]
        vv = _v_ones(v_ref[0])
        st = [dict() for _ in chains]

        def scores(c):
            st[c]["s"] = lax.dot_general(q_sc[chains[c][2]], kk, NT_DIMS, preferred_element_type=F32)

        def softmax(c):
            sl = chains[c][2]
            m_prev = m_sc[sl]
            m_new = jnp.maximum(m_prev, jnp.max(st[c]["s"], axis=1, keepdims=True))
            st[c]["p"] = jnp.exp(st[c].pop("s") - m_new).astype(BF16)
            st[c]["alpha"] = jnp.exp(m_prev - m_new)
            m_sc[sl] = m_new

        def values(c):
            sl = chains[c][2]
            acc_sc[sl] = st[c].pop("alpha") * acc_sc[sl] + jnp.dot(st[c].pop("p"), vv, preferred_element_type=F32)

        _skewed(len(chains), (scores, softmax, values))

        @pl.when(j == nk - 1)
        def _():
            for g, hr, sl in chains:
                acc = acc_sc[sl]
                l = acc[:, HD:HD + 1]
                o_ref[_q_at(heads, g, hr, HD)] = (acc[:, :HD] / l).astype(o_ref.dtype)
                lse_ref[0, g, hr, :] = m_sc[sl] + jnp.log(l)

    qspec = _q_spec(heads, G, bq, HD, lambda b, n, j: (b, n))
    kspec = pl.BlockSpec((1, tk, HD), lambda b, n, j: (b, j, 0))
    return _pc(body, name=name,
               out_shape=(jax.ShapeDtypeStruct(_q_shape(heads, NB, G, L, HD), BF16), jax.ShapeDtypeStruct((NB, G, L, 1), F32)),
               grid=(NB, nq, nk), in_specs=[qspec, kspec, kspec],
               out_specs=(qspec, pl.BlockSpec((1, G, bq, 1), lambda b, n, j: (b, 0, n, 0))),
               scratch_shapes=[pltpu.VMEM((rows, 1), F32), pltpu.VMEM((rows, 2 * HD), F32), pltpu.VMEM((rows, HD), BF16)],
               compiler_params=_cp("parallel", "parallel", "arbitrary"))(q, k, v)


def _attn_bwd_full(q, k, v, do, lse, o, *, bq, tk, heads=None, name="attn_bwd_full"):
    NB, G, L, HD = _q_dims(q, k, heads)
    Lk = k.shape[1]
    nq, nk = L // bq, Lk // tk
    chains = _chain_slices(G, bq)

    def body(q_ref, k_ref, v_ref, do_ref, lse_ref, o_ref, dqp_ref, dk_ref, dv_ref, dk_sc, dv_sc):
        n = pl.program_id(2)

        @pl.when(n == 0)
        def _():
            dk_sc[...] = jnp.zeros_like(dk_sc)
            dv_sc[...] = jnp.zeros_like(dv_sc)

        kk, vv = k_ref[0], v_ref[0]
        st = [dict() for _ in chains]

        def scores(c):
            g, hr, _ = chains[c]
            at = _q_at(heads, g, hr, HD)
            st[c]["q"] = q_ref[at] * ATTN_SCALE
            st[c]["do"] = do_ref[at]
            st[c]["s"] = lax.dot_general(st[c]["q"], kk, NT_DIMS, preferred_element_type=F32)
            st[c]["dp"] = lax.dot_general(st[c]["do"], vv, NT_DIMS, preferred_element_type=F32)
            st[c]["delta"] = jnp.sum(st[c]["do"].astype(F32) * o_ref[at].astype(F32), axis=-1, keepdims=True)

        def softmax(c):
            g, hr, _ = chains[c]
            p = jnp.exp(st[c].pop("s") - lse_ref[0, g, hr, :])
            st[c]["ds"] = (p * (st[c].pop("dp") - st[c].pop("delta"))).astype(BF16)
            st[c]["p"] = p.astype(BF16)

        def grads(c):
            g, hr, _ = chains[c]
            ds = st[c].pop("ds")
            dv_sc[...] += lax.dot_general(st[c].pop("p"), st[c].pop("do"), TN_DIMS, preferred_element_type=F32)
            dk_sc[...] += lax.dot_general(ds, st[c].pop("q"), TN_DIMS, preferred_element_type=F32)
            dqp_ref[(0,) + _q_at(heads, g, hr, HD)] = jnp.dot(ds, kk, preferred_element_type=F32) * ATTN_SCALE

        _skewed(len(chains), (scores, softmax, grads))

        @pl.when(n == nq - 1)
        def _():
            dk_ref[0] = dk_sc[...].astype(dk_ref.dtype)
            dv_ref[0] = dv_sc[...].astype(dv_ref.dtype)

    qspec = _q_spec(heads, G, bq, HD, lambda b, m, n: (b, n))
    cspec = pl.BlockSpec((1, G, bq, 1), lambda b, m, n: (b, 0, n, 0))
    kspec = pl.BlockSpec((1, tk, HD), lambda b, m, n: (b, m, 0))
    kv_shape = jax.ShapeDtypeStruct((NB, Lk, HD), BF16)
    if heads is None:
        pspec = pl.BlockSpec((1, 1, G, bq, HD), lambda b, m, n: (m, b, 0, n, 0))
    else:
        pspec = pl.BlockSpec((1, bq, G * HD), lambda b, m, n: (m, n, b))
    dqp, dk, dv = _pc(body, name=name,
                      out_shape=(jax.ShapeDtypeStruct((nk,) + _q_shape(heads, NB, G, L, HD), F32), kv_shape, kv_shape),
                      grid=(NB, nk, nq), in_specs=[qspec, kspec, kspec, qspec, cspec, qspec],
                      out_specs=(pspec, kspec, kspec),
                      scratch_shapes=[pltpu.VMEM((tk, HD), F32), pltpu.VMEM((tk, HD), F32)],
                      compiler_params=_cp("parallel", "parallel", "arbitrary"))(q, k, v, do, lse, o)
    if nk == 1:
        return dqp[0].astype(BF16), dk, dv
    bl = _row_tile(L, 512)

    def sum_body(p_ref, o_ref):
        acc = p_ref[0]
        for j in range(1, nk):
            acc = acc + p_ref[j]
        o_ref[...] = acc.astype(o_ref.dtype)

    if heads is None:
        pspec = pl.BlockSpec((nk, 1, G, bl, HD), lambda b, n: (0, b, 0, n, 0))
    else:
        pspec = pl.BlockSpec((nk, bl, G * HD), lambda b, n: (0, n, b))
    dq = _pc(sum_body, name=name + "_sum", out_shape=jax.ShapeDtypeStruct(_q_shape(heads, NB, G, L, HD), BF16),
             grid=(NB, L // bl), in_specs=[pspec], out_specs=_q_spec(heads, G, bl, HD, lambda b, n: (b, n)),
             compiler_params=_cp("parallel", "parallel"))(dqp)
    return dq, dk, dv


def _attn_fwd_win(q, k, v, *, radius, sink=None, bq, heads=None, carry=None, name="attn_fwd_win"):
    NB, G, L, HD = _q_dims(q, k, heads)
    W, start = _window(k.shape[1], bq, radius)
    chains = _chain_slices(G, bq)

    def body(*refs):
        if sink is not None:
            sink_ref, *refs = refs
        q_ref, k_ref, v_ref, o_ref, lse_ref = refs
        b, n = pl.program_id(0), pl.program_id(1)
        kk = k_ref[0]
        vv = _v_ones(v_ref[0])
        st = [dict() for _ in chains]

        def scores(c):
            g, hr, _ = chains[c]
            s = lax.dot_general(q_ref[_q_at(heads, g, hr, HD)] * ATTN_SCALE, kk, NT_DIMS, preferred_element_type=F32)
            if radius is not None:
                qpos = n * bq + hr.start + lax.broadcasted_iota(jnp.int32, (hr.stop - hr.start, 1), 0)
                kpos = start(n) + lax.broadcasted_iota(jnp.int32, (1, W), 1)
                s = jnp.where(jnp.abs(qpos - kpos) <= radius, s, NEG)
            st[c]["s"] = s

        def softmax(c):
            g = chains[c][0]
            m = jnp.max(st[c]["s"], axis=1, keepdims=True)
            if sink is not None:
                m = jnp.maximum(m, sink_ref[b * G + g])
            st[c]["p"] = jnp.exp(st[c].pop("s") - m).astype(BF16)
            st[c]["m"] = m

        def values(c):
            g, hr, _ = chains[c]
            acc = jnp.dot(st[c].pop("p"), vv, preferred_element_type=F32)
            m = st[c].pop("m")
            l = acc[:, HD:HD + 1]
            if sink is not None:
                l = l + jnp.exp(sink_ref[b * G + g] - m)
            o_ref[_q_at(heads, g, hr, HD)] = (acc[:, :HD] / l).astype(o_ref.dtype)
            lse_ref[0, g, hr, :] = m + jnp.log(l)

        _skewed(len(chains), (scores, softmax, values))

    qspec = _q_spec(heads, G, bq, HD, lambda b, n: (b, n))
    kspec = _win_specs(G, W, HD, start, False)
    ins, specs = [q, k, v], [qspec, kspec, kspec]
    if sink is not None:
        ins, specs = [sink] + ins, [pl.BlockSpec(memory_space=pltpu.SMEM)] + specs
    return _call(body, carry, ins, name=name,
                 out_shape=(jax.ShapeDtypeStruct(_q_shape(heads, NB, G, L, HD), BF16), jax.ShapeDtypeStruct((NB, G, L, 1), F32)),
                 grid=(NB, L // bq), in_specs=specs,
                 out_specs=(qspec, pl.BlockSpec((1, G, bq, 1), lambda b, n: (b, 0, n, 0))), sem=("parallel", "parallel"))


def _attn_dq_win(q, k, v, do, lse, delta, *, radius, bq, heads=None, name="attn_dq_win"):
    NB, G, L, HD = _q_dims(q, k, heads)
    W, start = _window(L, bq, radius)
    chains = _chain_slices(G, bq)

    def body(q_ref, k_ref, v_ref, do_ref, lse_ref, dl_ref, dq_ref):
        n = pl.program_id(1)
        kk, vv = k_ref[0], v_ref[0]
        kpos = start(n) + lax.broadcasted_iota(jnp.int32, (1, W), 1)
        st = [dict() for _ in chains]

        def scores(c):
            g, hr, _ = chains[c]
            at = _q_at(heads, g, hr, HD)
            st[c]["s"] = lax.dot_general(q_ref[at] * ATTN_SCALE, kk, NT_DIMS, preferred_element_type=F32)
            st[c]["dp"] = lax.dot_general(do_ref[at], vv, NT_DIMS, preferred_element_type=F32)

        def softmax(c):
            g, hr, _ = chains[c]
            qpos = n * bq + hr.start + lax.broadcasted_iota(jnp.int32, (hr.stop - hr.start, 1), 0)
            p = jnp.where(jnp.abs(qpos - kpos) <= radius, jnp.exp(st[c].pop("s") - lse_ref[0, g, hr, :]), 0.0)
            st[c]["ds"] = (p * (st[c].pop("dp") - dl_ref[0, g, hr, :])).astype(BF16)

        def grads(c):
            g, hr, _ = chains[c]
            dq = jnp.dot(st[c].pop("ds"), kk, preferred_element_type=F32) * ATTN_SCALE
            dq_ref[_q_at(heads, g, hr, HD)] = dq.astype(dq_ref.dtype)

        _skewed(len(chains), (scores, softmax, grads))

    qspec = _q_spec(heads, G, bq, HD, lambda b, n: (b, n))
    cspec = pl.BlockSpec((1, G, bq, 1), lambda b, n: (b, 0, n, 0))
    kspec = _win_specs(G, W, HD, start, False)
    return _pc(body, name=name, out_shape=jax.ShapeDtypeStruct(_q_shape(heads, NB, G, L, HD), BF16), grid=(NB, L // bq),
               in_specs=[qspec, kspec, kspec, qspec, cspec, cspec], out_specs=qspec,
               compiler_params=_cp("parallel", "parallel"))(q, k, v, do, lse, delta)


def _attn_dkv_win(q, k, v, do, lse, delta, *, radius, bk, heads=None, name="attn_dkv_win"):
    NB, G, L, HD = _q_dims(q, k, heads)
    W, start = _window(L, bk, radius)
    chains = _chain_slices(G, W)

    def body(q_ref, k_ref, v_ref, do_ref, lse_ref, dl_ref, dk_ref, dv_ref):
        m = pl.program_id(1)
        kk, vv = k_ref[0], v_ref[0]
        kpos = m * bk + lax.broadcasted_iota(jnp.int32, (1, bk), 1)
        st = [dict() for _ in chains]
        out = dict(dk=jnp.zeros((bk, HD), F32), dv=jnp.zeros((bk, HD), F32))

        def scores(c):
            g, hr, _ = chains[c]
            at = _q_at(heads, g, hr, HD)
            st[c]["q"] = q_ref[at] * ATTN_SCALE
            st[c]["do"] = do_ref[at]
            st[c]["s"] = lax.dot_general(st[c]["q"], kk, NT_DIMS, preferred_element_type=F32)
            st[c]["dp"] = lax.dot_general(st[c]["do"], vv, NT_DIMS, preferred_element_type=F32)

        def softmax(c):
            g, hr, _ = chains[c]
            qpos = start(m) + hr.start + lax.broadcasted_iota(jnp.int32, (hr.stop - hr.start, 1), 0)
            p = jnp.where(jnp.abs(qpos - kpos) <= radius, jnp.exp(st[c].pop("s") - lse_ref[0, g, hr, :]), 0.0)
            st[c]["ds"] = (p * (st[c].pop("dp") - dl_ref[0, g, hr, :])).astype(BF16)
            st[c]["p"] = p.astype(BF16)

        def grads(c):
            out["dv"] = out["dv"] + lax.dot_general(st[c].pop("p"), st[c].pop("do"), TN_DIMS, preferred_element_type=F32)
            out["dk"] = out["dk"] + lax.dot_general(st[c].pop("ds"), st[c].pop("q"), TN_DIMS, preferred_element_type=F32)

        _skewed(len(chains), (scores, softmax, grads))
        dk_ref[0] = out["dk"].astype(dk_ref.dtype)
        dv_ref[0] = out["dv"].astype(dv_ref.dtype)

    if heads is None:
        qspec = _win_specs(G, W, HD, start, True)
    else:
        qspec = pl.BlockSpec((pl.Element(W), pl.Element(G * HD)), lambda b, m: (start(m), b * G * HD))
    cspec = _win_specs(G, W, 1, start, True)
    kspec = pl.BlockSpec((1, bk, HD), lambda b, m: (b, m, 0))
    kv_shape = jax.ShapeDtypeStruct((NB, L, HD), BF16)
    return _pc(body, name=name, out_shape=(kv_shape, kv_shape), grid=(NB, L // bk),
               in_specs=[qspec, kspec, kspec, qspec, cspec, cspec], out_specs=(kspec, kspec),
               compiler_params=_cp("parallel", "parallel"))(q, k, v, do, lse, delta)


def _attn(q, k, v, *, radius=None, sink=None, bq, tk=None, heads=None, carry=None, tag):
    if radius is None and tk < k.shape[1]:
        return _attn_fwd_full(q, k, v, bq=bq, tk=tk, heads=heads, name=f"attn_fwd_{tag}")
    return _attn_fwd_win(q, k, v, radius=radius, sink=sink, bq=bq, heads=heads, carry=carry, name=f"attn_fwd_{tag}")


def _attn_bwd(q, k, v, o, lse, do, *, radius=None, sink=None, dlse=None, bq, tk=None, heads=None, tag):
    if radius is None:
        return (*_attn_bwd_full(q, k, v, do, lse, o, bq=bq, tk=tk, heads=heads, name=f"attn_bwd_{tag}"), None)
    nbg = None if heads is None else (k.shape[0], heads)
    if sink is not None:
        delta, ds = _attn_delta(do, o, lse=lse, sink=sink, heads=nbg, name=f"attn_delta_{tag}")
        dsink = ds[:, :, 0, 0].reshape(-1)
    else:
        delta, dsink = _attn_delta(do, o, dlse=dlse, heads=nbg, name=f"attn_delta_{tag}"), None
    dq = _attn_dq_win(q, k, v, do, lse, delta, radius=radius, bq=bq, heads=heads, name=f"attn_dq_{tag}")
    dk, dv = _attn_dkv_win(q, k, v, do, lse, delta, radius=radius, bk=bq, heads=heads, name=f"attn_dkv_{tag}")
    return dq, dk, dv, dsink


def _combine_fwd(o, lse, name="combine_fwd"):
    H, S, HD = o.shape
    tm = _row_tile(S, 512)

    def body(o_ref, lse_ref, t_ref):
        for g in range(GQA_GROUP):
            hs = [kv * GQA_GROUP + g for kv in range(N_KV_HEADS)]
            ls = [lse_ref[h] for h in hs]
            mx = functools.reduce(jnp.maximum, ls)
            es = [jnp.exp(l - mx) for l in ls]
            den = functools.reduce(jnp.add, es)
            for h, e in zip(hs, es):
                t_ref[h] = (o_ref[h].astype(F32) * (e / den)).astype(t_ref.dtype)

    blk = pl.BlockSpec((H, tm, HD), lambda i: (0, i, 0))
    col = pl.BlockSpec((H, tm, 1), lambda i: (0, i, 0))
    return _pc(body, name=name, out_shape=jax.ShapeDtypeStruct((H, S, HD), BF16), grid=(S // tm,),
               in_specs=[blk, col], out_specs=blk, compiler_params=_cp("parallel"))(o, lse)


def _combine_bwd(dt, o, lse, name="combine_bwd"):
    H, S, HD = o.shape
    tm = _row_tile(S, 512)

    def body(dt_ref, o_ref, lse_ref, do_ref, dlse_ref):
        for g in range(GQA_GROUP):
            hs = [kv * GQA_GROUP + g for kv in range(N_KV_HEADS)]
            ls = [lse_ref[h] for h in hs]
            mx = functools.reduce(jnp.maximum, ls)
            es = [jnp.exp(l - mx) for l in ls]
            den = functools.reduce(jnp.add, es)
            al = [e / den for e in es]
            dts = [dt_ref[h].astype(F32) for h in hs]
            da = [jnp.sum(d * o_ref[h].astype(F32), axis=-1, keepdims=True) for h, d in zip(hs, dts)]
            dot = functools.reduce(jnp.add, [a * d for a, d in zip(al, da)])
            for h, a, d, dd in zip(hs, al, da, dts):
                do_ref[h] = (dd * a).astype(do_ref.dtype)
                dlse_ref[h] = a * (d - dot)

    blk = pl.BlockSpec((H, tm, HD), lambda i: (0, i, 0))
    col = pl.BlockSpec((H, tm, 1), lambda i: (0, i, 0))
    return _pc(body, name=name,
               out_shape=(jax.ShapeDtypeStruct((H, S, HD), BF16), jax.ShapeDtypeStruct((H, S, 1), F32)),
               grid=(S // tm,), in_specs=[blk, blk, col], out_specs=(blk, col), compiler_params=_cp("parallel"))(dt, o, lse)


def _position():
    x, y, c = lax.axis_index("x"), lax.axis_index("y"), lax.axis_index("c")
    return x, y, c


def _peer(pos, k):
    x, y, c = pos
    return (1 - x if k & 4 else x, 1 - y if k & 2 else y, 1 - c if k & 1 else c)


def _linear(p):
    return 4 * p[0] + 2 * p[1] + p[2]


def _exchange_steps(s_ref, r_ref, send_sems, recv_sems, local_sem, gather):
    pos = _position()
    me = _linear(pos)
    own = pltpu.make_async_copy(s_ref if gather else s_ref.at[me], r_ref.at[me], local_sem)
    peers = range(1, N_DEV)

    def sems(k):
        return dict(send_sem=send_sems.at[k - 1], recv_sem=recv_sems.at[k - 1], device_id=_peer(pos, k), device_id_type=MESH)

    def send(k):
        src = s_ref if gather else s_ref.at[_linear(_peer(pos, k))]
        return pltpu.make_async_remote_copy(src_ref=src, dst_ref=r_ref.at[me], **sems(k))

    def arrival(k):
        slot = r_ref.at[_linear(_peer(pos, k))]
        return pltpu.make_async_remote_copy(src_ref=slot, dst_ref=slot, **sems(k))

    def start():
        own.start()
        for k in peers:
            send(k).start()

    def wait():
        for k in peers:
            arrival(k).wait_recv()
        for k in peers:
            send(k).wait_send()
        own.wait()

    return start, wait


EXCHANGE_SEMS = [pltpu.SemaphoreType.DMA((N_DEV - 1,)), pltpu.SemaphoreType.DMA((N_DEV - 1,)), pltpu.SemaphoreType.DMA]


def _exchange(buf, gather, name):
    def body(s_ref, r_ref, *sems):
        start, wait = _exchange_steps(s_ref, r_ref, *sems, gather)
        start()
        wait()

    hbm = pl.BlockSpec(memory_space=pltpu.HBM)
    out_shape = ((N_DEV,) + buf.shape) if gather else buf.shape
    return _pc(body, name=name, out_shape=jax.ShapeDtypeStruct(out_shape, buf.dtype), in_specs=[hbm], out_specs=hbm,
               scratch_shapes=list(EXCHANGE_SEMS))(buf)


def _call(body, carry, ins, *, name, out_shape, grid, in_specs, out_specs, scratch_shapes=(), sem):
    if carry is None:
        return _pc(body, name=name, out_shape=tuple(out_shape), grid=grid, in_specs=list(in_specs),
                   out_specs=tuple(out_specs), scratch_shapes=list(scratch_shapes), compiler_params=_cp(*sem))(*ins)
    buf, gather = carry
    n_in, n_out, n_sc = len(ins), len(out_shape), len(scratch_shapes)

    def wrapped(*refs):
        in_refs, buf_ref = refs[:n_in], refs[n_in]
        out_refs, recv_ref = refs[n_in + 1:n_in + 1 + n_out], refs[n_in + 1 + n_out]
        rest = refs[n_in + 2 + n_out:]
        first = functools.reduce(jnp.logical_and, [pl.program_id(a) == 0 for a in range(len(grid))])
        last = functools.reduce(jnp.logical_and, [pl.program_id(a) == grid[a] - 1 for a in range(len(grid))])

        @pl.when(first)
        def _():
            _exchange_steps(buf_ref, recv_ref, *rest[n_sc:], gather)[0]()

        body(*in_refs, *out_refs, *rest[:n_sc])

        @pl.when(last)
        def _():
            _exchange_steps(buf_ref, recv_ref, *rest[n_sc:], gather)[1]()

    hbm = pl.BlockSpec(memory_space=pltpu.HBM)
    recv_shape = ((N_DEV,) + buf.shape) if gather else buf.shape
    return _pc(wrapped, name=name, out_shape=(*out_shape, jax.ShapeDtypeStruct(recv_shape, buf.dtype)), grid=grid,
               in_specs=[*in_specs, hbm], out_specs=(*out_specs, hbm), scratch_shapes=[*scratch_shapes, *EXCHANGE_SEMS],
               compiler_params=_cp(*(("arbitrary",) * len(grid))))(*ins, buf)


def _reduce_adamw(recv, w, m, v, name):
    _, R, C = recv.shape
    tr = _row_tile(R, 512)

    def body(r_ref, w_ref, m_ref, v_ref, g_ref, d_ref, nm_ref, nv_ref):
        g = r_ref[0].astype(F32)
        for j in range(1, N_DEV):
            g = g + r_ref[j].astype(F32)
        g_ref[...] = g
        nm = ADAM_B1 * m_ref[...] + (1.0 - ADAM_B1) * g
        nv = ADAM_B2 * v_ref[...] + (1.0 - ADAM_B2) * jnp.square(g)
        m_hat = nm / (1.0 - ADAM_B1 ** ADAM_STEP)
        v_hat = nv / (1.0 - ADAM_B2 ** ADAM_STEP)
        d_ref[...] = -ADAM_LR * (m_hat / (jnp.sqrt(v_hat) + ADAM_EPS) + ADAM_WD * w_ref[...])
        nm_ref[...] = nm
        nv_ref[...] = nv

    row = pl.BlockSpec((tr, C), lambda i: (i, 0))
    out = jax.ShapeDtypeStruct((R, C), F32)
    return _pc(body, name=name, out_shape=(out, out, out, out), grid=(R // tr,),
               in_specs=[pl.BlockSpec((N_DEV, tr, C), lambda i: (0, i, 0)), row, row, row],
               out_specs=(row, row, row, row), compiler_params=_cp("parallel"))(recv, w, m, v)


BIG = (("w_in", 2), ("w_mem_kv", 1), ("w_o", 1), ("w_gate_up", 2), ("w_down", 1))
SMALL = ("mem_norm_g", "g_mix_pre", "g_mix_post", "attn_sink", "qk_norm_g", "g_ffn_pre", "g_ffn_post")
SMALL_W = 1024


def _pack_local(shards, dtype):
    return jnp.concatenate([s.astype(dtype).reshape(-1, LANES) for s in shards], axis=0)


def _unpack_local(flat, shapes):
    out, r = [], 0
    for shp in shapes:
        n = shp[0] * shp[1] * shp[2] // LANES
        out.append(flat[r:r + n].reshape(shp))
        r += n
    return out


def _unpack_gathered(g, shapes):
    out, r = [], 0
    for (name, dim), shp in zip(BIG, shapes):
        n = shp[0] * shp[1] * shp[2] // LANES
        t = g[:, r:r + n].reshape((N_DEV,) + tuple(shp))
        if dim == 2:
            t = t.transpose(1, 2, 0, 3).reshape(shp[0], shp[1], N_DEV * shp[2])
        else:
            t = t.transpose(1, 0, 2, 3).reshape(shp[0], N_DEV * shp[1], shp[2])
        out.append(t)
        r += n
    return out


def _pack_for_scatter(full, shapes, dtype):
    parts = []
    for (name, dim), shp, t in zip(BIG, shapes, full):
        if dim == 2:
            t = t.reshape(shp[0], shp[1], N_DEV, shp[2]).transpose(2, 0, 1, 3)
        else:
            t = t.reshape(shp[0], N_DEV, shp[1], shp[2]).transpose(1, 0, 2, 3)
        parts.append(t.astype(dtype).reshape(N_DEV, -1, LANES))
    return jnp.concatenate(parts, axis=1)


def _pack_small(arrs):
    flat = jnp.concatenate([a.reshape(-1) for a in arrs])
    pad = (-flat.shape[0]) % (8 * SMALL_W)
    return jnp.pad(flat, (0, pad)).reshape(-1, SMALL_W)


def _unpack_small(flat, shapes):
    flat = flat.reshape(-1)
    out, r = [], 0
    for shp in shapes:
        n = 1
        for d in shp:
            n *= d
        out.append(flat[r:r + n].reshape(shp))
        r += n
    return out


def _heads(t, nb, g):
    S = t.shape[0]
    return t.reshape(S, nb, g, HEAD_DIM).transpose(1, 2, 0, 3)


def _unheads(t):
    nb, g, S, hd = t.shape
    return t.transpose(2, 0, 1, 3).reshape(S, nb * g * hd)


def _dilate(t, dil):
    S = t.shape[0]
    g = t.shape[1] // HEAD_DIM
    return t.reshape(S // dil, dil, g, HEAD_DIM).transpose(1, 2, 0, 3)


def _undilate(t):
    dil, g, L, w = t.shape
    return t.transpose(1, 2, 0, 3).reshape(g, L * dil, w)


FULL_BQ_FWD, FULL_TK_FWD = 512, 4096
FULL_BQ_BWD, FULL_TK_BWD = 1024, 2048


def _mixer_fwd(kind, pr, kv, sink, li, carry=None):
    S = pr.shape[0]
    if kind == 0:
        tok, lse, *arrived = _attn(pr, *kv, radius=A_RADIUS, sink=sink, bq=min(256, S), heads=GQA_GROUP, carry=carry,
                                   tag=f"a{li}")
        return tok, lse, (arrived[0] if arrived else None)
    assert carry is None
    if kind == 1:
        tok, lse = _attn(pr, *kv, bq=min(FULL_BQ_FWD, S), tk=min(FULL_TK_FWD, S), heads=GQA_GROUP, tag=f"b{li}")
        return tok, lse, None
    saved, outs, lses = [], [], []
    for g, (window, dil) in enumerate(C_GROUPS):
        q = _dilate(pr[:, g * GQA_GROUP * HEAD_DIM:(g + 1) * GQA_GROUP * HEAD_DIM], dil)
        k = _dilate(pr[:, Q_W + g * HEAD_DIM:Q_W + (g + 1) * HEAD_DIM], dil)[:, 0]
        v = _dilate(pr[:, QK_W + g * HEAD_DIM:QK_W + (g + 1) * HEAD_DIM], dil)[:, 0]
        o, lse = _attn(q, k, v, radius=window // (2 * dil), bq=min(256, S // dil), tag=f"c{li}g{g}")
        saved.append((q, k, v, o, lse))
        outs.append(_undilate(o))
        lses.append(_undilate(lse))
    o_all, lse_all = jnp.concatenate(outs, 0), jnp.concatenate(lses, 0)
    tok = _combine_fwd(o_all, lse_all, name=f"combine_fwd_{li}")
    return tok.transpose(1, 0, 2).reshape(S, Q_W), (saved, o_all, lse_all), None


def _mixer_bwd(kind, dcat, pr, kv, cat, saved, sink, li):
    S = dcat.shape[0]
    if kind == 0:
        return _attn_bwd(pr, *kv, cat, saved, dcat, radius=A_RADIUS, sink=sink, bq=min(256, S), heads=GQA_GROUP, tag=f"a{li}")
    if kind == 1:
        return _attn_bwd(pr, *kv, cat, saved, dcat, bq=min(FULL_BQ_BWD, S), tk=min(FULL_TK_BWD, S), heads=GQA_GROUP,
                         tag=f"b{li}")
    per_group, o_all, lse_all = saved
    dt = dcat[:, :Q_W].reshape(S, N_TOK_HEADS, HEAD_DIM).transpose(1, 0, 2)
    do_all, dlse_all = _combine_bwd(dt, o_all, lse_all, name=f"combine_bwd_{li}")
    dqs, dks, dvs = [], [], []
    for g, (window, dil) in enumerate(C_GROUPS):
        q, k, v, o, lse = per_group[g]
        L = S // dil
        hs = slice(g * GQA_GROUP, (g + 1) * GQA_GROUP)
        do = do_all[hs].reshape(GQA_GROUP, L, dil, HEAD_DIM).transpose(2, 0, 1, 3)
        dlse = dlse_all[hs].reshape(GQA_GROUP, L, dil, 1).transpose(2, 0, 1, 3)
        dq, dk, dv, _ = _attn_bwd(q, k, v, o, lse, do, radius=window // (2 * dil), dlse=dlse, bq=min(256, L),
                                  tag=f"c{li}g{g}")
        dqs.append(dq.transpose(2, 0, 1, 3).reshape(S, GQA_GROUP * HEAD_DIM))
        dks.append(dk.transpose(1, 0, 2).reshape(S, HEAD_DIM))
        dvs.append(dv.transpose(1, 0, 2).reshape(S, HEAD_DIM))
    return jnp.concatenate(dqs, 1), jnp.concatenate(dks, 1), jnp.concatenate(dvs, 1), None


def kernel(x, mem, mem_norm_g, w_in, w_mem_kv, w_o, g_mix_pre, g_mix_post, attn_sink, qk_norm_g, w_gate_up, w_down, g_ffn_pre, g_ffn_post, loss_target, m_mem_norm_g, m_w_in, m_w_mem_kv, m_w_o, m_g_mix_pre, m_g_mix_post, m_attn_sink, m_qk_norm_g, m_w_gate_up, m_w_down, m_g_ffn_pre, m_g_ffn_post, v_mem_norm_g, v_w_in, v_w_mem_kv, v_w_o, v_g_mix_pre, v_g_mix_post, v_attn_sink, v_qk_norm_g, v_w_gate_up, v_w_down, v_g_ffn_pre, v_g_ffn_post):
    given = dict(locals())
    depth = w_in.shape[0]
    S, D = x.shape[1], x.shape[2]
    layer_shapes = [(1,) + tuple(given[n].shape[1:]) for n, _ in BIG]

    def layer_pack(pre, l, dtype):
        return _pack_local([given[pre + n][l:l + 1] for n, _ in BIG], dtype)

    def layer_weights(gathered):
        return [t[0] for t in _unpack_gathered(gathered, layer_shapes)]

    W = [None] * depth
    W[0] = layer_weights(_exchange(layer_pack("", 0, BF16), True, "gather_w0"))

    tabs = _rope_tables(S)
    mem_n = _rms_fwd(mem[0], mem_norm_g[None], BF16, name="rms_mem")

    saved = []
    xc = x[0]
    for i in range(depth):
        kind = i % N_MIXERS
        (tab, shift) = tabs[1] if kind == 1 else tabs[0]
        sink = attn_sink[i // N_MIXERS] if kind == 0 else None
        qk_gain = _qk_gain_row(qk_norm_g[i // N_MIXERS]) if kind == 1 else None
        W_in, W_mkv, W_o, W_gu, W_dn = W[i]
        h, proj = _mm(xc, W_in, F32, pre_g=g_mix_pre[i][None], name=f"mm_in_{i}")
        if kind == 2:
            pr, kv = _headprep_fwd(proj, tab, shift, qk_gain, name=f"headprep_fwd_{i}"), None
        else:
            pr, *kv = _headprep_fwd(proj, tab, shift, qk_gain, kv_heads=True, name=f"headprep_fwd_{i}")
        carry = (layer_pack("", 1, BF16), True) if i == 0 and depth > 1 else None
        tok, mix_saved, arrived = _mixer_fwd(kind, pr, kv, sink, i, carry)
        if carry is not None:
            W[1] = layer_weights(arrived)
        (mkv,) = _mm(mem_n, W_mkv, BF16, name=f"mm_mkv_{i}")
        qm = _heads(pr[:, QK_W + KV_W:], N_MEM_HEADS, 1)
        km = _heads(mkv[:, :QM_W], N_MEM_HEADS, 1)[:, 0]
        vm = _heads(mkv[:, QM_W:], N_MEM_HEADS, 1)[:, 0]
        mo, mlse = _attn(qm, km, vm, bq=min(1024, S), tk=km.shape[1], tag=f"m{i}")
        cat = jnp.concatenate([tok, _unheads(mo)], axis=1)
        o, x1 = _mm(cat, W_o, F32, post=(g_mix_post[i][None], xc), name=f"mm_o_{i}")
        carry = (layer_pack("", i + 2, BF16), True) if i + 2 < depth else None
        h2, gu, act, *arrived = _mm(x1, W_gu, BF16, pre_g=g_ffn_pre[i][None], swiglu=True, carry=carry, name=f"mm_gu_{i}")
        if carry is not None:
            W[i + 2] = layer_weights(arrived[0])
        f, x2 = _mm(act, W_dn, F32, post=(g_ffn_post[i][None], x1), name=f"mm_dn_{i}")
        saved.append(dict(x=xc, h=h, proj=proj, pr=pr, kv=kv, mix=mix_saved, qm=qm, km=km, vm=vm, mo=mo, mlse=mlse, cat=cat, o=o,
                          x1=x1, h2=h2, gu=gu, act=act, f=f))
        xc = x2

    dy, sq = _loss_head(xc, loss_target[0], name="loss_head")
    loss = lax.psum(sq[0, 0] * (0.5 / D), ("x", "y", "c"))

    grads = {n: [None] * depth for n in ("w_in", "w_mem_kv", "w_o", "w_gate_up", "w_down", "g_mix_pre", "g_mix_post",
                                         "g_ffn_pre", "g_ffn_post")}
    d_sink = [jnp.zeros((N_TOK_HEADS,), F32) for _ in range(attn_sink.shape[0])]
    d_qkg = [jnp.zeros((2, HEAD_DIM), F32) for _ in range(qk_norm_g.shape[0])]
    dmem_n = jnp.zeros((mem.shape[1], D), F32)
    recv = [None] * depth

    def scatter_pack(l):
        return _pack_for_scatter([grads[n][l][None] for n, _ in BIG], layer_shapes, BF16)

    dx = dy
    for i in reversed(range(depth)):
        kind = i % N_MIXERS
        sv = saved[i]
        (tab, shift) = tabs[1] if kind == 1 else tabs[0]
        sink = attn_sink[i // N_MIXERS] if kind == 0 else None
        W_in, W_mkv, W_o, W_gu, W_dn = W[i]
        df, dgu, dg = _mm(dx, W_dn, BF16, nt=True, gu=sv["gu"], pre_bwd=(sv["f"], g_ffn_post[i][None]), name=f"mmb_dn_{i}")
        grads["g_ffn_post"][i] = dg[0]
        grads["w_down"][i] = _mm_tn(sv["act"], df, name=f"mmw_dn_{i}")
        dx1, dg = _mm(dgu, W_gu, F32, nt=True, post_bwd=(sv["x1"], g_ffn_pre[i][None], dx), name=f"mmb_gu_{i}")
        grads["g_ffn_pre"][i] = dg[0]
        if i + 1 < depth:
            grads["w_gate_up"][i], recv[i + 1] = _mm_tn(sv["h2"], dgu, carry=(scatter_pack(i + 1), False), name=f"mmw_gu_{i}")
        else:
            grads["w_gate_up"][i] = _mm_tn(sv["h2"], dgu, name=f"mmw_gu_{i}")
        do, dcat, dg = _mm(dx1, W_o, BF16, nt=True, pre_bwd=(sv["o"], g_mix_post[i][None]), name=f"mmb_o_{i}")
        grads["g_mix_post"][i] = dg[0]
        grads["w_o"][i] = _mm_tn(sv["cat"], do, name=f"mmw_o_{i}")
        dmo = _heads(dcat[:, Q_W:], N_MEM_HEADS, 1)
        dqm, dkm, dvm, _ = _attn_bwd(sv["qm"], sv["km"], sv["vm"], sv["mo"], sv["mlse"], dmo, bq=min(1024, S),
                                     tk=sv["km"].shape[1], tag=f"m{i}")
        dmkv = jnp.concatenate([_unheads(dkm[:, None]), _unheads(dvm[:, None])], axis=1).astype(BF16)
        grads["w_mem_kv"][i] = _mm_tn(mem_n, dmkv, name=f"mmw_mkv_{i}")
        dmem_n = dmem_n + _mm(dmkv, W_mkv, F32, nt=True, name=f"mmb_mkv_{i}")[0]
        dq, dk, dv, dsink = _mixer_bwd(kind, dcat, sv["pr"], sv["kv"], sv["cat"], sv["mix"], sink, i)
        if dsink is not None:
            d_sink[i // N_MIXERS] = dsink
        dpr = jnp.concatenate([dq, dk, dv, _unheads(dqm)], axis=1) if kind == 2 else (dq, dk, dv, _unheads(dqm))
        if kind == 1:
            dproj, dgc = _headprep_bwd(dpr, tab, shift, sv["proj"], _qk_gain_row(qk_norm_g[i // N_MIXERS]),
                                       name=f"headprep_bwd_{i}")
            d_qkg[i // N_MIXERS] = jnp.stack([dgc[0, :Q_W].reshape(N_TOK_HEADS, HEAD_DIM).sum(0),
                                              dgc[0, Q_W:QK_W].reshape(N_KV_HEADS, HEAD_DIM).sum(0)])
        else:
            dproj = _headprep_bwd(dpr, tab, shift, name=f"headprep_bwd_{i}")
        dx, dg = _mm(dproj, W_in, F32, nt=True, post_bwd=(sv["x"], g_mix_pre[i][None], dx1), name=f"mmb_in_{i}")
        grads["g_mix_pre"][i] = dg[0]
        grads["w_in"][i] = _mm_tn(sv["h"], dproj, name=f"mmw_in_{i}")
    _, dg_mem = _rms_bwd(mem[0], mem_norm_g[None], dmem_n, BF16, name="rmsb_mem")

    recv[0] = _exchange(scatter_pack(0), False, "scatter_g0")
    per_layer = [_reduce_adamw(recv[l], layer_pack("", l, F32), layer_pack("m_", l, F32), layer_pack("v_", l, F32),
                               name=f"adamw_{l}") for l in range(depth)]
    big_out = [[jnp.concatenate(ts, axis=0) for ts in zip(*[_unpack_local(per_layer[l][j], layer_shapes) for l in range(depth)])]
               for j in range(4)]

    small_grads = dict(mem_norm_g=dg_mem[0], g_mix_pre=jnp.stack(grads["g_mix_pre"]), g_mix_post=jnp.stack(grads["g_mix_post"]),
                       attn_sink=jnp.stack(d_sink), qk_norm_g=jnp.stack(d_qkg), g_ffn_pre=jnp.stack(grads["g_ffn_pre"]),
                       g_ffn_post=jnp.stack(grads["g_ffn_post"]))
    sg = _pack_small([small_grads[n] for n in SMALL])
    srecv = _exchange(sg, True, "gather_small_grads")
    spacked = lambda pre: _pack_small([given[pre + n] for n in SMALL])
    gs, ds, ms, vs = _reduce_adamw(srecv, spacked(""), spacked("m_"), spacked("v_"), name="adamw_small")

    out = {}
    for pre, fb, fs in zip(("grad_", "delta_", "new_m_", "new_v_"), big_out, (gs, ds, ms, vs)):
        for (n, _), t in zip(BIG, fb):
            out[pre + n] = t
        for n, t in zip(SMALL, _unpack_small(fs, [given[n].shape for n in SMALL])):
            out[pre + n] = t
    order = ("mem_norm_g", "w_in", "w_mem_kv", "w_o", "g_mix_pre", "g_mix_post", "attn_sink", "qk_norm_g", "w_gate_up",
             "w_down", "g_ffn_pre", "g_ffn_post")
    return (loss, dx[None], *[out[p + n] for p in ("grad_", "delta_", "new_m_", "new_v_") for n in order])
```

```python
import functools

import jax
import jax.numpy as jnp
from jax import lax
from jax.experimental import pallas as pl
from jax.experimental.pallas import tpu as pltpu

F32 = jnp.float32
BF16 = jnp.bfloat16

HEAD_DIM = 64
N_TOK_HEADS = 12
N_KV_HEADS = 3
GQA_GROUP = 4
N_MEM_HEADS = 4
Q_W = N_TOK_HEADS * HEAD_DIM
KV_W = N_KV_HEADS * HEAD_DIM
QM_W = N_MEM_HEADS * HEAD_DIM
QK_W = Q_W + KV_W
IN_W = Q_W + 2 * KV_W + QM_W
N_MIXERS = 3
A_RADIUS = 128
C_GROUPS = ((128, 1), (512, 4), (2048, 16))
ROPE_THETA = 500000.0
ROPE_DIMS = HEAD_DIM // 4
AXIAL_THETA = 10000.0
GRID_W = 64
EPS = 1e-6
ATTN_SCALE = HEAD_DIM ** -0.5
NEG = -1e30

ADAM_LR = 0.001
ADAM_B1 = 0.9
ADAM_B2 = 0.999
ADAM_EPS = 1e-08
ADAM_WD = 0.01
ADAM_STEP = 10

N_DEV = 8
LANES = 128
VMEM_LIMIT = 56 * 1024 * 1024
MESH = pl.DeviceIdType.MESH
NT_DIMS = (((1,), (1,)), ((), ()))
TN_DIMS = (((0,), (0,)), ((), ()))


def _pc(body, **kw):
    return pl.pallas_call(body, **kw)


def _cp(*sem):
    return pltpu.CompilerParams(dimension_semantics=sem, vmem_limit_bytes=VMEM_LIMIT)


def _row_tile(m, cap=512):
    t = cap
    while m % t:
        t //= 2
    return t


def _rms_fwd(x, g, out_dtype, name="rms_fwd"):
    M, D = x.shape
    tm = _row_tile(M)

    def body(x_ref, g_ref, o_ref):
        xv = x_ref[...]
        y = xv * lax.rsqrt(jnp.mean(xv * xv, axis=-1, keepdims=True) + EPS) * g_ref[...]
        o_ref[...] = y.astype(o_ref.dtype)

    row = pl.BlockSpec((tm, D), lambda i: (i, 0))
    vec = pl.BlockSpec((1, D), lambda i: (0, 0))
    return _pc(body, name=name, out_shape=jax.ShapeDtypeStruct((M, D), out_dtype), grid=(M // tm,),
               in_specs=[row, vec], out_specs=row, compiler_params=_cp("parallel"))(x, g)


def _rms_bwd_tile(xv, g, d):
    r = lax.rsqrt(jnp.mean(xv * xv, axis=-1, keepdims=True) + EPS)
    xh = xv * r
    dxh = d * g
    return r * (dxh - xh * jnp.mean(dxh * xh, axis=-1, keepdims=True)), jnp.sum(d * xh, axis=0, keepdims=True)


def _rms_bwd(x, g, dy, out_dtype, name="rms_bwd"):
    M, D = x.shape
    tm = _row_tile(M)

    def body(x_ref, g_ref, dy_ref, dx_ref, dg_ref):
        dx, dg = _rms_bwd_tile(x_ref[...], g_ref[...], dy_ref[...].astype(F32))
        dx_ref[...] = dx.astype(dx_ref.dtype)

        @pl.when(pl.program_id(0) == 0)
        def _():
            dg_ref[...] = jnp.zeros_like(dg_ref)

        dg_ref[...] += dg

    row = pl.BlockSpec((tm, D), lambda i: (i, 0))
    vec = pl.BlockSpec((1, D), lambda i: (0, 0))
    return _pc(body, name=name,
               out_shape=(jax.ShapeDtypeStruct((M, D), out_dtype), jax.ShapeDtypeStruct((1, D), F32)),
               grid=(M // tm,), in_specs=[row, vec, row], out_specs=(row, vec), compiler_params=_cp("arbitrary"))(x, g, dy)


def _loss_head(y, t, name="loss_head"):
    M, D = y.shape
    tm = _row_tile(M)

    def body(y_ref, t_ref, dy_ref, acc_ref):
        e = y_ref[...] - t_ref[...]
        dy_ref[...] = e * (1.0 / D)

        @pl.when(pl.program_id(0) == 0)
        def _():
            acc_ref[...] = jnp.zeros_like(acc_ref)

        acc_ref[...] += jnp.sum(e * e)

    row = pl.BlockSpec((tm, D), lambda i: (i, 0))
    return _pc(body, name=name,
               out_shape=(jax.ShapeDtypeStruct((M, D), F32), jax.ShapeDtypeStruct((8, LANES), F32)),
               grid=(M // tm,), in_specs=[row, row],
               out_specs=(row, pl.BlockSpec((8, LANES), lambda i: (0, 0))), compiler_params=_cp("arbitrary"))(y, t)


def _mm(a, w, out_dtype, nt=False, pre_g=None, swiglu=False, post=None, gu=None, pre_bwd=None, post_bwd=None, carry=None,
        tm=512, name="mm"):
    M, K = a.shape
    N = w.shape[0] if nt else w.shape[1]
    tm = _row_tile(M, tm)
    gain_grad = pre_bwd is not None or post_bwd is not None

    def body(*refs):
        refs = list(refs)
        a_ref, w_ref = refs.pop(0), refs.pop(0)
        pg_ref = refs.pop(0) if pre_g is not None else None
        g_ref, r_ref = (refs.pop(0), refs.pop(0)) if post is not None else (None, None)
        gu_ref = refs.pop(0) if gu is not None else None
        bwd_refs = [refs.pop(0) for _ in (pre_bwd or post_bwd or ())]
        lhs = a_ref[...]
        if pre_g is not None:
            lhs = (lhs * lax.rsqrt(jnp.mean(lhs * lhs, axis=-1, keepdims=True) + EPS) * pg_ref[...]).astype(BF16)
            refs.pop(0)[...] = lhs
        if pre_bwd is not None:
            lhs, dg = _rms_bwd_tile(bwd_refs[0][...], bwd_refs[1][...], lhs.astype(F32))
            lhs = lhs.astype(BF16)
            refs.pop(0)[...] = lhs
        if nt:
            acc = lax.dot_general(lhs, w_ref[...], NT_DIMS, preferred_element_type=F32)
        else:
            acc = jnp.dot(lhs, w_ref[...], preferred_element_type=F32)
        o_ref = refs.pop(0)
        if post_bwd is not None:
            dx, dg = _rms_bwd_tile(bwd_refs[0][...], bwd_refs[1][...], acc)
            o_ref[...] = bwd_refs[2][...] + dx
        elif gu is None:
            o_ref[...] = acc.astype(o_ref.dtype)
        else:
            gate = gu_ref[:, :N].astype(F32)
            sig = 1.0 / (1.0 + jnp.exp(-gate))
            o_ref[:, :N] = (acc * gu_ref[:, N:].astype(F32) * (sig * (1.0 + gate * (1.0 - sig)))).astype(o_ref.dtype)
            o_ref[:, N:] = (acc * (gate * sig)).astype(o_ref.dtype)
        if swiglu:
            gate = acc[:, : N // 2]
            refs.pop(0)[...] = (gate * (1.0 / (1.0 + jnp.exp(-gate))) * acc[:, N // 2:]).astype(BF16)
        if post is not None:
            y = acc * lax.rsqrt(jnp.mean(acc * acc, axis=-1, keepdims=True) + EPS) * g_ref[...]
            refs.pop(0)[...] = r_ref[...] + y
        if gain_grad:
            dg_ref = refs.pop(0)

            @pl.when(pl.program_id(0) == 0)
            def _():
                dg_ref[...] = jnp.zeros_like(dg_ref)

            dg_ref[...] += dg

    row = lambda n: pl.BlockSpec((tm, n), lambda i: (i, 0))
    vec = lambda n: pl.BlockSpec((1, n), lambda i: (0, 0))
    ins = [a, w]
    specs = [row(K), pl.BlockSpec(w.shape, lambda i: (0, 0), pipeline_mode=pl.Buffered(1))]
    outs, ospecs = [], []
    if pre_g is not None:
        ins, specs = ins + [pre_g], specs + [vec(K)]
    if pre_g is not None or pre_bwd is not None:
        outs, ospecs = outs + [jax.ShapeDtypeStruct((M, K), BF16)], ospecs + [row(K)]
    if post is not None:
        ins, specs = ins + list(post), specs + [vec(N), row(N)]
    if gu is not None:
        ins, specs = ins + [gu], specs + [row(2 * N)]
        outs, ospecs = outs + [jax.ShapeDtypeStruct((M, 2 * N), BF16)], ospecs + [row(2 * N)]
    else:
        outs, ospecs = outs + [jax.ShapeDtypeStruct((M, N), out_dtype)], ospecs + [row(N)]
    if pre_bwd is not None:
        ins, specs = ins + list(pre_bwd), specs + [row(K), vec(K)]
    if post_bwd is not None:
        ins, specs = ins + list(post_bwd), specs + [row(N), vec(N), row(N)]
    if swiglu:
        outs, ospecs = outs + [jax.ShapeDtypeStruct((M, N // 2), BF16)], ospecs + [row(N // 2)]
    if post is not None:
        outs, ospecs = outs + [jax.ShapeDtypeStruct((M, N), F32)], ospecs + [row(N)]
    if gain_grad:
        D = K if pre_bwd is not None else N
        outs, ospecs = outs + [jax.ShapeDtypeStruct((1, D), F32)], ospecs + [vec(D)]
    return _call(body, carry, ins, name=name, out_shape=outs, grid=(M // tm,), in_specs=specs, out_specs=ospecs,
                 sem=("arbitrary" if gain_grad else "parallel",))


def _mm_tn(a, b, carry=None, name="mm_tn"):
    S, M = a.shape
    N = b.shape[1]
    tm = M if M <= 1408 else M // 2
    tn = N if N <= 1408 else N // 4
    ts = _row_tile(S, 1024)

    def body(a_ref, b_ref, o_ref):
        @pl.when(pl.program_id(2) == 0)
        def _():
            o_ref[...] = jnp.zeros_like(o_ref)

        o_ref[...] += lax.dot_general(a_ref[...], b_ref[...], TN_DIMS, preferred_element_type=F32)

    res = _call(body, carry, [a, b], name=name, out_shape=[jax.ShapeDtypeStruct((M, N), F32)],
                grid=(M // tm, N // tn, S // ts),
                in_specs=[pl.BlockSpec((ts, tm), lambda i, j, s: (s, i)), pl.BlockSpec((ts, tn), lambda i, j, s: (s, j))],
                out_specs=[pl.BlockSpec((tm, tn), lambda i, j, s: (i, j))], sem=("parallel", "parallel", "arbitrary"))
    return res[0] if carry is None else res


def _rope_tables(S):
    pos = jnp.arange(S, dtype=jnp.int32)

    def table(p, n_dims, theta):
        inv = theta ** (-(jnp.arange(0, n_dims, 2, dtype=F32) / n_dims))
        ang = p.astype(F32)[:, None] * inv[None, :]
        return jnp.cos(ang), jnp.sin(ang)

    one = lambda n: jnp.ones((S, n), F32)
    zero = lambda n: jnp.zeros((S, n), F32)
    cp, sp = table(pos, ROPE_DIMS, ROPE_THETA)
    rest = HEAD_DIM - ROPE_DIMS
    part = (jnp.concatenate([cp, cp, one(rest)], 1), jnp.concatenate([zero(8), sp, zero(rest)], 1),
            jnp.concatenate([-sp, zero(8), zero(rest)], 1))
    cr, sr = table(pos // GRID_W, HEAD_DIM // 2, AXIAL_THETA)
    cc, sc = table(pos % GRID_W, HEAD_DIM // 2, AXIAL_THETA)
    axial = (jnp.concatenate([cr, cr, cc, cc], 1), jnp.concatenate([zero(16), sr, zero(16), sc], 1),
             jnp.concatenate([-sr, zero(16), -sc, zero(16)], 1))
    rep = LANES // HEAD_DIM
    return (tuple(jnp.tile(t, (1, rep)) for t in part), ROPE_DIMS // 2), (tuple(jnp.tile(t, (1, rep)) for t in axial), HEAD_DIM // 4)


def _seg_mats():
    col = jnp.arange(IN_W)[:, None] // HEAD_DIM
    e = (col == jnp.arange(LANES)[None, :]).astype(BF16)
    return e, e.T


def _qk_gain_row(qk_g):
    return jnp.concatenate([jnp.tile(qk_g[0], N_TOK_HEADS), jnp.tile(qk_g[1], N_KV_HEADS),
                            jnp.ones((IN_W - QK_W,), F32)])[None, :]


def _rope_cols(tabs, tm):
    col = lax.broadcasted_iota(jnp.int32, (tm, IN_W), 1)
    qk = col < QK_W
    c, s_lo, s_hi = (jnp.tile(t[...], (1, IN_W // LANES)) for t in tabs)
    return jnp.where(qk, c, 1.0), jnp.where(qk, s_lo, 0.0), jnp.where(qk, s_hi, 0.0), qk


def _seg_mean(v, e_ref, et_ref):
    def split_dot(t, m_ref):
        hi = t.astype(BF16)
        lo = (t - hi.astype(F32)).astype(BF16)
        return jnp.dot(hi, m_ref[...], preferred_element_type=F32) + jnp.dot(lo, m_ref[...], preferred_element_type=F32)

    return split_dot(split_dot(v, e_ref) * (1.0 / HEAD_DIM), et_ref)


def _headprep_fwd(proj, tabs, shift, qk_gain=None, kv_heads=False, name="headprep_fwd"):
    S = proj.shape[0]
    tm = _row_tile(S, 256)
    norm = qk_gain is not None

    def body(*refs):
        refs = list(refs)
        p_ref, c_ref, lo_ref, hi_ref = (refs.pop(0) for _ in range(4))
        g_ref, e_ref, et_ref = (refs.pop(0) for _ in range(3)) if norm else (None, None, None)
        o_ref = refs.pop(0)
        x = p_ref[...]
        c, s_lo, s_hi, qk = _rope_cols((c_ref, lo_ref, hi_ref), tm)
        if norm:
            r = lax.rsqrt(_seg_mean(x * x, e_ref, et_ref) + EPS)
            x = x * jnp.where(qk, r, 1.0) * g_ref[...]
        y = (x * c + pltpu.roll(x, shift, 1) * s_lo + pltpu.roll(x, IN_W - shift, 1) * s_hi).astype(o_ref.dtype)
        o_ref[...] = y
        if kv_heads:
            k_ref, v_ref = refs
            for h in range(N_KV_HEADS):
                k_ref[h] = y[:, Q_W + h * HEAD_DIM:Q_W + (h + 1) * HEAD_DIM]
                v_ref[h] = y[:, QK_W + h * HEAD_DIM:QK_W + (h + 1) * HEAD_DIM]

    row = pl.BlockSpec((tm, IN_W), lambda i: (i, 0))
    tab = pl.BlockSpec((tm, LANES), lambda i: (i, 0))
    ins = [proj, *tabs]
    specs = [row, tab, tab, tab]
    if norm:
        e, et = _seg_mats()
        ins += [qk_gain, e, et]
        specs += [pl.BlockSpec((1, IN_W), lambda i: (0, 0)), pl.BlockSpec((IN_W, LANES), lambda i: (0, 0)),
                  pl.BlockSpec((LANES, IN_W), lambda i: (0, 0))]
    out_shape, out_specs = [jax.ShapeDtypeStruct((S, IN_W), BF16)], [row]
    if kv_heads:
        out_shape += [jax.ShapeDtypeStruct((N_KV_HEADS, S, HEAD_DIM), BF16)] * 2
        out_specs += [pl.BlockSpec((N_KV_HEADS, tm, HEAD_DIM), lambda i: (0, i, 0))] * 2
    res = _pc(body, name=name, out_shape=tuple(out_shape), grid=(S // tm,),
              in_specs=specs, out_specs=tuple(out_specs), compiler_params=_cp("parallel"))(*ins)
    return res if kv_heads else res[0]


def _headprep_bwd(dpr, tabs, shift, proj=None, qk_gain=None, name="headprep_bwd"):
    parts = isinstance(dpr, (tuple, list))
    S = dpr[0].shape[0] if parts else dpr.shape[0]
    tm = _row_tile(S, 256)
    norm = qk_gain is not None

    def body(*refs):
        refs = list(refs)
        d_refs = [refs.pop(0) for _ in range(4 if parts else 1)]
        c_ref, lo_ref, hi_ref = (refs.pop(0) for _ in range(3))
        if norm:
            p_ref, g_ref, e_ref, et_ref, o_ref, dg_ref = refs
        else:
            (o_ref,) = refs
        if parts:
            dq_ref, dk_ref, dv_ref, dqm_ref = d_refs
            d = jnp.concatenate([dq_ref[...]] + [dk_ref[h] for h in range(N_KV_HEADS)]
                                + [dv_ref[h] for h in range(N_KV_HEADS)] + [dqm_ref[...]], axis=1).astype(F32)
        else:
            d = d_refs[0][...].astype(F32)
        c, s_lo, s_hi, qk = _rope_cols((c_ref, lo_ref, hi_ref), tm)
        dx = d * c + pltpu.roll(d * s_lo, IN_W - shift, 1) + pltpu.roll(d * s_hi, shift, 1)
        if norm:
            x = p_ref[...]
            r = lax.rsqrt(_seg_mean(x * x, e_ref, et_ref) + EPS)
            xh = x * r

            @pl.when(pl.program_id(0) == 0)
            def _():
                dg_ref[...] = jnp.zeros_like(dg_ref)

            dg_ref[...] += jnp.sum(jnp.where(qk, dx * xh, 0.0), axis=0, keepdims=True)
            dxh = dx * g_ref[...]
            dn = r * (dxh - xh * _seg_mean(dxh * xh, e_ref, et_ref))
            dx = jnp.where(qk, dn, dx)
        o_ref[...] = dx.astype(o_ref.dtype)

    row = pl.BlockSpec((tm, IN_W), lambda i: (i, 0))
    tab = pl.BlockSpec((tm, LANES), lambda i: (i, 0))
    vec = pl.BlockSpec((1, IN_W), lambda i: (0, 0))
    if parts:
        heads = pl.BlockSpec((N_KV_HEADS, tm, HEAD_DIM), lambda i: (0, i, 0))
        ins = [*dpr, *tabs]
        specs = [pl.BlockSpec((tm, Q_W), lambda i: (i, 0)), heads, heads, pl.BlockSpec((tm, QM_W), lambda i: (i, 0)), tab, tab, tab]
    else:
        ins = [dpr, *tabs]
        specs = [row, tab, tab, tab]
    out_shape = jax.ShapeDtypeStruct((S, IN_W), BF16)
    out_specs = row
    if norm:
        e, et = _seg_mats()
        ins += [proj, qk_gain, e, et]
        specs += [row, vec, pl.BlockSpec((IN_W, LANES), lambda i: (0, 0)), pl.BlockSpec((LANES, IN_W), lambda i: (0, 0))]
        out_shape = (out_shape, jax.ShapeDtypeStruct((1, IN_W), F32))
        out_specs = (row, vec)
    return _pc(body, name=name, out_shape=out_shape, grid=(S // tm,), in_specs=specs, out_specs=out_specs,
               compiler_params=_cp("arbitrary" if norm else "parallel"))(*ins)


CHAIN_ROWS = 128


def _skewed(n, stages):
    for t in range(n + len(stages) - 1):
        for s, stage in enumerate(stages):
            if 0 <= t - s < n:
                stage(t - s)


def _chain_slices(G, bq):
    cr = min(CHAIN_ROWS, bq)
    per = bq // cr
    return [(c // per, slice((c % per) * cr, (c % per + 1) * cr), slice(c * cr, (c + 1) * cr)) for c in range(G * per)]


def _v_ones(v):
    return jnp.concatenate([v, jnp.ones(v.shape, v.dtype)], axis=1)


def _q_dims(q, k, heads):
    if heads is None:
        return q.shape
    return k.shape[0], heads, q.shape[0], k.shape[2]


def _q_shape(heads, NB, G, L, HD):
    return (NB, G, L, HD) if heads is None else (L, NB * G * HD)


def _q_spec(heads, G, rows, HD, index):
    if heads is None:
        return pl.BlockSpec((1, G, rows, HD), lambda *ids: (index(*ids)[0], 0, index(*ids)[1], 0))
    return pl.BlockSpec((rows, G * HD), lambda *ids: index(*ids)[::-1])


def _q_at(heads, g, hr, HD):
    return (0, g, hr, slice(None)) if heads is None else (hr, slice(g * HD, (g + 1) * HD))


def _window(L, blk, radius):
    if radius is None:
        return L, None
    W = min(L, blk + 2 * radius)
    assert blk % radius == 0 and (L - W) % radius == 0
    return W, lambda n: radius * jnp.clip(n * (blk // radius) - 1, 0, (L - W) // radius)


def _win_specs(G, W, HD, start, with_g):
    E = pl.Element
    st = (lambda n: 0) if start is None else start
    if with_g:
        return pl.BlockSpec((E(1), E(G), E(W), E(HD)), lambda b, n: (b, 0, st(n), 0))
    return pl.BlockSpec((E(1), E(W), E(HD)), lambda b, n: (b, st(n), 0))


def _attn_delta(do, o, *, dlse=None, lse=None, sink=None, heads=None, name="attn_delta"):
    HD = HEAD_DIM
    (NB, G), L = (heads, do.shape[0]) if heads is not None else (do.shape[:2], do.shape[2])
    bl = _row_tile(L, 1024)

    def body(*refs):
        refs = list(refs)
        sink_ref = refs.pop(0) if sink is not None else None
        do_ref, o_ref = refs.pop(0), refs.pop(0)
        dlse_ref = refs.pop(0) if dlse is not None else None
        lse_ref = refs.pop(0) if sink is not None else None
        delta_ref = refs.pop(0)
        b = pl.program_id(0)
        if heads is None:
            delta = jnp.sum(do_ref[0].astype(F32) * o_ref[0].astype(F32), axis=-1, keepdims=True)
        else:
            prod = do_ref[...].astype(F32) * o_ref[...].astype(F32)
            delta = jnp.concatenate([jnp.sum(prod[:, g * HD:(g + 1) * HD], axis=-1, keepdims=True)[None] for g in range(G)])
        if dlse is not None:
            delta = delta - dlse_ref[0]
        delta_ref[0] = delta
        if sink is not None:
            ds_ref = refs.pop(0)

            @pl.when(pl.program_id(1) == 0)
            def _():
                ds_ref[...] = jnp.zeros_like(ds_ref)

            for g in range(G):
                ps = jnp.exp(sink_ref[b * G + g] - lse_ref[0, g]) * delta[g]
                ds_ref[0, g] -= jnp.sum(ps)

    blk = _q_spec(None if heads is None else G, G, bl, HD, lambda b, n: (b, n))
    col = pl.BlockSpec((1, G, bl, 1), lambda b, n: (b, 0, n, 0))
    ins, specs = [do, o], [blk, blk]
    if dlse is not None:
        ins, specs = ins + [dlse], specs + [col]
    out_shape = jax.ShapeDtypeStruct((NB, G, L, 1), F32)
    out_specs = col
    if sink is not None:
        ins, specs = [sink] + ins + [lse], [pl.BlockSpec(memory_space=pltpu.SMEM)] + specs + [col]
        out_shape = (out_shape, jax.ShapeDtypeStruct((NB, G, 1, LANES), F32))
        out_specs = (col, pl.BlockSpec((1, G, 1, LANES), lambda b, n: (b, 0, 0, 0)))
    return _pc(body, name=name, out_shape=out_shape, grid=(NB, L // bl), in_specs=specs, out_specs=out_specs,
               compiler_params=_cp("parallel", "arbitrary"))(*ins)


def _attn_fwd_full(q, k, v, *, bq, tk, heads=None, name="attn_fwd_full"):
    NB, G, L, HD = _q_dims(q, k, heads)
    Lk = k.shape[1]
    nq, nk = L // bq, Lk // tk
    rows = G * bq
    chains = _chain_slices(G, bq)

    def body(q_ref, k_ref, v_ref, o_ref, lse_ref, m_sc, acc_sc, q_sc):
        j = pl.program_id(2)

        @pl.when(j == 0)
        def _():
            m_sc[...] = jnp.full_like(m_sc, NEG)
            acc_sc[...] = jnp.zeros_like(acc_sc)
            for g, hr, sl in chains:
                q_sc[sl] = q_ref[_q_at(heads, g, hr, HD)] * ATTN_SCALE

        kk = k_ref[0]
        vv = _v_ones(v_ref[0])
        st = [dict() for _ in chains]

        def scores(c):
            st[c]["s"] = lax.dot_general(q_sc[chains[c][2]], kk, NT_DIMS, preferred_element_type=F32)

        def softmax(c):
            sl = chains[c][2]
            m_prev = m_sc[sl]
            m_new = jnp.maximum(m_prev, jnp.max(st[c]["s"], axis=1, keepdims=True))
            st[c]["p"] = jnp.exp(st[c].pop("s") - m_new).astype(BF16)
            st[c]["alpha"] = jnp.exp(m_prev - m_new)
            m_sc[sl] = m_new

        def values(c):
            sl = chains[c][2]
            acc_sc[sl] = st[c].pop("alpha") * acc_sc[sl] + jnp.dot(st[c].pop("p"), vv, preferred_element_type=F32)

        _skewed(len(chains), (scores, softmax, values))

        @pl.when(j == nk - 1)
        def _():
            for g, hr, sl in chains:
                acc = acc_sc[sl]
                l = acc[:, HD:HD + 1]
                o_ref[_q_at(heads, g, hr, HD)] = (acc[:, :HD] / l).astype(o_ref.dtype)
                lse_ref[0, g, hr, :] = m_sc[sl] + jnp.log(l)

    qspec = _q_spec(heads, G, bq, HD, lambda b, n, j: (b, n))
    kspec = pl.BlockSpec((1, tk, HD), lambda b, n, j: (b, j, 0))
    return _pc(body, name=name,
               out_shape=(jax.ShapeDtypeStruct(_q_shape(heads, NB, G, L, HD), BF16), jax.ShapeDtypeStruct((NB, G, L, 1), F32)),
               grid=(NB, nq, nk), in_specs=[qspec, kspec, kspec],
               out_specs=(qspec, pl.BlockSpec((1, G, bq, 1), lambda b, n, j: (b, 0, n, 0))),
               scratch_shapes=[pltpu.VMEM((rows, 1), F32), pltpu.VMEM((rows, 2 * HD), F32), pltpu.VMEM((rows, HD), BF16)],
               compiler_params=_cp("parallel", "parallel", "arbitrary"))(q, k, v)


def _attn_bwd_full(q, k, v, do, lse, o, *, bq, tk, heads=None, name="attn_bwd_full"):
    NB, G, L, HD = _q_dims(q, k, heads)
    Lk = k.shape[1]
    nq, nk = L // bq, Lk // tk
    chains = _chain_slices(G, bq)

    def body(q_ref, k_ref, v_ref, do_ref, lse_ref, o_ref, dqp_ref, dk_ref, dv_ref, dk_sc, dv_sc):
        n = pl.program_id(2)

        @pl.when(n == 0)
        def _():
            dk_sc[...] = jnp.zeros_like(dk_sc)
            dv_sc[...] = jnp.zeros_like(dv_sc)

        kk, vv = k_ref[0], v_ref[0]
        st = [dict() for _ in chains]

        def scores(c):
            g, hr, _ = chains[c]
            at = _q_at(heads, g, hr, HD)
            st[c]["q"] = q_ref[at] * ATTN_SCALE
            st[c]["do"] = do_ref[at]
            st[c]["s"] = lax.dot_general(st[c]["q"], kk, NT_DIMS, preferred_element_type=F32)
            st[c]["dp"] = lax.dot_general(st[c]["do"], vv, NT_DIMS, preferred_element_type=F32)
            st[c]["delta"] = jnp.sum(st[c]["do"].astype(F32) * o_ref[at].astype(F32), axis=-1, keepdims=True)

        def softmax(c):
            g, hr, _ = chains[c]
            p = jnp.exp(st[c].pop("s") - lse_ref[0, g, hr, :])
            st[c]["ds"] = (p * (st[c].pop("dp") - st[c].pop("delta"))).astype(BF16)
            st[c]["p"] = p.astype(BF16)

        def grads(c):
            g, hr, _ = chains[c]
            ds = st[c].pop("ds")
            dv_sc[...] += lax.dot_general(st[c].pop("p"), st[c].pop("do"), TN_DIMS, preferred_element_type=F32)
            dk_sc[...] += lax.dot_general(ds, st[c].pop("q"), TN_DIMS, preferred_element_type=F32)
            dqp_ref[(0,) + _q_at(heads, g, hr, HD)] = jnp.dot(ds, kk, preferred_element_type=F32) * ATTN_SCALE

        _skewed(len(chains), (scores, softmax, grads))

        @pl.when(n == nq - 1)
        def _():
            dk_ref[0] = dk_sc[...].astype(dk_ref.dtype)
            dv_ref[0] = dv_sc[...].astype(dv_ref.dtype)

    qspec = _q_spec(heads, G, bq, HD, lambda b, m, n: (b, n))
    cspec = pl.BlockSpec((1, G, bq, 1), lambda b, m, n: (b, 0, n, 0))
    kspec = pl.BlockSpec((1, tk, HD), lambda b, m, n: (b, m, 0))
    kv_shape = jax.ShapeDtypeStruct((NB, Lk, HD), BF16)
    if heads is None:
        pspec = pl.BlockSpec((1, 1, G, bq, HD), lambda b, m, n: (m, b, 0, n, 0))
    else:
        pspec = pl.BlockSpec((1, bq, G * HD), lambda b, m, n: (m, n, b))
    dqp, dk, dv = _pc(body, name=name,
                      out_shape=(jax.ShapeDtypeStruct((nk,) + _q_shape(heads, NB, G, L, HD), F32), kv_shape, kv_shape),
                      grid=(NB, nk, nq), in_specs=[qspec, kspec, kspec, qspec, cspec, qspec],
                      out_specs=(pspec, kspec, kspec),
                      scratch_shapes=[pltpu.VMEM((tk, HD), F32), pltpu.VMEM((tk, HD), F32)],
                      compiler_params=_cp("parallel", "parallel", "arbitrary"))(q, k, v, do, lse, o)
    if nk == 1:
        return dqp[0].astype(BF16), dk, dv
    bl = _row_tile(L, 512)

    def sum_body(p_ref, o_ref):
        acc = p_ref[0]
        for j in range(1, nk):
            acc = acc + p_ref[j]
        o_ref[...] = acc.astype(o_ref.dtype)

    if heads is None:
        pspec = pl.BlockSpec((nk, 1, G, bl, HD), lambda b, n: (0, b, 0, n, 0))
    else:
        pspec = pl.BlockSpec((nk, bl, G * HD), lambda b, n: (0, n, b))
    dq = _pc(sum_body, name=name + "_sum", out_shape=jax.ShapeDtypeStruct(_q_shape(heads, NB, G, L, HD), BF16),
             grid=(NB, L // bl), in_specs=[pspec], out_specs=_q_spec(heads, G, bl, HD, lambda b, n: (b, n)),
             compiler_params=_cp("parallel", "parallel"))(dqp)
    return dq, dk, dv


def _attn_fwd_win(q, k, v, *, radius, sink=None, bq, heads=None, carry=None, name="attn_fwd_win"):
    NB, G, L, HD = _q_dims(q, k, heads)
    W, start = _window(k.shape[1], bq, radius)
    chains = _chain_slices(G, bq)

    def body(*refs):
        if sink is not None:
            sink_ref, *refs = refs
        q_ref, k_ref, v_ref, o_ref, lse_ref = refs
        b, n = pl.program_id(0), pl.program_id(1)
        kk = k_ref[0]
        vv = _v_ones(v_ref[0])
        st = [dict() for _ in chains]

        def scores(c):
            g, hr, _ = chains[c]
            s = lax.dot_general(q_ref[_q_at(heads, g, hr, HD)] * ATTN_SCALE, kk, NT_DIMS, preferred_element_type=F32)
            if radius is not None:
                qpos = n * bq + hr.start + lax.broadcasted_iota(jnp.int32, (hr.stop - hr.start, 1), 0)
                kpos = start(n) + lax.broadcasted_iota(jnp.int32, (1, W), 1)
                s = jnp.where(jnp.abs(qpos - kpos) <= radius, s, NEG)
            st[c]["s"] = s

        def softmax(c):
            g = chains[c][0]
            m = jnp.max(st[c]["s"], axis=1, keepdims=True)
            if sink is not None:
                m = jnp.maximum(m, sink_ref[b * G + g])
            st[c]["p"] = jnp.exp(st[c].pop("s") - m).astype(BF16)
            st[c]["m"] = m

        def values(c):
            g, hr, _ = chains[c]
            acc = jnp.dot(st[c].pop("p"), vv, preferred_element_type=F32)
            m = st[c].pop("m")
            l = acc[:, HD:HD + 1]
            if sink is not None:
                l = l + jnp.exp(sink_ref[b * G + g] - m)
            o_ref[_q_at(heads, g, hr, HD)] = (acc[:, :HD] / l).astype(o_ref.dtype)
            lse_ref[0, g, hr, :] = m + jnp.log(l)

        _skewed(len(chains), (scores, softmax, values))

    qspec = _q_spec(heads, G, bq, HD, lambda b, n: (b, n))
    kspec = _win_specs(G, W, HD, start, False)
    ins, specs = [q, k, v], [qspec, kspec, kspec]
    if sink is not None:
        ins, specs = [sink] + ins, [pl.BlockSpec(memory_space=pltpu.SMEM)] + specs
    return _call(body, carry, ins, name=name,
                 out_shape=(jax.ShapeDtypeStruct(_q_shape(heads, NB, G, L, HD), BF16), jax.ShapeDtypeStruct((NB, G, L, 1), F32)),
                 grid=(NB, L // bq), in_specs=specs,
                 out_specs=(qspec, pl.BlockSpec((1, G, bq, 1), lambda b, n: (b, 0, n, 0))), sem=("parallel", "parallel"))


def _attn_dq_win(q, k, v, do, lse, delta, *, radius, bq, heads=None, name="attn_dq_win"):
    NB, G, L, HD = _q_dims(q, k, heads)
    W, start = _window(L, bq, radius)
    chains = _chain_slices(G, bq)

    def body(q_ref, k_ref, v_ref, do_ref, lse_ref, dl_ref, dq_ref):
        n = pl.program_id(1)
        kk, vv = k_ref[0], v_ref[0]
        kpos = start(n) + lax.broadcasted_iota(jnp.int32, (1, W), 1)
        st = [dict() for _ in chains]

        def scores(c):
            g, hr, _ = chains[c]
            at = _q_at(heads, g, hr, HD)
            st[c]["s"] = lax.dot_general(q_ref[at] * ATTN_SCALE, kk, NT_DIMS, preferred_element_type=F32)
            st[c]["dp"] = lax.dot_general(do_ref[at], vv, NT_DIMS, preferred_element_type=F32)

        def softmax(c):
            g, hr, _ = chains[c]
            qpos = n * bq + hr.start + lax.broadcasted_iota(jnp.int32, (hr.stop - hr.start, 1), 0)
            p = jnp.where(jnp.abs(qpos - kpos) <= radius, jnp.exp(st[c].pop("s") - lse_ref[0, g, hr, :]), 0.0)
            st[c]["ds"] = (p * (st[c].pop("dp") - dl_ref[0, g, hr, :])).astype(BF16)

        def grads(c):
            g, hr, _ = chains[c]
            dq = jnp.dot(st[c].pop("ds"), kk, preferred_element_type=F32) * ATTN_SCALE
            dq_ref[_q_at(heads, g, hr, HD)] = dq.astype(dq_ref.dtype)

        _skewed(len(chains), (scores, softmax, grads))

    qspec = _q_spec(heads, G, bq, HD, lambda b, n: (b, n))
    cspec = pl.BlockSpec((1, G, bq, 1), lambda b, n: (b, 0, n, 0))
    kspec = _win_specs(G, W, HD, start, False)
    return _pc(body, name=name, out_shape=jax.ShapeDtypeStruct(_q_shape(heads, NB, G, L, HD), BF16), grid=(NB, L // bq),
               in_specs=[qspec, kspec, kspec, qspec, cspec, cspec], out_specs=qspec,
               compiler_params=_cp("parallel", "parallel"))(q, k, v, do, lse, delta)


def _attn_dkv_win(q, k, v, do, lse, delta, *, radius, bk, heads=None, carry=None, name="attn_dkv_win"):
    NB, G, L, HD = _q_dims(q, k, heads)
    W, start = _window(L, bk, radius)
    chains = _chain_slices(G, W)

    def body(q_ref, k_ref, v_ref, do_ref, lse_ref, dl_ref, dk_ref, dv_ref):
        m = pl.program_id(1)
        kk, vv = k_ref[0], v_ref[0]
        kpos = m * bk + lax.broadcasted_iota(jnp.int32, (1, bk), 1)
        st = [dict() for _ in chains]
        out = dict(dk=jnp.zeros((bk, HD), F32), dv=jnp.zeros((bk, HD), F32))

        def scores(c):
            g, hr, _ = chains[c]
            at = _q_at(heads, g, hr, HD)
            st[c]["q"] = q_ref[at] * ATTN_SCALE
            st[c]["do"] = do_ref[at]
            st[c]["s"] = lax.dot_general(st[c]["q"], kk, NT_DIMS, preferred_element_type=F32)
            st[c]["dp"] = lax.dot_general(st[c]["do"], vv, NT_DIMS, preferred_element_type=F32)

        def softmax(c):
            g, hr, _ = chains[c]
            qpos = start(m) + hr.start + lax.broadcasted_iota(jnp.int32, (hr.stop - hr.start, 1), 0)
            p = jnp.where(jnp.abs(qpos - kpos) <= radius, jnp.exp(st[c].pop("s") - lse_ref[0, g, hr, :]), 0.0)
            st[c]["ds"] = (p * (st[c].pop("dp") - dl_ref[0, g, hr, :])).astype(BF16)
            st[c]["p"] = p.astype(BF16)

        def grads(c):
            out["dv"] = out["dv"] + lax.dot_general(st[c].pop("p"), st[c].pop("do"), TN_DIMS, preferred_element_type=F32)
            out["dk"] = out["dk"] + lax.dot_general(st[c].pop("ds"), st[c].pop("q"), TN_DIMS, preferred_element_type=F32)

        _skewed(len(chains), (scores, softmax, grads))
        dk_ref[0] = out["dk"].astype(dk_ref.dtype)
        dv_ref[0] = out["dv"].astype(dv_ref.dtype)

    if heads is None:
        qspec = _win_specs(G, W, HD, start, True)
    else:
        qspec = pl.BlockSpec((pl.Element(W), pl.Element(G * HD)), lambda b, m: (start(m), b * G * HD))
    cspec = _win_specs(G, W, 1, start, True)
    kspec = pl.BlockSpec((1, bk, HD), lambda b, m: (b, m, 0))
    kv_shape = jax.ShapeDtypeStruct((NB, L, HD), BF16)
    return _call(body, carry, [q, k, v, do, lse, delta], name=name, out_shape=(kv_shape, kv_shape), grid=(NB, L // bk),
                 in_specs=[qspec, kspec, kspec, qspec, cspec, cspec], out_specs=(kspec, kspec), sem=("parallel", "parallel"))


def _attn(q, k, v, *, radius=None, sink=None, bq, tk=None, heads=None, carry=None, tag):
    if radius is None and tk < k.shape[1]:
        return _attn_fwd_full(q, k, v, bq=bq, tk=tk, heads=heads, name=f"attn_fwd_{tag}")
    return _attn_fwd_win(q, k, v, radius=radius, sink=sink, bq=bq, heads=heads, carry=carry, name=f"attn_fwd_{tag}")


def _attn_bwd(q, k, v, o, lse, do, *, radius=None, sink=None, dlse=None, bq, tk=None, heads=None, carry=None, tag):
    if radius is None:
        assert carry is None
        return (*_attn_bwd_full(q, k, v, do, lse, o, bq=bq, tk=tk, heads=heads, name=f"attn_bwd_{tag}"), None, None)
    nbg = None if heads is None else (k.shape[0], heads)
    if sink is not None:
        delta, ds = _attn_delta(do, o, lse=lse, sink=sink, heads=nbg, name=f"attn_delta_{tag}")
        dsink = ds[:, :, 0, 0].reshape(-1)
    else:
        delta, dsink = _attn_delta(do, o, dlse=dlse, heads=nbg, name=f"attn_delta_{tag}"), None
    dq = _attn_dq_win(q, k, v, do, lse, delta, radius=radius, bq=bq, heads=heads, name=f"attn_dq_{tag}")
    dk, dv, *arrived = _attn_dkv_win(q, k, v, do, lse, delta, radius=radius, bk=bq, heads=heads, carry=carry,
                                     name=f"attn_dkv_{tag}")
    return dq, dk, dv, dsink, (arrived[0] if arrived else None)


def _combine_fwd(o, lse, name="combine_fwd"):
    H, S, HD = o.shape
    tm = _row_tile(S, 512)

    def body(o_ref, lse_ref, t_ref):
        for g in range(GQA_GROUP):
            hs = [kv * GQA_GROUP + g for kv in range(N_KV_HEADS)]
            ls = [lse_ref[h] for h in hs]
            mx = functools.reduce(jnp.maximum, ls)
            es = [jnp.exp(l - mx) for l in ls]
            den = functools.reduce(jnp.add, es)
            for h, e in zip(hs, es):
                t_ref[h] = (o_ref[h].astype(F32) * (e / den)).astype(t_ref.dtype)

    blk = pl.BlockSpec((H, tm, HD), lambda i: (0, i, 0))
    col = pl.BlockSpec((H, tm, 1), lambda i: (0, i, 0))
    return _pc(body, name=name, out_shape=jax.ShapeDtypeStruct((H, S, HD), BF16), grid=(S // tm,),
               in_specs=[blk, col], out_specs=blk, compiler_params=_cp("parallel"))(o, lse)


def _combine_bwd(dt, o, lse, name="combine_bwd"):
    H, S, HD = o.shape
    tm = _row_tile(S, 512)

    def body(dt_ref, o_ref, lse_ref, do_ref, dlse_ref):
        for g in range(GQA_GROUP):
            hs = [kv * GQA_GROUP + g for kv in range(N_KV_HEADS)]
            ls = [lse_ref[h] for h in hs]
            mx = functools.reduce(jnp.maximum, ls)
            es = [jnp.exp(l - mx) for l in ls]
            den = functools.reduce(jnp.add, es)
            al = [e / den for e in es]
            dts = [dt_ref[h].astype(F32) for h in hs]
            da = [jnp.sum(d * o_ref[h].astype(F32), axis=-1, keepdims=True) for h, d in zip(hs, dts)]
            dot = functools.reduce(jnp.add, [a * d for a, d in zip(al, da)])
            for h, a, d, dd in zip(hs, al, da, dts):
                do_ref[h] = (dd * a).astype(do_ref.dtype)
                dlse_ref[h] = a * (d - dot)

    blk = pl.BlockSpec((H, tm, HD), lambda i: (0, i, 0))
    col = pl.BlockSpec((H, tm, 1), lambda i: (0, i, 0))
    return _pc(body, name=name,
               out_shape=(jax.ShapeDtypeStruct((H, S, HD), BF16), jax.ShapeDtypeStruct((H, S, 1), F32)),
               grid=(S // tm,), in_specs=[blk, blk, col], out_specs=(blk, col), compiler_params=_cp("parallel"))(dt, o, lse)


def _position():
    x, y, c = lax.axis_index("x"), lax.axis_index("y"), lax.axis_index("c")
    return x, y, c


def _peer(pos, k):
    x, y, c = pos
    return (1 - x if k & 4 else x, 1 - y if k & 2 else y, 1 - c if k & 1 else c)


def _linear(p):
    return 4 * p[0] + 2 * p[1] + p[2]


def _exchange_steps(s_ref, r_ref, send_sems, recv_sems, local_sem, gather):
    pos = _position()
    me = _linear(pos)
    own = pltpu.make_async_copy(s_ref if gather else s_ref.at[me], r_ref.at[me], local_sem)
    peers = range(1, N_DEV)

    def sems(k):
        return dict(send_sem=send_sems.at[k - 1], recv_sem=recv_sems.at[k - 1], device_id=_peer(pos, k), device_id_type=MESH)

    def send(k):
        src = s_ref if gather else s_ref.at[_linear(_peer(pos, k))]
        return pltpu.make_async_remote_copy(src_ref=src, dst_ref=r_ref.at[me], **sems(k))

    def arrival(k):
        slot = r_ref.at[_linear(_peer(pos, k))]
        return pltpu.make_async_remote_copy(src_ref=slot, dst_ref=slot, **sems(k))

    def start():
        own.start()
        for k in peers:
            send(k).start()

    def wait():
        for k in peers:
            arrival(k).wait_recv()
        for k in peers:
            send(k).wait_send()
        own.wait()

    return start, wait


EXCHANGE_SEMS = [pltpu.SemaphoreType.DMA((N_DEV - 1,)), pltpu.SemaphoreType.DMA((N_DEV - 1,)), pltpu.SemaphoreType.DMA]


def _exchange(buf, gather, name):
    def body(s_ref, r_ref, *sems):
        start, wait = _exchange_steps(s_ref, r_ref, *sems, gather)
        start()
        wait()

    hbm = pl.BlockSpec(memory_space=pltpu.HBM)
    out_shape = ((N_DEV,) + buf.shape) if gather else buf.shape
    return _pc(body, name=name, out_shape=jax.ShapeDtypeStruct(out_shape, buf.dtype), in_specs=[hbm], out_specs=hbm,
               scratch_shapes=list(EXCHANGE_SEMS))(buf)


def _call(body, carry, ins, *, name, out_shape, grid, in_specs, out_specs, scratch_shapes=(), sem):
    if carry is None:
        return _pc(body, name=name, out_shape=tuple(out_shape), grid=grid, in_specs=list(in_specs),
                   out_specs=tuple(out_specs), scratch_shapes=list(scratch_shapes), compiler_params=_cp(*sem))(*ins)
    buf, gather = carry
    n_in, n_out, n_sc = len(ins), len(out_shape), len(scratch_shapes)

    def wrapped(*refs):
        in_refs, buf_ref = refs[:n_in], refs[n_in]
        out_refs, recv_ref = refs[n_in + 1:n_in + 1 + n_out], refs[n_in + 1 + n_out]
        rest = refs[n_in + 2 + n_out:]
        first = functools.reduce(jnp.logical_and, [pl.program_id(a) == 0 for a in range(len(grid))])
        last = functools.reduce(jnp.logical_and, [pl.program_id(a) == grid[a] - 1 for a in range(len(grid))])

        @pl.when(first)
        def _():
            _exchange_steps(buf_ref, recv_ref, *rest[n_sc:], gather)[0]()

        body(*in_refs, *out_refs, *rest[:n_sc])

        @pl.when(last)
        def _():
            _exchange_steps(buf_ref, recv_ref, *rest[n_sc:], gather)[1]()

    hbm = pl.BlockSpec(memory_space=pltpu.HBM)
    recv_shape = ((N_DEV,) + buf.shape) if gather else buf.shape
    return _pc(wrapped, name=name, out_shape=(*out_shape, jax.ShapeDtypeStruct(recv_shape, buf.dtype)), grid=grid,
               in_specs=[*in_specs, hbm], out_specs=(*out_specs, hbm), scratch_shapes=[*scratch_shapes, *EXCHANGE_SEMS],
               compiler_params=_cp(*(("arbitrary",) * len(grid))))(*ins, buf)


def _reduce_adamw(recv, w, m, v, name):
    _, R, C = recv.shape
    tr = _row_tile(R, 512)

    def body(r_ref, w_ref, m_ref, v_ref, g_ref, d_ref, nm_ref, nv_ref):
        g = r_ref[0].astype(F32)
        for j in range(1, N_DEV):
            g = g + r_ref[j].astype(F32)
        g_ref[...] = g
        nm = ADAM_B1 * m_ref[...] + (1.0 - ADAM_B1) * g
        nv = ADAM_B2 * v_ref[...] + (1.0 - ADAM_B2) * jnp.square(g)
        m_hat = nm / (1.0 - ADAM_B1 ** ADAM_STEP)
        v_hat = nv / (1.0 - ADAM_B2 ** ADAM_STEP)
        d_ref[...] = -ADAM_LR * (m_hat / (jnp.sqrt(v_hat) + ADAM_EPS) + ADAM_WD * w_ref[...])
        nm_ref[...] = nm
        nv_ref[...] = nv

    row = pl.BlockSpec((tr, C), lambda i: (i, 0))
    out = jax.ShapeDtypeStruct((R, C), F32)
    return _pc(body, name=name, out_shape=(out, out, out, out), grid=(R // tr,),
               in_specs=[pl.BlockSpec((N_DEV, tr, C), lambda i: (0, i, 0)), row, row, row],
               out_specs=(row, row, row, row), compiler_params=_cp("parallel"))(recv, w, m, v)


BIG = (("w_in", 2), ("w_mem_kv", 1), ("w_o", 1), ("w_gate_up", 2), ("w_down", 1))
SMALL = ("mem_norm_g", "g_mix_pre", "g_mix_post", "attn_sink", "qk_norm_g", "g_ffn_pre", "g_ffn_post")
SMALL_W = 1024
FIRST, REST = BIG[:1], BIG[1:]


def _pack_local(shards, dtype):
    return jnp.concatenate([s.astype(dtype).reshape(-1, LANES) for s in shards], axis=0)


def _unpack_local(flat, shapes):
    out, r = [], 0
    for shp in shapes:
        n = shp[0] * shp[1] * shp[2] // LANES
        out.append(flat[r:r + n].reshape(shp))
        r += n
    return out


def _unpack_gathered(g, shapes, names=BIG):
    out, r = [], 0
    for (name, dim), shp in zip(names, shapes):
        n = shp[0] * shp[1] * shp[2] // LANES
        t = g[:, r:r + n].reshape((N_DEV,) + tuple(shp))
        if dim == 2:
            t = t.transpose(1, 2, 0, 3).reshape(shp[0], shp[1], N_DEV * shp[2])
        else:
            t = t.transpose(1, 0, 2, 3).reshape(shp[0], N_DEV * shp[1], shp[2])
        out.append(t)
        r += n
    return out


def _pack_for_scatter(full, shapes, dtype, names=BIG):
    parts = []
    for (name, dim), shp, t in zip(names, shapes, full):
        if dim == 2:
            t = t.reshape(shp[0], shp[1], N_DEV, shp[2]).transpose(2, 0, 1, 3)
        else:
            t = t.reshape(shp[0], N_DEV, shp[1], shp[2]).transpose(1, 0, 2, 3)
        parts.append(t.astype(dtype).reshape(N_DEV, -1, LANES))
    return jnp.concatenate(parts, axis=1)


def _pack_small(arrs):
    flat = jnp.concatenate([a.reshape(-1) for a in arrs])
    pad = (-flat.shape[0]) % (8 * SMALL_W)
    return jnp.pad(flat, (0, pad)).reshape(-1, SMALL_W)


def _unpack_small(flat, shapes):
    flat = flat.reshape(-1)
    out, r = [], 0
    for shp in shapes:
        n = 1
        for d in shp:
            n *= d
        out.append(flat[r:r + n].reshape(shp))
        r += n
    return out


def _heads(t, nb, g):
    S = t.shape[0]
    return t.reshape(S, nb, g, HEAD_DIM).transpose(1, 2, 0, 3)


def _unheads(t):
    nb, g, S, hd = t.shape
    return t.transpose(2, 0, 1, 3).reshape(S, nb * g * hd)


def _dilate(t, dil):
    S = t.shape[0]
    g = t.shape[1] // HEAD_DIM
    return t.reshape(S // dil, dil, g, HEAD_DIM).transpose(1, 2, 0, 3)


def _undilate(t):
    dil, g, L, w = t.shape
    return t.transpose(1, 2, 0, 3).reshape(g, L * dil, w)


FULL_BQ_FWD, FULL_TK_FWD = 512, 4096
FULL_BQ_BWD, FULL_TK_BWD = 1024, 2048


def _mixer_fwd(kind, pr, kv, sink, li, carry=None):
    S = pr.shape[0]
    if kind == 0:
        tok, lse, *arrived = _attn(pr, *kv, radius=A_RADIUS, sink=sink, bq=min(256, S), heads=GQA_GROUP, carry=carry,
                                   tag=f"a{li}")
        return tok, lse, (arrived[0] if arrived else None)
    assert carry is None
    if kind == 1:
        tok, lse = _attn(pr, *kv, bq=min(FULL_BQ_FWD, S), tk=min(FULL_TK_FWD, S), heads=GQA_GROUP, tag=f"b{li}")
        return tok, lse, None
    saved, outs, lses = [], [], []
    for g, (window, dil) in enumerate(C_GROUPS):
        q = _dilate(pr[:, g * GQA_GROUP * HEAD_DIM:(g + 1) * GQA_GROUP * HEAD_DIM], dil)
        k = _dilate(pr[:, Q_W + g * HEAD_DIM:Q_W + (g + 1) * HEAD_DIM], dil)[:, 0]
        v = _dilate(pr[:, QK_W + g * HEAD_DIM:QK_W + (g + 1) * HEAD_DIM], dil)[:, 0]
        o, lse = _attn(q, k, v, radius=window // (2 * dil), bq=min(256, S // dil), tag=f"c{li}g{g}")
        saved.append((q, k, v, o, lse))
        outs.append(_undilate(o))
        lses.append(_undilate(lse))
    o_all, lse_all = jnp.concatenate(outs, 0), jnp.concatenate(lses, 0)
    tok = _combine_fwd(o_all, lse_all, name=f"combine_fwd_{li}")
    return tok.transpose(1, 0, 2).reshape(S, Q_W), (saved, o_all, lse_all), None


def _mixer_bwd(kind, dcat, pr, kv, cat, saved, sink, li, carry=None):
    S = dcat.shape[0]
    if kind == 0:
        return _attn_bwd(pr, *kv, cat, saved, dcat, radius=A_RADIUS, sink=sink, bq=min(256, S), heads=GQA_GROUP, carry=carry,
                         tag=f"a{li}")
    assert carry is None
    if kind == 1:
        return _attn_bwd(pr, *kv, cat, saved, dcat, bq=min(FULL_BQ_BWD, S), tk=min(FULL_TK_BWD, S), heads=GQA_GROUP,
                         tag=f"b{li}")
    per_group, o_all, lse_all = saved
    dt = dcat[:, :Q_W].reshape(S, N_TOK_HEADS, HEAD_DIM).transpose(1, 0, 2)
    do_all, dlse_all = _combine_bwd(dt, o_all, lse_all, name=f"combine_bwd_{li}")
    dqs, dks, dvs = [], [], []
    for g, (window, dil) in enumerate(C_GROUPS):
        q, k, v, o, lse = per_group[g]
        L = S // dil
        hs = slice(g * GQA_GROUP, (g + 1) * GQA_GROUP)
        do = do_all[hs].reshape(GQA_GROUP, L, dil, HEAD_DIM).transpose(2, 0, 1, 3)
        dlse = dlse_all[hs].reshape(GQA_GROUP, L, dil, 1).transpose(2, 0, 1, 3)
        dq, dk, dv, _, _ = _attn_bwd(q, k, v, o, lse, do, radius=window // (2 * dil), dlse=dlse, bq=min(256, L),
                                     tag=f"c{li}g{g}")
        dqs.append(dq.transpose(2, 0, 1, 3).reshape(S, GQA_GROUP * HEAD_DIM))
        dks.append(dk.transpose(1, 0, 2).reshape(S, HEAD_DIM))
        dvs.append(dv.transpose(1, 0, 2).reshape(S, HEAD_DIM))
    return jnp.concatenate(dqs, 1), jnp.concatenate(dks, 1), jnp.concatenate(dvs, 1), None, None


def kernel(x, mem, mem_norm_g, w_in, w_mem_kv, w_o, g_mix_pre, g_mix_post, attn_sink, qk_norm_g, w_gate_up, w_down, g_ffn_pre, g_ffn_post, loss_target, m_mem_norm_g, m_w_in, m_w_mem_kv, m_w_o, m_g_mix_pre, m_g_mix_post, m_attn_sink, m_qk_norm_g, m_w_gate_up, m_w_down, m_g_ffn_pre, m_g_ffn_post, v_mem_norm_g, v_w_in, v_w_mem_kv, v_w_o, v_g_mix_pre, v_g_mix_post, v_attn_sink, v_qk_norm_g, v_w_gate_up, v_w_down, v_g_ffn_pre, v_g_ffn_post):
    given = dict(locals())
    depth = w_in.shape[0]
    S, D = x.shape[1], x.shape[2]
    def shapes_of(names):
        return [(1,) + tuple(given[n].shape[1:]) for n, _ in names]

    def layer_pack(pre, l, dtype, names=BIG):
        return _pack_local([given[pre + n][l:l + 1] for n, _ in names], dtype)

    def layer_weights(gathered, names=BIG):
        return [t[0] for t in _unpack_gathered(gathered, shapes_of(names), names)]

    W = [None] * depth
    W[0] = layer_weights(_exchange(layer_pack("", 0, BF16, FIRST), True, "gather_w0_in"), FIRST)

    tabs = _rope_tables(S)
    mem_n = _rms_fwd(mem[0], mem_norm_g[None], BF16, name="rms_mem")

    saved = []
    xc = x[0]
    for i in range(depth):
        kind = i % N_MIXERS
        (tab, shift) = tabs[1] if kind == 1 else tabs[0]
        sink = attn_sink[i // N_MIXERS] if kind == 0 else None
        qk_gain = _qk_gain_row(qk_norm_g[i // N_MIXERS]) if kind == 1 else None
        carry = (layer_pack("", 0, BF16, REST), True) if i == 0 else None
        h, proj, *arrived = _mm(xc, W[i][0], F32, pre_g=g_mix_pre[i][None], carry=carry, name=f"mm_in_{i}")
        if carry is not None:
            W[0] = W[0] + layer_weights(arrived[0], REST)
        W_in, W_mkv, W_o, W_gu, W_dn = W[i]
        if kind == 2:
            pr, kv = _headprep_fwd(proj, tab, shift, qk_gain, name=f"headprep_fwd_{i}"), None
        else:
            pr, *kv = _headprep_fwd(proj, tab, shift, qk_gain, kv_heads=True, name=f"headprep_fwd_{i}")
        carry = (layer_pack("", 1, BF16), True) if i == 0 and depth > 1 else None
        tok, mix_saved, arrived = _mixer_fwd(kind, pr, kv, sink, i, carry)
        if carry is not None:
            W[1] = layer_weights(arrived)
        (mkv,) = _mm(mem_n, W_mkv, BF16, name=f"mm_mkv_{i}")
        qm = _heads(pr[:, QK_W + KV_W:], N_MEM_HEADS, 1)
        km = _heads(mkv[:, :QM_W], N_MEM_HEADS, 1)[:, 0]
        vm = _heads(mkv[:, QM_W:], N_MEM_HEADS, 1)[:, 0]
        mo, mlse = _attn(qm, km, vm, bq=min(1024, S), tk=km.shape[1], tag=f"m{i}")
        cat = jnp.concatenate([tok, _unheads(mo)], axis=1)
        o, x1 = _mm(cat, W_o, F32, post=(g_mix_post[i][None], xc), name=f"mm_o_{i}")
        carry = (layer_pack("", i + 2, BF16), True) if i + 2 < depth else None
        h2, gu, act, *arrived = _mm(x1, W_gu, BF16, pre_g=g_ffn_pre[i][None], swiglu=True, carry=carry, name=f"mm_gu_{i}")
        if carry is not None:
            W[i + 2] = layer_weights(arrived[0])
        f, x2 = _mm(act, W_dn, F32, post=(g_ffn_post[i][None], x1), name=f"mm_dn_{i}")
        saved.append(dict(x=xc, h=h, proj=proj, pr=pr, kv=kv, mix=mix_saved, qm=qm, km=km, vm=vm, mo=mo, mlse=mlse, cat=cat, o=o,
                          x1=x1, h2=h2, gu=gu, act=act, f=f))
        xc = x2

    dy, sq = _loss_head(xc, loss_target[0], name="loss_head")
    loss = lax.psum(sq[0, 0] * (0.5 / D), ("x", "y", "c"))

    grads = {n: [None] * depth for n in ("w_in", "w_mem_kv", "w_o", "w_gate_up", "w_down", "g_mix_pre", "g_mix_post",
                                         "g_ffn_pre", "g_ffn_post")}
    d_sink = [jnp.zeros((N_TOK_HEADS,), F32) for _ in range(attn_sink.shape[0])]
    d_qkg = [jnp.zeros((2, HEAD_DIM), F32) for _ in range(qk_norm_g.shape[0])]
    dmem_n = jnp.zeros((mem.shape[1], D), F32)
    recv = [None] * depth

    def scatter_pack(l, names=BIG):
        return _pack_for_scatter([grads[n][l][None] for n, _ in names], shapes_of(names), BF16, names)

    dx = dy
    for i in reversed(range(depth)):
        kind = i % N_MIXERS
        sv = saved[i]
        (tab, shift) = tabs[1] if kind == 1 else tabs[0]
        sink = attn_sink[i // N_MIXERS] if kind == 0 else None
        W_in, W_mkv, W_o, W_gu, W_dn = W[i]
        df, dgu, dg = _mm(dx, W_dn, BF16, nt=True, gu=sv["gu"], pre_bwd=(sv["f"], g_ffn_post[i][None]), name=f"mmb_dn_{i}")
        grads["g_ffn_post"][i] = dg[0]
        grads["w_down"][i] = _mm_tn(sv["act"], df, name=f"mmw_dn_{i}")
        dx1, dg = _mm(dgu, W_gu, F32, nt=True, post_bwd=(sv["x1"], g_ffn_pre[i][None], dx), name=f"mmb_gu_{i}")
        grads["g_ffn_pre"][i] = dg[0]
        if i + 1 < depth:
            grads["w_gate_up"][i], recv[i + 1] = _mm_tn(sv["h2"], dgu, carry=(scatter_pack(i + 1), False), name=f"mmw_gu_{i}")
        else:
            grads["w_gate_up"][i] = _mm_tn(sv["h2"], dgu, name=f"mmw_gu_{i}")
        do, dcat, dg = _mm(dx1, W_o, BF16, nt=True, pre_bwd=(sv["o"], g_mix_post[i][None]), name=f"mmb_o_{i}")
        grads["g_mix_post"][i] = dg[0]
        grads["w_o"][i] = _mm_tn(sv["cat"], do, name=f"mmw_o_{i}")
        dmo = _heads(dcat[:, Q_W:], N_MEM_HEADS, 1)
        dqm, dkm, dvm, _, _ = _attn_bwd(sv["qm"], sv["km"], sv["vm"], sv["mo"], sv["mlse"], dmo, bq=min(1024, S),
                                        tk=sv["km"].shape[1], tag=f"m{i}")
        dmkv = jnp.concatenate([_unheads(dkm[:, None]), _unheads(dvm[:, None])], axis=1).astype(BF16)
        grads["w_mem_kv"][i] = _mm_tn(mem_n, dmkv, name=f"mmw_mkv_{i}")
        dmem_n = dmem_n + _mm(dmkv, W_mkv, F32, nt=True, name=f"mmb_mkv_{i}")[0]
        carry = (scatter_pack(0, REST), False) if i == 0 else None
        dq, dk, dv, dsink, recv0_rest = _mixer_bwd(kind, dcat, sv["pr"], sv["kv"], sv["cat"], sv["mix"], sink, i, carry)
        if dsink is not None:
            d_sink[i // N_MIXERS] = dsink
        dpr = jnp.concatenate([dq, dk, dv, _unheads(dqm)], axis=1) if kind == 2 else (dq, dk, dv, _unheads(dqm))
        if kind == 1:
            dproj, dgc = _headprep_bwd(dpr, tab, shift, sv["proj"], _qk_gain_row(qk_norm_g[i // N_MIXERS]),
                                       name=f"headprep_bwd_{i}")
            d_qkg[i // N_MIXERS] = jnp.stack([dgc[0, :Q_W].reshape(N_TOK_HEADS, HEAD_DIM).sum(0),
                                              dgc[0, Q_W:QK_W].reshape(N_KV_HEADS, HEAD_DIM).sum(0)])
        else:
            dproj = _headprep_bwd(dpr, tab, shift, name=f"headprep_bwd_{i}")
        dx, dg = _mm(dproj, W_in, F32, nt=True, post_bwd=(sv["x"], g_mix_pre[i][None], dx1), name=f"mmb_in_{i}")
        grads["g_mix_pre"][i] = dg[0]
        grads["w_in"][i] = _mm_tn(sv["h"], dproj, name=f"mmw_in_{i}")
    _, dg_mem = _rms_bwd(mem[0], mem_norm_g[None], dmem_n, BF16, name="rmsb_mem")

    def update(received, l, names, tag):
        res = _reduce_adamw(received, *[layer_pack(pre, l, F32, names) for pre in ("", "m_", "v_")], name=f"adamw_{tag}")
        return [_unpack_local(r, shapes_of(names)) for r in res]

    recv0_first = _exchange(scatter_pack(0, FIRST), False, "scatter_g0_in")
    per_layer = [[a + b for a, b in zip(update(recv0_first, 0, FIRST, "0_in"), update(recv0_rest, 0, REST, "0"))]]
    per_layer += [update(recv[l], l, BIG, str(l)) for l in range(1, depth)]
    big_out = [[jnp.concatenate(ts, axis=0) for ts in zip(*[per_layer[l][j] for l in range(depth)])] for j in range(4)]

    small_grads = dict(mem_norm_g=dg_mem[0], g_mix_pre=jnp.stack(grads["g_mix_pre"]), g_mix_post=jnp.stack(grads["g_mix_post"]),
                       attn_sink=jnp.stack(d_sink), qk_norm_g=jnp.stack(d_qkg), g_ffn_pre=jnp.stack(grads["g_ffn_pre"]),
                       g_ffn_post=jnp.stack(grads["g_ffn_post"]))
    sg = _pack_small([small_grads[n] for n in SMALL])
    srecv = _exchange(sg, True, "gather_small_grads")
    spacked = lambda pre: _pack_small([given[pre + n] for n in SMALL])
    gs, ds, ms, vs = _reduce_adamw(srecv, spacked(""), spacked("m_"), spacked("v_"), name="adamw_small")

    out = {}
    for pre, fb, fs in zip(("grad_", "delta_", "new_m_", "new_v_"), big_out, (gs, ds, ms, vs)):
        for (n, _), t in zip(BIG, fb):
            out[pre + n] = t
        for n, t in zip(SMALL, _unpack_small(fs, [given[n].shape for n in SMALL])):
            out[pre + n] = t
    order = ("mem_norm_g", "w_in", "w_mem_kv", "w_o", "g_mix_pre", "g_mix_post", "attn_sink", "qk_norm_g", "w_gate_up",
             "w_down", "g_ffn_pre", "g_ffn_post")
    return (loss, dx[None], *[out[p + n] for p in ("grad_", "delta_", "new_m_", "new_v_") for n in order])
```

```python
import functools

import jax
import jax.numpy as jnp
from jax import lax
from jax.experimental import pallas as pl
from jax.experimental.pallas import tpu as pltpu

F32 = jnp.float32
BF16 = jnp.bfloat16

HEAD_DIM = 64
N_TOK_HEADS = 12
N_KV_HEADS = 3
GQA_GROUP = 4
N_MEM_HEADS = 4
Q_W = N_TOK_HEADS * HEAD_DIM
KV_W = N_KV_HEADS * HEAD_DIM
QM_W = N_MEM_HEADS * HEAD_DIM
QK_W = Q_W + KV_W
IN_W = Q_W + 2 * KV_W + QM_W
N_MIXERS = 3
A_RADIUS = 128
C_GROUPS = ((128, 1), (512, 4), (2048, 16))
ROPE_THETA = 500000.0
ROPE_DIMS = HEAD_DIM // 4
AXIAL_THETA = 10000.0
GRID_W = 64
EPS = 1e-6
ATTN_SCALE = HEAD_DIM ** -0.5
NEG = -1e30

ADAM_LR = 0.001
ADAM_B1 = 0.9
ADAM_B2 = 0.999
ADAM_EPS = 1e-08
ADAM_WD = 0.01
ADAM_STEP = 10

N_DEV = 8
LANES = 128
VMEM_LIMIT = 56 * 1024 * 1024
MESH = pl.DeviceIdType.MESH
NT_DIMS = (((1,), (1,)), ((), ()))
TN_DIMS = (((0,), (0,)), ((), ()))


def _pc(body, **kw):
    return pl.pallas_call(body, **kw)


def _cp(*sem):
    return pltpu.CompilerParams(dimension_semantics=sem, vmem_limit_bytes=VMEM_LIMIT)


def _row_tile(m, cap=512):
    t = cap
    while m % t:
        t //= 2
    return t


def _rms_fwd(x, g, out_dtype, name="rms_fwd"):
    M, D = x.shape
    tm = _row_tile(M)

    def body(x_ref, g_ref, o_ref):
        xv = x_ref[...]
        y = xv * lax.rsqrt(jnp.mean(xv * xv, axis=-1, keepdims=True) + EPS) * g_ref[...]
        o_ref[...] = y.astype(o_ref.dtype)

    row = pl.BlockSpec((tm, D), lambda i: (i, 0))
    vec = pl.BlockSpec((1, D), lambda i: (0, 0))
    return _pc(body, name=name, out_shape=jax.ShapeDtypeStruct((M, D), out_dtype), grid=(M // tm,),
               in_specs=[row, vec], out_specs=row, compiler_params=_cp("parallel"))(x, g)


def _rms_bwd_tile(xv, g, d):
    r = lax.rsqrt(jnp.mean(xv * xv, axis=-1, keepdims=True) + EPS)
    xh = xv * r
    dxh = d * g
    return r * (dxh - xh * jnp.mean(dxh * xh, axis=-1, keepdims=True)), jnp.sum(d * xh, axis=0, keepdims=True)


def _rms_bwd(x, g, dy, out_dtype, name="rms_bwd"):
    M, D = x.shape
    tm = _row_tile(M)

    def body(x_ref, g_ref, dy_ref, dx_ref, dg_ref):
        dx, dg = _rms_bwd_tile(x_ref[...], g_ref[...], dy_ref[...].astype(F32))
        dx_ref[...] = dx.astype(dx_ref.dtype)

        @pl.when(pl.program_id(0) == 0)
        def _():
            dg_ref[...] = jnp.zeros_like(dg_ref)

        dg_ref[...] += dg

    row = pl.BlockSpec((tm, D), lambda i: (i, 0))
    vec = pl.BlockSpec((1, D), lambda i: (0, 0))
    return _pc(body, name=name,
               out_shape=(jax.ShapeDtypeStruct((M, D), out_dtype), jax.ShapeDtypeStruct((1, D), F32)),
               grid=(M // tm,), in_specs=[row, vec, row], out_specs=(row, vec), compiler_params=_cp("arbitrary"))(x, g, dy)


def _loss_head(y, t, name="loss_head"):
    M, D = y.shape
    tm = _row_tile(M)

    def body(y_ref, t_ref, dy_ref, acc_ref):
        e = y_ref[...] - t_ref[...]
        dy_ref[...] = e * (1.0 / D)

        @pl.when(pl.program_id(0) == 0)
        def _():
            acc_ref[...] = jnp.zeros_like(acc_ref)

        acc_ref[...] += jnp.sum(e * e)

    row = pl.BlockSpec((tm, D), lambda i: (i, 0))
    return _pc(body, name=name,
               out_shape=(jax.ShapeDtypeStruct((M, D), F32), jax.ShapeDtypeStruct((8, LANES), F32)),
               grid=(M // tm,), in_specs=[row, row],
               out_specs=(row, pl.BlockSpec((8, LANES), lambda i: (0, 0))), compiler_params=_cp("arbitrary"))(y, t)


def _mm(a, w, out_dtype, nt=False, pre_g=None, swiglu=False, post=None, gu=None, pre_bwd=None, post_bwd=None, carry=None,
        tm=512, name="mm"):
    M, K = a.shape
    N = w.shape[0] if nt else w.shape[1]
    tm = _row_tile(M, tm)
    gain_grad = pre_bwd is not None or post_bwd is not None

    def body(*refs):
        refs = list(refs)
        a_ref, w_ref = refs.pop(0), refs.pop(0)
        pg_ref = refs.pop(0) if pre_g is not None else None
        g_ref, r_ref = (refs.pop(0), refs.pop(0)) if post is not None else (None, None)
        gu_ref = refs.pop(0) if gu is not None else None
        bwd_refs = [refs.pop(0) for _ in (pre_bwd or post_bwd or ())]
        lhs = a_ref[...]
        if pre_g is not None:
            lhs = (lhs * lax.rsqrt(jnp.mean(lhs * lhs, axis=-1, keepdims=True) + EPS) * pg_ref[...]).astype(BF16)
            refs.pop(0)[...] = lhs
        if pre_bwd is not None:
            lhs, dg = _rms_bwd_tile(bwd_refs[0][...], bwd_refs[1][...], lhs.astype(F32))
            lhs = lhs.astype(BF16)
            refs.pop(0)[...] = lhs
        if nt:
            acc = lax.dot_general(lhs, w_ref[...], NT_DIMS, preferred_element_type=F32)
        else:
            acc = jnp.dot(lhs, w_ref[...], preferred_element_type=F32)
        o_ref = refs.pop(0)
        if post_bwd is not None:
            dx, dg = _rms_bwd_tile(bwd_refs[0][...], bwd_refs[1][...], acc)
            o_ref[...] = bwd_refs[2][...] + dx
        elif gu is None:
            o_ref[...] = acc.astype(o_ref.dtype)
        else:
            gate = gu_ref[:, :N].astype(F32)
            sig = 1.0 / (1.0 + jnp.exp(-gate))
            o_ref[:, :N] = (acc * gu_ref[:, N:].astype(F32) * (sig * (1.0 + gate * (1.0 - sig)))).astype(o_ref.dtype)
            o_ref[:, N:] = (acc * (gate * sig)).astype(o_ref.dtype)
        if swiglu:
            gate = acc[:, : N // 2]
            refs.pop(0)[...] = (gate * (1.0 / (1.0 + jnp.exp(-gate))) * acc[:, N // 2:]).astype(BF16)
        if post is not None:
            y = acc * lax.rsqrt(jnp.mean(acc * acc, axis=-1, keepdims=True) + EPS) * g_ref[...]
            refs.pop(0)[...] = r_ref[...] + y
        if gain_grad:
            dg_ref = refs.pop(0)

            @pl.when(pl.program_id(0) == 0)
            def _():
                dg_ref[...] = jnp.zeros_like(dg_ref)

            dg_ref[...] += dg

    row = lambda n: pl.BlockSpec((tm, n), lambda i: (i, 0))
    vec = lambda n: pl.BlockSpec((1, n), lambda i: (0, 0))
    ins = [a, w]
    specs = [row(K), pl.BlockSpec(w.shape, lambda i: (0, 0), pipeline_mode=pl.Buffered(1))]
    outs, ospecs = [], []
    if pre_g is not None:
        ins, specs = ins + [pre_g], specs + [vec(K)]
    if pre_g is not None or pre_bwd is not None:
        outs, ospecs = outs + [jax.ShapeDtypeStruct((M, K), BF16)], ospecs + [row(K)]
    if post is not None:
        ins, specs = ins + list(post), specs + [vec(N), row(N)]
    if gu is not None:
        ins, specs = ins + [gu], specs + [row(2 * N)]
        outs, ospecs = outs + [jax.ShapeDtypeStruct((M, 2 * N), BF16)], ospecs + [row(2 * N)]
    else:
        outs, ospecs = outs + [jax.ShapeDtypeStruct((M, N), out_dtype)], ospecs + [row(N)]
    if pre_bwd is not None:
        ins, specs = ins + list(pre_bwd), specs + [row(K), vec(K)]
    if post_bwd is not None:
        ins, specs = ins + list(post_bwd), specs + [row(N), vec(N), row(N)]
    if swiglu:
        outs, ospecs = outs + [jax.ShapeDtypeStruct((M, N // 2), BF16)], ospecs + [row(N // 2)]
    if post is not None:
        outs, ospecs = outs + [jax.ShapeDtypeStruct((M, N), F32)], ospecs + [row(N)]
    if gain_grad:
        D = K if pre_bwd is not None else N
        outs, ospecs = outs + [jax.ShapeDtypeStruct((1, D), F32)], ospecs + [vec(D)]
    return _call(body, carry, ins, name=name, out_shape=outs, grid=(M // tm,), in_specs=specs, out_specs=ospecs,
                 sem=("arbitrary" if gain_grad else "parallel",))


def _mm_tn(a, b, carry=None, name="mm_tn"):
    S, M = a.shape
    N = b.shape[1]
    tm = M if M <= 1408 else M // 2
    tn = N if N <= 1408 else N // 4
    ts = _row_tile(S, 1024)

    def body(a_ref, b_ref, o_ref):
        @pl.when(pl.program_id(2) == 0)
        def _():
            o_ref[...] = jnp.zeros_like(o_ref)

        o_ref[...] += lax.dot_general(a_ref[...], b_ref[...], TN_DIMS, preferred_element_type=F32)

    res = _call(body, carry, [a, b], name=name, out_shape=[jax.ShapeDtypeStruct((M, N), F32)],
                grid=(M // tm, N // tn, S // ts),
                in_specs=[pl.BlockSpec((ts, tm), lambda i, j, s: (s, i)), pl.BlockSpec((ts, tn), lambda i, j, s: (s, j))],
                out_specs=[pl.BlockSpec((tm, tn), lambda i, j, s: (i, j))], sem=("parallel", "parallel", "arbitrary"))
    return res[0] if carry is None else res


def _rope_tables(S):
    pos = jnp.arange(S, dtype=jnp.int32)

    def table(p, n_dims, theta):
        inv = theta ** (-(jnp.arange(0, n_dims, 2, dtype=F32) / n_dims))
        ang = p.astype(F32)[:, None] * inv[None, :]
        return jnp.cos(ang), jnp.sin(ang)

    one = lambda n: jnp.ones((S, n), F32)
    zero = lambda n: jnp.zeros((S, n), F32)
    cp, sp = table(pos, ROPE_DIMS, ROPE_THETA)
    rest = HEAD_DIM - ROPE_DIMS
    part = (jnp.concatenate([cp, cp, one(rest)], 1), jnp.concatenate([zero(8), sp, zero(rest)], 1),
            jnp.concatenate([-sp, zero(8), zero(rest)], 1))
    cr, sr = table(pos // GRID_W, HEAD_DIM // 2, AXIAL_THETA)
    cc, sc = table(pos % GRID_W, HEAD_DIM // 2, AXIAL_THETA)
    axial = (jnp.concatenate([cr, cr, cc, cc], 1), jnp.concatenate([zero(16), sr, zero(16), sc], 1),
             jnp.concatenate([-sr, zero(16), -sc, zero(16)], 1))
    rep = LANES // HEAD_DIM
    return (tuple(jnp.tile(t, (1, rep)) for t in part), ROPE_DIMS // 2), (tuple(jnp.tile(t, (1, rep)) for t in axial), HEAD_DIM // 4)


def _seg_mats():
    col = jnp.arange(IN_W)[:, None] // HEAD_DIM
    e = (col == jnp.arange(LANES)[None, :]).astype(BF16)
    return e, e.T


def _qk_gain_row(qk_g):
    return jnp.concatenate([jnp.tile(qk_g[0], N_TOK_HEADS), jnp.tile(qk_g[1], N_KV_HEADS),
                            jnp.ones((IN_W - QK_W,), F32)])[None, :]


def _rope_cols(tabs, tm):
    col = lax.broadcasted_iota(jnp.int32, (tm, IN_W), 1)
    qk = col < QK_W
    c, s_lo, s_hi = (jnp.tile(t[...], (1, IN_W // LANES)) for t in tabs)
    return jnp.where(qk, c, 1.0), jnp.where(qk, s_lo, 0.0), jnp.where(qk, s_hi, 0.0), qk


def _seg_mean(v, e_ref, et_ref):
    def split_dot(t, m_ref):
        hi = t.astype(BF16)
        lo = (t - hi.astype(F32)).astype(BF16)
        return jnp.dot(hi, m_ref[...], preferred_element_type=F32) + jnp.dot(lo, m_ref[...], preferred_element_type=F32)

    return split_dot(split_dot(v, e_ref) * (1.0 / HEAD_DIM), et_ref)


def _headprep_fwd(proj, tabs, shift, qk_gain=None, kv_heads=False, name="headprep_fwd"):
    S = proj.shape[0]
    tm = _row_tile(S, 256)
    norm = qk_gain is not None

    def body(*refs):
        refs = list(refs)
        p_ref, c_ref, lo_ref, hi_ref = (refs.pop(0) for _ in range(4))
        g_ref, e_ref, et_ref = (refs.pop(0) for _ in range(3)) if norm else (None, None, None)
        o_ref = refs.pop(0)
        x = p_ref[...]
        c, s_lo, s_hi, qk = _rope_cols((c_ref, lo_ref, hi_ref), tm)
        if norm:
            r = lax.rsqrt(_seg_mean(x * x, e_ref, et_ref) + EPS)
            x = x * jnp.where(qk, r, 1.0) * g_ref[...]
        y = (x * c + pltpu.roll(x, shift, 1) * s_lo + pltpu.roll(x, IN_W - shift, 1) * s_hi).astype(o_ref.dtype)
        o_ref[...] = y
        if kv_heads:
            k_ref, v_ref = refs
            for h in range(N_KV_HEADS):
                k_ref[h] = y[:, Q_W + h * HEAD_DIM:Q_W + (h + 1) * HEAD_DIM]
                v_ref[h] = y[:, QK_W + h * HEAD_DIM:QK_W + (h + 1) * HEAD_DIM]

    row = pl.BlockSpec((tm, IN_W), lambda i: (i, 0))
    tab = pl.BlockSpec((tm, LANES), lambda i: (i, 0))
    ins = [proj, *tabs]
    specs = [row, tab, tab, tab]
    if norm:
        e, et = _seg_mats()
        ins += [qk_gain, e, et]
        specs += [pl.BlockSpec((1, IN_W), lambda i: (0, 0)), pl.BlockSpec((IN_W, LANES), lambda i: (0, 0)),
                  pl.BlockSpec((LANES, IN_W), lambda i: (0, 0))]
    out_shape, out_specs = [jax.ShapeDtypeStruct((S, IN_W), BF16)], [row]
    if kv_heads:
        out_shape += [jax.ShapeDtypeStruct((N_KV_HEADS, S, HEAD_DIM), BF16)] * 2
        out_specs += [pl.BlockSpec((N_KV_HEADS, tm, HEAD_DIM), lambda i: (0, i, 0))] * 2
    res = _pc(body, name=name, out_shape=tuple(out_shape), grid=(S // tm,),
              in_specs=specs, out_specs=tuple(out_specs), compiler_params=_cp("parallel"))(*ins)
    return res if kv_heads else res[0]


def _headprep_bwd(dpr, tabs, shift, proj=None, qk_gain=None, name="headprep_bwd"):
    parts = isinstance(dpr, (tuple, list))
    S = dpr[0].shape[0] if parts else dpr.shape[0]
    tm = _row_tile(S, 256)
    norm = qk_gain is not None

    def body(*refs):
        refs = list(refs)
        d_refs = [refs.pop(0) for _ in range(4 if parts else 1)]
        c_ref, lo_ref, hi_ref = (refs.pop(0) for _ in range(3))
        if norm:
            p_ref, g_ref, e_ref, et_ref, o_ref, dg_ref = refs
        else:
            (o_ref,) = refs
        if parts:
            dq_ref, dk_ref, dv_ref, dqm_ref = d_refs
            d = jnp.concatenate([dq_ref[...]] + [dk_ref[h] for h in range(N_KV_HEADS)]
                                + [dv_ref[h] for h in range(N_KV_HEADS)] + [dqm_ref[...]], axis=1).astype(F32)
        else:
            d = d_refs[0][...].astype(F32)
        c, s_lo, s_hi, qk = _rope_cols((c_ref, lo_ref, hi_ref), tm)
        dx = d * c + pltpu.roll(d * s_lo, IN_W - shift, 1) + pltpu.roll(d * s_hi, shift, 1)
        if norm:
            x = p_ref[...]
            r = lax.rsqrt(_seg_mean(x * x, e_ref, et_ref) + EPS)
            xh = x * r

            @pl.when(pl.program_id(0) == 0)
            def _():
                dg_ref[...] = jnp.zeros_like(dg_ref)

            dg_ref[...] += jnp.sum(jnp.where(qk, dx * xh, 0.0), axis=0, keepdims=True)
            dxh = dx * g_ref[...]
            dn = r * (dxh - xh * _seg_mean(dxh * xh, e_ref, et_ref))
            dx = jnp.where(qk, dn, dx)
        o_ref[...] = dx.astype(o_ref.dtype)

    row = pl.BlockSpec((tm, IN_W), lambda i: (i, 0))
    tab = pl.BlockSpec((tm, LANES), lambda i: (i, 0))
    vec = pl.BlockSpec((1, IN_W), lambda i: (0, 0))
    if parts:
        heads = pl.BlockSpec((N_KV_HEADS, tm, HEAD_DIM), lambda i: (0, i, 0))
        ins = [*dpr, *tabs]
        specs = [pl.BlockSpec((tm, Q_W), lambda i: (i, 0)), heads, heads, pl.BlockSpec((tm, QM_W), lambda i: (i, 0)), tab, tab, tab]
    else:
        ins = [dpr, *tabs]
        specs = [row, tab, tab, tab]
    out_shape = jax.ShapeDtypeStruct((S, IN_W), BF16)
    out_specs = row
    if norm:
        e, et = _seg_mats()
        ins += [proj, qk_gain, e, et]
        specs += [row, vec, pl.BlockSpec((IN_W, LANES), lambda i: (0, 0)), pl.BlockSpec((LANES, IN_W), lambda i: (0, 0))]
        out_shape = (out_shape, jax.ShapeDtypeStruct((1, IN_W), F32))
        out_specs = (row, vec)
    return _pc(body, name=name, out_shape=out_shape, grid=(S // tm,), in_specs=specs, out_specs=out_specs,
               compiler_params=_cp("arbitrary" if norm else "parallel"))(*ins)


CHAIN_ROWS = 128


def _skewed(n, stages):
    for t in range(n + len(stages) - 1):
        for s, stage in enumerate(stages):
            if 0 <= t - s < n:
                stage(t - s)


def _chain_slices(G, bq):
    cr = min(CHAIN_ROWS, bq)
    per = bq // cr
    return [(c // per, slice((c % per) * cr, (c % per + 1) * cr), slice(c * cr, (c + 1) * cr)) for c in range(G * per)]


def _v_ones(v):
    return jnp.concatenate([v, jnp.ones(v.shape, v.dtype)], axis=1)


def _q_dims(q, k, heads, kv_per_head=False):
    if heads is None:
        return q.shape
    return (1 if kv_per_head else k.shape[0]), heads, q.shape[0], k.shape[2]


def _q_shape(heads, NB, G, L, HD):
    return (NB, G, L, HD) if heads is None else (L, NB * G * HD)


def _q_spec(heads, G, rows, HD, index):
    if heads is None:
        return pl.BlockSpec((1, G, rows, HD), lambda *ids: (index(*ids)[0], 0, index(*ids)[1], 0))
    return pl.BlockSpec((rows, G * HD), lambda *ids: index(*ids)[::-1])


def _q_at(heads, g, hr, HD):
    return (0, g, hr, slice(None)) if heads is None else (hr, slice(g * HD, (g + 1) * HD))


def _window(L, blk, radius):
    if radius is None:
        return L, None
    W = min(L, blk + 2 * radius)
    assert blk % radius == 0 and (L - W) % radius == 0
    return W, lambda n: radius * jnp.clip(n * (blk // radius) - 1, 0, (L - W) // radius)


def _win_specs(G, W, HD, start, with_g):
    E = pl.Element
    st = (lambda n: 0) if start is None else start
    if with_g:
        return pl.BlockSpec((E(1), E(G), E(W), E(HD)), lambda b, n: (b, 0, st(n), 0))
    return pl.BlockSpec((E(1), E(W), E(HD)), lambda b, n: (b, st(n), 0))


def _attn_delta(do, o, *, dlse=None, lse=None, sink=None, heads=None, name="attn_delta"):
    HD = HEAD_DIM
    (NB, G), L = (heads, do.shape[0]) if heads is not None else (do.shape[:2], do.shape[2])
    bl = _row_tile(L, 1024)

    def body(*refs):
        refs = list(refs)
        sink_ref = refs.pop(0) if sink is not None else None
        do_ref, o_ref = refs.pop(0), refs.pop(0)
        dlse_ref = refs.pop(0) if dlse is not None else None
        lse_ref = refs.pop(0) if sink is not None else None
        delta_ref = refs.pop(0)
        b = pl.program_id(0)
        if heads is None:
            delta = jnp.sum(do_ref[0].astype(F32) * o_ref[0].astype(F32), axis=-1, keepdims=True)
        else:
            prod = do_ref[...].astype(F32) * o_ref[...].astype(F32)
            delta = jnp.concatenate([jnp.sum(prod[:, g * HD:(g + 1) * HD], axis=-1, keepdims=True)[None] for g in range(G)])
        if dlse is not None:
            delta = delta - dlse_ref[0]
        delta_ref[0] = delta
        if sink is not None:
            ds_ref = refs.pop(0)

            @pl.when(pl.program_id(1) == 0)
            def _():
                ds_ref[...] = jnp.zeros_like(ds_ref)

            for g in range(G):
                ps = jnp.exp(sink_ref[b * G + g] - lse_ref[0, g]) * delta[g]
                ds_ref[0, g] -= jnp.sum(ps)

    blk = _q_spec(None if heads is None else G, G, bl, HD, lambda b, n: (b, n))
    col = pl.BlockSpec((1, G, bl, 1), lambda b, n: (b, 0, n, 0))
    ins, specs = [do, o], [blk, blk]
    if dlse is not None:
        ins, specs = ins + [dlse], specs + [col]
    out_shape = jax.ShapeDtypeStruct((NB, G, L, 1), F32)
    out_specs = col
    if sink is not None:
        ins, specs = [sink] + ins + [lse], [pl.BlockSpec(memory_space=pltpu.SMEM)] + specs + [col]
        out_shape = (out_shape, jax.ShapeDtypeStruct((NB, G, 1, LANES), F32))
        out_specs = (col, pl.BlockSpec((1, G, 1, LANES), lambda b, n: (b, 0, 0, 0)))
    return _pc(body, name=name, out_shape=out_shape, grid=(NB, L // bl), in_specs=specs, out_specs=out_specs,
               compiler_params=_cp("parallel", "arbitrary"))(*ins)


def _attn_fwd_full(q, k, v, *, bq, tk, heads=None, name="attn_fwd_full"):
    NB, G, L, HD = _q_dims(q, k, heads)
    Lk = k.shape[1]
    nq, nk = L // bq, Lk // tk
    rows = G * bq
    chains = _chain_slices(G, bq)

    def body(q_ref, k_ref, v_ref, o_ref, lse_ref, m_sc, acc_sc, q_sc):
        j = pl.program_id(2)

        @pl.when(j == 0)
        def _():
            m_sc[...] = jnp.full_like(m_sc, NEG)
            acc_sc[...] = jnp.zeros_like(acc_sc)
            for g, hr, sl in chains:
                q_sc[sl] = q_ref[_q_at(heads, g, hr, HD)] * ATTN_SCALE

        kk = k_ref[0]
        vv = _v_ones(v_ref[0])
        st = [dict() for _ in chains]

        def scores(c):
            st[c]["s"] = lax.dot_general(q_sc[chains[c][2]], kk, NT_DIMS, preferred_element_type=F32)

        def softmax(c):
            sl = chains[c][2]
            m_prev = m_sc[sl]
            m_new = jnp.maximum(m_prev, jnp.max(st[c]["s"], axis=1, keepdims=True))
            st[c]["p"] = jnp.exp(st[c].pop("s") - m_new).astype(BF16)
            st[c]["alpha"] = jnp.exp(m_prev - m_new)
            m_sc[sl] = m_new

        def values(c):
            sl = chains[c][2]
            acc_sc[sl] = st[c].pop("alpha") * acc_sc[sl] + jnp.dot(st[c].pop("p"), vv, preferred_element_type=F32)

        _skewed(len(chains), (scores, softmax, values))

        @pl.when(j == nk - 1)
        def _():
            for g, hr, sl in chains:
                acc = acc_sc[sl]
                l = acc[:, HD:HD + 1]
                o_ref[_q_at(heads, g, hr, HD)] = (acc[:, :HD] / l).astype(o_ref.dtype)
                lse_ref[0, g, hr, :] = m_sc[sl] + jnp.log(l)

    qspec = _q_spec(heads, G, bq, HD, lambda b, n, j: (b, n))
    kspec = pl.BlockSpec((1, tk, HD), lambda b, n, j: (b, j, 0))
    return _pc(body, name=name,
               out_shape=(jax.ShapeDtypeStruct(_q_shape(heads, NB, G, L, HD), BF16), jax.ShapeDtypeStruct((NB, G, L, 1), F32)),
               grid=(NB, nq, nk), in_specs=[qspec, kspec, kspec],
               out_specs=(qspec, pl.BlockSpec((1, G, bq, 1), lambda b, n, j: (b, 0, n, 0))),
               scratch_shapes=[pltpu.VMEM((rows, 1), F32), pltpu.VMEM((rows, 2 * HD), F32), pltpu.VMEM((rows, HD), BF16)],
               compiler_params=_cp("parallel", "parallel", "arbitrary"))(q, k, v)


def _attn_bwd_full(q, k, v, do, lse, o, *, bq, tk, heads=None, kv_per_head=False, name="attn_bwd_full"):
    NB, G, L, HD = _q_dims(q, k, heads, kv_per_head)
    Lk = k.shape[1]
    nq, nk = L // bq, Lk // tk
    chains = _chain_slices(G, bq)
    KH = G if kv_per_head else 1
    assert not kv_per_head or nk == 1

    def body(q_ref, k_ref, v_ref, do_ref, lse_ref, o_ref, dqp_ref, dk_ref, dv_ref, dk_sc, dv_sc):
        n = pl.program_id(2)

        @pl.when(n == 0)
        def _():
            dk_sc[...] = jnp.zeros_like(dk_sc)
            dv_sc[...] = jnp.zeros_like(dv_sc)

        kks = [k_ref[h] for h in range(KH)]
        vvs = [v_ref[h] for h in range(KH)]
        st = [dict() for _ in chains]

        def scores(c):
            g, hr, _ = chains[c]
            at = _q_at(heads, g, hr, HD)
            st[c]["q"] = q_ref[at] * ATTN_SCALE
            st[c]["do"] = do_ref[at]
            st[c]["s"] = lax.dot_general(st[c]["q"], kks[g % KH], NT_DIMS, preferred_element_type=F32)
            st[c]["dp"] = lax.dot_general(st[c]["do"], vvs[g % KH], NT_DIMS, preferred_element_type=F32)
            st[c]["delta"] = jnp.sum(st[c]["do"].astype(F32) * o_ref[at].astype(F32), axis=-1, keepdims=True)

        def softmax(c):
            g, hr, _ = chains[c]
            p = jnp.exp(st[c].pop("s") - lse_ref[0, g, hr, :])
            st[c]["ds"] = (p * (st[c].pop("dp") - st[c].pop("delta"))).astype(BF16)
            st[c]["p"] = p.astype(BF16)

        def grads(c):
            g, hr, _ = chains[c]
            ds = st[c].pop("ds")
            dv_sc[g % KH] += lax.dot_general(st[c].pop("p"), st[c].pop("do"), TN_DIMS, preferred_element_type=F32)
            dk_sc[g % KH] += lax.dot_general(ds, st[c].pop("q"), TN_DIMS, preferred_element_type=F32)
            dqp_ref[(0,) + _q_at(heads, g, hr, HD)] = jnp.dot(ds, kks[g % KH], preferred_element_type=F32) * ATTN_SCALE

        _skewed(len(chains), (scores, softmax, grads))

        @pl.when(n == nq - 1)
        def _():
            dk_ref[...] = dk_sc[...].astype(dk_ref.dtype)
            dv_ref[...] = dv_sc[...].astype(dv_ref.dtype)

    qspec = _q_spec(heads, G, bq, HD, lambda b, m, n: (b, n))
    cspec = pl.BlockSpec((1, G, bq, 1), lambda b, m, n: (b, 0, n, 0))
    kspec = pl.BlockSpec((KH, tk, HD), lambda b, m, n: (b, m, 0))
    kv_shape = jax.ShapeDtypeStruct((NB * KH, Lk, HD), BF16)
    if heads is None:
        pspec = pl.BlockSpec((1, 1, G, bq, HD), lambda b, m, n: (m, b, 0, n, 0))
    else:
        pspec = pl.BlockSpec((1, bq, G * HD), lambda b, m, n: (m, n, b))
    dqp, dk, dv = _pc(body, name=name,
                      out_shape=(jax.ShapeDtypeStruct((nk,) + _q_shape(heads, NB, G, L, HD), F32), kv_shape, kv_shape),
                      grid=(NB, nk, nq), in_specs=[qspec, kspec, kspec, qspec, cspec, qspec],
                      out_specs=(pspec, kspec, kspec),
                      scratch_shapes=[pltpu.VMEM((KH, tk, HD), F32), pltpu.VMEM((KH, tk, HD), F32)],
                      compiler_params=_cp("parallel", "parallel", "arbitrary"))(q, k, v, do, lse, o)
    if nk == 1:
        return dqp[0].astype(BF16), dk, dv
    bl = _row_tile(L, 512)

    def sum_body(p_ref, o_ref):
        acc = p_ref[0]
        for j in range(1, nk):
            acc = acc + p_ref[j]
        o_ref[...] = acc.astype(o_ref.dtype)

    if heads is None:
        pspec = pl.BlockSpec((nk, 1, G, bl, HD), lambda b, n: (0, b, 0, n, 0))
    else:
        pspec = pl.BlockSpec((nk, bl, G * HD), lambda b, n: (0, n, b))
    dq = _pc(sum_body, name=name + "_sum", out_shape=jax.ShapeDtypeStruct(_q_shape(heads, NB, G, L, HD), BF16),
             grid=(NB, L // bl), in_specs=[pspec], out_specs=_q_spec(heads, G, bl, HD, lambda b, n: (b, n)),
             compiler_params=_cp("parallel", "parallel"))(dqp)
    return dq, dk, dv


def _attn_fwd_win(q, k, v, *, radius, sink=None, bq, heads=None, kv_per_head=False, carry=None, name="attn_fwd_win"):
    NB, G, L, HD = _q_dims(q, k, heads, kv_per_head)
    W, start = _window(k.shape[1], bq, radius)
    chains = _chain_slices(G, bq)

    def body(*refs):
        if sink is not None:
            sink_ref, *refs = refs
        q_ref, k_ref, v_ref, o_ref, lse_ref = refs
        b, n = pl.program_id(0), pl.program_id(1)
        kks = [k_ref[g] for g in range(G)] if kv_per_head else [k_ref[0]] * G
        vvs = [_v_ones(v_ref[g]) for g in range(G)] if kv_per_head else [_v_ones(v_ref[0])] * G
        st = [dict() for _ in chains]

        def scores(c):
            g, hr, _ = chains[c]
            s = lax.dot_general(q_ref[_q_at(heads, g, hr, HD)] * ATTN_SCALE, kks[g], NT_DIMS, preferred_element_type=F32)
            if radius is not None:
                qpos = n * bq + hr.start + lax.broadcasted_iota(jnp.int32, (hr.stop - hr.start, 1), 0)
                kpos = start(n) + lax.broadcasted_iota(jnp.int32, (1, W), 1)
                s = jnp.where(jnp.abs(qpos - kpos) <= radius, s, NEG)
            st[c]["s"] = s

        def softmax(c):
            g = chains[c][0]
            m = jnp.max(st[c]["s"], axis=1, keepdims=True)
            if sink is not None:
                m = jnp.maximum(m, sink_ref[b * G + g])
            st[c]["p"] = jnp.exp(st[c].pop("s") - m).astype(BF16)
            st[c]["m"] = m

        def values(c):
            g, hr, _ = chains[c]
            acc = jnp.dot(st[c].pop("p"), vvs[g], preferred_element_type=F32)
            m = st[c].pop("m")
            l = acc[:, HD:HD + 1]
            if sink is not None:
                l = l + jnp.exp(sink_ref[b * G + g] - m)
            o_ref[_q_at(heads, g, hr, HD)] = (acc[:, :HD] / l).astype(o_ref.dtype)
            lse_ref[0, g, hr, :] = m + jnp.log(l)

        _skewed(len(chains), (scores, softmax, values))

    qspec = _q_spec(heads, G, bq, HD, lambda b, n: (b, n))
    kspec = pl.BlockSpec((G, W, HD), lambda b, n: (0, 0, 0)) if kv_per_head else _win_specs(G, W, HD, start, False)
    ins, specs = [q, k, v], [qspec, kspec, kspec]
    if sink is not None:
        ins, specs = [sink] + ins, [pl.BlockSpec(memory_space=pltpu.SMEM)] + specs
    return _call(body, carry, ins, name=name,
                 out_shape=(jax.ShapeDtypeStruct(_q_shape(heads, NB, G, L, HD), BF16), jax.ShapeDtypeStruct((NB, G, L, 1), F32)),
                 grid=(NB, L // bq), in_specs=specs,
                 out_specs=(qspec, pl.BlockSpec((1, G, bq, 1), lambda b, n: (b, 0, n, 0))), sem=("parallel", "parallel"))


def _attn_dq_win(q, k, v, do, lse, delta, *, radius, bq, heads=None, name="attn_dq_win"):
    NB, G, L, HD = _q_dims(q, k, heads)
    W, start = _window(L, bq, radius)
    chains = _chain_slices(G, bq)

    def body(q_ref, k_ref, v_ref, do_ref, lse_ref, dl_ref, dq_ref):
        n = pl.program_id(1)
        kk, vv = k_ref[0], v_ref[0]
        kpos = start(n) + lax.broadcasted_iota(jnp.int32, (1, W), 1)
        st = [dict() for _ in chains]

        def scores(c):
            g, hr, _ = chains[c]
            at = _q_at(heads, g, hr, HD)
            st[c]["s"] = lax.dot_general(q_ref[at] * ATTN_SCALE, kk, NT_DIMS, preferred_element_type=F32)
            st[c]["dp"] = lax.dot_general(do_ref[at], vv, NT_DIMS, preferred_element_type=F32)

        def softmax(c):
            g, hr, _ = chains[c]
            qpos = n * bq + hr.start + lax.broadcasted_iota(jnp.int32, (hr.stop - hr.start, 1), 0)
            p = jnp.where(jnp.abs(qpos - kpos) <= radius, jnp.exp(st[c].pop("s") - lse_ref[0, g, hr, :]), 0.0)
            st[c]["ds"] = (p * (st[c].pop("dp") - dl_ref[0, g, hr, :])).astype(BF16)

        def grads(c):
            g, hr, _ = chains[c]
            dq = jnp.dot(st[c].pop("ds"), kk, preferred_element_type=F32) * ATTN_SCALE
            dq_ref[_q_at(heads, g, hr, HD)] = dq.astype(dq_ref.dtype)

        _skewed(len(chains), (scores, softmax, grads))

    qspec = _q_spec(heads, G, bq, HD, lambda b, n: (b, n))
    cspec = pl.BlockSpec((1, G, bq, 1), lambda b, n: (b, 0, n, 0))
    kspec = _win_specs(G, W, HD, start, False)
    return _pc(body, name=name, out_shape=jax.ShapeDtypeStruct(_q_shape(heads, NB, G, L, HD), BF16), grid=(NB, L // bq),
               in_specs=[qspec, kspec, kspec, qspec, cspec, cspec], out_specs=qspec,
               compiler_params=_cp("parallel", "parallel"))(q, k, v, do, lse, delta)


def _attn_dkv_win(q, k, v, do, lse, delta, *, radius, bk, heads=None, carry=None, name="attn_dkv_win"):
    NB, G, L, HD = _q_dims(q, k, heads)
    W, start = _window(L, bk, radius)
    chains = _chain_slices(G, W)

    def body(q_ref, k_ref, v_ref, do_ref, lse_ref, dl_ref, dk_ref, dv_ref):
        m = pl.program_id(1)
        kk, vv = k_ref[0], v_ref[0]
        kpos = m * bk + lax.broadcasted_iota(jnp.int32, (1, bk), 1)
        st = [dict() for _ in chains]
        out = dict(dk=jnp.zeros((bk, HD), F32), dv=jnp.zeros((bk, HD), F32))

        def scores(c):
            g, hr, _ = chains[c]
            at = _q_at(heads, g, hr, HD)
            st[c]["q"] = q_ref[at] * ATTN_SCALE
            st[c]["do"] = do_ref[at]
            st[c]["s"] = lax.dot_general(st[c]["q"], kk, NT_DIMS, preferred_element_type=F32)
            st[c]["dp"] = lax.dot_general(st[c]["do"], vv, NT_DIMS, preferred_element_type=F32)

        def softmax(c):
            g, hr, _ = chains[c]
            qpos = start(m) + hr.start + lax.broadcasted_iota(jnp.int32, (hr.stop - hr.start, 1), 0)
            p = jnp.where(jnp.abs(qpos - kpos) <= radius, jnp.exp(st[c].pop("s") - lse_ref[0, g, hr, :]), 0.0)
            st[c]["ds"] = (p * (st[c].pop("dp") - dl_ref[0, g, hr, :])).astype(BF16)
            st[c]["p"] = p.astype(BF16)

        def grads(c):
            out["dv"] = out["dv"] + lax.dot_general(st[c].pop("p"), st[c].pop("do"), TN_DIMS, preferred_element_type=F32)
            out["dk"] = out["dk"] + lax.dot_general(st[c].pop("ds"), st[c].pop("q"), TN_DIMS, preferred_element_type=F32)

        _skewed(len(chains), (scores, softmax, grads))
        dk_ref[0] = out["dk"].astype(dk_ref.dtype)
        dv_ref[0] = out["dv"].astype(dv_ref.dtype)

    if heads is None:
        qspec = _win_specs(G, W, HD, start, True)
    else:
        qspec = pl.BlockSpec((pl.Element(W), pl.Element(G * HD)), lambda b, m: (start(m), b * G * HD))
    cspec = _win_specs(G, W, 1, start, True)
    kspec = pl.BlockSpec((1, bk, HD), lambda b, m: (b, m, 0))
    kv_shape = jax.ShapeDtypeStruct((NB, L, HD), BF16)
    return _call(body, carry, [q, k, v, do, lse, delta], name=name, out_shape=(kv_shape, kv_shape), grid=(NB, L // bk),
                 in_specs=[qspec, kspec, kspec, qspec, cspec, cspec], out_specs=(kspec, kspec), sem=("parallel", "parallel"))


def _attn(q, k, v, *, radius=None, sink=None, bq, tk=None, heads=None, kv_per_head=False, carry=None, tag):
    if radius is None and tk < k.shape[1]:
        return _attn_fwd_full(q, k, v, bq=bq, tk=tk, heads=heads, name=f"attn_fwd_{tag}")
    return _attn_fwd_win(q, k, v, radius=radius, sink=sink, bq=bq, heads=heads, kv_per_head=kv_per_head, carry=carry,
                         name=f"attn_fwd_{tag}")


def _attn_bwd(q, k, v, o, lse, do, *, radius=None, sink=None, dlse=None, bq, tk=None, heads=None, kv_per_head=False,
              carry=None, tag):
    if radius is None:
        assert carry is None
        return (*_attn_bwd_full(q, k, v, do, lse, o, bq=bq, tk=tk, heads=heads, kv_per_head=kv_per_head,
                                name=f"attn_bwd_{tag}"), None, None)
    nbg = None if heads is None else (k.shape[0], heads)
    if sink is not None:
        delta, ds = _attn_delta(do, o, lse=lse, sink=sink, heads=nbg, name=f"attn_delta_{tag}")
        dsink = ds[:, :, 0, 0].reshape(-1)
    else:
        delta, dsink = _attn_delta(do, o, dlse=dlse, heads=nbg, name=f"attn_delta_{tag}"), None
    dq = _attn_dq_win(q, k, v, do, lse, delta, radius=radius, bq=bq, heads=heads, name=f"attn_dq_{tag}")
    dk, dv, *arrived = _attn_dkv_win(q, k, v, do, lse, delta, radius=radius, bk=bq, heads=heads, carry=carry,
                                     name=f"attn_dkv_{tag}")
    return dq, dk, dv, dsink, (arrived[0] if arrived else None)


def _combine_fwd(o, lse, name="combine_fwd"):
    H, S, HD = o.shape
    tm = _row_tile(S, 512)

    def body(o_ref, lse_ref, t_ref):
        for g in range(GQA_GROUP):
            hs = [kv * GQA_GROUP + g for kv in range(N_KV_HEADS)]
            ls = [lse_ref[h] for h in hs]
            mx = functools.reduce(jnp.maximum, ls)
            es = [jnp.exp(l - mx) for l in ls]
            den = functools.reduce(jnp.add, es)
            for h, e in zip(hs, es):
                t_ref[h] = (o_ref[h].astype(F32) * (e / den)).astype(t_ref.dtype)

    blk = pl.BlockSpec((H, tm, HD), lambda i: (0, i, 0))
    col = pl.BlockSpec((H, tm, 1), lambda i: (0, i, 0))
    return _pc(body, name=name, out_shape=jax.ShapeDtypeStruct((H, S, HD), BF16), grid=(S // tm,),
               in_specs=[blk, col], out_specs=blk, compiler_params=_cp("parallel"))(o, lse)


def _combine_bwd(dt, o, lse, name="combine_bwd"):
    H, S, HD = o.shape
    tm = _row_tile(S, 512)

    def body(dt_ref, o_ref, lse_ref, do_ref, dlse_ref):
        for g in range(GQA_GROUP):
            hs = [kv * GQA_GROUP + g for kv in range(N_KV_HEADS)]
            ls = [lse_ref[h] for h in hs]
            mx = functools.reduce(jnp.maximum, ls)
            es = [jnp.exp(l - mx) for l in ls]
            den = functools.reduce(jnp.add, es)
            al = [e / den for e in es]
            dts = [dt_ref[h].astype(F32) for h in hs]
            da = [jnp.sum(d * o_ref[h].astype(F32), axis=-1, keepdims=True) for h, d in zip(hs, dts)]
            dot = functools.reduce(jnp.add, [a * d for a, d in zip(al, da)])
            for h, a, d, dd in zip(hs, al, da, dts):
                do_ref[h] = (dd * a).astype(do_ref.dtype)
                dlse_ref[h] = a * (d - dot)

    blk = pl.BlockSpec((H, tm, HD), lambda i: (0, i, 0))
    col = pl.BlockSpec((H, tm, 1), lambda i: (0, i, 0))
    return _pc(body, name=name,
               out_shape=(jax.ShapeDtypeStruct((H, S, HD), BF16), jax.ShapeDtypeStruct((H, S, 1), F32)),
               grid=(S // tm,), in_specs=[blk, blk, col], out_specs=(blk, col), compiler_params=_cp("parallel"))(dt, o, lse)


def _position():
    x, y, c = lax.axis_index("x"), lax.axis_index("y"), lax.axis_index("c")
    return x, y, c


def _peer(pos, k):
    x, y, c = pos
    return (1 - x if k & 4 else x, 1 - y if k & 2 else y, 1 - c if k & 1 else c)


def _linear(p):
    return 4 * p[0] + 2 * p[1] + p[2]


def _exchange_steps(s_ref, r_ref, send_sems, recv_sems, local_sem, gather):
    pos = _position()
    me = _linear(pos)
    own = pltpu.make_async_copy(s_ref if gather else s_ref.at[me], r_ref.at[me], local_sem)
    peers = range(1, N_DEV)

    def sems(k):
        return dict(send_sem=send_sems.at[k - 1], recv_sem=recv_sems.at[k - 1], device_id=_peer(pos, k), device_id_type=MESH)

    def send(k):
        src = s_ref if gather else s_ref.at[_linear(_peer(pos, k))]
        return pltpu.make_async_remote_copy(src_ref=src, dst_ref=r_ref.at[me], **sems(k))

    def arrival(k):
        slot = r_ref.at[_linear(_peer(pos, k))]
        return pltpu.make_async_remote_copy(src_ref=slot, dst_ref=slot, **sems(k))

    def start():
        own.start()
        for k in peers:
            send(k).start()

    def wait():
        for k in peers:
            arrival(k).wait_recv()
        for k in peers:
            send(k).wait_send()
        own.wait()

    return start, wait


EXCHANGE_SEMS = [pltpu.SemaphoreType.DMA((N_DEV - 1,)), pltpu.SemaphoreType.DMA((N_DEV - 1,)), pltpu.SemaphoreType.DMA]


def _exchange(buf, gather, name):
    def body(s_ref, r_ref, *sems):
        start, wait = _exchange_steps(s_ref, r_ref, *sems, gather)
        start()
        wait()

    hbm = pl.BlockSpec(memory_space=pltpu.HBM)
    out_shape = ((N_DEV,) + buf.shape) if gather else buf.shape
    return _pc(body, name=name, out_shape=jax.ShapeDtypeStruct(out_shape, buf.dtype), in_specs=[hbm], out_specs=hbm,
               scratch_shapes=list(EXCHANGE_SEMS))(buf)


def _call(body, carry, ins, *, name, out_shape, grid, in_specs, out_specs, scratch_shapes=(), sem):
    if carry is None:
        return _pc(body, name=name, out_shape=tuple(out_shape), grid=grid, in_specs=list(in_specs),
                   out_specs=tuple(out_specs), scratch_shapes=list(scratch_shapes), compiler_params=_cp(*sem))(*ins)
    buf, gather = carry
    n_in, n_out, n_sc = len(ins), len(out_shape), len(scratch_shapes)

    def wrapped(*refs):
        in_refs, buf_ref = refs[:n_in], refs[n_in]
        out_refs, recv_ref = refs[n_in + 1:n_in + 1 + n_out], refs[n_in + 1 + n_out]
        rest = refs[n_in + 2 + n_out:]
        first = functools.reduce(jnp.logical_and, [pl.program_id(a) == 0 for a in range(len(grid))])
        last = functools.reduce(jnp.logical_and, [pl.program_id(a) == grid[a] - 1 for a in range(len(grid))])

        @pl.when(first)
        def _():
            _exchange_steps(buf_ref, recv_ref, *rest[n_sc:], gather)[0]()

        body(*in_refs, *out_refs, *rest[:n_sc])

        @pl.when(last)
        def _():
            _exchange_steps(buf_ref, recv_ref, *rest[n_sc:], gather)[1]()

    hbm = pl.BlockSpec(memory_space=pltpu.HBM)
    recv_shape = ((N_DEV,) + buf.shape) if gather else buf.shape
    return _pc(wrapped, name=name, out_shape=(*out_shape, jax.ShapeDtypeStruct(recv_shape, buf.dtype)), grid=grid,
               in_specs=[*in_specs, hbm], out_specs=(*out_specs, hbm), scratch_shapes=[*scratch_shapes, *EXCHANGE_SEMS],
               compiler_params=_cp(*(("arbitrary",) * len(grid))))(*ins, buf)


def _reduce_adamw(recv, w, m, v, name):
    _, R, C = recv.shape
    tr = _row_tile(R, 512)

    def body(r_ref, w_ref, m_ref, v_ref, g_ref, d_ref, nm_ref, nv_ref):
        g = r_ref[0].astype(F32)
        for j in range(1, N_DEV):
            g = g + r_ref[j].astype(F32)
        g_ref[...] = g
        nm = ADAM_B1 * m_ref[...] + (1.0 - ADAM_B1) * g
        nv = ADAM_B2 * v_ref[...] + (1.0 - ADAM_B2) * jnp.square(g)
        m_hat = nm / (1.0 - ADAM_B1 ** ADAM_STEP)
        v_hat = nv / (1.0 - ADAM_B2 ** ADAM_STEP)
        d_ref[...] = -ADAM_LR * (m_hat / (jnp.sqrt(v_hat) + ADAM_EPS) + ADAM_WD * w_ref[...])
        nm_ref[...] = nm
        nv_ref[...] = nv

    row = pl.BlockSpec((tr, C), lambda i: (i, 0))
    out = jax.ShapeDtypeStruct((R, C), F32)
    return _pc(body, name=name, out_shape=(out, out, out, out), grid=(R // tr,),
               in_specs=[pl.BlockSpec((N_DEV, tr, C), lambda i: (0, i, 0)), row, row, row],
               out_specs=(row, row, row, row), compiler_params=_cp("parallel"))(recv, w, m, v)


BIG = (("w_in", 2), ("w_mem_kv", 1), ("w_o", 1), ("w_gate_up", 2), ("w_down", 1))
SMALL = ("mem_norm_g", "g_mix_pre", "g_mix_post", "attn_sink", "qk_norm_g", "g_ffn_pre", "g_ffn_post")
SMALL_W = 1024
FIRST, REST = BIG[:1], BIG[1:]


def _pack_local(shards, dtype):
    return jnp.concatenate([s.astype(dtype).reshape(-1, LANES) for s in shards], axis=0)


def _unpack_local(flat, shapes):
    out, r = [], 0
    for shp in shapes:
        n = shp[0] * shp[1] * shp[2] // LANES
        out.append(flat[r:r + n].reshape(shp))
        r += n
    return out


def _unpack_gathered(g, shapes, names=BIG):
    out, r = [], 0
    for (name, dim), shp in zip(names, shapes):
        n = shp[0] * shp[1] * shp[2] // LANES
        t = g[:, r:r + n].reshape((N_DEV,) + tuple(shp))
        if dim == 2:
            t = t.transpose(1, 2, 0, 3).reshape(shp[0], shp[1], N_DEV * shp[2])
        else:
            t = t.transpose(1, 0, 2, 3).reshape(shp[0], N_DEV * shp[1], shp[2])
        out.append(t)
        r += n
    return out


def _pack_for_scatter(full, shapes, dtype, names=BIG):
    parts = []
    for (name, dim), shp, t in zip(names, shapes, full):
        if dim == 2:
            t = t.reshape(shp[0], shp[1], N_DEV, shp[2]).transpose(2, 0, 1, 3)
        else:
            t = t.reshape(shp[0], N_DEV, shp[1], shp[2]).transpose(1, 0, 2, 3)
        parts.append(t.astype(dtype).reshape(N_DEV, -1, LANES))
    return jnp.concatenate(parts, axis=1)


def _pack_small(arrs):
    flat = jnp.concatenate([a.reshape(-1) for a in arrs])
    pad = (-flat.shape[0]) % (8 * SMALL_W)
    return jnp.pad(flat, (0, pad)).reshape(-1, SMALL_W)


def _unpack_small(flat, shapes):
    flat = flat.reshape(-1)
    out, r = [], 0
    for shp in shapes:
        n = 1
        for d in shp:
            n *= d
        out.append(flat[r:r + n].reshape(shp))
        r += n
    return out


def _heads(t, nb, g):
    S = t.shape[0]
    return t.reshape(S, nb, g, HEAD_DIM).transpose(1, 2, 0, 3)


def _unheads(t):
    nb, g, S, hd = t.shape
    return t.transpose(2, 0, 1, 3).reshape(S, nb * g * hd)


def _dilate(t, dil):
    S = t.shape[0]
    g = t.shape[1] // HEAD_DIM
    return t.reshape(S // dil, dil, g, HEAD_DIM).transpose(1, 2, 0, 3)


def _undilate(t):
    dil, g, L, w = t.shape
    return t.transpose(1, 2, 0, 3).reshape(g, L * dil, w)


FULL_BQ_FWD, FULL_TK_FWD = 512, 4096
FULL_BQ_BWD, FULL_TK_BWD = 1024, 2048


def _mixer_fwd(kind, pr, kv, sink, li, carry=None):
    S = pr.shape[0]
    if kind == 0:
        tok, lse, *arrived = _attn(pr, *kv, radius=A_RADIUS, sink=sink, bq=min(256, S), heads=GQA_GROUP, carry=carry,
                                   tag=f"a{li}")
        return tok, lse, (arrived[0] if arrived else None)
    assert carry is None
    if kind == 1:
        tok, lse = _attn(pr, *kv, bq=min(FULL_BQ_FWD, S), tk=min(FULL_TK_FWD, S), heads=GQA_GROUP, tag=f"b{li}")
        return tok, lse, None
    saved, outs, lses = [], [], []
    for g, (window, dil) in enumerate(C_GROUPS):
        q = _dilate(pr[:, g * GQA_GROUP * HEAD_DIM:(g + 1) * GQA_GROUP * HEAD_DIM], dil)
        k = _dilate(pr[:, Q_W + g * HEAD_DIM:Q_W + (g + 1) * HEAD_DIM], dil)[:, 0]
        v = _dilate(pr[:, QK_W + g * HEAD_DIM:QK_W + (g + 1) * HEAD_DIM], dil)[:, 0]
        o, lse = _attn(q, k, v, radius=window // (2 * dil), bq=min(256, S // dil), tag=f"c{li}g{g}")
        saved.append((q, k, v, o, lse))
        outs.append(_undilate(o))
        lses.append(_undilate(lse))
    o_all, lse_all = jnp.concatenate(outs, 0), jnp.concatenate(lses, 0)
    tok = _combine_fwd(o_all, lse_all, name=f"combine_fwd_{li}")
    return tok.transpose(1, 0, 2).reshape(S, Q_W), (saved, o_all, lse_all), None


def _mixer_bwd(kind, dcat, pr, kv, cat, saved, sink, li, carry=None):
    S = dcat.shape[0]
    if kind == 0:
        return _attn_bwd(pr, *kv, cat, saved, dcat, radius=A_RADIUS, sink=sink, bq=min(256, S), heads=GQA_GROUP, carry=carry,
                         tag=f"a{li}")
    assert carry is None
    if kind == 1:
        return _attn_bwd(pr, *kv, cat, saved, dcat, bq=min(FULL_BQ_BWD, S), tk=min(FULL_TK_BWD, S), heads=GQA_GROUP,
                         tag=f"b{li}")
    per_group, o_all, lse_all = saved
    dt = dcat[:, :Q_W].reshape(S, N_TOK_HEADS, HEAD_DIM).transpose(1, 0, 2)
    do_all, dlse_all = _combine_bwd(dt, o_all, lse_all, name=f"combine_bwd_{li}")
    dqs, dks, dvs = [], [], []
    for g, (window, dil) in enumerate(C_GROUPS):
        q, k, v, o, lse = per_group[g]
        L = S // dil
        hs = slice(g * GQA_GROUP, (g + 1) * GQA_GROUP)
        do = do_all[hs].reshape(GQA_GROUP, L, dil, HEAD_DIM).transpose(2, 0, 1, 3)
        dlse = dlse_all[hs].reshape(GQA_GROUP, L, dil, 1).transpose(2, 0, 1, 3)
        dq, dk, dv, _, _ = _attn_bwd(q, k, v, o, lse, do, radius=window // (2 * dil), dlse=dlse, bq=min(256, L),
                                     tag=f"c{li}g{g}")
        dqs.append(dq.transpose(2, 0, 1, 3).reshape(S, GQA_GROUP * HEAD_DIM))
        dks.append(dk.transpose(1, 0, 2).reshape(S, HEAD_DIM))
        dvs.append(dv.transpose(1, 0, 2).reshape(S, HEAD_DIM))
    return jnp.concatenate(dqs, 1), jnp.concatenate(dks, 1), jnp.concatenate(dvs, 1), None, None


def kernel(x, mem, mem_norm_g, w_in, w_mem_kv, w_o, g_mix_pre, g_mix_post, attn_sink, qk_norm_g, w_gate_up, w_down, g_ffn_pre, g_ffn_post, loss_target, m_mem_norm_g, m_w_in, m_w_mem_kv, m_w_o, m_g_mix_pre, m_g_mix_post, m_attn_sink, m_qk_norm_g, m_w_gate_up, m_w_down, m_g_ffn_pre, m_g_ffn_post, v_mem_norm_g, v_w_in, v_w_mem_kv, v_w_o, v_g_mix_pre, v_g_mix_post, v_attn_sink, v_qk_norm_g, v_w_gate_up, v_w_down, v_g_ffn_pre, v_g_ffn_post):
    given = dict(locals())
    depth = w_in.shape[0]
    S, D = x.shape[1], x.shape[2]
    def shapes_of(names):
        return [(1,) + tuple(given[n].shape[1:]) for n, _ in names]

    def layer_pack(pre, l, dtype, names=BIG):
        return _pack_local([given[pre + n][l:l + 1] for n, _ in names], dtype)

    def layer_weights(gathered, names=BIG):
        return [t[0] for t in _unpack_gathered(gathered, shapes_of(names), names)]

    W = [None] * depth
    W[0] = layer_weights(_exchange(layer_pack("", 0, BF16, FIRST), True, "gather_w0_in"), FIRST)

    tabs = _rope_tables(S)
    mem_n = _rms_fwd(mem[0], mem_norm_g[None], BF16, name="rms_mem")

    saved = []
    xc = x[0]
    for i in range(depth):
        kind = i % N_MIXERS
        (tab, shift) = tabs[1] if kind == 1 else tabs[0]
        sink = attn_sink[i // N_MIXERS] if kind == 0 else None
        qk_gain = _qk_gain_row(qk_norm_g[i // N_MIXERS]) if kind == 1 else None
        carry = (layer_pack("", 0, BF16, REST), True) if i == 0 else None
        h, proj, *arrived = _mm(xc, W[i][0], F32, pre_g=g_mix_pre[i][None], carry=carry, name=f"mm_in_{i}")
        if carry is not None:
            W[0] = W[0] + layer_weights(arrived[0], REST)
        W_in, W_mkv, W_o, W_gu, W_dn = W[i]
        if kind == 2:
            pr, kv = _headprep_fwd(proj, tab, shift, qk_gain, name=f"headprep_fwd_{i}"), None
        else:
            pr, *kv = _headprep_fwd(proj, tab, shift, qk_gain, kv_heads=True, name=f"headprep_fwd_{i}")
        carry = (layer_pack("", 1, BF16), True) if i == 0 and depth > 1 else None
        tok, mix_saved, arrived = _mixer_fwd(kind, pr, kv, sink, i, carry)
        if carry is not None:
            W[1] = layer_weights(arrived)
        (mkv,) = _mm(mem_n, W_mkv, BF16, name=f"mm_mkv_{i}")
        qm = pr[:, QK_W + KV_W:]
        km = _heads(mkv[:, :QM_W], N_MEM_HEADS, 1)[:, 0]
        vm = _heads(mkv[:, QM_W:], N_MEM_HEADS, 1)[:, 0]
        mo, mlse = _attn(qm, km, vm, bq=min(256, S), tk=km.shape[1], heads=N_MEM_HEADS, kv_per_head=True, tag=f"m{i}")
        cat = jnp.concatenate([tok, mo], axis=1)
        o, x1 = _mm(cat, W_o, F32, post=(g_mix_post[i][None], xc), name=f"mm_o_{i}")
        carry = (layer_pack("", i + 2, BF16), True) if i + 2 < depth else None
        h2, gu, act, *arrived = _mm(x1, W_gu, BF16, pre_g=g_ffn_pre[i][None], swiglu=True, carry=carry, name=f"mm_gu_{i}")
        if carry is not None:
            W[i + 2] = layer_weights(arrived[0])
        f, x2 = _mm(act, W_dn, F32, post=(g_ffn_post[i][None], x1), name=f"mm_dn_{i}")
        saved.append(dict(x=xc, h=h, proj=proj, pr=pr, kv=kv, mix=mix_saved, qm=qm, km=km, vm=vm, mo=mo, mlse=mlse, cat=cat, o=o,
                          x1=x1, h2=h2, gu=gu, act=act, f=f))
        xc = x2

    dy, sq = _loss_head(xc, loss_target[0], name="loss_head")
    loss = lax.psum(sq[0, 0] * (0.5 / D), ("x", "y", "c"))

    grads = {n: [None] * depth for n in ("w_in", "w_mem_kv", "w_o", "w_gate_up", "w_down", "g_mix_pre", "g_mix_post",
                                         "g_ffn_pre", "g_ffn_post")}
    d_sink = [jnp.zeros((N_TOK_HEADS,), F32) for _ in range(attn_sink.shape[0])]
    d_qkg = [jnp.zeros((2, HEAD_DIM), F32) for _ in range(qk_norm_g.shape[0])]
    dmem_n = jnp.zeros((mem.shape[1], D), F32)
    recv = [None] * depth

    def scatter_pack(l, names=BIG):
        return _pack_for_scatter([grads[n][l][None] for n, _ in names], shapes_of(names), BF16, names)

    dx = dy
    for i in reversed(range(depth)):
        kind = i % N_MIXERS
        sv = saved[i]
        (tab, shift) = tabs[1] if kind == 1 else tabs[0]
        sink = attn_sink[i // N_MIXERS] if kind == 0 else None
        W_in, W_mkv, W_o, W_gu, W_dn = W[i]
        df, dgu, dg = _mm(dx, W_dn, BF16, nt=True, gu=sv["gu"], pre_bwd=(sv["f"], g_ffn_post[i][None]), name=f"mmb_dn_{i}")
        grads["g_ffn_post"][i] = dg[0]
        grads["w_down"][i] = _mm_tn(sv["act"], df, name=f"mmw_dn_{i}")
        dx1, dg = _mm(dgu, W_gu, F32, nt=True, post_bwd=(sv["x1"], g_ffn_pre[i][None], dx), name=f"mmb_gu_{i}")
        grads["g_ffn_pre"][i] = dg[0]
        if i + 1 < depth:
            grads["w_gate_up"][i], recv[i + 1] = _mm_tn(sv["h2"], dgu, carry=(scatter_pack(i + 1), False), name=f"mmw_gu_{i}")
        else:
            grads["w_gate_up"][i] = _mm_tn(sv["h2"], dgu, name=f"mmw_gu_{i}")
        do, dcat, dg = _mm(dx1, W_o, BF16, nt=True, pre_bwd=(sv["o"], g_mix_post[i][None]), name=f"mmb_o_{i}")
        grads["g_mix_post"][i] = dg[0]
        grads["w_o"][i] = _mm_tn(sv["cat"], do, name=f"mmw_o_{i}")
        dqm, dkm, dvm, _, _ = _attn_bwd(sv["qm"], sv["km"], sv["vm"], sv["mo"], sv["mlse"], dcat[:, Q_W:], bq=min(256, S),
                                        tk=sv["km"].shape[1], heads=N_MEM_HEADS, kv_per_head=True, tag=f"m{i}")
        dmkv = jnp.concatenate([_unheads(dkm[:, None]), _unheads(dvm[:, None])], axis=1).astype(BF16)
        grads["w_mem_kv"][i] = _mm_tn(mem_n, dmkv, name=f"mmw_mkv_{i}")
        dmem_n = dmem_n + _mm(dmkv, W_mkv, F32, nt=True, name=f"mmb_mkv_{i}")[0]
        carry = (scatter_pack(0, REST), False) if i == 0 else None
        dq, dk, dv, dsink, recv0_rest = _mixer_bwd(kind, dcat, sv["pr"], sv["kv"], sv["cat"], sv["mix"], sink, i, carry)
        if dsink is not None:
            d_sink[i // N_MIXERS] = dsink
        dpr = jnp.concatenate([dq, dk, dv, dqm], axis=1) if kind == 2 else (dq, dk, dv, dqm)
        if kind == 1:
            dproj, dgc = _headprep_bwd(dpr, tab, shift, sv["proj"], _qk_gain_row(qk_norm_g[i // N_MIXERS]),
                                       name=f"headprep_bwd_{i}")
            d_qkg[i // N_MIXERS] = jnp.stack([dgc[0, :Q_W].reshape(N_TOK_HEADS, HEAD_DIM).sum(0),
                                              dgc[0, Q_W:QK_W].reshape(N_KV_HEADS, HEAD_DIM).sum(0)])
        else:
            dproj = _headprep_bwd(dpr, tab, shift, name=f"headprep_bwd_{i}")
        dx, dg = _mm(dproj, W_in, F32, nt=True, post_bwd=(sv["x"], g_mix_pre[i][None], dx1), name=f"mmb_in_{i}")
        grads["g_mix_pre"][i] = dg[0]
        grads["w_in"][i] = _mm_tn(sv["h"], dproj, name=f"mmw_in_{i}")
    _, dg_mem = _rms_bwd(mem[0], mem_norm_g[None], dmem_n, BF16, name="rmsb_mem")

    def update(received, l, names, tag):
        res = _reduce_adamw(received, *[layer_pack(pre, l, F32, names) for pre in ("", "m_", "v_")], name=f"adamw_{tag}")
        return [_unpack_local(r, shapes_of(names)) for r in res]

    recv0_first = _exchange(scatter_pack(0, FIRST), False, "scatter_g0_in")
    per_layer = [[a + b for a, b in zip(update(recv0_first, 0, FIRST, "0_in"), update(recv0_rest, 0, REST, "0"))]]
    per_layer += [update(recv[l], l, BIG, str(l)) for l in range(1, depth)]
    big_out = [[jnp.concatenate(ts, axis=0) for ts in zip(*[per_layer[l][j] for l in range(depth)])] for j in range(4)]

    small_grads = dict(mem_norm_g=dg_mem[0], g_mix_pre=jnp.stack(grads["g_mix_pre"]), g_mix_post=jnp.stack(grads["g_mix_post"]),
                       attn_sink=jnp.stack(d_sink), qk_norm_g=jnp.stack(d_qkg), g_ffn_pre=jnp.stack(grads["g_ffn_pre"]),
                       g_ffn_post=jnp.stack(grads["g_ffn_post"]))
    sg = _pack_small([small_grads[n] for n in SMALL])
    srecv = _exchange(sg, True, "gather_small_grads")
    spacked = lambda pre: _pack_small([given[pre + n] for n in SMALL])
    gs, ds, ms, vs = _reduce_adamw(srecv, spacked(""), spacked("m_"), spacked("v_"), name="adamw_small")

    out = {}
    for pre, fb, fs in zip(("grad_", "delta_", "new_m_", "new_v_"), big_out, (gs, ds, ms, vs)):
        for (n, _), t in zip(BIG, fb):
            out[pre + n] = t
        for n, t in zip(SMALL, _unpack_small(fs, [given[n].shape for n in SMALL])):
            out[pre + n] = t
    order = ("mem_norm_g", "w_in", "w_mem_kv", "w_o", "g_mix_pre", "g_mix_post", "attn_sink", "qk_norm_g", "w_gate_up",
             "w_down", "g_ffn_pre", "g_ffn_post")
    return (loss, dx[None], *[out[p + n] for p in ("grad_", "delta_", "new_m_", "new_v_") for n in order])
```

```python
import functools

import jax
import jax.numpy as jnp
from jax import lax
from jax.experimental import pallas as pl
from jax.experimental.pallas import tpu as pltpu

F32 = jnp.float32
BF16 = jnp.bfloat16

HEAD_DIM = 64
N_TOK_HEADS = 12
N_KV_HEADS = 3
GQA_GROUP = 4
N_MEM_HEADS = 4
Q_W = N_TOK_HEADS * HEAD_DIM
KV_W = N_KV_HEADS * HEAD_DIM
QM_W = N_MEM_HEADS * HEAD_DIM
QK_W = Q_W + KV_W
IN_W = Q_W + 2 * KV_W + QM_W
N_MIXERS = 3
A_RADIUS = 128
C_GROUPS = ((128, 1), (512, 4), (2048, 16))
ROPE_THETA = 500000.0
ROPE_DIMS = HEAD_DIM // 4
AXIAL_THETA = 10000.0
GRID_W = 64
EPS = 1e-6
ATTN_SCALE = HEAD_DIM ** -0.5
NEG = -1e30

ADAM_LR = 0.001
ADAM_B1 = 0.9
ADAM_B2 = 0.999
ADAM_EPS = 1e-08
ADAM_WD = 0.01
ADAM_STEP = 10

N_DEV = 8
LANES = 128
VMEM_LIMIT = 56 * 1024 * 1024
MESH = pl.DeviceIdType.MESH
NT_DIMS = (((1,), (1,)), ((), ()))
TN_DIMS = (((0,), (0,)), ((), ()))


def _pc(body, **kw):
    return pl.pallas_call(body, **kw)


def _cp(*sem):
    return pltpu.CompilerParams(dimension_semantics=sem, vmem_limit_bytes=VMEM_LIMIT)


def _row_tile(m, cap=512):
    t = cap
    while m % t:
        t //= 2
    return t


def _rms_fwd(x, g, out_dtype, name="rms_fwd"):
    M, D = x.shape
    tm = _row_tile(M)

    def body(x_ref, g_ref, o_ref):
        xv = x_ref[...]
        y = xv * lax.rsqrt(jnp.mean(xv * xv, axis=-1, keepdims=True) + EPS) * g_ref[...]
        o_ref[...] = y.astype(o_ref.dtype)

    row = pl.BlockSpec((tm, D), lambda i: (i, 0))
    vec = pl.BlockSpec((1, D), lambda i: (0, 0))
    return _pc(body, name=name, out_shape=jax.ShapeDtypeStruct((M, D), out_dtype), grid=(M // tm,),
               in_specs=[row, vec], out_specs=row, compiler_params=_cp("parallel"))(x, g)


def _rms_bwd_tile(xv, g, d):
    r = lax.rsqrt(jnp.mean(xv * xv, axis=-1, keepdims=True) + EPS)
    xh = xv * r
    dxh = d * g
    return r * (dxh - xh * jnp.mean(dxh * xh, axis=-1, keepdims=True)), jnp.sum(d * xh, axis=0, keepdims=True)


def _rms_bwd(x, g, dy, out_dtype, name="rms_bwd"):
    M, D = x.shape
    tm = _row_tile(M)

    def body(x_ref, g_ref, dy_ref, dx_ref, dg_ref):
        dx, dg = _rms_bwd_tile(x_ref[...], g_ref[...], dy_ref[...].astype(F32))
        dx_ref[...] = dx.astype(dx_ref.dtype)

        @pl.when(pl.program_id(0) == 0)
        def _():
            dg_ref[...] = jnp.zeros_like(dg_ref)

        dg_ref[...] += dg

    row = pl.BlockSpec((tm, D), lambda i: (i, 0))
    vec = pl.BlockSpec((1, D), lambda i: (0, 0))
    return _pc(body, name=name,
               out_shape=(jax.ShapeDtypeStruct((M, D), out_dtype), jax.ShapeDtypeStruct((1, D), F32)),
               grid=(M // tm,), in_specs=[row, vec, row], out_specs=(row, vec), compiler_params=_cp("arbitrary"))(x, g, dy)


def _loss_head(y, t, name="loss_head"):
    M, D = y.shape
    tm = _row_tile(M)

    def body(y_ref, t_ref, dy_ref, acc_ref):
        e = y_ref[...] - t_ref[...]
        dy_ref[...] = e * (1.0 / D)

        @pl.when(pl.program_id(0) == 0)
        def _():
            acc_ref[...] = jnp.zeros_like(acc_ref)

        acc_ref[...] += jnp.sum(e * e)

    row = pl.BlockSpec((tm, D), lambda i: (i, 0))
    return _pc(body, name=name,
               out_shape=(jax.ShapeDtypeStruct((M, D), F32), jax.ShapeDtypeStruct((8, LANES), F32)),
               grid=(M // tm,), in_specs=[row, row],
               out_specs=(row, pl.BlockSpec((8, LANES), lambda i: (0, 0))), compiler_params=_cp("arbitrary"))(y, t)


def _mm(a, w, out_dtype, nt=False, pre_g=None, swiglu=False, post=None, gu=None, pre_bwd=None, post_bwd=None, carry=None,
        tm=512, name="mm"):
    M, K = a.shape
    N = w.shape[0] if nt else w.shape[1]
    tm = _row_tile(M, tm)
    gain_grad = pre_bwd is not None or post_bwd is not None

    def body(*refs):
        refs = list(refs)
        a_ref, w_ref = refs.pop(0), refs.pop(0)
        pg_ref = refs.pop(0) if pre_g is not None else None
        g_ref, r_ref = (refs.pop(0), refs.pop(0)) if post is not None else (None, None)
        gu_ref = refs.pop(0) if gu is not None else None
        bwd_refs = [refs.pop(0) for _ in (pre_bwd or post_bwd or ())]
        lhs = a_ref[...]
        if pre_g is not None:
            lhs = (lhs * lax.rsqrt(jnp.mean(lhs * lhs, axis=-1, keepdims=True) + EPS) * pg_ref[...]).astype(BF16)
            refs.pop(0)[...] = lhs
        if pre_bwd is not None:
            lhs, dg = _rms_bwd_tile(bwd_refs[0][...], bwd_refs[1][...], lhs.astype(F32))
            lhs = lhs.astype(BF16)
            refs.pop(0)[...] = lhs
        if nt:
            acc = lax.dot_general(lhs, w_ref[...], NT_DIMS, preferred_element_type=F32)
        else:
            acc = jnp.dot(lhs, w_ref[...], preferred_element_type=F32)
        o_ref = refs.pop(0)
        if post_bwd is not None:
            dx, dg = _rms_bwd_tile(bwd_refs[0][...], bwd_refs[1][...], acc)
            o_ref[...] = bwd_refs[2][...] + dx
        elif gu is None:
            o_ref[...] = acc.astype(o_ref.dtype)
        else:
            gate = gu_ref[:, :N].astype(F32)
            sig = 1.0 / (1.0 + jnp.exp(-gate))
            o_ref[:, :N] = (acc * gu_ref[:, N:].astype(F32) * (sig * (1.0 + gate * (1.0 - sig)))).astype(o_ref.dtype)
            o_ref[:, N:] = (acc * (gate * sig)).astype(o_ref.dtype)
        if swiglu:
            gate = acc[:, : N // 2]
            refs.pop(0)[...] = (gate * (1.0 / (1.0 + jnp.exp(-gate))) * acc[:, N // 2:]).astype(BF16)
        if post is not None:
            y = acc * lax.rsqrt(jnp.mean(acc * acc, axis=-1, keepdims=True) + EPS) * g_ref[...]
            refs.pop(0)[...] = r_ref[...] + y
        if gain_grad:
            dg_ref = refs.pop(0)

            @pl.when(pl.program_id(0) == 0)
            def _():
                dg_ref[...] = jnp.zeros_like(dg_ref)

            dg_ref[...] += dg

    row = lambda n: pl.BlockSpec((tm, n), lambda i: (i, 0))
    vec = lambda n: pl.BlockSpec((1, n), lambda i: (0, 0))
    ins = [a, w]
    specs = [row(K), pl.BlockSpec(w.shape, lambda i: (0, 0), pipeline_mode=pl.Buffered(1))]
    outs, ospecs = [], []
    if pre_g is not None:
        ins, specs = ins + [pre_g], specs + [vec(K)]
    if pre_g is not None or pre_bwd is not None:
        outs, ospecs = outs + [jax.ShapeDtypeStruct((M, K), BF16)], ospecs + [row(K)]
    if post is not None:
        ins, specs = ins + list(post), specs + [vec(N), row(N)]
    if gu is not None:
        ins, specs = ins + [gu], specs + [row(2 * N)]
        outs, ospecs = outs + [jax.ShapeDtypeStruct((M, 2 * N), BF16)], ospecs + [row(2 * N)]
    else:
        outs, ospecs = outs + [jax.ShapeDtypeStruct((M, N), out_dtype)], ospecs + [row(N)]
    if pre_bwd is not None:
        ins, specs = ins + list(pre_bwd), specs + [row(K), vec(K)]
    if post_bwd is not None:
        ins, specs = ins + list(post_bwd), specs + [row(N), vec(N), row(N)]
    if swiglu:
        outs, ospecs = outs + [jax.ShapeDtypeStruct((M, N // 2), BF16)], ospecs + [row(N // 2)]
    if post is not None:
        outs, ospecs = outs + [jax.ShapeDtypeStruct((M, N), F32)], ospecs + [row(N)]
    if gain_grad:
        D = K if pre_bwd is not None else N
        outs, ospecs = outs + [jax.ShapeDtypeStruct((1, D), F32)], ospecs + [vec(D)]
    return _call(body, carry, ins, name=name, out_shape=outs, grid=(M // tm,), in_specs=specs, out_specs=ospecs,
                 sem=("arbitrary" if gain_grad else "parallel",))


def _mm_tn(a, b, carry=None, name="mm_tn"):
    S, M = a.shape
    N = b.shape[1]
    tm = M if M <= 1408 else M // 2
    tn = N if N <= 1408 else N // 4
    ts = _row_tile(S, 1024)

    def body(a_ref, b_ref, o_ref):
        @pl.when(pl.program_id(2) == 0)
        def _():
            o_ref[...] = jnp.zeros_like(o_ref)

        o_ref[...] += lax.dot_general(a_ref[...], b_ref[...], TN_DIMS, preferred_element_type=F32)

    res = _call(body, carry, [a, b], name=name, out_shape=[jax.ShapeDtypeStruct((M, N), F32)],
                grid=(M // tm, N // tn, S // ts),
                in_specs=[pl.BlockSpec((ts, tm), lambda i, j, s: (s, i)), pl.BlockSpec((ts, tn), lambda i, j, s: (s, j))],
                out_specs=[pl.BlockSpec((tm, tn), lambda i, j, s: (i, j))], sem=("parallel", "parallel", "arbitrary"))
    return res[0] if carry is None else res


def _rope_tables(S):
    pos = jnp.arange(S, dtype=jnp.int32)

    def table(p, n_dims, theta):
        inv = theta ** (-(jnp.arange(0, n_dims, 2, dtype=F32) / n_dims))
        ang = p.astype(F32)[:, None] * inv[None, :]
        return jnp.cos(ang), jnp.sin(ang)

    one = lambda n: jnp.ones((S, n), F32)
    zero = lambda n: jnp.zeros((S, n), F32)
    cp, sp = table(pos, ROPE_DIMS, ROPE_THETA)
    rest = HEAD_DIM - ROPE_DIMS
    part = (jnp.concatenate([cp, cp, one(rest)], 1), jnp.concatenate([zero(8), sp, zero(rest)], 1),
            jnp.concatenate([-sp, zero(8), zero(rest)], 1))
    cr, sr = table(pos // GRID_W, HEAD_DIM // 2, AXIAL_THETA)
    cc, sc = table(pos % GRID_W, HEAD_DIM // 2, AXIAL_THETA)
    axial = (jnp.concatenate([cr, cr, cc, cc], 1), jnp.concatenate([zero(16), sr, zero(16), sc], 1),
             jnp.concatenate([-sr, zero(16), -sc, zero(16)], 1))
    rep = LANES // HEAD_DIM
    return (tuple(jnp.tile(t, (1, rep)) for t in part), ROPE_DIMS // 2), (tuple(jnp.tile(t, (1, rep)) for t in axial), HEAD_DIM // 4)


def _seg_mats():
    col = jnp.arange(IN_W)[:, None] // HEAD_DIM
    e = (col == jnp.arange(LANES)[None, :]).astype(BF16)
    return e, e.T


def _qk_gain_row(qk_g):
    return jnp.concatenate([jnp.tile(qk_g[0], N_TOK_HEADS), jnp.tile(qk_g[1], N_KV_HEADS),
                            jnp.ones((IN_W - QK_W,), F32)])[None, :]


def _rope_cols(tabs, tm):
    col = lax.broadcasted_iota(jnp.int32, (tm, IN_W), 1)
    qk = col < QK_W
    c, s_lo, s_hi = (jnp.tile(t[...], (1, IN_W // LANES)) for t in tabs)
    return jnp.where(qk, c, 1.0), jnp.where(qk, s_lo, 0.0), jnp.where(qk, s_hi, 0.0), qk


def _seg_mean(v, e_ref, et_ref):
    def split_dot(t, m_ref):
        hi = t.astype(BF16)
        lo = (t - hi.astype(F32)).astype(BF16)
        return jnp.dot(hi, m_ref[...], preferred_element_type=F32) + jnp.dot(lo, m_ref[...], preferred_element_type=F32)

    return split_dot(split_dot(v, e_ref) * (1.0 / HEAD_DIM), et_ref)


def _headprep_fwd(proj, tabs, shift, qk_gain=None, kv_heads=False, name="headprep_fwd"):
    S = proj.shape[0]
    tm = _row_tile(S, 256)
    norm = qk_gain is not None

    def body(*refs):
        refs = list(refs)
        p_ref, c_ref, lo_ref, hi_ref = (refs.pop(0) for _ in range(4))
        g_ref, e_ref, et_ref = (refs.pop(0) for _ in range(3)) if norm else (None, None, None)
        o_ref = refs.pop(0)
        x = p_ref[...]
        c, s_lo, s_hi, qk = _rope_cols((c_ref, lo_ref, hi_ref), tm)
        if norm:
            r = lax.rsqrt(_seg_mean(x * x, e_ref, et_ref) + EPS)
            x = x * jnp.where(qk, r, 1.0) * g_ref[...]
        y = (x * c + pltpu.roll(x, shift, 1) * s_lo + pltpu.roll(x, IN_W - shift, 1) * s_hi).astype(o_ref.dtype)
        o_ref[...] = y
        if kv_heads:
            k_ref, v_ref = refs
            for h in range(N_KV_HEADS):
                k_ref[h] = y[:, Q_W + h * HEAD_DIM:Q_W + (h + 1) * HEAD_DIM]
                v_ref[h] = y[:, QK_W + h * HEAD_DIM:QK_W + (h + 1) * HEAD_DIM]

    row = pl.BlockSpec((tm, IN_W), lambda i: (i, 0))
    tab = pl.BlockSpec((tm, LANES), lambda i: (i, 0))
    ins = [proj, *tabs]
    specs = [row, tab, tab, tab]
    if norm:
        e, et = _seg_mats()
        ins += [qk_gain, e, et]
        specs += [pl.BlockSpec((1, IN_W), lambda i: (0, 0)), pl.BlockSpec((IN_W, LANES), lambda i: (0, 0)),
                  pl.BlockSpec((LANES, IN_W), lambda i: (0, 0))]
    out_shape, out_specs = [jax.ShapeDtypeStruct((S, IN_W), BF16)], [row]
    if kv_heads:
        out_shape += [jax.ShapeDtypeStruct((N_KV_HEADS, S, HEAD_DIM), BF16)] * 2
        out_specs += [pl.BlockSpec((N_KV_HEADS, tm, HEAD_DIM), lambda i: (0, i, 0))] * 2
    res = _pc(body, name=name, out_shape=tuple(out_shape), grid=(S // tm,),
              in_specs=specs, out_specs=tuple(out_specs), compiler_params=_cp("parallel"))(*ins)
    return res if kv_heads else res[0]


def _headprep_bwd(dpr, tabs, shift, proj=None, qk_gain=None, name="headprep_bwd"):
    parts = isinstance(dpr, (tuple, list))
    S = dpr[0].shape[0] if parts else dpr.shape[0]
    tm = _row_tile(S, 256)
    norm = qk_gain is not None

    def body(*refs):
        refs = list(refs)
        d_refs = [refs.pop(0) for _ in range(4 if parts else 1)]
        c_ref, lo_ref, hi_ref = (refs.pop(0) for _ in range(3))
        if norm:
            p_ref, g_ref, e_ref, et_ref, o_ref, dg_ref = refs
        else:
            (o_ref,) = refs
        if parts:
            dq_ref, dk_ref, dv_ref, dqm_ref = d_refs
            d = jnp.concatenate([dq_ref[...]] + [dk_ref[h] for h in range(N_KV_HEADS)]
                                + [dv_ref[h] for h in range(N_KV_HEADS)] + [dqm_ref[...]], axis=1).astype(F32)
        else:
            d = d_refs[0][...].astype(F32)
        c, s_lo, s_hi, qk = _rope_cols((c_ref, lo_ref, hi_ref), tm)
        dx = d * c + pltpu.roll(d * s_lo, IN_W - shift, 1) + pltpu.roll(d * s_hi, shift, 1)
        if norm:
            x = p_ref[...]
            r = lax.rsqrt(_seg_mean(x * x, e_ref, et_ref) + EPS)
            xh = x * r

            @pl.when(pl.program_id(0) == 0)
            def _():
                dg_ref[...] = jnp.zeros_like(dg_ref)

            dg_ref[...] += jnp.sum(jnp.where(qk, dx * xh, 0.0), axis=0, keepdims=True)
            dxh = dx * g_ref[...]
            dn = r * (dxh - xh * _seg_mean(dxh * xh, e_ref, et_ref))
            dx = jnp.where(qk, dn, dx)
        o_ref[...] = dx.astype(o_ref.dtype)

    row = pl.BlockSpec((tm, IN_W), lambda i: (i, 0))
    tab = pl.BlockSpec((tm, LANES), lambda i: (i, 0))
    vec = pl.BlockSpec((1, IN_W), lambda i: (0, 0))
    if parts:
        heads = pl.BlockSpec((N_KV_HEADS, tm, HEAD_DIM), lambda i: (0, i, 0))
        ins = [*dpr, *tabs]
        specs = [pl.BlockSpec((tm, Q_W), lambda i: (i, 0)), heads, heads, pl.BlockSpec((tm, QM_W), lambda i: (i, 0)), tab, tab, tab]
    else:
        ins = [dpr, *tabs]
        specs = [row, tab, tab, tab]
    out_shape = jax.ShapeDtypeStruct((S, IN_W), BF16)
    out_specs = row
    if norm:
        e, et = _seg_mats()
        ins += [proj, qk_gain, e, et]
        specs += [row, vec, pl.BlockSpec((IN_W, LANES), lambda i: (0, 0)), pl.BlockSpec((LANES, IN_W), lambda i: (0, 0))]
        out_shape = (out_shape, jax.ShapeDtypeStruct((1, IN_W), F32))
        out_specs = (row, vec)
    return _pc(body, name=name, out_shape=out_shape, grid=(S // tm,), in_specs=specs, out_specs=out_specs,
               compiler_params=_cp("arbitrary" if norm else "parallel"))(*ins)


CHAIN_ROWS_WIDE, CHAIN_ROWS_NARROW = 128, 256
CHAIN_NARROW_KEYS = 1024


def _skewed(n, stages):
    for t in range(n + len(stages) - 1):
        for s, stage in enumerate(stages):
            if 0 <= t - s < n:
                stage(t - s)


def _chain_slices(G, bq, keys):
    cr = min(CHAIN_ROWS_NARROW if keys <= CHAIN_NARROW_KEYS else CHAIN_ROWS_WIDE, bq)
    while bq % cr:
        cr //= 2
    per = bq // cr
    return [(c // per, slice((c % per) * cr, (c % per + 1) * cr), slice(c * cr, (c + 1) * cr)) for c in range(G * per)]


def _v_ones(v):
    return jnp.concatenate([v, jnp.ones(v.shape, v.dtype)], axis=1)


def _q_dims(q, k, heads, kv_per_head=False):
    if heads is None:
        return q.shape
    return (1 if kv_per_head else k.shape[0]), heads, q.shape[0], k.shape[2]


def _q_shape(heads, NB, G, L, HD):
    return (NB, G, L, HD) if heads is None else (L, NB * G * HD)


def _q_spec(heads, G, rows, HD, index):
    if heads is None:
        return pl.BlockSpec((1, G, rows, HD), lambda *ids: (index(*ids)[0], 0, index(*ids)[1], 0))
    return pl.BlockSpec((rows, G * HD), lambda *ids: index(*ids)[::-1])


def _q_at(heads, g, hr, HD):
    return (0, g, hr, slice(None)) if heads is None else (hr, slice(g * HD, (g + 1) * HD))


def _window(L, blk, radius):
    if radius is None:
        return L, None
    W = min(L, blk + 2 * radius)
    assert blk % radius == 0 and (L - W) % radius == 0
    return W, lambda n: radius * jnp.clip(n * (blk // radius) - 1, 0, (L - W) // radius)


def _win_specs(G, W, HD, start, with_g):
    E = pl.Element
    st = (lambda n: 0) if start is None else start
    if with_g:
        return pl.BlockSpec((E(1), E(G), E(W), E(HD)), lambda b, n: (b, 0, st(n), 0))
    return pl.BlockSpec((E(1), E(W), E(HD)), lambda b, n: (b, st(n), 0))


def _attn_delta(do, o, *, dlse=None, lse=None, sink=None, heads=None, name="attn_delta"):
    HD = HEAD_DIM
    (NB, G), L = (heads, do.shape[0]) if heads is not None else (do.shape[:2], do.shape[2])
    bl = _row_tile(L, 1024)

    def body(*refs):
        refs = list(refs)
        sink_ref = refs.pop(0) if sink is not None else None
        do_ref, o_ref = refs.pop(0), refs.pop(0)
        dlse_ref = refs.pop(0) if dlse is not None else None
        lse_ref = refs.pop(0) if sink is not None else None
        delta_ref = refs.pop(0)
        b = pl.program_id(0)
        if heads is None:
            delta = jnp.sum(do_ref[0].astype(F32) * o_ref[0].astype(F32), axis=-1, keepdims=True)
        else:
            prod = do_ref[...].astype(F32) * o_ref[...].astype(F32)
            delta = jnp.concatenate([jnp.sum(prod[:, g * HD:(g + 1) * HD], axis=-1, keepdims=True)[None] for g in range(G)])
        if dlse is not None:
            delta = delta - dlse_ref[0]
        delta_ref[0] = delta
        if sink is not None:
            ds_ref = refs.pop(0)

            @pl.when(pl.program_id(1) == 0)
            def _():
                ds_ref[...] = jnp.zeros_like(ds_ref)

            for g in range(G):
                ps = jnp.exp(sink_ref[b * G + g] - lse_ref[0, g]) * delta[g]
                ds_ref[0, g] -= jnp.sum(ps)

    blk = _q_spec(None if heads is None else G, G, bl, HD, lambda b, n: (b, n))
    col = pl.BlockSpec((1, G, bl, 1), lambda b, n: (b, 0, n, 0))
    ins, specs = [do, o], [blk, blk]
    if dlse is not None:
        ins, specs = ins + [dlse], specs + [col]
    out_shape = jax.ShapeDtypeStruct((NB, G, L, 1), F32)
    out_specs = col
    if sink is not None:
        ins, specs = [sink] + ins + [lse], [pl.BlockSpec(memory_space=pltpu.SMEM)] + specs + [col]
        out_shape = (out_shape, jax.ShapeDtypeStruct((NB, G, 1, LANES), F32))
        out_specs = (col, pl.BlockSpec((1, G, 1, LANES), lambda b, n: (b, 0, 0, 0)))
    return _pc(body, name=name, out_shape=out_shape, grid=(NB, L // bl), in_specs=specs, out_specs=out_specs,
               compiler_params=_cp("parallel", "arbitrary"))(*ins)


def _attn_fwd_full(q, k, v, *, bq, tk, heads=None, name="attn_fwd_full"):
    NB, G, L, HD = _q_dims(q, k, heads)
    Lk = k.shape[1]
    nq, nk = L // bq, Lk // tk
    rows = G * bq
    chains = _chain_slices(G, bq, tk)

    def body(q_ref, k_ref, v_ref, o_ref, lse_ref, m_sc, acc_sc, q_sc):
        j = pl.program_id(2)

        @pl.when(j == 0)
        def _():
            m_sc[...] = jnp.full_like(m_sc, NEG)
            acc_sc[...] = jnp.zeros_like(acc_sc)
            for g, hr, sl in chains:
                q_sc[sl] = q_ref[_q_at(heads, g, hr, HD)] * ATTN_SCALE

        kk = k_ref[0]
        vv = _v_ones(v_ref[0])
        st = [dict() for _ in chains]

        def scores(c):
            st[c]["s"] = lax.dot_general(q_sc[chains[c][2]], kk, NT_DIMS, preferred_element_type=F32)

        def softmax(c):
            sl = chains[c][2]
            m_prev = m_sc[sl]
            m_new = jnp.maximum(m_prev, jnp.max(st[c]["s"], axis=1, keepdims=True))
            st[c]["p"] = jnp.exp(st[c].pop("s") - m_new).astype(BF16)
            st[c]["alpha"] = jnp.exp(m_prev - m_new)
            m_sc[sl] = m_new

        def values(c):
            sl = chains[c][2]
            acc_sc[sl] = st[c].pop("alpha") * acc_sc[sl] + jnp.dot(st[c].pop("p"), vv, preferred_element_type=F32)

        _skewed(len(chains), (scores, softmax, values))

        @pl.when(j == nk - 1)
        def _():
            for g, hr, sl in chains:
                acc = acc_sc[sl]
                l = acc[:, HD:HD + 1]
                o_ref[_q_at(heads, g, hr, HD)] = (acc[:, :HD] / l).astype(o_ref.dtype)
                lse_ref[0, g, hr, :] = m_sc[sl] + jnp.log(l)

    qspec = _q_spec(heads, G, bq, HD, lambda b, n, j: (b, n))
    kspec = pl.BlockSpec((1, tk, HD), lambda b, n, j: (b, j, 0))
    return _pc(body, name=name,
               out_shape=(jax.ShapeDtypeStruct(_q_shape(heads, NB, G, L, HD), BF16), jax.ShapeDtypeStruct((NB, G, L, 1), F32)),
               grid=(NB, nq, nk), in_specs=[qspec, kspec, kspec],
               out_specs=(qspec, pl.BlockSpec((1, G, bq, 1), lambda b, n, j: (b, 0, n, 0))),
               scratch_shapes=[pltpu.VMEM((rows, 1), F32), pltpu.VMEM((rows, 2 * HD), F32), pltpu.VMEM((rows, HD), BF16)],
               compiler_params=_cp("parallel", "parallel", "arbitrary"))(q, k, v)


def _attn_bwd_full(q, k, v, do, lse, o, *, bq, tk, heads=None, kv_per_head=False, name="attn_bwd_full"):
    NB, G, L, HD = _q_dims(q, k, heads, kv_per_head)
    Lk = k.shape[1]
    nq, nk = L // bq, Lk // tk
    chains = _chain_slices(G, bq, tk)
    KH = G if kv_per_head else 1
    assert not kv_per_head or nk == 1

    def body(q_ref, k_ref, v_ref, do_ref, lse_ref, o_ref, dqp_ref, dk_ref, dv_ref, dk_sc, dv_sc):
        n = pl.program_id(2)

        @pl.when(n == 0)
        def _():
            dk_sc[...] = jnp.zeros_like(dk_sc)
            dv_sc[...] = jnp.zeros_like(dv_sc)

        kks = [k_ref[h] for h in range(KH)]
        vvs = [v_ref[h] for h in range(KH)]
        st = [dict() for _ in chains]

        def scores(c):
            g, hr, _ = chains[c]
            at = _q_at(heads, g, hr, HD)
            st[c]["q"] = q_ref[at] * ATTN_SCALE
            st[c]["do"] = do_ref[at]
            st[c]["s"] = lax.dot_general(st[c]["q"], kks[g % KH], NT_DIMS, preferred_element_type=F32)
            st[c]["dp"] = lax.dot_general(st[c]["do"], vvs[g % KH], NT_DIMS, preferred_element_type=F32)
            st[c]["delta"] = jnp.sum(st[c]["do"].astype(F32) * o_ref[at].astype(F32), axis=-1, keepdims=True)

        def softmax(c):
            g, hr, _ = chains[c]
            p = jnp.exp(st[c].pop("s") - lse_ref[0, g, hr, :])
            st[c]["ds"] = (p * (st[c].pop("dp") - st[c].pop("delta"))).astype(BF16)
            st[c]["p"] = p.astype(BF16)

        def grads(c):
            g, hr, _ = chains[c]
            ds = st[c].pop("ds")
            dv_sc[g % KH] += lax.dot_general(st[c].pop("p"), st[c].pop("do"), TN_DIMS, preferred_element_type=F32)
            dk_sc[g % KH] += lax.dot_general(ds, st[c].pop("q"), TN_DIMS, preferred_element_type=F32)
            dqp_ref[(0,) + _q_at(heads, g, hr, HD)] = jnp.dot(ds, kks[g % KH], preferred_element_type=F32) * ATTN_SCALE

        _skewed(len(chains), (scores, softmax, grads))

        @pl.when(n == nq - 1)
        def _():
            dk_ref[...] = dk_sc[...].astype(dk_ref.dtype)
            dv_ref[...] = dv_sc[...].astype(dv_ref.dtype)

    qspec = _q_spec(heads, G, bq, HD, lambda b, m, n: (b, n))
    cspec = pl.BlockSpec((1, G, bq, 1), lambda b, m, n: (b, 0, n, 0))
    kspec = pl.BlockSpec((KH, tk, HD), lambda b, m, n: (b, m, 0))
    kv_shape = jax.ShapeDtypeStruct((NB * KH, Lk, HD), BF16)
    if heads is None:
        pspec = pl.BlockSpec((1, 1, G, bq, HD), lambda b, m, n: (m, b, 0, n, 0))
    else:
        pspec = pl.BlockSpec((1, bq, G * HD), lambda b, m, n: (m, n, b))
    dqp, dk, dv = _pc(body, name=name,
                      out_shape=(jax.ShapeDtypeStruct((nk,) + _q_shape(heads, NB, G, L, HD), F32), kv_shape, kv_shape),
                      grid=(NB, nk, nq), in_specs=[qspec, kspec, kspec, qspec, cspec, qspec],
                      out_specs=(pspec, kspec, kspec),
                      scratch_shapes=[pltpu.VMEM((KH, tk, HD), F32), pltpu.VMEM((KH, tk, HD), F32)],
                      compiler_params=_cp("parallel", "parallel", "arbitrary"))(q, k, v, do, lse, o)
    if nk == 1:
        return dqp[0].astype(BF16), dk, dv
    bl = _row_tile(L, 512)

    def sum_body(p_ref, o_ref):
        acc = p_ref[0]
        for j in range(1, nk):
            acc = acc + p_ref[j]
        o_ref[...] = acc.astype(o_ref.dtype)

    if heads is None:
        pspec = pl.BlockSpec((nk, 1, G, bl, HD), lambda b, n: (0, b, 0, n, 0))
    else:
        pspec = pl.BlockSpec((nk, bl, G * HD), lambda b, n: (0, n, b))
    dq = _pc(sum_body, name=name + "_sum", out_shape=jax.ShapeDtypeStruct(_q_shape(heads, NB, G, L, HD), BF16),
             grid=(NB, L // bl), in_specs=[pspec], out_specs=_q_spec(heads, G, bl, HD, lambda b, n: (b, n)),
             compiler_params=_cp("parallel", "parallel"))(dqp)
    return dq, dk, dv


def _attn_fwd_win(q, k, v, *, radius, sink=None, bq, heads=None, kv_per_head=False, carry=None, name="attn_fwd_win"):
    NB, G, L, HD = _q_dims(q, k, heads, kv_per_head)
    W, start = _window(k.shape[1], bq, radius)
    chains = _chain_slices(G, bq, W)

    def body(*refs):
        if sink is not None:
            sink_ref, *refs = refs
        q_ref, k_ref, v_ref, o_ref, lse_ref = refs
        b, n = pl.program_id(0), pl.program_id(1)
        kks = [k_ref[g] for g in range(G)] if kv_per_head else [k_ref[0]] * G
        vvs = [_v_ones(v_ref[g]) for g in range(G)] if kv_per_head else [_v_ones(v_ref[0])] * G
        st = [dict() for _ in chains]

        def scores(c):
            g, hr, _ = chains[c]
            s = lax.dot_general(q_ref[_q_at(heads, g, hr, HD)] * ATTN_SCALE, kks[g], NT_DIMS, preferred_element_type=F32)
            if radius is not None:
                qpos = n * bq + hr.start + lax.broadcasted_iota(jnp.int32, (hr.stop - hr.start, 1), 0)
                kpos = start(n) + lax.broadcasted_iota(jnp.int32, (1, W), 1)
                s = jnp.where(jnp.abs(qpos - kpos) <= radius, s, NEG)
            st[c]["s"] = s

        def softmax(c):
            g = chains[c][0]
            m = jnp.max(st[c]["s"], axis=1, keepdims=True)
            if sink is not None:
                m = jnp.maximum(m, sink_ref[b * G + g])
            st[c]["p"] = jnp.exp(st[c].pop("s") - m).astype(BF16)
            st[c]["m"] = m

        def values(c):
            g, hr, _ = chains[c]
            acc = jnp.dot(st[c].pop("p"), vvs[g], preferred_element_type=F32)
            m = st[c].pop("m")
            l = acc[:, HD:HD + 1]
            if sink is not None:
                l = l + jnp.exp(sink_ref[b * G + g] - m)
            o_ref[_q_at(heads, g, hr, HD)] = (acc[:, :HD] / l).astype(o_ref.dtype)
            lse_ref[0, g, hr, :] = m + jnp.log(l)

        _skewed(len(chains), (scores, softmax, values))

    qspec = _q_spec(heads, G, bq, HD, lambda b, n: (b, n))
    kspec = pl.BlockSpec((G, W, HD), lambda b, n: (0, 0, 0)) if kv_per_head else _win_specs(G, W, HD, start, False)
    ins, specs = [q, k, v], [qspec, kspec, kspec]
    if sink is not None:
        ins, specs = [sink] + ins, [pl.BlockSpec(memory_space=pltpu.SMEM)] + specs
    return _call(body, carry, ins, name=name,
                 out_shape=(jax.ShapeDtypeStruct(_q_shape(heads, NB, G, L, HD), BF16), jax.ShapeDtypeStruct((NB, G, L, 1), F32)),
                 grid=(NB, L // bq), in_specs=specs,
                 out_specs=(qspec, pl.BlockSpec((1, G, bq, 1), lambda b, n: (b, 0, n, 0))), sem=("parallel", "parallel"))


def _attn_dq_win(q, k, v, do, lse, delta, *, radius, bq, heads=None, name="attn_dq_win"):
    NB, G, L, HD = _q_dims(q, k, heads)
    W, start = _window(L, bq, radius)
    chains = _chain_slices(G, bq, W)

    def body(q_ref, k_ref, v_ref, do_ref, lse_ref, dl_ref, dq_ref):
        n = pl.program_id(1)
        kk, vv = k_ref[0], v_ref[0]
        kpos = start(n) + lax.broadcasted_iota(jnp.int32, (1, W), 1)
        st = [dict() for _ in chains]

        def scores(c):
            g, hr, _ = chains[c]
            at = _q_at(heads, g, hr, HD)
            st[c]["s"] = lax.dot_general(q_ref[at] * ATTN_SCALE, kk, NT_DIMS, preferred_element_type=F32)
            st[c]["dp"] = lax.dot_general(do_ref[at], vv, NT_DIMS, preferred_element_type=F32)

        def softmax(c):
            g, hr, _ = chains[c]
            qpos = n * bq + hr.start + lax.broadcasted_iota(jnp.int32, (hr.stop - hr.start, 1), 0)
            p = jnp.where(jnp.abs(qpos - kpos) <= radius, jnp.exp(st[c].pop("s") - lse_ref[0, g, hr, :]), 0.0)
            st[c]["ds"] = (p * (st[c].pop("dp") - dl_ref[0, g, hr, :])).astype(BF16)

        def grads(c):
            g, hr, _ = chains[c]
            dq = jnp.dot(st[c].pop("ds"), kk, preferred_element_type=F32) * ATTN_SCALE
            dq_ref[_q_at(heads, g, hr, HD)] = dq.astype(dq_ref.dtype)

        _skewed(len(chains), (scores, softmax, grads))

    qspec = _q_spec(heads, G, bq, HD, lambda b, n: (b, n))
    cspec = pl.BlockSpec((1, G, bq, 1), lambda b, n: (b, 0, n, 0))
    kspec = _win_specs(G, W, HD, start, False)
    return _pc(body, name=name, out_shape=jax.ShapeDtypeStruct(_q_shape(heads, NB, G, L, HD), BF16), grid=(NB, L // bq),
               in_specs=[qspec, kspec, kspec, qspec, cspec, cspec], out_specs=qspec,
               compiler_params=_cp("parallel", "parallel"))(q, k, v, do, lse, delta)


def _attn_dkv_win(q, k, v, do, lse, delta, *, radius, bk, heads=None, carry=None, name="attn_dkv_win"):
    NB, G, L, HD = _q_dims(q, k, heads)
    W, start = _window(L, bk, radius)
    chains = _chain_slices(G, W, bk)

    def body(q_ref, k_ref, v_ref, do_ref, lse_ref, dl_ref, dk_ref, dv_ref):
        m = pl.program_id(1)
        kk, vv = k_ref[0], v_ref[0]
        kpos = m * bk + lax.broadcasted_iota(jnp.int32, (1, bk), 1)
        st = [dict() for _ in chains]
        out = dict(dk=jnp.zeros((bk, HD), F32), dv=jnp.zeros((bk, HD), F32))

        def scores(c):
            g, hr, _ = chains[c]
            at = _q_at(heads, g, hr, HD)
            st[c]["q"] = q_ref[at] * ATTN_SCALE
            st[c]["do"] = do_ref[at]
            st[c]["s"] = lax.dot_general(st[c]["q"], kk, NT_DIMS, preferred_element_type=F32)
            st[c]["dp"] = lax.dot_general(st[c]["do"], vv, NT_DIMS, preferred_element_type=F32)

        def softmax(c):
            g, hr, _ = chains[c]
            qpos = start(m) + hr.start + lax.broadcasted_iota(jnp.int32, (hr.stop - hr.start, 1), 0)
            p = jnp.where(jnp.abs(qpos - kpos) <= radius, jnp.exp(st[c].pop("s") - lse_ref[0, g, hr, :]), 0.0)
            st[c]["ds"] = (p * (st[c].pop("dp") - dl_ref[0, g, hr, :])).astype(BF16)
            st[c]["p"] = p.astype(BF16)

        def grads(c):
            out["dv"] = out["dv"] + lax.dot_general(st[c].pop("p"), st[c].pop("do"), TN_DIMS, preferred_element_type=F32)
            out["dk"] = out["dk"] + lax.dot_general(st[c].pop("ds"), st[c].pop("q"), TN_DIMS, preferred_element_type=F32)

        _skewed(len(chains), (scores, softmax, grads))
        dk_ref[0] = out["dk"].astype(dk_ref.dtype)
        dv_ref[0] = out["dv"].astype(dv_ref.dtype)

    if heads is None:
        qspec = _win_specs(G, W, HD, start, True)
    else:
        qspec = pl.BlockSpec((pl.Element(W), pl.Element(G * HD)), lambda b, m: (start(m), b * G * HD))
    cspec = _win_specs(G, W, 1, start, True)
    kspec = pl.BlockSpec((1, bk, HD), lambda b, m: (b, m, 0))
    kv_shape = jax.ShapeDtypeStruct((NB, L, HD), BF16)
    return _call(body, carry, [q, k, v, do, lse, delta], name=name, out_shape=(kv_shape, kv_shape), grid=(NB, L // bk),
                 in_specs=[qspec, kspec, kspec, qspec, cspec, cspec], out_specs=(kspec, kspec), sem=("parallel", "parallel"))


def _attn(q, k, v, *, radius=None, sink=None, bq, tk=None, heads=None, kv_per_head=False, carry=None, tag):
    if radius is None and tk < k.shape[1]:
        return _attn_fwd_full(q, k, v, bq=bq, tk=tk, heads=heads, name=f"attn_fwd_{tag}")
    return _attn_fwd_win(q, k, v, radius=radius, sink=sink, bq=bq, heads=heads, kv_per_head=kv_per_head, carry=carry,
                         name=f"attn_fwd_{tag}")


def _attn_bwd(q, k, v, o, lse, do, *, radius=None, sink=None, dlse=None, bq, tk=None, heads=None, kv_per_head=False,
              carry=None, tag):
    if radius is None:
        assert carry is None
        return (*_attn_bwd_full(q, k, v, do, lse, o, bq=bq, tk=tk, heads=heads, kv_per_head=kv_per_head,
                                name=f"attn_bwd_{tag}"), None, None)
    nbg = None if heads is None else (k.shape[0], heads)
    if sink is not None:
        delta, ds = _attn_delta(do, o, lse=lse, sink=sink, heads=nbg, name=f"attn_delta_{tag}")
        dsink = ds[:, :, 0, 0].reshape(-1)
    else:
        delta, dsink = _attn_delta(do, o, dlse=dlse, heads=nbg, name=f"attn_delta_{tag}"), None
    dq = _attn_dq_win(q, k, v, do, lse, delta, radius=radius, bq=bq, heads=heads, name=f"attn_dq_{tag}")
    dk, dv, *arrived = _attn_dkv_win(q, k, v, do, lse, delta, radius=radius, bk=bq, heads=heads, carry=carry,
                                     name=f"attn_dkv_{tag}")
    return dq, dk, dv, dsink, (arrived[0] if arrived else None)


def _combine_fwd(o, lse, name="combine_fwd"):
    H, S, HD = o.shape
    tm = _row_tile(S, 512)

    def body(o_ref, lse_ref, t_ref):
        for g in range(GQA_GROUP):
            hs = [kv * GQA_GROUP + g for kv in range(N_KV_HEADS)]
            ls = [lse_ref[h] for h in hs]
            mx = functools.reduce(jnp.maximum, ls)
            es = [jnp.exp(l - mx) for l in ls]
            den = functools.reduce(jnp.add, es)
            for h, e in zip(hs, es):
                t_ref[h] = (o_ref[h].astype(F32) * (e / den)).astype(t_ref.dtype)

    blk = pl.BlockSpec((H, tm, HD), lambda i: (0, i, 0))
    col = pl.BlockSpec((H, tm, 1), lambda i: (0, i, 0))
    return _pc(body, name=name, out_shape=jax.ShapeDtypeStruct((H, S, HD), BF16), grid=(S // tm,),
               in_specs=[blk, col], out_specs=blk, compiler_params=_cp("parallel"))(o, lse)


def _combine_bwd(dt, o, lse, name="combine_bwd"):
    H, S, HD = o.shape
    tm = _row_tile(S, 512)

    def body(dt_ref, o_ref, lse_ref, do_ref, dlse_ref):
        for g in range(GQA_GROUP):
            hs = [kv * GQA_GROUP + g for kv in range(N_KV_HEADS)]
            ls = [lse_ref[h] for h in hs]
            mx = functools.reduce(jnp.maximum, ls)
            es = [jnp.exp(l - mx) for l in ls]
            den = functools.reduce(jnp.add, es)
            al = [e / den for e in es]
            dts = [dt_ref[h].astype(F32) for h in hs]
            da = [jnp.sum(d * o_ref[h].astype(F32), axis=-1, keepdims=True) for h, d in zip(hs, dts)]
            dot = functools.reduce(jnp.add, [a * d for a, d in zip(al, da)])
            for h, a, d, dd in zip(hs, al, da, dts):
                do_ref[h] = (dd * a).astype(do_ref.dtype)
                dlse_ref[h] = a * (d - dot)

    blk = pl.BlockSpec((H, tm, HD), lambda i: (0, i, 0))
    col = pl.BlockSpec((H, tm, 1), lambda i: (0, i, 0))
    return _pc(body, name=name,
               out_shape=(jax.ShapeDtypeStruct((H, S, HD), BF16), jax.ShapeDtypeStruct((H, S, 1), F32)),
               grid=(S // tm,), in_specs=[blk, blk, col], out_specs=(blk, col), compiler_params=_cp("parallel"))(dt, o, lse)


def _position():
    x, y, c = lax.axis_index("x"), lax.axis_index("y"), lax.axis_index("c")
    return x, y, c


def _peer(pos, k):
    x, y, c = pos
    return (1 - x if k & 4 else x, 1 - y if k & 2 else y, 1 - c if k & 1 else c)


def _linear(p):
    return 4 * p[0] + 2 * p[1] + p[2]


def _exchange_steps(s_ref, r_ref, send_sems, recv_sems, local_sem, gather):
    pos = _position()
    me = _linear(pos)
    own = pltpu.make_async_copy(s_ref if gather else s_ref.at[me], r_ref.at[me], local_sem)
    peers = range(1, N_DEV)

    def sems(k):
        return dict(send_sem=send_sems.at[k - 1], recv_sem=recv_sems.at[k - 1], device_id=_peer(pos, k), device_id_type=MESH)

    def send(k):
        src = s_ref if gather else s_ref.at[_linear(_peer(pos, k))]
        return pltpu.make_async_remote_copy(src_ref=src, dst_ref=r_ref.at[me], **sems(k))

    def arrival(k):
        slot = r_ref.at[_linear(_peer(pos, k))]
        return pltpu.make_async_remote_copy(src_ref=slot, dst_ref=slot, **sems(k))

    def start():
        own.start()
        for k in peers:
            send(k).start()

    def wait():
        for k in peers:
            arrival(k).wait_recv()
        for k in peers:
            send(k).wait_send()
        own.wait()

    return start, wait


EXCHANGE_SEMS = [pltpu.SemaphoreType.DMA((N_DEV - 1,)), pltpu.SemaphoreType.DMA((N_DEV - 1,)), pltpu.SemaphoreType.DMA]


def _exchange(buf, gather, name):
    def body(s_ref, r_ref, *sems):
        start, wait = _exchange_steps(s_ref, r_ref, *sems, gather)
        start()
        wait()

    hbm = pl.BlockSpec(memory_space=pltpu.HBM)
    out_shape = ((N_DEV,) + buf.shape) if gather else buf.shape
    return _pc(body, name=name, out_shape=jax.ShapeDtypeStruct(out_shape, buf.dtype), in_specs=[hbm], out_specs=hbm,
               scratch_shapes=list(EXCHANGE_SEMS))(buf)


def _call(body, carry, ins, *, name, out_shape, grid, in_specs, out_specs, scratch_shapes=(), sem):
    if carry is None:
        return _pc(body, name=name, out_shape=tuple(out_shape), grid=grid, in_specs=list(in_specs),
                   out_specs=tuple(out_specs), scratch_shapes=list(scratch_shapes), compiler_params=_cp(*sem))(*ins)
    buf, gather = carry
    n_in, n_out, n_sc = len(ins), len(out_shape), len(scratch_shapes)

    def wrapped(*refs):
        in_refs, buf_ref = refs[:n_in], refs[n_in]
        out_refs, recv_ref = refs[n_in + 1:n_in + 1 + n_out], refs[n_in + 1 + n_out]
        rest = refs[n_in + 2 + n_out:]
        first = functools.reduce(jnp.logical_and, [pl.program_id(a) == 0 for a in range(len(grid))])
        last = functools.reduce(jnp.logical_and, [pl.program_id(a) == grid[a] - 1 for a in range(len(grid))])

        @pl.when(first)
        def _():
            _exchange_steps(buf_ref, recv_ref, *rest[n_sc:], gather)[0]()

        body(*in_refs, *out_refs, *rest[:n_sc])

        @pl.when(last)
        def _():
            _exchange_steps(buf_ref, recv_ref, *rest[n_sc:], gather)[1]()

    hbm = pl.BlockSpec(memory_space=pltpu.HBM)
    recv_shape = ((N_DEV,) + buf.shape) if gather else buf.shape
    return _pc(wrapped, name=name, out_shape=(*out_shape, jax.ShapeDtypeStruct(recv_shape, buf.dtype)), grid=grid,
               in_specs=[*in_specs, hbm], out_specs=(*out_specs, hbm), scratch_shapes=[*scratch_shapes, *EXCHANGE_SEMS],
               compiler_params=_cp(*(("arbitrary",) * len(grid))))(*ins, buf)


def _reduce_adamw(recv, w, m, v, name):
    _, R, C = recv.shape
    tr = _row_tile(R, 512)

    def body(r_ref, w_ref, m_ref, v_ref, g_ref, d_ref, nm_ref, nv_ref):
        g = r_ref[0].astype(F32)
        for j in range(1, N_DEV):
            g = g + r_ref[j].astype(F32)
        g_ref[...] = g
        nm = ADAM_B1 * m_ref[...] + (1.0 - ADAM_B1) * g
        nv = ADAM_B2 * v_ref[...] + (1.0 - ADAM_B2) * jnp.square(g)
        m_hat = nm / (1.0 - ADAM_B1 ** ADAM_STEP)
        v_hat = nv / (1.0 - ADAM_B2 ** ADAM_STEP)
        d_ref[...] = -ADAM_LR * (m_hat / (jnp.sqrt(v_hat) + ADAM_EPS) + ADAM_WD * w_ref[...])
        nm_ref[...] = nm
        nv_ref[...] = nv

    row = pl.BlockSpec((tr, C), lambda i: (i, 0))
    out = jax.ShapeDtypeStruct((R, C), F32)
    return _pc(body, name=name, out_shape=(out, out, out, out), grid=(R // tr,),
               in_specs=[pl.BlockSpec((N_DEV, tr, C), lambda i: (0, i, 0)), row, row, row],
               out_specs=(row, row, row, row), compiler_params=_cp("parallel"))(recv, w, m, v)


BIG = (("w_in", 2), ("w_mem_kv", 1), ("w_o", 1), ("w_gate_up", 2), ("w_down", 1))
SMALL = ("mem_norm_g", "g_mix_pre", "g_mix_post", "attn_sink", "qk_norm_g", "g_ffn_pre", "g_ffn_post")
SMALL_W = 1024
FIRST, REST = BIG[:1], BIG[1:]


def _pack_local(shards, dtype):
    return jnp.concatenate([s.astype(dtype).reshape(-1, LANES) for s in shards], axis=0)


def _unpack_local(flat, shapes):
    out, r = [], 0
    for shp in shapes:
        n = shp[0] * shp[1] * shp[2] // LANES
        out.append(flat[r:r + n].reshape(shp))
        r += n
    return out


def _unpack_gathered(g, shapes, names=BIG):
    out, r = [], 0
    for (name, dim), shp in zip(names, shapes):
        n = shp[0] * shp[1] * shp[2] // LANES
        t = g[:, r:r + n].reshape((N_DEV,) + tuple(shp))
        if dim == 2:
            t = t.transpose(1, 2, 0, 3).reshape(shp[0], shp[1], N_DEV * shp[2])
        else:
            t = t.transpose(1, 0, 2, 3).reshape(shp[0], N_DEV * shp[1], shp[2])
        out.append(t)
        r += n
    return out


def _pack_for_scatter(full, shapes, dtype, names=BIG):
    parts = []
    for (name, dim), shp, t in zip(names, shapes, full):
        if dim == 2:
            t = t.reshape(shp[0], shp[1], N_DEV, shp[2]).transpose(2, 0, 1, 3)
        else:
            t = t.reshape(shp[0], N_DEV, shp[1], shp[2]).transpose(1, 0, 2, 3)
        parts.append(t.astype(dtype).reshape(N_DEV, -1, LANES))
    return jnp.concatenate(parts, axis=1)


def _pack_small(arrs):
    flat = jnp.concatenate([a.reshape(-1) for a in arrs])
    pad = (-flat.shape[0]) % (8 * SMALL_W)
    return jnp.pad(flat, (0, pad)).reshape(-1, SMALL_W)


def _unpack_small(flat, shapes):
    flat = flat.reshape(-1)
    out, r = [], 0
    for shp in shapes:
        n = 1
        for d in shp:
            n *= d
        out.append(flat[r:r + n].reshape(shp))
        r += n
    return out


def _heads(t, nb, g):
    S = t.shape[0]
    return t.reshape(S, nb, g, HEAD_DIM).transpose(1, 2, 0, 3)


def _unheads(t):
    nb, g, S, hd = t.shape
    return t.transpose(2, 0, 1, 3).reshape(S, nb * g * hd)


def _dilate(t, dil):
    S = t.shape[0]
    g = t.shape[1] // HEAD_DIM
    return t.reshape(S // dil, dil, g, HEAD_DIM).transpose(1, 2, 0, 3)


def _undilate(t):
    dil, g, L, w = t.shape
    return t.transpose(1, 2, 0, 3).reshape(g, L * dil, w)


FULL_BQ_FWD, FULL_TK_FWD = 512, 4096
FULL_BQ_BWD, FULL_TK_BWD = 1024, 2048


def _mixer_fwd(kind, pr, kv, sink, li, carry=None):
    S = pr.shape[0]
    if kind == 0:
        tok, lse, *arrived = _attn(pr, *kv, radius=A_RADIUS, sink=sink, bq=min(256, S), heads=GQA_GROUP, carry=carry,
                                   tag=f"a{li}")
        return tok, lse, (arrived[0] if arrived else None)
    assert carry is None
    if kind == 1:
        tok, lse = _attn(pr, *kv, bq=min(FULL_BQ_FWD, S), tk=min(FULL_TK_FWD, S), heads=GQA_GROUP, tag=f"b{li}")
        return tok, lse, None
    saved, outs, lses = [], [], []
    for g, (window, dil) in enumerate(C_GROUPS):
        q = _dilate(pr[:, g * GQA_GROUP * HEAD_DIM:(g + 1) * GQA_GROUP * HEAD_DIM], dil)
        k = _dilate(pr[:, Q_W + g * HEAD_DIM:Q_W + (g + 1) * HEAD_DIM], dil)[:, 0]
        v = _dilate(pr[:, QK_W + g * HEAD_DIM:QK_W + (g + 1) * HEAD_DIM], dil)[:, 0]
        o, lse = _attn(q, k, v, radius=window // (2 * dil), bq=min(256, S // dil), tag=f"c{li}g{g}")
        saved.append((q, k, v, o, lse))
        outs.append(_undilate(o))
        lses.append(_undilate(lse))
    o_all, lse_all = jnp.concatenate(outs, 0), jnp.concatenate(lses, 0)
    tok = _combine_fwd(o_all, lse_all, name=f"combine_fwd_{li}")
    return tok.transpose(1, 0, 2).reshape(S, Q_W), (saved, o_all, lse_all), None


def _mixer_bwd(kind, dcat, pr, kv, cat, saved, sink, li, carry=None):
    S = dcat.shape[0]
    if kind == 0:
        return _attn_bwd(pr, *kv, cat, saved, dcat, radius=A_RADIUS, sink=sink, bq=min(256, S), heads=GQA_GROUP, carry=carry,
                         tag=f"a{li}")
    assert carry is None
    if kind == 1:
        return _attn_bwd(pr, *kv, cat, saved, dcat, bq=min(FULL_BQ_BWD, S), tk=min(FULL_TK_BWD, S), heads=GQA_GROUP,
                         tag=f"b{li}")
    per_group, o_all, lse_all = saved
    dt = dcat[:, :Q_W].reshape(S, N_TOK_HEADS, HEAD_DIM).transpose(1, 0, 2)
    do_all, dlse_all = _combine_bwd(dt, o_all, lse_all, name=f"combine_bwd_{li}")
    dqs, dks, dvs = [], [], []
    for g, (window, dil) in enumerate(C_GROUPS):
        q, k, v, o, lse = per_group[g]
        L = S // dil
        hs = slice(g * GQA_GROUP, (g + 1) * GQA_GROUP)
        do = do_all[hs].reshape(GQA_GROUP, L, dil, HEAD_DIM).transpose(2, 0, 1, 3)
        dlse = dlse_all[hs].reshape(GQA_GROUP, L, dil, 1).transpose(2, 0, 1, 3)
        dq, dk, dv, _, _ = _attn_bwd(q, k, v, o, lse, do, radius=window // (2 * dil), dlse=dlse, bq=min(256, L),
                                     tag=f"c{li}g{g}")
        dqs.append(dq.transpose(2, 0, 1, 3).reshape(S, GQA_GROUP * HEAD_DIM))
        dks.append(dk.transpose(1, 0, 2).reshape(S, HEAD_DIM))
        dvs.append(dv.transpose(1, 0, 2).reshape(S, HEAD_DIM))
    return jnp.concatenate(dqs, 1), jnp.concatenate(dks, 1), jnp.concatenate(dvs, 1), None, None


def kernel(x, mem, mem_norm_g, w_in, w_mem_kv, w_o, g_mix_pre, g_mix_post, attn_sink, qk_norm_g, w_gate_up, w_down, g_ffn_pre, g_ffn_post, loss_target, m_mem_norm_g, m_w_in, m_w_mem_kv, m_w_o, m_g_mix_pre, m_g_mix_post, m_attn_sink, m_qk_norm_g, m_w_gate_up, m_w_down, m_g_ffn_pre, m_g_ffn_post, v_mem_norm_g, v_w_in, v_w_mem_kv, v_w_o, v_g_mix_pre, v_g_mix_post, v_attn_sink, v_qk_norm_g, v_w_gate_up, v_w_down, v_g_ffn_pre, v_g_ffn_post):
    given = dict(locals())
    depth = w_in.shape[0]
    S, D = x.shape[1], x.shape[2]
    def shapes_of(names):
        return [(1,) + tuple(given[n].shape[1:]) for n, _ in names]

    def layer_pack(pre, l, dtype, names=BIG):
        return _pack_local([given[pre + n][l:l + 1] for n, _ in names], dtype)

    def layer_weights(gathered, names=BIG):
        return [t[0] for t in _unpack_gathered(gathered, shapes_of(names), names)]

    W = [None] * depth
    W[0] = layer_weights(_exchange(layer_pack("", 0, BF16, FIRST), True, "gather_w0_in"), FIRST)

    tabs = _rope_tables(S)
    mem_n = _rms_fwd(mem[0], mem_norm_g[None], BF16, name="rms_mem")

    saved = []
    xc = x[0]
    for i in range(depth):
        kind = i % N_MIXERS
        (tab, shift) = tabs[1] if kind == 1 else tabs[0]
        sink = attn_sink[i // N_MIXERS] if kind == 0 else None
        qk_gain = _qk_gain_row(qk_norm_g[i // N_MIXERS]) if kind == 1 else None
        carry = (layer_pack("", 0, BF16, REST), True) if i == 0 else None
        h, proj, *arrived = _mm(xc, W[i][0], F32, pre_g=g_mix_pre[i][None], carry=carry, name=f"mm_in_{i}")
        if carry is not None:
            W[0] = W[0] + layer_weights(arrived[0], REST)
        W_in, W_mkv, W_o, W_gu, W_dn = W[i]
        if kind == 2:
            pr, kv = _headprep_fwd(proj, tab, shift, qk_gain, name=f"headprep_fwd_{i}"), None
        else:
            pr, *kv = _headprep_fwd(proj, tab, shift, qk_gain, kv_heads=True, name=f"headprep_fwd_{i}")
        carry = (layer_pack("", 1, BF16), True) if i == 0 and depth > 1 else None
        tok, mix_saved, arrived = _mixer_fwd(kind, pr, kv, sink, i, carry)
        if carry is not None:
            W[1] = layer_weights(arrived)
        (mkv,) = _mm(mem_n, W_mkv, BF16, name=f"mm_mkv_{i}")
        qm = pr[:, QK_W + KV_W:]
        km = _heads(mkv[:, :QM_W], N_MEM_HEADS, 1)[:, 0]
        vm = _heads(mkv[:, QM_W:], N_MEM_HEADS, 1)[:, 0]
        mo, mlse = _attn(qm, km, vm, bq=min(256, S), tk=km.shape[1], heads=N_MEM_HEADS, kv_per_head=True, tag=f"m{i}")
        cat = jnp.concatenate([tok, mo], axis=1)
        o, x1 = _mm(cat, W_o, F32, post=(g_mix_post[i][None], xc), name=f"mm_o_{i}")
        carry = (layer_pack("", i + 2, BF16), True) if i + 2 < depth else None
        h2, gu, act, *arrived = _mm(x1, W_gu, BF16, pre_g=g_ffn_pre[i][None], swiglu=True, carry=carry, name=f"mm_gu_{i}")
        if carry is not None:
            W[i + 2] = layer_weights(arrived[0])
        f, x2 = _mm(act, W_dn, F32, post=(g_ffn_post[i][None], x1), name=f"mm_dn_{i}")
        saved.append(dict(x=xc, h=h, proj=proj, pr=pr, kv=kv, mix=mix_saved, qm=qm, km=km, vm=vm, mo=mo, mlse=mlse, cat=cat, o=o,
                          x1=x1, h2=h2, gu=gu, act=act, f=f))
        xc = x2

    dy, sq = _loss_head(xc, loss_target[0], name="loss_head")
    loss = lax.psum(sq[0, 0] * (0.5 / D), ("x", "y", "c"))

    grads = {n: [None] * depth for n in ("w_in", "w_mem_kv", "w_o", "w_gate_up", "w_down", "g_mix_pre", "g_mix_post",
                                         "g_ffn_pre", "g_ffn_post")}
    d_sink = [jnp.zeros((N_TOK_HEADS,), F32) for _ in range(attn_sink.shape[0])]
    d_qkg = [jnp.zeros((2, HEAD_DIM), F32) for _ in range(qk_norm_g.shape[0])]
    dmem_n = jnp.zeros((mem.shape[1], D), F32)
    recv = [None] * depth

    def scatter_pack(l, names=BIG):
        return _pack_for_scatter([grads[n][l][None] for n, _ in names], shapes_of(names), BF16, names)

    dx = dy
    for i in reversed(range(depth)):
        kind = i % N_MIXERS
        sv = saved[i]
        (tab, shift) = tabs[1] if kind == 1 else tabs[0]
        sink = attn_sink[i // N_MIXERS] if kind == 0 else None
        W_in, W_mkv, W_o, W_gu, W_dn = W[i]
        df, dgu, dg = _mm(dx, W_dn, BF16, nt=True, gu=sv["gu"], pre_bwd=(sv["f"], g_ffn_post[i][None]), name=f"mmb_dn_{i}")
        grads["g_ffn_post"][i] = dg[0]
        grads["w_down"][i] = _mm_tn(sv["act"], df, name=f"mmw_dn_{i}")
        dx1, dg = _mm(dgu, W_gu, F32, nt=True, post_bwd=(sv["x1"], g_ffn_pre[i][None], dx), name=f"mmb_gu_{i}")
        grads["g_ffn_pre"][i] = dg[0]
        if i + 1 < depth:
            grads["w_gate_up"][i], recv[i + 1] = _mm_tn(sv["h2"], dgu, carry=(scatter_pack(i + 1), False), name=f"mmw_gu_{i}")
        else:
            grads["w_gate_up"][i] = _mm_tn(sv["h2"], dgu, name=f"mmw_gu_{i}")
        do, dcat, dg = _mm(dx1, W_o, BF16, nt=True, pre_bwd=(sv["o"], g_mix_post[i][None]), name=f"mmb_o_{i}")
        grads["g_mix_post"][i] = dg[0]
        grads["w_o"][i] = _mm_tn(sv["cat"], do, name=f"mmw_o_{i}")
        dqm, dkm, dvm, _, _ = _attn_bwd(sv["qm"], sv["km"], sv["vm"], sv["mo"], sv["mlse"], dcat[:, Q_W:], bq=min(256, S),
                                        tk=sv["km"].shape[1], heads=N_MEM_HEADS, kv_per_head=True, tag=f"m{i}")
        dmkv = jnp.concatenate([_unheads(dkm[:, None]), _unheads(dvm[:, None])], axis=1).astype(BF16)
        grads["w_mem_kv"][i] = _mm_tn(mem_n, dmkv, name=f"mmw_mkv_{i}")
        dmem_n = dmem_n + _mm(dmkv, W_mkv, F32, nt=True, name=f"mmb_mkv_{i}")[0]
        carry = (scatter_pack(0, REST), False) if i == 0 else None
        dq, dk, dv, dsink, recv0_rest = _mixer_bwd(kind, dcat, sv["pr"], sv["kv"], sv["cat"], sv["mix"], sink, i, carry)
        if dsink is not None:
            d_sink[i // N_MIXERS] = dsink
        dpr = jnp.concatenate([dq, dk, dv, dqm], axis=1) if kind == 2 else (dq, dk, dv, dqm)
        if kind == 1:
            dproj, dgc = _headprep_bwd(dpr, tab, shift, sv["proj"], _qk_gain_row(qk_norm_g[i // N_MIXERS]),
                                       name=f"headprep_bwd_{i}")
            d_qkg[i // N_MIXERS] = jnp.stack([dgc[0, :Q_W].reshape(N_TOK_HEADS, HEAD_DIM).sum(0),
                                              dgc[0, Q_W:QK_W].reshape(N_KV_HEADS, HEAD_DIM).sum(0)])
        else:
            dproj = _headprep_bwd(dpr, tab, shift, name=f"headprep_bwd_{i}")
        dx, dg = _mm(dproj, W_in, F32, nt=True, post_bwd=(sv["x"], g_mix_pre[i][None], dx1), name=f"mmb_in_{i}")
        grads["g_mix_pre"][i] = dg[0]
        grads["w_in"][i] = _mm_tn(sv["h"], dproj, name=f"mmw_in_{i}")
    _, dg_mem = _rms_bwd(mem[0], mem_norm_g[None], dmem_n, BF16, name="rmsb_mem")

    def update(received, l, names, tag):
        res = _reduce_adamw(received, *[layer_pack(pre, l, F32, names) for pre in ("", "m_", "v_")], name=f"adamw_{tag}")
        return [_unpack_local(r, shapes_of(names)) for r in res]

    recv0_first = _exchange(scatter_pack(0, FIRST), False, "scatter_g0_in")
    per_layer = [[a + b for a, b in zip(update(recv0_first, 0, FIRST, "0_in"), update(recv0_rest, 0, REST, "0"))]]
    per_layer += [update(recv[l], l, BIG, str(l)) for l in range(1, depth)]
    big_out = [[jnp.concatenate(ts, axis=0) for ts in zip(*[per_layer[l][j] for l in range(depth)])] for j in range(4)]

    small_grads = dict(mem_norm_g=dg_mem[0], g_mix_pre=jnp.stack(grads["g_mix_pre"]), g_mix_post=jnp.stack(grads["g_mix_post"]),
                       attn_sink=jnp.stack(d_sink), qk_norm_g=jnp.stack(d_qkg), g_ffn_pre=jnp.stack(grads["g_ffn_pre"]),
                       g_ffn_post=jnp.stack(grads["g_ffn_post"]))
    sg = _pack_small([small_grads[n] for n in SMALL])
    srecv = _exchange(sg, True, "gather_small_grads")
    spacked = lambda pre: _pack_small([given[pre + n] for n in SMALL])
    gs, ds, ms, vs = _reduce_adamw(srecv, spacked(""), spacked("m_"), spacked("v_"), name="adamw_small")

    out = {}
    for pre, fb, fs in zip(("grad_", "delta_", "new_m_", "new_v_"), big_out, (gs, ds, ms, vs)):
        for (n, _), t in zip(BIG, fb):
            out[pre + n] = t
        for n, t in zip(SMALL, _unpack_small(fs, [given[n].shape for n in SMALL])):
            out[pre + n] = t
    order = ("mem_norm_g", "w_in", "w_mem_kv", "w_o", "g_mix_pre", "g_mix_post", "attn_sink", "qk_norm_g", "w_gate_up",
             "w_down", "g_ffn_pre", "g_ffn_post")
    return (loss, dx[None], *[out[p + n] for p in ("grad_", "delta_", "new_m_", "new_v_") for n in order])
```

```python
import functools

import jax
import jax.numpy as jnp
from jax import lax
from jax.experimental import pallas as pl
from jax.experimental.pallas import tpu as pltpu

F32 = jnp.float32
BF16 = jnp.bfloat16

HEAD_DIM = 64
N_TOK_HEADS = 12
N_KV_HEADS = 3
GQA_GROUP = 4
N_MEM_HEADS = 4
Q_W = N_TOK_HEADS * HEAD_DIM
KV_W = N_KV_HEADS * HEAD_DIM
QM_W = N_MEM_HEADS * HEAD_DIM
QK_W = Q_W + KV_W
IN_W = Q_W + 2 * KV_W + QM_W
N_MIXERS = 3
A_RADIUS = 128
C_GROUPS = ((128, 1), (512, 4), (2048, 16))
ROPE_THETA = 500000.0
ROPE_DIMS = HEAD_DIM // 4
AXIAL_THETA = 10000.0
GRID_W = 64
EPS = 1e-6
ATTN_SCALE = HEAD_DIM ** -0.5
NEG = -1e30

ADAM_LR = 0.001
ADAM_B1 = 0.9
ADAM_B2 = 0.999
ADAM_EPS = 1e-08
ADAM_WD = 0.01
ADAM_STEP = 10

N_DEV = 8
LANES = 128
VMEM_LIMIT = 56 * 1024 * 1024
MESH = pl.DeviceIdType.MESH
NT_DIMS = (((1,), (1,)), ((), ()))
TN_DIMS = (((0,), (0,)), ((), ()))


def _pc(body, **kw):
    return pl.pallas_call(body, **kw)


def _cp(*sem):
    return pltpu.CompilerParams(dimension_semantics=sem, vmem_limit_bytes=VMEM_LIMIT)


def _row_tile(m, cap=512):
    t = cap
    while m % t:
        t //= 2
    return t


def _rms_fwd(x, g, out_dtype, name="rms_fwd"):
    M, D = x.shape
    tm = _row_tile(M)

    def body(x_ref, g_ref, o_ref):
        xv = x_ref[...]
        y = xv * lax.rsqrt(jnp.mean(xv * xv, axis=-1, keepdims=True) + EPS) * g_ref[...]
        o_ref[...] = y.astype(o_ref.dtype)

    row = pl.BlockSpec((tm, D), lambda i: (i, 0))
    vec = pl.BlockSpec((1, D), lambda i: (0, 0))
    return _pc(body, name=name, out_shape=jax.ShapeDtypeStruct((M, D), out_dtype), grid=(M // tm,),
               in_specs=[row, vec], out_specs=row, compiler_params=_cp("parallel"))(x, g)


def _rms_bwd_tile(xv, g, d):
    r = lax.rsqrt(jnp.mean(xv * xv, axis=-1, keepdims=True) + EPS)
    xh = xv * r
    dxh = d * g
    return r * (dxh - xh * jnp.mean(dxh * xh, axis=-1, keepdims=True)), jnp.sum(d * xh, axis=0, keepdims=True)


def _rms_bwd(x, g, dy, out_dtype, name="rms_bwd"):
    M, D = x.shape
    tm = _row_tile(M)

    def body(x_ref, g_ref, dy_ref, dx_ref, dg_ref):
        dx, dg = _rms_bwd_tile(x_ref[...], g_ref[...], dy_ref[...].astype(F32))
        dx_ref[...] = dx.astype(dx_ref.dtype)

        @pl.when(pl.program_id(0) == 0)
        def _():
            dg_ref[...] = jnp.zeros_like(dg_ref)

        dg_ref[...] += dg

    row = pl.BlockSpec((tm, D), lambda i: (i, 0))
    vec = pl.BlockSpec((1, D), lambda i: (0, 0))
    return _pc(body, name=name,
               out_shape=(jax.ShapeDtypeStruct((M, D), out_dtype), jax.ShapeDtypeStruct((1, D), F32)),
               grid=(M // tm,), in_specs=[row, vec, row], out_specs=(row, vec), compiler_params=_cp("arbitrary"))(x, g, dy)


def _loss_head(y, t, name="loss_head"):
    M, D = y.shape
    tm = _row_tile(M)

    def body(y_ref, t_ref, dy_ref, acc_ref):
        e = y_ref[...] - t_ref[...]
        dy_ref[...] = e * (1.0 / D)

        @pl.when(pl.program_id(0) == 0)
        def _():
            acc_ref[...] = jnp.zeros_like(acc_ref)

        acc_ref[...] += jnp.sum(e * e)

    row = pl.BlockSpec((tm, D), lambda i: (i, 0))
    return _pc(body, name=name,
               out_shape=(jax.ShapeDtypeStruct((M, D), F32), jax.ShapeDtypeStruct((8, LANES), F32)),
               grid=(M // tm,), in_specs=[row, row],
               out_specs=(row, pl.BlockSpec((8, LANES), lambda i: (0, 0))), compiler_params=_cp("arbitrary"))(y, t)


def _mm(a, w, out_dtype, nt=False, pre_g=None, swiglu=False, post=None, gu=None, pre_bwd=None, post_bwd=None, carry=None,
        tm=512, name="mm"):
    M, K = a.shape
    N = w.shape[0] if nt else w.shape[1]
    tm = _row_tile(M, tm)
    gain_grad = pre_bwd is not None or post_bwd is not None

    def body(*refs):
        refs = list(refs)
        a_ref, w_ref = refs.pop(0), refs.pop(0)
        pg_ref = refs.pop(0) if pre_g is not None else None
        g_ref, r_ref = (refs.pop(0), refs.pop(0)) if post is not None else (None, None)
        gu_ref = refs.pop(0) if gu is not None else None
        bwd_refs = [refs.pop(0) for _ in (pre_bwd or post_bwd or ())]
        lhs = a_ref[...]
        if pre_g is not None:
            lhs = (lhs * lax.rsqrt(jnp.mean(lhs * lhs, axis=-1, keepdims=True) + EPS) * pg_ref[...]).astype(BF16)
            refs.pop(0)[...] = lhs
        if pre_bwd is not None:
            lhs, dg = _rms_bwd_tile(bwd_refs[0][...], bwd_refs[1][...], lhs.astype(F32))
            lhs = lhs.astype(BF16)
            refs.pop(0)[...] = lhs
        if nt:
            acc = lax.dot_general(lhs, w_ref[...], NT_DIMS, preferred_element_type=F32)
        else:
            acc = jnp.dot(lhs, w_ref[...], preferred_element_type=F32)
        o_ref = refs.pop(0)
        if post_bwd is not None:
            dx, dg = _rms_bwd_tile(bwd_refs[0][...], bwd_refs[1][...], acc)
            o_ref[...] = bwd_refs[2][...] + dx
        elif gu is None:
            o_ref[...] = acc.astype(o_ref.dtype)
        else:
            gate = gu_ref[:, :N].astype(F32)
            sig = 1.0 / (1.0 + jnp.exp(-gate))
            o_ref[:, :N] = (acc * gu_ref[:, N:].astype(F32) * (sig * (1.0 + gate * (1.0 - sig)))).astype(o_ref.dtype)
            o_ref[:, N:] = (acc * (gate * sig)).astype(o_ref.dtype)
        if swiglu:
            gate = acc[:, : N // 2]
            refs.pop(0)[...] = (gate * (1.0 / (1.0 + jnp.exp(-gate))) * acc[:, N // 2:]).astype(BF16)
        if post is not None:
            y = acc * lax.rsqrt(jnp.mean(acc * acc, axis=-1, keepdims=True) + EPS) * g_ref[...]
            refs.pop(0)[...] = r_ref[...] + y
        if gain_grad:
            dg_ref = refs.pop(0)

            @pl.when(pl.program_id(0) == 0)
            def _():
                dg_ref[...] = jnp.zeros_like(dg_ref)

            dg_ref[...] += dg

    row = lambda n: pl.BlockSpec((tm, n), lambda i: (i, 0))
    vec = lambda n: pl.BlockSpec((1, n), lambda i: (0, 0))
    ins = [a, w]
    specs = [row(K), pl.BlockSpec(w.shape, lambda i: (0, 0), pipeline_mode=pl.Buffered(1))]
    outs, ospecs = [], []
    if pre_g is not None:
        ins, specs = ins + [pre_g], specs + [vec(K)]
    if pre_g is not None or pre_bwd is not None:
        outs, ospecs = outs + [jax.ShapeDtypeStruct((M, K), BF16)], ospecs + [row(K)]
    if post is not None:
        ins, specs = ins + list(post), specs + [vec(N), row(N)]
    if gu is not None:
        ins, specs = ins + [gu], specs + [row(2 * N)]
        outs, ospecs = outs + [jax.ShapeDtypeStruct((M, 2 * N), BF16)], ospecs + [row(2 * N)]
    else:
        outs, ospecs = outs + [jax.ShapeDtypeStruct((M, N), out_dtype)], ospecs + [row(N)]
    if pre_bwd is not None:
        ins, specs = ins + list(pre_bwd), specs + [row(K), vec(K)]
    if post_bwd is not None:
        ins, specs = ins + list(post_bwd), specs + [row(N), vec(N), row(N)]
    if swiglu:
        outs, ospecs = outs + [jax.ShapeDtypeStruct((M, N // 2), BF16)], ospecs + [row(N // 2)]
    if post is not None:
        outs, ospecs = outs + [jax.ShapeDtypeStruct((M, N), F32)], ospecs + [row(N)]
    if gain_grad:
        D = K if pre_bwd is not None else N
        outs, ospecs = outs + [jax.ShapeDtypeStruct((1, D), F32)], ospecs + [vec(D)]
    return _call(body, carry, ins, name=name, out_shape=outs, grid=(M // tm,), in_specs=specs, out_specs=ospecs,
                 sem=("arbitrary" if gain_grad else "parallel",))


def _mm_tn(a, b, carry=None, name="mm_tn"):
    S, M = a.shape
    N = b.shape[1]
    tm = M if M <= 1408 else M // 2
    tn = N if N <= 1408 else N // 4
    ts = _row_tile(S, 1024)

    def body(a_ref, b_ref, o_ref):
        @pl.when(pl.program_id(2) == 0)
        def _():
            o_ref[...] = jnp.zeros_like(o_ref)

        o_ref[...] += lax.dot_general(a_ref[...], b_ref[...], TN_DIMS, preferred_element_type=F32)

    res = _call(body, carry, [a, b], name=name, out_shape=[jax.ShapeDtypeStruct((M, N), F32)],
                grid=(M // tm, N // tn, S // ts),
                in_specs=[pl.BlockSpec((ts, tm), lambda i, j, s: (s, i)), pl.BlockSpec((ts, tn), lambda i, j, s: (s, j))],
                out_specs=[pl.BlockSpec((tm, tn), lambda i, j, s: (i, j))], sem=("parallel", "parallel", "arbitrary"))
    return res[0] if carry is None else res


def _rope_tables(S):
    pos = jnp.arange(S, dtype=jnp.int32)

    def table(p, n_dims, theta):
        inv = theta ** (-(jnp.arange(0, n_dims, 2, dtype=F32) / n_dims))
        ang = p.astype(F32)[:, None] * inv[None, :]
        return jnp.cos(ang), jnp.sin(ang)

    one = lambda n: jnp.ones((S, n), F32)
    zero = lambda n: jnp.zeros((S, n), F32)
    cp, sp = table(pos, ROPE_DIMS, ROPE_THETA)
    rest = HEAD_DIM - ROPE_DIMS
    part = (jnp.concatenate([cp, cp, one(rest)], 1), jnp.concatenate([zero(8), sp, zero(rest)], 1),
            jnp.concatenate([-sp, zero(8), zero(rest)], 1))
    cr, sr = table(pos // GRID_W, HEAD_DIM // 2, AXIAL_THETA)
    cc, sc = table(pos % GRID_W, HEAD_DIM // 2, AXIAL_THETA)
    axial = (jnp.concatenate([cr, cr, cc, cc], 1), jnp.concatenate([zero(16), sr, zero(16), sc], 1),
             jnp.concatenate([-sr, zero(16), -sc, zero(16)], 1))
    rep = LANES // HEAD_DIM
    return (tuple(jnp.tile(t, (1, rep)) for t in part), ROPE_DIMS // 2), (tuple(jnp.tile(t, (1, rep)) for t in axial), HEAD_DIM // 4)


def _seg_mats():
    col = jnp.arange(IN_W)[:, None] // HEAD_DIM
    e = (col == jnp.arange(LANES)[None, :]).astype(BF16)
    return e, e.T


def _qk_gain_row(qk_g):
    return jnp.concatenate([jnp.tile(qk_g[0], N_TOK_HEADS), jnp.tile(qk_g[1], N_KV_HEADS),
                            jnp.ones((IN_W - QK_W,), F32)])[None, :]


def _rope_cols(tabs, tm):
    col = lax.broadcasted_iota(jnp.int32, (tm, IN_W), 1)
    qk = col < QK_W
    c, s_lo, s_hi = (jnp.tile(t[...], (1, IN_W // LANES)) for t in tabs)
    return jnp.where(qk, c, 1.0), jnp.where(qk, s_lo, 0.0), jnp.where(qk, s_hi, 0.0), qk


def _seg_mean(v, e_ref, et_ref):
    def split_dot(t, m_ref):
        hi = t.astype(BF16)
        lo = (t - hi.astype(F32)).astype(BF16)
        return jnp.dot(hi, m_ref[...], preferred_element_type=F32) + jnp.dot(lo, m_ref[...], preferred_element_type=F32)

    return split_dot(split_dot(v, e_ref) * (1.0 / HEAD_DIM), et_ref)


def _headprep_fwd(proj, tabs, shift, qk_gain=None, kv_heads=False, name="headprep_fwd"):
    S = proj.shape[0]
    tm = _row_tile(S, 512)
    norm = qk_gain is not None

    def body(*refs):
        refs = list(refs)
        p_ref, c_ref, lo_ref, hi_ref = (refs.pop(0) for _ in range(4))
        g_ref, e_ref, et_ref = (refs.pop(0) for _ in range(3)) if norm else (None, None, None)
        o_ref = refs.pop(0)
        x = p_ref[...]
        c, s_lo, s_hi, qk = _rope_cols((c_ref, lo_ref, hi_ref), tm)
        if norm:
            r = lax.rsqrt(_seg_mean(x * x, e_ref, et_ref) + EPS)
            x = x * jnp.where(qk, r, 1.0) * g_ref[...]
        y = (x * c + pltpu.roll(x, shift, 1) * s_lo + pltpu.roll(x, IN_W - shift, 1) * s_hi).astype(o_ref.dtype)
        o_ref[...] = y
        if kv_heads:
            k_ref, v_ref = refs
            for h in range(N_KV_HEADS):
                k_ref[h] = y[:, Q_W + h * HEAD_DIM:Q_W + (h + 1) * HEAD_DIM]
                v_ref[h] = y[:, QK_W + h * HEAD_DIM:QK_W + (h + 1) * HEAD_DIM]

    row = pl.BlockSpec((tm, IN_W), lambda i: (i, 0))
    tab = pl.BlockSpec((tm, LANES), lambda i: (i, 0))
    ins = [proj, *tabs]
    specs = [row, tab, tab, tab]
    if norm:
        e, et = _seg_mats()
        ins += [qk_gain, e, et]
        specs += [pl.BlockSpec((1, IN_W), lambda i: (0, 0)), pl.BlockSpec((IN_W, LANES), lambda i: (0, 0)),
                  pl.BlockSpec((LANES, IN_W), lambda i: (0, 0))]
    out_shape, out_specs = [jax.ShapeDtypeStruct((S, IN_W), BF16)], [row]
    if kv_heads:
        out_shape += [jax.ShapeDtypeStruct((N_KV_HEADS, S, HEAD_DIM), BF16)] * 2
        out_specs += [pl.BlockSpec((N_KV_HEADS, tm, HEAD_DIM), lambda i: (0, i, 0))] * 2
    res = _pc(body, name=name, out_shape=tuple(out_shape), grid=(S // tm,),
              in_specs=specs, out_specs=tuple(out_specs), compiler_params=_cp("parallel"))(*ins)
    return res if kv_heads else res[0]


def _headprep_bwd(dpr, tabs, shift, proj=None, qk_gain=None, name="headprep_bwd"):
    parts = isinstance(dpr, (tuple, list))
    S = dpr[0].shape[0] if parts else dpr.shape[0]
    tm = _row_tile(S, 512)
    norm = qk_gain is not None

    def body(*refs):
        refs = list(refs)
        d_refs = [refs.pop(0) for _ in range(4 if parts else 1)]
        c_ref, lo_ref, hi_ref = (refs.pop(0) for _ in range(3))
        if norm:
            p_ref, g_ref, e_ref, et_ref, o_ref, dg_ref = refs
        else:
            (o_ref,) = refs
        if parts:
            dq_ref, dk_ref, dv_ref, dqm_ref = d_refs
            d = jnp.concatenate([dq_ref[...]] + [dk_ref[h] for h in range(N_KV_HEADS)]
                                + [dv_ref[h] for h in range(N_KV_HEADS)] + [dqm_ref[...]], axis=1).astype(F32)
        else:
            d = d_refs[0][...].astype(F32)
        c, s_lo, s_hi, qk = _rope_cols((c_ref, lo_ref, hi_ref), tm)
        dx = d * c + pltpu.roll(d * s_lo, IN_W - shift, 1) + pltpu.roll(d * s_hi, shift, 1)
        if norm:
            x = p_ref[...]
            r = lax.rsqrt(_seg_mean(x * x, e_ref, et_ref) + EPS)
            xh = x * r

            @pl.when(pl.program_id(0) == 0)
            def _():
                dg_ref[...] = jnp.zeros_like(dg_ref)

            dg_ref[...] += jnp.sum(jnp.where(qk, dx * xh, 0.0), axis=0, keepdims=True)
            dxh = dx * g_ref[...]
            dn = r * (dxh - xh * _seg_mean(dxh * xh, e_ref, et_ref))
            dx = jnp.where(qk, dn, dx)
        o_ref[...] = dx.astype(o_ref.dtype)

    row = pl.BlockSpec((tm, IN_W), lambda i: (i, 0))
    tab = pl.BlockSpec((tm, LANES), lambda i: (i, 0))
    vec = pl.BlockSpec((1, IN_W), lambda i: (0, 0))
    if parts:
        heads = pl.BlockSpec((N_KV_HEADS, tm, HEAD_DIM), lambda i: (0, i, 0))
        ins = [*dpr, *tabs]
        specs = [pl.BlockSpec((tm, Q_W), lambda i: (i, 0)), heads, heads, pl.BlockSpec((tm, QM_W), lambda i: (i, 0)), tab, tab, tab]
    else:
        ins = [dpr, *tabs]
        specs = [row, tab, tab, tab]
    out_shape = jax.ShapeDtypeStruct((S, IN_W), BF16)
    out_specs = row
    if norm:
        e, et = _seg_mats()
        ins += [proj, qk_gain, e, et]
        specs += [row, vec, pl.BlockSpec((IN_W, LANES), lambda i: (0, 0)), pl.BlockSpec((LANES, IN_W), lambda i: (0, 0))]
        out_shape = (out_shape, jax.ShapeDtypeStruct((1, IN_W), F32))
        out_specs = (row, vec)
    return _pc(body, name=name, out_shape=out_shape, grid=(S // tm,), in_specs=specs, out_specs=out_specs,
               compiler_params=_cp("arbitrary" if norm else "parallel"))(*ins)


CHAIN_ROWS_WIDE, CHAIN_ROWS_NARROW = 128, 256
CHAIN_NARROW_KEYS = 1024


def _skewed(n, stages):
    for t in range(n + len(stages) - 1):
        for s, stage in enumerate(stages):
            if 0 <= t - s < n:
                stage(t - s)


def _chain_slices(G, bq, keys):
    cr = min(CHAIN_ROWS_NARROW if keys <= CHAIN_NARROW_KEYS else CHAIN_ROWS_WIDE, bq)
    while bq % cr:
        cr //= 2
    per = bq // cr
    return [(c // per, slice((c % per) * cr, (c % per + 1) * cr), slice(c * cr, (c + 1) * cr)) for c in range(G * per)]


def _v_ones(v):
    return jnp.concatenate([v, jnp.ones(v.shape, v.dtype)], axis=1)


def _q_dims(q, k, heads, kv_per_head=False):
    if heads is None:
        return q.shape
    return (1 if kv_per_head else k.shape[0]), heads, q.shape[0], k.shape[2]


def _q_shape(heads, NB, G, L, HD):
    return (NB, G, L, HD) if heads is None else (L, NB * G * HD)


def _q_spec(heads, G, rows, HD, index):
    if heads is None:
        return pl.BlockSpec((1, G, rows, HD), lambda *ids: (index(*ids)[0], 0, index(*ids)[1], 0))
    return pl.BlockSpec((rows, G * HD), lambda *ids: index(*ids)[::-1])


def _q_at(heads, g, hr, HD):
    return (0, g, hr, slice(None)) if heads is None else (hr, slice(g * HD, (g + 1) * HD))


def _window(L, blk, radius):
    if radius is None:
        return L, None
    W = min(L, blk + 2 * radius)
    assert blk % radius == 0 and (L - W) % radius == 0
    return W, lambda n: radius * jnp.clip(n * (blk // radius) - 1, 0, (L - W) // radius)


def _win_specs(G, W, HD, start, with_g):
    E = pl.Element
    st = (lambda n: 0) if start is None else start
    if with_g:
        return pl.BlockSpec((E(1), E(G), E(W), E(HD)), lambda b, n: (b, 0, st(n), 0))
    return pl.BlockSpec((E(1), E(W), E(HD)), lambda b, n: (b, st(n), 0))


def _attn_delta(do, o, *, dlse=None, lse=None, sink=None, heads=None, name="attn_delta"):
    HD = HEAD_DIM
    (NB, G), L = (heads, do.shape[0]) if heads is not None else (do.shape[:2], do.shape[2])
    bl = _row_tile(L, 1024)

    def body(*refs):
        refs = list(refs)
        sink_ref = refs.pop(0) if sink is not None else None
        do_ref, o_ref = refs.pop(0), refs.pop(0)
        dlse_ref = refs.pop(0) if dlse is not None else None
        lse_ref = refs.pop(0) if sink is not None else None
        delta_ref = refs.pop(0)
        b = pl.program_id(0)
        if heads is None:
            delta = jnp.sum(do_ref[0].astype(F32) * o_ref[0].astype(F32), axis=-1, keepdims=True)
        else:
            prod = do_ref[...].astype(F32) * o_ref[...].astype(F32)
            delta = jnp.concatenate([jnp.sum(prod[:, g * HD:(g + 1) * HD], axis=-1, keepdims=True)[None] for g in range(G)])
        if dlse is not None:
            delta = delta - dlse_ref[0]
        delta_ref[0] = delta
        if sink is not None:
            ds_ref = refs.pop(0)

            @pl.when(pl.program_id(1) == 0)
            def _():
                ds_ref[...] = jnp.zeros_like(ds_ref)

            for g in range(G):
                ps = jnp.exp(sink_ref[b * G + g] - lse_ref[0, g]) * delta[g]
                ds_ref[0, g] -= jnp.sum(ps)

    blk = _q_spec(None if heads is None else G, G, bl, HD, lambda b, n: (b, n))
    col = pl.BlockSpec((1, G, bl, 1), lambda b, n: (b, 0, n, 0))
    ins, specs = [do, o], [blk, blk]
    if dlse is not None:
        ins, specs = ins + [dlse], specs + [col]
    out_shape = jax.ShapeDtypeStruct((NB, G, L, 1), F32)
    out_specs = col
    if sink is not None:
        ins, specs = [sink] + ins + [lse], [pl.BlockSpec(memory_space=pltpu.SMEM)] + specs + [col]
        out_shape = (out_shape, jax.ShapeDtypeStruct((NB, G, 1, LANES), F32))
        out_specs = (col, pl.BlockSpec((1, G, 1, LANES), lambda b, n: (b, 0, 0, 0)))
    return _pc(body, name=name, out_shape=out_shape, grid=(NB, L // bl), in_specs=specs, out_specs=out_specs,
               compiler_params=_cp("parallel", "arbitrary"))(*ins)


def _attn_fwd_full(q, k, v, *, bq, tk, heads=None, name="attn_fwd_full"):
    NB, G, L, HD = _q_dims(q, k, heads)
    Lk = k.shape[1]
    nq, nk = L // bq, Lk // tk
    rows = G * bq
    chains = _chain_slices(G, bq, min(tk, CHAIN_NARROW_KEYS))

    def body(q_ref, k_ref, v_ref, o_ref, lse_ref, m_sc, acc_sc, q_sc):
        j = pl.program_id(2)

        @pl.when(j == 0)
        def _():
            m_sc[...] = jnp.full_like(m_sc, NEG)
            acc_sc[...] = jnp.zeros_like(acc_sc)
            for g, hr, sl in chains:
                q_sc[sl] = q_ref[_q_at(heads, g, hr, HD)] * ATTN_SCALE

        kk = k_ref[0]
        vv = _v_ones(v_ref[0])
        st = [dict() for _ in chains]

        def scores(c):
            st[c]["s"] = lax.dot_general(q_sc[chains[c][2]], kk, NT_DIMS, preferred_element_type=F32)

        def softmax(c):
            sl = chains[c][2]
            m_prev = m_sc[sl]
            m_new = jnp.maximum(m_prev, jnp.max(st[c]["s"], axis=1, keepdims=True))
            st[c]["p"] = jnp.exp(st[c].pop("s") - m_new).astype(BF16)
            st[c]["alpha"] = jnp.exp(m_prev - m_new)
            m_sc[sl] = m_new

        def values(c):
            sl = chains[c][2]
            acc_sc[sl] = st[c].pop("alpha") * acc_sc[sl] + jnp.dot(st[c].pop("p"), vv, preferred_element_type=F32)

        _skewed(len(chains), (scores, softmax, values))

        @pl.when(j == nk - 1)
        def _():
            for g, hr, sl in chains:
                acc = acc_sc[sl]
                l = acc[:, HD:HD + 1]
                o_ref[_q_at(heads, g, hr, HD)] = (acc[:, :HD] / l).astype(o_ref.dtype)
                lse_ref[0, g, hr, :] = m_sc[sl] + jnp.log(l)

    qspec = _q_spec(heads, G, bq, HD, lambda b, n, j: (b, n))
    kspec = pl.BlockSpec((1, tk, HD), lambda b, n, j: (b, j, 0))
    return _pc(body, name=name,
               out_shape=(jax.ShapeDtypeStruct(_q_shape(heads, NB, G, L, HD), BF16), jax.ShapeDtypeStruct((NB, G, L, 1), F32)),
               grid=(NB, nq, nk), in_specs=[qspec, kspec, kspec],
               out_specs=(qspec, pl.BlockSpec((1, G, bq, 1), lambda b, n, j: (b, 0, n, 0))),
               scratch_shapes=[pltpu.VMEM((rows, 1), F32), pltpu.VMEM((rows, 2 * HD), F32), pltpu.VMEM((rows, HD), BF16)],
               compiler_params=_cp("parallel", "parallel", "arbitrary"))(q, k, v)


def _attn_bwd_full(q, k, v, do, lse, o, *, bq, tk, heads=None, kv_per_head=False, name="attn_bwd_full"):
    NB, G, L, HD = _q_dims(q, k, heads, kv_per_head)
    Lk = k.shape[1]
    nq, nk = L // bq, Lk // tk
    chains = _chain_slices(G, bq, tk)
    KH = G if kv_per_head else 1
    assert not kv_per_head or nk == 1

    def body(q_ref, k_ref, v_ref, do_ref, lse_ref, o_ref, dqp_ref, dk_ref, dv_ref, dk_sc, dv_sc):
        n = pl.program_id(2)

        @pl.when(n == 0)
        def _():
            dk_sc[...] = jnp.zeros_like(dk_sc)
            dv_sc[...] = jnp.zeros_like(dv_sc)

        kks = [k_ref[h] for h in range(KH)]
        vvs = [v_ref[h] for h in range(KH)]
        st = [dict() for _ in chains]

        def scores(c):
            g, hr, _ = chains[c]
            at = _q_at(heads, g, hr, HD)
            st[c]["q"] = q_ref[at] * ATTN_SCALE
            st[c]["do"] = do_ref[at]
            st[c]["s"] = lax.dot_general(st[c]["q"], kks[g % KH], NT_DIMS, preferred_element_type=F32)
            st[c]["dp"] = lax.dot_general(st[c]["do"], vvs[g % KH], NT_DIMS, preferred_element_type=F32)
            st[c]["delta"] = jnp.sum(st[c]["do"].astype(F32) * o_ref[at].astype(F32), axis=-1, keepdims=True)

        def softmax(c):
            g, hr, _ = chains[c]
            p = jnp.exp(st[c].pop("s") - lse_ref[0, g, hr, :])
            st[c]["ds"] = (p * (st[c].pop("dp") - st[c].pop("delta"))).astype(BF16)
            st[c]["p"] = p.astype(BF16)

        def grads(c):
            g, hr, _ = chains[c]
            ds = st[c].pop("ds")
            dv_sc[g % KH] += lax.dot_general(st[c].pop("p"), st[c].pop("do"), TN_DIMS, preferred_element_type=F32)
            dk_sc[g % KH] += lax.dot_general(ds, st[c].pop("q"), TN_DIMS, preferred_element_type=F32)
            dqp_ref[(0,) + _q_at(heads, g, hr, HD)] = jnp.dot(ds, kks[g % KH], preferred_element_type=F32) * ATTN_SCALE

        _skewed(len(chains), (scores, softmax, grads))

        @pl.when(n == nq - 1)
        def _():
            dk_ref[...] = dk_sc[...].astype(dk_ref.dtype)
            dv_ref[...] = dv_sc[...].astype(dv_ref.dtype)

    qspec = _q_spec(heads, G, bq, HD, lambda b, m, n: (b, n))
    cspec = pl.BlockSpec((1, G, bq, 1), lambda b, m, n: (b, 0, n, 0))
    kspec = pl.BlockSpec((KH, tk, HD), lambda b, m, n: (b, m, 0))
    kv_shape = jax.ShapeDtypeStruct((NB * KH, Lk, HD), BF16)
    if heads is None:
        pspec = pl.BlockSpec((1, 1, G, bq, HD), lambda b, m, n: (m, b, 0, n, 0))
    else:
        pspec = pl.BlockSpec((1, bq, G * HD), lambda b, m, n: (m, n, b))
    dqp, dk, dv = _pc(body, name=name,
                      out_shape=(jax.ShapeDtypeStruct((nk,) + _q_shape(heads, NB, G, L, HD), F32), kv_shape, kv_shape),
                      grid=(NB, nk, nq), in_specs=[qspec, kspec, kspec, qspec, cspec, qspec],
                      out_specs=(pspec, kspec, kspec),
                      scratch_shapes=[pltpu.VMEM((KH, tk, HD), F32), pltpu.VMEM((KH, tk, HD), F32)],
                      compiler_params=_cp("parallel", "parallel", "arbitrary"))(q, k, v, do, lse, o)
    if nk == 1:
        return dqp[0].astype(BF16), dk, dv
    bl = _row_tile(L, 512)

    def sum_body(p_ref, o_ref):
        acc = p_ref[0]
        for j in range(1, nk):
            acc = acc + p_ref[j]
        o_ref[...] = acc.astype(o_ref.dtype)

    if heads is None:
        pspec = pl.BlockSpec((nk, 1, G, bl, HD), lambda b, n: (0, b, 0, n, 0))
    else:
        pspec = pl.BlockSpec((nk, bl, G * HD), lambda b, n: (0, n, b))
    dq = _pc(sum_body, name=name + "_sum", out_shape=jax.ShapeDtypeStruct(_q_shape(heads, NB, G, L, HD), BF16),
             grid=(NB, L // bl), in_specs=[pspec], out_specs=_q_spec(heads, G, bl, HD, lambda b, n: (b, n)),
             compiler_params=_cp("parallel", "parallel"))(dqp)
    return dq, dk, dv


def _attn_fwd_win(q, k, v, *, radius, sink=None, bq, heads=None, kv_per_head=False, carry=None, name="attn_fwd_win"):
    NB, G, L, HD = _q_dims(q, k, heads, kv_per_head)
    W, start = _window(k.shape[1], bq, radius)
    chains = _chain_slices(G, bq, W)

    def body(*refs):
        if sink is not None:
            sink_ref, *refs = refs
        q_ref, k_ref, v_ref, o_ref, lse_ref = refs
        b, n = pl.program_id(0), pl.program_id(1)
        kks = [k_ref[g] for g in range(G)] if kv_per_head else [k_ref[0]] * G
        vvs = [_v_ones(v_ref[g]) for g in range(G)] if kv_per_head else [_v_ones(v_ref[0])] * G
        st = [dict() for _ in chains]

        def scores(c):
            g, hr, _ = chains[c]
            s = lax.dot_general(q_ref[_q_at(heads, g, hr, HD)] * ATTN_SCALE, kks[g], NT_DIMS, preferred_element_type=F32)
            if radius is not None:
                qpos = n * bq + hr.start + lax.broadcasted_iota(jnp.int32, (hr.stop - hr.start, 1), 0)
                kpos = start(n) + lax.broadcasted_iota(jnp.int32, (1, W), 1)
                s = jnp.where(jnp.abs(qpos - kpos) <= radius, s, NEG)
            st[c]["s"] = s

        def softmax(c):
            g = chains[c][0]
            m = jnp.max(st[c]["s"], axis=1, keepdims=True)
            if sink is not None:
                m = jnp.maximum(m, sink_ref[b * G + g])
            st[c]["p"] = jnp.exp(st[c].pop("s") - m).astype(BF16)
            st[c]["m"] = m

        def values(c):
            g, hr, _ = chains[c]
            acc = jnp.dot(st[c].pop("p"), vvs[g], preferred_element_type=F32)
            m = st[c].pop("m")
            l = acc[:, HD:HD + 1]
            if sink is not None:
                l = l + jnp.exp(sink_ref[b * G + g] - m)
            o_ref[_q_at(heads, g, hr, HD)] = (acc[:, :HD] / l).astype(o_ref.dtype)
            lse_ref[0, g, hr, :] = m + jnp.log(l)

        _skewed(len(chains), (scores, softmax, values))

    qspec = _q_spec(heads, G, bq, HD, lambda b, n: (b, n))
    kspec = pl.BlockSpec((G, W, HD), lambda b, n: (0, 0, 0)) if kv_per_head else _win_specs(G, W, HD, start, False)
    ins, specs = [q, k, v], [qspec, kspec, kspec]
    if sink is not None:
        ins, specs = [sink] + ins, [pl.BlockSpec(memory_space=pltpu.SMEM)] + specs
    return _call(body, carry, ins, name=name,
                 out_shape=(jax.ShapeDtypeStruct(_q_shape(heads, NB, G, L, HD), BF16), jax.ShapeDtypeStruct((NB, G, L, 1), F32)),
                 grid=(NB, L // bq), in_specs=specs,
                 out_specs=(qspec, pl.BlockSpec((1, G, bq, 1), lambda b, n: (b, 0, n, 0))), sem=("parallel", "parallel"))


def _attn_dq_win(q, k, v, do, lse, delta, *, radius, bq, heads=None, name="attn_dq_win"):
    NB, G, L, HD = _q_dims(q, k, heads)
    W, start = _window(L, bq, radius)
    chains = _chain_slices(G, bq, W)

    def body(q_ref, k_ref, v_ref, do_ref, lse_ref, dl_ref, dq_ref):
        n = pl.program_id(1)
        kk, vv = k_ref[0], v_ref[0]
        kpos = start(n) + lax.broadcasted_iota(jnp.int32, (1, W), 1)
        st = [dict() for _ in chains]

        def scores(c):
            g, hr, _ = chains[c]
            at = _q_at(heads, g, hr, HD)
            st[c]["s"] = lax.dot_general(q_ref[at] * ATTN_SCALE, kk, NT_DIMS, preferred_element_type=F32)
            st[c]["dp"] = lax.dot_general(do_ref[at], vv, NT_DIMS, preferred_element_type=F32)

        def softmax(c):
            g, hr, _ = chains[c]
            qpos = n * bq + hr.start + lax.broadcasted_iota(jnp.int32, (hr.stop - hr.start, 1), 0)
            p = jnp.where(jnp.abs(qpos - kpos) <= radius, jnp.exp(st[c].pop("s") - lse_ref[0, g, hr, :]), 0.0)
            st[c]["ds"] = (p * (st[c].pop("dp") - dl_ref[0, g, hr, :])).astype(BF16)

        def grads(c):
            g, hr, _ = chains[c]
            dq = jnp.dot(st[c].pop("ds"), kk, preferred_element_type=F32) * ATTN_SCALE
            dq_ref[_q_at(heads, g, hr, HD)] = dq.astype(dq_ref.dtype)

        _skewed(len(chains), (scores, softmax, grads))

    qspec = _q_spec(heads, G, bq, HD, lambda b, n: (b, n))
    cspec = pl.BlockSpec((1, G, bq, 1), lambda b, n: (b, 0, n, 0))
    kspec = _win_specs(G, W, HD, start, False)
    return _pc(body, name=name, out_shape=jax.ShapeDtypeStruct(_q_shape(heads, NB, G, L, HD), BF16), grid=(NB, L // bq),
               in_specs=[qspec, kspec, kspec, qspec, cspec, cspec], out_specs=qspec,
               compiler_params=_cp("parallel", "parallel"))(q, k, v, do, lse, delta)


def _attn_dkv_win(q, k, v, do, lse, delta, *, radius, bk, heads=None, carry=None, name="attn_dkv_win"):
    NB, G, L, HD = _q_dims(q, k, heads)
    W, start = _window(L, bk, radius)
    chains = _chain_slices(G, W, bk)

    def body(q_ref, k_ref, v_ref, do_ref, lse_ref, dl_ref, dk_ref, dv_ref):
        m = pl.program_id(1)
        kk, vv = k_ref[0], v_ref[0]
        kpos = m * bk + lax.broadcasted_iota(jnp.int32, (1, bk), 1)
        st = [dict() for _ in chains]
        out = dict(dk=jnp.zeros((bk, HD), F32), dv=jnp.zeros((bk, HD), F32))

        def scores(c):
            g, hr, _ = chains[c]
            at = _q_at(heads, g, hr, HD)
            st[c]["q"] = q_ref[at] * ATTN_SCALE
            st[c]["do"] = do_ref[at]
            st[c]["s"] = lax.dot_general(st[c]["q"], kk, NT_DIMS, preferred_element_type=F32)
            st[c]["dp"] = lax.dot_general(st[c]["do"], vv, NT_DIMS, preferred_element_type=F32)

        def softmax(c):
            g, hr, _ = chains[c]
            qpos = start(m) + hr.start + lax.broadcasted_iota(jnp.int32, (hr.stop - hr.start, 1), 0)
            p = jnp.where(jnp.abs(qpos - kpos) <= radius, jnp.exp(st[c].pop("s") - lse_ref[0, g, hr, :]), 0.0)
            st[c]["ds"] = (p * (st[c].pop("dp") - dl_ref[0, g, hr, :])).astype(BF16)
            st[c]["p"] = p.astype(BF16)

        def grads(c):
            out["dv"] = out["dv"] + lax.dot_general(st[c].pop("p"), st[c].pop("do"), TN_DIMS, preferred_element_type=F32)
            out["dk"] = out["dk"] + lax.dot_general(st[c].pop("ds"), st[c].pop("q"), TN_DIMS, preferred_element_type=F32)

        _skewed(len(chains), (scores, softmax, grads))
        dk_ref[0] = out["dk"].astype(dk_ref.dtype)
        dv_ref[0] = out["dv"].astype(dv_ref.dtype)

    if heads is None:
        qspec = _win_specs(G, W, HD, start, True)
    else:
        qspec = pl.BlockSpec((pl.Element(W), pl.Element(G * HD)), lambda b, m: (start(m), b * G * HD))
    cspec = _win_specs(G, W, 1, start, True)
    kspec = pl.BlockSpec((1, bk, HD), lambda b, m: (b, m, 0))
    kv_shape = jax.ShapeDtypeStruct((NB, L, HD), BF16)
    return _call(body, carry, [q, k, v, do, lse, delta], name=name, out_shape=(kv_shape, kv_shape), grid=(NB, L // bk),
                 in_specs=[qspec, kspec, kspec, qspec, cspec, cspec], out_specs=(kspec, kspec), sem=("parallel", "parallel"))


def _attn(q, k, v, *, radius=None, sink=None, bq, tk=None, heads=None, kv_per_head=False, carry=None, tag):
    if radius is None and tk < k.shape[1]:
        return _attn_fwd_full(q, k, v, bq=bq, tk=tk, heads=heads, name=f"attn_fwd_{tag}")
    return _attn_fwd_win(q, k, v, radius=radius, sink=sink, bq=bq, heads=heads, kv_per_head=kv_per_head, carry=carry,
                         name=f"attn_fwd_{tag}")


def _attn_bwd(q, k, v, o, lse, do, *, radius=None, sink=None, dlse=None, bq, tk=None, heads=None, kv_per_head=False,
              carry=None, tag):
    if radius is None:
        assert carry is None
        return (*_attn_bwd_full(q, k, v, do, lse, o, bq=bq, tk=tk, heads=heads, kv_per_head=kv_per_head,
                                name=f"attn_bwd_{tag}"), None, None)
    nbg = None if heads is None else (k.shape[0], heads)
    if sink is not None:
        delta, ds = _attn_delta(do, o, lse=lse, sink=sink, heads=nbg, name=f"attn_delta_{tag}")
        dsink = ds[:, :, 0, 0].reshape(-1)
    else:
        delta, dsink = _attn_delta(do, o, dlse=dlse, heads=nbg, name=f"attn_delta_{tag}"), None
    dq = _attn_dq_win(q, k, v, do, lse, delta, radius=radius, bq=bq, heads=heads, name=f"attn_dq_{tag}")
    dk, dv, *arrived = _attn_dkv_win(q, k, v, do, lse, delta, radius=radius, bk=bq, heads=heads, carry=carry,
                                     name=f"attn_dkv_{tag}")
    return dq, dk, dv, dsink, (arrived[0] if arrived else None)


def _combine_fwd(o, lse, name="combine_fwd"):
    H, S, HD = o.shape
    tm = _row_tile(S, 512)

    def body(o_ref, lse_ref, t_ref):
        for g in range(GQA_GROUP):
            hs = [kv * GQA_GROUP + g for kv in range(N_KV_HEADS)]
            ls = [lse_ref[h] for h in hs]
            mx = functools.reduce(jnp.maximum, ls)
            es = [jnp.exp(l - mx) for l in ls]
            den = functools.reduce(jnp.add, es)
            for h, e in zip(hs, es):
                t_ref[h] = (o_ref[h].astype(F32) * (e / den)).astype(t_ref.dtype)

    blk = pl.BlockSpec((H, tm, HD), lambda i: (0, i, 0))
    col = pl.BlockSpec((H, tm, 1), lambda i: (0, i, 0))
    return _pc(body, name=name, out_shape=jax.ShapeDtypeStruct((H, S, HD), BF16), grid=(S // tm,),
               in_specs=[blk, col], out_specs=blk, compiler_params=_cp("parallel"))(o, lse)


def _combine_bwd(dt, o, lse, name="combine_bwd"):
    H, S, HD = o.shape
    tm = _row_tile(S, 512)

    def body(dt_ref, o_ref, lse_ref, do_ref, dlse_ref):
        for g in range(GQA_GROUP):
            hs = [kv * GQA_GROUP + g for kv in range(N_KV_HEADS)]
            ls = [lse_ref[h] for h in hs]
            mx = functools.reduce(jnp.maximum, ls)
            es = [jnp.exp(l - mx) for l in ls]
            den = functools.reduce(jnp.add, es)
            al = [e / den for e in es]
            dts = [dt_ref[h].astype(F32) for h in hs]
            da = [jnp.sum(d * o_ref[h].astype(F32), axis=-1, keepdims=True) for h, d in zip(hs, dts)]
            dot = functools.reduce(jnp.add, [a * d for a, d in zip(al, da)])
            for h, a, d, dd in zip(hs, al, da, dts):
                do_ref[h] = (dd * a).astype(do_ref.dtype)
                dlse_ref[h] = a * (d - dot)

    blk = pl.BlockSpec((H, tm, HD), lambda i: (0, i, 0))
    col = pl.BlockSpec((H, tm, 1), lambda i: (0, i, 0))
    return _pc(body, name=name,
               out_shape=(jax.ShapeDtypeStruct((H, S, HD), BF16), jax.ShapeDtypeStruct((H, S, 1), F32)),
               grid=(S // tm,), in_specs=[blk, blk, col], out_specs=(blk, col), compiler_params=_cp("parallel"))(dt, o, lse)


def _position():
    x, y, c = lax.axis_index("x"), lax.axis_index("y"), lax.axis_index("c")
    return x, y, c


def _peer(pos, k):
    x, y, c = pos
    return (1 - x if k & 4 else x, 1 - y if k & 2 else y, 1 - c if k & 1 else c)


def _linear(p):
    return 4 * p[0] + 2 * p[1] + p[2]


def _exchange_steps(s_ref, r_ref, send_sems, recv_sems, local_sem, gather):
    pos = _position()
    me = _linear(pos)
    own = pltpu.make_async_copy(s_ref if gather else s_ref.at[me], r_ref.at[me], local_sem)
    peers = range(1, N_DEV)

    def sems(k):
        return dict(send_sem=send_sems.at[k - 1], recv_sem=recv_sems.at[k - 1], device_id=_peer(pos, k), device_id_type=MESH)

    def send(k):
        src = s_ref if gather else s_ref.at[_linear(_peer(pos, k))]
        return pltpu.make_async_remote_copy(src_ref=src, dst_ref=r_ref.at[me], **sems(k))

    def arrival(k):
        slot = r_ref.at[_linear(_peer(pos, k))]
        return pltpu.make_async_remote_copy(src_ref=slot, dst_ref=slot, **sems(k))

    def start():
        own.start()
        for k in peers:
            send(k).start()

    def wait():
        for k in peers:
            arrival(k).wait_recv()
        for k in peers:
            send(k).wait_send()
        own.wait()

    return start, wait


EXCHANGE_SEMS = [pltpu.SemaphoreType.DMA((N_DEV - 1,)), pltpu.SemaphoreType.DMA((N_DEV - 1,)), pltpu.SemaphoreType.DMA]


def _exchange(buf, gather, name):
    def body(s_ref, r_ref, *sems):
        start, wait = _exchange_steps(s_ref, r_ref, *sems, gather)
        start()
        wait()

    hbm = pl.BlockSpec(memory_space=pltpu.HBM)
    out_shape = ((N_DEV,) + buf.shape) if gather else buf.shape
    return _pc(body, name=name, out_shape=jax.ShapeDtypeStruct(out_shape, buf.dtype), in_specs=[hbm], out_specs=hbm,
               scratch_shapes=list(EXCHANGE_SEMS))(buf)


def _call(body, carry, ins, *, name, out_shape, grid, in_specs, out_specs, scratch_shapes=(), sem):
    if carry is None:
        return _pc(body, name=name, out_shape=tuple(out_shape), grid=grid, in_specs=list(in_specs),
                   out_specs=tuple(out_specs), scratch_shapes=list(scratch_shapes), compiler_params=_cp(*sem))(*ins)
    buf, gather = carry
    n_in, n_out, n_sc = len(ins), len(out_shape), len(scratch_shapes)

    def wrapped(*refs):
        in_refs, buf_ref = refs[:n_in], refs[n_in]
        out_refs, recv_ref = refs[n_in + 1:n_in + 1 + n_out], refs[n_in + 1 + n_out]
        rest = refs[n_in + 2 + n_out:]
        first = functools.reduce(jnp.logical_and, [pl.program_id(a) == 0 for a in range(len(grid))])
        last = functools.reduce(jnp.logical_and, [pl.program_id(a) == grid[a] - 1 for a in range(len(grid))])

        @pl.when(first)
        def _():
            _exchange_steps(buf_ref, recv_ref, *rest[n_sc:], gather)[0]()

        body(*in_refs, *out_refs, *rest[:n_sc])

        @pl.when(last)
        def _():
            _exchange_steps(buf_ref, recv_ref, *rest[n_sc:], gather)[1]()

    hbm = pl.BlockSpec(memory_space=pltpu.HBM)
    recv_shape = ((N_DEV,) + buf.shape) if gather else buf.shape
    return _pc(wrapped, name=name, out_shape=(*out_shape, jax.ShapeDtypeStruct(recv_shape, buf.dtype)), grid=grid,
               in_specs=[*in_specs, hbm], out_specs=(*out_specs, hbm), scratch_shapes=[*scratch_shapes, *EXCHANGE_SEMS],
               compiler_params=_cp(*(("arbitrary",) * len(grid))))(*ins, buf)


def _reduce_adamw(recv, w, m, v, name):
    _, R, C = recv.shape
    tr = _row_tile(R, 512)

    def body(r_ref, w_ref, m_ref, v_ref, g_ref, d_ref, nm_ref, nv_ref):
        g = r_ref[0].astype(F32)
        for j in range(1, N_DEV):
            g = g + r_ref[j].astype(F32)
        g_ref[...] = g
        nm = ADAM_B1 * m_ref[...] + (1.0 - ADAM_B1) * g
        nv = ADAM_B2 * v_ref[...] + (1.0 - ADAM_B2) * jnp.square(g)
        m_hat = nm / (1.0 - ADAM_B1 ** ADAM_STEP)
        v_hat = nv / (1.0 - ADAM_B2 ** ADAM_STEP)
        d_ref[...] = -ADAM_LR * (m_hat / (jnp.sqrt(v_hat) + ADAM_EPS) + ADAM_WD * w_ref[...])
        nm_ref[...] = nm
        nv_ref[...] = nv

    row = pl.BlockSpec((tr, C), lambda i: (i, 0))
    out = jax.ShapeDtypeStruct((R, C), F32)
    return _pc(body, name=name, out_shape=(out, out, out, out), grid=(R // tr,),
               in_specs=[pl.BlockSpec((N_DEV, tr, C), lambda i: (0, i, 0)), row, row, row],
               out_specs=(row, row, row, row), compiler_params=_cp("parallel"))(recv, w, m, v)


BIG = (("w_in", 2), ("w_mem_kv", 1), ("w_o", 1), ("w_gate_up", 2), ("w_down", 1))
SMALL = ("mem_norm_g", "g_mix_pre", "g_mix_post", "attn_sink", "qk_norm_g", "g_ffn_pre", "g_ffn_post")
SMALL_W = 1024
FIRST, REST = BIG[:1], BIG[1:]


def _pack_local(shards, dtype):
    return jnp.concatenate([s.astype(dtype).reshape(-1, LANES) for s in shards], axis=0)


def _unpack_local(flat, shapes):
    out, r = [], 0
    for shp in shapes:
        n = shp[0] * shp[1] * shp[2] // LANES
        out.append(flat[r:r + n].reshape(shp))
        r += n
    return out


def _unpack_gathered(g, shapes, names=BIG):
    out, r = [], 0
    for (name, dim), shp in zip(names, shapes):
        n = shp[0] * shp[1] * shp[2] // LANES
        t = g[:, r:r + n].reshape((N_DEV,) + tuple(shp))
        if dim == 2:
            t = t.transpose(1, 2, 0, 3).reshape(shp[0], shp[1], N_DEV * shp[2])
        else:
            t = t.transpose(1, 0, 2, 3).reshape(shp[0], N_DEV * shp[1], shp[2])
        out.append(t)
        r += n
    return out


def _pack_for_scatter(full, shapes, dtype, names=BIG):
    parts = []
    for (name, dim), shp, t in zip(names, shapes, full):
        if dim == 2:
            t = t.reshape(shp[0], shp[1], N_DEV, shp[2]).transpose(2, 0, 1, 3)
        else:
            t = t.reshape(shp[0], N_DEV, shp[1], shp[2]).transpose(1, 0, 2, 3)
        parts.append(t.astype(dtype).reshape(N_DEV, -1, LANES))
    return jnp.concatenate(parts, axis=1)


def _pack_small(arrs):
    flat = jnp.concatenate([a.reshape(-1) for a in arrs])
    pad = (-flat.shape[0]) % (8 * SMALL_W)
    return jnp.pad(flat, (0, pad)).reshape(-1, SMALL_W)


def _unpack_small(flat, shapes):
    flat = flat.reshape(-1)
    out, r = [], 0
    for shp in shapes:
        n = 1
        for d in shp:
            n *= d
        out.append(flat[r:r + n].reshape(shp))
        r += n
    return out


def _heads(t, nb, g):
    S = t.shape[0]
    return t.reshape(S, nb, g, HEAD_DIM).transpose(1, 2, 0, 3)


def _unheads(t):
    nb, g, S, hd = t.shape
    return t.transpose(2, 0, 1, 3).reshape(S, nb * g * hd)


def _dilate(t, dil):
    S = t.shape[0]
    g = t.shape[1] // HEAD_DIM
    return t.reshape(S // dil, dil, g, HEAD_DIM).transpose(1, 2, 0, 3)


def _undilate(t):
    dil, g, L, w = t.shape
    return t.transpose(1, 2, 0, 3).reshape(g, L * dil, w)


FULL_BQ_FWD, FULL_TK_FWD = 512, 8192
FULL_BQ_BWD, FULL_TK_BWD = 1024, 2048


def _mixer_fwd(kind, pr, kv, sink, li, carry=None):
    S = pr.shape[0]
    if kind == 0:
        tok, lse, *arrived = _attn(pr, *kv, radius=A_RADIUS, sink=sink, bq=min(256, S), heads=GQA_GROUP, carry=carry,
                                   tag=f"a{li}")
        return tok, lse, (arrived[0] if arrived else None)
    assert carry is None
    if kind == 1:
        tok, lse = _attn(pr, *kv, bq=min(FULL_BQ_FWD, S), tk=min(FULL_TK_FWD, S), heads=GQA_GROUP, tag=f"b{li}")
        return tok, lse, None
    saved, outs, lses = [], [], []
    for g, (window, dil) in enumerate(C_GROUPS):
        q = _dilate(pr[:, g * GQA_GROUP * HEAD_DIM:(g + 1) * GQA_GROUP * HEAD_DIM], dil)
        k = _dilate(pr[:, Q_W + g * HEAD_DIM:Q_W + (g + 1) * HEAD_DIM], dil)[:, 0]
        v = _dilate(pr[:, QK_W + g * HEAD_DIM:QK_W + (g + 1) * HEAD_DIM], dil)[:, 0]
        o, lse = _attn(q, k, v, radius=window // (2 * dil), bq=min(256, S // dil), tag=f"c{li}g{g}")
        saved.append((q, k, v, o, lse))
        outs.append(_undilate(o))
        lses.append(_undilate(lse))
    o_all, lse_all = jnp.concatenate(outs, 0), jnp.concatenate(lses, 0)
    tok = _combine_fwd(o_all, lse_all, name=f"combine_fwd_{li}")
    return tok.transpose(1, 0, 2).reshape(S, Q_W), (saved, o_all, lse_all), None


def _mixer_bwd(kind, dcat, pr, kv, cat, saved, sink, li, carry=None):
    S = dcat.shape[0]
    if kind == 0:
        return _attn_bwd(pr, *kv, cat, saved, dcat, radius=A_RADIUS, sink=sink, bq=min(256, S), heads=GQA_GROUP, carry=carry,
                         tag=f"a{li}")
    assert carry is None
    if kind == 1:
        return _attn_bwd(pr, *kv, cat, saved, dcat, bq=min(FULL_BQ_BWD, S), tk=min(FULL_TK_BWD, S), heads=GQA_GROUP,
                         tag=f"b{li}")
    per_group, o_all, lse_all = saved
    dt = dcat[:, :Q_W].reshape(S, N_TOK_HEADS, HEAD_DIM).transpose(1, 0, 2)
    do_all, dlse_all = _combine_bwd(dt, o_all, lse_all, name=f"combine_bwd_{li}")
    dqs, dks, dvs = [], [], []
    for g, (window, dil) in enumerate(C_GROUPS):
        q, k, v, o, lse = per_group[g]
        L = S // dil
        hs = slice(g * GQA_GROUP, (g + 1) * GQA_GROUP)
        do = do_all[hs].reshape(GQA_GROUP, L, dil, HEAD_DIM).transpose(2, 0, 1, 3)
        dlse = dlse_all[hs].reshape(GQA_GROUP, L, dil, 1).transpose(2, 0, 1, 3)
        dq, dk, dv, _, _ = _attn_bwd(q, k, v, o, lse, do, radius=window // (2 * dil), dlse=dlse, bq=min(256, L),
                                     tag=f"c{li}g{g}")
        dqs.append(dq.transpose(2, 0, 1, 3).reshape(S, GQA_GROUP * HEAD_DIM))
        dks.append(dk.transpose(1, 0, 2).reshape(S, HEAD_DIM))
        dvs.append(dv.transpose(1, 0, 2).reshape(S, HEAD_DIM))
    return jnp.concatenate(dqs, 1), jnp.concatenate(dks, 1), jnp.concatenate(dvs, 1), None, None


def kernel(x, mem, mem_norm_g, w_in, w_mem_kv, w_o, g_mix_pre, g_mix_post, attn_sink, qk_norm_g, w_gate_up, w_down, g_ffn_pre, g_ffn_post, loss_target, m_mem_norm_g, m_w_in, m_w_mem_kv, m_w_o, m_g_mix_pre, m_g_mix_post, m_attn_sink, m_qk_norm_g, m_w_gate_up, m_w_down, m_g_ffn_pre, m_g_ffn_post, v_mem_norm_g, v_w_in, v_w_mem_kv, v_w_o, v_g_mix_pre, v_g_mix_post, v_attn_sink, v_qk_norm_g, v_w_gate_up, v_w_down, v_g_ffn_pre, v_g_ffn_post):
    given = dict(locals())
    depth = w_in.shape[0]
    S, D = x.shape[1], x.shape[2]
    def shapes_of(names):
        return [(1,) + tuple(given[n].shape[1:]) for n, _ in names]

    def layer_pack(pre, l, dtype, names=BIG):
        return _pack_local([given[pre + n][l:l + 1] for n, _ in names], dtype)

    def layer_weights(gathered, names=BIG):
        return [t[0] for t in _unpack_gathered(gathered, shapes_of(names), names)]

    W = [None] * depth
    W[0] = layer_weights(_exchange(layer_pack("", 0, BF16, FIRST), True, "gather_w0_in"), FIRST)

    tabs = _rope_tables(S)
    mem_n = _rms_fwd(mem[0], mem_norm_g[None], BF16, name="rms_mem")

    saved = []
    xc = x[0]
    for i in range(depth):
        kind = i % N_MIXERS
        (tab, shift) = tabs[1] if kind == 1 else tabs[0]
        sink = attn_sink[i // N_MIXERS] if kind == 0 else None
        qk_gain = _qk_gain_row(qk_norm_g[i // N_MIXERS]) if kind == 1 else None
        carry = (layer_pack("", 0, BF16, REST), True) if i == 0 else None
        h, proj, *arrived = _mm(xc, W[i][0], F32, pre_g=g_mix_pre[i][None], carry=carry, name=f"mm_in_{i}")
        if carry is not None:
            W[0] = W[0] + layer_weights(arrived[0], REST)
        W_in, W_mkv, W_o, W_gu, W_dn = W[i]
        if kind == 2:
            pr, kv = _headprep_fwd(proj, tab, shift, qk_gain, name=f"headprep_fwd_{i}"), None
        else:
            pr, *kv = _headprep_fwd(proj, tab, shift, qk_gain, kv_heads=True, name=f"headprep_fwd_{i}")
        carry = (layer_pack("", 1, BF16), True) if i == 0 and depth > 1 else None
        tok, mix_saved, arrived = _mixer_fwd(kind, pr, kv, sink, i, carry)
        if carry is not None:
            W[1] = layer_weights(arrived)
        (mkv,) = _mm(mem_n, W_mkv, BF16, name=f"mm_mkv_{i}")
        qm = pr[:, QK_W + KV_W:]
        km = _heads(mkv[:, :QM_W], N_MEM_HEADS, 1)[:, 0]
        vm = _heads(mkv[:, QM_W:], N_MEM_HEADS, 1)[:, 0]
        mo, mlse = _attn(qm, km, vm, bq=min(256, S), tk=km.shape[1], heads=N_MEM_HEADS, kv_per_head=True, tag=f"m{i}")
        cat = jnp.concatenate([tok, mo], axis=1)
        o, x1 = _mm(cat, W_o, F32, post=(g_mix_post[i][None], xc), name=f"mm_o_{i}")
        carry = (layer_pack("", i + 2, BF16), True) if i + 2 < depth else None
        h2, gu, act, *arrived = _mm(x1, W_gu, BF16, pre_g=g_ffn_pre[i][None], swiglu=True, carry=carry, name=f"mm_gu_{i}")
        if carry is not None:
            W[i + 2] = layer_weights(arrived[0])
        f, x2 = _mm(act, W_dn, F32, post=(g_ffn_post[i][None], x1), name=f"mm_dn_{i}")
        saved.append(dict(x=xc, h=h, proj=proj, pr=pr, kv=kv, mix=mix_saved, qm=qm, km=km, vm=vm, mo=mo, mlse=mlse, cat=cat, o=o,
                          x1=x1, h2=h2, gu=gu, act=act, f=f))
        xc = x2

    dy, sq = _loss_head(xc, loss_target[0], name="loss_head")
    loss = lax.psum(sq[0, 0] * (0.5 / D), ("x", "y", "c"))

    grads = {n: [None] * depth for n in ("w_in", "w_mem_kv", "w_o", "w_gate_up", "w_down", "g_mix_pre", "g_mix_post",
                                         "g_ffn_pre", "g_ffn_post")}
    d_sink = [jnp.zeros((N_TOK_HEADS,), F32) for _ in range(attn_sink.shape[0])]
    d_qkg = [jnp.zeros((2, HEAD_DIM), F32) for _ in range(qk_norm_g.shape[0])]
    dmem_n = jnp.zeros((mem.shape[1], D), F32)
    recv = [None] * depth

    def scatter_pack(l, names=BIG):
        return _pack_for_scatter([grads[n][l][None] for n, _ in names], shapes_of(names), BF16, names)

    dx = dy
    for i in reversed(range(depth)):
        kind = i % N_MIXERS
        sv = saved[i]
        (tab, shift) = tabs[1] if kind == 1 else tabs[0]
        sink = attn_sink[i // N_MIXERS] if kind == 0 else None
        W_in, W_mkv, W_o, W_gu, W_dn = W[i]
        df, dgu, dg = _mm(dx, W_dn, BF16, nt=True, gu=sv["gu"], pre_bwd=(sv["f"], g_ffn_post[i][None]), name=f"mmb_dn_{i}")
        grads["g_ffn_post"][i] = dg[0]
        grads["w_down"][i] = _mm_tn(sv["act"], df, name=f"mmw_dn_{i}")
        dx1, dg = _mm(dgu, W_gu, F32, nt=True, post_bwd=(sv["x1"], g_ffn_pre[i][None], dx), name=f"mmb_gu_{i}")
        grads["g_ffn_pre"][i] = dg[0]
        if i + 1 < depth:
            grads["w_gate_up"][i], recv[i + 1] = _mm_tn(sv["h2"], dgu, carry=(scatter_pack(i + 1), False), name=f"mmw_gu_{i}")
        else:
            grads["w_gate_up"][i] = _mm_tn(sv["h2"], dgu, name=f"mmw_gu_{i}")
        do, dcat, dg = _mm(dx1, W_o, BF16, nt=True, pre_bwd=(sv["o"], g_mix_post[i][None]), name=f"mmb_o_{i}")
        grads["g_mix_post"][i] = dg[0]
        grads["w_o"][i] = _mm_tn(sv["cat"], do, name=f"mmw_o_{i}")
        dqm, dkm, dvm, _, _ = _attn_bwd(sv["qm"], sv["km"], sv["vm"], sv["mo"], sv["mlse"], dcat[:, Q_W:], bq=min(256, S),
                                        tk=sv["km"].shape[1], heads=N_MEM_HEADS, kv_per_head=True, tag=f"m{i}")
        dmkv = jnp.concatenate([_unheads(dkm[:, None]), _unheads(dvm[:, None])], axis=1).astype(BF16)
        grads["w_mem_kv"][i] = _mm_tn(mem_n, dmkv, name=f"mmw_mkv_{i}")
        dmem_n = dmem_n + _mm(dmkv, W_mkv, F32, nt=True, name=f"mmb_mkv_{i}")[0]
        carry = (scatter_pack(0, REST), False) if i == 0 else None
        dq, dk, dv, dsink, recv0_rest = _mixer_bwd(kind, dcat, sv["pr"], sv["kv"], sv["cat"], sv["mix"], sink, i, carry)
        if dsink is not None:
            d_sink[i // N_MIXERS] = dsink
        dpr = jnp.concatenate([dq, dk, dv, dqm], axis=1) if kind == 2 else (dq, dk, dv, dqm)
        if kind == 1:
            dproj, dgc = _headprep_bwd(dpr, tab, shift, sv["proj"], _qk_gain_row(qk_norm_g[i // N_MIXERS]),
                                       name=f"headprep_bwd_{i}")
            d_qkg[i // N_MIXERS] = jnp.stack([dgc[0, :Q_W].reshape(N_TOK_HEADS, HEAD_DIM).sum(0),
                                              dgc[0, Q_W:QK_W].reshape(N_KV_HEADS, HEAD_DIM).sum(0)])
        else:
            dproj = _headprep_bwd(dpr, tab, shift, name=f"headprep_bwd_{i}")
        dx, dg = _mm(dproj, W_in, F32, nt=True, post_bwd=(sv["x"], g_mix_pre[i][None], dx1), name=f"mmb_in_{i}")
        grads["g_mix_pre"][i] = dg[0]
        grads["w_in"][i] = _mm_tn(sv["h"], dproj, name=f"mmw_in_{i}")
    _, dg_mem = _rms_bwd(mem[0], mem_norm_g[None], dmem_n, BF16, name="rmsb_mem")

    def update(received, l, names, tag):
        res = _reduce_adamw(received, *[layer_pack(pre, l, F32, names) for pre in ("", "m_", "v_")], name=f"adamw_{tag}")
        return [_unpack_local(r, shapes_of(names)) for r in res]

    recv0_first = _exchange(scatter_pack(0, FIRST), False, "scatter_g0_in")
    per_layer = [[a + b for a, b in zip(update(recv0_first, 0, FIRST, "0_in"), update(recv0_rest, 0, REST, "0"))]]
    per_layer += [update(recv[l], l, BIG, str(l)) for l in range(1, depth)]
    big_out = [[jnp.concatenate(ts, axis=0) for ts in zip(*[per_layer[l][j] for l in range(depth)])] for j in range(4)]

    small_grads = dict(mem_norm_g=dg_mem[0], g_mix_pre=jnp.stack(grads["g_mix_pre"]), g_mix_post=jnp.stack(grads["g_mix_post"]),
                       attn_sink=jnp.stack(d_sink), qk_norm_g=jnp.stack(d_qkg), g_ffn_pre=jnp.stack(grads["g_ffn_pre"]),
                       g_ffn_post=jnp.stack(grads["g_ffn_post"]))
    sg = _pack_small([small_grads[n] for n in SMALL])
    srecv = _exchange(sg, True, "gather_small_grads")
    spacked = lambda pre: _pack_small([given[pre + n] for n in SMALL])
    gs, ds, ms, vs = _reduce_adamw(srecv, spacked(""), spacked("m_"), spacked("v_"), name="adamw_small")

    out = {}
    for pre, fb, fs in zip(("grad_", "delta_", "new_m_", "new_v_"), big_out, (gs, ds, ms, vs)):
        for (n, _), t in zip(BIG, fb):
            out[pre + n] = t
        for n, t in zip(SMALL, _unpack_small(fs, [given[n].shape for n in SMALL])):
            out[pre + n] = t
    order = ("mem_norm_g", "w_in", "w_mem_kv", "w_o", "g_mix_pre", "g_mix_post", "attn_sink", "qk_norm_g", "w_gate_up",
             "w_down", "g_ffn_pre", "g_ffn_post")
    return (loss, dx[None], *[out[p + n] for p in ("grad_", "delta_", "new_m_", "new_v_") for n in order])
```

```python
import functools

import jax
import jax.numpy as jnp
from jax import lax
from jax.experimental import pallas as pl
from jax.experimental.pallas import tpu as pltpu

F32 = jnp.float32
BF16 = jnp.bfloat16

HEAD_DIM = 64
N_TOK_HEADS = 12
N_KV_HEADS = 3
GQA_GROUP = 4
N_MEM_HEADS = 4
Q_W = N_TOK_HEADS * HEAD_DIM
KV_W = N_KV_HEADS * HEAD_DIM
QM_W = N_MEM_HEADS * HEAD_DIM
QK_W = Q_W + KV_W
IN_W = Q_W + 2 * KV_W + QM_W
N_MIXERS = 3
A_RADIUS = 128
C_GROUPS = ((128, 1), (512, 4), (2048, 16))
ROPE_THETA = 500000.0
ROPE_DIMS = HEAD_DIM // 4
AXIAL_THETA = 10000.0
GRID_W = 64
EPS = 1e-6
ATTN_SCALE = HEAD_DIM ** -0.5
NEG = -1e30

ADAM_LR = 0.001
ADAM_B1 = 0.9
ADAM_B2 = 0.999
ADAM_EPS = 1e-08
ADAM_WD = 0.01
ADAM_STEP = 10

N_DEV = 8
LANES = 128
VMEM_LIMIT = 56 * 1024 * 1024
MESH = pl.DeviceIdType.MESH
MM_CHAIN_ROWS = 256
NT_DIMS = (((1,), (1,)), ((), ()))
TN_DIMS = (((0,), (0,)), ((), ()))


def _pc(body, **kw):
    return pl.pallas_call(body, **kw)


def _cp(*sem):
    return pltpu.CompilerParams(dimension_semantics=sem, vmem_limit_bytes=VMEM_LIMIT)


def _row_tile(m, cap=512):
    t = cap
    while m % t:
        t //= 2
    return t


def _rms_fwd(x, g, out_dtype, name="rms_fwd"):
    M, D = x.shape
    tm = _row_tile(M)

    def body(x_ref, g_ref, o_ref):
        xv = x_ref[...]
        y = xv * lax.rsqrt(jnp.mean(xv * xv, axis=-1, keepdims=True) + EPS) * g_ref[...]
        o_ref[...] = y.astype(o_ref.dtype)

    row = pl.BlockSpec((tm, D), lambda i: (i, 0))
    vec = pl.BlockSpec((1, D), lambda i: (0, 0))
    return _pc(body, name=name, out_shape=jax.ShapeDtypeStruct((M, D), out_dtype), grid=(M // tm,),
               in_specs=[row, vec], out_specs=row, compiler_params=_cp("parallel"))(x, g)


def _rms_bwd_tile(xv, g, d):
    r = lax.rsqrt(jnp.mean(xv * xv, axis=-1, keepdims=True) + EPS)
    xh = xv * r
    dxh = d * g
    return r * (dxh - xh * jnp.mean(dxh * xh, axis=-1, keepdims=True)), jnp.sum(d * xh, axis=0, keepdims=True)


def _rms_bwd(x, g, dy, out_dtype, name="rms_bwd"):
    M, D = x.shape
    tm = _row_tile(M)

    def body(x_ref, g_ref, dy_ref, dx_ref, dg_ref):
        dx, dg = _rms_bwd_tile(x_ref[...], g_ref[...], dy_ref[...].astype(F32))
        dx_ref[...] = dx.astype(dx_ref.dtype)

        @pl.when(pl.program_id(0) == 0)
        def _():
            dg_ref[...] = jnp.zeros_like(dg_ref)

        dg_ref[...] += dg

    row = pl.BlockSpec((tm, D), lambda i: (i, 0))
    vec = pl.BlockSpec((1, D), lambda i: (0, 0))
    return _pc(body, name=name,
               out_shape=(jax.ShapeDtypeStruct((M, D), out_dtype), jax.ShapeDtypeStruct((1, D), F32)),
               grid=(M // tm,), in_specs=[row, vec, row], out_specs=(row, vec), compiler_params=_cp("arbitrary"))(x, g, dy)


def _loss_head(y, t, name="loss_head"):
    M, D = y.shape
    tm = _row_tile(M)

    def body(y_ref, t_ref, dy_ref, acc_ref):
        e = y_ref[...] - t_ref[...]
        dy_ref[...] = e * (1.0 / D)

        @pl.when(pl.program_id(0) == 0)
        def _():
            acc_ref[...] = jnp.zeros_like(acc_ref)

        acc_ref[...] += jnp.sum(e * e)

    row = pl.BlockSpec((tm, D), lambda i: (i, 0))
    return _pc(body, name=name,
               out_shape=(jax.ShapeDtypeStruct((M, D), F32), jax.ShapeDtypeStruct((8, LANES), F32)),
               grid=(M // tm,), in_specs=[row, row],
               out_specs=(row, pl.BlockSpec((8, LANES), lambda i: (0, 0))), compiler_params=_cp("arbitrary"))(y, t)


def _mm(a, w, out_dtype, nt=False, pre_g=None, swiglu=False, post=None, gu=None, pre_bwd=None, post_bwd=None, carry=None,
        tm=512, name="mm"):
    M, K = a.shape
    N = w.shape[0] if nt else w.shape[1]
    tm = _row_tile(M, tm)
    gain_grad = pre_bwd is not None or post_bwd is not None

    cr = min(MM_CHAIN_ROWS, tm)
    n_chains = tm // cr

    def body(*refs):
        refs = list(refs)
        a_ref, w_ref = refs.pop(0), refs.pop(0)
        pg_ref = refs.pop(0) if pre_g is not None else None
        g_ref, r_ref = (refs.pop(0), refs.pop(0)) if post is not None else (None, None)
        gu_ref = refs.pop(0) if gu is not None else None
        bwd_refs = [refs.pop(0) for _ in (pre_bwd or post_bwd or ())]
        lhs_ref = refs.pop(0) if pre_g is not None or pre_bwd is not None else None
        o_ref = refs.pop(0)
        act_ref = refs.pop(0) if swiglu else None
        x_ref = refs.pop(0) if post is not None else None
        st = [dict() for _ in range(n_chains)]

        def left(c):
            r = slice(c * cr, (c + 1) * cr)
            lhs = a_ref[r, :]
            if pre_g is not None:
                lhs = (lhs * lax.rsqrt(jnp.mean(lhs * lhs, axis=-1, keepdims=True) + EPS) * pg_ref[...]).astype(BF16)
                lhs_ref[r, :] = lhs
            if pre_bwd is not None:
                lhs, st[c]["dg"] = _rms_bwd_tile(bwd_refs[0][r, :], bwd_refs[1][...], lhs.astype(F32))
                lhs = lhs.astype(BF16)
                lhs_ref[r, :] = lhs
            st[c]["lhs"] = lhs

        def product(c):
            if nt:
                st[c]["acc"] = lax.dot_general(st[c].pop("lhs"), w_ref[...], NT_DIMS, preferred_element_type=F32)
            else:
                st[c]["acc"] = jnp.dot(st[c].pop("lhs"), w_ref[...], preferred_element_type=F32)

        def result(c):
            r = slice(c * cr, (c + 1) * cr)
            acc = st[c].pop("acc")
            if post_bwd is not None:
                dx, st[c]["dg"] = _rms_bwd_tile(bwd_refs[0][r, :], bwd_refs[1][...], acc)
                o_ref[r, :] = bwd_refs[2][r, :] + dx
            elif gu is None:
                o_ref[r, :] = acc.astype(o_ref.dtype)
            else:
                gate = gu_ref[r, :N].astype(F32)
                sig = 1.0 / (1.0 + jnp.exp(-gate))
                o_ref[r, :N] = (acc * gu_ref[r, N:].astype(F32) * (sig * (1.0 + gate * (1.0 - sig)))).astype(o_ref.dtype)
                o_ref[r, N:] = (acc * (gate * sig)).astype(o_ref.dtype)
            if swiglu:
                gate = acc[:, : N // 2]
                act_ref[r, :] = (gate * (1.0 / (1.0 + jnp.exp(-gate))) * acc[:, N // 2:]).astype(BF16)
            if post is not None:
                y = acc * lax.rsqrt(jnp.mean(acc * acc, axis=-1, keepdims=True) + EPS) * g_ref[...]
                x_ref[r, :] = r_ref[r, :] + y

        _skewed(n_chains, (left, product, result))
        if gain_grad:
            dg_ref = refs.pop(0)

            @pl.when(pl.program_id(0) == 0)
            def _():
                dg_ref[...] = jnp.zeros_like(dg_ref)

            dg_ref[...] += functools.reduce(jnp.add, [d["dg"] for d in st])

    row = lambda n: pl.BlockSpec((tm, n), lambda i: (i, 0))
    vec = lambda n: pl.BlockSpec((1, n), lambda i: (0, 0))
    ins = [a, w]
    specs = [row(K), pl.BlockSpec(w.shape, lambda i: (0, 0), pipeline_mode=pl.Buffered(1))]
    outs, ospecs = [], []
    if pre_g is not None:
        ins, specs = ins + [pre_g], specs + [vec(K)]
    if pre_g is not None or pre_bwd is not None:
        outs, ospecs = outs + [jax.ShapeDtypeStruct((M, K), BF16)], ospecs + [row(K)]
    if post is not None:
        ins, specs = ins + list(post), specs + [vec(N), row(N)]
    if gu is not None:
        ins, specs = ins + [gu], specs + [row(2 * N)]
        outs, ospecs = outs + [jax.ShapeDtypeStruct((M, 2 * N), BF16)], ospecs + [row(2 * N)]
    else:
        outs, ospecs = outs + [jax.ShapeDtypeStruct((M, N), out_dtype)], ospecs + [row(N)]
    if pre_bwd is not None:
        ins, specs = ins + list(pre_bwd), specs + [row(K), vec(K)]
    if post_bwd is not None:
        ins, specs = ins + list(post_bwd), specs + [row(N), vec(N), row(N)]
    if swiglu:
        outs, ospecs = outs + [jax.ShapeDtypeStruct((M, N // 2), BF16)], ospecs + [row(N // 2)]
    if post is not None:
        outs, ospecs = outs + [jax.ShapeDtypeStruct((M, N), F32)], ospecs + [row(N)]
    if gain_grad:
        D = K if pre_bwd is not None else N
        outs, ospecs = outs + [jax.ShapeDtypeStruct((1, D), F32)], ospecs + [vec(D)]
    return _call(body, carry, ins, name=name, out_shape=outs, grid=(M // tm,), in_specs=specs, out_specs=ospecs,
                 sem=("arbitrary" if gain_grad else "parallel",))


def _mm_tn(a, b, carry=None, name="mm_tn"):
    S, M = a.shape
    N = b.shape[1]
    tm = M if M <= 1408 else M // 2
    tn = N if N <= 1408 else N // 4
    ts = _row_tile(S, 1024)

    def body(a_ref, b_ref, o_ref):
        @pl.when(pl.program_id(2) == 0)
        def _():
            o_ref[...] = jnp.zeros_like(o_ref)

        o_ref[...] += lax.dot_general(a_ref[...], b_ref[...], TN_DIMS, preferred_element_type=F32)

    res = _call(body, carry, [a, b], name=name, out_shape=[jax.ShapeDtypeStruct((M, N), F32)],
                grid=(M // tm, N // tn, S // ts),
                in_specs=[pl.BlockSpec((ts, tm), lambda i, j, s: (s, i)), pl.BlockSpec((ts, tn), lambda i, j, s: (s, j))],
                out_specs=[pl.BlockSpec((tm, tn), lambda i, j, s: (i, j))], sem=("parallel", "parallel", "arbitrary"))
    return res[0] if carry is None else res


def _rope_tables(S):
    pos = jnp.arange(S, dtype=jnp.int32)

    def table(p, n_dims, theta):
        inv = theta ** (-(jnp.arange(0, n_dims, 2, dtype=F32) / n_dims))
        ang = p.astype(F32)[:, None] * inv[None, :]
        return jnp.cos(ang), jnp.sin(ang)

    one = lambda n: jnp.ones((S, n), F32)
    zero = lambda n: jnp.zeros((S, n), F32)
    cp, sp = table(pos, ROPE_DIMS, ROPE_THETA)
    rest = HEAD_DIM - ROPE_DIMS
    part = (jnp.concatenate([cp, cp, one(rest)], 1), jnp.concatenate([zero(8), sp, zero(rest)], 1),
            jnp.concatenate([-sp, zero(8), zero(rest)], 1))
    cr, sr = table(pos // GRID_W, HEAD_DIM // 2, AXIAL_THETA)
    cc, sc = table(pos % GRID_W, HEAD_DIM // 2, AXIAL_THETA)
    axial = (jnp.concatenate([cr, cr, cc, cc], 1), jnp.concatenate([zero(16), sr, zero(16), sc], 1),
             jnp.concatenate([-sr, zero(16), -sc, zero(16)], 1))
    rep = LANES // HEAD_DIM
    return (tuple(jnp.tile(t, (1, rep)) for t in part), ROPE_DIMS // 2), (tuple(jnp.tile(t, (1, rep)) for t in axial), HEAD_DIM // 4)


def _seg_mats():
    col = jnp.arange(IN_W)[:, None] // HEAD_DIM
    e = (col == jnp.arange(LANES)[None, :]).astype(BF16)
    return e, e.T


def _qk_gain_row(qk_g):
    return jnp.concatenate([jnp.tile(qk_g[0], N_TOK_HEADS), jnp.tile(qk_g[1], N_KV_HEADS),
                            jnp.ones((IN_W - QK_W,), F32)])[None, :]


def _rope_cols(tabs, tm):
    col = lax.broadcasted_iota(jnp.int32, (tm, IN_W), 1)
    qk = col < QK_W
    c, s_lo, s_hi = (jnp.tile(t[...], (1, IN_W // LANES)) for t in tabs)
    return jnp.where(qk, c, 1.0), jnp.where(qk, s_lo, 0.0), jnp.where(qk, s_hi, 0.0), qk


def _seg_mean(v, e_ref, et_ref):
    def split_dot(t, m_ref):
        hi = t.astype(BF16)
        lo = (t - hi.astype(F32)).astype(BF16)
        return jnp.dot(hi, m_ref[...], preferred_element_type=F32) + jnp.dot(lo, m_ref[...], preferred_element_type=F32)

    return split_dot(split_dot(v, e_ref) * (1.0 / HEAD_DIM), et_ref)


def _headprep_fwd(proj, tabs, shift, qk_gain=None, kv_heads=False, name="headprep_fwd"):
    S = proj.shape[0]
    tm = _row_tile(S, 512)
    norm = qk_gain is not None

    def body(*refs):
        refs = list(refs)
        p_ref, c_ref, lo_ref, hi_ref = (refs.pop(0) for _ in range(4))
        g_ref, e_ref, et_ref = (refs.pop(0) for _ in range(3)) if norm else (None, None, None)
        o_ref = refs.pop(0)
        x = p_ref[...]
        c, s_lo, s_hi, qk = _rope_cols((c_ref, lo_ref, hi_ref), tm)
        if norm:
            r = lax.rsqrt(_seg_mean(x * x, e_ref, et_ref) + EPS)
            x = x * jnp.where(qk, r, 1.0) * g_ref[...]
        y = (x * c + pltpu.roll(x, shift, 1) * s_lo + pltpu.roll(x, IN_W - shift, 1) * s_hi).astype(o_ref.dtype)
        o_ref[...] = y
        if kv_heads:
            k_ref, v_ref = refs
            for h in range(N_KV_HEADS):
                k_ref[h] = y[:, Q_W + h * HEAD_DIM:Q_W + (h + 1) * HEAD_DIM]
                v_ref[h] = y[:, QK_W + h * HEAD_DIM:QK_W + (h + 1) * HEAD_DIM]

    row = pl.BlockSpec((tm, IN_W), lambda i: (i, 0))
    tab = pl.BlockSpec((tm, LANES), lambda i: (i, 0))
    ins = [proj, *tabs]
    specs = [row, tab, tab, tab]
    if norm:
        e, et = _seg_mats()
        ins += [qk_gain, e, et]
        specs += [pl.BlockSpec((1, IN_W), lambda i: (0, 0)), pl.BlockSpec((IN_W, LANES), lambda i: (0, 0)),
                  pl.BlockSpec((LANES, IN_W), lambda i: (0, 0))]
    out_shape, out_specs = [jax.ShapeDtypeStruct((S, IN_W), BF16)], [row]
    if kv_heads:
        out_shape += [jax.ShapeDtypeStruct((N_KV_HEADS, S, HEAD_DIM), BF16)] * 2
        out_specs += [pl.BlockSpec((N_KV_HEADS, tm, HEAD_DIM), lambda i: (0, i, 0))] * 2
    res = _pc(body, name=name, out_shape=tuple(out_shape), grid=(S // tm,),
              in_specs=specs, out_specs=tuple(out_specs), compiler_params=_cp("parallel"))(*ins)
    return res if kv_heads else res[0]


def _headprep_bwd(dpr, tabs, shift, proj=None, qk_gain=None, name="headprep_bwd"):
    parts = isinstance(dpr, (tuple, list))
    S = dpr[0].shape[0] if parts else dpr.shape[0]
    tm = _row_tile(S, 512)
    norm = qk_gain is not None

    def body(*refs):
        refs = list(refs)
        d_refs = [refs.pop(0) for _ in range(4 if parts else 1)]
        c_ref, lo_ref, hi_ref = (refs.pop(0) for _ in range(3))
        if norm:
            p_ref, g_ref, e_ref, et_ref, o_ref, dg_ref = refs
        else:
            (o_ref,) = refs
        if parts:
            dq_ref, dk_ref, dv_ref, dqm_ref = d_refs
            d = jnp.concatenate([dq_ref[...]] + [dk_ref[h] for h in range(N_KV_HEADS)]
                                + [dv_ref[h] for h in range(N_KV_HEADS)] + [dqm_ref[...]], axis=1).astype(F32)
        else:
            d = d_refs[0][...].astype(F32)
        c, s_lo, s_hi, qk = _rope_cols((c_ref, lo_ref, hi_ref), tm)
        dx = d * c + pltpu.roll(d * s_lo, IN_W - shift, 1) + pltpu.roll(d * s_hi, shift, 1)
        if norm:
            x = p_ref[...]
            r = lax.rsqrt(_seg_mean(x * x, e_ref, et_ref) + EPS)
            xh = x * r

            @pl.when(pl.program_id(0) == 0)
            def _():
                dg_ref[...] = jnp.zeros_like(dg_ref)

            dg_ref[...] += jnp.sum(jnp.where(qk, dx * xh, 0.0), axis=0, keepdims=True)
            dxh = dx * g_ref[...]
            dn = r * (dxh - xh * _seg_mean(dxh * xh, e_ref, et_ref))
            dx = jnp.where(qk, dn, dx)
        o_ref[...] = dx.astype(o_ref.dtype)

    row = pl.BlockSpec((tm, IN_W), lambda i: (i, 0))
    tab = pl.BlockSpec((tm, LANES), lambda i: (i, 0))
    vec = pl.BlockSpec((1, IN_W), lambda i: (0, 0))
    if parts:
        heads = pl.BlockSpec((N_KV_HEADS, tm, HEAD_DIM), lambda i: (0, i, 0))
        ins = [*dpr, *tabs]
        specs = [pl.BlockSpec((tm, Q_W), lambda i: (i, 0)), heads, heads, pl.BlockSpec((tm, QM_W), lambda i: (i, 0)), tab, tab, tab]
    else:
        ins = [dpr, *tabs]
        specs = [row, tab, tab, tab]
    out_shape = jax.ShapeDtypeStruct((S, IN_W), BF16)
    out_specs = row
    if norm:
        e, et = _seg_mats()
        ins += [proj, qk_gain, e, et]
        specs += [row, vec, pl.BlockSpec((IN_W, LANES), lambda i: (0, 0)), pl.BlockSpec((LANES, IN_W), lambda i: (0, 0))]
        out_shape = (out_shape, jax.ShapeDtypeStruct((1, IN_W), F32))
        out_specs = (row, vec)
    return _pc(body, name=name, out_shape=out_shape, grid=(S // tm,), in_specs=specs, out_specs=out_specs,
               compiler_params=_cp("arbitrary" if norm else "parallel"))(*ins)


CHAIN_ROWS_WIDE, CHAIN_ROWS_NARROW = 128, 256
CHAIN_NARROW_KEYS = 1024


def _skewed(n, stages):
    for t in range(n + len(stages) - 1):
        for s, stage in enumerate(stages):
            if 0 <= t - s < n:
                stage(t - s)


def _chain_slices(G, bq, keys):
    cr = min(CHAIN_ROWS_NARROW if keys <= CHAIN_NARROW_KEYS else CHAIN_ROWS_WIDE, bq)
    while bq % cr:
        cr //= 2
    per = bq // cr
    return [(c // per, slice((c % per) * cr, (c % per + 1) * cr), slice(c * cr, (c + 1) * cr)) for c in range(G * per)]


def _v_ones(v):
    return jnp.concatenate([v, jnp.ones(v.shape, v.dtype)], axis=1)


def _q_dims(q, k, heads, kv_per_head=False):
    if heads is None:
        return q.shape
    return (1 if kv_per_head else k.shape[0]), heads, q.shape[0], k.shape[2]


def _q_shape(heads, NB, G, L, HD):
    return (NB, G, L, HD) if heads is None else (L, NB * G * HD)


def _q_spec(heads, G, rows, HD, index):
    if heads is None:
        return pl.BlockSpec((1, G, rows, HD), lambda *ids: (index(*ids)[0], 0, index(*ids)[1], 0))
    return pl.BlockSpec((rows, G * HD), lambda *ids: index(*ids)[::-1])


def _q_at(heads, g, hr, HD):
    return (0, g, hr, slice(None)) if heads is None else (hr, slice(g * HD, (g + 1) * HD))


def _window(L, blk, radius):
    if radius is None:
        return L, None
    W = min(L, blk + 2 * radius)
    assert blk % radius == 0 and (L - W) % radius == 0
    return W, lambda n: radius * jnp.clip(n * (blk // radius) - 1, 0, (L - W) // radius)


def _win_specs(G, W, HD, start, with_g):
    E = pl.Element
    st = (lambda n: 0) if start is None else start
    if with_g:
        return pl.BlockSpec((E(1), E(G), E(W), E(HD)), lambda b, n: (b, 0, st(n), 0))
    return pl.BlockSpec((E(1), E(W), E(HD)), lambda b, n: (b, st(n), 0))


def _attn_delta(do, o, *, dlse=None, lse=None, sink=None, heads=None, name="attn_delta"):
    HD = HEAD_DIM
    (NB, G), L = (heads, do.shape[0]) if heads is not None else (do.shape[:2], do.shape[2])
    bl = _row_tile(L, 1024)

    def body(*refs):
        refs = list(refs)
        sink_ref = refs.pop(0) if sink is not None else None
        do_ref, o_ref = refs.pop(0), refs.pop(0)
        dlse_ref = refs.pop(0) if dlse is not None else None
        lse_ref = refs.pop(0) if sink is not None else None
        delta_ref = refs.pop(0)
        b = pl.program_id(0)
        if heads is None:
            delta = jnp.sum(do_ref[0].astype(F32) * o_ref[0].astype(F32), axis=-1, keepdims=True)
        else:
            prod = do_ref[...].astype(F32) * o_ref[...].astype(F32)
            delta = jnp.concatenate([jnp.sum(prod[:, g * HD:(g + 1) * HD], axis=-1, keepdims=True)[None] for g in range(G)])
        if dlse is not None:
            delta = delta - dlse_ref[0]
        delta_ref[0] = delta
        if sink is not None:
            ds_ref = refs.pop(0)

            @pl.when(pl.program_id(1) == 0)
            def _():
                ds_ref[...] = jnp.zeros_like(ds_ref)

            for g in range(G):
                ps = jnp.exp(sink_ref[b * G + g] - lse_ref[0, g]) * delta[g]
                ds_ref[0, g] -= jnp.sum(ps)

    blk = _q_spec(None if heads is None else G, G, bl, HD, lambda b, n: (b, n))
    col = pl.BlockSpec((1, G, bl, 1), lambda b, n: (b, 0, n, 0))
    ins, specs = [do, o], [blk, blk]
    if dlse is not None:
        ins, specs = ins + [dlse], specs + [col]
    out_shape = jax.ShapeDtypeStruct((NB, G, L, 1), F32)
    out_specs = col
    if sink is not None:
        ins, specs = [sink] + ins + [lse], [pl.BlockSpec(memory_space=pltpu.SMEM)] + specs + [col]
        out_shape = (out_shape, jax.ShapeDtypeStruct((NB, G, 1, LANES), F32))
        out_specs = (col, pl.BlockSpec((1, G, 1, LANES), lambda b, n: (b, 0, 0, 0)))
    return _pc(body, name=name, out_shape=out_shape, grid=(NB, L // bl), in_specs=specs, out_specs=out_specs,
               compiler_params=_cp("parallel", "arbitrary"))(*ins)


def _attn_fwd_full(q, k, v, *, bq, tk, heads=None, name="attn_fwd_full"):
    NB, G, L, HD = _q_dims(q, k, heads)
    Lk = k.shape[1]
    nq, nk = L // bq, Lk // tk
    rows = G * bq
    chains = _chain_slices(G, bq, min(tk, CHAIN_NARROW_KEYS))

    def body(q_ref, k_ref, v_ref, o_ref, lse_ref, m_sc, acc_sc, q_sc):
        j = pl.program_id(2)

        @pl.when(j == 0)
        def _():
            m_sc[...] = jnp.full_like(m_sc, NEG)
            acc_sc[...] = jnp.zeros_like(acc_sc)
            for g, hr, sl in chains:
                q_sc[sl] = q_ref[_q_at(heads, g, hr, HD)] * ATTN_SCALE

        kk = k_ref[0]
        vv = _v_ones(v_ref[0])
        st = [dict() for _ in chains]

        def scores(c):
            st[c]["s"] = lax.dot_general(q_sc[chains[c][2]], kk, NT_DIMS, preferred_element_type=F32)

        def softmax(c):
            sl = chains[c][2]
            m_prev = m_sc[sl]
            m_new = jnp.maximum(m_prev, jnp.max(st[c]["s"], axis=1, keepdims=True))
            st[c]["p"] = jnp.exp(st[c].pop("s") - m_new).astype(BF16)
            st[c]["alpha"] = jnp.exp(m_prev - m_new)
            m_sc[sl] = m_new

        def values(c):
            sl = chains[c][2]
            acc_sc[sl] = st[c].pop("alpha") * acc_sc[sl] + jnp.dot(st[c].pop("p"), vv, preferred_element_type=F32)

        _skewed(len(chains), (scores, softmax, values))

        @pl.when(j == nk - 1)
        def _():
            for g, hr, sl in chains:
                acc = acc_sc[sl]
                l = acc[:, HD:HD + 1]
                o_ref[_q_at(heads, g, hr, HD)] = (acc[:, :HD] / l).astype(o_ref.dtype)
                lse_ref[0, g, hr, :] = m_sc[sl] + jnp.log(l)

    qspec = _q_spec(heads, G, bq, HD, lambda b, n, j: (b, n))
    kspec = pl.BlockSpec((1, tk, HD), lambda b, n, j: (b, j, 0))
    return _pc(body, name=name,
               out_shape=(jax.ShapeDtypeStruct(_q_shape(heads, NB, G, L, HD), BF16), jax.ShapeDtypeStruct((NB, G, L, 1), F32)),
               grid=(NB, nq, nk), in_specs=[qspec, kspec, kspec],
               out_specs=(qspec, pl.BlockSpec((1, G, bq, 1), lambda b, n, j: (b, 0, n, 0))),
               scratch_shapes=[pltpu.VMEM((rows, 1), F32), pltpu.VMEM((rows, 2 * HD), F32), pltpu.VMEM((rows, HD), BF16)],
               compiler_params=_cp("parallel", "parallel", "arbitrary"))(q, k, v)


def _attn_bwd_full(q, k, v, do, lse, o, *, bq, tk, heads=None, kv_per_head=False, name="attn_bwd_full"):
    NB, G, L, HD = _q_dims(q, k, heads, kv_per_head)
    Lk = k.shape[1]
    nq, nk = L // bq, Lk // tk
    chains = _chain_slices(G, bq, tk)
    KH = G if kv_per_head else 1
    assert not kv_per_head or nk == 1

    def body(q_ref, k_ref, v_ref, do_ref, lse_ref, o_ref, dqp_ref, dk_ref, dv_ref, dk_sc, dv_sc):
        n = pl.program_id(2)

        @pl.when(n == 0)
        def _():
            dk_sc[...] = jnp.zeros_like(dk_sc)
            dv_sc[...] = jnp.zeros_like(dv_sc)

        kks = [k_ref[h] for h in range(KH)]
        vvs = [v_ref[h] for h in range(KH)]
        st = [dict() for _ in chains]

        def scores(c):
            g, hr, _ = chains[c]
            at = _q_at(heads, g, hr, HD)
            st[c]["q"] = q_ref[at] * ATTN_SCALE
            st[c]["do"] = do_ref[at]
            st[c]["s"] = lax.dot_general(st[c]["q"], kks[g % KH], NT_DIMS, preferred_element_type=F32)
            st[c]["dp"] = lax.dot_general(st[c]["do"], vvs[g % KH], NT_DIMS, preferred_element_type=F32)
            st[c]["delta"] = jnp.sum(st[c]["do"].astype(F32) * o_ref[at].astype(F32), axis=-1, keepdims=True)

        def softmax(c):
            g, hr, _ = chains[c]
            p = jnp.exp(st[c].pop("s") - lse_ref[0, g, hr, :])
            st[c]["ds"] = (p * (st[c].pop("dp") - st[c].pop("delta"))).astype(BF16)
            st[c]["p"] = p.astype(BF16)

        def grads(c):
            g, hr, _ = chains[c]
            ds = st[c].pop("ds")
            dv_sc[g % KH] += lax.dot_general(st[c].pop("p"), st[c].pop("do"), TN_DIMS, preferred_element_type=F32)
            dk_sc[g % KH] += lax.dot_general(ds, st[c].pop("q"), TN_DIMS, preferred_element_type=F32)
            dqp_ref[(0,) + _q_at(heads, g, hr, HD)] = jnp.dot(ds, kks[g % KH], preferred_element_type=F32) * ATTN_SCALE

        _skewed(len(chains), (scores, softmax, grads))

        @pl.when(n == nq - 1)
        def _():
            dk_ref[...] = dk_sc[...].astype(dk_ref.dtype)
            dv_ref[...] = dv_sc[...].astype(dv_ref.dtype)

    qspec = _q_spec(heads, G, bq, HD, lambda b, m, n: (b, n))
    cspec = pl.BlockSpec((1, G, bq, 1), lambda b, m, n: (b, 0, n, 0))
    kspec = pl.BlockSpec((KH, tk, HD), lambda b, m, n: (b, m, 0))
    kv_shape = jax.ShapeDtypeStruct((NB * KH, Lk, HD), BF16)
    if heads is None:
        pspec = pl.BlockSpec((1, 1, G, bq, HD), lambda b, m, n: (m, b, 0, n, 0))
    else:
        pspec = pl.BlockSpec((1, bq, G * HD), lambda b, m, n: (m, n, b))
    dqp, dk, dv = _pc(body, name=name,
                      out_shape=(jax.ShapeDtypeStruct((nk,) + _q_shape(heads, NB, G, L, HD), F32), kv_shape, kv_shape),
                      grid=(NB, nk, nq), in_specs=[qspec, kspec, kspec, qspec, cspec, qspec],
                      out_specs=(pspec, kspec, kspec),
                      scratch_shapes=[pltpu.VMEM((KH, tk, HD), F32), pltpu.VMEM((KH, tk, HD), F32)],
                      compiler_params=_cp("parallel", "parallel", "arbitrary"))(q, k, v, do, lse, o)
    if nk == 1:
        return dqp[0].astype(BF16), dk, dv
    bl = _row_tile(L, 512)

    def sum_body(p_ref, o_ref):
        acc = p_ref[0]
        for j in range(1, nk):
            acc = acc + p_ref[j]
        o_ref[...] = acc.astype(o_ref.dtype)

    if heads is None:
        pspec = pl.BlockSpec((nk, 1, G, bl, HD), lambda b, n: (0, b, 0, n, 0))
    else:
        pspec = pl.BlockSpec((nk, bl, G * HD), lambda b, n: (0, n, b))
    dq = _pc(sum_body, name=name + "_sum", out_shape=jax.ShapeDtypeStruct(_q_shape(heads, NB, G, L, HD), BF16),
             grid=(NB, L // bl), in_specs=[pspec], out_specs=_q_spec(heads, G, bl, HD, lambda b, n: (b, n)),
             compiler_params=_cp("parallel", "parallel"))(dqp)
    return dq, dk, dv


def _attn_fwd_win(q, k, v, *, radius, sink=None, bq, heads=None, kv_per_head=False, carry=None, name="attn_fwd_win"):
    NB, G, L, HD = _q_dims(q, k, heads, kv_per_head)
    W, start = _window(k.shape[1], bq, radius)
    chains = _chain_slices(G, bq, W)

    def body(*refs):
        if sink is not None:
            sink_ref, *refs = refs
        q_ref, k_ref, v_ref, o_ref, lse_ref = refs
        b, n = pl.program_id(0), pl.program_id(1)
        kks = [k_ref[g] for g in range(G)] if kv_per_head else [k_ref[0]] * G
        vvs = [_v_ones(v_ref[g]) for g in range(G)] if kv_per_head else [_v_ones(v_ref[0])] * G
        st = [dict() for _ in chains]

        def scores(c):
            g, hr, _ = chains[c]
            s = lax.dot_general(q_ref[_q_at(heads, g, hr, HD)] * ATTN_SCALE, kks[g], NT_DIMS, preferred_element_type=F32)
            if radius is not None:
                qpos = n * bq + hr.start + lax.broadcasted_iota(jnp.int32, (hr.stop - hr.start, 1), 0)
                kpos = start(n) + lax.broadcasted_iota(jnp.int32, (1, W), 1)
                s = jnp.where(jnp.abs(qpos - kpos) <= radius, s, NEG)
            st[c]["s"] = s

        def softmax(c):
            g = chains[c][0]
            m = jnp.max(st[c]["s"], axis=1, keepdims=True)
            if sink is not None:
                m = jnp.maximum(m, sink_ref[b * G + g])
            st[c]["p"] = jnp.exp(st[c].pop("s") - m).astype(BF16)
            st[c]["m"] = m

        def values(c):
            g, hr, _ = chains[c]
            acc = jnp.dot(st[c].pop("p"), vvs[g], preferred_element_type=F32)
            m = st[c].pop("m")
            l = acc[:, HD:HD + 1]
            if sink is not None:
                l = l + jnp.exp(sink_ref[b * G + g] - m)
            o_ref[_q_at(heads, g, hr, HD)] = (acc[:, :HD] / l).astype(o_ref.dtype)
            lse_ref[0, g, hr, :] = m + jnp.log(l)

        _skewed(len(chains), (scores, softmax, values))

    qspec = _q_spec(heads, G, bq, HD, lambda b, n: (b, n))
    kspec = pl.BlockSpec((G, W, HD), lambda b, n: (0, 0, 0)) if kv_per_head else _win_specs(G, W, HD, start, False)
    ins, specs = [q, k, v], [qspec, kspec, kspec]
    if sink is not None:
        ins, specs = [sink] + ins, [pl.BlockSpec(memory_space=pltpu.SMEM)] + specs
    return _call(body, carry, ins, name=name,
                 out_shape=(jax.ShapeDtypeStruct(_q_shape(heads, NB, G, L, HD), BF16), jax.ShapeDtypeStruct((NB, G, L, 1), F32)),
                 grid=(NB, L // bq), in_specs=specs,
                 out_specs=(qspec, pl.BlockSpec((1, G, bq, 1), lambda b, n: (b, 0, n, 0))), sem=("parallel", "parallel"))


def _attn_dq_win(q, k, v, do, lse, delta, *, radius, bq, heads=None, name="attn_dq_win"):
    NB, G, L, HD = _q_dims(q, k, heads)
    W, start = _window(L, bq, radius)
    chains = _chain_slices(G, bq, W)

    def body(q_ref, k_ref, v_ref, do_ref, lse_ref, dl_ref, dq_ref):
        n = pl.program_id(1)
        kk, vv = k_ref[0], v_ref[0]
        kpos = start(n) + lax.broadcasted_iota(jnp.int32, (1, W), 1)
        st = [dict() for _ in chains]

        def scores(c):
            g, hr, _ = chains[c]
            at = _q_at(heads, g, hr, HD)
            st[c]["s"] = lax.dot_general(q_ref[at] * ATTN_SCALE, kk, NT_DIMS, preferred_element_type=F32)
            st[c]["dp"] = lax.dot_general(do_ref[at], vv, NT_DIMS, preferred_element_type=F32)

        def softmax(c):
            g, hr, _ = chains[c]
            qpos = n * bq + hr.start + lax.broadcasted_iota(jnp.int32, (hr.stop - hr.start, 1), 0)
            p = jnp.where(jnp.abs(qpos - kpos) <= radius, jnp.exp(st[c].pop("s") - lse_ref[0, g, hr, :]), 0.0)
            st[c]["ds"] = (p * (st[c].pop("dp") - dl_ref[0, g, hr, :])).astype(BF16)

        def grads(c):
            g, hr, _ = chains[c]
            dq = jnp.dot(st[c].pop("ds"), kk, preferred_element_type=F32) * ATTN_SCALE
            dq_ref[_q_at(heads, g, hr, HD)] = dq.astype(dq_ref.dtype)

        _skewed(len(chains), (scores, softmax, grads))

    qspec = _q_spec(heads, G, bq, HD, lambda b, n: (b, n))
    cspec = pl.BlockSpec((1, G, bq, 1), lambda b, n: (b, 0, n, 0))
    kspec = _win_specs(G, W, HD, start, False)
    return _pc(body, name=name, out_shape=jax.ShapeDtypeStruct(_q_shape(heads, NB, G, L, HD), BF16), grid=(NB, L // bq),
               in_specs=[qspec, kspec, kspec, qspec, cspec, cspec], out_specs=qspec,
               compiler_params=_cp("parallel", "parallel"))(q, k, v, do, lse, delta)


def _attn_dkv_win(q, k, v, do, lse, delta, *, radius, bk, heads=None, carry=None, name="attn_dkv_win"):
    NB, G, L, HD = _q_dims(q, k, heads)
    W, start = _window(L, bk, radius)
    chains = _chain_slices(G, W, bk)

    def body(q_ref, k_ref, v_ref, do_ref, lse_ref, dl_ref, dk_ref, dv_ref):
        m = pl.program_id(1)
        kk, vv = k_ref[0], v_ref[0]
        kpos = m * bk + lax.broadcasted_iota(jnp.int32, (1, bk), 1)
        st = [dict() for _ in chains]
        out = dict(dk=jnp.zeros((bk, HD), F32), dv=jnp.zeros((bk, HD), F32))

        def scores(c):
            g, hr, _ = chains[c]
            at = _q_at(heads, g, hr, HD)
            st[c]["q"] = q_ref[at] * ATTN_SCALE
            st[c]["do"] = do_ref[at]
            st[c]["s"] = lax.dot_general(st[c]["q"], kk, NT_DIMS, preferred_element_type=F32)
            st[c]["dp"] = lax.dot_general(st[c]["do"], vv, NT_DIMS, preferred_element_type=F32)

        def softmax(c):
            g, hr, _ = chains[c]
            qpos = start(m) + hr.start + lax.broadcasted_iota(jnp.int32, (hr.stop - hr.start, 1), 0)
            p = jnp.where(jnp.abs(qpos - kpos) <= radius, jnp.exp(st[c].pop("s") - lse_ref[0, g, hr, :]), 0.0)
            st[c]["ds"] = (p * (st[c].pop("dp") - dl_ref[0, g, hr, :])).astype(BF16)
            st[c]["p"] = p.astype(BF16)

        def grads(c):
            out["dv"] = out["dv"] + lax.dot_general(st[c].pop("p"), st[c].pop("do"), TN_DIMS, preferred_element_type=F32)
            out["dk"] = out["dk"] + lax.dot_general(st[c].pop("ds"), st[c].pop("q"), TN_DIMS, preferred_element_type=F32)

        _skewed(len(chains), (scores, softmax, grads))
        dk_ref[0] = out["dk"].astype(dk_ref.dtype)
        dv_ref[0] = out["dv"].astype(dv_ref.dtype)

    if heads is None:
        qspec = _win_specs(G, W, HD, start, True)
    else:
        qspec = pl.BlockSpec((pl.Element(W), pl.Element(G * HD)), lambda b, m: (start(m), b * G * HD))
    cspec = _win_specs(G, W, 1, start, True)
    kspec = pl.BlockSpec((1, bk, HD), lambda b, m: (b, m, 0))
    kv_shape = jax.ShapeDtypeStruct((NB, L, HD), BF16)
    return _call(body, carry, [q, k, v, do, lse, delta], name=name, out_shape=(kv_shape, kv_shape), grid=(NB, L // bk),
                 in_specs=[qspec, kspec, kspec, qspec, cspec, cspec], out_specs=(kspec, kspec), sem=("parallel", "parallel"))


def _attn(q, k, v, *, radius=None, sink=None, bq, tk=None, heads=None, kv_per_head=False, carry=None, tag):
    if radius is None and tk < k.shape[1]:
        return _attn_fwd_full(q, k, v, bq=bq, tk=tk, heads=heads, name=f"attn_fwd_{tag}")
    return _attn_fwd_win(q, k, v, radius=radius, sink=sink, bq=bq, heads=heads, kv_per_head=kv_per_head, carry=carry,
                         name=f"attn_fwd_{tag}")


def _attn_bwd(q, k, v, o, lse, do, *, radius=None, sink=None, dlse=None, bq, tk=None, heads=None, kv_per_head=False,
              carry=None, tag):
    if radius is None:
        assert carry is None
        return (*_attn_bwd_full(q, k, v, do, lse, o, bq=bq, tk=tk, heads=heads, kv_per_head=kv_per_head,
                                name=f"attn_bwd_{tag}"), None, None)
    nbg = None if heads is None else (k.shape[0], heads)
    if sink is not None:
        delta, ds = _attn_delta(do, o, lse=lse, sink=sink, heads=nbg, name=f"attn_delta_{tag}")
        dsink = ds[:, :, 0, 0].reshape(-1)
    else:
        delta, dsink = _attn_delta(do, o, dlse=dlse, heads=nbg, name=f"attn_delta_{tag}"), None
    dq = _attn_dq_win(q, k, v, do, lse, delta, radius=radius, bq=bq, heads=heads, name=f"attn_dq_{tag}")
    dk, dv, *arrived = _attn_dkv_win(q, k, v, do, lse, delta, radius=radius, bk=bq, heads=heads, carry=carry,
                                     name=f"attn_dkv_{tag}")
    return dq, dk, dv, dsink, (arrived[0] if arrived else None)


def _combine_fwd(o, lse, name="combine_fwd"):
    H, S, HD = o.shape
    tm = _row_tile(S, 512)

    def body(o_ref, lse_ref, t_ref):
        for g in range(GQA_GROUP):
            hs = [kv * GQA_GROUP + g for kv in range(N_KV_HEADS)]
            ls = [lse_ref[h] for h in hs]
            mx = functools.reduce(jnp.maximum, ls)
            es = [jnp.exp(l - mx) for l in ls]
            den = functools.reduce(jnp.add, es)
            for h, e in zip(hs, es):
                t_ref[h] = (o_ref[h].astype(F32) * (e / den)).astype(t_ref.dtype)

    blk = pl.BlockSpec((H, tm, HD), lambda i: (0, i, 0))
    col = pl.BlockSpec((H, tm, 1), lambda i: (0, i, 0))
    return _pc(body, name=name, out_shape=jax.ShapeDtypeStruct((H, S, HD), BF16), grid=(S // tm,),
               in_specs=[blk, col], out_specs=blk, compiler_params=_cp("parallel"))(o, lse)


def _combine_bwd(dt, o, lse, name="combine_bwd"):
    H, S, HD = o.shape
    tm = _row_tile(S, 512)

    def body(dt_ref, o_ref, lse_ref, do_ref, dlse_ref):
        for g in range(GQA_GROUP):
            hs = [kv * GQA_GROUP + g for kv in range(N_KV_HEADS)]
            ls = [lse_ref[h] for h in hs]
            mx = functools.reduce(jnp.maximum, ls)
            es = [jnp.exp(l - mx) for l in ls]
            den = functools.reduce(jnp.add, es)
            al = [e / den for e in es]
            dts = [dt_ref[h].astype(F32) for h in hs]
            da = [jnp.sum(d * o_ref[h].astype(F32), axis=-1, keepdims=True) for h, d in zip(hs, dts)]
            dot = functools.reduce(jnp.add, [a * d for a, d in zip(al, da)])
            for h, a, d, dd in zip(hs, al, da, dts):
                do_ref[h] = (dd * a).astype(do_ref.dtype)
                dlse_ref[h] = a * (d - dot)

    blk = pl.BlockSpec((H, tm, HD), lambda i: (0, i, 0))
    col = pl.BlockSpec((H, tm, 1), lambda i: (0, i, 0))
    return _pc(body, name=name,
               out_shape=(jax.ShapeDtypeStruct((H, S, HD), BF16), jax.ShapeDtypeStruct((H, S, 1), F32)),
               grid=(S // tm,), in_specs=[blk, blk, col], out_specs=(blk, col), compiler_params=_cp("parallel"))(dt, o, lse)


def _position():
    x, y, c = lax.axis_index("x"), lax.axis_index("y"), lax.axis_index("c")
    return x, y, c


def _peer(pos, k):
    x, y, c = pos
    return (1 - x if k & 4 else x, 1 - y if k & 2 else y, 1 - c if k & 1 else c)


def _linear(p):
    return 4 * p[0] + 2 * p[1] + p[2]


def _exchange_steps(s_ref, r_ref, send_sems, recv_sems, local_sem, gather):
    pos = _position()
    me = _linear(pos)
    own = pltpu.make_async_copy(s_ref if gather else s_ref.at[me], r_ref.at[me], local_sem)
    peers = range(1, N_DEV)

    def sems(k):
        return dict(send_sem=send_sems.at[k - 1], recv_sem=recv_sems.at[k - 1], device_id=_peer(pos, k), device_id_type=MESH)

    def send(k):
        src = s_ref if gather else s_ref.at[_linear(_peer(pos, k))]
        return pltpu.make_async_remote_copy(src_ref=src, dst_ref=r_ref.at[me], **sems(k))

    def arrival(k):
        slot = r_ref.at[_linear(_peer(pos, k))]
        return pltpu.make_async_remote_copy(src_ref=slot, dst_ref=slot, **sems(k))

    def start():
        own.start()
        for k in peers:
            send(k).start()

    def wait():
        for k in peers:
            arrival(k).wait_recv()
        for k in peers:
            send(k).wait_send()
        own.wait()

    return start, wait


EXCHANGE_SEMS = [pltpu.SemaphoreType.DMA((N_DEV - 1,)), pltpu.SemaphoreType.DMA((N_DEV - 1,)), pltpu.SemaphoreType.DMA]


def _exchange(buf, gather, name):
    def body(s_ref, r_ref, *sems):
        start, wait = _exchange_steps(s_ref, r_ref, *sems, gather)
        start()
        wait()

    hbm = pl.BlockSpec(memory_space=pltpu.HBM)
    out_shape = ((N_DEV,) + buf.shape) if gather else buf.shape
    return _pc(body, name=name, out_shape=jax.ShapeDtypeStruct(out_shape, buf.dtype), in_specs=[hbm], out_specs=hbm,
               scratch_shapes=list(EXCHANGE_SEMS))(buf)


def _call(body, carry, ins, *, name, out_shape, grid, in_specs, out_specs, scratch_shapes=(), sem):
    if carry is None:
        return _pc(body, name=name, out_shape=tuple(out_shape), grid=grid, in_specs=list(in_specs),
                   out_specs=tuple(out_specs), scratch_shapes=list(scratch_shapes), compiler_params=_cp(*sem))(*ins)
    buf, gather = carry
    n_in, n_out, n_sc = len(ins), len(out_shape), len(scratch_shapes)

    def wrapped(*refs):
        in_refs, buf_ref = refs[:n_in], refs[n_in]
        out_refs, recv_ref = refs[n_in + 1:n_in + 1 + n_out], refs[n_in + 1 + n_out]
        rest = refs[n_in + 2 + n_out:]
        first = functools.reduce(jnp.logical_and, [pl.program_id(a) == 0 for a in range(len(grid))])
        last = functools.reduce(jnp.logical_and, [pl.program_id(a) == grid[a] - 1 for a in range(len(grid))])

        @pl.when(first)
        def _():
            _exchange_steps(buf_ref, recv_ref, *rest[n_sc:], gather)[0]()

        body(*in_refs, *out_refs, *rest[:n_sc])

        @pl.when(last)
        def _():
            _exchange_steps(buf_ref, recv_ref, *rest[n_sc:], gather)[1]()

    hbm = pl.BlockSpec(memory_space=pltpu.HBM)
    recv_shape = ((N_DEV,) + buf.shape) if gather else buf.shape
    return _pc(wrapped, name=name, out_shape=(*out_shape, jax.ShapeDtypeStruct(recv_shape, buf.dtype)), grid=grid,
               in_specs=[*in_specs, hbm], out_specs=(*out_specs, hbm), scratch_shapes=[*scratch_shapes, *EXCHANGE_SEMS],
               compiler_params=_cp(*(("arbitrary",) * len(grid))))(*ins, buf)


def _reduce_adamw(recv, w, m, v, name):
    _, R, C = recv.shape
    tr = _row_tile(R, 512)

    def body(r_ref, w_ref, m_ref, v_ref, g_ref, d_ref, nm_ref, nv_ref):
        g = r_ref[0].astype(F32)
        for j in range(1, N_DEV):
            g = g + r_ref[j].astype(F32)
        g_ref[...] = g
        nm = ADAM_B1 * m_ref[...] + (1.0 - ADAM_B1) * g
        nv = ADAM_B2 * v_ref[...] + (1.0 - ADAM_B2) * jnp.square(g)
        m_hat = nm / (1.0 - ADAM_B1 ** ADAM_STEP)
        v_hat = nv / (1.0 - ADAM_B2 ** ADAM_STEP)
        d_ref[...] = -ADAM_LR * (m_hat / (jnp.sqrt(v_hat) + ADAM_EPS) + ADAM_WD * w_ref[...])
        nm_ref[...] = nm
        nv_ref[...] = nv

    row = pl.BlockSpec((tr, C), lambda i: (i, 0))
    out = jax.ShapeDtypeStruct((R, C), F32)
    return _pc(body, name=name, out_shape=(out, out, out, out), grid=(R // tr,),
               in_specs=[pl.BlockSpec((N_DEV, tr, C), lambda i: (0, i, 0)), row, row, row],
               out_specs=(row, row, row, row), compiler_params=_cp("parallel"))(recv, w, m, v)


BIG = (("w_in", 2), ("w_mem_kv", 1), ("w_o", 1), ("w_gate_up", 2), ("w_down", 1))
SMALL = ("mem_norm_g", "g_mix_pre", "g_mix_post", "attn_sink", "qk_norm_g", "g_ffn_pre", "g_ffn_post")
SMALL_W = 1024
FIRST, REST = BIG[:1], BIG[1:]


def _pack_local(shards, dtype):
    return jnp.concatenate([s.astype(dtype).reshape(-1, LANES) for s in shards], axis=0)


def _unpack_local(flat, shapes):
    out, r = [], 0
    for shp in shapes:
        n = shp[0] * shp[1] * shp[2] // LANES
        out.append(flat[r:r + n].reshape(shp))
        r += n
    return out


def _unpack_gathered(g, shapes, names=BIG):
    out, r = [], 0
    for (name, dim), shp in zip(names, shapes):
        n = shp[0] * shp[1] * shp[2] // LANES
        t = g[:, r:r + n].reshape((N_DEV,) + tuple(shp))
        if dim == 2:
            t = t.transpose(1, 2, 0, 3).reshape(shp[0], shp[1], N_DEV * shp[2])
        else:
            t = t.transpose(1, 0, 2, 3).reshape(shp[0], N_DEV * shp[1], shp[2])
        out.append(t)
        r += n
    return out


def _pack_for_scatter(full, shapes, dtype, names=BIG):
    parts = []
    for (name, dim), shp, t in zip(names, shapes, full):
        if dim == 2:
            t = t.reshape(shp[0], shp[1], N_DEV, shp[2]).transpose(2, 0, 1, 3)
        else:
            t = t.reshape(shp[0], N_DEV, shp[1], shp[2]).transpose(1, 0, 2, 3)
        parts.append(t.astype(dtype).reshape(N_DEV, -1, LANES))
    return jnp.concatenate(parts, axis=1)


def _pack_small(arrs):
    flat = jnp.concatenate([a.reshape(-1) for a in arrs])
    pad = (-flat.shape[0]) % (8 * SMALL_W)
    return jnp.pad(flat, (0, pad)).reshape(-1, SMALL_W)


def _unpack_small(flat, shapes):
    flat = flat.reshape(-1)
    out, r = [], 0
    for shp in shapes:
        n = 1
        for d in shp:
            n *= d
        out.append(flat[r:r + n].reshape(shp))
        r += n
    return out


def _heads(t, nb, g):
    S = t.shape[0]
    return t.reshape(S, nb, g, HEAD_DIM).transpose(1, 2, 0, 3)


def _unheads(t):
    nb, g, S, hd = t.shape
    return t.transpose(2, 0, 1, 3).reshape(S, nb * g * hd)


def _dilate(t, dil):
    S = t.shape[0]
    g = t.shape[1] // HEAD_DIM
    return t.reshape(S // dil, dil, g, HEAD_DIM).transpose(1, 2, 0, 3)


def _undilate(t):
    dil, g, L, w = t.shape
    return t.transpose(1, 2, 0, 3).reshape(g, L * dil, w)


FULL_BQ_FWD, FULL_TK_FWD = 512, 8192
FULL_BQ_BWD, FULL_TK_BWD = 1024, 2048


def _mixer_fwd(kind, pr, kv, sink, li, carry=None):
    S = pr.shape[0]
    if kind == 0:
        tok, lse, *arrived = _attn(pr, *kv, radius=A_RADIUS, sink=sink, bq=min(256, S), heads=GQA_GROUP, carry=carry,
                                   tag=f"a{li}")
        return tok, lse, (arrived[0] if arrived else None)
    assert carry is None
    if kind == 1:
        tok, lse = _attn(pr, *kv, bq=min(FULL_BQ_FWD, S), tk=min(FULL_TK_FWD, S), heads=GQA_GROUP, tag=f"b{li}")
        return tok, lse, None
    saved, outs, lses = [], [], []
    for g, (window, dil) in enumerate(C_GROUPS):
        q = _dilate(pr[:, g * GQA_GROUP * HEAD_DIM:(g + 1) * GQA_GROUP * HEAD_DIM], dil)
        k = _dilate(pr[:, Q_W + g * HEAD_DIM:Q_W + (g + 1) * HEAD_DIM], dil)[:, 0]
        v = _dilate(pr[:, QK_W + g * HEAD_DIM:QK_W + (g + 1) * HEAD_DIM], dil)[:, 0]
        o, lse = _attn(q, k, v, radius=window // (2 * dil), bq=min(256, S // dil), tag=f"c{li}g{g}")
        saved.append((q, k, v, o, lse))
        outs.append(_undilate(o))
        lses.append(_undilate(lse))
    o_all, lse_all = jnp.concatenate(outs, 0), jnp.concatenate(lses, 0)
    tok = _combine_fwd(o_all, lse_all, name=f"combine_fwd_{li}")
    return tok.transpose(1, 0, 2).reshape(S, Q_W), (saved, o_all, lse_all), None


def _mixer_bwd(kind, dcat, pr, kv, cat, saved, sink, li, carry=None):
    S = dcat.shape[0]
    if kind == 0:
        return _attn_bwd(pr, *kv, cat, saved, dcat, radius=A_RADIUS, sink=sink, bq=min(256, S), heads=GQA_GROUP, carry=carry,
                         tag=f"a{li}")
    assert carry is None
    if kind == 1:
        return _attn_bwd(pr, *kv, cat, saved, dcat, bq=min(FULL_BQ_BWD, S), tk=min(FULL_TK_BWD, S), heads=GQA_GROUP,
                         tag=f"b{li}")
    per_group, o_all, lse_all = saved
    dt = dcat[:, :Q_W].reshape(S, N_TOK_HEADS, HEAD_DIM).transpose(1, 0, 2)
    do_all, dlse_all = _combine_bwd(dt, o_all, lse_all, name=f"combine_bwd_{li}")
    dqs, dks, dvs = [], [], []
    for g, (window, dil) in enumerate(C_GROUPS):
        q, k, v, o, lse = per_group[g]
        L = S // dil
        hs = slice(g * GQA_GROUP, (g + 1) * GQA_GROUP)
        do = do_all[hs].reshape(GQA_GROUP, L, dil, HEAD_DIM).transpose(2, 0, 1, 3)
        dlse = dlse_all[hs].reshape(GQA_GROUP, L, dil, 1).transpose(2, 0, 1, 3)
        dq, dk, dv, _, _ = _attn_bwd(q, k, v, o, lse, do, radius=window // (2 * dil), dlse=dlse, bq=min(256, L),
                                     tag=f"c{li}g{g}")
        dqs.append(dq.transpose(2, 0, 1, 3).reshape(S, GQA_GROUP * HEAD_DIM))
        dks.append(dk.transpose(1, 0, 2).reshape(S, HEAD_DIM))
        dvs.append(dv.transpose(1, 0, 2).reshape(S, HEAD_DIM))
    return jnp.concatenate(dqs, 1), jnp.concatenate(dks, 1), jnp.concatenate(dvs, 1), None, None


def kernel(x, mem, mem_norm_g, w_in, w_mem_kv, w_o, g_mix_pre, g_mix_post, attn_sink, qk_norm_g, w_gate_up, w_down, g_ffn_pre, g_ffn_post, loss_target, m_mem_norm_g, m_w_in, m_w_mem_kv, m_w_o, m_g_mix_pre, m_g_mix_post, m_attn_sink, m_qk_norm_g, m_w_gate_up, m_w_down, m_g_ffn_pre, m_g_ffn_post, v_mem_norm_g, v_w_in, v_w_mem_kv, v_w_o, v_g_mix_pre, v_g_mix_post, v_attn_sink, v_qk_norm_g, v_w_gate_up, v_w_down, v_g_ffn_pre, v_g_ffn_post):
    given = dict(locals())
    depth = w_in.shape[0]
    S, D = x.shape[1], x.shape[2]
    def shapes_of(names):
        return [(1,) + tuple(given[n].shape[1:]) for n, _ in names]

    def layer_pack(pre, l, dtype, names=BIG):
        return _pack_local([given[pre + n][l:l + 1] for n, _ in names], dtype)

    def layer_weights(gathered, names=BIG):
        return [t[0] for t in _unpack_gathered(gathered, shapes_of(names), names)]

    W = [None] * depth
    W[0] = layer_weights(_exchange(layer_pack("", 0, BF16, FIRST), True, "gather_w0_in"), FIRST)

    tabs = _rope_tables(S)
    mem_n = _rms_fwd(mem[0], mem_norm_g[None], BF16, name="rms_mem")

    saved = []
    xc = x[0]
    for i in range(depth):
        kind = i % N_MIXERS
        (tab, shift) = tabs[1] if kind == 1 else tabs[0]
        sink = attn_sink[i // N_MIXERS] if kind == 0 else None
        qk_gain = _qk_gain_row(qk_norm_g[i // N_MIXERS]) if kind == 1 else None
        carry = (layer_pack("", 0, BF16, REST), True) if i == 0 else None
        h, proj, *arrived = _mm(xc, W[i][0], F32, pre_g=g_mix_pre[i][None], carry=carry, name=f"mm_in_{i}")
        if carry is not None:
            W[0] = W[0] + layer_weights(arrived[0], REST)
        W_in, W_mkv, W_o, W_gu, W_dn = W[i]
        if kind == 2:
            pr, kv = _headprep_fwd(proj, tab, shift, qk_gain, name=f"headprep_fwd_{i}"), None
        else:
            pr, *kv = _headprep_fwd(proj, tab, shift, qk_gain, kv_heads=True, name=f"headprep_fwd_{i}")
        carry = (layer_pack("", 1, BF16), True) if i == 0 and depth > 1 else None
        tok, mix_saved, arrived = _mixer_fwd(kind, pr, kv, sink, i, carry)
        if carry is not None:
            W[1] = layer_weights(arrived)
        (mkv,) = _mm(mem_n, W_mkv, BF16, name=f"mm_mkv_{i}")
        qm = pr[:, QK_W + KV_W:]
        km = _heads(mkv[:, :QM_W], N_MEM_HEADS, 1)[:, 0]
        vm = _heads(mkv[:, QM_W:], N_MEM_HEADS, 1)[:, 0]
        mo, mlse = _attn(qm, km, vm, bq=min(256, S), tk=km.shape[1], heads=N_MEM_HEADS, kv_per_head=True, tag=f"m{i}")
        cat = jnp.concatenate([tok, mo], axis=1)
        o, x1 = _mm(cat, W_o, F32, post=(g_mix_post[i][None], xc), name=f"mm_o_{i}")
        carry = (layer_pack("", i + 2, BF16), True) if i + 2 < depth else None
        h2, gu, act, *arrived = _mm(x1, W_gu, BF16, pre_g=g_ffn_pre[i][None], swiglu=True, carry=carry, name=f"mm_gu_{i}")
        if carry is not None:
            W[i + 2] = layer_weights(arrived[0])
        f, x2 = _mm(act, W_dn, F32, post=(g_ffn_post[i][None], x1), name=f"mm_dn_{i}")
        saved.append(dict(x=xc, h=h, proj=proj, pr=pr, kv=kv, mix=mix_saved, qm=qm, km=km, vm=vm, mo=mo, mlse=mlse, cat=cat, o=o,
                          x1=x1, h2=h2, gu=gu, act=act, f=f))
        xc = x2

    dy, sq = _loss_head(xc, loss_target[0], name="loss_head")
    loss = lax.psum(sq[0, 0] * (0.5 / D), ("x", "y", "c"))

    grads = {n: [None] * depth for n in ("w_in", "w_mem_kv", "w_o", "w_gate_up", "w_down", "g_mix_pre", "g_mix_post",
                                         "g_ffn_pre", "g_ffn_post")}
    d_sink = [jnp.zeros((N_TOK_HEADS,), F32) for _ in range(attn_sink.shape[0])]
    d_qkg = [jnp.zeros((2, HEAD_DIM), F32) for _ in range(qk_norm_g.shape[0])]
    dmem_n = jnp.zeros((mem.shape[1], D), F32)
    recv = [None] * depth

    def scatter_pack(l, names=BIG):
        return _pack_for_scatter([grads[n][l][None] for n, _ in names], shapes_of(names), BF16, names)

    dx = dy
    for i in reversed(range(depth)):
        kind = i % N_MIXERS
        sv = saved[i]
        (tab, shift) = tabs[1] if kind == 1 else tabs[0]
        sink = attn_sink[i // N_MIXERS] if kind == 0 else None
        W_in, W_mkv, W_o, W_gu, W_dn = W[i]
        df, dgu, dg = _mm(dx, W_dn, BF16, nt=True, gu=sv["gu"], pre_bwd=(sv["f"], g_ffn_post[i][None]), name=f"mmb_dn_{i}")
        grads["g_ffn_post"][i] = dg[0]
        grads["w_down"][i] = _mm_tn(sv["act"], df, name=f"mmw_dn_{i}")
        dx1, dg = _mm(dgu, W_gu, F32, nt=True, post_bwd=(sv["x1"], g_ffn_pre[i][None], dx), name=f"mmb_gu_{i}")
        grads["g_ffn_pre"][i] = dg[0]
        if i + 1 < depth:
            grads["w_gate_up"][i], recv[i + 1] = _mm_tn(sv["h2"], dgu, carry=(scatter_pack(i + 1), False), name=f"mmw_gu_{i}")
        else:
            grads["w_gate_up"][i] = _mm_tn(sv["h2"], dgu, name=f"mmw_gu_{i}")
        do, dcat, dg = _mm(dx1, W_o, BF16, nt=True, pre_bwd=(sv["o"], g_mix_post[i][None]), name=f"mmb_o_{i}")
        grads["g_mix_post"][i] = dg[0]
        grads["w_o"][i] = _mm_tn(sv["cat"], do, name=f"mmw_o_{i}")
        dqm, dkm, dvm, _, _ = _attn_bwd(sv["qm"], sv["km"], sv["vm"], sv["mo"], sv["mlse"], dcat[:, Q_W:], bq=min(256, S),
                                        tk=sv["km"].shape[1], heads=N_MEM_HEADS, kv_per_head=True, tag=f"m{i}")
        dmkv = jnp.concatenate([_unheads(dkm[:, None]), _unheads(dvm[:, None])], axis=1).astype(BF16)
        grads["w_mem_kv"][i] = _mm_tn(mem_n, dmkv, name=f"mmw_mkv_{i}")
        dmem_n = dmem_n + _mm(dmkv, W_mkv, F32, nt=True, name=f"mmb_mkv_{i}")[0]
        carry = (scatter_pack(0, REST), False) if i == 0 else None
        dq, dk, dv, dsink, recv0_rest = _mixer_bwd(kind, dcat, sv["pr"], sv["kv"], sv["cat"], sv["mix"], sink, i, carry)
        if dsink is not None:
            d_sink[i // N_MIXERS] = dsink
        dpr = jnp.concatenate([dq, dk, dv, dqm], axis=1) if kind == 2 else (dq, dk, dv, dqm)
        if kind == 1:
            dproj, dgc = _headprep_bwd(dpr, tab, shift, sv["proj"], _qk_gain_row(qk_norm_g[i // N_MIXERS]),
                                       name=f"headprep_bwd_{i}")
            d_qkg[i // N_MIXERS] = jnp.stack([dgc[0, :Q_W].reshape(N_TOK_HEADS, HEAD_DIM).sum(0),
                                              dgc[0, Q_W:QK_W].reshape(N_KV_HEADS, HEAD_DIM).sum(0)])
        else:
            dproj = _headprep_bwd(dpr, tab, shift, name=f"headprep_bwd_{i}")
        dx, dg = _mm(dproj, W_in, F32, nt=True, post_bwd=(sv["x"], g_mix_pre[i][None], dx1), name=f"mmb_in_{i}")
        grads["g_mix_pre"][i] = dg[0]
        grads["w_in"][i] = _mm_tn(sv["h"], dproj, name=f"mmw_in_{i}")
    _, dg_mem = _rms_bwd(mem[0], mem_norm_g[None], dmem_n, BF16, name="rmsb_mem")

    def update(received, l, names, tag):
        res = _reduce_adamw(received, *[layer_pack(pre, l, F32, names) for pre in ("", "m_", "v_")], name=f"adamw_{tag}")
        return [_unpack_local(r, shapes_of(names)) for r in res]

    recv0_first = _exchange(scatter_pack(0, FIRST), False, "scatter_g0_in")
    per_layer = [[a + b for a, b in zip(update(recv0_first, 0, FIRST, "0_in"), update(recv0_rest, 0, REST, "0"))]]
    per_layer += [update(recv[l], l, BIG, str(l)) for l in range(1, depth)]
    big_out = [[jnp.concatenate(ts, axis=0) for ts in zip(*[per_layer[l][j] for l in range(depth)])] for j in range(4)]

    small_grads = dict(mem_norm_g=dg_mem[0], g_mix_pre=jnp.stack(grads["g_mix_pre"]), g_mix_post=jnp.stack(grads["g_mix_post"]),
                       attn_sink=jnp.stack(d_sink), qk_norm_g=jnp.stack(d_qkg), g_ffn_pre=jnp.stack(grads["g_ffn_pre"]),
                       g_ffn_post=jnp.stack(grads["g_ffn_post"]))
    sg = _pack_small([small_grads[n] for n in SMALL])
    srecv = _exchange(sg, True, "gather_small_grads")
    spacked = lambda pre: _pack_small([given[pre + n] for n in SMALL])
    gs, ds, ms, vs = _reduce_adamw(srecv, spacked(""), spacked("m_"), spacked("v_"), name="adamw_small")

    out = {}
    for pre, fb, fs in zip(("grad_", "delta_", "new_m_", "new_v_"), big_out, (gs, ds, ms, vs)):
        for (n, _), t in zip(BIG, fb):
            out[pre + n] = t
        for n, t in zip(SMALL, _unpack_small(fs, [given[n].shape for n in SMALL])):
            out[pre + n] = t
    order = ("mem_norm_g", "w_in", "w_mem_kv", "w_o", "g_mix_pre", "g_mix_post", "attn_sink", "qk_norm_g", "w_gate_up",
             "w_down", "g_ffn_pre", "g_ffn_post")
    return (loss, dx[None], *[out[p + n] for p in ("grad_", "delta_", "new_m_", "new_v_") for n in order])
```

```python
import functools

import jax
import jax.numpy as jnp
from jax import lax
from jax.experimental import pallas as pl
from jax.experimental.pallas import tpu as pltpu

F32 = jnp.float32
BF16 = jnp.bfloat16

HEAD_DIM = 64
N_TOK_HEADS = 12
N_KV_HEADS = 3
GQA_GROUP = 4
N_MEM_HEADS = 4
Q_W = N_TOK_HEADS * HEAD_DIM
KV_W = N_KV_HEADS * HEAD_DIM
QM_W = N_MEM_HEADS * HEAD_DIM
QK_W = Q_W + KV_W
IN_W = Q_W + 2 * KV_W + QM_W
N_MIXERS = 3
A_RADIUS = 128
C_GROUPS = ((128, 1), (512, 4), (2048, 16))
ROPE_THETA = 500000.0
ROPE_DIMS = HEAD_DIM // 4
AXIAL_THETA = 10000.0
GRID_W = 64
EPS = 1e-6
ATTN_SCALE = HEAD_DIM ** -0.5
NEG = -1e30

ADAM_LR = 0.001
ADAM_B1 = 0.9
ADAM_B2 = 0.999
ADAM_EPS = 1e-08
ADAM_WD = 0.01
ADAM_STEP = 10

N_DEV = 8
LANES = 128
VMEM_LIMIT = 56 * 1024 * 1024
MESH = pl.DeviceIdType.MESH
MM_CHAIN_ROWS = 256
MM_ROWS_WIDE, MM_ROWS_NARROW = 512, 1024
MM_NARROW_COLS = 1408
NT_DIMS = (((1,), (1,)), ((), ()))
TN_DIMS = (((0,), (0,)), ((), ()))


def _pc(body, **kw):
    return pl.pallas_call(body, **kw)


def _cp(*sem):
    return pltpu.CompilerParams(dimension_semantics=sem, vmem_limit_bytes=VMEM_LIMIT)


def _row_tile(m, cap=512):
    t = cap
    while m % t:
        t //= 2
    return t


def _rms_fwd(x, g, out_dtype, name="rms_fwd"):
    M, D = x.shape
    tm = _row_tile(M)

    def body(x_ref, g_ref, o_ref):
        xv = x_ref[...]
        y = xv * lax.rsqrt(jnp.mean(xv * xv, axis=-1, keepdims=True) + EPS) * g_ref[...]
        o_ref[...] = y.astype(o_ref.dtype)

    row = pl.BlockSpec((tm, D), lambda i: (i, 0))
    vec = pl.BlockSpec((1, D), lambda i: (0, 0))
    return _pc(body, name=name, out_shape=jax.ShapeDtypeStruct((M, D), out_dtype), grid=(M // tm,),
               in_specs=[row, vec], out_specs=row, compiler_params=_cp("parallel"))(x, g)


def _rms_bwd_tile(xv, g, d):
    r = lax.rsqrt(jnp.mean(xv * xv, axis=-1, keepdims=True) + EPS)
    xh = xv * r
    dxh = d * g
    return r * (dxh - xh * jnp.mean(dxh * xh, axis=-1, keepdims=True)), jnp.sum(d * xh, axis=0, keepdims=True)


def _rms_bwd(x, g, dy, out_dtype, name="rms_bwd"):
    M, D = x.shape
    tm = _row_tile(M)

    def body(x_ref, g_ref, dy_ref, dx_ref, dg_ref):
        dx, dg = _rms_bwd_tile(x_ref[...], g_ref[...], dy_ref[...].astype(F32))
        dx_ref[...] = dx.astype(dx_ref.dtype)

        @pl.when(pl.program_id(0) == 0)
        def _():
            dg_ref[...] = jnp.zeros_like(dg_ref)

        dg_ref[...] += dg

    row = pl.BlockSpec((tm, D), lambda i: (i, 0))
    vec = pl.BlockSpec((1, D), lambda i: (0, 0))
    return _pc(body, name=name,
               out_shape=(jax.ShapeDtypeStruct((M, D), out_dtype), jax.ShapeDtypeStruct((1, D), F32)),
               grid=(M // tm,), in_specs=[row, vec, row], out_specs=(row, vec), compiler_params=_cp("arbitrary"))(x, g, dy)


def _loss_head(y, t, name="loss_head"):
    M, D = y.shape
    tm = _row_tile(M)

    def body(y_ref, t_ref, dy_ref, acc_ref):
        e = y_ref[...] - t_ref[...]
        dy_ref[...] = e * (1.0 / D)

        @pl.when(pl.program_id(0) == 0)
        def _():
            acc_ref[...] = jnp.zeros_like(acc_ref)

        acc_ref[...] += jnp.sum(e * e)

    row = pl.BlockSpec((tm, D), lambda i: (i, 0))
    return _pc(body, name=name,
               out_shape=(jax.ShapeDtypeStruct((M, D), F32), jax.ShapeDtypeStruct((8, LANES), F32)),
               grid=(M // tm,), in_specs=[row, row],
               out_specs=(row, pl.BlockSpec((8, LANES), lambda i: (0, 0))), compiler_params=_cp("arbitrary"))(y, t)


def _mm(a, w, out_dtype, nt=False, pre_g=None, swiglu=False, post=None, gu=None, pre_bwd=None, post_bwd=None, carry=None,
        name="mm"):
    M, K = a.shape
    N = w.shape[0] if nt else w.shape[1]
    tm = _row_tile(M, MM_ROWS_NARROW if N <= MM_NARROW_COLS and K <= 2 * MM_NARROW_COLS else MM_ROWS_WIDE)
    gain_grad = pre_bwd is not None or post_bwd is not None

    cr = min(MM_CHAIN_ROWS, tm)
    n_chains = tm // cr

    def body(*refs):
        refs = list(refs)
        a_ref, w_ref = refs.pop(0), refs.pop(0)
        pg_ref = refs.pop(0) if pre_g is not None else None
        g_ref, r_ref = (refs.pop(0), refs.pop(0)) if post is not None else (None, None)
        gu_ref = refs.pop(0) if gu is not None else None
        bwd_refs = [refs.pop(0) for _ in (pre_bwd or post_bwd or ())]
        lhs_ref = refs.pop(0) if pre_g is not None or pre_bwd is not None else None
        o_ref = refs.pop(0)
        act_ref = refs.pop(0) if swiglu else None
        x_ref = refs.pop(0) if post is not None else None
        st = [dict() for _ in range(n_chains)]

        def left(c):
            r = slice(c * cr, (c + 1) * cr)
            lhs = a_ref[r, :]
            if pre_g is not None:
                lhs = (lhs * lax.rsqrt(jnp.mean(lhs * lhs, axis=-1, keepdims=True) + EPS) * pg_ref[...]).astype(BF16)
                lhs_ref[r, :] = lhs
            if pre_bwd is not None:
                lhs, st[c]["dg"] = _rms_bwd_tile(bwd_refs[0][r, :], bwd_refs[1][...], lhs.astype(F32))
                lhs = lhs.astype(BF16)
                lhs_ref[r, :] = lhs
            st[c]["lhs"] = lhs

        def product(c):
            if nt:
                st[c]["acc"] = lax.dot_general(st[c].pop("lhs"), w_ref[...], NT_DIMS, preferred_element_type=F32)
            else:
                st[c]["acc"] = jnp.dot(st[c].pop("lhs"), w_ref[...], preferred_element_type=F32)

        def result(c):
            r = slice(c * cr, (c + 1) * cr)
            acc = st[c].pop("acc")
            if post_bwd is not None:
                dx, st[c]["dg"] = _rms_bwd_tile(bwd_refs[0][r, :], bwd_refs[1][...], acc)
                o_ref[r, :] = bwd_refs[2][r, :] + dx
            elif gu is None:
                o_ref[r, :] = acc.astype(o_ref.dtype)
            else:
                gate = gu_ref[r, :N].astype(F32)
                sig = 1.0 / (1.0 + jnp.exp(-gate))
                o_ref[r, :N] = (acc * gu_ref[r, N:].astype(F32) * (sig * (1.0 + gate * (1.0 - sig)))).astype(o_ref.dtype)
                o_ref[r, N:] = (acc * (gate * sig)).astype(o_ref.dtype)
            if swiglu:
                gate = acc[:, : N // 2]
                act_ref[r, :] = (gate * (1.0 / (1.0 + jnp.exp(-gate))) * acc[:, N // 2:]).astype(BF16)
            if post is not None:
                y = acc * lax.rsqrt(jnp.mean(acc * acc, axis=-1, keepdims=True) + EPS) * g_ref[...]
                x_ref[r, :] = r_ref[r, :] + y

        _skewed(n_chains, (left, product, result))
        if gain_grad:
            dg_ref = refs.pop(0)

            @pl.when(pl.program_id(0) == 0)
            def _():
                dg_ref[...] = jnp.zeros_like(dg_ref)

            dg_ref[...] += functools.reduce(jnp.add, [d["dg"] for d in st])

    row = lambda n: pl.BlockSpec((tm, n), lambda i: (i, 0))
    vec = lambda n: pl.BlockSpec((1, n), lambda i: (0, 0))
    ins = [a, w]
    specs = [row(K), pl.BlockSpec(w.shape, lambda i: (0, 0), pipeline_mode=pl.Buffered(1))]
    outs, ospecs = [], []
    if pre_g is not None:
        ins, specs = ins + [pre_g], specs + [vec(K)]
    if pre_g is not None or pre_bwd is not None:
        outs, ospecs = outs + [jax.ShapeDtypeStruct((M, K), BF16)], ospecs + [row(K)]
    if post is not None:
        ins, specs = ins + list(post), specs + [vec(N), row(N)]
    if gu is not None:
        ins, specs = ins + [gu], specs + [row(2 * N)]
        outs, ospecs = outs + [jax.ShapeDtypeStruct((M, 2 * N), BF16)], ospecs + [row(2 * N)]
    else:
        outs, ospecs = outs + [jax.ShapeDtypeStruct((M, N), out_dtype)], ospecs + [row(N)]
    if pre_bwd is not None:
        ins, specs = ins + list(pre_bwd), specs + [row(K), vec(K)]
    if post_bwd is not None:
        ins, specs = ins + list(post_bwd), specs + [row(N), vec(N), row(N)]
    if swiglu:
        outs, ospecs = outs + [jax.ShapeDtypeStruct((M, N // 2), BF16)], ospecs + [row(N // 2)]
    if post is not None:
        outs, ospecs = outs + [jax.ShapeDtypeStruct((M, N), F32)], ospecs + [row(N)]
    if gain_grad:
        D = K if pre_bwd is not None else N
        outs, ospecs = outs + [jax.ShapeDtypeStruct((1, D), F32)], ospecs + [vec(D)]
    return _call(body, carry, ins, name=name, out_shape=outs, grid=(M // tm,), in_specs=specs, out_specs=ospecs,
                 sem=("arbitrary" if gain_grad else "parallel",))


def _mm_tn(a, b, carry=None, name="mm_tn"):
    S, M = a.shape
    N = b.shape[1]
    tm = M if M <= 1408 else M // 2
    tn = N if N <= 1408 else N // 4
    ts = _row_tile(S, 1024)

    def body(a_ref, b_ref, o_ref):
        @pl.when(pl.program_id(2) == 0)
        def _():
            o_ref[...] = jnp.zeros_like(o_ref)

        o_ref[...] += lax.dot_general(a_ref[...], b_ref[...], TN_DIMS, preferred_element_type=F32)

    res = _call(body, carry, [a, b], name=name, out_shape=[jax.ShapeDtypeStruct((M, N), F32)],
                grid=(M // tm, N // tn, S // ts),
                in_specs=[pl.BlockSpec((ts, tm), lambda i, j, s: (s, i)), pl.BlockSpec((ts, tn), lambda i, j, s: (s, j))],
                out_specs=[pl.BlockSpec((tm, tn), lambda i, j, s: (i, j))], sem=("parallel", "parallel", "arbitrary"))
    return res[0] if carry is None else res


def _rope_tables(S):
    pos = jnp.arange(S, dtype=jnp.int32)

    def table(p, n_dims, theta):
        inv = theta ** (-(jnp.arange(0, n_dims, 2, dtype=F32) / n_dims))
        ang = p.astype(F32)[:, None] * inv[None, :]
        return jnp.cos(ang), jnp.sin(ang)

    one = lambda n: jnp.ones((S, n), F32)
    zero = lambda n: jnp.zeros((S, n), F32)
    cp, sp = table(pos, ROPE_DIMS, ROPE_THETA)
    rest = HEAD_DIM - ROPE_DIMS
    part = (jnp.concatenate([cp, cp, one(rest)], 1), jnp.concatenate([zero(8), sp, zero(rest)], 1),
            jnp.concatenate([-sp, zero(8), zero(rest)], 1))
    cr, sr = table(pos // GRID_W, HEAD_DIM // 2, AXIAL_THETA)
    cc, sc = table(pos % GRID_W, HEAD_DIM // 2, AXIAL_THETA)
    axial = (jnp.concatenate([cr, cr, cc, cc], 1), jnp.concatenate([zero(16), sr, zero(16), sc], 1),
             jnp.concatenate([-sr, zero(16), -sc, zero(16)], 1))
    rep = LANES // HEAD_DIM
    return (tuple(jnp.tile(t, (1, rep)) for t in part), ROPE_DIMS // 2), (tuple(jnp.tile(t, (1, rep)) for t in axial), HEAD_DIM // 4)


def _seg_mats():
    col = jnp.arange(IN_W)[:, None] // HEAD_DIM
    e = (col == jnp.arange(LANES)[None, :]).astype(BF16)
    return e, e.T


def _qk_gain_row(qk_g):
    return jnp.concatenate([jnp.tile(qk_g[0], N_TOK_HEADS), jnp.tile(qk_g[1], N_KV_HEADS),
                            jnp.ones((IN_W - QK_W,), F32)])[None, :]


def _rope_cols(tabs, tm):
    col = lax.broadcasted_iota(jnp.int32, (tm, IN_W), 1)
    qk = col < QK_W
    c, s_lo, s_hi = (jnp.tile(t[...], (1, IN_W // LANES)) for t in tabs)
    return jnp.where(qk, c, 1.0), jnp.where(qk, s_lo, 0.0), jnp.where(qk, s_hi, 0.0), qk


def _seg_mean(v, e_ref, et_ref):
    def split_dot(t, m_ref):
        hi = t.astype(BF16)
        lo = (t - hi.astype(F32)).astype(BF16)
        return jnp.dot(hi, m_ref[...], preferred_element_type=F32) + jnp.dot(lo, m_ref[...], preferred_element_type=F32)

    return split_dot(split_dot(v, e_ref) * (1.0 / HEAD_DIM), et_ref)


def _headprep_fwd(proj, tabs, shift, qk_gain=None, kv_heads=False, name="headprep_fwd"):
    S = proj.shape[0]
    tm = _row_tile(S, 512)
    norm = qk_gain is not None

    def body(*refs):
        refs = list(refs)
        p_ref, c_ref, lo_ref, hi_ref = (refs.pop(0) for _ in range(4))
        g_ref, e_ref, et_ref = (refs.pop(0) for _ in range(3)) if norm else (None, None, None)
        o_ref = refs.pop(0)
        x = p_ref[...]
        c, s_lo, s_hi, qk = _rope_cols((c_ref, lo_ref, hi_ref), tm)
        if norm:
            r = lax.rsqrt(_seg_mean(x * x, e_ref, et_ref) + EPS)
            x = x * jnp.where(qk, r, 1.0) * g_ref[...]
        y = (x * c + pltpu.roll(x, shift, 1) * s_lo + pltpu.roll(x, IN_W - shift, 1) * s_hi).astype(o_ref.dtype)
        o_ref[...] = y
        if kv_heads:
            k_ref, v_ref = refs
            for h in range(N_KV_HEADS):
                k_ref[h] = y[:, Q_W + h * HEAD_DIM:Q_W + (h + 1) * HEAD_DIM]
                v_ref[h] = y[:, QK_W + h * HEAD_DIM:QK_W + (h + 1) * HEAD_DIM]

    row = pl.BlockSpec((tm, IN_W), lambda i: (i, 0))
    tab = pl.BlockSpec((tm, LANES), lambda i: (i, 0))
    ins = [proj, *tabs]
    specs = [row, tab, tab, tab]
    if norm:
        e, et = _seg_mats()
        ins += [qk_gain, e, et]
        specs += [pl.BlockSpec((1, IN_W), lambda i: (0, 0)), pl.BlockSpec((IN_W, LANES), lambda i: (0, 0)),
                  pl.BlockSpec((LANES, IN_W), lambda i: (0, 0))]
    out_shape, out_specs = [jax.ShapeDtypeStruct((S, IN_W), BF16)], [row]
    if kv_heads:
        out_shape += [jax.ShapeDtypeStruct((N_KV_HEADS, S, HEAD_DIM), BF16)] * 2
        out_specs += [pl.BlockSpec((N_KV_HEADS, tm, HEAD_DIM), lambda i: (0, i, 0))] * 2
    res = _pc(body, name=name, out_shape=tuple(out_shape), grid=(S // tm,),
              in_specs=specs, out_specs=tuple(out_specs), compiler_params=_cp("parallel"))(*ins)
    return res if kv_heads else res[0]


def _headprep_bwd(dpr, tabs, shift, proj=None, qk_gain=None, name="headprep_bwd"):
    parts = isinstance(dpr, (tuple, list))
    S = dpr[0].shape[0] if parts else dpr.shape[0]
    tm = _row_tile(S, 512)
    norm = qk_gain is not None

    def body(*refs):
        refs = list(refs)
        d_refs = [refs.pop(0) for _ in range(4 if parts else 1)]
        c_ref, lo_ref, hi_ref = (refs.pop(0) for _ in range(3))
        if norm:
            p_ref, g_ref, e_ref, et_ref, o_ref, dg_ref = refs
        else:
            (o_ref,) = refs
        if parts:
            dq_ref, dk_ref, dv_ref, dqm_ref = d_refs
            d = jnp.concatenate([dq_ref[...]] + [dk_ref[h] for h in range(N_KV_HEADS)]
                                + [dv_ref[h] for h in range(N_KV_HEADS)] + [dqm_ref[...]], axis=1).astype(F32)
        else:
            d = d_refs[0][...].astype(F32)
        c, s_lo, s_hi, qk = _rope_cols((c_ref, lo_ref, hi_ref), tm)
        dx = d * c + pltpu.roll(d * s_lo, IN_W - shift, 1) + pltpu.roll(d * s_hi, shift, 1)
        if norm:
            x = p_ref[...]
            r = lax.rsqrt(_seg_mean(x * x, e_ref, et_ref) + EPS)
            xh = x * r

            @pl.when(pl.program_id(0) == 0)
            def _():
                dg_ref[...] = jnp.zeros_like(dg_ref)

            dg_ref[...] += jnp.sum(jnp.where(qk, dx * xh, 0.0), axis=0, keepdims=True)
            dxh = dx * g_ref[...]
            dn = r * (dxh - xh * _seg_mean(dxh * xh, e_ref, et_ref))
            dx = jnp.where(qk, dn, dx)
        o_ref[...] = dx.astype(o_ref.dtype)

    row = pl.BlockSpec((tm, IN_W), lambda i: (i, 0))
    tab = pl.BlockSpec((tm, LANES), lambda i: (i, 0))
    vec = pl.BlockSpec((1, IN_W), lambda i: (0, 0))
    if parts:
        heads = pl.BlockSpec((N_KV_HEADS, tm, HEAD_DIM), lambda i: (0, i, 0))
        ins = [*dpr, *tabs]
        specs = [pl.BlockSpec((tm, Q_W), lambda i: (i, 0)), heads, heads, pl.BlockSpec((tm, QM_W), lambda i: (i, 0)), tab, tab, tab]
    else:
        ins = [dpr, *tabs]
        specs = [row, tab, tab, tab]
    out_shape = jax.ShapeDtypeStruct((S, IN_W), BF16)
    out_specs = row
    if norm:
        e, et = _seg_mats()
        ins += [proj, qk_gain, e, et]
        specs += [row, vec, pl.BlockSpec((IN_W, LANES), lambda i: (0, 0)), pl.BlockSpec((LANES, IN_W), lambda i: (0, 0))]
        out_shape = (out_shape, jax.ShapeDtypeStruct((1, IN_W), F32))
        out_specs = (row, vec)
    return _pc(body, name=name, out_shape=out_shape, grid=(S // tm,), in_specs=specs, out_specs=out_specs,
               compiler_params=_cp("arbitrary" if norm else "parallel"))(*ins)


CHAIN_ROWS_WIDE, CHAIN_ROWS_NARROW = 128, 256
CHAIN_NARROW_KEYS = 1024


def _skewed(n, stages):
    for t in range(n + len(stages) - 1):
        for s, stage in enumerate(stages):
            if 0 <= t - s < n:
                stage(t - s)


def _chain_slices(G, bq, keys):
    cr = min(CHAIN_ROWS_NARROW if keys <= CHAIN_NARROW_KEYS else CHAIN_ROWS_WIDE, bq)
    while bq % cr:
        cr //= 2
    per = bq // cr
    return [(c // per, slice((c % per) * cr, (c % per + 1) * cr), slice(c * cr, (c + 1) * cr)) for c in range(G * per)]


def _v_ones(v):
    return jnp.concatenate([v, jnp.ones(v.shape, v.dtype)], axis=1)


def _q_dims(q, k, heads, kv_per_head=False):
    if heads is None:
        return q.shape
    return (1 if kv_per_head else k.shape[0]), heads, q.shape[0], k.shape[2]


def _q_shape(heads, NB, G, L, HD):
    return (NB, G, L, HD) if heads is None else (L, NB * G * HD)


def _q_spec(heads, G, rows, HD, index):
    if heads is None:
        return pl.BlockSpec((1, G, rows, HD), lambda *ids: (index(*ids)[0], 0, index(*ids)[1], 0))
    return pl.BlockSpec((rows, G * HD), lambda *ids: index(*ids)[::-1])


def _q_at(heads, g, hr, HD):
    return (0, g, hr, slice(None)) if heads is None else (hr, slice(g * HD, (g + 1) * HD))


def _window(L, blk, radius):
    if radius is None:
        return L, None
    W = min(L, blk + 2 * radius)
    assert blk % radius == 0 and (L - W) % radius == 0
    return W, lambda n: radius * jnp.clip(n * (blk // radius) - 1, 0, (L - W) // radius)


def _win_specs(G, W, HD, start, with_g):
    E = pl.Element
    st = (lambda n: 0) if start is None else start
    if with_g:
        return pl.BlockSpec((E(1), E(G), E(W), E(HD)), lambda b, n: (b, 0, st(n), 0))
    return pl.BlockSpec((E(1), E(W), E(HD)), lambda b, n: (b, st(n), 0))


def _attn_delta(do, o, *, dlse=None, lse=None, sink=None, heads=None, name="attn_delta"):
    HD = HEAD_DIM
    (NB, G), L = (heads, do.shape[0]) if heads is not None else (do.shape[:2], do.shape[2])
    bl = _row_tile(L, 1024)

    def body(*refs):
        refs = list(refs)
        sink_ref = refs.pop(0) if sink is not None else None
        do_ref, o_ref = refs.pop(0), refs.pop(0)
        dlse_ref = refs.pop(0) if dlse is not None else None
        lse_ref = refs.pop(0) if sink is not None else None
        delta_ref = refs.pop(0)
        b = pl.program_id(0)
        if heads is None:
            delta = jnp.sum(do_ref[0].astype(F32) * o_ref[0].astype(F32), axis=-1, keepdims=True)
        else:
            prod = do_ref[...].astype(F32) * o_ref[...].astype(F32)
            delta = jnp.concatenate([jnp.sum(prod[:, g * HD:(g + 1) * HD], axis=-1, keepdims=True)[None] for g in range(G)])
        if dlse is not None:
            delta = delta - dlse_ref[0]
        delta_ref[0] = delta
        if sink is not None:
            ds_ref = refs.pop(0)

            @pl.when(pl.program_id(1) == 0)
            def _():
                ds_ref[...] = jnp.zeros_like(ds_ref)

            for g in range(G):
                ps = jnp.exp(sink_ref[b * G + g] - lse_ref[0, g]) * delta[g]
                ds_ref[0, g] -= jnp.sum(ps)

    blk = _q_spec(None if heads is None else G, G, bl, HD, lambda b, n: (b, n))
    col = pl.BlockSpec((1, G, bl, 1), lambda b, n: (b, 0, n, 0))
    ins, specs = [do, o], [blk, blk]
    if dlse is not None:
        ins, specs = ins + [dlse], specs + [col]
    out_shape = jax.ShapeDtypeStruct((NB, G, L, 1), F32)
    out_specs = col
    if sink is not None:
        ins, specs = [sink] + ins + [lse], [pl.BlockSpec(memory_space=pltpu.SMEM)] + specs + [col]
        out_shape = (out_shape, jax.ShapeDtypeStruct((NB, G, 1, LANES), F32))
        out_specs = (col, pl.BlockSpec((1, G, 1, LANES), lambda b, n: (b, 0, 0, 0)))
    return _pc(body, name=name, out_shape=out_shape, grid=(NB, L // bl), in_specs=specs, out_specs=out_specs,
               compiler_params=_cp("parallel", "arbitrary"))(*ins)


def _attn_fwd_full(q, k, v, *, bq, tk, heads=None, name="attn_fwd_full"):
    NB, G, L, HD = _q_dims(q, k, heads)
    Lk = k.shape[1]
    nq, nk = L // bq, Lk // tk
    rows = G * bq
    chains = _chain_slices(G, bq, min(tk, CHAIN_NARROW_KEYS))

    def body(q_ref, k_ref, v_ref, o_ref, lse_ref, m_sc, acc_sc, q_sc):
        j = pl.program_id(2)

        @pl.when(j == 0)
        def _():
            m_sc[...] = jnp.full_like(m_sc, NEG)
            acc_sc[...] = jnp.zeros_like(acc_sc)
            for g, hr, sl in chains:
                q_sc[sl] = q_ref[_q_at(heads, g, hr, HD)] * ATTN_SCALE

        kk = k_ref[0]
        vv = _v_ones(v_ref[0])
        st = [dict() for _ in chains]

        def scores(c):
            st[c]["s"] = lax.dot_general(q_sc[chains[c][2]], kk, NT_DIMS, preferred_element_type=F32)

        def softmax(c):
            sl = chains[c][2]
            m_prev = m_sc[sl]
            m_new = jnp.maximum(m_prev, jnp.max(st[c]["s"], axis=1, keepdims=True))
            st[c]["p"] = jnp.exp(st[c].pop("s") - m_new).astype(BF16)
            st[c]["alpha"] = jnp.exp(m_prev - m_new)
            m_sc[sl] = m_new

        def values(c):
            sl = chains[c][2]
            acc_sc[sl] = st[c].pop("alpha") * acc_sc[sl] + jnp.dot(st[c].pop("p"), vv, preferred_element_type=F32)

        _skewed(len(chains), (scores, softmax, values))

        @pl.when(j == nk - 1)
        def _():
            for g, hr, sl in chains:
                acc = acc_sc[sl]
                l = acc[:, HD:HD + 1]
                o_ref[_q_at(heads, g, hr, HD)] = (acc[:, :HD] / l).astype(o_ref.dtype)
                lse_ref[0, g, hr, :] = m_sc[sl] + jnp.log(l)

    qspec = _q_spec(heads, G, bq, HD, lambda b, n, j: (b, n))
    kspec = pl.BlockSpec((1, tk, HD), lambda b, n, j: (b, j, 0))
    return _pc(body, name=name,
               out_shape=(jax.ShapeDtypeStruct(_q_shape(heads, NB, G, L, HD), BF16), jax.ShapeDtypeStruct((NB, G, L, 1), F32)),
               grid=(NB, nq, nk), in_specs=[qspec, kspec, kspec],
               out_specs=(qspec, pl.BlockSpec((1, G, bq, 1), lambda b, n, j: (b, 0, n, 0))),
               scratch_shapes=[pltpu.VMEM((rows, 1), F32), pltpu.VMEM((rows, 2 * HD), F32), pltpu.VMEM((rows, HD), BF16)],
               compiler_params=_cp("parallel", "parallel", "arbitrary"))(q, k, v)


def _attn_bwd_full(q, k, v, do, lse, o, *, bq, tk, heads=None, kv_per_head=False, name="attn_bwd_full"):
    NB, G, L, HD = _q_dims(q, k, heads, kv_per_head)
    Lk = k.shape[1]
    nq, nk = L // bq, Lk // tk
    chains = _chain_slices(G, bq, tk)
    KH = G if kv_per_head else 1
    assert not kv_per_head or nk == 1

    def body(q_ref, k_ref, v_ref, do_ref, lse_ref, o_ref, dqp_ref, dk_ref, dv_ref, dk_sc, dv_sc):
        n = pl.program_id(2)

        @pl.when(n == 0)
        def _():
            dk_sc[...] = jnp.zeros_like(dk_sc)
            dv_sc[...] = jnp.zeros_like(dv_sc)

        kks = [k_ref[h] for h in range(KH)]
        vvs = [v_ref[h] for h in range(KH)]
        st = [dict() for _ in chains]

        def scores(c):
            g, hr, _ = chains[c]
            at = _q_at(heads, g, hr, HD)
            st[c]["q"] = q_ref[at] * ATTN_SCALE
            st[c]["do"] = do_ref[at]
            st[c]["s"] = lax.dot_general(st[c]["q"], kks[g % KH], NT_DIMS, preferred_element_type=F32)
            st[c]["dp"] = lax.dot_general(st[c]["do"], vvs[g % KH], NT_DIMS, preferred_element_type=F32)
            st[c]["delta"] = jnp.sum(st[c]["do"].astype(F32) * o_ref[at].astype(F32), axis=-1, keepdims=True)

        def softmax(c):
            g, hr, _ = chains[c]
            p = jnp.exp(st[c].pop("s") - lse_ref[0, g, hr, :])
            st[c]["ds"] = (p * (st[c].pop("dp") - st[c].pop("delta"))).astype(BF16)
            st[c]["p"] = p.astype(BF16)

        def grads(c):
            g, hr, _ = chains[c]
            ds = st[c].pop("ds")
            dv_sc[g % KH] += lax.dot_general(st[c].pop("p"), st[c].pop("do"), TN_DIMS, preferred_element_type=F32)
            dk_sc[g % KH] += lax.dot_general(ds, st[c].pop("q"), TN_DIMS, preferred_element_type=F32)
            dqp_ref[(0,) + _q_at(heads, g, hr, HD)] = jnp.dot(ds, kks[g % KH], preferred_element_type=F32) * ATTN_SCALE

        _skewed(len(chains), (scores, softmax, grads))

        @pl.when(n == nq - 1)
        def _():
            dk_ref[...] = dk_sc[...].astype(dk_ref.dtype)
            dv_ref[...] = dv_sc[...].astype(dv_ref.dtype)

    qspec = _q_spec(heads, G, bq, HD, lambda b, m, n: (b, n))
    cspec = pl.BlockSpec((1, G, bq, 1), lambda b, m, n: (b, 0, n, 0))
    kspec = pl.BlockSpec((KH, tk, HD), lambda b, m, n: (b, m, 0))
    kv_shape = jax.ShapeDtypeStruct((NB * KH, Lk, HD), BF16)
    if heads is None:
        pspec = pl.BlockSpec((1, 1, G, bq, HD), lambda b, m, n: (m, b, 0, n, 0))
    else:
        pspec = pl.BlockSpec((1, bq, G * HD), lambda b, m, n: (m, n, b))
    dqp, dk, dv = _pc(body, name=name,
                      out_shape=(jax.ShapeDtypeStruct((nk,) + _q_shape(heads, NB, G, L, HD), F32), kv_shape, kv_shape),
                      grid=(NB, nk, nq), in_specs=[qspec, kspec, kspec, qspec, cspec, qspec],
                      out_specs=(pspec, kspec, kspec),
                      scratch_shapes=[pltpu.VMEM((KH, tk, HD), F32), pltpu.VMEM((KH, tk, HD), F32)],
                      compiler_params=_cp("parallel", "parallel", "arbitrary"))(q, k, v, do, lse, o)
    if nk == 1:
        return dqp[0].astype(BF16), dk, dv
    bl = _row_tile(L, 512)

    def sum_body(p_ref, o_ref):
        acc = p_ref[0]
        for j in range(1, nk):
            acc = acc + p_ref[j]
        o_ref[...] = acc.astype(o_ref.dtype)

    if heads is None:
        pspec = pl.BlockSpec((nk, 1, G, bl, HD), lambda b, n: (0, b, 0, n, 0))
    else:
        pspec = pl.BlockSpec((nk, bl, G * HD), lambda b, n: (0, n, b))
    dq = _pc(sum_body, name=name + "_sum", out_shape=jax.ShapeDtypeStruct(_q_shape(heads, NB, G, L, HD), BF16),
             grid=(NB, L // bl), in_specs=[pspec], out_specs=_q_spec(heads, G, bl, HD, lambda b, n: (b, n)),
             compiler_params=_cp("parallel", "parallel"))(dqp)
    return dq, dk, dv


def _attn_fwd_win(q, k, v, *, radius, sink=None, bq, heads=None, kv_per_head=False, carry=None, name="attn_fwd_win"):
    NB, G, L, HD = _q_dims(q, k, heads, kv_per_head)
    W, start = _window(k.shape[1], bq, radius)
    chains = _chain_slices(G, bq, W)

    def body(*refs):
        if sink is not None:
            sink_ref, *refs = refs
        q_ref, k_ref, v_ref, o_ref, lse_ref = refs
        b, n = pl.program_id(0), pl.program_id(1)
        kks = [k_ref[g] for g in range(G)] if kv_per_head else [k_ref[0]] * G
        vvs = [_v_ones(v_ref[g]) for g in range(G)] if kv_per_head else [_v_ones(v_ref[0])] * G
        st = [dict() for _ in chains]

        def scores(c):
            g, hr, _ = chains[c]
            s = lax.dot_general(q_ref[_q_at(heads, g, hr, HD)] * ATTN_SCALE, kks[g], NT_DIMS, preferred_element_type=F32)
            if radius is not None:
                qpos = n * bq + hr.start + lax.broadcasted_iota(jnp.int32, (hr.stop - hr.start, 1), 0)
                kpos = start(n) + lax.broadcasted_iota(jnp.int32, (1, W), 1)
                s = jnp.where(jnp.abs(qpos - kpos) <= radius, s, NEG)
            st[c]["s"] = s

        def softmax(c):
            g = chains[c][0]
            m = jnp.max(st[c]["s"], axis=1, keepdims=True)
            if sink is not None:
                m = jnp.maximum(m, sink_ref[b * G + g])
            st[c]["p"] = jnp.exp(st[c].pop("s") - m).astype(BF16)
            st[c]["m"] = m

        def values(c):
            g, hr, _ = chains[c]
            acc = jnp.dot(st[c].pop("p"), vvs[g], preferred_element_type=F32)
            m = st[c].pop("m")
            l = acc[:, HD:HD + 1]
            if sink is not None:
                l = l + jnp.exp(sink_ref[b * G + g] - m)
            o_ref[_q_at(heads, g, hr, HD)] = (acc[:, :HD] / l).astype(o_ref.dtype)
            lse_ref[0, g, hr, :] = m + jnp.log(l)

        _skewed(len(chains), (scores, softmax, values))

    qspec = _q_spec(heads, G, bq, HD, lambda b, n: (b, n))
    kspec = pl.BlockSpec((G, W, HD), lambda b, n: (0, 0, 0)) if kv_per_head else _win_specs(G, W, HD, start, False)
    ins, specs = [q, k, v], [qspec, kspec, kspec]
    if sink is not None:
        ins, specs = [sink] + ins, [pl.BlockSpec(memory_space=pltpu.SMEM)] + specs
    return _call(body, carry, ins, name=name,
                 out_shape=(jax.ShapeDtypeStruct(_q_shape(heads, NB, G, L, HD), BF16), jax.ShapeDtypeStruct((NB, G, L, 1), F32)),
                 grid=(NB, L // bq), in_specs=specs,
                 out_specs=(qspec, pl.BlockSpec((1, G, bq, 1), lambda b, n: (b, 0, n, 0))), sem=("parallel", "parallel"))


def _attn_dq_win(q, k, v, do, lse, delta, *, radius, bq, heads=None, name="attn_dq_win"):
    NB, G, L, HD = _q_dims(q, k, heads)
    W, start = _window(L, bq, radius)
    chains = _chain_slices(G, bq, W)

    def body(q_ref, k_ref, v_ref, do_ref, lse_ref, dl_ref, dq_ref):
        n = pl.program_id(1)
        kk, vv = k_ref[0], v_ref[0]
        kpos = start(n) + lax.broadcasted_iota(jnp.int32, (1, W), 1)
        st = [dict() for _ in chains]

        def scores(c):
            g, hr, _ = chains[c]
            at = _q_at(heads, g, hr, HD)
            st[c]["s"] = lax.dot_general(q_ref[at] * ATTN_SCALE, kk, NT_DIMS, preferred_element_type=F32)
            st[c]["dp"] = lax.dot_general(do_ref[at], vv, NT_DIMS, preferred_element_type=F32)

        def softmax(c):
            g, hr, _ = chains[c]
            qpos = n * bq + hr.start + lax.broadcasted_iota(jnp.int32, (hr.stop - hr.start, 1), 0)
            p = jnp.where(jnp.abs(qpos - kpos) <= radius, jnp.exp(st[c].pop("s") - lse_ref[0, g, hr, :]), 0.0)
            st[c]["ds"] = (p * (st[c].pop("dp") - dl_ref[0, g, hr, :])).astype(BF16)

        def grads(c):
            g, hr, _ = chains[c]
            dq = jnp.dot(st[c].pop("ds"), kk, preferred_element_type=F32) * ATTN_SCALE
            dq_ref[_q_at(heads, g, hr, HD)] = dq.astype(dq_ref.dtype)

        _skewed(len(chains), (scores, softmax, grads))

    qspec = _q_spec(heads, G, bq, HD, lambda b, n: (b, n))
    cspec = pl.BlockSpec((1, G, bq, 1), lambda b, n: (b, 0, n, 0))
    kspec = _win_specs(G, W, HD, start, False)
    return _pc(body, name=name, out_shape=jax.ShapeDtypeStruct(_q_shape(heads, NB, G, L, HD), BF16), grid=(NB, L // bq),
               in_specs=[qspec, kspec, kspec, qspec, cspec, cspec], out_specs=qspec,
               compiler_params=_cp("parallel", "parallel"))(q, k, v, do, lse, delta)


def _attn_dkv_win(q, k, v, do, lse, delta, *, radius, bk, heads=None, carry=None, name="attn_dkv_win"):
    NB, G, L, HD = _q_dims(q, k, heads)
    W, start = _window(L, bk, radius)
    chains = _chain_slices(G, W, bk)

    def body(q_ref, k_ref, v_ref, do_ref, lse_ref, dl_ref, dk_ref, dv_ref):
        m = pl.program_id(1)
        kk, vv = k_ref[0], v_ref[0]
        kpos = m * bk + lax.broadcasted_iota(jnp.int32, (1, bk), 1)
        st = [dict() for _ in chains]
        out = dict(dk=jnp.zeros((bk, HD), F32), dv=jnp.zeros((bk, HD), F32))

        def scores(c):
            g, hr, _ = chains[c]
            at = _q_at(heads, g, hr, HD)
            st[c]["q"] = q_ref[at] * ATTN_SCALE
            st[c]["do"] = do_ref[at]
            st[c]["s"] = lax.dot_general(st[c]["q"], kk, NT_DIMS, preferred_element_type=F32)
            st[c]["dp"] = lax.dot_general(st[c]["do"], vv, NT_DIMS, preferred_element_type=F32)

        def softmax(c):
            g, hr, _ = chains[c]
            qpos = start(m) + hr.start + lax.broadcasted_iota(jnp.int32, (hr.stop - hr.start, 1), 0)
            p = jnp.where(jnp.abs(qpos - kpos) <= radius, jnp.exp(st[c].pop("s") - lse_ref[0, g, hr, :]), 0.0)
            st[c]["ds"] = (p * (st[c].pop("dp") - dl_ref[0, g, hr, :])).astype(BF16)
            st[c]["p"] = p.astype(BF16)

        def grads(c):
            out["dv"] = out["dv"] + lax.dot_general(st[c].pop("p"), st[c].pop("do"), TN_DIMS, preferred_element_type=F32)
            out["dk"] = out["dk"] + lax.dot_general(st[c].pop("ds"), st[c].pop("q"), TN_DIMS, preferred_element_type=F32)

        _skewed(len(chains), (scores, softmax, grads))
        dk_ref[0] = out["dk"].astype(dk_ref.dtype)
        dv_ref[0] = out["dv"].astype(dv_ref.dtype)

    if heads is None:
        qspec = _win_specs(G, W, HD, start, True)
    else:
        qspec = pl.BlockSpec((pl.Element(W), pl.Element(G * HD)), lambda b, m: (start(m), b * G * HD))
    cspec = _win_specs(G, W, 1, start, True)
    kspec = pl.BlockSpec((1, bk, HD), lambda b, m: (b, m, 0))
    kv_shape = jax.ShapeDtypeStruct((NB, L, HD), BF16)
    return _call(body, carry, [q, k, v, do, lse, delta], name=name, out_shape=(kv_shape, kv_shape), grid=(NB, L // bk),
                 in_specs=[qspec, kspec, kspec, qspec, cspec, cspec], out_specs=(kspec, kspec), sem=("parallel", "parallel"))


def _attn(q, k, v, *, radius=None, sink=None, bq, tk=None, heads=None, kv_per_head=False, carry=None, tag):
    if radius is None and tk < k.shape[1]:
        return _attn_fwd_full(q, k, v, bq=bq, tk=tk, heads=heads, name=f"attn_fwd_{tag}")
    return _attn_fwd_win(q, k, v, radius=radius, sink=sink, bq=bq, heads=heads, kv_per_head=kv_per_head, carry=carry,
                         name=f"attn_fwd_{tag}")


def _attn_bwd(q, k, v, o, lse, do, *, radius=None, sink=None, dlse=None, bq, tk=None, heads=None, kv_per_head=False,
              carry=None, tag):
    if radius is None:
        assert carry is None
        return (*_attn_bwd_full(q, k, v, do, lse, o, bq=bq, tk=tk, heads=heads, kv_per_head=kv_per_head,
                                name=f"attn_bwd_{tag}"), None, None)
    nbg = None if heads is None else (k.shape[0], heads)
    if sink is not None:
        delta, ds = _attn_delta(do, o, lse=lse, sink=sink, heads=nbg, name=f"attn_delta_{tag}")
        dsink = ds[:, :, 0, 0].reshape(-1)
    else:
        delta, dsink = _attn_delta(do, o, dlse=dlse, heads=nbg, name=f"attn_delta_{tag}"), None
    dq = _attn_dq_win(q, k, v, do, lse, delta, radius=radius, bq=bq, heads=heads, name=f"attn_dq_{tag}")
    dk, dv, *arrived = _attn_dkv_win(q, k, v, do, lse, delta, radius=radius, bk=bq, heads=heads, carry=carry,
                                     name=f"attn_dkv_{tag}")
    return dq, dk, dv, dsink, (arrived[0] if arrived else None)


def _combine_fwd(o, lse, name="combine_fwd"):
    H, S, HD = o.shape
    tm = _row_tile(S, 512)

    def body(o_ref, lse_ref, t_ref):
        for g in range(GQA_GROUP):
            hs = [kv * GQA_GROUP + g for kv in range(N_KV_HEADS)]
            ls = [lse_ref[h] for h in hs]
            mx = functools.reduce(jnp.maximum, ls)
            es = [jnp.exp(l - mx) for l in ls]
            den = functools.reduce(jnp.add, es)
            for h, e in zip(hs, es):
                t_ref[h] = (o_ref[h].astype(F32) * (e / den)).astype(t_ref.dtype)

    blk = pl.BlockSpec((H, tm, HD), lambda i: (0, i, 0))
    col = pl.BlockSpec((H, tm, 1), lambda i: (0, i, 0))
    return _pc(body, name=name, out_shape=jax.ShapeDtypeStruct((H, S, HD), BF16), grid=(S // tm,),
               in_specs=[blk, col], out_specs=blk, compiler_params=_cp("parallel"))(o, lse)


def _combine_bwd(dt, o, lse, name="combine_bwd"):
    H, S, HD = o.shape
    tm = _row_tile(S, 512)

    def body(dt_ref, o_ref, lse_ref, do_ref, dlse_ref):
        for g in range(GQA_GROUP):
            hs = [kv * GQA_GROUP + g for kv in range(N_KV_HEADS)]
            ls = [lse_ref[h] for h in hs]
            mx = functools.reduce(jnp.maximum, ls)
            es = [jnp.exp(l - mx) for l in ls]
            den = functools.reduce(jnp.add, es)
            al = [e / den for e in es]
            dts = [dt_ref[h].astype(F32) for h in hs]
            da = [jnp.sum(d * o_ref[h].astype(F32), axis=-1, keepdims=True) for h, d in zip(hs, dts)]
            dot = functools.reduce(jnp.add, [a * d for a, d in zip(al, da)])
            for h, a, d, dd in zip(hs, al, da, dts):
                do_ref[h] = (dd * a).astype(do_ref.dtype)
                dlse_ref[h] = a * (d - dot)

    blk = pl.BlockSpec((H, tm, HD), lambda i: (0, i, 0))
    col = pl.BlockSpec((H, tm, 1), lambda i: (0, i, 0))
    return _pc(body, name=name,
               out_shape=(jax.ShapeDtypeStruct((H, S, HD), BF16), jax.ShapeDtypeStruct((H, S, 1), F32)),
               grid=(S // tm,), in_specs=[blk, blk, col], out_specs=(blk, col), compiler_params=_cp("parallel"))(dt, o, lse)


def _position():
    x, y, c = lax.axis_index("x"), lax.axis_index("y"), lax.axis_index("c")
    return x, y, c


def _peer(pos, k):
    x, y, c = pos
    return (1 - x if k & 4 else x, 1 - y if k & 2 else y, 1 - c if k & 1 else c)


def _linear(p):
    return 4 * p[0] + 2 * p[1] + p[2]


def _exchange_steps(s_ref, r_ref, send_sems, recv_sems, local_sem, gather):
    pos = _position()
    me = _linear(pos)
    own = pltpu.make_async_copy(s_ref if gather else s_ref.at[me], r_ref.at[me], local_sem)
    peers = range(1, N_DEV)

    def sems(k):
        return dict(send_sem=send_sems.at[k - 1], recv_sem=recv_sems.at[k - 1], device_id=_peer(pos, k), device_id_type=MESH)

    def send(k):
        src = s_ref if gather else s_ref.at[_linear(_peer(pos, k))]
        return pltpu.make_async_remote_copy(src_ref=src, dst_ref=r_ref.at[me], **sems(k))

    def arrival(k):
        slot = r_ref.at[_linear(_peer(pos, k))]
        return pltpu.make_async_remote_copy(src_ref=slot, dst_ref=slot, **sems(k))

    def start():
        own.start()
        for k in peers:
            send(k).start()

    def wait():
        for k in peers:
            arrival(k).wait_recv()
        for k in peers:
            send(k).wait_send()
        own.wait()

    return start, wait


EXCHANGE_SEMS = [pltpu.SemaphoreType.DMA((N_DEV - 1,)), pltpu.SemaphoreType.DMA((N_DEV - 1,)), pltpu.SemaphoreType.DMA]


def _exchange(buf, gather, name):
    def body(s_ref, r_ref, *sems):
        start, wait = _exchange_steps(s_ref, r_ref, *sems, gather)
        start()
        wait()

    hbm = pl.BlockSpec(memory_space=pltpu.HBM)
    out_shape = ((N_DEV,) + buf.shape) if gather else buf.shape
    return _pc(body, name=name, out_shape=jax.ShapeDtypeStruct(out_shape, buf.dtype), in_specs=[hbm], out_specs=hbm,
               scratch_shapes=list(EXCHANGE_SEMS))(buf)


def _call(body, carry, ins, *, name, out_shape, grid, in_specs, out_specs, scratch_shapes=(), sem):
    if carry is None:
        return _pc(body, name=name, out_shape=tuple(out_shape), grid=grid, in_specs=list(in_specs),
                   out_specs=tuple(out_specs), scratch_shapes=list(scratch_shapes), compiler_params=_cp(*sem))(*ins)
    buf, gather = carry
    n_in, n_out, n_sc = len(ins), len(out_shape), len(scratch_shapes)

    def wrapped(*refs):
        in_refs, buf_ref = refs[:n_in], refs[n_in]
        out_refs, recv_ref = refs[n_in + 1:n_in + 1 + n_out], refs[n_in + 1 + n_out]
        rest = refs[n_in + 2 + n_out:]
        first = functools.reduce(jnp.logical_and, [pl.program_id(a) == 0 for a in range(len(grid))])
        last = functools.reduce(jnp.logical_and, [pl.program_id(a) == grid[a] - 1 for a in range(len(grid))])

        @pl.when(first)
        def _():
            _exchange_steps(buf_ref, recv_ref, *rest[n_sc:], gather)[0]()

        body(*in_refs, *out_refs, *rest[:n_sc])

        @pl.when(last)
        def _():
            _exchange_steps(buf_ref, recv_ref, *rest[n_sc:], gather)[1]()

    hbm = pl.BlockSpec(memory_space=pltpu.HBM)
    recv_shape = ((N_DEV,) + buf.shape) if gather else buf.shape
    return _pc(wrapped, name=name, out_shape=(*out_shape, jax.ShapeDtypeStruct(recv_shape, buf.dtype)), grid=grid,
               in_specs=[*in_specs, hbm], out_specs=(*out_specs, hbm), scratch_shapes=[*scratch_shapes, *EXCHANGE_SEMS],
               compiler_params=_cp(*(("arbitrary",) * len(grid))))(*ins, buf)


def _reduce_adamw(recv, w, m, v, name):
    _, R, C = recv.shape
    tr = _row_tile(R, 512)

    def body(r_ref, w_ref, m_ref, v_ref, g_ref, d_ref, nm_ref, nv_ref):
        g = r_ref[0].astype(F32)
        for j in range(1, N_DEV):
            g = g + r_ref[j].astype(F32)
        g_ref[...] = g
        nm = ADAM_B1 * m_ref[...] + (1.0 - ADAM_B1) * g
        nv = ADAM_B2 * v_ref[...] + (1.0 - ADAM_B2) * jnp.square(g)
        m_hat = nm / (1.0 - ADAM_B1 ** ADAM_STEP)
        v_hat = nv / (1.0 - ADAM_B2 ** ADAM_STEP)
        d_ref[...] = -ADAM_LR * (m_hat / (jnp.sqrt(v_hat) + ADAM_EPS) + ADAM_WD * w_ref[...])
        nm_ref[...] = nm
        nv_ref[...] = nv

    row = pl.BlockSpec((tr, C), lambda i: (i, 0))
    out = jax.ShapeDtypeStruct((R, C), F32)
    return _pc(body, name=name, out_shape=(out, out, out, out), grid=(R // tr,),
               in_specs=[pl.BlockSpec((N_DEV, tr, C), lambda i: (0, i, 0)), row, row, row],
               out_specs=(row, row, row, row), compiler_params=_cp("parallel"))(recv, w, m, v)


BIG = (("w_in", 2), ("w_mem_kv", 1), ("w_o", 1), ("w_gate_up", 2), ("w_down", 1))
SMALL = ("mem_norm_g", "g_mix_pre", "g_mix_post", "attn_sink", "qk_norm_g", "g_ffn_pre", "g_ffn_post")
SMALL_W = 1024
FIRST, REST = BIG[:1], BIG[1:]


def _pack_local(shards, dtype):
    return jnp.concatenate([s.astype(dtype).reshape(-1, LANES) for s in shards], axis=0)


def _unpack_local(flat, shapes):
    out, r = [], 0
    for shp in shapes:
        n = shp[0] * shp[1] * shp[2] // LANES
        out.append(flat[r:r + n].reshape(shp))
        r += n
    return out


def _unpack_gathered(g, shapes, names=BIG):
    out, r = [], 0
    for (name, dim), shp in zip(names, shapes):
        n = shp[0] * shp[1] * shp[2] // LANES
        t = g[:, r:r + n].reshape((N_DEV,) + tuple(shp))
        if dim == 2:
            t = t.transpose(1, 2, 0, 3).reshape(shp[0], shp[1], N_DEV * shp[2])
        else:
            t = t.transpose(1, 0, 2, 3).reshape(shp[0], N_DEV * shp[1], shp[2])
        out.append(t)
        r += n
    return out


def _pack_for_scatter(full, shapes, dtype, names=BIG):
    parts = []
    for (name, dim), shp, t in zip(names, shapes, full):
        if dim == 2:
            t = t.reshape(shp[0], shp[1], N_DEV, shp[2]).transpose(2, 0, 1, 3)
        else:
            t = t.reshape(shp[0], N_DEV, shp[1], shp[2]).transpose(1, 0, 2, 3)
        parts.append(t.astype(dtype).reshape(N_DEV, -1, LANES))
    return jnp.concatenate(parts, axis=1)


def _pack_small(arrs):
    flat = jnp.concatenate([a.reshape(-1) for a in arrs])
    pad = (-flat.shape[0]) % (8 * SMALL_W)
    return jnp.pad(flat, (0, pad)).reshape(-1, SMALL_W)


def _unpack_small(flat, shapes):
    flat = flat.reshape(-1)
    out, r = [], 0
    for shp in shapes:
        n = 1
        for d in shp:
            n *= d
        out.append(flat[r:r + n].reshape(shp))
        r += n
    return out


def _heads(t, nb, g):
    S = t.shape[0]
    return t.reshape(S, nb, g, HEAD_DIM).transpose(1, 2, 0, 3)


def _unheads(t):
    nb, g, S, hd = t.shape
    return t.transpose(2, 0, 1, 3).reshape(S, nb * g * hd)


def _dilate(t, dil):
    S = t.shape[0]
    g = t.shape[1] // HEAD_DIM
    return t.reshape(S // dil, dil, g, HEAD_DIM).transpose(1, 2, 0, 3)


def _undilate(t):
    dil, g, L, w = t.shape
    return t.transpose(1, 2, 0, 3).reshape(g, L * dil, w)


FULL_BQ_FWD, FULL_TK_FWD = 512, 8192
FULL_BQ_BWD, FULL_TK_BWD = 1024, 2048


def _mixer_fwd(kind, pr, kv, sink, li, carry=None):
    S = pr.shape[0]
    if kind == 0:
        tok, lse, *arrived = _attn(pr, *kv, radius=A_RADIUS, sink=sink, bq=min(256, S), heads=GQA_GROUP, carry=carry,
                                   tag=f"a{li}")
        return tok, lse, (arrived[0] if arrived else None)
    assert carry is None
    if kind == 1:
        tok, lse = _attn(pr, *kv, bq=min(FULL_BQ_FWD, S), tk=min(FULL_TK_FWD, S), heads=GQA_GROUP, tag=f"b{li}")
        return tok, lse, None
    saved, outs, lses = [], [], []
    for g, (window, dil) in enumerate(C_GROUPS):
        q = _dilate(pr[:, g * GQA_GROUP * HEAD_DIM:(g + 1) * GQA_GROUP * HEAD_DIM], dil)
        k = _dilate(pr[:, Q_W + g * HEAD_DIM:Q_W + (g + 1) * HEAD_DIM], dil)[:, 0]
        v = _dilate(pr[:, QK_W + g * HEAD_DIM:QK_W + (g + 1) * HEAD_DIM], dil)[:, 0]
        o, lse = _attn(q, k, v, radius=window // (2 * dil), bq=min(256, S // dil), tag=f"c{li}g{g}")
        saved.append((q, k, v, o, lse))
        outs.append(_undilate(o))
        lses.append(_undilate(lse))
    o_all, lse_all = jnp.concatenate(outs, 0), jnp.concatenate(lses, 0)
    tok = _combine_fwd(o_all, lse_all, name=f"combine_fwd_{li}")
    return tok.transpose(1, 0, 2).reshape(S, Q_W), (saved, o_all, lse_all), None


def _mixer_bwd(kind, dcat, pr, kv, cat, saved, sink, li, carry=None):
    S = dcat.shape[0]
    if kind == 0:
        return _attn_bwd(pr, *kv, cat, saved, dcat, radius=A_RADIUS, sink=sink, bq=min(256, S), heads=GQA_GROUP, carry=carry,
                         tag=f"a{li}")
    assert carry is None
    if kind == 1:
        return _attn_bwd(pr, *kv, cat, saved, dcat, bq=min(FULL_BQ_BWD, S), tk=min(FULL_TK_BWD, S), heads=GQA_GROUP,
                         tag=f"b{li}")
    per_group, o_all, lse_all = saved
    dt = dcat[:, :Q_W].reshape(S, N_TOK_HEADS, HEAD_DIM).transpose(1, 0, 2)
    do_all, dlse_all = _combine_bwd(dt, o_all, lse_all, name=f"combine_bwd_{li}")
    dqs, dks, dvs = [], [], []
    for g, (window, dil) in enumerate(C_GROUPS):
        q, k, v, o, lse = per_group[g]
        L = S // dil
        hs = slice(g * GQA_GROUP, (g + 1) * GQA_GROUP)
        do = do_all[hs].reshape(GQA_GROUP, L, dil, HEAD_DIM).transpose(2, 0, 1, 3)
        dlse = dlse_all[hs].reshape(GQA_GROUP, L, dil, 1).transpose(2, 0, 1, 3)
        dq, dk, dv, _, _ = _attn_bwd(q, k, v, o, lse, do, radius=window // (2 * dil), dlse=dlse, bq=min(256, L),
                                     tag=f"c{li}g{g}")
        dqs.append(dq.transpose(2, 0, 1, 3).reshape(S, GQA_GROUP * HEAD_DIM))
        dks.append(dk.transpose(1, 0, 2).reshape(S, HEAD_DIM))
        dvs.append(dv.transpose(1, 0, 2).reshape(S, HEAD_DIM))
    return jnp.concatenate(dqs, 1), jnp.concatenate(dks, 1), jnp.concatenate(dvs, 1), None, None


def kernel(x, mem, mem_norm_g, w_in, w_mem_kv, w_o, g_mix_pre, g_mix_post, attn_sink, qk_norm_g, w_gate_up, w_down, g_ffn_pre, g_ffn_post, loss_target, m_mem_norm_g, m_w_in, m_w_mem_kv, m_w_o, m_g_mix_pre, m_g_mix_post, m_attn_sink, m_qk_norm_g, m_w_gate_up, m_w_down, m_g_ffn_pre, m_g_ffn_post, v_mem_norm_g, v_w_in, v_w_mem_kv, v_w_o, v_g_mix_pre, v_g_mix_post, v_attn_sink, v_qk_norm_g, v_w_gate_up, v_w_down, v_g_ffn_pre, v_g_ffn_post):
    given = dict(locals())
    depth = w_in.shape[0]
    S, D = x.shape[1], x.shape[2]
    def shapes_of(names):
        return [(1,) + tuple(given[n].shape[1:]) for n, _ in names]

    def layer_pack(pre, l, dtype, names=BIG):
        return _pack_local([given[pre + n][l:l + 1] for n, _ in names], dtype)

    def layer_weights(gathered, names=BIG):
        return [t[0] for t in _unpack_gathered(gathered, shapes_of(names), names)]

    W = [None] * depth
    W[0] = layer_weights(_exchange(layer_pack("", 0, BF16, FIRST), True, "gather_w0_in"), FIRST)

    tabs = _rope_tables(S)
    mem_n = _rms_fwd(mem[0], mem_norm_g[None], BF16, name="rms_mem")

    saved = []
    xc = x[0]
    for i in range(depth):
        kind = i % N_MIXERS
        (tab, shift) = tabs[1] if kind == 1 else tabs[0]
        sink = attn_sink[i // N_MIXERS] if kind == 0 else None
        qk_gain = _qk_gain_row(qk_norm_g[i // N_MIXERS]) if kind == 1 else None
        carry = (layer_pack("", 0, BF16, REST), True) if i == 0 else None
        h, proj, *arrived = _mm(xc, W[i][0], F32, pre_g=g_mix_pre[i][None], carry=carry, name=f"mm_in_{i}")
        if carry is not None:
            W[0] = W[0] + layer_weights(arrived[0], REST)
        W_in, W_mkv, W_o, W_gu, W_dn = W[i]
        if kind == 2:
            pr, kv = _headprep_fwd(proj, tab, shift, qk_gain, name=f"headprep_fwd_{i}"), None
        else:
            pr, *kv = _headprep_fwd(proj, tab, shift, qk_gain, kv_heads=True, name=f"headprep_fwd_{i}")
        carry = (layer_pack("", 1, BF16), True) if i == 0 and depth > 1 else None
        tok, mix_saved, arrived = _mixer_fwd(kind, pr, kv, sink, i, carry)
        if carry is not None:
            W[1] = layer_weights(arrived)
        (mkv,) = _mm(mem_n, W_mkv, BF16, name=f"mm_mkv_{i}")
        qm = pr[:, QK_W + KV_W:]
        km = _heads(mkv[:, :QM_W], N_MEM_HEADS, 1)[:, 0]
        vm = _heads(mkv[:, QM_W:], N_MEM_HEADS, 1)[:, 0]
        mo, mlse = _attn(qm, km, vm, bq=min(256, S), tk=km.shape[1], heads=N_MEM_HEADS, kv_per_head=True, tag=f"m{i}")
        cat = jnp.concatenate([tok, mo], axis=1)
        o, x1 = _mm(cat, W_o, F32, post=(g_mix_post[i][None], xc), name=f"mm_o_{i}")
        carry = (layer_pack("", i + 2, BF16), True) if i + 2 < depth else None
        h2, gu, act, *arrived = _mm(x1, W_gu, BF16, pre_g=g_ffn_pre[i][None], swiglu=True, carry=carry, name=f"mm_gu_{i}")
        if carry is not None:
            W[i + 2] = layer_weights(arrived[0])
        f, x2 = _mm(act, W_dn, F32, post=(g_ffn_post[i][None], x1), name=f"mm_dn_{i}")
        saved.append(dict(x=xc, h=h, proj=proj, pr=pr, kv=kv, mix=mix_saved, qm=qm, km=km, vm=vm, mo=mo, mlse=mlse, cat=cat, o=o,
                          x1=x1, h2=h2, gu=gu, act=act, f=f))
        xc = x2

    dy, sq = _loss_head(xc, loss_target[0], name="loss_head")
    loss = lax.psum(sq[0, 0] * (0.5 / D), ("x", "y", "c"))

    grads = {n: [None] * depth for n in ("w_in", "w_mem_kv", "w_o", "w_gate_up", "w_down", "g_mix_pre", "g_mix_post",
                                         "g_ffn_pre", "g_ffn_post")}
    d_sink = [jnp.zeros((N_TOK_HEADS,), F32) for _ in range(attn_sink.shape[0])]
    d_qkg = [jnp.zeros((2, HEAD_DIM), F32) for _ in range(qk_norm_g.shape[0])]
    dmem_n = jnp.zeros((mem.shape[1], D), F32)
    recv = [None] * depth

    def scatter_pack(l, names=BIG):
        return _pack_for_scatter([grads[n][l][None] for n, _ in names], shapes_of(names), BF16, names)

    dx = dy
    for i in reversed(range(depth)):
        kind = i % N_MIXERS
        sv = saved[i]
        (tab, shift) = tabs[1] if kind == 1 else tabs[0]
        sink = attn_sink[i // N_MIXERS] if kind == 0 else None
        W_in, W_mkv, W_o, W_gu, W_dn = W[i]
        df, dgu, dg = _mm(dx, W_dn, BF16, nt=True, gu=sv["gu"], pre_bwd=(sv["f"], g_ffn_post[i][None]), name=f"mmb_dn_{i}")
        grads["g_ffn_post"][i] = dg[0]
        grads["w_down"][i] = _mm_tn(sv["act"], df, name=f"mmw_dn_{i}")
        dx1, dg = _mm(dgu, W_gu, F32, nt=True, post_bwd=(sv["x1"], g_ffn_pre[i][None], dx), name=f"mmb_gu_{i}")
        grads["g_ffn_pre"][i] = dg[0]
        if i + 1 < depth:
            grads["w_gate_up"][i], recv[i + 1] = _mm_tn(sv["h2"], dgu, carry=(scatter_pack(i + 1), False), name=f"mmw_gu_{i}")
        else:
            grads["w_gate_up"][i] = _mm_tn(sv["h2"], dgu, name=f"mmw_gu_{i}")
        do, dcat, dg = _mm(dx1, W_o, BF16, nt=True, pre_bwd=(sv["o"], g_mix_post[i][None]), name=f"mmb_o_{i}")
        grads["g_mix_post"][i] = dg[0]
        grads["w_o"][i] = _mm_tn(sv["cat"], do, name=f"mmw_o_{i}")
        dqm, dkm, dvm, _, _ = _attn_bwd(sv["qm"], sv["km"], sv["vm"], sv["mo"], sv["mlse"], dcat[:, Q_W:], bq=min(256, S),
                                        tk=sv["km"].shape[1], heads=N_MEM_HEADS, kv_per_head=True, tag=f"m{i}")
        dmkv = jnp.concatenate([_unheads(dkm[:, None]), _unheads(dvm[:, None])], axis=1).astype(BF16)
        grads["w_mem_kv"][i] = _mm_tn(mem_n, dmkv, name=f"mmw_mkv_{i}")
        dmem_n = dmem_n + _mm(dmkv, W_mkv, F32, nt=True, name=f"mmb_mkv_{i}")[0]
        carry = (scatter_pack(0, REST), False) if i == 0 else None
        dq, dk, dv, dsink, recv0_rest = _mixer_bwd(kind, dcat, sv["pr"], sv["kv"], sv["cat"], sv["mix"], sink, i, carry)
        if dsink is not None:
            d_sink[i // N_MIXERS] = dsink
        dpr = jnp.concatenate([dq, dk, dv, dqm], axis=1) if kind == 2 else (dq, dk, dv, dqm)
        if kind == 1:
            dproj, dgc = _headprep_bwd(dpr, tab, shift, sv["proj"], _qk_gain_row(qk_norm_g[i // N_MIXERS]),
                                       name=f"headprep_bwd_{i}")
            d_qkg[i // N_MIXERS] = jnp.stack([dgc[0, :Q_W].reshape(N_TOK_HEADS, HEAD_DIM).sum(0),
                                              dgc[0, Q_W:QK_W].reshape(N_KV_HEADS, HEAD_DIM).sum(0)])
        else:
            dproj = _headprep_bwd(dpr, tab, shift, name=f"headprep_bwd_{i}")
        dx, dg = _mm(dproj, W_in, F32, nt=True, post_bwd=(sv["x"], g_mix_pre[i][None], dx1), name=f"mmb_in_{i}")
        grads["g_mix_pre"][i] = dg[0]
        grads["w_in"][i] = _mm_tn(sv["h"], dproj, name=f"mmw_in_{i}")
    _, dg_mem = _rms_bwd(mem[0], mem_norm_g[None], dmem_n, BF16, name="rmsb_mem")

    def update(received, l, names, tag):
        res = _reduce_adamw(received, *[layer_pack(pre, l, F32, names) for pre in ("", "m_", "v_")], name=f"adamw_{tag}")
        return [_unpack_local(r, shapes_of(names)) for r in res]

    recv0_first = _exchange(scatter_pack(0, FIRST), False, "scatter_g0_in")
    per_layer = [[a + b for a, b in zip(update(recv0_first, 0, FIRST, "0_in"), update(recv0_rest, 0, REST, "0"))]]
    per_layer += [update(recv[l], l, BIG, str(l)) for l in range(1, depth)]
    big_out = [[jnp.concatenate(ts, axis=0) for ts in zip(*[per_layer[l][j] for l in range(depth)])] for j in range(4)]

    small_grads = dict(mem_norm_g=dg_mem[0], g_mix_pre=jnp.stack(grads["g_mix_pre"]), g_mix_post=jnp.stack(grads["g_mix_post"]),
                       attn_sink=jnp.stack(d_sink), qk_norm_g=jnp.stack(d_qkg), g_ffn_pre=jnp.stack(grads["g_ffn_pre"]),
                       g_ffn_post=jnp.stack(grads["g_ffn_post"]))
    sg = _pack_small([small_grads[n] for n in SMALL])
    srecv = _exchange(sg, True, "gather_small_grads")
    spacked = lambda pre: _pack_small([given[pre + n] for n in SMALL])
    gs, ds, ms, vs = _reduce_adamw(srecv, spacked(""), spacked("m_"), spacked("v_"), name="adamw_small")

    out = {}
    for pre, fb, fs in zip(("grad_", "delta_", "new_m_", "new_v_"), big_out, (gs, ds, ms, vs)):
        for (n, _), t in zip(BIG, fb):
            out[pre + n] = t
        for n, t in zip(SMALL, _unpack_small(fs, [given[n].shape for n in SMALL])):
            out[pre + n] = t
    order = ("mem_norm_g", "w_in", "w_mem_kv", "w_o", "g_mix_pre", "g_mix_post", "attn_sink", "qk_norm_g", "w_gate_up",
             "w_down", "g_ffn_pre", "g_ffn_post")
    return (loss, dx[None], *[out[p + n] for p in ("grad_", "delta_", "new_m_", "new_v_") for n in order])
```

```python
import functools

import jax
import jax.numpy as jnp
from jax import lax
from jax.experimental import pallas as pl
from jax.experimental.pallas import tpu as pltpu

F32 = jnp.float32
BF16 = jnp.bfloat16

HEAD_DIM = 64
N_TOK_HEADS = 12
N_KV_HEADS = 3
GQA_GROUP = 4
N_MEM_HEADS = 4
Q_W = N_TOK_HEADS * HEAD_DIM
KV_W = N_KV_HEADS * HEAD_DIM
QM_W = N_MEM_HEADS * HEAD_DIM
QK_W = Q_W + KV_W
IN_W = Q_W + 2 * KV_W + QM_W
N_MIXERS = 3
A_RADIUS = 128
C_GROUPS = ((128, 1), (512, 4), (2048, 16))
ROPE_THETA = 500000.0
ROPE_DIMS = HEAD_DIM // 4
AXIAL_THETA = 10000.0
GRID_W = 64
EPS = 1e-6
ATTN_SCALE = HEAD_DIM ** -0.5
NEG = -1e30

ADAM_LR = 0.001
ADAM_B1 = 0.9
ADAM_B2 = 0.999
ADAM_EPS = 1e-08
ADAM_WD = 0.01
ADAM_STEP = 10

N_DEV = 8
LANES = 128
VMEM_LIMIT = 56 * 1024 * 1024
MESH = pl.DeviceIdType.MESH
MM_CHAIN_ROWS = 256
MM_ROWS_WIDE, MM_ROWS_NARROW = 512, 1024
MM_NARROW_COLS = 1408
NT_DIMS = (((1,), (1,)), ((), ()))
TN_DIMS = (((0,), (0,)), ((), ()))


def _pc(body, **kw):
    return pl.pallas_call(body, **kw)


def _cp(*sem):
    return pltpu.CompilerParams(dimension_semantics=sem, vmem_limit_bytes=VMEM_LIMIT)


def _row_tile(m, cap=512):
    t = cap
    while m % t:
        t //= 2
    return t


def _rms_fwd(x, g, out_dtype, name="rms_fwd"):
    M, D = x.shape
    tm = _row_tile(M)

    def body(x_ref, g_ref, o_ref):
        xv = x_ref[...]
        y = xv * lax.rsqrt(jnp.mean(xv * xv, axis=-1, keepdims=True) + EPS) * g_ref[...]
        o_ref[...] = y.astype(o_ref.dtype)

    row = pl.BlockSpec((tm, D), lambda i: (i, 0))
    vec = pl.BlockSpec((1, D), lambda i: (0, 0))
    return _pc(body, name=name, out_shape=jax.ShapeDtypeStruct((M, D), out_dtype), grid=(M // tm,),
               in_specs=[row, vec], out_specs=row, compiler_params=_cp("parallel"))(x, g)


def _rms_bwd_tile(xv, g, d):
    r = lax.rsqrt(jnp.mean(xv * xv, axis=-1, keepdims=True) + EPS)
    xh = xv * r
    dxh = d * g
    return r * (dxh - xh * jnp.mean(dxh * xh, axis=-1, keepdims=True)), jnp.sum(d * xh, axis=0, keepdims=True)


def _rms_bwd(x, g, dy, out_dtype, name="rms_bwd"):
    M, D = x.shape
    tm = _row_tile(M)

    def body(x_ref, g_ref, dy_ref, dx_ref, dg_ref):
        dx, dg = _rms_bwd_tile(x_ref[...], g_ref[...], dy_ref[...].astype(F32))
        dx_ref[...] = dx.astype(dx_ref.dtype)

        @pl.when(pl.program_id(0) == 0)
        def _():
            dg_ref[...] = jnp.zeros_like(dg_ref)

        dg_ref[...] += dg

    row = pl.BlockSpec((tm, D), lambda i: (i, 0))
    vec = pl.BlockSpec((1, D), lambda i: (0, 0))
    return _pc(body, name=name,
               out_shape=(jax.ShapeDtypeStruct((M, D), out_dtype), jax.ShapeDtypeStruct((1, D), F32)),
               grid=(M // tm,), in_specs=[row, vec, row], out_specs=(row, vec), compiler_params=_cp("arbitrary"))(x, g, dy)


def _loss_head(y, t, name="loss_head"):
    M, D = y.shape
    tm = _row_tile(M)

    def body(y_ref, t_ref, dy_ref, acc_ref):
        e = y_ref[...] - t_ref[...]
        dy_ref[...] = e * (1.0 / D)

        @pl.when(pl.program_id(0) == 0)
        def _():
            acc_ref[...] = jnp.zeros_like(acc_ref)

        acc_ref[...] += jnp.sum(e * e)

    row = pl.BlockSpec((tm, D), lambda i: (i, 0))
    return _pc(body, name=name,
               out_shape=(jax.ShapeDtypeStruct((M, D), F32), jax.ShapeDtypeStruct((8, LANES), F32)),
               grid=(M // tm,), in_specs=[row, row],
               out_specs=(row, pl.BlockSpec((8, LANES), lambda i: (0, 0))), compiler_params=_cp("arbitrary"))(y, t)


def _mm(a, w, out_dtype, nt=False, pre_g=None, swiglu=False, post=None, gu=None, pre_bwd=None, post_bwd=None, carry=None,
        name="mm"):
    M, K = a.shape
    N = w.shape[0] if nt else w.shape[1]
    tm = _row_tile(M, MM_ROWS_NARROW if N <= MM_NARROW_COLS and K <= 2 * MM_NARROW_COLS else MM_ROWS_WIDE)
    gain_grad = pre_bwd is not None or post_bwd is not None

    cr = min(MM_CHAIN_ROWS, tm)
    n_chains = tm // cr

    def body(*refs):
        refs = list(refs)
        a_ref, w_ref = refs.pop(0), refs.pop(0)
        pg_ref = refs.pop(0) if pre_g is not None else None
        g_ref, r_ref = (refs.pop(0), refs.pop(0)) if post is not None else (None, None)
        gu_ref = refs.pop(0) if gu is not None else None
        bwd_refs = [refs.pop(0) for _ in (pre_bwd or post_bwd or ())]
        lhs_ref = refs.pop(0) if pre_g is not None or pre_bwd is not None else None
        o_ref = refs.pop(0)
        act_ref = refs.pop(0) if swiglu else None
        x_ref = refs.pop(0) if post is not None else None
        st = [dict() for _ in range(n_chains)]

        def left(c):
            r = slice(c * cr, (c + 1) * cr)
            lhs = a_ref[r, :]
            if pre_g is not None:
                lhs = (lhs * lax.rsqrt(jnp.mean(lhs * lhs, axis=-1, keepdims=True) + EPS) * pg_ref[...]).astype(BF16)
                lhs_ref[r, :] = lhs
            if pre_bwd is not None:
                lhs, st[c]["dg"] = _rms_bwd_tile(bwd_refs[0][r, :], bwd_refs[1][...], lhs.astype(F32))
                lhs = lhs.astype(BF16)
                lhs_ref[r, :] = lhs
            st[c]["lhs"] = lhs

        def product(c):
            if nt:
                st[c]["acc"] = lax.dot_general(st[c].pop("lhs"), w_ref[...], NT_DIMS, preferred_element_type=F32)
            else:
                st[c]["acc"] = jnp.dot(st[c].pop("lhs"), w_ref[...], preferred_element_type=F32)

        def result(c):
            r = slice(c * cr, (c + 1) * cr)
            acc = st[c].pop("acc")
            if post_bwd is not None:
                dx, st[c]["dg"] = _rms_bwd_tile(bwd_refs[0][r, :], bwd_refs[1][...], acc)
                o_ref[r, :] = bwd_refs[2][r, :] + dx
            elif gu is None:
                o_ref[r, :] = acc.astype(o_ref.dtype)
            else:
                gate = gu_ref[r, :N].astype(F32)
                sig = 1.0 / (1.0 + jnp.exp(-gate))
                o_ref[r, :N] = (acc * gu_ref[r, N:].astype(F32) * (sig * (1.0 + gate * (1.0 - sig)))).astype(o_ref.dtype)
                o_ref[r, N:] = (acc * (gate * sig)).astype(o_ref.dtype)
            if swiglu:
                gate = acc[:, : N // 2]
                act_ref[r, :] = (gate * (1.0 / (1.0 + jnp.exp(-gate))) * acc[:, N // 2:]).astype(BF16)
            if post is not None:
                y = acc * lax.rsqrt(jnp.mean(acc * acc, axis=-1, keepdims=True) + EPS) * g_ref[...]
                x_ref[r, :] = r_ref[r, :] + y

        _skewed(n_chains, (left, product, result))
        if gain_grad:
            dg_ref = refs.pop(0)

            @pl.when(pl.program_id(0) == 0)
            def _():
                dg_ref[...] = jnp.zeros_like(dg_ref)

            dg_ref[...] += functools.reduce(jnp.add, [d["dg"] for d in st])

    row = lambda n: pl.BlockSpec((tm, n), lambda i: (i, 0))
    vec = lambda n: pl.BlockSpec((1, n), lambda i: (0, 0))
    ins = [a, w]
    specs = [row(K), pl.BlockSpec(w.shape, lambda i: (0, 0), pipeline_mode=pl.Buffered(1))]
    outs, ospecs = [], []
    if pre_g is not None:
        ins, specs = ins + [pre_g], specs + [vec(K)]
    if pre_g is not None or pre_bwd is not None:
        outs, ospecs = outs + [jax.ShapeDtypeStruct((M, K), BF16)], ospecs + [row(K)]
    if post is not None:
        ins, specs = ins + list(post), specs + [vec(N), row(N)]
    if gu is not None:
        ins, specs = ins + [gu], specs + [row(2 * N)]
        outs, ospecs = outs + [jax.ShapeDtypeStruct((M, 2 * N), BF16)], ospecs + [row(2 * N)]
    else:
        outs, ospecs = outs + [jax.ShapeDtypeStruct((M, N), out_dtype)], ospecs + [row(N)]
    if pre_bwd is not None:
        ins, specs = ins + list(pre_bwd), specs + [row(K), vec(K)]
    if post_bwd is not None:
        ins, specs = ins + list(post_bwd), specs + [row(N), vec(N), row(N)]
    if swiglu:
        outs, ospecs = outs + [jax.ShapeDtypeStruct((M, N // 2), BF16)], ospecs + [row(N // 2)]
    if post is not None:
        outs, ospecs = outs + [jax.ShapeDtypeStruct((M, N), F32)], ospecs + [row(N)]
    if gain_grad:
        D = K if pre_bwd is not None else N
        outs, ospecs = outs + [jax.ShapeDtypeStruct((1, D), F32)], ospecs + [vec(D)]
    return _call(body, carry, ins, name=name, out_shape=outs, grid=(M // tm,), in_specs=specs, out_specs=ospecs,
                 sem=("arbitrary" if gain_grad else "parallel",))


def _mm_tn(a, b, carry=None, name="mm_tn"):
    S, M = a.shape
    N = b.shape[1]
    tm = M if M <= 1408 else M // 2
    tn = N if N <= 2816 else N // 2
    ts = _row_tile(S, 1024)

    def body(a_ref, b_ref, o_ref):
        @pl.when(pl.program_id(2) == 0)
        def _():
            o_ref[...] = jnp.zeros_like(o_ref)

        o_ref[...] += lax.dot_general(a_ref[...], b_ref[...], TN_DIMS, preferred_element_type=F32)

    res = _call(body, carry, [a, b], name=name, out_shape=[jax.ShapeDtypeStruct((M, N), F32)],
                grid=(M // tm, N // tn, S // ts),
                in_specs=[pl.BlockSpec((ts, tm), lambda i, j, s: (s, i)), pl.BlockSpec((ts, tn), lambda i, j, s: (s, j))],
                out_specs=[pl.BlockSpec((tm, tn), lambda i, j, s: (i, j))], sem=("parallel", "parallel", "arbitrary"))
    return res[0] if carry is None else res


def _rope_tables(S):
    pos = jnp.arange(S, dtype=jnp.int32)

    def table(p, n_dims, theta):
        inv = theta ** (-(jnp.arange(0, n_dims, 2, dtype=F32) / n_dims))
        ang = p.astype(F32)[:, None] * inv[None, :]
        return jnp.cos(ang), jnp.sin(ang)

    one = lambda n: jnp.ones((S, n), F32)
    zero = lambda n: jnp.zeros((S, n), F32)
    cp, sp = table(pos, ROPE_DIMS, ROPE_THETA)
    rest = HEAD_DIM - ROPE_DIMS
    part = (jnp.concatenate([cp, cp, one(rest)], 1), jnp.concatenate([zero(8), sp, zero(rest)], 1),
            jnp.concatenate([-sp, zero(8), zero(rest)], 1))
    cr, sr = table(pos // GRID_W, HEAD_DIM // 2, AXIAL_THETA)
    cc, sc = table(pos % GRID_W, HEAD_DIM // 2, AXIAL_THETA)
    axial = (jnp.concatenate([cr, cr, cc, cc], 1), jnp.concatenate([zero(16), sr, zero(16), sc], 1),
             jnp.concatenate([-sr, zero(16), -sc, zero(16)], 1))
    rep = LANES // HEAD_DIM
    return (tuple(jnp.tile(t, (1, rep)) for t in part), ROPE_DIMS // 2), (tuple(jnp.tile(t, (1, rep)) for t in axial), HEAD_DIM // 4)


def _seg_mats():
    col = jnp.arange(IN_W)[:, None] // HEAD_DIM
    e = (col == jnp.arange(LANES)[None, :]).astype(BF16)
    return e, e.T


def _qk_gain_row(qk_g):
    return jnp.concatenate([jnp.tile(qk_g[0], N_TOK_HEADS), jnp.tile(qk_g[1], N_KV_HEADS),
                            jnp.ones((IN_W - QK_W,), F32)])[None, :]


def _rope_cols(tabs, tm):
    col = lax.broadcasted_iota(jnp.int32, (tm, IN_W), 1)
    qk = col < QK_W
    c, s_lo, s_hi = (jnp.tile(t[...], (1, IN_W // LANES)) for t in tabs)
    return jnp.where(qk, c, 1.0), jnp.where(qk, s_lo, 0.0), jnp.where(qk, s_hi, 0.0), qk


def _seg_mean(v, e_ref, et_ref):
    def split_dot(t, m_ref):
        hi = t.astype(BF16)
        lo = (t - hi.astype(F32)).astype(BF16)
        return jnp.dot(hi, m_ref[...], preferred_element_type=F32) + jnp.dot(lo, m_ref[...], preferred_element_type=F32)

    return split_dot(split_dot(v, e_ref) * (1.0 / HEAD_DIM), et_ref)


def _headprep_fwd(proj, tabs, shift, qk_gain=None, kv_heads=False, name="headprep_fwd"):
    S = proj.shape[0]
    tm = _row_tile(S, 512)
    norm = qk_gain is not None

    def body(*refs):
        refs = list(refs)
        p_ref, c_ref, lo_ref, hi_ref = (refs.pop(0) for _ in range(4))
        g_ref, e_ref, et_ref = (refs.pop(0) for _ in range(3)) if norm else (None, None, None)
        o_ref = refs.pop(0)
        x = p_ref[...]
        c, s_lo, s_hi, qk = _rope_cols((c_ref, lo_ref, hi_ref), tm)
        if norm:
            r = lax.rsqrt(_seg_mean(x * x, e_ref, et_ref) + EPS)
            x = x * jnp.where(qk, r, 1.0) * g_ref[...]
        y = (x * c + pltpu.roll(x, shift, 1) * s_lo + pltpu.roll(x, IN_W - shift, 1) * s_hi).astype(o_ref.dtype)
        o_ref[...] = y
        if kv_heads:
            k_ref, v_ref = refs
            for h in range(N_KV_HEADS):
                k_ref[h] = y[:, Q_W + h * HEAD_DIM:Q_W + (h + 1) * HEAD_DIM]
                v_ref[h] = y[:, QK_W + h * HEAD_DIM:QK_W + (h + 1) * HEAD_DIM]

    row = pl.BlockSpec((tm, IN_W), lambda i: (i, 0))
    tab = pl.BlockSpec((tm, LANES), lambda i: (i, 0))
    ins = [proj, *tabs]
    specs = [row, tab, tab, tab]
    if norm:
        e, et = _seg_mats()
        ins += [qk_gain, e, et]
        specs += [pl.BlockSpec((1, IN_W), lambda i: (0, 0)), pl.BlockSpec((IN_W, LANES), lambda i: (0, 0)),
                  pl.BlockSpec((LANES, IN_W), lambda i: (0, 0))]
    out_shape, out_specs = [jax.ShapeDtypeStruct((S, IN_W), BF16)], [row]
    if kv_heads:
        out_shape += [jax.ShapeDtypeStruct((N_KV_HEADS, S, HEAD_DIM), BF16)] * 2
        out_specs += [pl.BlockSpec((N_KV_HEADS, tm, HEAD_DIM), lambda i: (0, i, 0))] * 2
    res = _pc(body, name=name, out_shape=tuple(out_shape), grid=(S // tm,),
              in_specs=specs, out_specs=tuple(out_specs), compiler_params=_cp("parallel"))(*ins)
    return res if kv_heads else res[0]


def _headprep_bwd(dpr, tabs, shift, proj=None, qk_gain=None, name="headprep_bwd"):
    parts = isinstance(dpr, (tuple, list))
    S = dpr[0].shape[0] if parts else dpr.shape[0]
    tm = _row_tile(S, 512)
    norm = qk_gain is not None

    def body(*refs):
        refs = list(refs)
        d_refs = [refs.pop(0) for _ in range(4 if parts else 1)]
        c_ref, lo_ref, hi_ref = (refs.pop(0) for _ in range(3))
        if norm:
            p_ref, g_ref, e_ref, et_ref, o_ref, dg_ref = refs
        else:
            (o_ref,) = refs
        if parts:
            dq_ref, dk_ref, dv_ref, dqm_ref = d_refs
            d = jnp.concatenate([dq_ref[...]] + [dk_ref[h] for h in range(N_KV_HEADS)]
                                + [dv_ref[h] for h in range(N_KV_HEADS)] + [dqm_ref[...]], axis=1).astype(F32)
        else:
            d = d_refs[0][...].astype(F32)
        c, s_lo, s_hi, qk = _rope_cols((c_ref, lo_ref, hi_ref), tm)
        dx = d * c + pltpu.roll(d * s_lo, IN_W - shift, 1) + pltpu.roll(d * s_hi, shift, 1)
        if norm:
            x = p_ref[...]
            r = lax.rsqrt(_seg_mean(x * x, e_ref, et_ref) + EPS)
            xh = x * r

            @pl.when(pl.program_id(0) == 0)
            def _():
                dg_ref[...] = jnp.zeros_like(dg_ref)

            dg_ref[...] += jnp.sum(jnp.where(qk, dx * xh, 0.0), axis=0, keepdims=True)
            dxh = dx * g_ref[...]
            dn = r * (dxh - xh * _seg_mean(dxh * xh, e_ref, et_ref))
            dx = jnp.where(qk, dn, dx)
        o_ref[...] = dx.astype(o_ref.dtype)

    row = pl.BlockSpec((tm, IN_W), lambda i: (i, 0))
    tab = pl.BlockSpec((tm, LANES), lambda i: (i, 0))
    vec = pl.BlockSpec((1, IN_W), lambda i: (0, 0))
    if parts:
        heads = pl.BlockSpec((N_KV_HEADS, tm, HEAD_DIM), lambda i: (0, i, 0))
        ins = [*dpr, *tabs]
        specs = [pl.BlockSpec((tm, Q_W), lambda i: (i, 0)), heads, heads, pl.BlockSpec((tm, QM_W), lambda i: (i, 0)), tab, tab, tab]
    else:
        ins = [dpr, *tabs]
        specs = [row, tab, tab, tab]
    out_shape = jax.ShapeDtypeStruct((S, IN_W), BF16)
    out_specs = row
    if norm:
        e, et = _seg_mats()
        ins += [proj, qk_gain, e, et]
        specs += [row, vec, pl.BlockSpec((IN_W, LANES), lambda i: (0, 0)), pl.BlockSpec((LANES, IN_W), lambda i: (0, 0))]
        out_shape = (out_shape, jax.ShapeDtypeStruct((1, IN_W), F32))
        out_specs = (row, vec)
    return _pc(body, name=name, out_shape=out_shape, grid=(S // tm,), in_specs=specs, out_specs=out_specs,
               compiler_params=_cp("arbitrary" if norm else "parallel"))(*ins)


CHAIN_ROWS_WIDE, CHAIN_ROWS_NARROW = 128, 256
CHAIN_NARROW_KEYS = 1024


def _skewed(n, stages):
    for t in range(n + len(stages) - 1):
        for s, stage in enumerate(stages):
            if 0 <= t - s < n:
                stage(t - s)


def _chain_slices(G, bq, keys):
    cr = min(CHAIN_ROWS_NARROW if keys <= CHAIN_NARROW_KEYS else CHAIN_ROWS_WIDE, bq)
    while bq % cr:
        cr //= 2
    per = bq // cr
    return [(c // per, slice((c % per) * cr, (c % per + 1) * cr), slice(c * cr, (c + 1) * cr)) for c in range(G * per)]


def _v_ones(v):
    return jnp.concatenate([v, jnp.ones(v.shape, v.dtype)], axis=1)


def _q_dims(q, k, heads, kv_per_head=False):
    if heads is None:
        return q.shape
    return (1 if kv_per_head else k.shape[0]), heads, q.shape[0], k.shape[2]


def _q_shape(heads, NB, G, L, HD):
    return (NB, G, L, HD) if heads is None else (L, NB * G * HD)


def _q_spec(heads, G, rows, HD, index):
    if heads is None:
        return pl.BlockSpec((1, G, rows, HD), lambda *ids: (index(*ids)[0], 0, index(*ids)[1], 0))
    return pl.BlockSpec((rows, G * HD), lambda *ids: index(*ids)[::-1])


def _q_at(heads, g, hr, HD):
    return (0, g, hr, slice(None)) if heads is None else (hr, slice(g * HD, (g + 1) * HD))


def _window(L, blk, radius):
    if radius is None:
        return L, None
    W = min(L, blk + 2 * radius)
    assert blk % radius == 0 and (L - W) % radius == 0
    return W, lambda n: radius * jnp.clip(n * (blk // radius) - 1, 0, (L - W) // radius)


def _win_specs(G, W, HD, start, with_g):
    E = pl.Element
    st = (lambda n: 0) if start is None else start
    if with_g:
        return pl.BlockSpec((E(1), E(G), E(W), E(HD)), lambda b, n: (b, 0, st(n), 0))
    return pl.BlockSpec((E(1), E(W), E(HD)), lambda b, n: (b, st(n), 0))


def _attn_delta(do, o, *, dlse=None, lse=None, sink=None, heads=None, name="attn_delta"):
    HD = HEAD_DIM
    (NB, G), L = (heads, do.shape[0]) if heads is not None else (do.shape[:2], do.shape[2])
    bl = _row_tile(L, 1024)

    def body(*refs):
        refs = list(refs)
        sink_ref = refs.pop(0) if sink is not None else None
        do_ref, o_ref = refs.pop(0), refs.pop(0)
        dlse_ref = refs.pop(0) if dlse is not None else None
        lse_ref = refs.pop(0) if sink is not None else None
        delta_ref = refs.pop(0)
        b = pl.program_id(0)
        if heads is None:
            delta = jnp.sum(do_ref[0].astype(F32) * o_ref[0].astype(F32), axis=-1, keepdims=True)
        else:
            prod = do_ref[...].astype(F32) * o_ref[...].astype(F32)
            delta = jnp.concatenate([jnp.sum(prod[:, g * HD:(g + 1) * HD], axis=-1, keepdims=True)[None] for g in range(G)])
        if dlse is not None:
            delta = delta - dlse_ref[0]
        delta_ref[0] = delta
        if sink is not None:
            ds_ref = refs.pop(0)

            @pl.when(pl.program_id(1) == 0)
            def _():
                ds_ref[...] = jnp.zeros_like(ds_ref)

            for g in range(G):
                ps = jnp.exp(sink_ref[b * G + g] - lse_ref[0, g]) * delta[g]
                ds_ref[0, g] -= jnp.sum(ps)

    blk = _q_spec(None if heads is None else G, G, bl, HD, lambda b, n: (b, n))
    col = pl.BlockSpec((1, G, bl, 1), lambda b, n: (b, 0, n, 0))
    ins, specs = [do, o], [blk, blk]
    if dlse is not None:
        ins, specs = ins + [dlse], specs + [col]
    out_shape = jax.ShapeDtypeStruct((NB, G, L, 1), F32)
    out_specs = col
    if sink is not None:
        ins, specs = [sink] + ins + [lse], [pl.BlockSpec(memory_space=pltpu.SMEM)] + specs + [col]
        out_shape = (out_shape, jax.ShapeDtypeStruct((NB, G, 1, LANES), F32))
        out_specs = (col, pl.BlockSpec((1, G, 1, LANES), lambda b, n: (b, 0, 0, 0)))
    return _pc(body, name=name, out_shape=out_shape, grid=(NB, L // bl), in_specs=specs, out_specs=out_specs,
               compiler_params=_cp("parallel", "arbitrary"))(*ins)


def _attn_fwd_full(q, k, v, *, bq, tk, heads=None, name="attn_fwd_full"):
    NB, G, L, HD = _q_dims(q, k, heads)
    Lk = k.shape[1]
    nq, nk = L // bq, Lk // tk
    rows = G * bq
    chains = _chain_slices(G, bq, min(tk, CHAIN_NARROW_KEYS))

    def body(q_ref, k_ref, v_ref, o_ref, lse_ref, m_sc, acc_sc, q_sc):
        j = pl.program_id(2)

        @pl.when(j == 0)
        def _():
            m_sc[...] = jnp.full_like(m_sc, NEG)
            acc_sc[...] = jnp.zeros_like(acc_sc)
            for g, hr, sl in chains:
                q_sc[sl] = q_ref[_q_at(heads, g, hr, HD)] * ATTN_SCALE

        kk = k_ref[0]
        vv = _v_ones(v_ref[0])
        st = [dict() for _ in chains]

        def scores(c):
            st[c]["s"] = lax.dot_general(q_sc[chains[c][2]], kk, NT_DIMS, preferred_element_type=F32)

        def softmax(c):
            sl = chains[c][2]
            m_prev = m_sc[sl]
            m_new = jnp.maximum(m_prev, jnp.max(st[c]["s"], axis=1, keepdims=True))
            st[c]["p"] = jnp.exp(st[c].pop("s") - m_new).astype(BF16)
            st[c]["alpha"] = jnp.exp(m_prev - m_new)
            m_sc[sl] = m_new

        def values(c):
            sl = chains[c][2]
            acc_sc[sl] = st[c].pop("alpha") * acc_sc[sl] + jnp.dot(st[c].pop("p"), vv, preferred_element_type=F32)

        _skewed(len(chains), (scores, softmax, values))

        @pl.when(j == nk - 1)
        def _():
            for g, hr, sl in chains:
                acc = acc_sc[sl]
                l = acc[:, HD:HD + 1]
                o_ref[_q_at(heads, g, hr, HD)] = (acc[:, :HD] / l).astype(o_ref.dtype)
                lse_ref[0, g, hr, :] = m_sc[sl] + jnp.log(l)

    qspec = _q_spec(heads, G, bq, HD, lambda b, n, j: (b, n))
    kspec = pl.BlockSpec((1, tk, HD), lambda b, n, j: (b, j, 0))
    return _pc(body, name=name,
               out_shape=(jax.ShapeDtypeStruct(_q_shape(heads, NB, G, L, HD), BF16), jax.ShapeDtypeStruct((NB, G, L, 1), F32)),
               grid=(NB, nq, nk), in_specs=[qspec, kspec, kspec],
               out_specs=(qspec, pl.BlockSpec((1, G, bq, 1), lambda b, n, j: (b, 0, n, 0))),
               scratch_shapes=[pltpu.VMEM((rows, 1), F32), pltpu.VMEM((rows, 2 * HD), F32), pltpu.VMEM((rows, HD), BF16)],
               compiler_params=_cp("parallel", "parallel", "arbitrary"))(q, k, v)


def _attn_bwd_full(q, k, v, do, lse, o, *, bq, tk, heads=None, kv_per_head=False, name="attn_bwd_full"):
    NB, G, L, HD = _q_dims(q, k, heads, kv_per_head)
    Lk = k.shape[1]
    nq, nk = L // bq, Lk // tk
    chains = _chain_slices(G, bq, tk)
    KH = G if kv_per_head else 1
    assert not kv_per_head or nk == 1

    def body(q_ref, k_ref, v_ref, do_ref, lse_ref, o_ref, dqp_ref, dk_ref, dv_ref, dk_sc, dv_sc):
        n = pl.program_id(2)

        @pl.when(n == 0)
        def _():
            dk_sc[...] = jnp.zeros_like(dk_sc)
            dv_sc[...] = jnp.zeros_like(dv_sc)

        kks = [k_ref[h] for h in range(KH)]
        vvs = [v_ref[h] for h in range(KH)]
        st = [dict() for _ in chains]

        def scores(c):
            g, hr, _ = chains[c]
            at = _q_at(heads, g, hr, HD)
            st[c]["q"] = q_ref[at] * ATTN_SCALE
            st[c]["do"] = do_ref[at]
            st[c]["s"] = lax.dot_general(st[c]["q"], kks[g % KH], NT_DIMS, preferred_element_type=F32)
            st[c]["dp"] = lax.dot_general(st[c]["do"], vvs[g % KH], NT_DIMS, preferred_element_type=F32)
            st[c]["delta"] = jnp.sum(st[c]["do"].astype(F32) * o_ref[at].astype(F32), axis=-1, keepdims=True)

        def softmax(c):
            g, hr, _ = chains[c]
            p = jnp.exp(st[c].pop("s") - lse_ref[0, g, hr, :])
            st[c]["ds"] = (p * (st[c].pop("dp") - st[c].pop("delta"))).astype(BF16)
            st[c]["p"] = p.astype(BF16)

        def grads(c):
            g, hr, _ = chains[c]
            ds = st[c].pop("ds")
            dv_sc[g % KH] += lax.dot_general(st[c].pop("p"), st[c].pop("do"), TN_DIMS, preferred_element_type=F32)
            dk_sc[g % KH] += lax.dot_general(ds, st[c].pop("q"), TN_DIMS, preferred_element_type=F32)
            dqp_ref[(0,) + _q_at(heads, g, hr, HD)] = jnp.dot(ds, kks[g % KH], preferred_element_type=F32) * ATTN_SCALE

        _skewed(len(chains), (scores, softmax, grads))

        @pl.when(n == nq - 1)
        def _():
            dk_ref[...] = dk_sc[...].astype(dk_ref.dtype)
            dv_ref[...] = dv_sc[...].astype(dv_ref.dtype)

    qspec = _q_spec(heads, G, bq, HD, lambda b, m, n: (b, n))
    cspec = pl.BlockSpec((1, G, bq, 1), lambda b, m, n: (b, 0, n, 0))
    kspec = pl.BlockSpec((KH, tk, HD), lambda b, m, n: (b, m, 0))
    kv_shape = jax.ShapeDtypeStruct((NB * KH, Lk, HD), BF16)
    if heads is None:
        pspec = pl.BlockSpec((1, 1, G, bq, HD), lambda b, m, n: (m, b, 0, n, 0))
    else:
        pspec = pl.BlockSpec((1, bq, G * HD), lambda b, m, n: (m, n, b))
    dqp, dk, dv = _pc(body, name=name,
                      out_shape=(jax.ShapeDtypeStruct((nk,) + _q_shape(heads, NB, G, L, HD), F32), kv_shape, kv_shape),
                      grid=(NB, nk, nq), in_specs=[qspec, kspec, kspec, qspec, cspec, qspec],
                      out_specs=(pspec, kspec, kspec),
                      scratch_shapes=[pltpu.VMEM((KH, tk, HD), F32), pltpu.VMEM((KH, tk, HD), F32)],
                      compiler_params=_cp("parallel", "parallel", "arbitrary"))(q, k, v, do, lse, o)
    if nk == 1:
        return dqp[0].astype(BF16), dk, dv
    bl = _row_tile(L, 512)

    def sum_body(p_ref, o_ref):
        acc = p_ref[0]
        for j in range(1, nk):
            acc = acc + p_ref[j]
        o_ref[...] = acc.astype(o_ref.dtype)

    if heads is None:
        pspec = pl.BlockSpec((nk, 1, G, bl, HD), lambda b, n: (0, b, 0, n, 0))
    else:
        pspec = pl.BlockSpec((nk, bl, G * HD), lambda b, n: (0, n, b))
    dq = _pc(sum_body, name=name + "_sum", out_shape=jax.ShapeDtypeStruct(_q_shape(heads, NB, G, L, HD), BF16),
             grid=(NB, L // bl), in_specs=[pspec], out_specs=_q_spec(heads, G, bl, HD, lambda b, n: (b, n)),
             compiler_params=_cp("parallel", "parallel"))(dqp)
    return dq, dk, dv


def _attn_fwd_win(q, k, v, *, radius, sink=None, bq, heads=None, kv_per_head=False, carry=None, name="attn_fwd_win"):
    NB, G, L, HD = _q_dims(q, k, heads, kv_per_head)
    W, start = _window(k.shape[1], bq, radius)
    chains = _chain_slices(G, bq, W)

    def body(*refs):
        if sink is not None:
            sink_ref, *refs = refs
        q_ref, k_ref, v_ref, o_ref, lse_ref = refs
        b, n = pl.program_id(0), pl.program_id(1)
        kks = [k_ref[g] for g in range(G)] if kv_per_head else [k_ref[0]] * G
        vvs = [_v_ones(v_ref[g]) for g in range(G)] if kv_per_head else [_v_ones(v_ref[0])] * G
        st = [dict() for _ in chains]

        def scores(c):
            g, hr, _ = chains[c]
            s = lax.dot_general(q_ref[_q_at(heads, g, hr, HD)] * ATTN_SCALE, kks[g], NT_DIMS, preferred_element_type=F32)
            if radius is not None:
                qpos = n * bq + hr.start + lax.broadcasted_iota(jnp.int32, (hr.stop - hr.start, 1), 0)
                kpos = start(n) + lax.broadcasted_iota(jnp.int32, (1, W), 1)
                s = jnp.where(jnp.abs(qpos - kpos) <= radius, s, NEG)
            st[c]["s"] = s

        def softmax(c):
            g = chains[c][0]
            m = jnp.max(st[c]["s"], axis=1, keepdims=True)
            if sink is not None:
                m = jnp.maximum(m, sink_ref[b * G + g])
            st[c]["p"] = jnp.exp(st[c].pop("s") - m).astype(BF16)
            st[c]["m"] = m

        def values(c):
            g, hr, _ = chains[c]
            acc = jnp.dot(st[c].pop("p"), vvs[g], preferred_element_type=F32)
            m = st[c].pop("m")
            l = acc[:, HD:HD + 1]
            if sink is not None:
                l = l + jnp.exp(sink_ref[b * G + g] - m)
            o_ref[_q_at(heads, g, hr, HD)] = (acc[:, :HD] / l).astype(o_ref.dtype)
            lse_ref[0, g, hr, :] = m + jnp.log(l)

        _skewed(len(chains), (scores, softmax, values))

    qspec = _q_spec(heads, G, bq, HD, lambda b, n: (b, n))
    kspec = pl.BlockSpec((G, W, HD), lambda b, n: (0, 0, 0)) if kv_per_head else _win_specs(G, W, HD, start, False)
    ins, specs = [q, k, v], [qspec, kspec, kspec]
    if sink is not None:
        ins, specs = [sink] + ins, [pl.BlockSpec(memory_space=pltpu.SMEM)] + specs
    return _call(body, carry, ins, name=name,
                 out_shape=(jax.ShapeDtypeStruct(_q_shape(heads, NB, G, L, HD), BF16), jax.ShapeDtypeStruct((NB, G, L, 1), F32)),
                 grid=(NB, L // bq), in_specs=specs,
                 out_specs=(qspec, pl.BlockSpec((1, G, bq, 1), lambda b, n: (b, 0, n, 0))), sem=("parallel", "parallel"))


def _attn_dq_win(q, k, v, do, lse, delta, *, radius, bq, heads=None, name="attn_dq_win"):
    NB, G, L, HD = _q_dims(q, k, heads)
    W, start = _window(L, bq, radius)
    chains = _chain_slices(G, bq, W)

    def body(q_ref, k_ref, v_ref, do_ref, lse_ref, dl_ref, dq_ref):
        n = pl.program_id(1)
        kk, vv = k_ref[0], v_ref[0]
        kpos = start(n) + lax.broadcasted_iota(jnp.int32, (1, W), 1)
        st = [dict() for _ in chains]

        def scores(c):
            g, hr, _ = chains[c]
            at = _q_at(heads, g, hr, HD)
            st[c]["s"] = lax.dot_general(q_ref[at] * ATTN_SCALE, kk, NT_DIMS, preferred_element_type=F32)
            st[c]["dp"] = lax.dot_general(do_ref[at], vv, NT_DIMS, preferred_element_type=F32)

        def softmax(c):
            g, hr, _ = chains[c]
            qpos = n * bq + hr.start + lax.broadcasted_iota(jnp.int32, (hr.stop - hr.start, 1), 0)
            p = jnp.where(jnp.abs(qpos - kpos) <= radius, jnp.exp(st[c].pop("s") - lse_ref[0, g, hr, :]), 0.0)
            st[c]["ds"] = (p * (st[c].pop("dp") - dl_ref[0, g, hr, :])).astype(BF16)

        def grads(c):
            g, hr, _ = chains[c]
            dq = jnp.dot(st[c].pop("ds"), kk, preferred_element_type=F32) * ATTN_SCALE
            dq_ref[_q_at(heads, g, hr, HD)] = dq.astype(dq_ref.dtype)

        _skewed(len(chains), (scores, softmax, grads))

    qspec = _q_spec(heads, G, bq, HD, lambda b, n: (b, n))
    cspec = pl.BlockSpec((1, G, bq, 1), lambda b, n: (b, 0, n, 0))
    kspec = _win_specs(G, W, HD, start, False)
    return _pc(body, name=name, out_shape=jax.ShapeDtypeStruct(_q_shape(heads, NB, G, L, HD), BF16), grid=(NB, L // bq),
               in_specs=[qspec, kspec, kspec, qspec, cspec, cspec], out_specs=qspec,
               compiler_params=_cp("parallel", "parallel"))(q, k, v, do, lse, delta)


def _attn_dkv_win(q, k, v, do, lse, delta, *, radius, bk, heads=None, carry=None, name="attn_dkv_win"):
    NB, G, L, HD = _q_dims(q, k, heads)
    W, start = _window(L, bk, radius)
    chains = _chain_slices(G, W, bk)

    def body(q_ref, k_ref, v_ref, do_ref, lse_ref, dl_ref, dk_ref, dv_ref):
        m = pl.program_id(1)
        kk, vv = k_ref[0], v_ref[0]
        kpos = m * bk + lax.broadcasted_iota(jnp.int32, (1, bk), 1)
        st = [dict() for _ in chains]
        out = dict(dk=jnp.zeros((bk, HD), F32), dv=jnp.zeros((bk, HD), F32))

        def scores(c):
            g, hr, _ = chains[c]
            at = _q_at(heads, g, hr, HD)
            st[c]["q"] = q_ref[at] * ATTN_SCALE
            st[c]["do"] = do_ref[at]
            st[c]["s"] = lax.dot_general(st[c]["q"], kk, NT_DIMS, preferred_element_type=F32)
            st[c]["dp"] = lax.dot_general(st[c]["do"], vv, NT_DIMS, preferred_element_type=F32)

        def softmax(c):
            g, hr, _ = chains[c]
            qpos = start(m) + hr.start + lax.broadcasted_iota(jnp.int32, (hr.stop - hr.start, 1), 0)
            p = jnp.where(jnp.abs(qpos - kpos) <= radius, jnp.exp(st[c].pop("s") - lse_ref[0, g, hr, :]), 0.0)
            st[c]["ds"] = (p * (st[c].pop("dp") - dl_ref[0, g, hr, :])).astype(BF16)
            st[c]["p"] = p.astype(BF16)

        def grads(c):
            out["dv"] = out["dv"] + lax.dot_general(st[c].pop("p"), st[c].pop("do"), TN_DIMS, preferred_element_type=F32)
            out["dk"] = out["dk"] + lax.dot_general(st[c].pop("ds"), st[c].pop("q"), TN_DIMS, preferred_element_type=F32)

        _skewed(len(chains), (scores, softmax, grads))
        dk_ref[0] = out["dk"].astype(dk_ref.dtype)
        dv_ref[0] = out["dv"].astype(dv_ref.dtype)

    if heads is None:
        qspec = _win_specs(G, W, HD, start, True)
    else:
        qspec = pl.BlockSpec((pl.Element(W), pl.Element(G * HD)), lambda b, m: (start(m), b * G * HD))
    cspec = _win_specs(G, W, 1, start, True)
    kspec = pl.BlockSpec((1, bk, HD), lambda b, m: (b, m, 0))
    kv_shape = jax.ShapeDtypeStruct((NB, L, HD), BF16)
    return _call(body, carry, [q, k, v, do, lse, delta], name=name, out_shape=(kv_shape, kv_shape), grid=(NB, L // bk),
                 in_specs=[qspec, kspec, kspec, qspec, cspec, cspec], out_specs=(kspec, kspec), sem=("parallel", "parallel"))


def _attn(q, k, v, *, radius=None, sink=None, bq, tk=None, heads=None, kv_per_head=False, carry=None, tag):
    if radius is None and tk < k.shape[1]:
        return _attn_fwd_full(q, k, v, bq=bq, tk=tk, heads=heads, name=f"attn_fwd_{tag}")
    return _attn_fwd_win(q, k, v, radius=radius, sink=sink, bq=bq, heads=heads, kv_per_head=kv_per_head, carry=carry,
                         name=f"attn_fwd_{tag}")


def _attn_bwd(q, k, v, o, lse, do, *, radius=None, sink=None, dlse=None, bq, tk=None, heads=None, kv_per_head=False,
              carry=None, tag):
    if radius is None:
        assert carry is None
        return (*_attn_bwd_full(q, k, v, do, lse, o, bq=bq, tk=tk, heads=heads, kv_per_head=kv_per_head,
                                name=f"attn_bwd_{tag}"), None, None)
    nbg = None if heads is None else (k.shape[0], heads)
    if sink is not None:
        delta, ds = _attn_delta(do, o, lse=lse, sink=sink, heads=nbg, name=f"attn_delta_{tag}")
        dsink = ds[:, :, 0, 0].reshape(-1)
    else:
        delta, dsink = _attn_delta(do, o, dlse=dlse, heads=nbg, name=f"attn_delta_{tag}"), None
    dq = _attn_dq_win(q, k, v, do, lse, delta, radius=radius, bq=bq, heads=heads, name=f"attn_dq_{tag}")
    dk, dv, *arrived = _attn_dkv_win(q, k, v, do, lse, delta, radius=radius, bk=bq, heads=heads, carry=carry,
                                     name=f"attn_dkv_{tag}")
    return dq, dk, dv, dsink, (arrived[0] if arrived else None)


def _combine_fwd(o, lse, name="combine_fwd"):
    H, S, HD = o.shape
    tm = _row_tile(S, 512)

    def body(o_ref, lse_ref, t_ref):
        for g in range(GQA_GROUP):
            hs = [kv * GQA_GROUP + g for kv in range(N_KV_HEADS)]
            ls = [lse_ref[h] for h in hs]
            mx = functools.reduce(jnp.maximum, ls)
            es = [jnp.exp(l - mx) for l in ls]
            den = functools.reduce(jnp.add, es)
            for h, e in zip(hs, es):
                t_ref[h] = (o_ref[h].astype(F32) * (e / den)).astype(t_ref.dtype)

    blk = pl.BlockSpec((H, tm, HD), lambda i: (0, i, 0))
    col = pl.BlockSpec((H, tm, 1), lambda i: (0, i, 0))
    return _pc(body, name=name, out_shape=jax.ShapeDtypeStruct((H, S, HD), BF16), grid=(S // tm,),
               in_specs=[blk, col], out_specs=blk, compiler_params=_cp("parallel"))(o, lse)


def _combine_bwd(dt, o, lse, name="combine_bwd"):
    H, S, HD = o.shape
    tm = _row_tile(S, 512)

    def body(dt_ref, o_ref, lse_ref, do_ref, dlse_ref):
        for g in range(GQA_GROUP):
            hs = [kv * GQA_GROUP + g for kv in range(N_KV_HEADS)]
            ls = [lse_ref[h] for h in hs]
            mx = functools.reduce(jnp.maximum, ls)
            es = [jnp.exp(l - mx) for l in ls]
            den = functools.reduce(jnp.add, es)
            al = [e / den for e in es]
            dts = [dt_ref[h].astype(F32) for h in hs]
            da = [jnp.sum(d * o_ref[h].astype(F32), axis=-1, keepdims=True) for h, d in zip(hs, dts)]
            dot = functools.reduce(jnp.add, [a * d for a, d in zip(al, da)])
            for h, a, d, dd in zip(hs, al, da, dts):
                do_ref[h] = (dd * a).astype(do_ref.dtype)
                dlse_ref[h] = a * (d - dot)

    blk = pl.BlockSpec((H, tm, HD), lambda i: (0, i, 0))
    col = pl.BlockSpec((H, tm, 1), lambda i: (0, i, 0))
    return _pc(body, name=name,
               out_shape=(jax.ShapeDtypeStruct((H, S, HD), BF16), jax.ShapeDtypeStruct((H, S, 1), F32)),
               grid=(S // tm,), in_specs=[blk, blk, col], out_specs=(blk, col), compiler_params=_cp("parallel"))(dt, o, lse)


def _position():
    x, y, c = lax.axis_index("x"), lax.axis_index("y"), lax.axis_index("c")
    return x, y, c


def _peer(pos, k):
    x, y, c = pos
    return (1 - x if k & 4 else x, 1 - y if k & 2 else y, 1 - c if k & 1 else c)


def _linear(p):
    return 4 * p[0] + 2 * p[1] + p[2]


def _exchange_steps(s_ref, r_ref, send_sems, recv_sems, local_sem, gather):
    pos = _position()
    me = _linear(pos)
    own = pltpu.make_async_copy(s_ref if gather else s_ref.at[me], r_ref.at[me], local_sem)
    peers = range(1, N_DEV)

    def sems(k):
        return dict(send_sem=send_sems.at[k - 1], recv_sem=recv_sems.at[k - 1], device_id=_peer(pos, k), device_id_type=MESH)

    def send(k):
        src = s_ref if gather else s_ref.at[_linear(_peer(pos, k))]
        return pltpu.make_async_remote_copy(src_ref=src, dst_ref=r_ref.at[me], **sems(k))

    def arrival(k):
        slot = r_ref.at[_linear(_peer(pos, k))]
        return pltpu.make_async_remote_copy(src_ref=slot, dst_ref=slot, **sems(k))

    def start():
        own.start()
        for k in peers:
            send(k).start()

    def wait():
        for k in peers:
            arrival(k).wait_recv()
        for k in peers:
            send(k).wait_send()
        own.wait()

    return start, wait


EXCHANGE_SEMS = [pltpu.SemaphoreType.DMA((N_DEV - 1,)), pltpu.SemaphoreType.DMA((N_DEV - 1,)), pltpu.SemaphoreType.DMA]


def _exchange(buf, gather, name):
    def body(s_ref, r_ref, *sems):
        start, wait = _exchange_steps(s_ref, r_ref, *sems, gather)
        start()
        wait()

    hbm = pl.BlockSpec(memory_space=pltpu.HBM)
    out_shape = ((N_DEV,) + buf.shape) if gather else buf.shape
    return _pc(body, name=name, out_shape=jax.ShapeDtypeStruct(out_shape, buf.dtype), in_specs=[hbm], out_specs=hbm,
               scratch_shapes=list(EXCHANGE_SEMS))(buf)


def _call(body, carry, ins, *, name, out_shape, grid, in_specs, out_specs, scratch_shapes=(), sem):
    if carry is None:
        return _pc(body, name=name, out_shape=tuple(out_shape), grid=grid, in_specs=list(in_specs),
                   out_specs=tuple(out_specs), scratch_shapes=list(scratch_shapes), compiler_params=_cp(*sem))(*ins)
    buf, gather = carry
    n_in, n_out, n_sc = len(ins), len(out_shape), len(scratch_shapes)

    def wrapped(*refs):
        in_refs, buf_ref = refs[:n_in], refs[n_in]
        out_refs, recv_ref = refs[n_in + 1:n_in + 1 + n_out], refs[n_in + 1 + n_out]
        rest = refs[n_in + 2 + n_out:]
        first = functools.reduce(jnp.logical_and, [pl.program_id(a) == 0 for a in range(len(grid))])
        last = functools.reduce(jnp.logical_and, [pl.program_id(a) == grid[a] - 1 for a in range(len(grid))])

        @pl.when(first)
        def _():
            _exchange_steps(buf_ref, recv_ref, *rest[n_sc:], gather)[0]()

        body(*in_refs, *out_refs, *rest[:n_sc])

        @pl.when(last)
        def _():
            _exchange_steps(buf_ref, recv_ref, *rest[n_sc:], gather)[1]()

    hbm = pl.BlockSpec(memory_space=pltpu.HBM)
    recv_shape = ((N_DEV,) + buf.shape) if gather else buf.shape
    return _pc(wrapped, name=name, out_shape=(*out_shape, jax.ShapeDtypeStruct(recv_shape, buf.dtype)), grid=grid,
               in_specs=[*in_specs, hbm], out_specs=(*out_specs, hbm), scratch_shapes=[*scratch_shapes, *EXCHANGE_SEMS],
               compiler_params=_cp(*(("arbitrary",) * len(grid))))(*ins, buf)


def _reduce_adamw(recv, w, m, v, name):
    _, R, C = recv.shape
    tr = _row_tile(R, 512)

    def body(r_ref, w_ref, m_ref, v_ref, g_ref, d_ref, nm_ref, nv_ref):
        g = r_ref[0].astype(F32)
        for j in range(1, N_DEV):
            g = g + r_ref[j].astype(F32)
        g_ref[...] = g
        nm = ADAM_B1 * m_ref[...] + (1.0 - ADAM_B1) * g
        nv = ADAM_B2 * v_ref[...] + (1.0 - ADAM_B2) * jnp.square(g)
        m_hat = nm / (1.0 - ADAM_B1 ** ADAM_STEP)
        v_hat = nv / (1.0 - ADAM_B2 ** ADAM_STEP)
        d_ref[...] = -ADAM_LR * (m_hat / (jnp.sqrt(v_hat) + ADAM_EPS) + ADAM_WD * w_ref[...])
        nm_ref[...] = nm
        nv_ref[...] = nv

    row = pl.BlockSpec((tr, C), lambda i: (i, 0))
    out = jax.ShapeDtypeStruct((R, C), F32)
    return _pc(body, name=name, out_shape=(out, out, out, out), grid=(R // tr,),
               in_specs=[pl.BlockSpec((N_DEV, tr, C), lambda i: (0, i, 0)), row, row, row],
               out_specs=(row, row, row, row), compiler_params=_cp("parallel"))(recv, w, m, v)


BIG = (("w_in", 2), ("w_mem_kv", 1), ("w_o", 1), ("w_gate_up", 2), ("w_down", 1))
SMALL = ("mem_norm_g", "g_mix_pre", "g_mix_post", "attn_sink", "qk_norm_g", "g_ffn_pre", "g_ffn_post")
SMALL_W = 1024
FIRST, REST = BIG[:1], BIG[1:]


def _pack_local(shards, dtype):
    return jnp.concatenate([s.astype(dtype).reshape(-1, LANES) for s in shards], axis=0)


def _unpack_local(flat, shapes):
    out, r = [], 0
    for shp in shapes:
        n = shp[0] * shp[1] * shp[2] // LANES
        out.append(flat[r:r + n].reshape(shp))
        r += n
    return out


def _unpack_gathered(g, shapes, names=BIG):
    out, r = [], 0
    for (name, dim), shp in zip(names, shapes):
        n = shp[0] * shp[1] * shp[2] // LANES
        t = g[:, r:r + n].reshape((N_DEV,) + tuple(shp))
        if dim == 2:
            t = t.transpose(1, 2, 0, 3).reshape(shp[0], shp[1], N_DEV * shp[2])
        else:
            t = t.transpose(1, 0, 2, 3).reshape(shp[0], N_DEV * shp[1], shp[2])
        out.append(t)
        r += n
    return out


def _pack_for_scatter(full, shapes, dtype, names=BIG):
    parts = []
    for (name, dim), shp, t in zip(names, shapes, full):
        if dim == 2:
            t = t.reshape(shp[0], shp[1], N_DEV, shp[2]).transpose(2, 0, 1, 3)
        else:
            t = t.reshape(shp[0], N_DEV, shp[1], shp[2]).transpose(1, 0, 2, 3)
        parts.append(t.astype(dtype).reshape(N_DEV, -1, LANES))
    return jnp.concatenate(parts, axis=1)


def _pack_small(arrs):
    flat = jnp.concatenate([a.reshape(-1) for a in arrs])
    pad = (-flat.shape[0]) % (8 * SMALL_W)
    return jnp.pad(flat, (0, pad)).reshape(-1, SMALL_W)


def _unpack_small(flat, shapes):
    flat = flat.reshape(-1)
    out, r = [], 0
    for shp in shapes:
        n = 1
        for d in shp:
            n *= d
        out.append(flat[r:r + n].reshape(shp))
        r += n
    return out


def _heads(t, nb, g):
    S = t.shape[0]
    return t.reshape(S, nb, g, HEAD_DIM).transpose(1, 2, 0, 3)


def _unheads(t):
    nb, g, S, hd = t.shape
    return t.transpose(2, 0, 1, 3).reshape(S, nb * g * hd)


def _dilate(t, dil):
    S = t.shape[0]
    g = t.shape[1] // HEAD_DIM
    return t.reshape(S // dil, dil, g, HEAD_DIM).transpose(1, 2, 0, 3)


def _undilate(t):
    dil, g, L, w = t.shape
    return t.transpose(1, 2, 0, 3).reshape(g, L * dil, w)


FULL_BQ_FWD, FULL_TK_FWD = 512, 8192
FULL_BQ_BWD, FULL_TK_BWD = 1024, 2048


def _mixer_fwd(kind, pr, kv, sink, li, carry=None):
    S = pr.shape[0]
    if kind == 0:
        tok, lse, *arrived = _attn(pr, *kv, radius=A_RADIUS, sink=sink, bq=min(256, S), heads=GQA_GROUP, carry=carry,
                                   tag=f"a{li}")
        return tok, lse, (arrived[0] if arrived else None)
    assert carry is None
    if kind == 1:
        tok, lse = _attn(pr, *kv, bq=min(FULL_BQ_FWD, S), tk=min(FULL_TK_FWD, S), heads=GQA_GROUP, tag=f"b{li}")
        return tok, lse, None
    saved, outs, lses = [], [], []
    for g, (window, dil) in enumerate(C_GROUPS):
        q = _dilate(pr[:, g * GQA_GROUP * HEAD_DIM:(g + 1) * GQA_GROUP * HEAD_DIM], dil)
        k = _dilate(pr[:, Q_W + g * HEAD_DIM:Q_W + (g + 1) * HEAD_DIM], dil)[:, 0]
        v = _dilate(pr[:, QK_W + g * HEAD_DIM:QK_W + (g + 1) * HEAD_DIM], dil)[:, 0]
        o, lse = _attn(q, k, v, radius=window // (2 * dil), bq=min(256, S // dil), tag=f"c{li}g{g}")
        saved.append((q, k, v, o, lse))
        outs.append(_undilate(o))
        lses.append(_undilate(lse))
    o_all, lse_all = jnp.concatenate(outs, 0), jnp.concatenate(lses, 0)
    tok = _combine_fwd(o_all, lse_all, name=f"combine_fwd_{li}")
    return tok.transpose(1, 0, 2).reshape(S, Q_W), (saved, o_all, lse_all), None


def _mixer_bwd(kind, dcat, pr, kv, cat, saved, sink, li, carry=None):
    S = dcat.shape[0]
    if kind == 0:
        return _attn_bwd(pr, *kv, cat, saved, dcat, radius=A_RADIUS, sink=sink, bq=min(256, S), heads=GQA_GROUP, carry=carry,
                         tag=f"a{li}")
    assert carry is None
    if kind == 1:
        return _attn_bwd(pr, *kv, cat, saved, dcat, bq=min(FULL_BQ_BWD, S), tk=min(FULL_TK_BWD, S), heads=GQA_GROUP,
                         tag=f"b{li}")
    per_group, o_all, lse_all = saved
    dt = dcat[:, :Q_W].reshape(S, N_TOK_HEADS, HEAD_DIM).transpose(1, 0, 2)
    do_all, dlse_all = _combine_bwd(dt, o_all, lse_all, name=f"combine_bwd_{li}")
    dqs, dks, dvs = [], [], []
    for g, (window, dil) in enumerate(C_GROUPS):
        q, k, v, o, lse = per_group[g]
        L = S // dil
        hs = slice(g * GQA_GROUP, (g + 1) * GQA_GROUP)
        do = do_all[hs].reshape(GQA_GROUP, L, dil, HEAD_DIM).transpose(2, 0, 1, 3)
        dlse = dlse_all[hs].reshape(GQA_GROUP, L, dil, 1).transpose(2, 0, 1, 3)
        dq, dk, dv, _, _ = _attn_bwd(q, k, v, o, lse, do, radius=window // (2 * dil), dlse=dlse, bq=min(256, L),
                                     tag=f"c{li}g{g}")
        dqs.append(dq.transpose(2, 0, 1, 3).reshape(S, GQA_GROUP * HEAD_DIM))
        dks.append(dk.transpose(1, 0, 2).reshape(S, HEAD_DIM))
        dvs.append(dv.transpose(1, 0, 2).reshape(S, HEAD_DIM))
    return jnp.concatenate(dqs, 1), jnp.concatenate(dks, 1), jnp.concatenate(dvs, 1), None, None


def kernel(x, mem, mem_norm_g, w_in, w_mem_kv, w_o, g_mix_pre, g_mix_post, attn_sink, qk_norm_g, w_gate_up, w_down, g_ffn_pre, g_ffn_post, loss_target, m_mem_norm_g, m_w_in, m_w_mem_kv, m_w_o, m_g_mix_pre, m_g_mix_post, m_attn_sink, m_qk_norm_g, m_w_gate_up, m_w_down, m_g_ffn_pre, m_g_ffn_post, v_mem_norm_g, v_w_in, v_w_mem_kv, v_w_o, v_g_mix_pre, v_g_mix_post, v_attn_sink, v_qk_norm_g, v_w_gate_up, v_w_down, v_g_ffn_pre, v_g_ffn_post):
    given = dict(locals())
    depth = w_in.shape[0]
    S, D = x.shape[1], x.shape[2]
    def shapes_of(names):
        return [(1,) + tuple(given[n].shape[1:]) for n, _ in names]

    def layer_pack(pre, l, dtype, names=BIG):
        return _pack_local([given[pre + n][l:l + 1] for n, _ in names], dtype)

    def layer_weights(gathered, names=BIG):
        return [t[0] for t in _unpack_gathered(gathered, shapes_of(names), names)]

    W = [None] * depth
    W[0] = layer_weights(_exchange(layer_pack("", 0, BF16, FIRST), True, "gather_w0_in"), FIRST)

    tabs = _rope_tables(S)
    mem_n = _rms_fwd(mem[0], mem_norm_g[None], BF16, name="rms_mem")

    saved = []
    xc = x[0]
    for i in range(depth):
        kind = i % N_MIXERS
        (tab, shift) = tabs[1] if kind == 1 else tabs[0]
        sink = attn_sink[i // N_MIXERS] if kind == 0 else None
        qk_gain = _qk_gain_row(qk_norm_g[i // N_MIXERS]) if kind == 1 else None
        carry = (layer_pack("", 0, BF16, REST), True) if i == 0 else None
        h, proj, *arrived = _mm(xc, W[i][0], F32, pre_g=g_mix_pre[i][None], carry=carry, name=f"mm_in_{i}")
        if carry is not None:
            W[0] = W[0] + layer_weights(arrived[0], REST)
        W_in, W_mkv, W_o, W_gu, W_dn = W[i]
        if kind == 2:
            pr, kv = _headprep_fwd(proj, tab, shift, qk_gain, name=f"headprep_fwd_{i}"), None
        else:
            pr, *kv = _headprep_fwd(proj, tab, shift, qk_gain, kv_heads=True, name=f"headprep_fwd_{i}")
        carry = (layer_pack("", 1, BF16), True) if i == 0 and depth > 1 else None
        tok, mix_saved, arrived = _mixer_fwd(kind, pr, kv, sink, i, carry)
        if carry is not None:
            W[1] = layer_weights(arrived)
        (mkv,) = _mm(mem_n, W_mkv, BF16, name=f"mm_mkv_{i}")
        qm = pr[:, QK_W + KV_W:]
        km = _heads(mkv[:, :QM_W], N_MEM_HEADS, 1)[:, 0]
        vm = _heads(mkv[:, QM_W:], N_MEM_HEADS, 1)[:, 0]
        mo, mlse = _attn(qm, km, vm, bq=min(256, S), tk=km.shape[1], heads=N_MEM_HEADS, kv_per_head=True, tag=f"m{i}")
        cat = jnp.concatenate([tok, mo], axis=1)
        o, x1 = _mm(cat, W_o, F32, post=(g_mix_post[i][None], xc), name=f"mm_o_{i}")
        carry = (layer_pack("", i + 2, BF16), True) if i + 2 < depth else None
        h2, gu, act, *arrived = _mm(x1, W_gu, BF16, pre_g=g_ffn_pre[i][None], swiglu=True, carry=carry, name=f"mm_gu_{i}")
        if carry is not None:
            W[i + 2] = layer_weights(arrived[0])
        f, x2 = _mm(act, W_dn, F32, post=(g_ffn_post[i][None], x1), name=f"mm_dn_{i}")
        saved.append(dict(x=xc, h=h, proj=proj, pr=pr, kv=kv, mix=mix_saved, qm=qm, km=km, vm=vm, mo=mo, mlse=mlse, cat=cat, o=o,
                          x1=x1, h2=h2, gu=gu, act=act, f=f))
        xc = x2

    dy, sq = _loss_head(xc, loss_target[0], name="loss_head")
    loss = lax.psum(sq[0, 0] * (0.5 / D), ("x", "y", "c"))

    grads = {n: [None] * depth for n in ("w_in", "w_mem_kv", "w_o", "w_gate_up", "w_down", "g_mix_pre", "g_mix_post",
                                         "g_ffn_pre", "g_ffn_post")}
    d_sink = [jnp.zeros((N_TOK_HEADS,), F32) for _ in range(attn_sink.shape[0])]
    d_qkg = [jnp.zeros((2, HEAD_DIM), F32) for _ in range(qk_norm_g.shape[0])]
    dmem_n = jnp.zeros((mem.shape[1], D), F32)
    recv = [None] * depth

    def scatter_pack(l, names=BIG):
        return _pack_for_scatter([grads[n][l][None] for n, _ in names], shapes_of(names), BF16, names)

    dx = dy
    for i in reversed(range(depth)):
        kind = i % N_MIXERS
        sv = saved[i]
        (tab, shift) = tabs[1] if kind == 1 else tabs[0]
        sink = attn_sink[i // N_MIXERS] if kind == 0 else None
        W_in, W_mkv, W_o, W_gu, W_dn = W[i]
        df, dgu, dg = _mm(dx, W_dn, BF16, nt=True, gu=sv["gu"], pre_bwd=(sv["f"], g_ffn_post[i][None]), name=f"mmb_dn_{i}")
        grads["g_ffn_post"][i] = dg[0]
        grads["w_down"][i] = _mm_tn(sv["act"], df, name=f"mmw_dn_{i}")
        dx1, dg = _mm(dgu, W_gu, F32, nt=True, post_bwd=(sv["x1"], g_ffn_pre[i][None], dx), name=f"mmb_gu_{i}")
        grads["g_ffn_pre"][i] = dg[0]
        if i + 1 < depth:
            grads["w_gate_up"][i], recv[i + 1] = _mm_tn(sv["h2"], dgu, carry=(scatter_pack(i + 1), False), name=f"mmw_gu_{i}")
        else:
            grads["w_gate_up"][i] = _mm_tn(sv["h2"], dgu, name=f"mmw_gu_{i}")
        do, dcat, dg = _mm(dx1, W_o, BF16, nt=True, pre_bwd=(sv["o"], g_mix_post[i][None]), name=f"mmb_o_{i}")
        grads["g_mix_post"][i] = dg[0]
        grads["w_o"][i] = _mm_tn(sv["cat"], do, name=f"mmw_o_{i}")
        dqm, dkm, dvm, _, _ = _attn_bwd(sv["qm"], sv["km"], sv["vm"], sv["mo"], sv["mlse"], dcat[:, Q_W:], bq=min(256, S),
                                        tk=sv["km"].shape[1], heads=N_MEM_HEADS, kv_per_head=True, tag=f"m{i}")
        dmkv = jnp.concatenate([_unheads(dkm[:, None]), _unheads(dvm[:, None])], axis=1).astype(BF16)
        grads["w_mem_kv"][i] = _mm_tn(mem_n, dmkv, name=f"mmw_mkv_{i}")
        dmem_n = dmem_n + _mm(dmkv, W_mkv, F32, nt=True, name=f"mmb_mkv_{i}")[0]
        carry = (scatter_pack(0, REST), False) if i == 0 else None
        dq, dk, dv, dsink, recv0_rest = _mixer_bwd(kind, dcat, sv["pr"], sv["kv"], sv["cat"], sv["mix"], sink, i, carry)
        if dsink is not None:
            d_sink[i // N_MIXERS] = dsink
        dpr = jnp.concatenate([dq, dk, dv, dqm], axis=1) if kind == 2 else (dq, dk, dv, dqm)
        if kind == 1:
            dproj, dgc = _headprep_bwd(dpr, tab, shift, sv["proj"], _qk_gain_row(qk_norm_g[i // N_MIXERS]),
                                       name=f"headprep_bwd_{i}")
            d_qkg[i // N_MIXERS] = jnp.stack([dgc[0, :Q_W].reshape(N_TOK_HEADS, HEAD_DIM).sum(0),
                                              dgc[0, Q_W:QK_W].reshape(N_KV_HEADS, HEAD_DIM).sum(0)])
        else:
            dproj = _headprep_bwd(dpr, tab, shift, name=f"headprep_bwd_{i}")
        dx, dg = _mm(dproj, W_in, F32, nt=True, post_bwd=(sv["x"], g_mix_pre[i][None], dx1), name=f"mmb_in_{i}")
        grads["g_mix_pre"][i] = dg[0]
        grads["w_in"][i] = _mm_tn(sv["h"], dproj, name=f"mmw_in_{i}")
    _, dg_mem = _rms_bwd(mem[0], mem_norm_g[None], dmem_n, BF16, name="rmsb_mem")

    def update(received, l, names, tag):
        res = _reduce_adamw(received, *[layer_pack(pre, l, F32, names) for pre in ("", "m_", "v_")], name=f"adamw_{tag}")
        return [_unpack_local(r, shapes_of(names)) for r in res]

    recv0_first = _exchange(scatter_pack(0, FIRST), False, "scatter_g0_in")
    per_layer = [[a + b for a, b in zip(update(recv0_first, 0, FIRST, "0_in"), update(recv0_rest, 0, REST, "0"))]]
    per_layer += [update(recv[l], l, BIG, str(l)) for l in range(1, depth)]
    big_out = [[jnp.concatenate(ts, axis=0) for ts in zip(*[per_layer[l][j] for l in range(depth)])] for j in range(4)]

    small_grads = dict(mem_norm_g=dg_mem[0], g_mix_pre=jnp.stack(grads["g_mix_pre"]), g_mix_post=jnp.stack(grads["g_mix_post"]),
                       attn_sink=jnp.stack(d_sink), qk_norm_g=jnp.stack(d_qkg), g_ffn_pre=jnp.stack(grads["g_ffn_pre"]),
                       g_ffn_post=jnp.stack(grads["g_ffn_post"]))
    sg = _pack_small([small_grads[n] for n in SMALL])
    srecv = _exchange(sg, True, "gather_small_grads")
    spacked = lambda pre: _pack_small([given[pre + n] for n in SMALL])
    gs, ds, ms, vs = _reduce_adamw(srecv, spacked(""), spacked("m_"), spacked("v_"), name="adamw_small")

    out = {}
    for pre, fb, fs in zip(("grad_", "delta_", "new_m_", "new_v_"), big_out, (gs, ds, ms, vs)):
        for (n, _), t in zip(BIG, fb):
            out[pre + n] = t
        for n, t in zip(SMALL, _unpack_small(fs, [given[n].shape for n in SMALL])):
            out[pre + n] = t
    order = ("mem_norm_g", "w_in", "w_mem_kv", "w_o", "g_mix_pre", "g_mix_post", "attn_sink", "qk_norm_g", "w_gate_up",
             "w_down", "g_ffn_pre", "g_ffn_post")
    return (loss, dx[None], *[out[p + n] for p in ("grad_", "delta_", "new_m_", "new_v_") for n in order])
```

```python
import functools

import jax
import jax.numpy as jnp
from jax import lax
from jax.experimental import pallas as pl
from jax.experimental.pallas import tpu as pltpu

F32 = jnp.float32
BF16 = jnp.bfloat16

HEAD_DIM = 64
N_TOK_HEADS = 12
N_KV_HEADS = 3
GQA_GROUP = 4
N_MEM_HEADS = 4
Q_W = N_TOK_HEADS * HEAD_DIM
KV_W = N_KV_HEADS * HEAD_DIM
QM_W = N_MEM_HEADS * HEAD_DIM
QK_W = Q_W + KV_W
IN_W = Q_W + 2 * KV_W + QM_W
N_MIXERS = 3
A_RADIUS = 128
C_GROUPS = ((128, 1), (512, 4), (2048, 16))
ROPE_THETA = 500000.0
ROPE_DIMS = HEAD_DIM // 4
AXIAL_THETA = 10000.0
GRID_W = 64
EPS = 1e-6
ATTN_SCALE = HEAD_DIM ** -0.5
NEG = -1e30

ADAM_LR = 0.001
ADAM_B1 = 0.9
ADAM_B2 = 0.999
ADAM_EPS = 1e-08
ADAM_WD = 0.01
ADAM_STEP = 10

N_DEV = 8
LANES = 128
VMEM_LIMIT = 56 * 1024 * 1024
MESH = pl.DeviceIdType.MESH
MM_CHAIN_ROWS = 256
MM_ROWS_WIDE, MM_ROWS_NARROW = 512, 1024
MM_NARROW_COLS = 1408
NT_DIMS = (((1,), (1,)), ((), ()))
TN_DIMS = (((0,), (0,)), ((), ()))


def _pc(body, **kw):
    return pl.pallas_call(body, **kw)


def _cp(*sem):
    return pltpu.CompilerParams(dimension_semantics=sem, vmem_limit_bytes=VMEM_LIMIT)


def _row_tile(m, cap=512):
    t = cap
    while m % t:
        t //= 2
    return t


def _rms_fwd(x, g, out_dtype, name="rms_fwd"):
    M, D = x.shape
    tm = _row_tile(M)

    def body(x_ref, g_ref, o_ref):
        xv = x_ref[...]
        y = xv * lax.rsqrt(jnp.mean(xv * xv, axis=-1, keepdims=True) + EPS) * g_ref[...]
        o_ref[...] = y.astype(o_ref.dtype)

    row = pl.BlockSpec((tm, D), lambda i: (i, 0))
    vec = pl.BlockSpec((1, D), lambda i: (0, 0))
    return _pc(body, name=name, out_shape=jax.ShapeDtypeStruct((M, D), out_dtype), grid=(M // tm,),
               in_specs=[row, vec], out_specs=row, compiler_params=_cp("parallel"))(x, g)


def _rms_bwd_tile(xv, g, d):
    r = lax.rsqrt(jnp.mean(xv * xv, axis=-1, keepdims=True) + EPS)
    xh = xv * r
    dxh = d * g
    return r * (dxh - xh * jnp.mean(dxh * xh, axis=-1, keepdims=True)), jnp.sum(d * xh, axis=0, keepdims=True)


def _rms_bwd(x, g, dy, out_dtype, name="rms_bwd"):
    M, D = x.shape
    tm = _row_tile(M)

    def body(x_ref, g_ref, dy_ref, dx_ref, dg_ref):
        dx, dg = _rms_bwd_tile(x_ref[...], g_ref[...], dy_ref[...].astype(F32))
        dx_ref[...] = dx.astype(dx_ref.dtype)

        @pl.when(pl.program_id(0) == 0)
        def _():
            dg_ref[...] = jnp.zeros_like(dg_ref)

        dg_ref[...] += dg

    row = pl.BlockSpec((tm, D), lambda i: (i, 0))
    vec = pl.BlockSpec((1, D), lambda i: (0, 0))
    return _pc(body, name=name,
               out_shape=(jax.ShapeDtypeStruct((M, D), out_dtype), jax.ShapeDtypeStruct((1, D), F32)),
               grid=(M // tm,), in_specs=[row, vec, row], out_specs=(row, vec), compiler_params=_cp("arbitrary"))(x, g, dy)


def _loss_head(y, t, name="loss_head"):
    M, D = y.shape
    tm = _row_tile(M)

    def body(y_ref, t_ref, dy_ref, acc_ref):
        e = y_ref[...] - t_ref[...]
        dy_ref[...] = e * (1.0 / D)

        @pl.when(pl.program_id(0) == 0)
        def _():
            acc_ref[...] = jnp.zeros_like(acc_ref)

        acc_ref[...] += jnp.sum(e * e)

    row = pl.BlockSpec((tm, D), lambda i: (i, 0))
    return _pc(body, name=name,
               out_shape=(jax.ShapeDtypeStruct((M, D), F32), jax.ShapeDtypeStruct((8, LANES), F32)),
               grid=(M // tm,), in_specs=[row, row],
               out_specs=(row, pl.BlockSpec((8, LANES), lambda i: (0, 0))), compiler_params=_cp("arbitrary"))(y, t)


def _mm(a, w, out_dtype, nt=False, pre_g=None, swiglu=False, post=None, gu=None, pre_bwd=None, post_bwd=None, carry=None,
        name="mm"):
    M, K = a.shape
    N = w.shape[0] if nt else w.shape[1]
    tm = _row_tile(M, MM_ROWS_NARROW if N <= MM_NARROW_COLS and K <= 2 * MM_NARROW_COLS else MM_ROWS_WIDE)
    gain_grad = pre_bwd is not None or post_bwd is not None

    cr = min(MM_CHAIN_ROWS, tm)
    n_chains = tm // cr

    def body(*refs):
        refs = list(refs)
        a_ref, w_ref = refs.pop(0), refs.pop(0)
        pg_ref = refs.pop(0) if pre_g is not None else None
        g_ref, r_ref = (refs.pop(0), refs.pop(0)) if post is not None else (None, None)
        gu_ref = refs.pop(0) if gu is not None else None
        bwd_refs = [refs.pop(0) for _ in (pre_bwd or post_bwd or ())]
        lhs_ref = refs.pop(0) if pre_g is not None or pre_bwd is not None else None
        o_ref = refs.pop(0)
        act_ref = refs.pop(0) if swiglu else None
        x_ref = refs.pop(0) if post is not None else None
        st = [dict() for _ in range(n_chains)]

        def left(c):
            r = slice(c * cr, (c + 1) * cr)
            lhs = a_ref[r, :]
            if pre_g is not None:
                lhs = (lhs * lax.rsqrt(jnp.mean(lhs * lhs, axis=-1, keepdims=True) + EPS) * pg_ref[...]).astype(BF16)
                lhs_ref[r, :] = lhs
            if pre_bwd is not None:
                lhs, st[c]["dg"] = _rms_bwd_tile(bwd_refs[0][r, :], bwd_refs[1][...], lhs.astype(F32))
                lhs = lhs.astype(BF16)
                lhs_ref[r, :] = lhs
            st[c]["lhs"] = lhs

        def product(c):
            if nt:
                st[c]["acc"] = lax.dot_general(st[c].pop("lhs"), w_ref[...], NT_DIMS, preferred_element_type=F32)
            else:
                st[c]["acc"] = jnp.dot(st[c].pop("lhs"), w_ref[...], preferred_element_type=F32)

        def result(c):
            r = slice(c * cr, (c + 1) * cr)
            acc = st[c].pop("acc")
            if post_bwd is not None:
                dx, st[c]["dg"] = _rms_bwd_tile(bwd_refs[0][r, :], bwd_refs[1][...], acc)
                o_ref[r, :] = bwd_refs[2][r, :] + dx
            elif gu is None:
                o_ref[r, :] = acc.astype(o_ref.dtype)
            else:
                gate = gu_ref[r, :N].astype(F32)
                sig = 1.0 / (1.0 + jnp.exp(-gate))
                o_ref[r, :N] = (acc * gu_ref[r, N:].astype(F32) * (sig * (1.0 + gate * (1.0 - sig)))).astype(o_ref.dtype)
                o_ref[r, N:] = (acc * (gate * sig)).astype(o_ref.dtype)
            if swiglu:
                gate = acc[:, : N // 2]
                act_ref[r, :] = (gate * (1.0 / (1.0 + jnp.exp(-gate))) * acc[:, N // 2:]).astype(BF16)
            if post is not None:
                y = acc * lax.rsqrt(jnp.mean(acc * acc, axis=-1, keepdims=True) + EPS) * g_ref[...]
                x_ref[r, :] = r_ref[r, :] + y

        _skewed(n_chains, (left, product, result))
        if gain_grad:
            dg_ref = refs.pop(0)

            @pl.when(pl.program_id(0) == 0)
            def _():
                dg_ref[...] = jnp.zeros_like(dg_ref)

            dg_ref[...] += functools.reduce(jnp.add, [d["dg"] for d in st])

    row = lambda n: pl.BlockSpec((tm, n), lambda i: (i, 0))
    vec = lambda n: pl.BlockSpec((1, n), lambda i: (0, 0))
    ins = [a, w]
    specs = [row(K), pl.BlockSpec(w.shape, lambda i: (0, 0), pipeline_mode=pl.Buffered(1))]
    outs, ospecs = [], []
    if pre_g is not None:
        ins, specs = ins + [pre_g], specs + [vec(K)]
    if pre_g is not None or pre_bwd is not None:
        outs, ospecs = outs + [jax.ShapeDtypeStruct((M, K), BF16)], ospecs + [row(K)]
    if post is not None:
        ins, specs = ins + list(post), specs + [vec(N), row(N)]
    if gu is not None:
        ins, specs = ins + [gu], specs + [row(2 * N)]
        outs, ospecs = outs + [jax.ShapeDtypeStruct((M, 2 * N), BF16)], ospecs + [row(2 * N)]
    else:
        outs, ospecs = outs + [jax.ShapeDtypeStruct((M, N), out_dtype)], ospecs + [row(N)]
    if pre_bwd is not None:
        ins, specs = ins + list(pre_bwd), specs + [row(K), vec(K)]
    if post_bwd is not None:
        ins, specs = ins + list(post_bwd), specs + [row(N), vec(N), row(N)]
    if swiglu:
        outs, ospecs = outs + [jax.ShapeDtypeStruct((M, N // 2), BF16)], ospecs + [row(N // 2)]
    if post is not None:
        outs, ospecs = outs + [jax.ShapeDtypeStruct((M, N), F32)], ospecs + [row(N)]
    if gain_grad:
        D = K if pre_bwd is not None else N
        outs, ospecs = outs + [jax.ShapeDtypeStruct((1, D), F32)], ospecs + [vec(D)]
    return _call(body, carry, ins, name=name, out_shape=outs, grid=(M // tm,), in_specs=specs, out_specs=ospecs,
                 sem=("arbitrary" if gain_grad else "parallel",))


def _mm_tn(a, b, carry=None, name="mm_tn"):
    S, M = a.shape
    N = b.shape[1]
    tm = M if M <= 1408 else M // 2
    tn = N if N <= 2816 else N // 2
    ts = _row_tile(S, 2048 if tm * tn <= 1408 * 1024 else 1024)

    def body(a_ref, b_ref, o_ref):
        @pl.when(pl.program_id(2) == 0)
        def _():
            o_ref[...] = jnp.zeros_like(o_ref)

        o_ref[...] += lax.dot_general(a_ref[...], b_ref[...], TN_DIMS, preferred_element_type=F32)

    res = _call(body, carry, [a, b], name=name, out_shape=[jax.ShapeDtypeStruct((M, N), F32)],
                grid=(M // tm, N // tn, S // ts),
                in_specs=[pl.BlockSpec((ts, tm), lambda i, j, s: (s, i)), pl.BlockSpec((ts, tn), lambda i, j, s: (s, j))],
                out_specs=[pl.BlockSpec((tm, tn), lambda i, j, s: (i, j))], sem=("parallel", "parallel", "arbitrary"))
    return res[0] if carry is None else res


def _rope_tables(S):
    pos = jnp.arange(S, dtype=jnp.int32)

    def table(p, n_dims, theta):
        inv = theta ** (-(jnp.arange(0, n_dims, 2, dtype=F32) / n_dims))
        ang = p.astype(F32)[:, None] * inv[None, :]
        return jnp.cos(ang), jnp.sin(ang)

    one = lambda n: jnp.ones((S, n), F32)
    zero = lambda n: jnp.zeros((S, n), F32)
    cp, sp = table(pos, ROPE_DIMS, ROPE_THETA)
    rest = HEAD_DIM - ROPE_DIMS
    part = (jnp.concatenate([cp, cp, one(rest)], 1), jnp.concatenate([zero(8), sp, zero(rest)], 1),
            jnp.concatenate([-sp, zero(8), zero(rest)], 1))
    cr, sr = table(pos // GRID_W, HEAD_DIM // 2, AXIAL_THETA)
    cc, sc = table(pos % GRID_W, HEAD_DIM // 2, AXIAL_THETA)
    axial = (jnp.concatenate([cr, cr, cc, cc], 1), jnp.concatenate([zero(16), sr, zero(16), sc], 1),
             jnp.concatenate([-sr, zero(16), -sc, zero(16)], 1))
    rep = LANES // HEAD_DIM
    return (tuple(jnp.tile(t, (1, rep)) for t in part), ROPE_DIMS // 2), (tuple(jnp.tile(t, (1, rep)) for t in axial), HEAD_DIM // 4)


def _seg_mats():
    col = jnp.arange(IN_W)[:, None] // HEAD_DIM
    e = (col == jnp.arange(LANES)[None, :]).astype(BF16)
    return e, e.T


def _qk_gain_row(qk_g):
    return jnp.concatenate([jnp.tile(qk_g[0], N_TOK_HEADS), jnp.tile(qk_g[1], N_KV_HEADS),
                            jnp.ones((IN_W - QK_W,), F32)])[None, :]


def _rope_cols(tabs, tm):
    col = lax.broadcasted_iota(jnp.int32, (tm, IN_W), 1)
    qk = col < QK_W
    c, s_lo, s_hi = (jnp.tile(t[...], (1, IN_W // LANES)) for t in tabs)
    return jnp.where(qk, c, 1.0), jnp.where(qk, s_lo, 0.0), jnp.where(qk, s_hi, 0.0), qk


def _seg_mean(v, e_ref, et_ref):
    def split_dot(t, m_ref):
        hi = t.astype(BF16)
        lo = (t - hi.astype(F32)).astype(BF16)
        return jnp.dot(hi, m_ref[...], preferred_element_type=F32) + jnp.dot(lo, m_ref[...], preferred_element_type=F32)

    return split_dot(split_dot(v, e_ref) * (1.0 / HEAD_DIM), et_ref)


def _headprep_fwd(proj, tabs, shift, qk_gain=None, kv_heads=False, name="headprep_fwd"):
    S = proj.shape[0]
    tm = _row_tile(S, 512)
    norm = qk_gain is not None

    def body(*refs):
        refs = list(refs)
        p_ref, c_ref, lo_ref, hi_ref = (refs.pop(0) for _ in range(4))
        g_ref, e_ref, et_ref = (refs.pop(0) for _ in range(3)) if norm else (None, None, None)
        o_ref = refs.pop(0)
        x = p_ref[...]
        c, s_lo, s_hi, qk = _rope_cols((c_ref, lo_ref, hi_ref), tm)
        if norm:
            r = lax.rsqrt(_seg_mean(x * x, e_ref, et_ref) + EPS)
            x = x * jnp.where(qk, r, 1.0) * g_ref[...]
        y = (x * c + pltpu.roll(x, shift, 1) * s_lo + pltpu.roll(x, IN_W - shift, 1) * s_hi).astype(o_ref.dtype)
        o_ref[...] = y
        if kv_heads:
            k_ref, v_ref = refs
            for h in range(N_KV_HEADS):
                k_ref[h] = y[:, Q_W + h * HEAD_DIM:Q_W + (h + 1) * HEAD_DIM]
                v_ref[h] = y[:, QK_W + h * HEAD_DIM:QK_W + (h + 1) * HEAD_DIM]

    row = pl.BlockSpec((tm, IN_W), lambda i: (i, 0))
    tab = pl.BlockSpec((tm, LANES), lambda i: (i, 0))
    ins = [proj, *tabs]
    specs = [row, tab, tab, tab]
    if norm:
        e, et = _seg_mats()
        ins += [qk_gain, e, et]
        specs += [pl.BlockSpec((1, IN_W), lambda i: (0, 0)), pl.BlockSpec((IN_W, LANES), lambda i: (0, 0)),
                  pl.BlockSpec((LANES, IN_W), lambda i: (0, 0))]
    out_shape, out_specs = [jax.ShapeDtypeStruct((S, IN_W), BF16)], [row]
    if kv_heads:
        out_shape += [jax.ShapeDtypeStruct((N_KV_HEADS, S, HEAD_DIM), BF16)] * 2
        out_specs += [pl.BlockSpec((N_KV_HEADS, tm, HEAD_DIM), lambda i: (0, i, 0))] * 2
    res = _pc(body, name=name, out_shape=tuple(out_shape), grid=(S // tm,),
              in_specs=specs, out_specs=tuple(out_specs), compiler_params=_cp("parallel"))(*ins)
    return res if kv_heads else res[0]


def _headprep_bwd(dpr, tabs, shift, proj=None, qk_gain=None, name="headprep_bwd"):
    parts = isinstance(dpr, (tuple, list))
    S = dpr[0].shape[0] if parts else dpr.shape[0]
    tm = _row_tile(S, 512)
    norm = qk_gain is not None

    def body(*refs):
        refs = list(refs)
        d_refs = [refs.pop(0) for _ in range(4 if parts else 1)]
        c_ref, lo_ref, hi_ref = (refs.pop(0) for _ in range(3))
        if norm:
            p_ref, g_ref, e_ref, et_ref, o_ref, dg_ref = refs
        else:
            (o_ref,) = refs
        if parts:
            dq_ref, dk_ref, dv_ref, dqm_ref = d_refs
            d = jnp.concatenate([dq_ref[...]] + [dk_ref[h] for h in range(N_KV_HEADS)]
                                + [dv_ref[h] for h in range(N_KV_HEADS)] + [dqm_ref[...]], axis=1).astype(F32)
        else:
            d = d_refs[0][...].astype(F32)
        c, s_lo, s_hi, qk = _rope_cols((c_ref, lo_ref, hi_ref), tm)
        dx = d * c + pltpu.roll(d * s_lo, IN_W - shift, 1) + pltpu.roll(d * s_hi, shift, 1)
        if norm:
            x = p_ref[...]
            r = lax.rsqrt(_seg_mean(x * x, e_ref, et_ref) + EPS)
            xh = x * r

            @pl.when(pl.program_id(0) == 0)
            def _():
                dg_ref[...] = jnp.zeros_like(dg_ref)

            dg_ref[...] += jnp.sum(jnp.where(qk, dx * xh, 0.0), axis=0, keepdims=True)
            dxh = dx * g_ref[...]
            dn = r * (dxh - xh * _seg_mean(dxh * xh, e_ref, et_ref))
            dx = jnp.where(qk, dn, dx)
        o_ref[...] = dx.astype(o_ref.dtype)

    row = pl.BlockSpec((tm, IN_W), lambda i: (i, 0))
    tab = pl.BlockSpec((tm, LANES), lambda i: (i, 0))
    vec = pl.BlockSpec((1, IN_W), lambda i: (0, 0))
    if parts:
        heads = pl.BlockSpec((N_KV_HEADS, tm, HEAD_DIM), lambda i: (0, i, 0))
        ins = [*dpr, *tabs]
        specs = [pl.BlockSpec((tm, Q_W), lambda i: (i, 0)), heads, heads, pl.BlockSpec((tm, QM_W), lambda i: (i, 0)), tab, tab, tab]
    else:
        ins = [dpr, *tabs]
        specs = [row, tab, tab, tab]
    out_shape = jax.ShapeDtypeStruct((S, IN_W), BF16)
    out_specs = row
    if norm:
        e, et = _seg_mats()
        ins += [proj, qk_gain, e, et]
        specs += [row, vec, pl.BlockSpec((IN_W, LANES), lambda i: (0, 0)), pl.BlockSpec((LANES, IN_W), lambda i: (0, 0))]
        out_shape = (out_shape, jax.ShapeDtypeStruct((1, IN_W), F32))
        out_specs = (row, vec)
    return _pc(body, name=name, out_shape=out_shape, grid=(S // tm,), in_specs=specs, out_specs=out_specs,
               compiler_params=_cp("arbitrary" if norm else "parallel"))(*ins)


CHAIN_ROWS_WIDE, CHAIN_ROWS_NARROW = 128, 256
CHAIN_NARROW_KEYS = 1024


def _skewed(n, stages):
    for t in range(n + len(stages) - 1):
        for s, stage in enumerate(stages):
            if 0 <= t - s < n:
                stage(t - s)


def _chain_slices(G, bq, keys):
    cr = min(CHAIN_ROWS_NARROW if keys <= CHAIN_NARROW_KEYS else CHAIN_ROWS_WIDE, bq)
    while bq % cr:
        cr //= 2
    per = bq // cr
    return [(c // per, slice((c % per) * cr, (c % per + 1) * cr), slice(c * cr, (c + 1) * cr)) for c in range(G * per)]


def _v_ones(v):
    return jnp.concatenate([v, jnp.ones(v.shape, v.dtype)], axis=1)


def _q_dims(q, k, heads, kv_per_head=False):
    if heads is None:
        return q.shape
    return (1 if kv_per_head else k.shape[0]), heads, q.shape[0], k.shape[2]


def _q_shape(heads, NB, G, L, HD):
    return (NB, G, L, HD) if heads is None else (L, NB * G * HD)


def _q_spec(heads, G, rows, HD, index):
    if heads is None:
        return pl.BlockSpec((1, G, rows, HD), lambda *ids: (index(*ids)[0], 0, index(*ids)[1], 0))
    return pl.BlockSpec((rows, G * HD), lambda *ids: index(*ids)[::-1])


def _q_at(heads, g, hr, HD):
    return (0, g, hr, slice(None)) if heads is None else (hr, slice(g * HD, (g + 1) * HD))


def _window(L, blk, radius):
    if radius is None:
        return L, None
    W = min(L, blk + 2 * radius)
    assert blk % radius == 0 and (L - W) % radius == 0
    return W, lambda n: radius * jnp.clip(n * (blk // radius) - 1, 0, (L - W) // radius)


def _win_specs(G, W, HD, start, with_g):
    E = pl.Element
    st = (lambda n: 0) if start is None else start
    if with_g:
        return pl.BlockSpec((E(1), E(G), E(W), E(HD)), lambda b, n: (b, 0, st(n), 0))
    return pl.BlockSpec((E(1), E(W), E(HD)), lambda b, n: (b, st(n), 0))


def _attn_delta(do, o, *, dlse=None, lse=None, sink=None, heads=None, name="attn_delta"):
    HD = HEAD_DIM
    (NB, G), L = (heads, do.shape[0]) if heads is not None else (do.shape[:2], do.shape[2])
    bl = _row_tile(L, 1024)

    def body(*refs):
        refs = list(refs)
        sink_ref = refs.pop(0) if sink is not None else None
        do_ref, o_ref = refs.pop(0), refs.pop(0)
        dlse_ref = refs.pop(0) if dlse is not None else None
        lse_ref = refs.pop(0) if sink is not None else None
        delta_ref = refs.pop(0)
        b = pl.program_id(0)
        if heads is None:
            delta = jnp.sum(do_ref[0].astype(F32) * o_ref[0].astype(F32), axis=-1, keepdims=True)
        else:
            prod = do_ref[...].astype(F32) * o_ref[...].astype(F32)
            delta = jnp.concatenate([jnp.sum(prod[:, g * HD:(g + 1) * HD], axis=-1, keepdims=True)[None] for g in range(G)])
        if dlse is not None:
            delta = delta - dlse_ref[0]
        delta_ref[0] = delta
        if sink is not None:
            ds_ref = refs.pop(0)

            @pl.when(pl.program_id(1) == 0)
            def _():
                ds_ref[...] = jnp.zeros_like(ds_ref)

            for g in range(G):
                ps = jnp.exp(sink_ref[b * G + g] - lse_ref[0, g]) * delta[g]
                ds_ref[0, g] -= jnp.sum(ps)

    blk = _q_spec(None if heads is None else G, G, bl, HD, lambda b, n: (b, n))
    col = pl.BlockSpec((1, G, bl, 1), lambda b, n: (b, 0, n, 0))
    ins, specs = [do, o], [blk, blk]
    if dlse is not None:
        ins, specs = ins + [dlse], specs + [col]
    out_shape = jax.ShapeDtypeStruct((NB, G, L, 1), F32)
    out_specs = col
    if sink is not None:
        ins, specs = [sink] + ins + [lse], [pl.BlockSpec(memory_space=pltpu.SMEM)] + specs + [col]
        out_shape = (out_shape, jax.ShapeDtypeStruct((NB, G, 1, LANES), F32))
        out_specs = (col, pl.BlockSpec((1, G, 1, LANES), lambda b, n: (b, 0, 0, 0)))
    return _pc(body, name=name, out_shape=out_shape, grid=(NB, L // bl), in_specs=specs, out_specs=out_specs,
               compiler_params=_cp("parallel", "arbitrary"))(*ins)


def _attn_fwd_full(q, k, v, *, bq, tk, heads=None, name="attn_fwd_full"):
    NB, G, L, HD = _q_dims(q, k, heads)
    Lk = k.shape[1]
    nq, nk = L // bq, Lk // tk
    rows = G * bq
    chains = _chain_slices(G, bq, min(tk, CHAIN_NARROW_KEYS))

    def body(q_ref, k_ref, v_ref, o_ref, lse_ref, m_sc, acc_sc, q_sc):
        j = pl.program_id(2)

        @pl.when(j == 0)
        def _():
            m_sc[...] = jnp.full_like(m_sc, NEG)
            acc_sc[...] = jnp.zeros_like(acc_sc)
            for g, hr, sl in chains:
                q_sc[sl] = q_ref[_q_at(heads, g, hr, HD)] * ATTN_SCALE

        kk = k_ref[0]
        vv = _v_ones(v_ref[0])
        st = [dict() for _ in chains]

        def scores(c):
            st[c]["s"] = lax.dot_general(q_sc[chains[c][2]], kk, NT_DIMS, preferred_element_type=F32)

        def softmax(c):
            sl = chains[c][2]
            m_prev = m_sc[sl]
            m_new = jnp.maximum(m_prev, jnp.max(st[c]["s"], axis=1, keepdims=True))
            st[c]["p"] = jnp.exp(st[c].pop("s") - m_new).astype(BF16)
            st[c]["alpha"] = jnp.exp(m_prev - m_new)
            m_sc[sl] = m_new

        def values(c):
            sl = chains[c][2]
            acc_sc[sl] = st[c].pop("alpha") * acc_sc[sl] + jnp.dot(st[c].pop("p"), vv, preferred_element_type=F32)

        _skewed(len(chains), (scores, softmax, values))

        @pl.when(j == nk - 1)
        def _():
            for g, hr, sl in chains:
                acc = acc_sc[sl]
                l = acc[:, HD:HD + 1]
                o_ref[_q_at(heads, g, hr, HD)] = (acc[:, :HD] / l).astype(o_ref.dtype)
                lse_ref[0, g, hr, :] = m_sc[sl] + jnp.log(l)

    qspec = _q_spec(heads, G, bq, HD, lambda b, n, j: (b, n))
    kspec = pl.BlockSpec((1, tk, HD), lambda b, n, j: (b, j, 0))
    return _pc(body, name=name,
               out_shape=(jax.ShapeDtypeStruct(_q_shape(heads, NB, G, L, HD), BF16), jax.ShapeDtypeStruct((NB, G, L, 1), F32)),
               grid=(NB, nq, nk), in_specs=[qspec, kspec, kspec],
               out_specs=(qspec, pl.BlockSpec((1, G, bq, 1), lambda b, n, j: (b, 0, n, 0))),
               scratch_shapes=[pltpu.VMEM((rows, 1), F32), pltpu.VMEM((rows, 2 * HD), F32), pltpu.VMEM((rows, HD), BF16)],
               compiler_params=_cp("parallel", "parallel", "arbitrary"))(q, k, v)


def _attn_bwd_full(q, k, v, do, lse, o, *, bq, tk, heads=None, kv_per_head=False, name="attn_bwd_full"):
    NB, G, L, HD = _q_dims(q, k, heads, kv_per_head)
    Lk = k.shape[1]
    nq, nk = L // bq, Lk // tk
    chains = _chain_slices(G, bq, tk)
    KH = G if kv_per_head else 1
    assert not kv_per_head or nk == 1

    def body(q_ref, k_ref, v_ref, do_ref, lse_ref, o_ref, dqp_ref, dk_ref, dv_ref, dk_sc, dv_sc):
        n = pl.program_id(2)

        @pl.when(n == 0)
        def _():
            dk_sc[...] = jnp.zeros_like(dk_sc)
            dv_sc[...] = jnp.zeros_like(dv_sc)

        kks = [k_ref[h] for h in range(KH)]
        vvs = [v_ref[h] for h in range(KH)]
        st = [dict() for _ in chains]

        def scores(c):
            g, hr, _ = chains[c]
            at = _q_at(heads, g, hr, HD)
            st[c]["q"] = q_ref[at] * ATTN_SCALE
            st[c]["do"] = do_ref[at]
            st[c]["s"] = lax.dot_general(st[c]["q"], kks[g % KH], NT_DIMS, preferred_element_type=F32)
            st[c]["dp"] = lax.dot_general(st[c]["do"], vvs[g % KH], NT_DIMS, preferred_element_type=F32)
            st[c]["delta"] = jnp.sum(st[c]["do"].astype(F32) * o_ref[at].astype(F32), axis=-1, keepdims=True)

        def softmax(c):
            g, hr, _ = chains[c]
            p = jnp.exp(st[c].pop("s") - lse_ref[0, g, hr, :])
            st[c]["ds"] = (p * (st[c].pop("dp") - st[c].pop("delta"))).astype(BF16)
            st[c]["p"] = p.astype(BF16)

        def grads(c):
            g, hr, _ = chains[c]
            ds = st[c].pop("ds")
            dv_sc[g % KH] += lax.dot_general(st[c].pop("p"), st[c].pop("do"), TN_DIMS, preferred_element_type=F32)
            dk_sc[g % KH] += lax.dot_general(ds, st[c].pop("q"), TN_DIMS, preferred_element_type=F32)
            dqp_ref[(0,) + _q_at(heads, g, hr, HD)] = jnp.dot(ds, kks[g % KH], preferred_element_type=F32) * ATTN_SCALE

        _skewed(len(chains), (scores, softmax, grads))

        @pl.when(n == nq - 1)
        def _():
            dk_ref[...] = dk_sc[...].astype(dk_ref.dtype)
            dv_ref[...] = dv_sc[...].astype(dv_ref.dtype)

    qspec = _q_spec(heads, G, bq, HD, lambda b, m, n: (b, n))
    cspec = pl.BlockSpec((1, G, bq, 1), lambda b, m, n: (b, 0, n, 0))
    kspec = pl.BlockSpec((KH, tk, HD), lambda b, m, n: (b, m, 0))
    kv_shape = jax.ShapeDtypeStruct((NB * KH, Lk, HD), BF16)
    if heads is None:
        pspec = pl.BlockSpec((1, 1, G, bq, HD), lambda b, m, n: (m, b, 0, n, 0))
    else:
        pspec = pl.BlockSpec((1, bq, G * HD), lambda b, m, n: (m, n, b))
    dqp, dk, dv = _pc(body, name=name,
                      out_shape=(jax.ShapeDtypeStruct((nk,) + _q_shape(heads, NB, G, L, HD), F32), kv_shape, kv_shape),
                      grid=(NB, nk, nq), in_specs=[qspec, kspec, kspec, qspec, cspec, qspec],
                      out_specs=(pspec, kspec, kspec),
                      scratch_shapes=[pltpu.VMEM((KH, tk, HD), F32), pltpu.VMEM((KH, tk, HD), F32)],
                      compiler_params=_cp("parallel", "parallel", "arbitrary"))(q, k, v, do, lse, o)
    if nk == 1:
        return dqp[0].astype(BF16), dk, dv
    bl = _row_tile(L, 512)

    def sum_body(p_ref, o_ref):
        acc = p_ref[0]
        for j in range(1, nk):
            acc = acc + p_ref[j]
        o_ref[...] = acc.astype(o_ref.dtype)

    if heads is None:
        pspec = pl.BlockSpec((nk, 1, G, bl, HD), lambda b, n: (0, b, 0, n, 0))
    else:
        pspec = pl.BlockSpec((nk, bl, G * HD), lambda b, n: (0, n, b))
    dq = _pc(sum_body, name=name + "_sum", out_shape=jax.ShapeDtypeStruct(_q_shape(heads, NB, G, L, HD), BF16),
             grid=(NB, L // bl), in_specs=[pspec], out_specs=_q_spec(heads, G, bl, HD, lambda b, n: (b, n)),
             compiler_params=_cp("parallel", "parallel"))(dqp)
    return dq, dk, dv


def _attn_fwd_win(q, k, v, *, radius, sink=None, bq, heads=None, kv_per_head=False, carry=None, name="attn_fwd_win"):
    NB, G, L, HD = _q_dims(q, k, heads, kv_per_head)
    W, start = _window(k.shape[1], bq, radius)
    chains = _chain_slices(G, bq, W)

    def body(*refs):
        if sink is not None:
            sink_ref, *refs = refs
        q_ref, k_ref, v_ref, o_ref, lse_ref = refs
        b, n = pl.program_id(0), pl.program_id(1)
        kks = [k_ref[g] for g in range(G)] if kv_per_head else [k_ref[0]] * G
        vvs = [_v_ones(v_ref[g]) for g in range(G)] if kv_per_head else [_v_ones(v_ref[0])] * G
        st = [dict() for _ in chains]

        def scores(c):
            g, hr, _ = chains[c]
            s = lax.dot_general(q_ref[_q_at(heads, g, hr, HD)] * ATTN_SCALE, kks[g], NT_DIMS, preferred_element_type=F32)
            if radius is not None:
                qpos = n * bq + hr.start + lax.broadcasted_iota(jnp.int32, (hr.stop - hr.start, 1), 0)
                kpos = start(n) + lax.broadcasted_iota(jnp.int32, (1, W), 1)
                s = jnp.where(jnp.abs(qpos - kpos) <= radius, s, NEG)
            st[c]["s"] = s

        def softmax(c):
            g = chains[c][0]
            m = jnp.max(st[c]["s"], axis=1, keepdims=True)
            if sink is not None:
                m = jnp.maximum(m, sink_ref[b * G + g])
            st[c]["p"] = jnp.exp(st[c].pop("s") - m).astype(BF16)
            st[c]["m"] = m

        def values(c):
            g, hr, _ = chains[c]
            acc = jnp.dot(st[c].pop("p"), vvs[g], preferred_element_type=F32)
            m = st[c].pop("m")
            l = acc[:, HD:HD + 1]
            if sink is not None:
                l = l + jnp.exp(sink_ref[b * G + g] - m)
            o_ref[_q_at(heads, g, hr, HD)] = (acc[:, :HD] / l).astype(o_ref.dtype)
            lse_ref[0, g, hr, :] = m + jnp.log(l)

        _skewed(len(chains), (scores, softmax, values))

    qspec = _q_spec(heads, G, bq, HD, lambda b, n: (b, n))
    kspec = pl.BlockSpec((G, W, HD), lambda b, n: (0, 0, 0)) if kv_per_head else _win_specs(G, W, HD, start, False)
    ins, specs = [q, k, v], [qspec, kspec, kspec]
    if sink is not None:
        ins, specs = [sink] + ins, [pl.BlockSpec(memory_space=pltpu.SMEM)] + specs
    return _call(body, carry, ins, name=name,
                 out_shape=(jax.ShapeDtypeStruct(_q_shape(heads, NB, G, L, HD), BF16), jax.ShapeDtypeStruct((NB, G, L, 1), F32)),
                 grid=(NB, L // bq), in_specs=specs,
                 out_specs=(qspec, pl.BlockSpec((1, G, bq, 1), lambda b, n: (b, 0, n, 0))), sem=("parallel", "parallel"))


def _attn_dq_win(q, k, v, do, lse, delta, *, radius, bq, heads=None, name="attn_dq_win"):
    NB, G, L, HD = _q_dims(q, k, heads)
    W, start = _window(L, bq, radius)
    chains = _chain_slices(G, bq, W)

    def body(q_ref, k_ref, v_ref, do_ref, lse_ref, dl_ref, dq_ref):
        n = pl.program_id(1)
        kk, vv = k_ref[0], v_ref[0]
        kpos = start(n) + lax.broadcasted_iota(jnp.int32, (1, W), 1)
        st = [dict() for _ in chains]

        def scores(c):
            g, hr, _ = chains[c]
            at = _q_at(heads, g, hr, HD)
            st[c]["s"] = lax.dot_general(q_ref[at] * ATTN_SCALE, kk, NT_DIMS, preferred_element_type=F32)
            st[c]["dp"] = lax.dot_general(do_ref[at], vv, NT_DIMS, preferred_element_type=F32)

        def softmax(c):
            g, hr, _ = chains[c]
            qpos = n * bq + hr.start + lax.broadcasted_iota(jnp.int32, (hr.stop - hr.start, 1), 0)
            p = jnp.where(jnp.abs(qpos - kpos) <= radius, jnp.exp(st[c].pop("s") - lse_ref[0, g, hr, :]), 0.0)
            st[c]["ds"] = (p * (st[c].pop("dp") - dl_ref[0, g, hr, :])).astype(BF16)

        def grads(c):
            g, hr, _ = chains[c]
            dq = jnp.dot(st[c].pop("ds"), kk, preferred_element_type=F32) * ATTN_SCALE
            dq_ref[_q_at(heads, g, hr, HD)] = dq.astype(dq_ref.dtype)

        _skewed(len(chains), (scores, softmax, grads))

    qspec = _q_spec(heads, G, bq, HD, lambda b, n: (b, n))
    cspec = pl.BlockSpec((1, G, bq, 1), lambda b, n: (b, 0, n, 0))
    kspec = _win_specs(G, W, HD, start, False)
    return _pc(body, name=name, out_shape=jax.ShapeDtypeStruct(_q_shape(heads, NB, G, L, HD), BF16), grid=(NB, L // bq),
               in_specs=[qspec, kspec, kspec, qspec, cspec, cspec], out_specs=qspec,
               compiler_params=_cp("parallel", "parallel"))(q, k, v, do, lse, delta)


def _attn_dkv_win(q, k, v, do, lse, delta, *, radius, bk, heads=None, carry=None, name="attn_dkv_win"):
    NB, G, L, HD = _q_dims(q, k, heads)
    W, start = _window(L, bk, radius)
    chains = _chain_slices(G, W, bk)

    def body(q_ref, k_ref, v_ref, do_ref, lse_ref, dl_ref, dk_ref, dv_ref):
        m = pl.program_id(1)
        kk, vv = k_ref[0], v_ref[0]
        kpos = m * bk + lax.broadcasted_iota(jnp.int32, (1, bk), 1)
        st = [dict() for _ in chains]
        out = dict(dk=jnp.zeros((bk, HD), F32), dv=jnp.zeros((bk, HD), F32))

        def scores(c):
            g, hr, _ = chains[c]
            at = _q_at(heads, g, hr, HD)
            st[c]["q"] = q_ref[at] * ATTN_SCALE
            st[c]["do"] = do_ref[at]
            st[c]["s"] = lax.dot_general(st[c]["q"], kk, NT_DIMS, preferred_element_type=F32)
            st[c]["dp"] = lax.dot_general(st[c]["do"], vv, NT_DIMS, preferred_element_type=F32)

        def softmax(c):
            g, hr, _ = chains[c]
            qpos = start(m) + hr.start + lax.broadcasted_iota(jnp.int32, (hr.stop - hr.start, 1), 0)
            p = jnp.where(jnp.abs(qpos - kpos) <= radius, jnp.exp(st[c].pop("s") - lse_ref[0, g, hr, :]), 0.0)
            st[c]["ds"] = (p * (st[c].pop("dp") - dl_ref[0, g, hr, :])).astype(BF16)
            st[c]["p"] = p.astype(BF16)

        def grads(c):
            out["dv"] = out["dv"] + lax.dot_general(st[c].pop("p"), st[c].pop("do"), TN_DIMS, preferred_element_type=F32)
            out["dk"] = out["dk"] + lax.dot_general(st[c].pop("ds"), st[c].pop("q"), TN_DIMS, preferred_element_type=F32)

        _skewed(len(chains), (scores, softmax, grads))
        dk_ref[0] = out["dk"].astype(dk_ref.dtype)
        dv_ref[0] = out["dv"].astype(dv_ref.dtype)

    if heads is None:
        qspec = _win_specs(G, W, HD, start, True)
    else:
        qspec = pl.BlockSpec((pl.Element(W), pl.Element(G * HD)), lambda b, m: (start(m), b * G * HD))
    cspec = _win_specs(G, W, 1, start, True)
    kspec = pl.BlockSpec((1, bk, HD), lambda b, m: (b, m, 0))
    kv_shape = jax.ShapeDtypeStruct((NB, L, HD), BF16)
    return _call(body, carry, [q, k, v, do, lse, delta], name=name, out_shape=(kv_shape, kv_shape), grid=(NB, L // bk),
                 in_specs=[qspec, kspec, kspec, qspec, cspec, cspec], out_specs=(kspec, kspec), sem=("parallel", "parallel"))


def _attn(q, k, v, *, radius=None, sink=None, bq, tk=None, heads=None, kv_per_head=False, carry=None, tag):
    if radius is None and tk < k.shape[1]:
        return _attn_fwd_full(q, k, v, bq=bq, tk=tk, heads=heads, name=f"attn_fwd_{tag}")
    return _attn_fwd_win(q, k, v, radius=radius, sink=sink, bq=bq, heads=heads, kv_per_head=kv_per_head, carry=carry,
                         name=f"attn_fwd_{tag}")


def _attn_bwd(q, k, v, o, lse, do, *, radius=None, sink=None, dlse=None, bq, tk=None, heads=None, kv_per_head=False,
              carry=None, tag):
    if radius is None:
        assert carry is None
        return (*_attn_bwd_full(q, k, v, do, lse, o, bq=bq, tk=tk, heads=heads, kv_per_head=kv_per_head,
                                name=f"attn_bwd_{tag}"), None, None)
    nbg = None if heads is None else (k.shape[0], heads)
    if sink is not None:
        delta, ds = _attn_delta(do, o, lse=lse, sink=sink, heads=nbg, name=f"attn_delta_{tag}")
        dsink = ds[:, :, 0, 0].reshape(-1)
    else:
        delta, dsink = _attn_delta(do, o, dlse=dlse, heads=nbg, name=f"attn_delta_{tag}"), None
    dq = _attn_dq_win(q, k, v, do, lse, delta, radius=radius, bq=bq, heads=heads, name=f"attn_dq_{tag}")
    dk, dv, *arrived = _attn_dkv_win(q, k, v, do, lse, delta, radius=radius, bk=bq, heads=heads, carry=carry,
                                     name=f"attn_dkv_{tag}")
    return dq, dk, dv, dsink, (arrived[0] if arrived else None)


def _combine_fwd(o, lse, name="combine_fwd"):
    H, S, HD = o.shape
    tm = _row_tile(S, 512)

    def body(o_ref, lse_ref, t_ref):
        for g in range(GQA_GROUP):
            hs = [kv * GQA_GROUP + g for kv in range(N_KV_HEADS)]
            ls = [lse_ref[h] for h in hs]
            mx = functools.reduce(jnp.maximum, ls)
            es = [jnp.exp(l - mx) for l in ls]
            den = functools.reduce(jnp.add, es)
            for h, e in zip(hs, es):
                t_ref[h] = (o_ref[h].astype(F32) * (e / den)).astype(t_ref.dtype)

    blk = pl.BlockSpec((H, tm, HD), lambda i: (0, i, 0))
    col = pl.BlockSpec((H, tm, 1), lambda i: (0, i, 0))
    return _pc(body, name=name, out_shape=jax.ShapeDtypeStruct((H, S, HD), BF16), grid=(S // tm,),
               in_specs=[blk, col], out_specs=blk, compiler_params=_cp("parallel"))(o, lse)


def _combine_bwd(dt, o, lse, name="combine_bwd"):
    H, S, HD = o.shape
    tm = _row_tile(S, 512)

    def body(dt_ref, o_ref, lse_ref, do_ref, dlse_ref):
        for g in range(GQA_GROUP):
            hs = [kv * GQA_GROUP + g for kv in range(N_KV_HEADS)]
            ls = [lse_ref[h] for h in hs]
            mx = functools.reduce(jnp.maximum, ls)
            es = [jnp.exp(l - mx) for l in ls]
            den = functools.reduce(jnp.add, es)
            al = [e / den for e in es]
            dts = [dt_ref[h].astype(F32) for h in hs]
            da = [jnp.sum(d * o_ref[h].astype(F32), axis=-1, keepdims=True) for h, d in zip(hs, dts)]
            dot = functools.reduce(jnp.add, [a * d for a, d in zip(al, da)])
            for h, a, d, dd in zip(hs, al, da, dts):
                do_ref[h] = (dd * a).astype(do_ref.dtype)
                dlse_ref[h] = a * (d - dot)

    blk = pl.BlockSpec((H, tm, HD), lambda i: (0, i, 0))
    col = pl.BlockSpec((H, tm, 1), lambda i: (0, i, 0))
    return _pc(body, name=name,
               out_shape=(jax.ShapeDtypeStruct((H, S, HD), BF16), jax.ShapeDtypeStruct((H, S, 1), F32)),
               grid=(S // tm,), in_specs=[blk, blk, col], out_specs=(blk, col), compiler_params=_cp("parallel"))(dt, o, lse)


def _position():
    x, y, c = lax.axis_index("x"), lax.axis_index("y"), lax.axis_index("c")
    return x, y, c


def _peer(pos, k):
    x, y, c = pos
    return (1 - x if k & 4 else x, 1 - y if k & 2 else y, 1 - c if k & 1 else c)


def _linear(p):
    return 4 * p[0] + 2 * p[1] + p[2]


def _exchange_steps(s_ref, r_ref, send_sems, recv_sems, local_sem, gather):
    pos = _position()
    me = _linear(pos)
    own = pltpu.make_async_copy(s_ref if gather else s_ref.at[me], r_ref.at[me], local_sem)
    peers = range(1, N_DEV)

    def sems(k):
        return dict(send_sem=send_sems.at[k - 1], recv_sem=recv_sems.at[k - 1], device_id=_peer(pos, k), device_id_type=MESH)

    def send(k):
        src = s_ref if gather else s_ref.at[_linear(_peer(pos, k))]
        return pltpu.make_async_remote_copy(src_ref=src, dst_ref=r_ref.at[me], **sems(k))

    def arrival(k):
        slot = r_ref.at[_linear(_peer(pos, k))]
        return pltpu.make_async_remote_copy(src_ref=slot, dst_ref=slot, **sems(k))

    def start():
        own.start()
        for k in peers:
            send(k).start()

    def wait():
        for k in peers:
            arrival(k).wait_recv()
        for k in peers:
            send(k).wait_send()
        own.wait()

    return start, wait


EXCHANGE_SEMS = [pltpu.SemaphoreType.DMA((N_DEV - 1,)), pltpu.SemaphoreType.DMA((N_DEV - 1,)), pltpu.SemaphoreType.DMA]


def _exchange(buf, gather, name):
    def body(s_ref, r_ref, *sems):
        start, wait = _exchange_steps(s_ref, r_ref, *sems, gather)
        start()
        wait()

    hbm = pl.BlockSpec(memory_space=pltpu.HBM)
    out_shape = ((N_DEV,) + buf.shape) if gather else buf.shape
    return _pc(body, name=name, out_shape=jax.ShapeDtypeStruct(out_shape, buf.dtype), in_specs=[hbm], out_specs=hbm,
               scratch_shapes=list(EXCHANGE_SEMS))(buf)


def _call(body, carry, ins, *, name, out_shape, grid, in_specs, out_specs, scratch_shapes=(), sem):
    if carry is None:
        return _pc(body, name=name, out_shape=tuple(out_shape), grid=grid, in_specs=list(in_specs),
                   out_specs=tuple(out_specs), scratch_shapes=list(scratch_shapes), compiler_params=_cp(*sem))(*ins)
    buf, gather = carry
    n_in, n_out, n_sc = len(ins), len(out_shape), len(scratch_shapes)

    def wrapped(*refs):
        in_refs, buf_ref = refs[:n_in], refs[n_in]
        out_refs, recv_ref = refs[n_in + 1:n_in + 1 + n_out], refs[n_in + 1 + n_out]
        rest = refs[n_in + 2 + n_out:]
        first = functools.reduce(jnp.logical_and, [pl.program_id(a) == 0 for a in range(len(grid))])
        last = functools.reduce(jnp.logical_and, [pl.program_id(a) == grid[a] - 1 for a in range(len(grid))])

        @pl.when(first)
        def _():
            _exchange_steps(buf_ref, recv_ref, *rest[n_sc:], gather)[0]()

        body(*in_refs, *out_refs, *rest[:n_sc])

        @pl.when(last)
        def _():
            _exchange_steps(buf_ref, recv_ref, *rest[n_sc:], gather)[1]()

    hbm = pl.BlockSpec(memory_space=pltpu.HBM)
    recv_shape = ((N_DEV,) + buf.shape) if gather else buf.shape
    return _pc(wrapped, name=name, out_shape=(*out_shape, jax.ShapeDtypeStruct(recv_shape, buf.dtype)), grid=grid,
               in_specs=[*in_specs, hbm], out_specs=(*out_specs, hbm), scratch_shapes=[*scratch_shapes, *EXCHANGE_SEMS],
               compiler_params=_cp(*(("arbitrary",) * len(grid))))(*ins, buf)


def _reduce_adamw(recv, w, m, v, name):
    _, R, C = recv.shape
    tr = _row_tile(R, 512)

    def body(r_ref, w_ref, m_ref, v_ref, g_ref, d_ref, nm_ref, nv_ref):
        g = r_ref[0].astype(F32)
        for j in range(1, N_DEV):
            g = g + r_ref[j].astype(F32)
        g_ref[...] = g
        nm = ADAM_B1 * m_ref[...] + (1.0 - ADAM_B1) * g
        nv = ADAM_B2 * v_ref[...] + (1.0 - ADAM_B2) * jnp.square(g)
        m_hat = nm / (1.0 - ADAM_B1 ** ADAM_STEP)
        v_hat = nv / (1.0 - ADAM_B2 ** ADAM_STEP)
        d_ref[...] = -ADAM_LR * (m_hat / (jnp.sqrt(v_hat) + ADAM_EPS) + ADAM_WD * w_ref[...])
        nm_ref[...] = nm
        nv_ref[...] = nv

    row = pl.BlockSpec((tr, C), lambda i: (i, 0))
    out = jax.ShapeDtypeStruct((R, C), F32)
    return _pc(body, name=name, out_shape=(out, out, out, out), grid=(R // tr,),
               in_specs=[pl.BlockSpec((N_DEV, tr, C), lambda i: (0, i, 0)), row, row, row],
               out_specs=(row, row, row, row), compiler_params=_cp("parallel"))(recv, w, m, v)


BIG = (("w_in", 2), ("w_mem_kv", 1), ("w_o", 1), ("w_gate_up", 2), ("w_down", 1))
SMALL = ("mem_norm_g", "g_mix_pre", "g_mix_post", "attn_sink", "qk_norm_g", "g_ffn_pre", "g_ffn_post")
SMALL_W = 1024
FIRST, REST = BIG[:1], BIG[1:]


def _pack_local(shards, dtype):
    return jnp.concatenate([s.astype(dtype).reshape(-1, LANES) for s in shards], axis=0)


def _unpack_local(flat, shapes):
    out, r = [], 0
    for shp in shapes:
        n = shp[0] * shp[1] * shp[2] // LANES
        out.append(flat[r:r + n].reshape(shp))
        r += n
    return out


def _unpack_gathered(g, shapes, names=BIG):
    out, r = [], 0
    for (name, dim), shp in zip(names, shapes):
        n = shp[0] * shp[1] * shp[2] // LANES
        t = g[:, r:r + n].reshape((N_DEV,) + tuple(shp))
        if dim == 2:
            t = t.transpose(1, 2, 0, 3).reshape(shp[0], shp[1], N_DEV * shp[2])
        else:
            t = t.transpose(1, 0, 2, 3).reshape(shp[0], N_DEV * shp[1], shp[2])
        out.append(t)
        r += n
    return out


def _pack_for_scatter(full, shapes, dtype, names=BIG):
    parts = []
    for (name, dim), shp, t in zip(names, shapes, full):
        if dim == 2:
            t = t.reshape(shp[0], shp[1], N_DEV, shp[2]).transpose(2, 0, 1, 3)
        else:
            t = t.reshape(shp[0], N_DEV, shp[1], shp[2]).transpose(1, 0, 2, 3)
        parts.append(t.astype(dtype).reshape(N_DEV, -1, LANES))
    return jnp.concatenate(parts, axis=1)


def _pack_small(arrs):
    flat = jnp.concatenate([a.reshape(-1) for a in arrs])
    pad = (-flat.shape[0]) % (8 * SMALL_W)
    return jnp.pad(flat, (0, pad)).reshape(-1, SMALL_W)


def _unpack_small(flat, shapes):
    flat = flat.reshape(-1)
    out, r = [], 0
    for shp in shapes:
        n = 1
        for d in shp:
            n *= d
        out.append(flat[r:r + n].reshape(shp))
        r += n
    return out


def _heads(t, nb, g):
    S = t.shape[0]
    return t.reshape(S, nb, g, HEAD_DIM).transpose(1, 2, 0, 3)


def _unheads(t):
    nb, g, S, hd = t.shape
    return t.transpose(2, 0, 1, 3).reshape(S, nb * g * hd)


def _dilate(t, dil):
    S = t.shape[0]
    g = t.shape[1] // HEAD_DIM
    return t.reshape(S // dil, dil, g, HEAD_DIM).transpose(1, 2, 0, 3)


def _undilate(t):
    dil, g, L, w = t.shape
    return t.transpose(1, 2, 0, 3).reshape(g, L * dil, w)


FULL_BQ_FWD, FULL_TK_FWD = 512, 8192
FULL_BQ_BWD, FULL_TK_BWD = 1024, 2048


def _mixer_fwd(kind, pr, kv, sink, li, carry=None):
    S = pr.shape[0]
    if kind == 0:
        tok, lse, *arrived = _attn(pr, *kv, radius=A_RADIUS, sink=sink, bq=min(256, S), heads=GQA_GROUP, carry=carry,
                                   tag=f"a{li}")
        return tok, lse, (arrived[0] if arrived else None)
    assert carry is None
    if kind == 1:
        tok, lse = _attn(pr, *kv, bq=min(FULL_BQ_FWD, S), tk=min(FULL_TK_FWD, S), heads=GQA_GROUP, tag=f"b{li}")
        return tok, lse, None
    saved, outs, lses = [], [], []
    for g, (window, dil) in enumerate(C_GROUPS):
        q = _dilate(pr[:, g * GQA_GROUP * HEAD_DIM:(g + 1) * GQA_GROUP * HEAD_DIM], dil)
        k = _dilate(pr[:, Q_W + g * HEAD_DIM:Q_W + (g + 1) * HEAD_DIM], dil)[:, 0]
        v = _dilate(pr[:, QK_W + g * HEAD_DIM:QK_W + (g + 1) * HEAD_DIM], dil)[:, 0]
        o, lse = _attn(q, k, v, radius=window // (2 * dil), bq=min(256, S // dil), tag=f"c{li}g{g}")
        saved.append((q, k, v, o, lse))
        outs.append(_undilate(o))
        lses.append(_undilate(lse))
    o_all, lse_all = jnp.concatenate(outs, 0), jnp.concatenate(lses, 0)
    tok = _combine_fwd(o_all, lse_all, name=f"combine_fwd_{li}")
    return tok.transpose(1, 0, 2).reshape(S, Q_W), (saved, o_all, lse_all), None


def _mixer_bwd(kind, dcat, pr, kv, cat, saved, sink, li, carry=None):
    S = dcat.shape[0]
    if kind == 0:
        return _attn_bwd(pr, *kv, cat, saved, dcat, radius=A_RADIUS, sink=sink, bq=min(256, S), heads=GQA_GROUP, carry=carry,
                         tag=f"a{li}")
    assert carry is None
    if kind == 1:
        return _attn_bwd(pr, *kv, cat, saved, dcat, bq=min(FULL_BQ_BWD, S), tk=min(FULL_TK_BWD, S), heads=GQA_GROUP,
                         tag=f"b{li}")
    per_group, o_all, lse_all = saved
    dt = dcat[:, :Q_W].reshape(S, N_TOK_HEADS, HEAD_DIM).transpose(1, 0, 2)
    do_all, dlse_all = _combine_bwd(dt, o_all, lse_all, name=f"combine_bwd_{li}")
    dqs, dks, dvs = [], [], []
    for g, (window, dil) in enumerate(C_GROUPS):
        q, k, v, o, lse = per_group[g]
        L = S // dil
        hs = slice(g * GQA_GROUP, (g + 1) * GQA_GROUP)
        do = do_all[hs].reshape(GQA_GROUP, L, dil, HEAD_DIM).transpose(2, 0, 1, 3)
        dlse = dlse_all[hs].reshape(GQA_GROUP, L, dil, 1).transpose(2, 0, 1, 3)
        dq, dk, dv, _, _ = _attn_bwd(q, k, v, o, lse, do, radius=window // (2 * dil), dlse=dlse, bq=min(256, L),
                                     tag=f"c{li}g{g}")
        dqs.append(dq.transpose(2, 0, 1, 3).reshape(S, GQA_GROUP * HEAD_DIM))
        dks.append(dk.transpose(1, 0, 2).reshape(S, HEAD_DIM))
        dvs.append(dv.transpose(1, 0, 2).reshape(S, HEAD_DIM))
    return jnp.concatenate(dqs, 1), jnp.concatenate(dks, 1), jnp.concatenate(dvs, 1), None, None


def kernel(x, mem, mem_norm_g, w_in, w_mem_kv, w_o, g_mix_pre, g_mix_post, attn_sink, qk_norm_g, w_gate_up, w_down, g_ffn_pre, g_ffn_post, loss_target, m_mem_norm_g, m_w_in, m_w_mem_kv, m_w_o, m_g_mix_pre, m_g_mix_post, m_attn_sink, m_qk_norm_g, m_w_gate_up, m_w_down, m_g_ffn_pre, m_g_ffn_post, v_mem_norm_g, v_w_in, v_w_mem_kv, v_w_o, v_g_mix_pre, v_g_mix_post, v_attn_sink, v_qk_norm_g, v_w_gate_up, v_w_down, v_g_ffn_pre, v_g_ffn_post):
    given = dict(locals())
    depth = w_in.shape[0]
    S, D = x.shape[1], x.shape[2]
    def shapes_of(names):
        return [(1,) + tuple(given[n].shape[1:]) for n, _ in names]

    def layer_pack(pre, l, dtype, names=BIG):
        return _pack_local([given[pre + n][l:l + 1] for n, _ in names], dtype)

    def layer_weights(gathered, names=BIG):
        return [t[0] for t in _unpack_gathered(gathered, shapes_of(names), names)]

    W = [None] * depth
    W[0] = layer_weights(_exchange(layer_pack("", 0, BF16, FIRST), True, "gather_w0_in"), FIRST)

    tabs = _rope_tables(S)
    mem_n = _rms_fwd(mem[0], mem_norm_g[None], BF16, name="rms_mem")

    saved = []
    xc = x[0]
    for i in range(depth):
        kind = i % N_MIXERS
        (tab, shift) = tabs[1] if kind == 1 else tabs[0]
        sink = attn_sink[i // N_MIXERS] if kind == 0 else None
        qk_gain = _qk_gain_row(qk_norm_g[i // N_MIXERS]) if kind == 1 else None
        carry = (layer_pack("", 0, BF16, REST), True) if i == 0 else None
        h, proj, *arrived = _mm(xc, W[i][0], F32, pre_g=g_mix_pre[i][None], carry=carry, name=f"mm_in_{i}")
        if carry is not None:
            W[0] = W[0] + layer_weights(arrived[0], REST)
        W_in, W_mkv, W_o, W_gu, W_dn = W[i]
        if kind == 2:
            pr, kv = _headprep_fwd(proj, tab, shift, qk_gain, name=f"headprep_fwd_{i}"), None
        else:
            pr, *kv = _headprep_fwd(proj, tab, shift, qk_gain, kv_heads=True, name=f"headprep_fwd_{i}")
        carry = (layer_pack("", 1, BF16), True) if i == 0 and depth > 1 else None
        tok, mix_saved, arrived = _mixer_fwd(kind, pr, kv, sink, i, carry)
        if carry is not None:
            W[1] = layer_weights(arrived)
        (mkv,) = _mm(mem_n, W_mkv, BF16, name=f"mm_mkv_{i}")
        qm = pr[:, QK_W + KV_W:]
        km = _heads(mkv[:, :QM_W], N_MEM_HEADS, 1)[:, 0]
        vm = _heads(mkv[:, QM_W:], N_MEM_HEADS, 1)[:, 0]
        mo, mlse = _attn(qm, km, vm, bq=min(256, S), tk=km.shape[1], heads=N_MEM_HEADS, kv_per_head=True, tag=f"m{i}")
        cat = jnp.concatenate([tok, mo], axis=1)
        o, x1 = _mm(cat, W_o, F32, post=(g_mix_post[i][None], xc), name=f"mm_o_{i}")
        carry = (layer_pack("", i + 2, BF16), True) if i + 2 < depth else None
        h2, gu, act, *arrived = _mm(x1, W_gu, BF16, pre_g=g_ffn_pre[i][None], swiglu=True, carry=carry, name=f"mm_gu_{i}")
        if carry is not None:
            W[i + 2] = layer_weights(arrived[0])
        f, x2 = _mm(act, W_dn, F32, post=(g_ffn_post[i][None], x1), name=f"mm_dn_{i}")
        saved.append(dict(x=xc, h=h, proj=proj, pr=pr, kv=kv, mix=mix_saved, qm=qm, km=km, vm=vm, mo=mo, mlse=mlse, cat=cat, o=o,
                          x1=x1, h2=h2, gu=gu, act=act, f=f))
        xc = x2

    dy, sq = _loss_head(xc, loss_target[0], name="loss_head")
    loss = lax.psum(sq[0, 0] * (0.5 / D), ("x", "y", "c"))

    grads = {n: [None] * depth for n in ("w_in", "w_mem_kv", "w_o", "w_gate_up", "w_down", "g_mix_pre", "g_mix_post",
                                         "g_ffn_pre", "g_ffn_post")}
    d_sink = [jnp.zeros((N_TOK_HEADS,), F32) for _ in range(attn_sink.shape[0])]
    d_qkg = [jnp.zeros((2, HEAD_DIM), F32) for _ in range(qk_norm_g.shape[0])]
    dmem_n = jnp.zeros((mem.shape[1], D), F32)
    recv = [None] * depth

    def scatter_pack(l, names=BIG):
        return _pack_for_scatter([grads[n][l][None] for n, _ in names], shapes_of(names), BF16, names)

    dx = dy
    for i in reversed(range(depth)):
        kind = i % N_MIXERS
        sv = saved[i]
        (tab, shift) = tabs[1] if kind == 1 else tabs[0]
        sink = attn_sink[i // N_MIXERS] if kind == 0 else None
        W_in, W_mkv, W_o, W_gu, W_dn = W[i]
        df, dgu, dg = _mm(dx, W_dn, BF16, nt=True, gu=sv["gu"], pre_bwd=(sv["f"], g_ffn_post[i][None]), name=f"mmb_dn_{i}")
        grads["g_ffn_post"][i] = dg[0]
        grads["w_down"][i] = _mm_tn(sv["act"], df, name=f"mmw_dn_{i}")
        dx1, dg = _mm(dgu, W_gu, F32, nt=True, post_bwd=(sv["x1"], g_ffn_pre[i][None], dx), name=f"mmb_gu_{i}")
        grads["g_ffn_pre"][i] = dg[0]
        if i + 1 < depth:
            grads["w_gate_up"][i], recv[i + 1] = _mm_tn(sv["h2"], dgu, carry=(scatter_pack(i + 1), False), name=f"mmw_gu_{i}")
        else:
            grads["w_gate_up"][i] = _mm_tn(sv["h2"], dgu, name=f"mmw_gu_{i}")
        do, dcat, dg = _mm(dx1, W_o, BF16, nt=True, pre_bwd=(sv["o"], g_mix_post[i][None]), name=f"mmb_o_{i}")
        grads["g_mix_post"][i] = dg[0]
        grads["w_o"][i] = _mm_tn(sv["cat"], do, name=f"mmw_o_{i}")
        dqm, dkm, dvm, _, _ = _attn_bwd(sv["qm"], sv["km"], sv["vm"], sv["mo"], sv["mlse"], dcat[:, Q_W:], bq=min(256, S),
                                        tk=sv["km"].shape[1], heads=N_MEM_HEADS, kv_per_head=True, tag=f"m{i}")
        dmkv = jnp.concatenate([_unheads(dkm[:, None]), _unheads(dvm[:, None])], axis=1).astype(BF16)
        grads["w_mem_kv"][i] = _mm_tn(mem_n, dmkv, name=f"mmw_mkv_{i}")
        dmem_n = dmem_n + _mm(dmkv, W_mkv, F32, nt=True, name=f"mmb_mkv_{i}")[0]
        carry = (scatter_pack(0, REST), False) if i == 0 else None
        dq, dk, dv, dsink, recv0_rest = _mixer_bwd(kind, dcat, sv["pr"], sv["kv"], sv["cat"], sv["mix"], sink, i, carry)
        if dsink is not None:
            d_sink[i // N_MIXERS] = dsink
        dpr = jnp.concatenate([dq, dk, dv, dqm], axis=1) if kind == 2 else (dq, dk, dv, dqm)
        if kind == 1:
            dproj, dgc = _headprep_bwd(dpr, tab, shift, sv["proj"], _qk_gain_row(qk_norm_g[i // N_MIXERS]),
                                       name=f"headprep_bwd_{i}")
            d_qkg[i // N_MIXERS] = jnp.stack([dgc[0, :Q_W].reshape(N_TOK_HEADS, HEAD_DIM).sum(0),
                                              dgc[0, Q_W:QK_W].reshape(N_KV_HEADS, HEAD_DIM).sum(0)])
        else:
            dproj = _headprep_bwd(dpr, tab, shift, name=f"headprep_bwd_{i}")
        dx, dg = _mm(dproj, W_in, F32, nt=True, post_bwd=(sv["x"], g_mix_pre[i][None], dx1), name=f"mmb_in_{i}")
        grads["g_mix_pre"][i] = dg[0]
        grads["w_in"][i] = _mm_tn(sv["h"], dproj, name=f"mmw_in_{i}")
    _, dg_mem = _rms_bwd(mem[0], mem_norm_g[None], dmem_n, BF16, name="rmsb_mem")

    def update(received, l, names, tag):
        res = _reduce_adamw(received, *[layer_pack(pre, l, F32, names) for pre in ("", "m_", "v_")], name=f"adamw_{tag}")
        return [_unpack_local(r, shapes_of(names)) for r in res]

    recv0_first = _exchange(scatter_pack(0, FIRST), False, "scatter_g0_in")
    per_layer = [[a + b for a, b in zip(update(recv0_first, 0, FIRST, "0_in"), update(recv0_rest, 0, REST, "0"))]]
    per_layer += [update(recv[l], l, BIG, str(l)) for l in range(1, depth)]
    big_out = [[jnp.concatenate(ts, axis=0) for ts in zip(*[per_layer[l][j] for l in range(depth)])] for j in range(4)]

    small_grads = dict(mem_norm_g=dg_mem[0], g_mix_pre=jnp.stack(grads["g_mix_pre"]), g_mix_post=jnp.stack(grads["g_mix_post"]),
                       attn_sink=jnp.stack(d_sink), qk_norm_g=jnp.stack(d_qkg), g_ffn_pre=jnp.stack(grads["g_ffn_pre"]),
                       g_ffn_post=jnp.stack(grads["g_ffn_post"]))
    sg = _pack_small([small_grads[n] for n in SMALL])
    srecv = _exchange(sg, True, "gather_small_grads")
    spacked = lambda pre: _pack_small([given[pre + n] for n in SMALL])
    gs, ds, ms, vs = _reduce_adamw(srecv, spacked(""), spacked("m_"), spacked("v_"), name="adamw_small")

    out = {}
    for pre, fb, fs in zip(("grad_", "delta_", "new_m_", "new_v_"), big_out, (gs, ds, ms, vs)):
        for (n, _), t in zip(BIG, fb):
            out[pre + n] = t
        for n, t in zip(SMALL, _unpack_small(fs, [given[n].shape for n in SMALL])):
            out[pre + n] = t
    order = ("mem_norm_g", "w_in", "w_mem_kv", "w_o", "g_mix_pre", "g_mix_post", "attn_sink", "qk_norm_g", "w_gate_up",
             "w_down", "g_ffn_pre", "g_ffn_post")
    return (loss, dx[None], *[out[p + n] for p in ("grad_", "delta_", "new_m_", "new_v_") for n in order])
```

```python
import functools

import jax
import jax.numpy as jnp
from jax import lax
from jax.experimental import pallas as pl
from jax.experimental.pallas import tpu as pltpu

F32 = jnp.float32
BF16 = jnp.bfloat16

HEAD_DIM = 64
N_TOK_HEADS = 12
N_KV_HEADS = 3
GQA_GROUP = 4
N_MEM_HEADS = 4
Q_W = N_TOK_HEADS * HEAD_DIM
KV_W = N_KV_HEADS * HEAD_DIM
QM_W = N_MEM_HEADS * HEAD_DIM
QK_W = Q_W + KV_W
IN_W = Q_W + 2 * KV_W + QM_W
N_MIXERS = 3
A_RADIUS = 128
C_GROUPS = ((128, 1), (512, 4), (2048, 16))
ROPE_THETA = 500000.0
ROPE_DIMS = HEAD_DIM // 4
AXIAL_THETA = 10000.0
GRID_W = 64
EPS = 1e-6
ATTN_SCALE = HEAD_DIM ** -0.5
NEG = -1e30

ADAM_LR = 0.001
ADAM_B1 = 0.9
ADAM_B2 = 0.999
ADAM_EPS = 1e-08
ADAM_WD = 0.01
ADAM_STEP = 10

N_DEV = 8
LANES = 128
VMEM_LIMIT = 56 * 1024 * 1024
MESH = pl.DeviceIdType.MESH
MM_CHAIN_ROWS = 256
MM_ROWS_WIDE, MM_ROWS_NARROW = 512, 1024
MM_NARROW_COLS = 1408
NT_DIMS = (((1,), (1,)), ((), ()))
TN_DIMS = (((0,), (0,)), ((), ()))


def _pc(body, **kw):
    return pl.pallas_call(body, **kw)


def _cp(*sem):
    return pltpu.CompilerParams(dimension_semantics=sem, vmem_limit_bytes=VMEM_LIMIT)


def _row_tile(m, cap=512):
    t = cap
    while m % t:
        t //= 2
    return t


def _rms_fwd(x, g, out_dtype, name="rms_fwd"):
    M, D = x.shape
    tm = _row_tile(M)

    def body(x_ref, g_ref, o_ref):
        xv = x_ref[...]
        y = xv * lax.rsqrt(jnp.mean(xv * xv, axis=-1, keepdims=True) + EPS) * g_ref[...]
        o_ref[...] = y.astype(o_ref.dtype)

    row = pl.BlockSpec((tm, D), lambda i: (i, 0))
    vec = pl.BlockSpec((1, D), lambda i: (0, 0))
    return _pc(body, name=name, out_shape=jax.ShapeDtypeStruct((M, D), out_dtype), grid=(M // tm,),
               in_specs=[row, vec], out_specs=row, compiler_params=_cp("parallel"))(x, g)


def _rms_bwd_tile(xv, g, d):
    r = lax.rsqrt(jnp.mean(xv * xv, axis=-1, keepdims=True) + EPS)
    xh = xv * r
    dxh = d * g
    return r * (dxh - xh * jnp.mean(dxh * xh, axis=-1, keepdims=True)), jnp.sum(d * xh, axis=0, keepdims=True)


def _rms_bwd(x, g, dy, out_dtype, name="rms_bwd"):
    M, D = x.shape
    tm = _row_tile(M)

    def body(x_ref, g_ref, dy_ref, dx_ref, dg_ref):
        dx, dg = _rms_bwd_tile(x_ref[...], g_ref[...], dy_ref[...].astype(F32))
        dx_ref[...] = dx.astype(dx_ref.dtype)

        @pl.when(pl.program_id(0) == 0)
        def _():
            dg_ref[...] = jnp.zeros_like(dg_ref)

        dg_ref[...] += dg

    row = pl.BlockSpec((tm, D), lambda i: (i, 0))
    vec = pl.BlockSpec((1, D), lambda i: (0, 0))
    return _pc(body, name=name,
               out_shape=(jax.ShapeDtypeStruct((M, D), out_dtype), jax.ShapeDtypeStruct((1, D), F32)),
               grid=(M // tm,), in_specs=[row, vec, row], out_specs=(row, vec), compiler_params=_cp("arbitrary"))(x, g, dy)


def _loss_head(y, t, name="loss_head"):
    M, D = y.shape
    tm = _row_tile(M)

    def body(y_ref, t_ref, dy_ref, acc_ref):
        e = y_ref[...] - t_ref[...]
        dy_ref[...] = e * (1.0 / D)

        @pl.when(pl.program_id(0) == 0)
        def _():
            acc_ref[...] = jnp.zeros_like(acc_ref)

        acc_ref[...] += jnp.sum(e * e)

    row = pl.BlockSpec((tm, D), lambda i: (i, 0))
    return _pc(body, name=name,
               out_shape=(jax.ShapeDtypeStruct((M, D), F32), jax.ShapeDtypeStruct((8, LANES), F32)),
               grid=(M // tm,), in_specs=[row, row],
               out_specs=(row, pl.BlockSpec((8, LANES), lambda i: (0, 0))), compiler_params=_cp("arbitrary"))(y, t)


def _mm(a, w, out_dtype, nt=False, pre_g=None, swiglu=False, post=None, gu=None, pre_bwd=None, post_bwd=None, carry=None,
        name="mm"):
    M, K = a.shape
    N = w.shape[0] if nt else w.shape[1]
    tm = _row_tile(M, MM_ROWS_NARROW if N <= MM_NARROW_COLS and K <= 2 * MM_NARROW_COLS else MM_ROWS_WIDE)
    gain_grad = pre_bwd is not None or post_bwd is not None

    cr = min(MM_CHAIN_ROWS, tm)
    n_chains = tm // cr

    def body(*refs):
        refs = list(refs)
        a_ref, w_ref = refs.pop(0), refs.pop(0)
        pg_ref = refs.pop(0) if pre_g is not None else None
        g_ref, r_ref = (refs.pop(0), refs.pop(0)) if post is not None else (None, None)
        gu_ref = refs.pop(0) if gu is not None else None
        bwd_refs = [refs.pop(0) for _ in (pre_bwd or post_bwd or ())]
        lhs_ref = refs.pop(0) if pre_g is not None or pre_bwd is not None else None
        o_ref = refs.pop(0)
        act_ref = refs.pop(0) if swiglu else None
        x_ref = refs.pop(0) if post is not None else None
        st = [dict() for _ in range(n_chains)]

        def left(c):
            r = slice(c * cr, (c + 1) * cr)
            lhs = a_ref[r, :]
            if pre_g is not None:
                lhs = (lhs * lax.rsqrt(jnp.mean(lhs * lhs, axis=-1, keepdims=True) + EPS) * pg_ref[...]).astype(BF16)
                lhs_ref[r, :] = lhs
            if pre_bwd is not None:
                lhs, st[c]["dg"] = _rms_bwd_tile(bwd_refs[0][r, :].astype(F32), bwd_refs[1][...], lhs.astype(F32))
                lhs = lhs.astype(BF16)
                lhs_ref[r, :] = lhs
            st[c]["lhs"] = lhs

        def product(c):
            if nt:
                st[c]["acc"] = lax.dot_general(st[c].pop("lhs"), w_ref[...], NT_DIMS, preferred_element_type=F32)
            else:
                st[c]["acc"] = jnp.dot(st[c].pop("lhs"), w_ref[...], preferred_element_type=F32)

        def result(c):
            r = slice(c * cr, (c + 1) * cr)
            acc = st[c].pop("acc")
            if post_bwd is not None:
                dx, st[c]["dg"] = _rms_bwd_tile(bwd_refs[0][r, :], bwd_refs[1][...], acc)
                o_ref[r, :] = bwd_refs[2][r, :] + dx
            elif gu is None:
                o_ref[r, :] = acc.astype(o_ref.dtype)
            else:
                gate = gu_ref[r, :N].astype(F32)
                sig = 1.0 / (1.0 + jnp.exp(-gate))
                o_ref[r, :N] = (acc * gu_ref[r, N:].astype(F32) * (sig * (1.0 + gate * (1.0 - sig)))).astype(o_ref.dtype)
                o_ref[r, N:] = (acc * (gate * sig)).astype(o_ref.dtype)
            if swiglu:
                gate = acc[:, : N // 2]
                act_ref[r, :] = (gate * (1.0 / (1.0 + jnp.exp(-gate))) * acc[:, N // 2:]).astype(BF16)
            if post is not None:
                y = acc * lax.rsqrt(jnp.mean(acc * acc, axis=-1, keepdims=True) + EPS) * g_ref[...]
                x_ref[r, :] = r_ref[r, :] + y

        _skewed(n_chains, (left, product, result))
        if gain_grad:
            dg_ref = refs.pop(0)

            @pl.when(pl.program_id(0) == 0)
            def _():
                dg_ref[...] = jnp.zeros_like(dg_ref)

            dg_ref[...] += functools.reduce(jnp.add, [d["dg"] for d in st])

    row = lambda n: pl.BlockSpec((tm, n), lambda i: (i, 0))
    vec = lambda n: pl.BlockSpec((1, n), lambda i: (0, 0))
    ins = [a, w]
    specs = [row(K), pl.BlockSpec(w.shape, lambda i: (0, 0), pipeline_mode=pl.Buffered(1))]
    outs, ospecs = [], []
    if pre_g is not None:
        ins, specs = ins + [pre_g], specs + [vec(K)]
    if pre_g is not None or pre_bwd is not None:
        outs, ospecs = outs + [jax.ShapeDtypeStruct((M, K), BF16)], ospecs + [row(K)]
    if post is not None:
        ins, specs = ins + list(post), specs + [vec(N), row(N)]
    if gu is not None:
        ins, specs = ins + [gu], specs + [row(2 * N)]
        outs, ospecs = outs + [jax.ShapeDtypeStruct((M, 2 * N), BF16)], ospecs + [row(2 * N)]
    else:
        outs, ospecs = outs + [jax.ShapeDtypeStruct((M, N), out_dtype)], ospecs + [row(N)]
    if pre_bwd is not None:
        ins, specs = ins + list(pre_bwd), specs + [row(K), vec(K)]
    if post_bwd is not None:
        ins, specs = ins + list(post_bwd), specs + [row(N), vec(N), row(N)]
    if swiglu:
        outs, ospecs = outs + [jax.ShapeDtypeStruct((M, N // 2), BF16)], ospecs + [row(N // 2)]
    if post is not None:
        outs, ospecs = outs + [jax.ShapeDtypeStruct((M, N), F32)], ospecs + [row(N)]
    if gain_grad:
        D = K if pre_bwd is not None else N
        outs, ospecs = outs + [jax.ShapeDtypeStruct((1, D), F32)], ospecs + [vec(D)]
    return _call(body, carry, ins, name=name, out_shape=outs, grid=(M // tm,), in_specs=specs, out_specs=ospecs,
                 sem=("arbitrary" if gain_grad else "parallel",))


def _mm_tn(a, b, carry=None, name="mm_tn"):
    S, M = a.shape
    N = b.shape[1]
    tm = M if M <= 1408 else M // 2
    tn = N if N <= 2816 else N // 2
    ts = _row_tile(S, 2048 if tm * tn <= 1408 * 1024 else 1024)

    def body(a_ref, b_ref, o_ref):
        @pl.when(pl.program_id(2) == 0)
        def _():
            o_ref[...] = jnp.zeros_like(o_ref)

        o_ref[...] += lax.dot_general(a_ref[...], b_ref[...], TN_DIMS, preferred_element_type=F32)

    res = _call(body, carry, [a, b], name=name, out_shape=[jax.ShapeDtypeStruct((M, N), F32)],
                grid=(M // tm, N // tn, S // ts),
                in_specs=[pl.BlockSpec((ts, tm), lambda i, j, s: (s, i)), pl.BlockSpec((ts, tn), lambda i, j, s: (s, j))],
                out_specs=[pl.BlockSpec((tm, tn), lambda i, j, s: (i, j))], sem=("parallel", "parallel", "arbitrary"))
    return res[0] if carry is None else res


def _rope_tables(S):
    pos = jnp.arange(S, dtype=jnp.int32)

    def table(p, n_dims, theta):
        inv = theta ** (-(jnp.arange(0, n_dims, 2, dtype=F32) / n_dims))
        ang = p.astype(F32)[:, None] * inv[None, :]
        return jnp.cos(ang), jnp.sin(ang)

    one = lambda n: jnp.ones((S, n), F32)
    zero = lambda n: jnp.zeros((S, n), F32)
    cp, sp = table(pos, ROPE_DIMS, ROPE_THETA)
    rest = HEAD_DIM - ROPE_DIMS
    part = (jnp.concatenate([cp, cp, one(rest)], 1), jnp.concatenate([zero(8), sp, zero(rest)], 1),
            jnp.concatenate([-sp, zero(8), zero(rest)], 1))
    cr, sr = table(pos // GRID_W, HEAD_DIM // 2, AXIAL_THETA)
    cc, sc = table(pos % GRID_W, HEAD_DIM // 2, AXIAL_THETA)
    axial = (jnp.concatenate([cr, cr, cc, cc], 1), jnp.concatenate([zero(16), sr, zero(16), sc], 1),
             jnp.concatenate([-sr, zero(16), -sc, zero(16)], 1))
    rep = LANES // HEAD_DIM
    return (tuple(jnp.tile(t, (1, rep)) for t in part), ROPE_DIMS // 2), (tuple(jnp.tile(t, (1, rep)) for t in axial), HEAD_DIM // 4)


def _seg_mats():
    col = jnp.arange(IN_W)[:, None] // HEAD_DIM
    e = (col == jnp.arange(LANES)[None, :]).astype(BF16)
    return e, e.T


def _qk_gain_row(qk_g):
    return jnp.concatenate([jnp.tile(qk_g[0], N_TOK_HEADS), jnp.tile(qk_g[1], N_KV_HEADS),
                            jnp.ones((IN_W - QK_W,), F32)])[None, :]


def _rope_cols(tabs, tm):
    col = lax.broadcasted_iota(jnp.int32, (tm, IN_W), 1)
    qk = col < QK_W
    c, s_lo, s_hi = (jnp.tile(t[...], (1, IN_W // LANES)) for t in tabs)
    return jnp.where(qk, c, 1.0), jnp.where(qk, s_lo, 0.0), jnp.where(qk, s_hi, 0.0), qk


def _seg_mean(v, e_ref, et_ref):
    def split_dot(t, m_ref):
        hi = t.astype(BF16)
        lo = (t - hi.astype(F32)).astype(BF16)
        return jnp.dot(hi, m_ref[...], preferred_element_type=F32) + jnp.dot(lo, m_ref[...], preferred_element_type=F32)

    return split_dot(split_dot(v, e_ref) * (1.0 / HEAD_DIM), et_ref)


def _headprep_fwd(proj, tabs, shift, qk_gain=None, kv_heads=False, name="headprep_fwd"):
    S = proj.shape[0]
    tm = _row_tile(S, 512)
    norm = qk_gain is not None

    def body(*refs):
        refs = list(refs)
        p_ref, c_ref, lo_ref, hi_ref = (refs.pop(0) for _ in range(4))
        g_ref, e_ref, et_ref = (refs.pop(0) for _ in range(3)) if norm else (None, None, None)
        o_ref = refs.pop(0)
        x = p_ref[...]
        c, s_lo, s_hi, qk = _rope_cols((c_ref, lo_ref, hi_ref), tm)
        if norm:
            r = lax.rsqrt(_seg_mean(x * x, e_ref, et_ref) + EPS)
            x = x * jnp.where(qk, r, 1.0) * g_ref[...]
        y = (x * c + pltpu.roll(x, shift, 1) * s_lo + pltpu.roll(x, IN_W - shift, 1) * s_hi).astype(o_ref.dtype)
        o_ref[...] = y
        if kv_heads:
            k_ref, v_ref = refs
            for h in range(N_KV_HEADS):
                k_ref[h] = y[:, Q_W + h * HEAD_DIM:Q_W + (h + 1) * HEAD_DIM]
                v_ref[h] = y[:, QK_W + h * HEAD_DIM:QK_W + (h + 1) * HEAD_DIM]

    row = pl.BlockSpec((tm, IN_W), lambda i: (i, 0))
    tab = pl.BlockSpec((tm, LANES), lambda i: (i, 0))
    ins = [proj, *tabs]
    specs = [row, tab, tab, tab]
    if norm:
        e, et = _seg_mats()
        ins += [qk_gain, e, et]
        specs += [pl.BlockSpec((1, IN_W), lambda i: (0, 0)), pl.BlockSpec((IN_W, LANES), lambda i: (0, 0)),
                  pl.BlockSpec((LANES, IN_W), lambda i: (0, 0))]
    out_shape, out_specs = [jax.ShapeDtypeStruct((S, IN_W), BF16)], [row]
    if kv_heads:
        out_shape += [jax.ShapeDtypeStruct((N_KV_HEADS, S, HEAD_DIM), BF16)] * 2
        out_specs += [pl.BlockSpec((N_KV_HEADS, tm, HEAD_DIM), lambda i: (0, i, 0))] * 2
    res = _pc(body, name=name, out_shape=tuple(out_shape), grid=(S // tm,),
              in_specs=specs, out_specs=tuple(out_specs), compiler_params=_cp("parallel"))(*ins)
    return res if kv_heads else res[0]


def _headprep_bwd(dpr, tabs, shift, proj=None, qk_gain=None, name="headprep_bwd"):
    parts = isinstance(dpr, (tuple, list))
    S = dpr[0].shape[0] if parts else dpr.shape[0]
    tm = _row_tile(S, 512)
    norm = qk_gain is not None

    def body(*refs):
        refs = list(refs)
        d_refs = [refs.pop(0) for _ in range(4 if parts else 1)]
        c_ref, lo_ref, hi_ref = (refs.pop(0) for _ in range(3))
        if norm:
            p_ref, g_ref, e_ref, et_ref, o_ref, dg_ref = refs
        else:
            (o_ref,) = refs
        if parts:
            dq_ref, dk_ref, dv_ref, dqm_ref = d_refs
            d = jnp.concatenate([dq_ref[...]] + [dk_ref[h] for h in range(N_KV_HEADS)]
                                + [dv_ref[h] for h in range(N_KV_HEADS)] + [dqm_ref[...]], axis=1).astype(F32)
        else:
            d = d_refs[0][...].astype(F32)
        c, s_lo, s_hi, qk = _rope_cols((c_ref, lo_ref, hi_ref), tm)
        dx = d * c + pltpu.roll(d * s_lo, IN_W - shift, 1) + pltpu.roll(d * s_hi, shift, 1)
        if norm:
            x = p_ref[...]
            r = lax.rsqrt(_seg_mean(x * x, e_ref, et_ref) + EPS)
            xh = x * r

            @pl.when(pl.program_id(0) == 0)
            def _():
                dg_ref[...] = jnp.zeros_like(dg_ref)

            dg_ref[...] += jnp.sum(jnp.where(qk, dx * xh, 0.0), axis=0, keepdims=True)
            dxh = dx * g_ref[...]
            dn = r * (dxh - xh * _seg_mean(dxh * xh, e_ref, et_ref))
            dx = jnp.where(qk, dn, dx)
        o_ref[...] = dx.astype(o_ref.dtype)

    row = pl.BlockSpec((tm, IN_W), lambda i: (i, 0))
    tab = pl.BlockSpec((tm, LANES), lambda i: (i, 0))
    vec = pl.BlockSpec((1, IN_W), lambda i: (0, 0))
    if parts:
        heads = pl.BlockSpec((N_KV_HEADS, tm, HEAD_DIM), lambda i: (0, i, 0))
        ins = [*dpr, *tabs]
        specs = [pl.BlockSpec((tm, Q_W), lambda i: (i, 0)), heads, heads, pl.BlockSpec((tm, QM_W), lambda i: (i, 0)), tab, tab, tab]
    else:
        ins = [dpr, *tabs]
        specs = [row, tab, tab, tab]
    out_shape = jax.ShapeDtypeStruct((S, IN_W), BF16)
    out_specs = row
    if norm:
        e, et = _seg_mats()
        ins += [proj, qk_gain, e, et]
        specs += [row, vec, pl.BlockSpec((IN_W, LANES), lambda i: (0, 0)), pl.BlockSpec((LANES, IN_W), lambda i: (0, 0))]
        out_shape = (out_shape, jax.ShapeDtypeStruct((1, IN_W), F32))
        out_specs = (row, vec)
    return _pc(body, name=name, out_shape=out_shape, grid=(S // tm,), in_specs=specs, out_specs=out_specs,
               compiler_params=_cp("arbitrary" if norm else "parallel"))(*ins)


CHAIN_ROWS_WIDE, CHAIN_ROWS_NARROW = 128, 256
CHAIN_NARROW_KEYS = 1024


def _skewed(n, stages):
    for t in range(n + len(stages) - 1):
        for s, stage in enumerate(stages):
            if 0 <= t - s < n:
                stage(t - s)


def _chain_slices(G, bq, keys):
    cr = min(CHAIN_ROWS_NARROW if keys <= CHAIN_NARROW_KEYS else CHAIN_ROWS_WIDE, bq)
    while bq % cr:
        cr //= 2
    per = bq // cr
    return [(c // per, slice((c % per) * cr, (c % per + 1) * cr), slice(c * cr, (c + 1) * cr)) for c in range(G * per)]


def _v_ones(v):
    return jnp.concatenate([v, jnp.ones(v.shape, v.dtype)], axis=1)


def _q_dims(q, k, heads, kv_per_head=False):
    if heads is None:
        return q.shape
    return (1 if kv_per_head else k.shape[0]), heads, q.shape[0], k.shape[2]


def _q_shape(heads, NB, G, L, HD):
    return (NB, G, L, HD) if heads is None else (L, NB * G * HD)


def _q_spec(heads, G, rows, HD, index):
    if heads is None:
        return pl.BlockSpec((1, G, rows, HD), lambda *ids: (index(*ids)[0], 0, index(*ids)[1], 0))
    return pl.BlockSpec((rows, G * HD), lambda *ids: index(*ids)[::-1])


def _q_at(heads, g, hr, HD):
    return (0, g, hr, slice(None)) if heads is None else (hr, slice(g * HD, (g + 1) * HD))


def _window(L, blk, radius):
    if radius is None:
        return L, None
    W = min(L, blk + 2 * radius)
    assert blk % radius == 0 and (L - W) % radius == 0
    return W, lambda n: radius * jnp.clip(n * (blk // radius) - 1, 0, (L - W) // radius)


def _win_specs(G, W, HD, start, with_g):
    E = pl.Element
    st = (lambda n: 0) if start is None else start
    if with_g:
        return pl.BlockSpec((E(1), E(G), E(W), E(HD)), lambda b, n: (b, 0, st(n), 0))
    return pl.BlockSpec((E(1), E(W), E(HD)), lambda b, n: (b, st(n), 0))


def _attn_delta(do, o, *, dlse=None, lse=None, sink=None, heads=None, name="attn_delta"):
    HD = HEAD_DIM
    (NB, G), L = (heads, do.shape[0]) if heads is not None else (do.shape[:2], do.shape[2])
    bl = _row_tile(L, 1024)

    def body(*refs):
        refs = list(refs)
        sink_ref = refs.pop(0) if sink is not None else None
        do_ref, o_ref = refs.pop(0), refs.pop(0)
        dlse_ref = refs.pop(0) if dlse is not None else None
        lse_ref = refs.pop(0) if sink is not None else None
        delta_ref = refs.pop(0)
        b = pl.program_id(0)
        if heads is None:
            delta = jnp.sum(do_ref[0].astype(F32) * o_ref[0].astype(F32), axis=-1, keepdims=True)
        else:
            prod = do_ref[...].astype(F32) * o_ref[...].astype(F32)
            delta = jnp.concatenate([jnp.sum(prod[:, g * HD:(g + 1) * HD], axis=-1, keepdims=True)[None] for g in range(G)])
        if dlse is not None:
            delta = delta - dlse_ref[0]
        delta_ref[0] = delta
        if sink is not None:
            ds_ref = refs.pop(0)

            @pl.when(pl.program_id(1) == 0)
            def _():
                ds_ref[...] = jnp.zeros_like(ds_ref)

            for g in range(G):
                ps = jnp.exp(sink_ref[b * G + g] - lse_ref[0, g]) * delta[g]
                ds_ref[0, g] -= jnp.sum(ps)

    blk = _q_spec(None if heads is None else G, G, bl, HD, lambda b, n: (b, n))
    col = pl.BlockSpec((1, G, bl, 1), lambda b, n: (b, 0, n, 0))
    ins, specs = [do, o], [blk, blk]
    if dlse is not None:
        ins, specs = ins + [dlse], specs + [col]
    out_shape = jax.ShapeDtypeStruct((NB, G, L, 1), F32)
    out_specs = col
    if sink is not None:
        ins, specs = [sink] + ins + [lse], [pl.BlockSpec(memory_space=pltpu.SMEM)] + specs + [col]
        out_shape = (out_shape, jax.ShapeDtypeStruct((NB, G, 1, LANES), F32))
        out_specs = (col, pl.BlockSpec((1, G, 1, LANES), lambda b, n: (b, 0, 0, 0)))
    return _pc(body, name=name, out_shape=out_shape, grid=(NB, L // bl), in_specs=specs, out_specs=out_specs,
               compiler_params=_cp("parallel", "arbitrary"))(*ins)


def _attn_fwd_full(q, k, v, *, bq, tk, heads=None, name="attn_fwd_full"):
    NB, G, L, HD = _q_dims(q, k, heads)
    Lk = k.shape[1]
    nq, nk = L // bq, Lk // tk
    rows = G * bq
    chains = _chain_slices(G, bq, min(tk, CHAIN_NARROW_KEYS))

    def body(q_ref, k_ref, v_ref, o_ref, lse_ref, m_sc, acc_sc, q_sc):
        j = pl.program_id(2)

        @pl.when(j == 0)
        def _():
            m_sc[...] = jnp.full_like(m_sc, NEG)
            acc_sc[...] = jnp.zeros_like(acc_sc)
            for g, hr, sl in chains:
                q_sc[sl] = q_ref[_q_at(heads, g, hr, HD)] * ATTN_SCALE

        kk = k_ref[0]
        vv = _v_ones(v_ref[0])
        st = [dict() for _ in chains]

        def scores(c):
            st[c]["s"] = lax.dot_general(q_sc[chains[c][2]], kk, NT_DIMS, preferred_element_type=F32)

        def softmax(c):
            sl = chains[c][2]
            m_prev = m_sc[sl]
            m_new = jnp.maximum(m_prev, jnp.max(st[c]["s"], axis=1, keepdims=True))
            st[c]["p"] = jnp.exp(st[c].pop("s") - m_new).astype(BF16)
            st[c]["alpha"] = jnp.exp(m_prev - m_new)
            m_sc[sl] = m_new

        def values(c):
            sl = chains[c][2]
            acc_sc[sl] = st[c].pop("alpha") * acc_sc[sl] + jnp.dot(st[c].pop("p"), vv, preferred_element_type=F32)

        _skewed(len(chains), (scores, softmax, values))

        @pl.when(j == nk - 1)
        def _():
            for g, hr, sl in chains:
                acc = acc_sc[sl]
                l = acc[:, HD:HD + 1]
                o_ref[_q_at(heads, g, hr, HD)] = (acc[:, :HD] / l).astype(o_ref.dtype)
                lse_ref[0, g, hr, :] = m_sc[sl] + jnp.log(l)

    qspec = _q_spec(heads, G, bq, HD, lambda b, n, j: (b, n))
    kspec = pl.BlockSpec((1, tk, HD), lambda b, n, j: (b, j, 0))
    return _pc(body, name=name,
               out_shape=(jax.ShapeDtypeStruct(_q_shape(heads, NB, G, L, HD), BF16), jax.ShapeDtypeStruct((NB, G, L, 1), F32)),
               grid=(NB, nq, nk), in_specs=[qspec, kspec, kspec],
               out_specs=(qspec, pl.BlockSpec((1, G, bq, 1), lambda b, n, j: (b, 0, n, 0))),
               scratch_shapes=[pltpu.VMEM((rows, 1), F32), pltpu.VMEM((rows, 2 * HD), F32), pltpu.VMEM((rows, HD), BF16)],
               compiler_params=_cp("parallel", "parallel", "arbitrary"))(q, k, v)


def _attn_bwd_full(q, k, v, do, lse, o, *, bq, tk, heads=None, kv_per_head=False, name="attn_bwd_full"):
    NB, G, L, HD = _q_dims(q, k, heads, kv_per_head)
    Lk = k.shape[1]
    nq, nk = L // bq, Lk // tk
    chains = _chain_slices(G, bq, tk)
    KH = G if kv_per_head else 1
    assert not kv_per_head or nk == 1

    def body(q_ref, k_ref, v_ref, do_ref, lse_ref, o_ref, dqp_ref, dk_ref, dv_ref, dk_sc, dv_sc):
        n = pl.program_id(2)

        @pl.when(n == 0)
        def _():
            dk_sc[...] = jnp.zeros_like(dk_sc)
            dv_sc[...] = jnp.zeros_like(dv_sc)

        kks = [k_ref[h] for h in range(KH)]
        vvs = [v_ref[h] for h in range(KH)]
        st = [dict() for _ in chains]

        def scores(c):
            g, hr, _ = chains[c]
            at = _q_at(heads, g, hr, HD)
            st[c]["q"] = q_ref[at] * ATTN_SCALE
            st[c]["do"] = do_ref[at]
            st[c]["s"] = lax.dot_general(st[c]["q"], kks[g % KH], NT_DIMS, preferred_element_type=F32)
            st[c]["dp"] = lax.dot_general(st[c]["do"], vvs[g % KH], NT_DIMS, preferred_element_type=F32)
            st[c]["delta"] = jnp.sum(st[c]["do"].astype(F32) * o_ref[at].astype(F32), axis=-1, keepdims=True)

        def softmax(c):
            g, hr, _ = chains[c]
            p = jnp.exp(st[c].pop("s") - lse_ref[0, g, hr, :])
            st[c]["ds"] = (p * (st[c].pop("dp") - st[c].pop("delta"))).astype(BF16)
            st[c]["p"] = p.astype(BF16)

        def grads(c):
            g, hr, _ = chains[c]
            ds = st[c].pop("ds")
            dv_sc[g % KH] += lax.dot_general(st[c].pop("p"), st[c].pop("do"), TN_DIMS, preferred_element_type=F32)
            dk_sc[g % KH] += lax.dot_general(ds, st[c].pop("q"), TN_DIMS, preferred_element_type=F32)
            dqp_ref[(0,) + _q_at(heads, g, hr, HD)] = jnp.dot(ds, kks[g % KH], preferred_element_type=F32) * ATTN_SCALE

        _skewed(len(chains), (scores, softmax, grads))

        @pl.when(n == nq - 1)
        def _():
            dk_ref[...] = dk_sc[...].astype(dk_ref.dtype)
            dv_ref[...] = dv_sc[...].astype(dv_ref.dtype)

    qspec = _q_spec(heads, G, bq, HD, lambda b, m, n: (b, n))
    cspec = pl.BlockSpec((1, G, bq, 1), lambda b, m, n: (b, 0, n, 0))
    kspec = pl.BlockSpec((KH, tk, HD), lambda b, m, n: (b, m, 0))
    kv_shape = jax.ShapeDtypeStruct((NB * KH, Lk, HD), BF16)
    if heads is None:
        pspec = pl.BlockSpec((1, 1, G, bq, HD), lambda b, m, n: (m, b, 0, n, 0))
    else:
        pspec = pl.BlockSpec((1, bq, G * HD), lambda b, m, n: (m, n, b))
    dqp, dk, dv = _pc(body, name=name,
                      out_shape=(jax.ShapeDtypeStruct((nk,) + _q_shape(heads, NB, G, L, HD), F32), kv_shape, kv_shape),
                      grid=(NB, nk, nq), in_specs=[qspec, kspec, kspec, qspec, cspec, qspec],
                      out_specs=(pspec, kspec, kspec),
                      scratch_shapes=[pltpu.VMEM((KH, tk, HD), F32), pltpu.VMEM((KH, tk, HD), F32)],
                      compiler_params=_cp("parallel", "parallel", "arbitrary"))(q, k, v, do, lse, o)
    if nk == 1:
        return dqp[0].astype(BF16), dk, dv
    bl = _row_tile(L, 512)

    def sum_body(p_ref, o_ref):
        acc = p_ref[0]
        for j in range(1, nk):
            acc = acc + p_ref[j]
        o_ref[...] = acc.astype(o_ref.dtype)

    if heads is None:
        pspec = pl.BlockSpec((nk, 1, G, bl, HD), lambda b, n: (0, b, 0, n, 0))
    else:
        pspec = pl.BlockSpec((nk, bl, G * HD), lambda b, n: (0, n, b))
    dq = _pc(sum_body, name=name + "_sum", out_shape=jax.ShapeDtypeStruct(_q_shape(heads, NB, G, L, HD), BF16),
             grid=(NB, L // bl), in_specs=[pspec], out_specs=_q_spec(heads, G, bl, HD, lambda b, n: (b, n)),
             compiler_params=_cp("parallel", "parallel"))(dqp)
    return dq, dk, dv


def _attn_fwd_win(q, k, v, *, radius, sink=None, bq, heads=None, kv_per_head=False, carry=None, name="attn_fwd_win"):
    NB, G, L, HD = _q_dims(q, k, heads, kv_per_head)
    W, start = _window(k.shape[1], bq, radius)
    chains = _chain_slices(G, bq, W)

    def body(*refs):
        if sink is not None:
            sink_ref, *refs = refs
        q_ref, k_ref, v_ref, o_ref, lse_ref = refs
        b, n = pl.program_id(0), pl.program_id(1)
        kks = [k_ref[g] for g in range(G)] if kv_per_head else [k_ref[0]] * G
        vvs = [_v_ones(v_ref[g]) for g in range(G)] if kv_per_head else [_v_ones(v_ref[0])] * G
        st = [dict() for _ in chains]

        def scores(c):
            g, hr, _ = chains[c]
            s = lax.dot_general(q_ref[_q_at(heads, g, hr, HD)] * ATTN_SCALE, kks[g], NT_DIMS, preferred_element_type=F32)
            if radius is not None:
                qpos = n * bq + hr.start + lax.broadcasted_iota(jnp.int32, (hr.stop - hr.start, 1), 0)
                kpos = start(n) + lax.broadcasted_iota(jnp.int32, (1, W), 1)
                s = jnp.where(jnp.abs(qpos - kpos) <= radius, s, NEG)
            st[c]["s"] = s

        def softmax(c):
            g = chains[c][0]
            m = jnp.max(st[c]["s"], axis=1, keepdims=True)
            if sink is not None:
                m = jnp.maximum(m, sink_ref[b * G + g])
            st[c]["p"] = jnp.exp(st[c].pop("s") - m).astype(BF16)
            st[c]["m"] = m

        def values(c):
            g, hr, _ = chains[c]
            acc = jnp.dot(st[c].pop("p"), vvs[g], preferred_element_type=F32)
            m = st[c].pop("m")
            l = acc[:, HD:HD + 1]
            if sink is not None:
                l = l + jnp.exp(sink_ref[b * G + g] - m)
            o_ref[_q_at(heads, g, hr, HD)] = (acc[:, :HD] / l).astype(o_ref.dtype)
            lse_ref[0, g, hr, :] = m + jnp.log(l)

        _skewed(len(chains), (scores, softmax, values))

    qspec = _q_spec(heads, G, bq, HD, lambda b, n: (b, n))
    kspec = pl.BlockSpec((G, W, HD), lambda b, n: (0, 0, 0)) if kv_per_head else _win_specs(G, W, HD, start, False)
    ins, specs = [q, k, v], [qspec, kspec, kspec]
    if sink is not None:
        ins, specs = [sink] + ins, [pl.BlockSpec(memory_space=pltpu.SMEM)] + specs
    return _call(body, carry, ins, name=name,
                 out_shape=(jax.ShapeDtypeStruct(_q_shape(heads, NB, G, L, HD), BF16), jax.ShapeDtypeStruct((NB, G, L, 1), F32)),
                 grid=(NB, L // bq), in_specs=specs,
                 out_specs=(qspec, pl.BlockSpec((1, G, bq, 1), lambda b, n: (b, 0, n, 0))), sem=("parallel", "parallel"))


def _attn_dq_win(q, k, v, do, lse, delta, *, radius, bq, heads=None, name="attn_dq_win"):
    NB, G, L, HD = _q_dims(q, k, heads)
    W, start = _window(L, bq, radius)
    chains = _chain_slices(G, bq, W)

    def body(q_ref, k_ref, v_ref, do_ref, lse_ref, dl_ref, dq_ref):
        n = pl.program_id(1)
        kk, vv = k_ref[0], v_ref[0]
        kpos = start(n) + lax.broadcasted_iota(jnp.int32, (1, W), 1)
        st = [dict() for _ in chains]

        def scores(c):
            g, hr, _ = chains[c]
            at = _q_at(heads, g, hr, HD)
            st[c]["s"] = lax.dot_general(q_ref[at] * ATTN_SCALE, kk, NT_DIMS, preferred_element_type=F32)
            st[c]["dp"] = lax.dot_general(do_ref[at], vv, NT_DIMS, preferred_element_type=F32)

        def softmax(c):
            g, hr, _ = chains[c]
            qpos = n * bq + hr.start + lax.broadcasted_iota(jnp.int32, (hr.stop - hr.start, 1), 0)
            p = jnp.where(jnp.abs(qpos - kpos) <= radius, jnp.exp(st[c].pop("s") - lse_ref[0, g, hr, :]), 0.0)
            st[c]["ds"] = (p * (st[c].pop("dp") - dl_ref[0, g, hr, :])).astype(BF16)

        def grads(c):
            g, hr, _ = chains[c]
            dq = jnp.dot(st[c].pop("ds"), kk, preferred_element_type=F32) * ATTN_SCALE
            dq_ref[_q_at(heads, g, hr, HD)] = dq.astype(dq_ref.dtype)

        _skewed(len(chains), (scores, softmax, grads))

    qspec = _q_spec(heads, G, bq, HD, lambda b, n: (b, n))
    cspec = pl.BlockSpec((1, G, bq, 1), lambda b, n: (b, 0, n, 0))
    kspec = _win_specs(G, W, HD, start, False)
    return _pc(body, name=name, out_shape=jax.ShapeDtypeStruct(_q_shape(heads, NB, G, L, HD), BF16), grid=(NB, L // bq),
               in_specs=[qspec, kspec, kspec, qspec, cspec, cspec], out_specs=qspec,
               compiler_params=_cp("parallel", "parallel"))(q, k, v, do, lse, delta)


def _attn_dkv_win(q, k, v, do, lse, delta, *, radius, bk, heads=None, carry=None, name="attn_dkv_win"):
    NB, G, L, HD = _q_dims(q, k, heads)
    W, start = _window(L, bk, radius)
    chains = _chain_slices(G, W, bk)

    def body(q_ref, k_ref, v_ref, do_ref, lse_ref, dl_ref, dk_ref, dv_ref):
        m = pl.program_id(1)
        kk, vv = k_ref[0], v_ref[0]
        kpos = m * bk + lax.broadcasted_iota(jnp.int32, (1, bk), 1)
        st = [dict() for _ in chains]
        out = dict(dk=jnp.zeros((bk, HD), F32), dv=jnp.zeros((bk, HD), F32))

        def scores(c):
            g, hr, _ = chains[c]
            at = _q_at(heads, g, hr, HD)
            st[c]["q"] = q_ref[at] * ATTN_SCALE
            st[c]["do"] = do_ref[at]
            st[c]["s"] = lax.dot_general(st[c]["q"], kk, NT_DIMS, preferred_element_type=F32)
            st[c]["dp"] = lax.dot_general(st[c]["do"], vv, NT_DIMS, preferred_element_type=F32)

        def softmax(c):
            g, hr, _ = chains[c]
            qpos = start(m) + hr.start + lax.broadcasted_iota(jnp.int32, (hr.stop - hr.start, 1), 0)
            p = jnp.where(jnp.abs(qpos - kpos) <= radius, jnp.exp(st[c].pop("s") - lse_ref[0, g, hr, :]), 0.0)
            st[c]["ds"] = (p * (st[c].pop("dp") - dl_ref[0, g, hr, :])).astype(BF16)
            st[c]["p"] = p.astype(BF16)

        def grads(c):
            out["dv"] = out["dv"] + lax.dot_general(st[c].pop("p"), st[c].pop("do"), TN_DIMS, preferred_element_type=F32)
            out["dk"] = out["dk"] + lax.dot_general(st[c].pop("ds"), st[c].pop("q"), TN_DIMS, preferred_element_type=F32)

        _skewed(len(chains), (scores, softmax, grads))
        dk_ref[0] = out["dk"].astype(dk_ref.dtype)
        dv_ref[0] = out["dv"].astype(dv_ref.dtype)

    if heads is None:
        qspec = _win_specs(G, W, HD, start, True)
    else:
        qspec = pl.BlockSpec((pl.Element(W), pl.Element(G * HD)), lambda b, m: (start(m), b * G * HD))
    cspec = _win_specs(G, W, 1, start, True)
    kspec = pl.BlockSpec((1, bk, HD), lambda b, m: (b, m, 0))
    kv_shape = jax.ShapeDtypeStruct((NB, L, HD), BF16)
    return _call(body, carry, [q, k, v, do, lse, delta], name=name, out_shape=(kv_shape, kv_shape), grid=(NB, L // bk),
                 in_specs=[qspec, kspec, kspec, qspec, cspec, cspec], out_specs=(kspec, kspec), sem=("parallel", "parallel"))


def _attn(q, k, v, *, radius=None, sink=None, bq, tk=None, heads=None, kv_per_head=False, carry=None, tag):
    if radius is None and tk < k.shape[1]:
        return _attn_fwd_full(q, k, v, bq=bq, tk=tk, heads=heads, name=f"attn_fwd_{tag}")
    return _attn_fwd_win(q, k, v, radius=radius, sink=sink, bq=bq, heads=heads, kv_per_head=kv_per_head, carry=carry,
                         name=f"attn_fwd_{tag}")


def _attn_bwd(q, k, v, o, lse, do, *, radius=None, sink=None, dlse=None, bq, tk=None, heads=None, kv_per_head=False,
              carry=None, tag):
    if radius is None:
        assert carry is None
        return (*_attn_bwd_full(q, k, v, do, lse, o, bq=bq, tk=tk, heads=heads, kv_per_head=kv_per_head,
                                name=f"attn_bwd_{tag}"), None, None)
    nbg = None if heads is None else (k.shape[0], heads)
    if sink is not None:
        delta, ds = _attn_delta(do, o, lse=lse, sink=sink, heads=nbg, name=f"attn_delta_{tag}")
        dsink = ds[:, :, 0, 0].reshape(-1)
    else:
        delta, dsink = _attn_delta(do, o, dlse=dlse, heads=nbg, name=f"attn_delta_{tag}"), None
    dq = _attn_dq_win(q, k, v, do, lse, delta, radius=radius, bq=bq, heads=heads, name=f"attn_dq_{tag}")
    dk, dv, *arrived = _attn_dkv_win(q, k, v, do, lse, delta, radius=radius, bk=bq, heads=heads, carry=carry,
                                     name=f"attn_dkv_{tag}")
    return dq, dk, dv, dsink, (arrived[0] if arrived else None)


def _combine_fwd(o, lse, name="combine_fwd"):
    H, S, HD = o.shape
    tm = _row_tile(S, 512)

    def body(o_ref, lse_ref, t_ref):
        for g in range(GQA_GROUP):
            hs = [kv * GQA_GROUP + g for kv in range(N_KV_HEADS)]
            ls = [lse_ref[h] for h in hs]
            mx = functools.reduce(jnp.maximum, ls)
            es = [jnp.exp(l - mx) for l in ls]
            den = functools.reduce(jnp.add, es)
            for h, e in zip(hs, es):
                t_ref[h] = (o_ref[h].astype(F32) * (e / den)).astype(t_ref.dtype)

    blk = pl.BlockSpec((H, tm, HD), lambda i: (0, i, 0))
    col = pl.BlockSpec((H, tm, 1), lambda i: (0, i, 0))
    return _pc(body, name=name, out_shape=jax.ShapeDtypeStruct((H, S, HD), BF16), grid=(S // tm,),
               in_specs=[blk, col], out_specs=blk, compiler_params=_cp("parallel"))(o, lse)


def _combine_bwd(dt, o, lse, name="combine_bwd"):
    H, S, HD = o.shape
    tm = _row_tile(S, 512)

    def body(dt_ref, o_ref, lse_ref, do_ref, dlse_ref):
        for g in range(GQA_GROUP):
            hs = [kv * GQA_GROUP + g for kv in range(N_KV_HEADS)]
            ls = [lse_ref[h] for h in hs]
            mx = functools.reduce(jnp.maximum, ls)
            es = [jnp.exp(l - mx) for l in ls]
            den = functools.reduce(jnp.add, es)
            al = [e / den for e in es]
            dts = [dt_ref[h].astype(F32) for h in hs]
            da = [jnp.sum(d * o_ref[h].astype(F32), axis=-1, keepdims=True) for h, d in zip(hs, dts)]
            dot = functools.reduce(jnp.add, [a * d for a, d in zip(al, da)])
            for h, a, d, dd in zip(hs, al, da, dts):
                do_ref[h] = (dd * a).astype(do_ref.dtype)
                dlse_ref[h] = a * (d - dot)

    blk = pl.BlockSpec((H, tm, HD), lambda i: (0, i, 0))
    col = pl.BlockSpec((H, tm, 1), lambda i: (0, i, 0))
    return _pc(body, name=name,
               out_shape=(jax.ShapeDtypeStruct((H, S, HD), BF16), jax.ShapeDtypeStruct((H, S, 1), F32)),
               grid=(S // tm,), in_specs=[blk, blk, col], out_specs=(blk, col), compiler_params=_cp("parallel"))(dt, o, lse)


def _position():
    x, y, c = lax.axis_index("x"), lax.axis_index("y"), lax.axis_index("c")
    return x, y, c


def _peer(pos, k):
    x, y, c = pos
    return (1 - x if k & 4 else x, 1 - y if k & 2 else y, 1 - c if k & 1 else c)


def _linear(p):
    return 4 * p[0] + 2 * p[1] + p[2]


def _exchange_steps(s_ref, r_ref, send_sems, recv_sems, local_sem, gather):
    pos = _position()
    me = _linear(pos)
    own = pltpu.make_async_copy(s_ref if gather else s_ref.at[me], r_ref.at[me], local_sem)
    peers = range(1, N_DEV)

    def sems(k):
        return dict(send_sem=send_sems.at[k - 1], recv_sem=recv_sems.at[k - 1], device_id=_peer(pos, k), device_id_type=MESH)

    def send(k):
        src = s_ref if gather else s_ref.at[_linear(_peer(pos, k))]
        return pltpu.make_async_remote_copy(src_ref=src, dst_ref=r_ref.at[me], **sems(k))

    def arrival(k):
        slot = r_ref.at[_linear(_peer(pos, k))]
        return pltpu.make_async_remote_copy(src_ref=slot, dst_ref=slot, **sems(k))

    def start():
        own.start()
        for k in peers:
            send(k).start()

    def wait():
        for k in peers:
            arrival(k).wait_recv()
        for k in peers:
            send(k).wait_send()
        own.wait()

    return start, wait


EXCHANGE_SEMS = [pltpu.SemaphoreType.DMA((N_DEV - 1,)), pltpu.SemaphoreType.DMA((N_DEV - 1,)), pltpu.SemaphoreType.DMA]


def _exchange(buf, gather, name):
    def body(s_ref, r_ref, *sems):
        start, wait = _exchange_steps(s_ref, r_ref, *sems, gather)
        start()
        wait()

    hbm = pl.BlockSpec(memory_space=pltpu.HBM)
    out_shape = ((N_DEV,) + buf.shape) if gather else buf.shape
    return _pc(body, name=name, out_shape=jax.ShapeDtypeStruct(out_shape, buf.dtype), in_specs=[hbm], out_specs=hbm,
               scratch_shapes=list(EXCHANGE_SEMS))(buf)


def _call(body, carry, ins, *, name, out_shape, grid, in_specs, out_specs, scratch_shapes=(), sem):
    if carry is None:
        return _pc(body, name=name, out_shape=tuple(out_shape), grid=grid, in_specs=list(in_specs),
                   out_specs=tuple(out_specs), scratch_shapes=list(scratch_shapes), compiler_params=_cp(*sem))(*ins)
    buf, gather = carry
    n_in, n_out, n_sc = len(ins), len(out_shape), len(scratch_shapes)

    def wrapped(*refs):
        in_refs, buf_ref = refs[:n_in], refs[n_in]
        out_refs, recv_ref = refs[n_in + 1:n_in + 1 + n_out], refs[n_in + 1 + n_out]
        rest = refs[n_in + 2 + n_out:]
        first = functools.reduce(jnp.logical_and, [pl.program_id(a) == 0 for a in range(len(grid))])
        last = functools.reduce(jnp.logical_and, [pl.program_id(a) == grid[a] - 1 for a in range(len(grid))])

        @pl.when(first)
        def _():
            _exchange_steps(buf_ref, recv_ref, *rest[n_sc:], gather)[0]()

        body(*in_refs, *out_refs, *rest[:n_sc])

        @pl.when(last)
        def _():
            _exchange_steps(buf_ref, recv_ref, *rest[n_sc:], gather)[1]()

    hbm = pl.BlockSpec(memory_space=pltpu.HBM)
    recv_shape = ((N_DEV,) + buf.shape) if gather else buf.shape
    return _pc(wrapped, name=name, out_shape=(*out_shape, jax.ShapeDtypeStruct(recv_shape, buf.dtype)), grid=grid,
               in_specs=[*in_specs, hbm], out_specs=(*out_specs, hbm), scratch_shapes=[*scratch_shapes, *EXCHANGE_SEMS],
               compiler_params=_cp(*(("arbitrary",) * len(grid))))(*ins, buf)


def _reduce_adamw(recv, w, m, v, name):
    _, R, C = recv.shape
    tr = _row_tile(R, 512)

    def body(r_ref, w_ref, m_ref, v_ref, g_ref, d_ref, nm_ref, nv_ref):
        g = r_ref[0].astype(F32)
        for j in range(1, N_DEV):
            g = g + r_ref[j].astype(F32)
        g_ref[...] = g
        nm = ADAM_B1 * m_ref[...] + (1.0 - ADAM_B1) * g
        nv = ADAM_B2 * v_ref[...] + (1.0 - ADAM_B2) * jnp.square(g)
        m_hat = nm / (1.0 - ADAM_B1 ** ADAM_STEP)
        v_hat = nv / (1.0 - ADAM_B2 ** ADAM_STEP)
        d_ref[...] = -ADAM_LR * (m_hat / (jnp.sqrt(v_hat) + ADAM_EPS) + ADAM_WD * w_ref[...])
        nm_ref[...] = nm
        nv_ref[...] = nv

    row = pl.BlockSpec((tr, C), lambda i: (i, 0))
    out = jax.ShapeDtypeStruct((R, C), F32)
    return _pc(body, name=name, out_shape=(out, out, out, out), grid=(R // tr,),
               in_specs=[pl.BlockSpec((N_DEV, tr, C), lambda i: (0, i, 0)), row, row, row],
               out_specs=(row, row, row, row), compiler_params=_cp("parallel"))(recv, w, m, v)


BIG = (("w_in", 2), ("w_mem_kv", 1), ("w_o", 1), ("w_gate_up", 2), ("w_down", 1))
SMALL = ("mem_norm_g", "g_mix_pre", "g_mix_post", "attn_sink", "qk_norm_g", "g_ffn_pre", "g_ffn_post")
SMALL_W = 1024
FIRST, REST = BIG[:1], BIG[1:]


def _pack_local(shards, dtype):
    return jnp.concatenate([s.astype(dtype).reshape(-1, LANES) for s in shards], axis=0)


def _unpack_local(flat, shapes):
    out, r = [], 0
    for shp in shapes:
        n = shp[0] * shp[1] * shp[2] // LANES
        out.append(flat[r:r + n].reshape(shp))
        r += n
    return out


def _unpack_gathered(g, shapes, names=BIG):
    out, r = [], 0
    for (name, dim), shp in zip(names, shapes):
        n = shp[0] * shp[1] * shp[2] // LANES
        t = g[:, r:r + n].reshape((N_DEV,) + tuple(shp))
        if dim == 2:
            t = t.transpose(1, 2, 0, 3).reshape(shp[0], shp[1], N_DEV * shp[2])
        else:
            t = t.transpose(1, 0, 2, 3).reshape(shp[0], N_DEV * shp[1], shp[2])
        out.append(t)
        r += n
    return out


def _pack_for_scatter(full, shapes, dtype, names=BIG):
    parts = []
    for (name, dim), shp, t in zip(names, shapes, full):
        if dim == 2:
            t = t.reshape(shp[0], shp[1], N_DEV, shp[2]).transpose(2, 0, 1, 3)
        else:
            t = t.reshape(shp[0], N_DEV, shp[1], shp[2]).transpose(1, 0, 2, 3)
        parts.append(t.astype(dtype).reshape(N_DEV, -1, LANES))
    return jnp.concatenate(parts, axis=1)


def _pack_small(arrs):
    flat = jnp.concatenate([a.reshape(-1) for a in arrs])
    pad = (-flat.shape[0]) % (8 * SMALL_W)
    return jnp.pad(flat, (0, pad)).reshape(-1, SMALL_W)


def _unpack_small(flat, shapes):
    flat = flat.reshape(-1)
    out, r = [], 0
    for shp in shapes:
        n = 1
        for d in shp:
            n *= d
        out.append(flat[r:r + n].reshape(shp))
        r += n
    return out


def _heads(t, nb, g):
    S = t.shape[0]
    return t.reshape(S, nb, g, HEAD_DIM).transpose(1, 2, 0, 3)


def _unheads(t):
    nb, g, S, hd = t.shape
    return t.transpose(2, 0, 1, 3).reshape(S, nb * g * hd)


def _dilate(t, dil):
    S = t.shape[0]
    g = t.shape[1] // HEAD_DIM
    return t.reshape(S // dil, dil, g, HEAD_DIM).transpose(1, 2, 0, 3)


def _undilate(t):
    dil, g, L, w = t.shape
    return t.transpose(1, 2, 0, 3).reshape(g, L * dil, w)


FULL_BQ_FWD, FULL_TK_FWD = 512, 8192
FULL_BQ_BWD, FULL_TK_BWD = 1024, 2048


def _mixer_fwd(kind, pr, kv, sink, li, carry=None):
    S = pr.shape[0]
    if kind == 0:
        tok, lse, *arrived = _attn(pr, *kv, radius=A_RADIUS, sink=sink, bq=min(256, S), heads=GQA_GROUP, carry=carry,
                                   tag=f"a{li}")
        return tok, lse, (arrived[0] if arrived else None)
    assert carry is None
    if kind == 1:
        tok, lse = _attn(pr, *kv, bq=min(FULL_BQ_FWD, S), tk=min(FULL_TK_FWD, S), heads=GQA_GROUP, tag=f"b{li}")
        return tok, lse, None
    saved, outs, lses = [], [], []
    for g, (window, dil) in enumerate(C_GROUPS):
        q = _dilate(pr[:, g * GQA_GROUP * HEAD_DIM:(g + 1) * GQA_GROUP * HEAD_DIM], dil)
        k = _dilate(pr[:, Q_W + g * HEAD_DIM:Q_W + (g + 1) * HEAD_DIM], dil)[:, 0]
        v = _dilate(pr[:, QK_W + g * HEAD_DIM:QK_W + (g + 1) * HEAD_DIM], dil)[:, 0]
        o, lse = _attn(q, k, v, radius=window // (2 * dil), bq=min(256, S // dil), tag=f"c{li}g{g}")
        saved.append((q, k, v, o, lse))
        outs.append(_undilate(o))
        lses.append(_undilate(lse))
    o_all, lse_all = jnp.concatenate(outs, 0), jnp.concatenate(lses, 0)
    tok = _combine_fwd(o_all, lse_all, name=f"combine_fwd_{li}")
    return tok.transpose(1, 0, 2).reshape(S, Q_W), (saved, o_all, lse_all), None


def _mixer_bwd(kind, dcat, pr, kv, cat, saved, sink, li, carry=None):
    S = dcat.shape[0]
    if kind == 0:
        return _attn_bwd(pr, *kv, cat, saved, dcat, radius=A_RADIUS, sink=sink, bq=min(256, S), heads=GQA_GROUP, carry=carry,
                         tag=f"a{li}")
    assert carry is None
    if kind == 1:
        return _attn_bwd(pr, *kv, cat, saved, dcat, bq=min(FULL_BQ_BWD, S), tk=min(FULL_TK_BWD, S), heads=GQA_GROUP,
                         tag=f"b{li}")
    per_group, o_all, lse_all = saved
    dt = dcat[:, :Q_W].reshape(S, N_TOK_HEADS, HEAD_DIM).transpose(1, 0, 2)
    do_all, dlse_all = _combine_bwd(dt, o_all, lse_all, name=f"combine_bwd_{li}")
    dqs, dks, dvs = [], [], []
    for g, (window, dil) in enumerate(C_GROUPS):
        q, k, v, o, lse = per_group[g]
        L = S // dil
        hs = slice(g * GQA_GROUP, (g + 1) * GQA_GROUP)
        do = do_all[hs].reshape(GQA_GROUP, L, dil, HEAD_DIM).transpose(2, 0, 1, 3)
        dlse = dlse_all[hs].reshape(GQA_GROUP, L, dil, 1).transpose(2, 0, 1, 3)
        dq, dk, dv, _, _ = _attn_bwd(q, k, v, o, lse, do, radius=window // (2 * dil), dlse=dlse, bq=min(256, L),
                                     tag=f"c{li}g{g}")
        dqs.append(dq.transpose(2, 0, 1, 3).reshape(S, GQA_GROUP * HEAD_DIM))
        dks.append(dk.transpose(1, 0, 2).reshape(S, HEAD_DIM))
        dvs.append(dv.transpose(1, 0, 2).reshape(S, HEAD_DIM))
    return jnp.concatenate(dqs, 1), jnp.concatenate(dks, 1), jnp.concatenate(dvs, 1), None, None


def kernel(x, mem, mem_norm_g, w_in, w_mem_kv, w_o, g_mix_pre, g_mix_post, attn_sink, qk_norm_g, w_gate_up, w_down, g_ffn_pre, g_ffn_post, loss_target, m_mem_norm_g, m_w_in, m_w_mem_kv, m_w_o, m_g_mix_pre, m_g_mix_post, m_attn_sink, m_qk_norm_g, m_w_gate_up, m_w_down, m_g_ffn_pre, m_g_ffn_post, v_mem_norm_g, v_w_in, v_w_mem_kv, v_w_o, v_g_mix_pre, v_g_mix_post, v_attn_sink, v_qk_norm_g, v_w_gate_up, v_w_down, v_g_ffn_pre, v_g_ffn_post):
    given = dict(locals())
    depth = w_in.shape[0]
    S, D = x.shape[1], x.shape[2]
    def shapes_of(names):
        return [(1,) + tuple(given[n].shape[1:]) for n, _ in names]

    def layer_pack(pre, l, dtype, names=BIG):
        return _pack_local([given[pre + n][l:l + 1] for n, _ in names], dtype)

    def layer_weights(gathered, names=BIG):
        return [t[0] for t in _unpack_gathered(gathered, shapes_of(names), names)]

    W = [None] * depth
    W[0] = layer_weights(_exchange(layer_pack("", 0, BF16, FIRST), True, "gather_w0_in"), FIRST)

    tabs = _rope_tables(S)
    mem_n = _rms_fwd(mem[0], mem_norm_g[None], BF16, name="rms_mem")

    saved = []
    xc = x[0]
    for i in range(depth):
        kind = i % N_MIXERS
        (tab, shift) = tabs[1] if kind == 1 else tabs[0]
        sink = attn_sink[i // N_MIXERS] if kind == 0 else None
        qk_gain = _qk_gain_row(qk_norm_g[i // N_MIXERS]) if kind == 1 else None
        carry = (layer_pack("", 0, BF16, REST), True) if i == 0 else None
        h, proj, *arrived = _mm(xc, W[i][0], F32, pre_g=g_mix_pre[i][None], carry=carry, name=f"mm_in_{i}")
        if carry is not None:
            W[0] = W[0] + layer_weights(arrived[0], REST)
        W_in, W_mkv, W_o, W_gu, W_dn = W[i]
        if kind == 2:
            pr, kv = _headprep_fwd(proj, tab, shift, qk_gain, name=f"headprep_fwd_{i}"), None
        else:
            pr, *kv = _headprep_fwd(proj, tab, shift, qk_gain, kv_heads=True, name=f"headprep_fwd_{i}")
        carry = (layer_pack("", 1, BF16), True) if i == 0 and depth > 1 else None
        tok, mix_saved, arrived = _mixer_fwd(kind, pr, kv, sink, i, carry)
        if carry is not None:
            W[1] = layer_weights(arrived)
        (mkv,) = _mm(mem_n, W_mkv, BF16, name=f"mm_mkv_{i}")
        qm = pr[:, QK_W + KV_W:]
        km = _heads(mkv[:, :QM_W], N_MEM_HEADS, 1)[:, 0]
        vm = _heads(mkv[:, QM_W:], N_MEM_HEADS, 1)[:, 0]
        mo, mlse = _attn(qm, km, vm, bq=min(256, S), tk=km.shape[1], heads=N_MEM_HEADS, kv_per_head=True, tag=f"m{i}")
        cat = jnp.concatenate([tok, mo], axis=1)
        o, x1 = _mm(cat, W_o, BF16, post=(g_mix_post[i][None], xc), name=f"mm_o_{i}")
        carry = (layer_pack("", i + 2, BF16), True) if i + 2 < depth else None
        h2, gu, act, *arrived = _mm(x1, W_gu, BF16, pre_g=g_ffn_pre[i][None], swiglu=True, carry=carry, name=f"mm_gu_{i}")
        if carry is not None:
            W[i + 2] = layer_weights(arrived[0])
        f, x2 = _mm(act, W_dn, BF16, post=(g_ffn_post[i][None], x1), name=f"mm_dn_{i}")
        saved.append(dict(x=xc, h=h, proj=proj, pr=pr, kv=kv, mix=mix_saved, qm=qm, km=km, vm=vm, mo=mo, mlse=mlse, cat=cat, o=o,
                          x1=x1, h2=h2, gu=gu, act=act, f=f))
        xc = x2

    dy, sq = _loss_head(xc, loss_target[0], name="loss_head")
    loss = lax.psum(sq[0, 0] * (0.5 / D), ("x", "y", "c"))

    grads = {n: [None] * depth for n in ("w_in", "w_mem_kv", "w_o", "w_gate_up", "w_down", "g_mix_pre", "g_mix_post",
                                         "g_ffn_pre", "g_ffn_post")}
    d_sink = [jnp.zeros((N_TOK_HEADS,), F32) for _ in range(attn_sink.shape[0])]
    d_qkg = [jnp.zeros((2, HEAD_DIM), F32) for _ in range(qk_norm_g.shape[0])]
    dmem_n = jnp.zeros((mem.shape[1], D), F32)
    recv = [None] * depth

    def scatter_pack(l, names=BIG):
        return _pack_for_scatter([grads[n][l][None] for n, _ in names], shapes_of(names), BF16, names)

    dx = dy
    for i in reversed(range(depth)):
        kind = i % N_MIXERS
        sv = saved[i]
        (tab, shift) = tabs[1] if kind == 1 else tabs[0]
        sink = attn_sink[i // N_MIXERS] if kind == 0 else None
        W_in, W_mkv, W_o, W_gu, W_dn = W[i]
        df, dgu, dg = _mm(dx, W_dn, BF16, nt=True, gu=sv["gu"], pre_bwd=(sv["f"], g_ffn_post[i][None]), name=f"mmb_dn_{i}")
        grads["g_ffn_post"][i] = dg[0]
        grads["w_down"][i] = _mm_tn(sv["act"], df, name=f"mmw_dn_{i}")
        dx1, dg = _mm(dgu, W_gu, F32, nt=True, post_bwd=(sv["x1"], g_ffn_pre[i][None], dx), name=f"mmb_gu_{i}")
        grads["g_ffn_pre"][i] = dg[0]
        if i + 1 < depth:
            grads["w_gate_up"][i], recv[i + 1] = _mm_tn(sv["h2"], dgu, carry=(scatter_pack(i + 1), False), name=f"mmw_gu_{i}")
        else:
            grads["w_gate_up"][i] = _mm_tn(sv["h2"], dgu, name=f"mmw_gu_{i}")
        do, dcat, dg = _mm(dx1, W_o, BF16, nt=True, pre_bwd=(sv["o"], g_mix_post[i][None]), name=f"mmb_o_{i}")
        grads["g_mix_post"][i] = dg[0]
        grads["w_o"][i] = _mm_tn(sv["cat"], do, name=f"mmw_o_{i}")
        dqm, dkm, dvm, _, _ = _attn_bwd(sv["qm"], sv["km"], sv["vm"], sv["mo"], sv["mlse"], dcat[:, Q_W:], bq=min(256, S),
                                        tk=sv["km"].shape[1], heads=N_MEM_HEADS, kv_per_head=True, tag=f"m{i}")
        dmkv = jnp.concatenate([_unheads(dkm[:, None]), _unheads(dvm[:, None])], axis=1).astype(BF16)
        grads["w_mem_kv"][i] = _mm_tn(mem_n, dmkv, name=f"mmw_mkv_{i}")
        dmem_n = dmem_n + _mm(dmkv, W_mkv, F32, nt=True, name=f"mmb_mkv_{i}")[0]
        carry = (scatter_pack(0, REST), False) if i == 0 else None
        dq, dk, dv, dsink, recv0_rest = _mixer_bwd(kind, dcat, sv["pr"], sv["kv"], sv["cat"], sv["mix"], sink, i, carry)
        if dsink is not None:
            d_sink[i // N_MIXERS] = dsink
        dpr = jnp.concatenate([dq, dk, dv, dqm], axis=1) if kind == 2 else (dq, dk, dv, dqm)
        if kind == 1:
            dproj, dgc = _headprep_bwd(dpr, tab, shift, sv["proj"], _qk_gain_row(qk_norm_g[i // N_MIXERS]),
                                       name=f"headprep_bwd_{i}")
            d_qkg[i // N_MIXERS] = jnp.stack([dgc[0, :Q_W].reshape(N_TOK_HEADS, HEAD_DIM).sum(0),
                                              dgc[0, Q_W:QK_W].reshape(N_KV_HEADS, HEAD_DIM).sum(0)])
        else:
            dproj = _headprep_bwd(dpr, tab, shift, name=f"headprep_bwd_{i}")
        dx, dg = _mm(dproj, W_in, F32, nt=True, post_bwd=(sv["x"], g_mix_pre[i][None], dx1), name=f"mmb_in_{i}")
        grads["g_mix_pre"][i] = dg[0]
        grads["w_in"][i] = _mm_tn(sv["h"], dproj, name=f"mmw_in_{i}")
    _, dg_mem = _rms_bwd(mem[0], mem_norm_g[None], dmem_n, BF16, name="rmsb_mem")

    def update(received, l, names, tag):
        res = _reduce_adamw(received, *[layer_pack(pre, l, F32, names) for pre in ("", "m_", "v_")], name=f"adamw_{tag}")
        return [_unpack_local(r, shapes_of(names)) for r in res]

    recv0_first = _exchange(scatter_pack(0, FIRST), False, "scatter_g0_in")
    per_layer = [[a + b for a, b in zip(update(recv0_first, 0, FIRST, "0_in"), update(recv0_rest, 0, REST, "0"))]]
    per_layer += [update(recv[l], l, BIG, str(l)) for l in range(1, depth)]
    big_out = [[jnp.concatenate(ts, axis=0) for ts in zip(*[per_layer[l][j] for l in range(depth)])] for j in range(4)]

    small_grads = dict(mem_norm_g=dg_mem[0], g_mix_pre=jnp.stack(grads["g_mix_pre"]), g_mix_post=jnp.stack(grads["g_mix_post"]),
                       attn_sink=jnp.stack(d_sink), qk_norm_g=jnp.stack(d_qkg), g_ffn_pre=jnp.stack(grads["g_ffn_pre"]),
                       g_ffn_post=jnp.stack(grads["g_ffn_post"]))
    sg = _pack_small([small_grads[n] for n in SMALL])
    srecv = _exchange(sg, True, "gather_small_grads")
    spacked = lambda pre: _pack_small([given[pre + n] for n in SMALL])
    gs, ds, ms, vs = _reduce_adamw(srecv, spacked(""), spacked("m_"), spacked("v_"), name="adamw_small")

    out = {}
    for pre, fb, fs in zip(("grad_", "delta_", "new_m_", "new_v_"), big_out, (gs, ds, ms, vs)):
        for (n, _), t in zip(BIG, fb):
            out[pre + n] = t
        for n, t in zip(SMALL, _unpack_small(fs, [given[n].shape for n in SMALL])):
            out[pre + n] = t
    order = ("mem_norm_g", "w_in", "w_mem_kv", "w_o", "g_mix_pre", "g_mix_post", "attn_sink", "qk_norm_g", "w_gate_up",
             "w_down", "g_ffn_pre", "g_ffn_post")
    return (loss, dx[None], *[out[p + n] for p in ("grad_", "delta_", "new_m_", "new_v_") for n in order])
```
